```python
import jax, jax.numpy as jnp
from jax import lax
import numpy as np

D_MODEL = 1024
BATCH = 16
SEQ = 2048
DEPTH = 4

N_HEADS = 8
QK_NOPE_DIM = 128
QK_ROPE_DIM = 64
QK_DIM = QK_NOPE_DIM + QK_ROPE_DIM
V_HEAD_DIM = 128
Q_LORA_RANK = 384
KV_LORA_RANK = 256
ROPE_THETA = 10000.0
Q_BLOCK = 128
POOL_WINDOWS = (2, 4, 8, 16)
N_POOL_GROUPS = 4
POOL_GROUP_DIM = 128
POOL_DIM = N_POOL_GROUPS * POOL_GROUP_DIM
N_BRANCHES = 2
IN_DIM = POOL_DIM + Q_LORA_RANK + KV_LORA_RANK + QK_ROPE_DIM + N_BRANCHES * D_MODEL
SPLIT_POINTS = (POOL_DIM,
                POOL_DIM + Q_LORA_RANK,
                POOL_DIM + Q_LORA_RANK + KV_LORA_RANK,
                POOL_DIM + Q_LORA_RANK + KV_LORA_RANK + QK_ROPE_DIM)
D_FF = 2816
NORM_EPS = 1e-6

kernel_name = "macaron_gated_pool_mla_trunk"


def rms_norm(x, g):
    xf = x.astype(jnp.float32)
    y = xf * lax.rsqrt(jnp.mean(xf * xf, axis=-1, keepdims=True) + NORM_EPS)
    return (y * g.astype(jnp.float32)).astype(x.dtype)


def swiglu_ffn(h, w_up, w_down):
    gate, up = jnp.split(h @ w_up, 2, axis=-1)
    return (jax.nn.silu(gate) * up) @ w_down


def rope_tables(positions):
    inv_freq = ROPE_THETA ** (-jnp.arange(0, QK_ROPE_DIM, 2, dtype=jnp.float32) / QK_ROPE_DIM)
    ang = positions.astype(jnp.float32)[..., None] * inv_freq
    return jnp.cos(ang), jnp.sin(ang)


def apply_rope(x, cos, sin):
    xf = x.astype(jnp.float32)
    x1, x2 = jnp.split(xf, 2, axis=-1)
    out = jnp.concatenate([x1 * cos - x2 * sin, x2 * cos + x1 * sin], axis=-1)
    return out.astype(x.dtype)


def causal_multiscale_pool(xp, pool_maps, pool_scale):
    B, S, _ = xp.shape
    xg = xp.reshape(B, S, N_POOL_GROUPS, POOL_GROUP_DIM).astype(jnp.float32)
    csum = jnp.pad(jnp.cumsum(xg, axis=1), ((0, 0), (1, 0), (0, 0), (0, 0)))
    windows = jnp.array(POOL_WINDOWS, dtype=jnp.int32)
    t = jnp.arange(S, dtype=jnp.int32)[:, None]
    start = jnp.maximum(t + 1 - windows[None, :], 0)
    csum_start = csum[:, start, jnp.arange(N_POOL_GROUPS)[None, :]]
    count = jnp.minimum(t + 1, windows[None, :]).astype(jnp.float32)
    pooled = (csum[:, 1:] - csum_start) / count[None, :, :, None] - xg
    mixed = jnp.einsum('bsgc,gcd->bsgd', pooled.astype(xp.dtype), pool_maps)
    return mixed.reshape(B, S, POOL_DIM) * pool_scale


def mla_attention(q_lat, kv_lat, k_rope, cos, sin, q_norm, w_uq, kv_norm, w_ukv):
    B, S, _ = q_lat.shape
    q = (rms_norm(q_lat, q_norm) @ w_uq).reshape(B, S, N_HEADS, QK_DIM)
    q_nope, q_rope = q[..., :QK_NOPE_DIM], q[..., QK_NOPE_DIM:]
    kv = (rms_norm(kv_lat, kv_norm) @ w_ukv).reshape(B, S, N_HEADS, QK_NOPE_DIM + V_HEAD_DIM)
    k_nope, v = kv[..., :QK_NOPE_DIM], kv[..., QK_NOPE_DIM:]
    q_rope = apply_rope(q_rope, cos[:, :, None, :], sin[:, :, None, :])
    k_rope = apply_rope(k_rope, cos, sin)
    scale = QK_DIM ** -0.5
    outs = []
    for blk in range(S // Q_BLOCK):
        q0 = blk * Q_BLOCK
        k_end = q0 + Q_BLOCK
        s = (jnp.einsum('bqhd,bkhd->bhqk', q_nope[:, q0:k_end], k_nope[:, :k_end])
             + jnp.einsum('bqhr,bkr->bhqk', q_rope[:, q0:k_end], k_rope[:, :k_end]))
        s = s.astype(jnp.float32) * scale
        mask = (q0 + jnp.arange(Q_BLOCK))[:, None] >= jnp.arange(k_end)[None, :]
        s = jnp.where(mask[None, None], s, jnp.finfo(jnp.float32).min)
        p = jax.nn.softmax(s, axis=-1).astype(v.dtype)
        outs.append(jnp.einsum('bhqk,bkhd->bqhd', p, v[:, :k_end]))
    o = jnp.concatenate(outs, axis=1)
    return o.reshape(B, S, N_HEADS * V_HEAD_DIM)


def _fwd_setup_inputs(seed: int = 0) -> dict:
    key = jax.random.key(seed)
    ks = jax.random.split(key, 24)
    f32 = jnp.float32

    def w(k, shape, fan_in):
        return jax.random.normal(k, shape, f32) * (fan_in ** -0.5)

    def gain(k, shape):
        return 1.0 + 0.02 * jax.random.normal(k, shape, f32)

    L, D = DEPTH, D_MODEL
    offsets = jax.random.randint(ks[1], (BATCH, 1), 0, 4096, dtype=jnp.int32)
    positions = offsets + jnp.arange(SEQ, dtype=jnp.int32)[None, :]
    return {
        "x": jax.random.normal(ks[0], (BATCH, SEQ, D), f32),
        "positions": positions,
        "norm_ffn1": gain(ks[2], (L, D)),
        "ffn1_up": w(ks[3], (L, D, 2 * D_FF), D),
        "ffn1_down": w(ks[4], (L, D_FF, D), D_FF),
        "norm_mix": gain(ks[5], (L, D)),
        "w_in": w(ks[6], (L, D, IN_DIM), D),
        "b_gate": 0.02 * jax.random.normal(ks[7], (L, N_BRANCHES * D), f32),
        "pool_maps": w(ks[8], (L, N_POOL_GROUPS, POOL_GROUP_DIM, POOL_GROUP_DIM), POOL_GROUP_DIM),
        "pool_scale": 1.0 + 0.1 * jax.random.normal(ks[9], (L, POOL_DIM), f32),
        "w_pool_proj": w(ks[10], (L, POOL_DIM, D), POOL_DIM),
        "q_latent_norm": gain(ks[11], (L, Q_LORA_RANK)),
        "w_uq": w(ks[12], (L, Q_LORA_RANK, N_HEADS * QK_DIM), Q_LORA_RANK),
        "kv_latent_norm": gain(ks[13], (L, KV_LORA_RANK)),
        "w_ukv": w(ks[14], (L, KV_LORA_RANK, N_HEADS * (QK_NOPE_DIM + V_HEAD_DIM)), KV_LORA_RANK),
        "w_attn_proj": w(ks[15], (L, N_HEADS * V_HEAD_DIM, D), N_HEADS * V_HEAD_DIM),
        "w_out": w(ks[16], (L, D, D), D),
        "norm_ffn2": gain(ks[17], (L, D)),
        "ffn2_up": w(ks[18], (L, D, 2 * D_FF), D),
        "ffn2_down": w(ks[19], (L, D_FF, D), D_FF),
        "final_norm": gain(ks[20], (D,)),
    }


def _fwd_reference(x, positions, norm_ffn1, ffn1_up, ffn1_down, norm_mix, w_in, b_gate,
              pool_maps, pool_scale, w_pool_proj, q_latent_norm, w_uq, kv_latent_norm,
              w_ukv, w_attn_proj, w_out, norm_ffn2, ffn2_up, ffn2_down, final_norm):
    B, S, D = x.shape
    cos, sin = rope_tables(positions)
    for l in range(DEPTH):
        x = x + 0.5 * swiglu_ffn(rms_norm(x, norm_ffn1[l]), ffn1_up[l], ffn1_down[l])
        h = rms_norm(x, norm_mix[l])
        proj = h @ w_in[l]
        x_pool, q_lat, kv_lat, k_rope, gate_logits = jnp.split(proj, SPLIT_POINTS, axis=-1)
        gates = jax.nn.sigmoid((gate_logits + b_gate[l]).astype(jnp.float32)).astype(x.dtype)
        gates = gates.reshape(B, S, N_BRANCHES, D)
        branch_a = causal_multiscale_pool(x_pool, pool_maps[l], pool_scale[l]) @ w_pool_proj[l]
        branch_b = mla_attention(q_lat, kv_lat, k_rope, cos, sin, q_latent_norm[l], w_uq[l],
                                 kv_latent_norm[l], w_ukv[l]) @ w_attn_proj[l]
        merged = gates[:, :, 0] * branch_a + gates[:, :, 1] * branch_b
        x = x + merged @ w_out[l]
        x = x + 0.5 * swiglu_ffn(rms_norm(x, norm_ffn2[l]), ffn2_up[l], ffn2_down[l])
    return rms_norm(x, final_norm)


import jax as _jax
import jax.numpy as _jnp

TWIN_FORMAT = 'train_step'
FWD_PARAMS = ['x', 'positions', 'norm_ffn1', 'ffn1_up', 'ffn1_down', 'norm_mix', 'w_in', 'b_gate', 'pool_maps', 'pool_scale', 'w_pool_proj', 'q_latent_norm', 'w_uq', 'kv_latent_norm', 'w_ukv', 'w_attn_proj', 'w_out', 'norm_ffn2', 'ffn2_up', 'ffn2_down', 'final_norm']
TWIN_WEIGHTS = ['norm_ffn1', 'ffn1_up', 'ffn1_down', 'norm_mix', 'w_in', 'b_gate', 'pool_maps', 'pool_scale', 'w_pool_proj', 'q_latent_norm', 'w_uq', 'kv_latent_norm', 'w_ukv', 'w_attn_proj', 'w_out', 'norm_ffn2', 'ffn2_up', 'ffn2_down', 'final_norm']
TWIN_DIFF_INPUT = 'x'
TWIN_INPUTS = ['x', 'positions', 'norm_ffn1', 'ffn1_up', 'ffn1_down', 'norm_mix', 'w_in', 'b_gate', 'pool_maps', 'pool_scale', 'w_pool_proj', 'q_latent_norm', 'w_uq', 'kv_latent_norm', 'w_ukv', 'w_attn_proj', 'w_out', 'norm_ffn2', 'ffn2_up', 'ffn2_down', 'final_norm', 'loss_target', 'm_norm_ffn1', 'm_ffn1_up', 'm_ffn1_down', 'm_norm_mix', 'm_w_in', 'm_b_gate', 'm_pool_maps', 'm_pool_scale', 'm_w_pool_proj', 'm_q_latent_norm', 'm_w_uq', 'm_kv_latent_norm', 'm_w_ukv', 'm_w_attn_proj', 'm_w_out', 'm_norm_ffn2', 'm_ffn2_up', 'm_ffn2_down', 'm_final_norm', 'v_norm_ffn1', 'v_ffn1_up', 'v_ffn1_down', 'v_norm_mix', 'v_w_in', 'v_b_gate', 'v_pool_maps', 'v_pool_scale', 'v_w_pool_proj', 'v_q_latent_norm', 'v_w_uq', 'v_kv_latent_norm', 'v_w_ukv', 'v_w_attn_proj', 'v_w_out', 'v_norm_ffn2', 'v_ffn2_up', 'v_ffn2_down', 'v_final_norm']
TWIN_OUTPUTS = ['loss', 'grad_x', 'grad_norm_ffn1', 'grad_ffn1_up', 'grad_ffn1_down', 'grad_norm_mix', 'grad_w_in', 'grad_b_gate', 'grad_pool_maps', 'grad_pool_scale', 'grad_w_pool_proj', 'grad_q_latent_norm', 'grad_w_uq', 'grad_kv_latent_norm', 'grad_w_ukv', 'grad_w_attn_proj', 'grad_w_out', 'grad_norm_ffn2', 'grad_ffn2_up', 'grad_ffn2_down', 'grad_final_norm', 'delta_norm_ffn1', 'delta_ffn1_up', 'delta_ffn1_down', 'delta_norm_mix', 'delta_w_in', 'delta_b_gate', 'delta_pool_maps', 'delta_pool_scale', 'delta_w_pool_proj', 'delta_q_latent_norm', 'delta_w_uq', 'delta_kv_latent_norm', 'delta_w_ukv', 'delta_w_attn_proj', 'delta_w_out', 'delta_norm_ffn2', 'delta_ffn2_up', 'delta_ffn2_down', 'delta_final_norm', 'new_m_norm_ffn1', 'new_m_ffn1_up', 'new_m_ffn1_down', 'new_m_norm_mix', 'new_m_w_in', 'new_m_b_gate', 'new_m_pool_maps', 'new_m_pool_scale', 'new_m_w_pool_proj', 'new_m_q_latent_norm', 'new_m_w_uq', 'new_m_kv_latent_norm', 'new_m_w_ukv', 'new_m_w_attn_proj', 'new_m_w_out', 'new_m_norm_ffn2', 'new_m_ffn2_up', 'new_m_ffn2_down', 'new_m_final_norm', 'new_v_norm_ffn1', 'new_v_ffn1_up', 'new_v_ffn1_down', 'new_v_norm_mix', 'new_v_w_in', 'new_v_b_gate', 'new_v_pool_maps', 'new_v_pool_scale', 'new_v_w_pool_proj', 'new_v_q_latent_norm', 'new_v_w_uq', 'new_v_kv_latent_norm', 'new_v_w_ukv', 'new_v_w_attn_proj', 'new_v_w_out', 'new_v_norm_ffn2', 'new_v_ffn2_up', 'new_v_ffn2_down', 'new_v_final_norm']
TWIN_LEAF_KINDS = {'loss': 'loss', 'grad_x': 'grad_x', 'grad_norm_ffn1': 'grad_w', 'grad_ffn1_up': 'grad_w', 'grad_ffn1_down': 'grad_w', 'grad_norm_mix': 'grad_w', 'grad_w_in': 'grad_w', 'grad_b_gate': 'grad_w', 'grad_pool_maps': 'grad_w', 'grad_pool_scale': 'grad_w', 'grad_w_pool_proj': 'grad_w', 'grad_q_latent_norm': 'grad_w', 'grad_w_uq': 'grad_w', 'grad_kv_latent_norm': 'grad_w', 'grad_w_ukv': 'grad_w', 'grad_w_attn_proj': 'grad_w', 'grad_w_out': 'grad_w', 'grad_norm_ffn2': 'grad_w', 'grad_ffn2_up': 'grad_w', 'grad_ffn2_down': 'grad_w', 'grad_final_norm': 'grad_w', 'delta_norm_ffn1': 'delta_w', 'delta_ffn1_up': 'delta_w', 'delta_ffn1_down': 'delta_w', 'delta_norm_mix': 'delta_w', 'delta_w_in': 'delta_w', 'delta_b_gate': 'delta_w', 'delta_pool_maps': 'delta_w', 'delta_pool_scale': 'delta_w', 'delta_w_pool_proj': 'delta_w', 'delta_q_latent_norm': 'delta_w', 'delta_w_uq': 'delta_w', 'delta_kv_latent_norm': 'delta_w', 'delta_w_ukv': 'delta_w', 'delta_w_attn_proj': 'delta_w', 'delta_w_out': 'delta_w', 'delta_norm_ffn2': 'delta_w', 'delta_ffn2_up': 'delta_w', 'delta_ffn2_down': 'delta_w', 'delta_final_norm': 'delta_w', 'new_m_norm_ffn1': 'new_m', 'new_m_ffn1_up': 'new_m', 'new_m_ffn1_down': 'new_m', 'new_m_norm_mix': 'new_m', 'new_m_w_in': 'new_m', 'new_m_b_gate': 'new_m', 'new_m_pool_maps': 'new_m', 'new_m_pool_scale': 'new_m', 'new_m_w_pool_proj': 'new_m', 'new_m_q_latent_norm': 'new_m', 'new_m_w_uq': 'new_m', 'new_m_kv_latent_norm': 'new_m', 'new_m_w_ukv': 'new_m', 'new_m_w_attn_proj': 'new_m', 'new_m_w_out': 'new_m', 'new_m_norm_ffn2': 'new_m', 'new_m_ffn2_up': 'new_m', 'new_m_ffn2_down': 'new_m', 'new_m_final_norm': 'new_m', 'new_v_norm_ffn1': 'new_v', 'new_v_ffn1_up': 'new_v', 'new_v_ffn1_down': 'new_v', 'new_v_norm_mix': 'new_v', 'new_v_w_in': 'new_v', 'new_v_b_gate': 'new_v', 'new_v_pool_maps': 'new_v', 'new_v_pool_scale': 'new_v', 'new_v_w_pool_proj': 'new_v', 'new_v_q_latent_norm': 'new_v', 'new_v_w_uq': 'new_v', 'new_v_kv_latent_norm': 'new_v', 'new_v_w_ukv': 'new_v', 'new_v_w_attn_proj': 'new_v', 'new_v_w_out': 'new_v', 'new_v_norm_ffn2': 'new_v', 'new_v_ffn2_up': 'new_v', 'new_v_ffn2_down': 'new_v', 'new_v_final_norm': 'new_v'}


def _forward(args):
    return _fwd_reference(*[args[k] for k in FWD_PARAMS])


def _output_shape():
    out = _jax.eval_shape(lambda: _forward(_fwd_setup_inputs(0)))
    return out.shape, out.dtype

N_MICROBATCH = 1
ADAM_LR = 0.001
ADAM_B1 = 0.9
ADAM_B2 = 0.999
ADAM_EPS = 1e-08
ADAM_WD = 0.01
ADAM_STEP = 10
PER_EXAMPLE_BATCH_AXIS = {'x': 0, 'positions': 0, 'loss_target': 0}
SHARED_INPUTS = []
_WEIGHT_DTYPES = {'norm_ffn1': _jnp.float32, 'ffn1_up': _jnp.float32, 'ffn1_down': _jnp.float32, 'norm_mix': _jnp.float32, 'w_in': _jnp.float32, 'b_gate': _jnp.float32, 'pool_maps': _jnp.float32, 'pool_scale': _jnp.float32, 'w_pool_proj': _jnp.float32, 'q_latent_norm': _jnp.float32, 'w_uq': _jnp.float32, 'kv_latent_norm': _jnp.float32, 'w_ukv': _jnp.float32, 'w_attn_proj': _jnp.float32, 'w_out': _jnp.float32, 'norm_ffn2': _jnp.float32, 'ffn2_up': _jnp.float32, 'ffn2_down': _jnp.float32, 'final_norm': _jnp.float32}
MOMENT_SCALE = {'norm_ffn1': 7.037314e-02, 'ffn1_up': 2.994024e-02, 'ffn1_down': 4.881019e-02, 'norm_mix': 8.103910e-02, 'w_in': 4.528608e-02, 'b_gate': 2.001592e-02, 'pool_maps': 1.018663e-01, 'pool_scale': 9.999991e-02, 'w_pool_proj': 7.193097e-02, 'q_latent_norm': 2.340329e-02, 'w_uq': 1.193378e-02, 'kv_latent_norm': 4.755181e-02, 'w_ukv': 1.618188e-02, 'w_attn_proj': 1.932815e-02, 'w_out': 7.295430e-02, 'norm_ffn2': 6.668886e-02, 'ffn2_up': 2.585191e-02, 'ffn2_down': 4.220306e-02, 'final_norm': 3.203354e+01}


def _to_microbatches(a, axis):
    t = _jnp.moveaxis(a, axis, 0)
    t = t.reshape((N_MICROBATCH, t.shape[0] // N_MICROBATCH) + t.shape[1:])
    return _jnp.moveaxis(t, 1, axis + 1)


def setup_inputs(seed: int = 0) -> dict:
    inp = _fwd_setup_inputs(seed)
    key = _jax.random.fold_in(_jax.random.key(seed), 7919)
    shape, _ = _output_shape()
    out = dict(inp)
    out["loss_target"] = _jax.random.normal(_jax.random.fold_in(key, 0), shape, _jnp.float32)
    for i, name in enumerate(TWIN_WEIGHTS):
        w = inp[name].astype(_jnp.float32)
        if MOMENT_SCALE is None:
            s = _jnp.sqrt(_jnp.mean(_jnp.square(w)) + 1e-30)
        else:
            s = MOMENT_SCALE[name]
        km, kv = _jax.random.split(_jax.random.fold_in(key, i + 1))
        out[name] = w
        out["m_" + name] = s * _jax.random.normal(km, w.shape, _jnp.float32)
        out["v_" + name] = (s * s) * _jax.random.uniform(kv, w.shape, _jnp.float32, 0.5, 1.5)
    if N_MICROBATCH > 1:
        for name, axis in PER_EXAMPLE_BATCH_AXIS.items():
            out[name] = _to_microbatches(out[name], axis)
    return {'x': out['x'], 'positions': out['positions'], 'norm_ffn1': out['norm_ffn1'], 'ffn1_up': out['ffn1_up'], 'ffn1_down': out['ffn1_down'], 'norm_mix': out['norm_mix'], 'w_in': out['w_in'], 'b_gate': out['b_gate'], 'pool_maps': out['pool_maps'], 'pool_scale': out['pool_scale'], 'w_pool_proj': out['w_pool_proj'], 'q_latent_norm': out['q_latent_norm'], 'w_uq': out['w_uq'], 'kv_latent_norm': out['kv_latent_norm'], 'w_ukv': out['w_ukv'], 'w_attn_proj': out['w_attn_proj'], 'w_out': out['w_out'], 'norm_ffn2': out['norm_ffn2'], 'ffn2_up': out['ffn2_up'], 'ffn2_down': out['ffn2_down'], 'final_norm': out['final_norm'], 'loss_target': out['loss_target'], 'm_norm_ffn1': out['m_norm_ffn1'], 'm_ffn1_up': out['m_ffn1_up'], 'm_ffn1_down': out['m_ffn1_down'], 'm_norm_mix': out['m_norm_mix'], 'm_w_in': out['m_w_in'], 'm_b_gate': out['m_b_gate'], 'm_pool_maps': out['m_pool_maps'], 'm_pool_scale': out['m_pool_scale'], 'm_w_pool_proj': out['m_w_pool_proj'], 'm_q_latent_norm': out['m_q_latent_norm'], 'm_w_uq': out['m_w_uq'], 'm_kv_latent_norm': out['m_kv_latent_norm'], 'm_w_ukv': out['m_w_ukv'], 'm_w_attn_proj': out['m_w_attn_proj'], 'm_w_out': out['m_w_out'], 'm_norm_ffn2': out['m_norm_ffn2'], 'm_ffn2_up': out['m_ffn2_up'], 'm_ffn2_down': out['m_ffn2_down'], 'm_final_norm': out['m_final_norm'], 'v_norm_ffn1': out['v_norm_ffn1'], 'v_ffn1_up': out['v_ffn1_up'], 'v_ffn1_down': out['v_ffn1_down'], 'v_norm_mix': out['v_norm_mix'], 'v_w_in': out['v_w_in'], 'v_b_gate': out['v_b_gate'], 'v_pool_maps': out['v_pool_maps'], 'v_pool_scale': out['v_pool_scale'], 'v_w_pool_proj': out['v_w_pool_proj'], 'v_q_latent_norm': out['v_q_latent_norm'], 'v_w_uq': out['v_w_uq'], 'v_kv_latent_norm': out['v_kv_latent_norm'], 'v_w_ukv': out['v_w_ukv'], 'v_w_attn_proj': out['v_w_attn_proj'], 'v_w_out': out['v_w_out'], 'v_norm_ffn2': out['v_norm_ffn2'], 'v_ffn2_up': out['v_ffn2_up'], 'v_ffn2_down': out['v_ffn2_down'], 'v_final_norm': out['v_final_norm']}


def _loss(weights, diff, rest, loss_target):
    with _jax.named_scope("forward"):
        args = {**rest, TWIN_DIFF_INPUT: diff, **{k: w.astype(_WEIGHT_DTYPES[k]) for k, w in weights.items()}}
        y = _forward(args)
    with _jax.named_scope("loss_head"):
        err = _jnp.square(y.astype(_jnp.float32) - loss_target)
        return 0.5 * _jnp.sum(_jnp.mean(err, axis=-1)) if err.ndim else 0.5 * err


def _adamw(w, g, m, v):
    m = ADAM_B1 * m + (1.0 - ADAM_B1) * g
    v = ADAM_B2 * v + (1.0 - ADAM_B2) * _jnp.square(g)
    m_hat = m / (1.0 - ADAM_B1 ** ADAM_STEP)
    v_hat = v / (1.0 - ADAM_B2 ** ADAM_STEP)
    delta = -ADAM_LR * (m_hat / (_jnp.sqrt(v_hat) + ADAM_EPS) + ADAM_WD * w)
    return delta, m, v


def reference(x, positions, norm_ffn1, ffn1_up, ffn1_down, norm_mix, w_in, b_gate, pool_maps, pool_scale, w_pool_proj, q_latent_norm, w_uq, kv_latent_norm, w_ukv, w_attn_proj, w_out, norm_ffn2, ffn2_up, ffn2_down, final_norm, loss_target, m_norm_ffn1, m_ffn1_up, m_ffn1_down, m_norm_mix, m_w_in, m_b_gate, m_pool_maps, m_pool_scale, m_w_pool_proj, m_q_latent_norm, m_w_uq, m_kv_latent_norm, m_w_ukv, m_w_attn_proj, m_w_out, m_norm_ffn2, m_ffn2_up, m_ffn2_down, m_final_norm, v_norm_ffn1, v_ffn1_up, v_ffn1_down, v_norm_mix, v_w_in, v_b_gate, v_pool_maps, v_pool_scale, v_w_pool_proj, v_q_latent_norm, v_w_uq, v_kv_latent_norm, v_w_ukv, v_w_attn_proj, v_w_out, v_norm_ffn2, v_ffn2_up, v_ffn2_down, v_final_norm):
    given = dict(x=x, positions=positions, norm_ffn1=norm_ffn1, ffn1_up=ffn1_up, ffn1_down=ffn1_down, norm_mix=norm_mix, w_in=w_in, b_gate=b_gate, pool_maps=pool_maps, pool_scale=pool_scale, w_pool_proj=w_pool_proj, q_latent_norm=q_latent_norm, w_uq=w_uq, kv_latent_norm=kv_latent_norm, w_ukv=w_ukv, w_attn_proj=w_attn_proj, w_out=w_out, norm_ffn2=norm_ffn2, ffn2_up=ffn2_up, ffn2_down=ffn2_down, final_norm=final_norm, loss_target=loss_target, m_norm_ffn1=m_norm_ffn1, m_ffn1_up=m_ffn1_up, m_ffn1_down=m_ffn1_down, m_norm_mix=m_norm_mix, m_w_in=m_w_in, m_b_gate=m_b_gate, m_pool_maps=m_pool_maps, m_pool_scale=m_pool_scale, m_w_pool_proj=m_w_pool_proj, m_q_latent_norm=m_q_latent_norm, m_w_uq=m_w_uq, m_kv_latent_norm=m_kv_latent_norm, m_w_ukv=m_w_ukv, m_w_attn_proj=m_w_attn_proj, m_w_out=m_w_out, m_norm_ffn2=m_norm_ffn2, m_ffn2_up=m_ffn2_up, m_ffn2_down=m_ffn2_down, m_final_norm=m_final_norm, v_norm_ffn1=v_norm_ffn1, v_ffn1_up=v_ffn1_up, v_ffn1_down=v_ffn1_down, v_norm_mix=v_norm_mix, v_w_in=v_w_in, v_b_gate=v_b_gate, v_pool_maps=v_pool_maps, v_pool_scale=v_pool_scale, v_w_pool_proj=v_w_pool_proj, v_q_latent_norm=v_q_latent_norm, v_w_uq=v_w_uq, v_kv_latent_norm=v_kv_latent_norm, v_w_ukv=v_w_ukv, v_w_attn_proj=v_w_attn_proj, v_w_out=v_w_out, v_norm_ffn2=v_norm_ffn2, v_ffn2_up=v_ffn2_up, v_ffn2_down=v_ffn2_down, v_final_norm=v_final_norm)
    weights = {n: given[n] for n in TWIN_WEIGHTS}
    shared = {n: given[n] for n in SHARED_INPUTS}
    per_example = {n: given[n] for n in ['x', 'positions']}
    grad_fn = _jax.value_and_grad(_loss, argnums=(0, 1))

    def one_microbatch(ex, loss_target):
        ex = dict(ex)
        diff = ex.pop(TWIN_DIFF_INPUT)
        return grad_fn(weights, diff, {**shared, **ex}, loss_target)

    if N_MICROBATCH == 1:
        loss, (grad_w, grad_x) = one_microbatch(per_example, given["loss_target"])
    else:
        def body(carry, xs):
            loss_sum, grad_sum = carry
            l_k, (gw_k, gx_k) = one_microbatch(xs[0], xs[1])
            with _jax.named_scope("update"):
                return (loss_sum + l_k, _jax.tree.map(_jnp.add, grad_sum, gw_k)), gx_k

        init = (_jnp.zeros((), _jnp.float32), _jax.tree.map(_jnp.zeros_like, weights))
        (loss, grad_w), grad_x = _jax.lax.scan(body, init, (per_example, given["loss_target"]))
    with _jax.named_scope("update"):
        delta_w, new_m, new_v = {}, {}, {}
        for n in TWIN_WEIGHTS:
            delta_w[n], new_m[n], new_v[n] = _adamw(weights[n], grad_w[n], given["m_" + n], given["v_" + n])
    return (loss, grad_x, *[grad_w[n] for n in TWIN_WEIGHTS], *[delta_w[n] for n in TWIN_WEIGHTS],
            *[new_m[n] for n in TWIN_WEIGHTS], *[new_v[n] for n in TWIN_WEIGHTS])
```

```python
import functools

import numpy as np
import jax
import jax.numpy as jnp
from jax import lax
from jax.experimental import pallas as pl
from jax.experimental.pallas import tpu as pltpu

F32 = jnp.float32
BF16 = jnp.bfloat16

NORM_EPS = 1e-6
ROPE_THETA = 10000.0
QK_NOPE = 128
QK_ROPE = 64
V_DIM = 128
HEAD_W = 256
POOL_WINDOWS = (2, 4, 8, 16)
POOL_G = 128
POOL_DIM = 512
LANE = 128
ATTN_SCALE = float((QK_NOPE + QK_ROPE) ** -0.5)
MASK_VALUE = -1e30

ADAM_LR = 0.001
ADAM_B1 = 0.9
ADAM_B2 = 0.999
ADAM_EPS = 1e-08
ADAM_WD = 0.01
ADAM_STEP = 10

N_DEV = 8
VMEM_LIMIT = 52 * 1024 * 1024

MESH = pl.DeviceIdType.MESH
ANY = pl.BlockSpec(memory_space=pl.ANY)


def _tile(dim, target, align=LANE):
    if dim <= target:
        return dim
    t = (target // align) * align
    while t >= align:
        if dim % t == 0:
            return t
        t -= align
    return dim


def _params(*sem):
    return pltpu.CompilerParams(dimension_semantics=sem, vmem_limit_bytes=VMEM_LIMIT)


def _rstd(x):
    return lax.rsqrt(jnp.mean(x * x, axis=-1, keepdims=True) + NORM_EPS)


def _mm(a, b, *, name, ta=False, tb=False, out_dtype=F32, res=None, alpha=1.0, tm=512, tn=1024, tk=1024):
    if ta:
        K, M = a.shape
    else:
        M, K = a.shape
    if tb:
        N, K2 = b.shape
    else:
        K2, N = b.shape
    assert K == K2, (a.shape, b.shape, ta, tb)
    tm, tn, tk = _tile(M, tm), _tile(N, tn), _tile(K, tk)
    nk = K // tk
    dims = (((0 if ta else 1,), (1 if tb else 0,)), ((), ()))
    has_res = res is not None

    def body(*refs):
        a_ref, b_ref = refs[0], refs[1]
        res_ref = refs[2] if has_res else None
        o_ref = refs[2 + has_res]
        acc_ref = refs[3 + has_res] if nk > 1 else None
        part = lax.dot_general(a_ref[...].astype(BF16), b_ref[...].astype(BF16), dims,
                               preferred_element_type=F32)

        def finish(acc):
            r = acc * alpha if alpha != 1.0 else acc
            if has_res:
                r = res_ref[...].astype(F32) + r
            o_ref[...] = r.astype(out_dtype)

        if nk == 1:
            finish(part)
        else:
            k = pl.program_id(2)

            @pl.when(k == 0)
            def _():
                acc_ref[...] = part

            @pl.when(k > 0)
            def _():
                acc_ref[...] += part

            @pl.when(k == nk - 1)
            def _():
                finish(acc_ref[...])

    a_spec = pl.BlockSpec((tk, tm), lambda i, j, k: (k, i)) if ta else pl.BlockSpec((tm, tk), lambda i, j, k: (i, k))
    b_spec = pl.BlockSpec((tn, tk), lambda i, j, k: (j, k)) if tb else pl.BlockSpec((tk, tn), lambda i, j, k: (k, j))
    in_specs = [a_spec, b_spec]
    operands = [a, b]
    if has_res:
        in_specs.append(pl.BlockSpec((tm, tn), lambda i, j, k: (i, j)))
        operands.append(res)
    return pl.pallas_call(
        body, name=name,
        out_shape=jax.ShapeDtypeStruct((M, N), out_dtype),
        grid=(M // tm, N // tn, nk),
        in_specs=in_specs,
        out_specs=pl.BlockSpec((tm, tn), lambda i, j, k: (i, j)),
        scratch_shapes=[pltpu.VMEM((tm, tn), F32)] if nk > 1 else [],
        compiler_params=_params("parallel", "parallel", "arbitrary"),
    )(*operands)


def _rms_fwd(x, g, *, name):
    T, D = x.shape
    tm = _tile(T, 512, 16)

    def body(x_ref, g_ref, h_ref):
        x = x_ref[...]
        h_ref[...] = (x * _rstd(x) * g_ref[...]).astype(BF16)

    return pl.pallas_call(
        body, name=name,
        out_shape=jax.ShapeDtypeStruct((T, D), BF16),
        grid=(T // tm,),
        in_specs=[pl.BlockSpec((tm, D), lambda i: (i, 0)), pl.BlockSpec((1, D), lambda i: (0, 0))],
        out_specs=pl.BlockSpec((tm, D), lambda i: (i, 0)),
        compiler_params=_params("parallel"),
    )(x, g.reshape(1, D))


def _rms_bwd(x, g, dh, dxo, *, name):
    T, D = x.shape
    tm = _tile(T, 512, 16)

    def body(x_ref, g_ref, dh_ref, dxo_ref, dx_ref, dg_ref):
        x = x_ref[...]
        r = _rstd(x)
        xhat = x * r
        dh = dh_ref[...]
        dxh = dh * g_ref[...]
        dx_ref[...] = dxo_ref[...] + r * (dxh - xhat * jnp.mean(dxh * xhat, axis=-1, keepdims=True))
        part = jnp.sum(dh * xhat, axis=0, keepdims=True)

        @pl.when(pl.program_id(0) == 0)
        def _():
            dg_ref[...] = part

        @pl.when(pl.program_id(0) > 0)
        def _():
            dg_ref[...] += part

    row = pl.BlockSpec((tm, D), lambda i: (i, 0))
    vec = pl.BlockSpec((1, D), lambda i: (0, 0))
    return pl.pallas_call(
        body, name=name,
        out_shape=(jax.ShapeDtypeStruct((T, D), F32), jax.ShapeDtypeStruct((1, D), F32)),
        grid=(T // tm,),
        in_specs=[row, vec, row, row],
        out_specs=(row, vec),
        compiler_params=_params("arbitrary"),
    )(x, g.reshape(1, D), dh, dxo)


def _loss_head(x, g, target, *, name):
    T, D = x.shape
    tm = _tile(T, 512, 16)

    def body(x_ref, g_ref, t_ref, dx_ref, dg_ref, loss_ref):
        x = x_ref[...]
        gain = g_ref[...]
        r = _rstd(x)
        xhat = x * r
        err = xhat * gain - t_ref[...]
        dy = err * (1.0 / D)
        dxh = dy * gain
        dx_ref[...] = r * (dxh - xhat * jnp.mean(dxh * xhat, axis=-1, keepdims=True))
        dg_part = jnp.sum(dy * xhat, axis=0, keepdims=True)
        loss_part = jnp.full((1, LANE), 0.5 / D, F32) * jnp.sum(err * err)

        @pl.when(pl.program_id(0) == 0)
        def _():
            dg_ref[...] = dg_part
            loss_ref[...] = loss_part

        @pl.when(pl.program_id(0) > 0)
        def _():
            dg_ref[...] += dg_part
            loss_ref[...] += loss_part

    row = pl.BlockSpec((tm, D), lambda i: (i, 0))
    vec = pl.BlockSpec((1, D), lambda i: (0, 0))
    return pl.pallas_call(
        body, name=name,
        out_shape=(jax.ShapeDtypeStruct((T, D), F32), jax.ShapeDtypeStruct((1, D), F32),
                   jax.ShapeDtypeStruct((1, LANE), F32)),
        grid=(T // tm,),
        in_specs=[row, vec, row],
        out_specs=(row, vec, pl.BlockSpec((1, LANE), lambda i: (0, 0))),
        compiler_params=_params("arbitrary"),
    )(x, g.reshape(1, D), target)


def _ffn_up(h, wg, wu, *, name):
    T, D = h.shape
    F = wg.shape[1]
    tm, tn = _tile(T, 512, 16), _tile(F, 1408)

    def body(h_ref, wg_ref, wu_ref, gate_ref, up_ref, a_ref):
        h = h_ref[...]
        gate = jnp.dot(h, wg_ref[...], preferred_element_type=F32)
        up = jnp.dot(h, wu_ref[...], preferred_element_type=F32)
        gate_ref[...] = gate.astype(BF16)
        up_ref[...] = up.astype(BF16)
        a_ref[...] = (gate * jax.nn.sigmoid(gate) * up).astype(BF16)

    w_spec = pl.BlockSpec((D, tn), lambda j, i: (0, j))
    o_spec = pl.BlockSpec((tm, tn), lambda j, i: (i, j))
    out = jax.ShapeDtypeStruct((T, F), BF16)
    return pl.pallas_call(
        body, name=name,
        out_shape=(out, out, out),
        grid=(F // tn, T // tm),
        in_specs=[pl.BlockSpec((tm, D), lambda j, i: (i, 0)), w_spec, w_spec],
        out_specs=(o_spec, o_spec, o_spec),
        compiler_params=_params("parallel", "parallel"),
    )(h, wg, wu)


def _ffn_bwd_act(dxo, wd, gate, up, *, alpha, name):
    T, D = dxo.shape
    F = wd.shape[0]
    tm, tn = _tile(T, 512, 16), _tile(F, 1408)

    def body(dxo_ref, wd_ref, gate_ref, up_ref, dgate_ref, dup_ref):
        da = lax.dot_general(dxo_ref[...].astype(BF16), wd_ref[...], (((1,), (1,)), ((), ())),
                             preferred_element_type=F32) * alpha
        gate = gate_ref[...].astype(F32)
        up = up_ref[...].astype(F32)
        sig = jax.nn.sigmoid(gate)
        dgate_ref[...] = (da * up * (sig * (1.0 + gate * (1.0 - sig)))).astype(BF16)
        dup_ref[...] = (da * (gate * sig)).astype(BF16)

    t_spec = pl.BlockSpec((tm, tn), lambda j, i: (i, j))
    out = jax.ShapeDtypeStruct((T, F), BF16)
    return pl.pallas_call(
        body, name=name,
        out_shape=(out, out),
        grid=(F // tn, T // tm),
        in_specs=[pl.BlockSpec((tm, D), lambda j, i: (i, 0)), pl.BlockSpec((tn, D), lambda j, i: (j, 0)),
                  t_spec, t_spec],
        out_specs=(t_spec, t_spec),
        compiler_params=_params("parallel", "parallel"),
    )(dxo, wd, gate, up)


def _rope_tables(positions):
    half = QK_ROPE // 2
    inv_freq = ROPE_THETA ** (-jnp.arange(0, QK_ROPE, 2, dtype=F32) / QK_ROPE)
    ang = positions.astype(F32)[:, None] * inv_freq
    cos, sin = jnp.cos(ang), jnp.sin(ang)
    z = jnp.zeros_like(cos)
    zz = jnp.zeros((positions.shape[0], LANE - QK_ROPE), F32)
    c = jnp.concatenate([cos, cos, zz], axis=1)
    sa = jnp.concatenate([z, sin, zz], axis=1)
    sb = jnp.concatenate([-sin, z, zz], axis=1)
    return c, sa, sb


def _rotate(seg, c, sa, sb, sign):
    half = QK_ROPE // 2
    mix = pltpu.roll(seg, half, 1) * sa + pltpu.roll(seg, LANE - half, 1) * sb
    return seg * c + mix if sign > 0 else seg * c - mix


def _mixer_in(h, wa, wuq, wukv, gq, gkv, tabs, *, name):
    T, D = h.shape
    PA = wa.shape[1]
    QL, HQ = wuq.shape
    KVL = wukv.shape[0]
    H = HQ // HEAD_W
    o_q, o_kv, o_kr = POOL_DIM, POOL_DIM + QL, POOL_DIM + QL + KVL
    assert PA == o_kr + LANE
    tm = _tile(T, 256, 16)

    def body(h_ref, wa_ref, wuq_ref, wukv_ref, gq_ref, gkv_ref, c_ref, sa_ref, sb_ref,
             xp_ref, ql_ref, kvl_ref, qn_ref, kvn_ref, q_ref, kv_ref, kr_ref):
        proj = jnp.dot(h_ref[...], wa_ref[...], preferred_element_type=F32)
        xp_ref[...] = proj[:, :POOL_DIM]
        ql = proj[:, o_q:o_kv]
        kvl = proj[:, o_kv:o_kr]
        ql_ref[...] = ql
        kvl_ref[...] = kvl
        qn = (ql * _rstd(ql) * gq_ref[...]).astype(BF16)
        kvn = (kvl * _rstd(kvl) * gkv_ref[...]).astype(BF16)
        qn_ref[...] = qn
        kvn_ref[...] = kvn
        c, sa, sb = c_ref[...], sa_ref[...], sb_ref[...]
        q = jnp.dot(qn, wuq_ref[...], preferred_element_type=F32)
        for hh in range(H):
            base = hh * HEAD_W
            q_ref[:, base:base + QK_NOPE] = q[:, base:base + QK_NOPE].astype(BF16)
            q_ref[:, base + QK_NOPE:base + HEAD_W] = _rotate(
                q[:, base + QK_NOPE:base + HEAD_W], c, sa, sb, 1).astype(BF16)
        kv_ref[...] = jnp.dot(kvn, wukv_ref[...], preferred_element_type=F32).astype(BF16)
        kr_ref[...] = _rotate(proj[:, o_kr:o_kr + LANE], c, sa, sb, 1).astype(BF16)

    def row(w):
        return pl.BlockSpec((tm, w), lambda i: (i, 0))

    def whole(arr):
        return pl.BlockSpec(arr.shape, lambda i: (0,) * arr.ndim)

    gq2, gkv2 = gq.reshape(1, QL), gkv.reshape(1, KVL)
    outs = [(POOL_DIM, F32), (QL, F32), (KVL, F32), (QL, BF16), (KVL, BF16), (HQ, BF16), (HQ, BF16), (LANE, BF16)]
    return pl.pallas_call(
        body, name=name,
        out_shape=tuple(jax.ShapeDtypeStruct((T, w), dt) for w, dt in outs),
        grid=(T // tm,),
        in_specs=[row(D), whole(wa), whole(wuq), whole(wukv), whole(gq2), whole(gkv2),
                  row(LANE), row(LANE), row(LANE)],
        out_specs=tuple(row(w) for w, _ in outs),
        compiler_params=_params("parallel"),
    )(h, wa, wuq, wukv, gq2, gkv2, *tabs)


def _mixer_in_bwd(dq, dkv, dkr, ql, kvl, dxp, wuq, wukv, gq, gkv, tabs, *, name):
    T, HQ = dq.shape
    QL, KVL = wuq.shape[0], wukv.shape[0]
    H = HQ // HEAD_W
    PA = POOL_DIM + QL + KVL + LANE
    o_q, o_kv, o_kr = POOL_DIM, POOL_DIM + QL, POOL_DIM + QL + KVL
    tm = _tile(T, 256, 16)

    def norm_bwd(lat, gain, dn):
        r = _rstd(lat)
        xhat = lat * r
        dxh = dn * gain
        dlat = r * (dxh - xhat * jnp.mean(dxh * xhat, axis=-1, keepdims=True))
        return dlat, jnp.sum(dn * xhat, axis=0, keepdims=True)

    def body(dq_ref, dkv_ref, dkr_ref, ql_ref, kvl_ref, dxp_ref, wuq_ref, wukv_ref, gq_ref, gkv_ref,
             c_ref, sa_ref, sb_ref, dproj_ref, dqp_ref, dgq_ref, dgkv_ref):
        c, sa, sb = c_ref[...], sa_ref[...], sb_ref[...]
        dkr_sum = dkr_ref[:, :LANE]
        for hh in range(H):
            base = hh * HEAD_W
            dqp_ref[:, base:base + QK_NOPE] = dq_ref[:, base:base + QK_NOPE]
            dqp_ref[:, base + QK_NOPE:base + HEAD_W] = _rotate(
                dq_ref[:, base + QK_NOPE:base + HEAD_W].astype(F32), c, sa, sb, -1).astype(BF16)
            if hh:
                dkr_sum = dkr_sum + dkr_ref[:, hh * LANE:(hh + 1) * LANE]
        contract_last = (((1,), (1,)), ((), ()))
        dqn = lax.dot_general(dqp_ref[...], wuq_ref[...], contract_last, preferred_element_type=F32)
        dkvn = lax.dot_general(dkv_ref[...], wukv_ref[...], contract_last, preferred_element_type=F32)
        dql, dgq = norm_bwd(ql_ref[...], gq_ref[...], dqn)
        dkvl, dgkv = norm_bwd(kvl_ref[...], gkv_ref[...], dkvn)
        dproj_ref[:, :POOL_DIM] = dxp_ref[...].astype(BF16)
        dproj_ref[:, o_q:o_kv] = dql.astype(BF16)
        dproj_ref[:, o_kv:o_kr] = dkvl.astype(BF16)
        dproj_ref[:, o_kr:PA] = _rotate(dkr_sum, c, sa, sb, -1).astype(BF16)

        @pl.when(pl.program_id(0) == 0)
        def _():
            dgq_ref[...] = dgq
            dgkv_ref[...] = dgkv

        @pl.when(pl.program_id(0) > 0)
        def _():
            dgq_ref[...] += dgq
            dgkv_ref[...] += dgkv

    def row(w):
        return pl.BlockSpec((tm, w), lambda i: (i, 0))

    def whole(arr):
        return pl.BlockSpec(arr.shape, lambda i: (0,) * arr.ndim)

    gq2, gkv2 = gq.reshape(1, QL), gkv.reshape(1, KVL)
    return pl.pallas_call(
        body, name=name,
        out_shape=(jax.ShapeDtypeStruct((T, PA), BF16), jax.ShapeDtypeStruct((T, HQ), BF16),
                   jax.ShapeDtypeStruct((1, QL), F32), jax.ShapeDtypeStruct((1, KVL), F32)),
        grid=(T // tm,),
        in_specs=[row(HQ), row(HQ), row(H * LANE), row(QL), row(KVL), row(POOL_DIM), whole(wuq), whole(wukv),
                  whole(gq2), whole(gkv2), row(LANE), row(LANE), row(LANE)],
        out_specs=(row(PA), row(HQ), whole(gq2), whole(gkv2)),
        compiler_params=_params("arbitrary"),
    )(dq, dkv, dkr, ql, kvl, dxp, wuq, wukv, gq2, gkv2, *tabs)


def _pool_groups(x_of, S):
    row = lax.broadcasted_iota(jnp.int32, (S, POOL_G), 0)
    for g, w in enumerate(POOL_WINDOWS):
        x = x_of(g)
        s = x
        d = 1
        while d < w:
            s = s + jnp.where(row >= d, pltpu.roll(s, d, 0), 0.0)
            d *= 2
        cnt = jnp.minimum(row + 1, w).astype(F32)
        yield g, w, x, s / cnt - x, cnt, row


def _pool_fwd(xp, maps, scale, *, S, name):
    T = xp.shape[0]

    def body(xp_ref, maps_ref, scale_ref, ms_ref):
        for g, _, _, pooled, _, _ in _pool_groups(lambda g: xp_ref[:, g * POOL_G:(g + 1) * POOL_G], S):
            mixed = jnp.dot(pooled.astype(BF16), maps_ref[g].astype(BF16), preferred_element_type=F32)
            ms_ref[:, g * POOL_G:(g + 1) * POOL_G] = (mixed * scale_ref[:, g * POOL_G:(g + 1) * POOL_G]).astype(BF16)

    return pl.pallas_call(
        body, name=name,
        out_shape=jax.ShapeDtypeStruct((T, POOL_DIM), BF16),
        grid=(T // S,),
        in_specs=[pl.BlockSpec((S, POOL_DIM), lambda b: (b, 0)),
                  pl.BlockSpec(maps.shape, lambda b: (0, 0, 0)),
                  pl.BlockSpec((1, POOL_DIM), lambda b: (0, 0))],
        out_specs=pl.BlockSpec((S, POOL_DIM), lambda b: (b, 0)),
        compiler_params=_params("parallel"),
    )(xp, maps, scale.reshape(1, POOL_DIM))


def _pool_bwd(xp, dms, maps, scale, *, S, name):
    T = xp.shape[0]

    def body(xp_ref, dms_ref, maps_ref, scale_ref, dxp_ref, dmaps_ref, dscale_ref):
        first = pl.program_id(0) == 0
        for g, w, _, pooled, cnt, row in _pool_groups(lambda g: xp_ref[:, g * POOL_G:(g + 1) * POOL_G], S):
            cols = slice(g * POOL_G, (g + 1) * POOL_G)
            pooled_b = pooled.astype(BF16)
            maps_b = maps_ref[g].astype(BF16)
            mixed = jnp.dot(pooled_b, maps_b, preferred_element_type=F32)
            dms = dms_ref[:, cols]
            dscale = jnp.sum(dms * mixed, axis=0, keepdims=True)
            dmixed = (dms * scale_ref[:, cols]).astype(BF16)
            dmaps = lax.dot_general(pooled_b, dmixed, (((0,), (0,)), ((), ())), preferred_element_type=F32)
            dpooled = lax.dot_general(dmixed, maps_b, (((1,), (1,)), ((), ())), preferred_element_type=F32)
            z = dpooled / cnt
            d = 1
            while d < w:
                z = z + jnp.where(row < S - d, pltpu.roll(z, S - d, 0), 0.0)
                d *= 2
            dxp_ref[:, cols] = z - dpooled

            @pl.when(first)
            def _():
                dmaps_ref[g] = dmaps
                dscale_ref[:, cols] = dscale

            @pl.when(jnp.logical_not(first))
            def _():
                dmaps_ref[g] += dmaps
                dscale_ref[:, cols] += dscale

    seq = pl.BlockSpec((S, POOL_DIM), lambda b: (b, 0))
    maps_spec = pl.BlockSpec(maps.shape, lambda b: (0, 0, 0))
    vec = pl.BlockSpec((1, POOL_DIM), lambda b: (0, 0))
    return pl.pallas_call(
        body, name=name,
        out_shape=(jax.ShapeDtypeStruct((T, POOL_DIM), F32), jax.ShapeDtypeStruct(maps.shape, F32),
                   jax.ShapeDtypeStruct((1, POOL_DIM), F32)),
        grid=(T // S,),
        in_specs=[seq, seq, maps_spec, vec],
        out_specs=(seq, maps_spec, vec),
        compiler_params=_params("arbitrary"),
    )(xp, dms, maps, scale.reshape(1, POOL_DIM))


def _causal_mask(s, t):
    r = lax.broadcasted_iota(jnp.int32, (t, t), 0)
    c = lax.broadcasted_iota(jnp.int32, (t, t), 1)
    return jnp.where(r >= c, s, MASK_VALUE)


_NT = (((1,), (1,)), ((), ()))
_TN = (((0,), (0,)), ((), ()))


def _attn_fwd(q, kv, kr, *, S, name):
    T, HQ = q.shape
    H = HQ // HEAD_W
    B = T // S
    t = _tile(S, 256)
    nq = S // t

    def body(q_ref, k_ref, v_ref, kr_ref, o_ref, lse_ref, m_sc, l_sc, acc_sc):
        i = pl.program_id(2)
        qt = q_ref[...]
        m_sc[...] = jnp.full((t, LANE), MASK_VALUE, F32)
        l_sc[...] = jnp.zeros((t, LANE), F32)
        acc_sc[...] = jnp.zeros((t, V_DIM), F32)

        def tile(kt, masked):
            off = pl.multiple_of(kt * t, t)
            kc = jnp.concatenate([k_ref[pl.ds(off, t), :], kr_ref[pl.ds(off, t), :]], axis=1)
            s = lax.dot_general(qt, kc, _NT, preferred_element_type=F32) * ATTN_SCALE
            if masked:
                s = _causal_mask(s, t)
            m_prev = m_sc[...]
            m_new = jnp.maximum(m_prev, jnp.max(s, axis=1, keepdims=True))
            p = jnp.exp(s - m_new[:, :1])
            corr = jnp.exp(m_prev - m_new)
            l_sc[...] = corr * l_sc[...] + jnp.sum(p, axis=1, keepdims=True)
            acc_sc[...] = corr * acc_sc[...] + jnp.dot(p.astype(BF16), v_ref[pl.ds(off, t), :],
                                                       preferred_element_type=F32)
            m_sc[...] = m_new

        def unmasked(kt, carry):
            tile(kt, False)
            return carry

        lax.fori_loop(0, i, unmasked, 0)
        tile(i, True)
        l = l_sc[...]
        o_ref[...] = (acc_sc[...] / l).astype(BF16)
        lse_ref[...] = m_sc[...] + jnp.log(l)

    return pl.pallas_call(
        body, name=name,
        out_shape=(jax.ShapeDtypeStruct((T, H * V_DIM), BF16), jax.ShapeDtypeStruct((T, H * LANE), F32)),
        grid=(B, H, nq),
        in_specs=[pl.BlockSpec((t, HEAD_W), lambda b, h, i: (b * nq + i, h)),
                  pl.BlockSpec((S, QK_NOPE), lambda b, h, i: (b, 2 * h)),
                  pl.BlockSpec((S, V_DIM), lambda b, h, i: (b, 2 * h + 1)),
                  pl.BlockSpec((S, LANE), lambda b, h, i: (b, 0))],
        out_specs=(pl.BlockSpec((t, V_DIM), lambda b, h, i: (b * nq + i, h)),
                   pl.BlockSpec((t, LANE), lambda b, h, i: (b * nq + i, h))),
        scratch_shapes=[pltpu.VMEM((t, LANE), F32), pltpu.VMEM((t, LANE), F32), pltpu.VMEM((t, V_DIM), F32)],
        compiler_params=_params("parallel", "parallel", "arbitrary"),
    )(q, kv, kv, kr)


def _attn_bwd_dq(q, kv, kr, o, do, lse, *, S, name):
    T, HQ = q.shape
    H = HQ // HEAD_W
    B = T // S
    t = _tile(S, 256)
    nq = S // t

    def body(q_ref, k_ref, v_ref, kr_ref, o_ref, do_ref, lse_ref, dq_ref, delta_ref, acc_sc):
        i = pl.program_id(2)
        qt = q_ref[...]
        dot_ = do_ref[...]
        lse = lse_ref[:, :1]
        delta = jnp.sum(dot_.astype(F32) * o_ref[...].astype(F32), axis=1, keepdims=True)
        delta_ref[...] = jnp.broadcast_to(delta, (t, LANE))
        acc_sc[...] = jnp.zeros((t, HEAD_W), F32)

        def tile(kt, masked):
            off = pl.multiple_of(kt * t, t)
            kc = jnp.concatenate([k_ref[pl.ds(off, t), :], kr_ref[pl.ds(off, t), :]], axis=1)
            s = lax.dot_general(qt, kc, _NT, preferred_element_type=F32) * ATTN_SCALE
            if masked:
                s = _causal_mask(s, t)
            p = jnp.exp(s - lse)
            dp = lax.dot_general(dot_, v_ref[pl.ds(off, t), :], _NT, preferred_element_type=F32)
            ds = (p * (dp - delta) * ATTN_SCALE).astype(BF16)
            acc_sc[...] += jnp.dot(ds, kc, preferred_element_type=F32)

        def unmasked(kt, carry):
            tile(kt, False)
            return carry

        lax.fori_loop(0, i, unmasked, 0)
        tile(i, True)
        dq_ref[...] = acc_sc[...].astype(BF16)

    q_tile = pl.BlockSpec((t, HEAD_W), lambda b, h, i: (b * nq + i, h))
    h_tile = pl.BlockSpec((t, LANE), lambda b, h, i: (b * nq + i, h))
    return pl.pallas_call(
        body, name=name,
        out_shape=(jax.ShapeDtypeStruct((T, HQ), BF16), jax.ShapeDtypeStruct((T, H * LANE), F32)),
        grid=(B, H, nq),
        in_specs=[q_tile,
                  pl.BlockSpec((S, QK_NOPE), lambda b, h, i: (b, 2 * h)),
                  pl.BlockSpec((S, V_DIM), lambda b, h, i: (b, 2 * h + 1)),
                  pl.BlockSpec((S, LANE), lambda b, h, i: (b, 0)),
                  h_tile, h_tile, h_tile],
        out_specs=(q_tile, h_tile),
        scratch_shapes=[pltpu.VMEM((t, HEAD_W), F32)],
        compiler_params=_params("parallel", "parallel", "arbitrary"),
    )(q, kv, kv, kr, o, do, lse)


def _attn_bwd_dkv(q, kv, kr, do, lse, delta, *, S, name):
    T, HQ = q.shape
    H = HQ // HEAD_W
    B = T // S
    t = _tile(S, 256)
    nk = S // t

    def body(q_ref, k_ref, v_ref, kr_ref, do_ref, lse_ref, delta_ref, dkv_ref, dkr_ref, dk_sc, dv_sc):
        j = pl.program_id(2)
        kc = jnp.concatenate([k_ref[...], kr_ref[...]], axis=1)
        vt = v_ref[...]
        dk_sc[...] = jnp.zeros((t, HEAD_W), F32)
        dv_sc[...] = jnp.zeros((t, V_DIM), F32)

        def tile(qi, masked):
            off = pl.multiple_of(qi * t, t)
            qt = q_ref[pl.ds(off, t), :]
            dot_ = do_ref[pl.ds(off, t), :]
            s = lax.dot_general(qt, kc, _NT, preferred_element_type=F32) * ATTN_SCALE
            if masked:
                s = _causal_mask(s, t)
            p = jnp.exp(s - lse_ref[pl.ds(off, t), :][:, :1])
            dv_sc[...] += lax.dot_general(p.astype(BF16), dot_, _TN, preferred_element_type=F32)
            dp = lax.dot_general(dot_, vt, _NT, preferred_element_type=F32)
            ds = (p * (dp - delta_ref[pl.ds(off, t), :][:, :1]) * ATTN_SCALE).astype(BF16)
            dk_sc[...] += lax.dot_general(ds, qt, _TN, preferred_element_type=F32)

        def unmasked(qi, carry):
            tile(qi, False)
            return carry

        tile(j, True)
        lax.fori_loop(j + 1, nk, unmasked, 0)
        dkv_ref[:, :QK_NOPE] = dk_sc[:, :QK_NOPE].astype(BF16)
        dkv_ref[:, QK_NOPE:] = dv_sc[...].astype(BF16)
        dkr_ref[...] = dk_sc[:, QK_NOPE:]

    seq_h = pl.BlockSpec((S, LANE), lambda b, h, j: (b, h))
    return pl.pallas_call(
        body, name=name,
        out_shape=(jax.ShapeDtypeStruct((T, HQ), BF16), jax.ShapeDtypeStruct((T, H * LANE), F32)),
        grid=(B, H, nk),
        in_specs=[pl.BlockSpec((S, HEAD_W), lambda b, h, j: (b, h)),
                  pl.BlockSpec((t, QK_NOPE), lambda b, h, j: (b * nk + j, 2 * h)),
                  pl.BlockSpec((t, V_DIM), lambda b, h, j: (b * nk + j, 2 * h + 1)),
                  pl.BlockSpec((t, LANE), lambda b, h, j: (b * nk + j, 0)),
                  seq_h, seq_h, seq_h],
        out_specs=(pl.BlockSpec((t, HEAD_W), lambda b, h, j: (b * nk + j, h)),
                   pl.BlockSpec((t, LANE), lambda b, h, j: (b * nk + j, h))),
        scratch_shapes=[pltpu.VMEM((t, HEAD_W), F32), pltpu.VMEM((t, V_DIM), F32)],
        compiler_params=_params("parallel", "parallel", "arbitrary"),
    )(q, kv, kv, kr, do, lse, delta)


def _merge_out(h, ms, o, x, wgate, bgate, wpp, wap, wout, *, name):
    T, D = x.shape
    tm = _tile(T, 256, 16)

    def body(h_ref, ms_ref, o_ref, x_ref, wgate_ref, bgate_ref, wpp_ref, wap_ref, wout_ref,
             gates_ref, ba_ref, bb_ref, merged_ref, xn_ref):
        logits = jnp.dot(h_ref[...], wgate_ref[...], preferred_element_type=F32) + bgate_ref[...]
        gates = jax.nn.sigmoid(logits)
        ba = jnp.dot(ms_ref[...], wpp_ref[...], preferred_element_type=F32)
        bb = jnp.dot(o_ref[...], wap_ref[...], preferred_element_type=F32)
        merged = (gates[:, :D] * ba + gates[:, D:] * bb).astype(BF16)
        gates_ref[...] = gates.astype(BF16)
        ba_ref[...] = ba.astype(BF16)
        bb_ref[...] = bb.astype(BF16)
        merged_ref[...] = merged
        xn_ref[...] = x_ref[...] + jnp.dot(merged, wout_ref[...], preferred_element_type=F32)

    def row(w):
        return pl.BlockSpec((tm, w), lambda i: (i, 0))

    def whole(arr):
        return pl.BlockSpec(arr.shape, lambda i: (0,) * arr.ndim)

    bg2 = bgate.reshape(1, 2 * D)
    act = jax.ShapeDtypeStruct((T, D), BF16)
    return pl.pallas_call(
        body, name=name,
        out_shape=(jax.ShapeDtypeStruct((T, 2 * D), BF16), act, act, act, jax.ShapeDtypeStruct((T, D), F32)),
        grid=(T // tm,),
        in_specs=[row(D), row(ms.shape[1]), row(o.shape[1]), row(D), whole(wgate), whole(bg2), whole(wpp),
                  whole(wap), whole(wout)],
        out_specs=(row(2 * D), row(D), row(D), row(D), row(D)),
        compiler_params=_params("parallel"),
    )(h, ms, o, x, wgate, bg2, wpp, wap, wout)


def _merge_bwd(dxo, wout, gates, ba, bb, *, name):
    T, D = dxo.shape
    tm = _tile(T, 256, 16)

    def body(dxo_ref, wout_ref, gates_ref, ba_ref, bb_ref, dba_ref, dbb_ref, dgl_ref, dbg_ref):
        dm = lax.dot_general(dxo_ref[...].astype(BF16), wout_ref[...], _NT, preferred_element_type=F32)
        ga = gates_ref[:, :D].astype(F32)
        gb = gates_ref[:, D:].astype(F32)
        dba_ref[...] = (dm * ga).astype(BF16)
        dbb_ref[...] = (dm * gb).astype(BF16)
        dgl_a = dm * ba_ref[...].astype(F32) * (ga * (1.0 - ga))
        dgl_b = dm * bb_ref[...].astype(F32) * (gb * (1.0 - gb))
        dgl_ref[:, :D] = dgl_a.astype(BF16)
        dgl_ref[:, D:] = dgl_b.astype(BF16)
        sa = jnp.sum(dgl_a, axis=0, keepdims=True)
        sb = jnp.sum(dgl_b, axis=0, keepdims=True)

        @pl.when(pl.program_id(0) == 0)
        def _():
            dbg_ref[:, :D] = sa
            dbg_ref[:, D:] = sb

        @pl.when(pl.program_id(0) > 0)
        def _():
            dbg_ref[:, :D] += sa
            dbg_ref[:, D:] += sb

    def row(w):
        return pl.BlockSpec((tm, w), lambda i: (i, 0))

    act = jax.ShapeDtypeStruct((T, D), BF16)
    return pl.pallas_call(
        body, name=name,
        out_shape=(act, act, jax.ShapeDtypeStruct((T, 2 * D), BF16), jax.ShapeDtypeStruct((1, 2 * D), F32)),
        grid=(T // tm,),
        in_specs=[row(D), pl.BlockSpec(wout.shape, lambda i: (0, 0)), row(2 * D), row(D), row(D)],
        out_specs=(row(D), row(D), row(2 * D), pl.BlockSpec((1, 2 * D), lambda i: (0, 0))),
        compiler_params=_params("arbitrary"),
    )(dxo, wout, gates, ba, bb)


def _ffn_fwd(x, gain, w, tag):
    h = _rms_fwd(x, gain, name=f"{tag}_norm")
    gate, up, a = _ffn_up(h, w["wg"], w["wu"], name=f"{tag}_up")
    xn = _mm(a, w["wd"], res=x, alpha=0.5, name=f"{tag}_down", tk=1408)
    return xn, (x, h, gate, up, a)


def _ffn_bwd(dxo, gain, w, saved, tag):
    x, h, gate, up, a = saved
    dgate, dup = _ffn_bwd_act(dxo, w["wd"], gate, up, alpha=0.5, name=f"{tag}_bwd_act")
    dwd = _mm(a, dxo, ta=True, alpha=0.5, name=f"{tag}_dwd", tm=1408, tn=1024, tk=512)
    dwg = _mm(h, dgate, ta=True, name=f"{tag}_dwg", tm=1024, tn=1408, tk=512)
    dwu = _mm(h, dup, ta=True, name=f"{tag}_dwu", tm=1024, tn=1408, tk=512)
    dh = _mm(dgate, w["wg"], tb=True, name=f"{tag}_dh_gate", tk=1408)
    dh = _mm(dup, w["wu"], tb=True, res=dh, name=f"{tag}_dh_up", tk=1408)
    dx, dgain = _rms_bwd(x, gain, dh, dxo, name=f"{tag}_norm_bwd")
    return dx, dgain, dwg, dwu, dwd


def _mixer_fwd(x, p, w, tabs, S):
    h = _rms_fwd(x, p["norm_mix"], name="mix_norm")
    xp, ql, kvl, qn, kvn, q, kv, kr = _mixer_in(h, w["wa"], w["wuq"], w["wukv"], p["q_latent_norm"],
                                                 p["kv_latent_norm"], tabs, name="mix_in")
    ms = _pool_fwd(xp, p["pool_maps"], p["pool_scale"], S=S, name="pool_fwd")
    o, lse = _attn_fwd(q, kv, kr, S=S, name="attn_fwd")
    gates, ba, bb, merged, xn = _merge_out(h, ms, o, x, w["wgate"], p["b_gate"], w["wpp"], w["wap"], w["wout"],
                                           name="merge_out")
    return xn, (x, h, xp, ql, kvl, qn, kvn, q, kv, kr, ms, o, lse, gates, ba, bb, merged)


def _mixer_bwd(dxo, p, w, tabs, saved, S):
    x, h, xp, ql, kvl, qn, kvn, q, kv, kr, ms, o, lse, gates, ba, bb, merged = saved
    dba, dbb, dgl, dbg = _merge_bwd(dxo, w["wout"], gates, ba, bb, name="merge_bwd")
    g = {}
    g["wout"] = _mm(merged, dxo, ta=True, name="d_wout", tk=512)
    g["wpp"] = _mm(ms, dba, ta=True, name="d_wpp", tk=512)
    g["wap"] = _mm(o, dbb, ta=True, name="d_wap", tk=512)
    dms = _mm(dba, w["wpp"], tb=True, name="d_ms")
    do = _mm(dbb, w["wap"], tb=True, out_dtype=BF16, name="d_o")
    dxp, g["pool_maps"], g["pool_scale"] = _pool_bwd(xp, dms, p["pool_maps"], p["pool_scale"], S=S, name="pool_bwd")
    dq, delta = _attn_bwd_dq(q, kv, kr, o, do, lse, S=S, name="attn_bwd_dq")
    dkv, dkr = _attn_bwd_dkv(q, kv, kr, do, lse, delta, S=S, name="attn_bwd_dkv")
    dproj, dqp, g["q_latent_norm"], g["kv_latent_norm"] = _mixer_in_bwd(
        dq, dkv, dkr, ql, kvl, dxp, w["wuq"], w["wukv"], p["q_latent_norm"], p["kv_latent_norm"], tabs,
        name="mix_in_bwd")
    g["wuq"] = _mm(qn, dqp, ta=True, name="d_wuq", tk=512)
    g["wukv"] = _mm(kvn, dkv, ta=True, name="d_wukv", tk=512)
    g["wa"] = _mm(h, dproj, ta=True, name="d_wa", tn=1280, tk=512)
    g["wgate"] = _mm(h, dgl, ta=True, name="d_wgate", tk=512)
    dh = _mm(dproj, w["wa"], tb=True, name="dh_mix_a", tk=1280)
    dh = _mm(dgl, w["wgate"], tb=True, res=dh, name="dh_mix_gate")
    dx, g["norm_mix"] = _rms_bwd(x, p["norm_mix"], dh, dxo, name="mix_norm_bwd")
    g["b_gate"] = dbg
    return dx, g


def _local_step(x, positions, small, full, final_norm, target):
    B, S, D = x.shape
    T = B * S
    tabs = _rope_tables(positions.reshape(T))
    xs = x.reshape(T, D)
    saved = []
    for p, w in zip(small, full):
        xs, s1 = _ffn_fwd(xs, p["norm_ffn1"], w["ffn1"], "ffn1")
        xs, s2 = _mixer_fwd(xs, p, w, tabs, S)
        xs, s3 = _ffn_fwd(xs, p["norm_ffn2"], w["ffn2"], "ffn2")
        saved.append((s1, s2, s3))
    dx, dfinal, loss = _loss_head(xs, final_norm, target.reshape(T, D), name="loss_head")
    grads = []
    for p, w, (s1, s2, s3) in reversed(list(zip(small, full, saved))):
        g = {}
        dx, g["norm_ffn2"], g["ffn2_wg"], g["ffn2_wu"], g["ffn2_wd"] = _ffn_bwd(dx, p["norm_ffn2"], w["ffn2"], s3, "ffn2")
        dx, gm = _mixer_bwd(dx, p, w, tabs, s2, S)
        g.update(gm)
        dx, g["norm_ffn1"], g["ffn1_wg"], g["ffn1_wu"], g["ffn1_wd"] = _ffn_bwd(dx, p["norm_ffn1"], w["ffn1"], s1, "ffn1")
        grads.append(g)
    grads.reverse()
    return loss, dx.reshape(B, S, D), grads, dfinal


BIG = ("ffn1_up", "ffn1_down", "w_in", "w_pool_proj", "w_uq", "w_ukv", "w_attn_proj", "w_out", "ffn2_up", "ffn2_down")
SMALL = ("norm_ffn1", "norm_mix", "b_gate", "pool_maps", "pool_scale", "q_latent_norm", "kv_latent_norm", "norm_ffn2")


def _layout_full(nat):
    w_in, w_uq = nat["w_in"], nat["w_uq"]
    D = w_in.shape[0]
    QL = w_uq.shape[0]
    H = w_uq.shape[1] // (QK_NOPE + QK_ROPE)
    n_a = w_in.shape[1] - 2 * D
    wa = jnp.pad(w_in[:, :n_a], ((0, 0), (0, LANE - QK_ROPE)))
    wuq = jnp.pad(w_uq.reshape(QL, H, QK_NOPE + QK_ROPE), ((0, 0), (0, 0), (0, HEAD_W - QK_NOPE - QK_ROPE)))
    full = {"wa": wa, "wgate": w_in[:, n_a:], "wuq": wuq.reshape(QL, H * HEAD_W), "wukv": nat["w_ukv"],
            "wpp": nat["w_pool_proj"], "wap": nat["w_attn_proj"], "wout": nat["w_out"]}
    for tag in ("ffn1", "ffn2"):
        up = nat[tag + "_up"]
        F = up.shape[1] // 2
        full[tag] = {"wg": up[:, :F], "wu": up[:, F:], "wd": nat[tag + "_down"]}
    return full


def _grads_natural(g):
    D = g["wgate"].shape[0]
    QL = g["wuq"].shape[0]
    H = g["wuq"].shape[1] // HEAD_W
    n_a = g["wa"].shape[1] - (LANE - QK_ROPE)
    out = {
        "norm_ffn1": g["norm_ffn1"].reshape(D),
        "ffn1_up": jnp.concatenate([g["ffn1_wg"], g["ffn1_wu"]], axis=1),
        "ffn1_down": g["ffn1_wd"],
        "norm_mix": g["norm_mix"].reshape(D),
        "w_in": jnp.concatenate([g["wa"][:, :n_a], g["wgate"]], axis=1),
        "b_gate": g["b_gate"].reshape(2 * D),
        "pool_maps": g["pool_maps"],
        "pool_scale": g["pool_scale"].reshape(POOL_DIM),
        "w_pool_proj": g["wpp"],
        "q_latent_norm": g["q_latent_norm"].reshape(QL),
        "w_uq": g["wuq"].reshape(QL, H, HEAD_W)[:, :, :QK_NOPE + QK_ROPE].reshape(QL, H * (QK_NOPE + QK_ROPE)),
        "kv_latent_norm": g["kv_latent_norm"].reshape(-1),
        "w_ukv": g["wukv"],
        "w_attn_proj": g["wap"],
        "w_out": g["wout"],
        "norm_ffn2": g["norm_ffn2"].reshape(D),
        "ffn2_up": jnp.concatenate([g["ffn2_wg"], g["ffn2_wu"]], axis=1),
        "ffn2_down": g["ffn2_wd"],
    }
    return out


def _mesh_place():
    x, y, c = lax.axis_index("x"), lax.axis_index("y"), lax.axis_index("c")
    chips = [(1 - x, y), (x, 1 - y), (1 - x, 1 - y)]
    return x, y, c, chips


def _all_gather(block, *, name):
    R, C = block.shape

    def body(x_ref, out_ref, send_sems, recv_sems, local_sem):
        x, y, c, chips = _mesh_place()
        me, sibling = (x, y, c), (x, y, 1 - c)

        def slot(px, py, pc):
            return out_ref.at[4 * px + 2 * py + pc]

        def copy(k, blk, to, src=None):
            return pltpu.make_async_remote_copy(
                src_ref=slot(*blk) if src is None else src, dst_ref=slot(*blk),
                send_sem=send_sems.at[k], recv_sem=recv_sems.at[k], device_id=to, device_id_type=MESH)

        mine = pltpu.make_async_copy(x_ref, slot(*me), local_sem)
        mine.start()
        first = [copy(0, me, sibling, src=x_ref)]
        first += [copy(1 + j, me, (*chip, c), src=x_ref) for j, chip in enumerate(chips)]
        for cp in first:
            cp.start()
        passed = [copy(4 + j, (*chip, c), sibling) for j, chip in enumerate(chips)]
        for j, chip in enumerate(chips):
            copy(1 + j, (*chip, c), me).wait_recv()
            passed[j].start()
        copy(0, sibling, me).wait_recv()
        for j, chip in enumerate(chips):
            copy(4 + j, (*chip, 1 - c), me).wait_recv()
        for cp in first + passed:
            cp.wait_send()
        mine.wait()

    return pl.pallas_call(
        body, name=name,
        out_shape=jax.ShapeDtypeStruct((N_DEV, R, C), block.dtype),
        in_specs=[ANY], out_specs=ANY,
        scratch_shapes=[pltpu.SemaphoreType.DMA((7,)), pltpu.SemaphoreType.DMA((7,)), pltpu.SemaphoreType.DMA],
    )(block)


def _rs_pair(parts, *, name):
    _, _, R, C = parts.shape

    def body(p_ref, got_ref, send_sems, recv_sems):
        x, y, c, _ = _mesh_place()
        copies = [pltpu.make_async_remote_copy(
            src_ref=p_ref.at[k, 1 - c], dst_ref=got_ref.at[k], send_sem=send_sems.at[k], recv_sem=recv_sems.at[k],
            device_id=(x, y, 1 - c), device_id_type=MESH) for k in range(4)]
        for cp in copies:
            cp.start()
        for cp in copies:
            cp.wait()

    return pl.pallas_call(
        body, name=name,
        out_shape=jax.ShapeDtypeStruct((4, R, C), parts.dtype),
        in_specs=[ANY], out_specs=ANY,
        scratch_shapes=[pltpu.SemaphoreType.DMA((4,)), pltpu.SemaphoreType.DMA((4,))],
    )(parts)


def _rs_chips(sums, *, name):
    _, R, C = sums.shape

    def body(s_ref, got_ref, send_sems, recv_sems):
        x, y, c, chips = _mesh_place()
        copies = [pltpu.make_async_remote_copy(
            src_ref=s_ref.at[2 * cx + cy], dst_ref=got_ref.at[j], send_sem=send_sems.at[j], recv_sem=recv_sems.at[j],
            device_id=(cx, cy, c), device_id_type=MESH) for j, (cx, cy) in enumerate(chips)]
        for cp in copies:
            cp.start()
        for cp in copies:
            cp.wait()

    return pl.pallas_call(
        body, name=name,
        out_shape=jax.ShapeDtypeStruct((3, R, C), sums.dtype),
        in_specs=[ANY], out_specs=ANY,
        scratch_shapes=[pltpu.SemaphoreType.DMA((3,)), pltpu.SemaphoreType.DMA((3,))],
    )(sums)


def _rs_add(parts, got, core, *, name):
    _, _, R, C = parts.shape
    tr = _tile(R, 1024, 16)

    def body(core_ref, p_ref, g_ref, o_ref):
        o_ref[...] = p_ref[...] + g_ref[...]

    return pl.pallas_call(
        body, name=name,
        out_shape=jax.ShapeDtypeStruct((4, R, C), F32),
        grid_spec=pltpu.PrefetchScalarGridSpec(
            num_scalar_prefetch=1, grid=(4, R // tr),
            in_specs=[pl.BlockSpec((None, None, tr, C), lambda k, r, core_ref: (k, core_ref[0], r, 0)),
                      pl.BlockSpec((None, tr, C), lambda k, r, core_ref: (k, r, 0))],
            out_specs=pl.BlockSpec((None, tr, C), lambda k, r, core_ref: (k, r, 0))),
        compiler_params=_params("parallel", "parallel"),
    )(core, parts, got)


def _rs_final(sums, got, chip, *, name):
    _, R, C = sums.shape
    tr = _tile(R, 1024, 16)

    def body(chip_ref, s_ref, g_ref, o_ref):
        o_ref[...] = ((s_ref[...] + g_ref[0]) + g_ref[1]) + g_ref[2]

    return pl.pallas_call(
        body, name=name,
        out_shape=jax.ShapeDtypeStruct((R, C), F32),
        grid_spec=pltpu.PrefetchScalarGridSpec(
            num_scalar_prefetch=1, grid=(R // tr,),
            in_specs=[pl.BlockSpec((None, tr, C), lambda r, chip_ref: (chip_ref[0], r, 0)),
                      pl.BlockSpec((3, tr, C), lambda r, chip_ref: (0, r, 0))],
            out_specs=pl.BlockSpec((tr, C), lambda r, chip_ref: (r, 0))),
        compiler_params=_params("parallel"),
    )(chip, sums, got)


def _sum_devices(parts, *, name):
    _, R, C = parts.shape
    tr = _tile(R, 512, 8)

    def body(p_ref, o_ref):
        acc = p_ref[0]
        for d in range(1, N_DEV):
            acc = acc + p_ref[d]
        o_ref[...] = acc

    return pl.pallas_call(
        body, name=name,
        out_shape=jax.ShapeDtypeStruct((R, C), F32),
        grid=(R // tr,),
        in_specs=[pl.BlockSpec((N_DEV, tr, C), lambda r: (0, r, 0))],
        out_specs=pl.BlockSpec((tr, C), lambda r: (r, 0)),
        compiler_params=_params("parallel"),
    )(parts)


def _adamw(w, g, m, v, *, name):
    shape = w.shape
    cols = shape[-1]
    rows = w.size // cols
    tr = _tile(rows, 256, 8)

    def body(w_ref, g_ref, m_ref, v_ref, d_ref, nm_ref, nv_ref):
        g = g_ref[...]
        m = ADAM_B1 * m_ref[...] + (1.0 - ADAM_B1) * g
        v = ADAM_B2 * v_ref[...] + (1.0 - ADAM_B2) * jnp.square(g)
        m_hat = m / (1.0 - ADAM_B1 ** ADAM_STEP)
        v_hat = v / (1.0 - ADAM_B2 ** ADAM_STEP)
        d_ref[...] = -ADAM_LR * (m_hat / (jnp.sqrt(v_hat) + ADAM_EPS) + ADAM_WD * w_ref[...])
        nm_ref[...] = m
        nv_ref[...] = v

    spec = pl.BlockSpec((tr, cols), lambda i: (i, 0))
    out = jax.ShapeDtypeStruct((rows, cols), F32)
    d, nm, nv = pl.pallas_call(
        body, name=name,
        out_shape=(out, out, out),
        grid=(rows // tr,),
        in_specs=[spec] * 4, out_specs=(spec,) * 3,
        compiler_params=_params("parallel"),
    )(*(a.reshape(rows, cols) for a in (w, g, m, v)))
    return d.reshape(shape), nm.reshape(shape), nv.reshape(shape)


PACK_ALIGN = 16 * LANE
COL_SHARDED = ("ffn1_up", "w_in", "w_pool_proj", "w_uq", "w_ukv", "ffn2_up")


def _pack(pieces, lead):
    out = []
    for p in pieces:
        keep = p.shape[:lead]
        flat = p.reshape(*keep, -1)
        pad = (-flat.shape[-1]) % PACK_ALIGN
        if pad:
            flat = jnp.pad(flat, [(0, 0)] * lead + [(0, pad)])
        out.append(flat.reshape(*keep, -1, LANE))
    return jnp.concatenate(out, axis=lead)


def _unpack(buf, shapes, lead):
    keep = buf.shape[:lead]
    out, row = [], 0
    for shape in shapes:
        size = int(np.prod(shape))
        rows = -(-size // PACK_ALIGN) * (PACK_ALIGN // LANE)
        piece = lax.slice_in_dim(buf, row, row + rows, axis=lead).reshape(*keep, rows * LANE)
        out.append(lax.slice_in_dim(piece, 0, size, axis=lead).reshape(*keep, *shape))
        row += rows
    return out


def _stacked_to_full(name, stacked):
    if name in COL_SHARDED:
        n, k, cols = stacked.shape
        return stacked.transpose(1, 0, 2).reshape(k, n * cols)
    n, rows, cols = stacked.shape
    return stacked.reshape(n * rows, cols)


def _full_to_stacked(name, full):
    if name in COL_SHARDED:
        k, cols = full.shape
        return full.reshape(k, N_DEV, cols // N_DEV).transpose(1, 0, 2)
    rows, cols = full.shape
    return full.reshape(N_DEV, rows // N_DEV, cols)


def kernel(x, positions, norm_ffn1, ffn1_up, ffn1_down, norm_mix, w_in, b_gate, pool_maps, pool_scale, w_pool_proj, q_latent_norm, w_uq, kv_latent_norm, w_ukv, w_attn_proj, w_out, norm_ffn2, ffn2_up, ffn2_down, final_norm, loss_target, m_norm_ffn1, m_ffn1_up, m_ffn1_down, m_norm_mix, m_w_in, m_b_gate, m_pool_maps, m_pool_scale, m_w_pool_proj, m_q_latent_norm, m_w_uq, m_kv_latent_norm, m_w_ukv, m_w_attn_proj, m_w_out, m_norm_ffn2, m_ffn2_up, m_ffn2_down, m_final_norm, v_norm_ffn1, v_ffn1_up, v_ffn1_down, v_norm_mix, v_w_in, v_b_gate, v_pool_maps, v_pool_scale, v_w_pool_proj, v_q_latent_norm, v_w_uq, v_kv_latent_norm, v_w_ukv, v_w_attn_proj, v_w_out, v_norm_ffn2, v_ffn2_up, v_ffn2_down, v_final_norm):
    order = ("norm_ffn1", "ffn1_up", "ffn1_down", "norm_mix", "w_in", "b_gate", "pool_maps", "pool_scale",
             "w_pool_proj", "q_latent_norm", "w_uq", "kv_latent_norm", "w_ukv", "w_attn_proj", "w_out",
             "norm_ffn2", "ffn2_up", "ffn2_down", "final_norm")
    w = dict(zip(order, (norm_ffn1, ffn1_up, ffn1_down, norm_mix, w_in, b_gate, pool_maps, pool_scale, w_pool_proj,
                         q_latent_norm, w_uq, kv_latent_norm, w_ukv, w_attn_proj, w_out, norm_ffn2, ffn2_up,
                         ffn2_down, final_norm)))
    m = dict(zip(order, (m_norm_ffn1, m_ffn1_up, m_ffn1_down, m_norm_mix, m_w_in, m_b_gate, m_pool_maps, m_pool_scale,
                         m_w_pool_proj, m_q_latent_norm, m_w_uq, m_kv_latent_norm, m_w_ukv, m_w_attn_proj, m_w_out,
                         m_norm_ffn2, m_ffn2_up, m_ffn2_down, m_final_norm)))
    v = dict(zip(order, (v_norm_ffn1, v_ffn1_up, v_ffn1_down, v_norm_mix, v_w_in, v_b_gate, v_pool_maps, v_pool_scale,
                         v_w_pool_proj, v_q_latent_norm, v_w_uq, v_kv_latent_norm, v_w_ukv, v_w_attn_proj, v_w_out,
                         v_norm_ffn2, v_ffn2_up, v_ffn2_down, v_final_norm)))
    L = norm_ffn1.shape[0]
    shard_shapes = [w[n].shape[1:] for n in BIG]

    full, small = [], []
    for l in range(L):
        packed = _pack([w[n][l].astype(BF16) for n in BIG], 0)
        gathered = _unpack(_all_gather(packed, name="gather_weights"), shard_shapes, 1)
        full.append(_layout_full({n: _stacked_to_full(n, s) for n, s in zip(BIG, gathered)}))
        small.append({n: w[n][l] for n in SMALL})

    loss, grad_x, grads, dfinal = _local_step(x, positions, small, full, final_norm, loss_target)

    core = jnp.stack([lax.axis_index("c")]).astype(jnp.int32)
    chip = jnp.stack([2 * lax.axis_index("x") + lax.axis_index("y")]).astype(jnp.int32)
    big_grads = {n: [] for n in BIG}
    small_parts = []
    for l in range(L):
        nat = _grads_natural(grads[l])
        parts = _pack([_full_to_stacked(n, nat[n]) for n in BIG], 1)
        parts = parts.reshape(4, 2, *parts.shape[1:])
        sums = _rs_add(parts, _rs_pair(parts, name="rs_pair"), core, name="rs_add")
        mine = _rs_final(sums, _rs_chips(sums, name="rs_chips"), chip, name="rs_final")
        for n, g in zip(BIG, _unpack(mine, shard_shapes, 0)):
            big_grads[n].append(g)
        small_parts += [nat[n] for n in SMALL]

    small_parts += [dfinal, loss[0, :1]]
    small_shapes = [p.shape for p in small_parts]
    vec = _pack([jnp.concatenate([p.reshape(-1) for p in small_parts])], 0)
    total = _sum_devices(_all_gather(vec, name="gather_small"), name="sum_small")
    flat = total.reshape(-1)
    small_grads, at = [], 0
    for shape in small_shapes:
        size = int(np.prod(shape))
        small_grads.append(lax.slice_in_dim(flat, at, at + size).reshape(shape))
        at += size
    loss_total = small_grads[-1].reshape(())
    grad = {n: jnp.stack(big_grads[n]) for n in BIG}
    for i, n in enumerate(SMALL):
        grad[n] = jnp.stack([small_grads[l * len(SMALL) + i] for l in range(L)]).reshape(w[n].shape)
    grad["final_norm"] = small_grads[-2].reshape(final_norm.shape)

    delta, new_m, new_v = {}, {}, {}
    for n in order:
        wn, gn, mn, vn = (a.reshape(1, -1) if a.ndim == 1 else a for a in (w[n], grad[n], m[n], v[n]))
        d, nm, nv = _adamw(wn, gn, mn, vn, name="adamw_" + n)
        delta[n], new_m[n], new_v[n] = (a.reshape(w[n].shape) for a in (d, nm, nv))
    return (loss_total, grad_x, *[grad[n] for n in order], *[delta[n] for n in order],
            *[new_m[n] for n in order], *[new_v[n] for n in order])
```

```python
import functools

import numpy as np
import jax
import jax.numpy as jnp
from jax import lax
from jax.experimental import pallas as pl
from jax.experimental.pallas import tpu as pltpu

F32 = jnp.float32
BF16 = jnp.bfloat16

NORM_EPS = 1e-6
ROPE_THETA = 10000.0
QK_NOPE = 128
QK_ROPE = 64
V_DIM = 128
HEAD_W = 256
POOL_WINDOWS = (2, 4, 8, 16)
POOL_G = 128
POOL_DIM = 512
LANE = 128
ATTN_SCALE = float((QK_NOPE + QK_ROPE) ** -0.5)
MASK_VALUE = -1e30
ATTN_TILE = 512

ADAM_LR = 0.001
ADAM_B1 = 0.9
ADAM_B2 = 0.999
ADAM_EPS = 1e-08
ADAM_WD = 0.01
ADAM_STEP = 10

N_DEV = 8
VMEM_LIMIT = 52 * 1024 * 1024

MESH = pl.DeviceIdType.MESH
ANY = pl.BlockSpec(memory_space=pl.ANY)


def _tile(dim, target, align=LANE):
    if dim <= target:
        return dim
    t = (target // align) * align
    while t >= align:
        if dim % t == 0:
            return t
        t -= align
    return dim


def _params(*sem):
    return pltpu.CompilerParams(dimension_semantics=sem, vmem_limit_bytes=VMEM_LIMIT)


def _rstd(x):
    return lax.rsqrt(jnp.mean(x * x, axis=-1, keepdims=True) + NORM_EPS)


def _mm(a, b, *, name, ta=False, tb=False, out_dtype=F32, res=None, alpha=1.0, tm=512, tn=1024, tk=1024):
    if ta:
        K, M = a.shape
    else:
        M, K = a.shape
    if tb:
        N, K2 = b.shape
    else:
        K2, N = b.shape
    assert K == K2, (a.shape, b.shape, ta, tb)
    tm, tn, tk = _tile(M, tm), _tile(N, tn), _tile(K, tk)
    nk = K // tk
    dims = (((0 if ta else 1,), (1 if tb else 0,)), ((), ()))
    has_res = res is not None

    def body(*refs):
        a_ref, b_ref = refs[0], refs[1]
        res_ref = refs[2] if has_res else None
        o_ref = refs[2 + has_res]
        acc_ref = refs[3 + has_res] if nk > 1 else None
        part = lax.dot_general(a_ref[...].astype(BF16), b_ref[...].astype(BF16), dims,
                               preferred_element_type=F32)

        def finish(acc):
            r = acc * alpha if alpha != 1.0 else acc
            if has_res:
                r = res_ref[...].astype(F32) + r
            o_ref[...] = r.astype(out_dtype)

        if nk == 1:
            finish(part)
        else:
            k = pl.program_id(2)

            @pl.when(k == 0)
            def _():
                acc_ref[...] = part

            @pl.when(k > 0)
            def _():
                acc_ref[...] += part

            @pl.when(k == nk - 1)
            def _():
                finish(acc_ref[...])

    a_spec = pl.BlockSpec((tk, tm), lambda i, j, k: (k, i)) if ta else pl.BlockSpec((tm, tk), lambda i, j, k: (i, k))
    b_spec = pl.BlockSpec((tn, tk), lambda i, j, k: (j, k)) if tb else pl.BlockSpec((tk, tn), lambda i, j, k: (k, j))
    in_specs = [a_spec, b_spec]
    operands = [a, b]
    if has_res:
        in_specs.append(pl.BlockSpec((tm, tn), lambda i, j, k: (i, j)))
        operands.append(res)
    return pl.pallas_call(
        body, name=name,
        out_shape=jax.ShapeDtypeStruct((M, N), out_dtype),
        grid=(M // tm, N // tn, nk),
        in_specs=in_specs,
        out_specs=pl.BlockSpec((tm, tn), lambda i, j, k: (i, j)),
        scratch_shapes=[pltpu.VMEM((tm, tn), F32)] if nk > 1 else [],
        compiler_params=_params("parallel", "parallel", "arbitrary"),
    )(*operands)


def _rms_fwd(x, g, *, name):
    T, D = x.shape
    tm = _tile(T, 512, 16)

    def body(x_ref, g_ref, h_ref):
        x = x_ref[...]
        h_ref[...] = (x * _rstd(x) * g_ref[...]).astype(BF16)

    return pl.pallas_call(
        body, name=name,
        out_shape=jax.ShapeDtypeStruct((T, D), BF16),
        grid=(T // tm,),
        in_specs=[pl.BlockSpec((tm, D), lambda i: (i, 0)), pl.BlockSpec((1, D), lambda i: (0, 0))],
        out_specs=pl.BlockSpec((tm, D), lambda i: (i, 0)),
        compiler_params=_params("parallel"),
    )(x, g.reshape(1, D))


def _rms_bwd(x, g, dh, dxo, *, name):
    T, D = x.shape
    tm = _tile(T, 512, 16)

    def body(x_ref, g_ref, dh_ref, dxo_ref, dx_ref, dg_ref):
        x = x_ref[...]
        r = _rstd(x)
        xhat = x * r
        dh = dh_ref[...]
        dxh = dh * g_ref[...]
        dx_ref[...] = dxo_ref[...] + r * (dxh - xhat * jnp.mean(dxh * xhat, axis=-1, keepdims=True))
        part = jnp.sum(dh * xhat, axis=0, keepdims=True)

        @pl.when(pl.program_id(0) == 0)
        def _():
            dg_ref[...] = part

        @pl.when(pl.program_id(0) > 0)
        def _():
            dg_ref[...] += part

    row = pl.BlockSpec((tm, D), lambda i: (i, 0))
    vec = pl.BlockSpec((1, D), lambda i: (0, 0))
    return pl.pallas_call(
        body, name=name,
        out_shape=(jax.ShapeDtypeStruct((T, D), F32), jax.ShapeDtypeStruct((1, D), F32)),
        grid=(T // tm,),
        in_specs=[row, vec, row, row],
        out_specs=(row, vec),
        compiler_params=_params("arbitrary"),
    )(x, g.reshape(1, D), dh, dxo)


def _loss_head(x, g, target, *, name):
    T, D = x.shape
    tm = _tile(T, 512, 16)

    def body(x_ref, g_ref, t_ref, dx_ref, dg_ref, loss_ref):
        x = x_ref[...]
        gain = g_ref[...]
        r = _rstd(x)
        xhat = x * r
        err = xhat * gain - t_ref[...]
        dy = err * (1.0 / D)
        dxh = dy * gain
        dx_ref[...] = r * (dxh - xhat * jnp.mean(dxh * xhat, axis=-1, keepdims=True))
        dg_part = jnp.sum(dy * xhat, axis=0, keepdims=True)
        loss_part = jnp.full((1, LANE), 0.5 / D, F32) * jnp.sum(err * err)

        @pl.when(pl.program_id(0) == 0)
        def _():
            dg_ref[...] = dg_part
            loss_ref[...] = loss_part

        @pl.when(pl.program_id(0) > 0)
        def _():
            dg_ref[...] += dg_part
            loss_ref[...] += loss_part

    row = pl.BlockSpec((tm, D), lambda i: (i, 0))
    vec = pl.BlockSpec((1, D), lambda i: (0, 0))
    return pl.pallas_call(
        body, name=name,
        out_shape=(jax.ShapeDtypeStruct((T, D), F32), jax.ShapeDtypeStruct((1, D), F32),
                   jax.ShapeDtypeStruct((1, LANE), F32)),
        grid=(T // tm,),
        in_specs=[row, vec, row],
        out_specs=(row, vec, pl.BlockSpec((1, LANE), lambda i: (0, 0))),
        compiler_params=_params("arbitrary"),
    )(x, g.reshape(1, D), target)


def _ffn_up(h, wg, wu, *, name):
    T, D = h.shape
    F = wg.shape[1]
    tm, tn = _tile(T, 512, 16), _tile(F, 1408)

    def body(h_ref, wg_ref, wu_ref, gate_ref, up_ref, a_ref):
        h = h_ref[...]
        gate = jnp.dot(h, wg_ref[...], preferred_element_type=F32)
        up = jnp.dot(h, wu_ref[...], preferred_element_type=F32)
        gate_ref[...] = gate.astype(BF16)
        up_ref[...] = up.astype(BF16)
        a_ref[...] = (gate * jax.nn.sigmoid(gate) * up).astype(BF16)

    w_spec = pl.BlockSpec((D, tn), lambda j, i: (0, j))
    o_spec = pl.BlockSpec((tm, tn), lambda j, i: (i, j))
    out = jax.ShapeDtypeStruct((T, F), BF16)
    return pl.pallas_call(
        body, name=name,
        out_shape=(out, out, out),
        grid=(F // tn, T // tm),
        in_specs=[pl.BlockSpec((tm, D), lambda j, i: (i, 0)), w_spec, w_spec],
        out_specs=(o_spec, o_spec, o_spec),
        compiler_params=_params("parallel", "parallel"),
    )(h, wg, wu)


def _ffn_bwd_act(dxo, wd, gate, up, *, alpha, name):
    T, D = dxo.shape
    F = wd.shape[0]
    tm, tn = _tile(T, 512, 16), _tile(F, 1408)

    def body(dxo_ref, wd_ref, gate_ref, up_ref, dgate_ref, dup_ref):
        da = lax.dot_general(dxo_ref[...].astype(BF16), wd_ref[...], (((1,), (1,)), ((), ())),
                             preferred_element_type=F32) * alpha
        gate = gate_ref[...].astype(F32)
        up = up_ref[...].astype(F32)
        sig = jax.nn.sigmoid(gate)
        dgate_ref[...] = (da * up * (sig * (1.0 + gate * (1.0 - sig)))).astype(BF16)
        dup_ref[...] = (da * (gate * sig)).astype(BF16)

    t_spec = pl.BlockSpec((tm, tn), lambda j, i: (i, j))
    out = jax.ShapeDtypeStruct((T, F), BF16)
    return pl.pallas_call(
        body, name=name,
        out_shape=(out, out),
        grid=(F // tn, T // tm),
        in_specs=[pl.BlockSpec((tm, D), lambda j, i: (i, 0)), pl.BlockSpec((tn, D), lambda j, i: (j, 0)),
                  t_spec, t_spec],
        out_specs=(t_spec, t_spec),
        compiler_params=_params("parallel", "parallel"),
    )(dxo, wd, gate, up)


def _rope_tables(positions):
    half = QK_ROPE // 2
    inv_freq = ROPE_THETA ** (-jnp.arange(0, QK_ROPE, 2, dtype=F32) / QK_ROPE)
    ang = positions.astype(F32)[:, None] * inv_freq
    cos, sin = jnp.cos(ang), jnp.sin(ang)
    z = jnp.zeros_like(cos)
    zz = jnp.zeros((positions.shape[0], LANE - QK_ROPE), F32)
    c = jnp.concatenate([cos, cos, zz], axis=1)
    sa = jnp.concatenate([z, sin, zz], axis=1)
    sb = jnp.concatenate([-sin, z, zz], axis=1)
    return c, sa, sb


def _rotate(seg, c, sa, sb, sign):
    half = QK_ROPE // 2
    mix = pltpu.roll(seg, half, 1) * sa + pltpu.roll(seg, LANE - half, 1) * sb
    return seg * c + mix if sign > 0 else seg * c - mix


def _mixer_in(h, wa, wuq, wukv, gq, gkv, tabs, *, name):
    T, D = h.shape
    PA = wa.shape[1]
    QL, HQ = wuq.shape
    KVL = wukv.shape[0]
    H = HQ // HEAD_W
    o_q, o_kv, o_kr = POOL_DIM, POOL_DIM + QL, POOL_DIM + QL + KVL
    assert PA == o_kr + LANE
    tm = _tile(T, 256, 16)

    def body(h_ref, wa_ref, wuq_ref, wukv_ref, gq_ref, gkv_ref, c_ref, sa_ref, sb_ref,
             xp_ref, ql_ref, kvl_ref, qn_ref, kvn_ref, q_ref, kv_ref, kr_ref):
        proj = jnp.dot(h_ref[...], wa_ref[...], preferred_element_type=F32)
        xp_ref[...] = proj[:, :POOL_DIM]
        ql = proj[:, o_q:o_kv]
        kvl = proj[:, o_kv:o_kr]
        ql_ref[...] = ql
        kvl_ref[...] = kvl
        qn = (ql * _rstd(ql) * gq_ref[...]).astype(BF16)
        kvn = (kvl * _rstd(kvl) * gkv_ref[...]).astype(BF16)
        qn_ref[...] = qn
        kvn_ref[...] = kvn
        c, sa, sb = c_ref[...], sa_ref[...], sb_ref[...]
        q = jnp.dot(qn, wuq_ref[...], preferred_element_type=F32)
        for hh in range(H):
            base = hh * HEAD_W
            q_ref[:, base:base + QK_NOPE] = q[:, base:base + QK_NOPE].astype(BF16)
            q_ref[:, base + QK_NOPE:base + HEAD_W] = _rotate(
                q[:, base + QK_NOPE:base + HEAD_W], c, sa, sb, 1).astype(BF16)
        kv_ref[...] = jnp.dot(kvn, wukv_ref[...], preferred_element_type=F32).astype(BF16)
        kr_ref[...] = _rotate(proj[:, o_kr:o_kr + LANE], c, sa, sb, 1).astype(BF16)

    def row(w):
        return pl.BlockSpec((tm, w), lambda i: (i, 0))

    def whole(arr):
        return pl.BlockSpec(arr.shape, lambda i: (0,) * arr.ndim)

    gq2, gkv2 = gq.reshape(1, QL), gkv.reshape(1, KVL)
    outs = [(POOL_DIM, F32), (QL, F32), (KVL, F32), (QL, BF16), (KVL, BF16), (HQ, BF16), (HQ, BF16), (LANE, BF16)]
    return pl.pallas_call(
        body, name=name,
        out_shape=tuple(jax.ShapeDtypeStruct((T, w), dt) for w, dt in outs),
        grid=(T // tm,),
        in_specs=[row(D), whole(wa), whole(wuq), whole(wukv), whole(gq2), whole(gkv2),
                  row(LANE), row(LANE), row(LANE)],
        out_specs=tuple(row(w) for w, _ in outs),
        compiler_params=_params("parallel"),
    )(h, wa, wuq, wukv, gq2, gkv2, *tabs)


def _mixer_in_bwd(dq, dkv, dkr, ql, kvl, dxp, wuq, wukv, gq, gkv, tabs, *, name):
    T, HQ = dq.shape
    QL, KVL = wuq.shape[0], wukv.shape[0]
    H = HQ // HEAD_W
    PA = POOL_DIM + QL + KVL + LANE
    o_q, o_kv, o_kr = POOL_DIM, POOL_DIM + QL, POOL_DIM + QL + KVL
    tm = _tile(T, 256, 16)

    def norm_bwd(lat, gain, dn):
        r = _rstd(lat)
        xhat = lat * r
        dxh = dn * gain
        dlat = r * (dxh - xhat * jnp.mean(dxh * xhat, axis=-1, keepdims=True))
        return dlat, jnp.sum(dn * xhat, axis=0, keepdims=True)

    def body(dq_ref, dkv_ref, dkr_ref, ql_ref, kvl_ref, dxp_ref, wuq_ref, wukv_ref, gq_ref, gkv_ref,
             c_ref, sa_ref, sb_ref, dproj_ref, dqp_ref, dgq_ref, dgkv_ref):
        c, sa, sb = c_ref[...], sa_ref[...], sb_ref[...]
        dkr_sum = dkr_ref[:, :LANE]
        for hh in range(H):
            base = hh * HEAD_W
            dqp_ref[:, base:base + QK_NOPE] = dq_ref[:, base:base + QK_NOPE]
            dqp_ref[:, base + QK_NOPE:base + HEAD_W] = _rotate(
                dq_ref[:, base + QK_NOPE:base + HEAD_W].astype(F32), c, sa, sb, -1).astype(BF16)
            if hh:
                dkr_sum = dkr_sum + dkr_ref[:, hh * LANE:(hh + 1) * LANE]
        contract_last = (((1,), (1,)), ((), ()))
        dqn = lax.dot_general(dqp_ref[...], wuq_ref[...], contract_last, preferred_element_type=F32)
        dkvn = lax.dot_general(dkv_ref[...], wukv_ref[...], contract_last, preferred_element_type=F32)
        dql, dgq = norm_bwd(ql_ref[...], gq_ref[...], dqn)
        dkvl, dgkv = norm_bwd(kvl_ref[...], gkv_ref[...], dkvn)
        dproj_ref[:, :POOL_DIM] = dxp_ref[...].astype(BF16)
        dproj_ref[:, o_q:o_kv] = dql.astype(BF16)
        dproj_ref[:, o_kv:o_kr] = dkvl.astype(BF16)
        dproj_ref[:, o_kr:PA] = _rotate(dkr_sum, c, sa, sb, -1).astype(BF16)

        @pl.when(pl.program_id(0) == 0)
        def _():
            dgq_ref[...] = dgq
            dgkv_ref[...] = dgkv

        @pl.when(pl.program_id(0) > 0)
        def _():
            dgq_ref[...] += dgq
            dgkv_ref[...] += dgkv

    def row(w):
        return pl.BlockSpec((tm, w), lambda i: (i, 0))

    def whole(arr):
        return pl.BlockSpec(arr.shape, lambda i: (0,) * arr.ndim)

    gq2, gkv2 = gq.reshape(1, QL), gkv.reshape(1, KVL)
    return pl.pallas_call(
        body, name=name,
        out_shape=(jax.ShapeDtypeStruct((T, PA), BF16), jax.ShapeDtypeStruct((T, HQ), BF16),
                   jax.ShapeDtypeStruct((1, QL), F32), jax.ShapeDtypeStruct((1, KVL), F32)),
        grid=(T // tm,),
        in_specs=[row(HQ), row(HQ), row(H * LANE), row(QL), row(KVL), row(POOL_DIM), whole(wuq), whole(wukv),
                  whole(gq2), whole(gkv2), row(LANE), row(LANE), row(LANE)],
        out_specs=(row(PA), row(HQ), whole(gq2), whole(gkv2)),
        compiler_params=_params("arbitrary"),
    )(dq, dkv, dkr, ql, kvl, dxp, wuq, wukv, gq2, gkv2, *tabs)


def _pool_groups(x_of, S):
    row = lax.broadcasted_iota(jnp.int32, (S, POOL_G), 0)
    for g, w in enumerate(POOL_WINDOWS):
        x = x_of(g)
        s = x
        d = 1
        while d < w:
            s = s + jnp.where(row >= d, pltpu.roll(s, d, 0), 0.0)
            d *= 2
        cnt = jnp.minimum(row + 1, w).astype(F32)
        yield g, w, x, s / cnt - x, cnt, row


def _pool_fwd(xp, maps, scale, *, S, name):
    T = xp.shape[0]

    def body(xp_ref, maps_ref, scale_ref, ms_ref):
        for g, _, _, pooled, _, _ in _pool_groups(lambda g: xp_ref[:, g * POOL_G:(g + 1) * POOL_G], S):
            mixed = jnp.dot(pooled.astype(BF16), maps_ref[g].astype(BF16), preferred_element_type=F32)
            ms_ref[:, g * POOL_G:(g + 1) * POOL_G] = (mixed * scale_ref[:, g * POOL_G:(g + 1) * POOL_G]).astype(BF16)

    return pl.pallas_call(
        body, name=name,
        out_shape=jax.ShapeDtypeStruct((T, POOL_DIM), BF16),
        grid=(T // S,),
        in_specs=[pl.BlockSpec((S, POOL_DIM), lambda b: (b, 0)),
                  pl.BlockSpec(maps.shape, lambda b: (0, 0, 0)),
                  pl.BlockSpec((1, POOL_DIM), lambda b: (0, 0))],
        out_specs=pl.BlockSpec((S, POOL_DIM), lambda b: (b, 0)),
        compiler_params=_params("parallel"),
    )(xp, maps, scale.reshape(1, POOL_DIM))


def _pool_bwd(xp, dms, maps, scale, *, S, name):
    T = xp.shape[0]

    def body(xp_ref, dms_ref, maps_ref, scale_ref, dxp_ref, dmaps_ref, dscale_ref):
        first = pl.program_id(0) == 0
        for g, w, _, pooled, cnt, row in _pool_groups(lambda g: xp_ref[:, g * POOL_G:(g + 1) * POOL_G], S):
            cols = slice(g * POOL_G, (g + 1) * POOL_G)
            pooled_b = pooled.astype(BF16)
            maps_b = maps_ref[g].astype(BF16)
            mixed = jnp.dot(pooled_b, maps_b, preferred_element_type=F32)
            dms = dms_ref[:, cols]
            dscale = jnp.sum(dms * mixed, axis=0, keepdims=True)
            dmixed = (dms * scale_ref[:, cols]).astype(BF16)
            dmaps = lax.dot_general(pooled_b, dmixed, (((0,), (0,)), ((), ())), preferred_element_type=F32)
            dpooled = lax.dot_general(dmixed, maps_b, (((1,), (1,)), ((), ())), preferred_element_type=F32)
            z = dpooled / cnt
            d = 1
            while d < w:
                z = z + jnp.where(row < S - d, pltpu.roll(z, S - d, 0), 0.0)
                d *= 2
            dxp_ref[:, cols] = z - dpooled

            @pl.when(first)
            def _():
                dmaps_ref[g] = dmaps
                dscale_ref[:, cols] = dscale

            @pl.when(jnp.logical_not(first))
            def _():
                dmaps_ref[g] += dmaps
                dscale_ref[:, cols] += dscale

    seq = pl.BlockSpec((S, POOL_DIM), lambda b: (b, 0))
    maps_spec = pl.BlockSpec(maps.shape, lambda b: (0, 0, 0))
    vec = pl.BlockSpec((1, POOL_DIM), lambda b: (0, 0))
    return pl.pallas_call(
        body, name=name,
        out_shape=(jax.ShapeDtypeStruct((T, POOL_DIM), F32), jax.ShapeDtypeStruct(maps.shape, F32),
                   jax.ShapeDtypeStruct((1, POOL_DIM), F32)),
        grid=(T // S,),
        in_specs=[seq, seq, maps_spec, vec],
        out_specs=(seq, maps_spec, vec),
        compiler_params=_params("arbitrary"),
    )(xp, dms, maps, scale.reshape(1, POOL_DIM))


def _causal_mask(s, t):
    r = lax.broadcasted_iota(jnp.int32, (t, t), 0)
    c = lax.broadcasted_iota(jnp.int32, (t, t), 1)
    return jnp.where(r >= c, s, MASK_VALUE)


_NT = (((1,), (1,)), ((), ()))
_TN = (((0,), (0,)), ((), ()))


def _attn_fwd(q, kv, kr, *, S, name):
    T, HQ = q.shape
    H = HQ // HEAD_W
    B = T // S
    t = _tile(S, ATTN_TILE)
    n = S // t

    def body(q_ref, k_ref, v_ref, kr_ref, o_ref, lse_ref, kcat):
        kcat[:, :QK_NOPE] = k_ref[...]
        kcat[:, QK_NOPE:] = kr_ref[...]
        for i in range(n):
            rows = slice(i * t, (i + 1) * t)
            qt = q_ref[rows, :]
            m = jnp.full((t, 1), MASK_VALUE, F32)
            l = jnp.zeros((t, 1), F32)
            acc = jnp.zeros((t, V_DIM), F32)
            for j in range(i + 1):
                cols = slice(j * t, (j + 1) * t)
                s = lax.dot_general(qt, kcat[cols, :], _NT, preferred_element_type=F32) * ATTN_SCALE
                if j == i:
                    s = _causal_mask(s, t)
                m_new = jnp.maximum(m, jnp.max(s, axis=1, keepdims=True))
                p = jnp.exp(s - m_new)
                corr = jnp.exp(m - m_new)
                l = corr * l + jnp.sum(p, axis=1, keepdims=True)
                acc = corr * acc + jnp.dot(p.astype(BF16), v_ref[cols, :], preferred_element_type=F32)
                m = m_new
            o_ref[rows, :] = (acc / l).astype(BF16)
            lse_ref[rows, :] = jnp.broadcast_to(m + jnp.log(l), (t, LANE))

    seq_h = pl.BlockSpec((S, LANE), lambda b, h: (b, h))
    return pl.pallas_call(
        body, name=name,
        out_shape=(jax.ShapeDtypeStruct((T, H * V_DIM), BF16), jax.ShapeDtypeStruct((T, H * LANE), F32)),
        grid=(B, H),
        in_specs=[pl.BlockSpec((S, HEAD_W), lambda b, h: (b, h)),
                  pl.BlockSpec((S, QK_NOPE), lambda b, h: (b, 2 * h)),
                  pl.BlockSpec((S, V_DIM), lambda b, h: (b, 2 * h + 1)),
                  pl.BlockSpec((S, LANE), lambda b, h: (b, 0))],
        out_specs=(seq_h, seq_h),
        scratch_shapes=[pltpu.VMEM((S, HEAD_W), BF16)],
        compiler_params=_params("parallel", "parallel"),
    )(q, kv, kv, kr)


def _attn_bwd(q, kv, kr, o, do, lse, *, S, name):
    T, HQ = q.shape
    H = HQ // HEAD_W
    B = T // S
    t = _tile(S, ATTN_TILE)
    n = S // t

    def body(q_ref, k_ref, v_ref, kr_ref, o_ref, do_ref, lse_ref, dq_ref, dkv_ref, dkr_ref, kcat, dq_acc):
        kcat[:, :QK_NOPE] = k_ref[...]
        kcat[:, QK_NOPE:] = kr_ref[...]
        delta = [jnp.sum(do_ref[i * t:(i + 1) * t, :].astype(F32) * o_ref[i * t:(i + 1) * t, :].astype(F32),
                         axis=1, keepdims=True) for i in range(n)]
        for j in range(n):
            cols = slice(j * t, (j + 1) * t)
            kc = kcat[cols, :]
            vt = v_ref[cols, :]
            dk = jnp.zeros((t, HEAD_W), F32)
            dv = jnp.zeros((t, V_DIM), F32)
            for i in range(j, n):
                rows = slice(i * t, (i + 1) * t)
                qt = q_ref[rows, :]
                dot_ = do_ref[rows, :]
                s = lax.dot_general(qt, kc, _NT, preferred_element_type=F32) * ATTN_SCALE
                if i == j:
                    s = _causal_mask(s, t)
                p = jnp.exp(s - lse_ref[rows, :][:, :1])
                dv = dv + lax.dot_general(p.astype(BF16), dot_, _TN, preferred_element_type=F32)
                dp = lax.dot_general(dot_, vt, _NT, preferred_element_type=F32)
                ds = (p * (dp - delta[i]) * ATTN_SCALE).astype(BF16)
                dk = dk + lax.dot_general(ds, qt, _TN, preferred_element_type=F32)
                dq_part = jnp.dot(ds, kc, preferred_element_type=F32)
                if j == 0:
                    dq_acc[rows, :] = dq_part
                else:
                    dq_acc[rows, :] += dq_part
            dkv_ref[cols, :QK_NOPE] = dk[:, :QK_NOPE].astype(BF16)
            dkv_ref[cols, QK_NOPE:] = dv.astype(BF16)
            dkr_ref[cols, :] = dk[:, QK_NOPE:]
        dq_ref[...] = dq_acc[...].astype(BF16)

    seq_q = pl.BlockSpec((S, HEAD_W), lambda b, h: (b, h))
    seq_h = pl.BlockSpec((S, LANE), lambda b, h: (b, h))
    return pl.pallas_call(
        body, name=name,
        out_shape=(jax.ShapeDtypeStruct((T, HQ), BF16), jax.ShapeDtypeStruct((T, HQ), BF16),
                   jax.ShapeDtypeStruct((T, H * LANE), F32)),
        grid=(B, H),
        in_specs=[seq_q,
                  pl.BlockSpec((S, QK_NOPE), lambda b, h: (b, 2 * h)),
                  pl.BlockSpec((S, V_DIM), lambda b, h: (b, 2 * h + 1)),
                  pl.BlockSpec((S, LANE), lambda b, h: (b, 0)),
                  seq_h, seq_h, seq_h],
        out_specs=(seq_q, seq_q, seq_h),
        scratch_shapes=[pltpu.VMEM((S, HEAD_W), BF16), pltpu.VMEM((S, HEAD_W), F32)],
        compiler_params=_params("parallel", "parallel"),
    )(q, kv, kv, kr, o, do, lse)


def _merge_out(h, ms, o, x, wgate, bgate, wpp, wap, wout, *, name):
    T, D = x.shape
    tm = _tile(T, 256, 16)

    def body(h_ref, ms_ref, o_ref, x_ref, wgate_ref, bgate_ref, wpp_ref, wap_ref, wout_ref,
             gates_ref, ba_ref, bb_ref, merged_ref, xn_ref):
        logits = jnp.dot(h_ref[...], wgate_ref[...], preferred_element_type=F32) + bgate_ref[...]
        gates = jax.nn.sigmoid(logits)
        ba = jnp.dot(ms_ref[...], wpp_ref[...], preferred_element_type=F32)
        bb = jnp.dot(o_ref[...], wap_ref[...], preferred_element_type=F32)
        merged = (gates[:, :D] * ba + gates[:, D:] * bb).astype(BF16)
        gates_ref[...] = gates.astype(BF16)
        ba_ref[...] = ba.astype(BF16)
        bb_ref[...] = bb.astype(BF16)
        merged_ref[...] = merged
        xn_ref[...] = x_ref[...] + jnp.dot(merged, wout_ref[...], preferred_element_type=F32)

    def row(w):
        return pl.BlockSpec((tm, w), lambda i: (i, 0))

    def whole(arr):
        return pl.BlockSpec(arr.shape, lambda i: (0,) * arr.ndim)

    bg2 = bgate.reshape(1, 2 * D)
    act = jax.ShapeDtypeStruct((T, D), BF16)
    return pl.pallas_call(
        body, name=name,
        out_shape=(jax.ShapeDtypeStruct((T, 2 * D), BF16), act, act, act, jax.ShapeDtypeStruct((T, D), F32)),
        grid=(T // tm,),
        in_specs=[row(D), row(ms.shape[1]), row(o.shape[1]), row(D), whole(wgate), whole(bg2), whole(wpp),
                  whole(wap), whole(wout)],
        out_specs=(row(2 * D), row(D), row(D), row(D), row(D)),
        compiler_params=_params("parallel"),
    )(h, ms, o, x, wgate, bg2, wpp, wap, wout)


def _merge_bwd(dxo, wout, gates, ba, bb, *, name):
    T, D = dxo.shape
    tm = _tile(T, 256, 16)

    def body(dxo_ref, wout_ref, gates_ref, ba_ref, bb_ref, dba_ref, dbb_ref, dgl_ref, dbg_ref):
        dm = lax.dot_general(dxo_ref[...].astype(BF16), wout_ref[...], _NT, preferred_element_type=F32)
        ga = gates_ref[:, :D].astype(F32)
        gb = gates_ref[:, D:].astype(F32)
        dba_ref[...] = (dm * ga).astype(BF16)
        dbb_ref[...] = (dm * gb).astype(BF16)
        dgl_a = dm * ba_ref[...].astype(F32) * (ga * (1.0 - ga))
        dgl_b = dm * bb_ref[...].astype(F32) * (gb * (1.0 - gb))
        dgl_ref[:, :D] = dgl_a.astype(BF16)
        dgl_ref[:, D:] = dgl_b.astype(BF16)
        sa = jnp.sum(dgl_a, axis=0, keepdims=True)
        sb = jnp.sum(dgl_b, axis=0, keepdims=True)

        @pl.when(pl.program_id(0) == 0)
        def _():
            dbg_ref[:, :D] = sa
            dbg_ref[:, D:] = sb

        @pl.when(pl.program_id(0) > 0)
        def _():
            dbg_ref[:, :D] += sa
            dbg_ref[:, D:] += sb

    def row(w):
        return pl.BlockSpec((tm, w), lambda i: (i, 0))

    act = jax.ShapeDtypeStruct((T, D), BF16)
    return pl.pallas_call(
        body, name=name,
        out_shape=(act, act, jax.ShapeDtypeStruct((T, 2 * D), BF16), jax.ShapeDtypeStruct((1, 2 * D), F32)),
        grid=(T // tm,),
        in_specs=[row(D), pl.BlockSpec(wout.shape, lambda i: (0, 0)), row(2 * D), row(D), row(D)],
        out_specs=(row(D), row(D), row(2 * D), pl.BlockSpec((1, 2 * D), lambda i: (0, 0))),
        compiler_params=_params("arbitrary"),
    )(dxo, wout, gates, ba, bb)


def _ffn_fwd(x, gain, w, tag):
    h = _rms_fwd(x, gain, name=f"{tag}_norm")
    gate, up, a = _ffn_up(h, w["wg"], w["wu"], name=f"{tag}_up")
    xn = _mm(a, w["wd"], res=x, alpha=0.5, name=f"{tag}_down", tk=2816)
    return xn, (x, h, gate, up, a)


def _ffn_bwd(dxo, gain, w, saved, tag):
    x, h, gate, up, a = saved
    dgate, dup = _ffn_bwd_act(dxo, w["wd"], gate, up, alpha=0.5, name=f"{tag}_bwd_act")
    dwd = _mm(a, dxo, ta=True, alpha=0.5, out_dtype=BF16, name=f"{tag}_dwd", tm=1408, tn=1024, tk=1024)
    dwg = _mm(h, dgate, ta=True, out_dtype=BF16, name=f"{tag}_dwg", tm=1024, tn=1408, tk=2048)
    dwu = _mm(h, dup, ta=True, out_dtype=BF16, name=f"{tag}_dwu", tm=1024, tn=1408, tk=2048)
    dh = _mm(dgate, w["wg"], tb=True, name=f"{tag}_dh_gate", tk=2816)
    dh = _mm(dup, w["wu"], tb=True, res=dh, name=f"{tag}_dh_up", tk=2816)
    dx, dgain = _rms_bwd(x, gain, dh, dxo, name=f"{tag}_norm_bwd")
    return dx, dgain, dwg, dwu, dwd


def _mixer_fwd(x, p, w, tabs, S):
    h = _rms_fwd(x, p["norm_mix"], name="mix_norm")
    xp, ql, kvl, qn, kvn, q, kv, kr = _mixer_in(h, w["wa"], w["wuq"], w["wukv"], p["q_latent_norm"],
                                                 p["kv_latent_norm"], tabs, name="mix_in")
    ms = _pool_fwd(xp, p["pool_maps"], p["pool_scale"], S=S, name="pool_fwd")
    o, lse = _attn_fwd(q, kv, kr, S=S, name="attn_fwd")
    gates, ba, bb, merged, xn = _merge_out(h, ms, o, x, w["wgate"], p["b_gate"], w["wpp"], w["wap"], w["wout"],
                                           name="merge_out")
    return xn, (x, h, xp, ql, kvl, qn, kvn, q, kv, kr, ms, o, lse, gates, ba, bb, merged)


def _mixer_bwd(dxo, p, w, tabs, saved, S):
    x, h, xp, ql, kvl, qn, kvn, q, kv, kr, ms, o, lse, gates, ba, bb, merged = saved
    dba, dbb, dgl, dbg = _merge_bwd(dxo, w["wout"], gates, ba, bb, name="merge_bwd")
    g = {}
    g["wout"] = _mm(merged, dxo, ta=True, out_dtype=BF16, name="d_wout", tm=1024, tk=1024)
    g["wpp"] = _mm(ms, dba, ta=True, out_dtype=BF16, name="d_wpp", tk=2048)
    g["wap"] = _mm(o, dbb, ta=True, out_dtype=BF16, name="d_wap", tm=1024, tk=2048)
    dms = _mm(dba, w["wpp"], tb=True, name="d_ms")
    do = _mm(dbb, w["wap"], tb=True, out_dtype=BF16, name="d_o")
    dxp, g["pool_maps"], g["pool_scale"] = _pool_bwd(xp, dms, p["pool_maps"], p["pool_scale"], S=S, name="pool_bwd")
    dq, dkv, dkr = _attn_bwd(q, kv, kr, o, do, lse, S=S, name="attn_bwd")
    dproj, dqp, g["q_latent_norm"], g["kv_latent_norm"] = _mixer_in_bwd(
        dq, dkv, dkr, ql, kvl, dxp, w["wuq"], w["wukv"], p["q_latent_norm"], p["kv_latent_norm"], tabs,
        name="mix_in_bwd")
    g["wuq"] = _mm(qn, dqp, ta=True, out_dtype=BF16, name="d_wuq", tn=2048, tk=2048)
    g["wukv"] = _mm(kvn, dkv, ta=True, out_dtype=BF16, name="d_wukv", tn=2048, tk=2048)
    g["wa"] = _mm(h, dproj, ta=True, out_dtype=BF16, name="d_wa", tm=1024, tn=1280, tk=2048)
    g["wgate"] = _mm(h, dgl, ta=True, out_dtype=BF16, name="d_wgate", tm=1024, tn=2048, tk=1024)
    dh = _mm(dproj, w["wa"], tb=True, name="dh_mix_a", tk=1280)
    dh = _mm(dgl, w["wgate"], tb=True, res=dh, name="dh_mix_gate", tk=2048)
    dx, g["norm_mix"] = _rms_bwd(x, p["norm_mix"], dh, dxo, name="mix_norm_bwd")
    g["b_gate"] = dbg
    return dx, g


def _local_step(x, positions, small, full, final_norm, target):
    B, S, D = x.shape
    T = B * S
    tabs = _rope_tables(positions.reshape(T))
    xs = x.reshape(T, D)
    saved = []
    for p, w in zip(small, full):
        xs, s1 = _ffn_fwd(xs, p["norm_ffn1"], w["ffn1"], "ffn1")
        xs, s2 = _mixer_fwd(xs, p, w, tabs, S)
        xs, s3 = _ffn_fwd(xs, p["norm_ffn2"], w["ffn2"], "ffn2")
        saved.append((s1, s2, s3))
    dx, dfinal, loss = _loss_head(xs, final_norm, target.reshape(T, D), name="loss_head")
    grads = []
    for p, w, (s1, s2, s3) in reversed(list(zip(small, full, saved))):
        g = {}
        dx, g["norm_ffn2"], g["ffn2_wg"], g["ffn2_wu"], g["ffn2_wd"] = _ffn_bwd(dx, p["norm_ffn2"], w["ffn2"], s3, "ffn2")
        dx, gm = _mixer_bwd(dx, p, w, tabs, s2, S)
        g.update(gm)
        dx, g["norm_ffn1"], g["ffn1_wg"], g["ffn1_wu"], g["ffn1_wd"] = _ffn_bwd(dx, p["norm_ffn1"], w["ffn1"], s1, "ffn1")
        grads.append(g)
    grads.reverse()
    return loss, dx.reshape(B, S, D), grads, dfinal


BIG = ("ffn1_up", "ffn1_down", "w_in", "w_pool_proj", "w_uq", "w_ukv", "w_attn_proj", "w_out", "ffn2_up", "ffn2_down")
SMALL = ("norm_ffn1", "norm_mix", "b_gate", "pool_maps", "pool_scale", "q_latent_norm", "kv_latent_norm", "norm_ffn2")


def _layout_full(nat):
    w_in, w_uq = nat["w_in"], nat["w_uq"]
    D = w_in.shape[0]
    QL = w_uq.shape[0]
    H = w_uq.shape[1] // (QK_NOPE + QK_ROPE)
    n_a = w_in.shape[1] - 2 * D
    wa = jnp.pad(w_in[:, :n_a], ((0, 0), (0, LANE - QK_ROPE)))
    wuq = jnp.pad(w_uq.reshape(QL, H, QK_NOPE + QK_ROPE), ((0, 0), (0, 0), (0, HEAD_W - QK_NOPE - QK_ROPE)))
    full = {"wa": wa, "wgate": w_in[:, n_a:], "wuq": wuq.reshape(QL, H * HEAD_W), "wukv": nat["w_ukv"],
            "wpp": nat["w_pool_proj"], "wap": nat["w_attn_proj"], "wout": nat["w_out"]}
    for tag in ("ffn1", "ffn2"):
        up = nat[tag + "_up"]
        F = up.shape[1] // 2
        full[tag] = {"wg": up[:, :F], "wu": up[:, F:], "wd": nat[tag + "_down"]}
    return full


def _grads_natural(g):
    D = g["wgate"].shape[0]
    QL = g["wuq"].shape[0]
    H = g["wuq"].shape[1] // HEAD_W
    n_a = g["wa"].shape[1] - (LANE - QK_ROPE)
    out = {
        "norm_ffn1": g["norm_ffn1"].reshape(D),
        "ffn1_up": jnp.concatenate([g["ffn1_wg"], g["ffn1_wu"]], axis=1),
        "ffn1_down": g["ffn1_wd"],
        "norm_mix": g["norm_mix"].reshape(D),
        "w_in": jnp.concatenate([g["wa"][:, :n_a], g["wgate"]], axis=1),
        "b_gate": g["b_gate"].reshape(2 * D),
        "pool_maps": g["pool_maps"],
        "pool_scale": g["pool_scale"].reshape(POOL_DIM),
        "w_pool_proj": g["wpp"],
        "q_latent_norm": g["q_latent_norm"].reshape(QL),
        "w_uq": g["wuq"].reshape(QL, H, HEAD_W)[:, :, :QK_NOPE + QK_ROPE].reshape(QL, H * (QK_NOPE + QK_ROPE)),
        "kv_latent_norm": g["kv_latent_norm"].reshape(-1),
        "w_ukv": g["wukv"],
        "w_attn_proj": g["wap"],
        "w_out": g["wout"],
        "norm_ffn2": g["norm_ffn2"].reshape(D),
        "ffn2_up": jnp.concatenate([g["ffn2_wg"], g["ffn2_wu"]], axis=1),
        "ffn2_down": g["ffn2_wd"],
    }
    return out


def _mesh_place():
    x, y, c = lax.axis_index("x"), lax.axis_index("y"), lax.axis_index("c")
    chips = [(1 - x, y), (x, 1 - y), (1 - x, 1 - y)]
    return x, y, c, chips


def _all_gather(block, *, name):
    R, C = block.shape

    def body(x_ref, out_ref, send_sems, recv_sems, local_sem):
        x, y, c, chips = _mesh_place()
        me, sibling = (x, y, c), (x, y, 1 - c)

        def slot(px, py, pc):
            return out_ref.at[4 * px + 2 * py + pc]

        def copy(k, blk, to, src=None):
            return pltpu.make_async_remote_copy(
                src_ref=slot(*blk) if src is None else src, dst_ref=slot(*blk),
                send_sem=send_sems.at[k], recv_sem=recv_sems.at[k], device_id=to, device_id_type=MESH)

        mine = pltpu.make_async_copy(x_ref, slot(*me), local_sem)
        mine.start()
        first = [copy(0, me, sibling, src=x_ref)]
        first += [copy(1 + j, me, (*chip, c), src=x_ref) for j, chip in enumerate(chips)]
        for cp in first:
            cp.start()
        passed = [copy(4 + j, (*chip, c), sibling) for j, chip in enumerate(chips)]
        for j, chip in enumerate(chips):
            copy(1 + j, (*chip, c), me).wait_recv()
            passed[j].start()
        copy(0, sibling, me).wait_recv()
        for j, chip in enumerate(chips):
            copy(4 + j, (*chip, 1 - c), me).wait_recv()
        for cp in first + passed:
            cp.wait_send()
        mine.wait()

    return pl.pallas_call(
        body, name=name,
        out_shape=jax.ShapeDtypeStruct((N_DEV, R, C), block.dtype),
        in_specs=[ANY], out_specs=ANY,
        scratch_shapes=[pltpu.SemaphoreType.DMA((7,)), pltpu.SemaphoreType.DMA((7,)), pltpu.SemaphoreType.DMA],
    )(block)


def _rs_pair(parts, *, name):
    _, _, R, C = parts.shape

    def body(p_ref, got_ref, send_sems, recv_sems):
        x, y, c, _ = _mesh_place()
        copies = [pltpu.make_async_remote_copy(
            src_ref=p_ref.at[k, 1 - c], dst_ref=got_ref.at[k], send_sem=send_sems.at[k], recv_sem=recv_sems.at[k],
            device_id=(x, y, 1 - c), device_id_type=MESH) for k in range(4)]
        for cp in copies:
            cp.start()
        for cp in copies:
            cp.wait()

    return pl.pallas_call(
        body, name=name,
        out_shape=jax.ShapeDtypeStruct((4, R, C), parts.dtype),
        in_specs=[ANY], out_specs=ANY,
        scratch_shapes=[pltpu.SemaphoreType.DMA((4,)), pltpu.SemaphoreType.DMA((4,))],
    )(parts)


def _rs_chips(sums, *, name):
    _, R, C = sums.shape

    def body(s_ref, got_ref, send_sems, recv_sems):
        x, y, c, chips = _mesh_place()
        copies = [pltpu.make_async_remote_copy(
            src_ref=s_ref.at[2 * cx + cy], dst_ref=got_ref.at[j], send_sem=send_sems.at[j], recv_sem=recv_sems.at[j],
            device_id=(cx, cy, c), device_id_type=MESH) for j, (cx, cy) in enumerate(chips)]
        for cp in copies:
            cp.start()
        for cp in copies:
            cp.wait()

    return pl.pallas_call(
        body, name=name,
        out_shape=jax.ShapeDtypeStruct((3, R, C), sums.dtype),
        in_specs=[ANY], out_specs=ANY,
        scratch_shapes=[pltpu.SemaphoreType.DMA((3,)), pltpu.SemaphoreType.DMA((3,))],
    )(sums)


def _rs_add(parts, got, core, *, name):
    _, _, R, C = parts.shape
    tr = _tile(R, 1024, 16)

    def body(core_ref, p_ref, g_ref, o_ref):
        o_ref[...] = (p_ref[...].astype(F32) + g_ref[...].astype(F32)).astype(o_ref.dtype)

    return pl.pallas_call(
        body, name=name,
        out_shape=jax.ShapeDtypeStruct((4, R, C), parts.dtype),
        grid_spec=pltpu.PrefetchScalarGridSpec(
            num_scalar_prefetch=1, grid=(4, R // tr),
            in_specs=[pl.BlockSpec((None, None, tr, C), lambda k, r, core_ref: (k, core_ref[0], r, 0)),
                      pl.BlockSpec((None, tr, C), lambda k, r, core_ref: (k, r, 0))],
            out_specs=pl.BlockSpec((None, tr, C), lambda k, r, core_ref: (k, r, 0))),
        compiler_params=_params("parallel", "parallel"),
    )(core, parts, got)


def _rs_final(sums, got, chip, *, name):
    _, R, C = sums.shape
    tr = _tile(R, 1024, 16)

    def body(chip_ref, s_ref, g_ref, o_ref):
        o_ref[...] = ((s_ref[...].astype(F32) + g_ref[0].astype(F32)) + g_ref[1].astype(F32)) + g_ref[2].astype(F32)

    return pl.pallas_call(
        body, name=name,
        out_shape=jax.ShapeDtypeStruct((R, C), F32),
        grid_spec=pltpu.PrefetchScalarGridSpec(
            num_scalar_prefetch=1, grid=(R // tr,),
            in_specs=[pl.BlockSpec((None, tr, C), lambda r, chip_ref: (chip_ref[0], r, 0)),
                      pl.BlockSpec((3, tr, C), lambda r, chip_ref: (0, r, 0))],
            out_specs=pl.BlockSpec((tr, C), lambda r, chip_ref: (r, 0))),
        compiler_params=_params("parallel"),
    )(chip, sums, got)


def _sum_devices(parts, *, name):
    _, R, C = parts.shape
    tr = _tile(R, 512, 8)

    def body(p_ref, o_ref):
        acc = p_ref[0]
        for d in range(1, N_DEV):
            acc = acc + p_ref[d]
        o_ref[...] = acc

    return pl.pallas_call(
        body, name=name,
        out_shape=jax.ShapeDtypeStruct((R, C), F32),
        grid=(R // tr,),
        in_specs=[pl.BlockSpec((N_DEV, tr, C), lambda r: (0, r, 0))],
        out_specs=pl.BlockSpec((tr, C), lambda r: (r, 0)),
        compiler_params=_params("parallel"),
    )(parts)


def _adamw(w, g, m, v, *, name):
    shape = w.shape
    cols = shape[-1]
    rows = w.size // cols
    tr = _tile(rows, 256, 8)

    def body(w_ref, g_ref, m_ref, v_ref, d_ref, nm_ref, nv_ref):
        g = g_ref[...]
        m = ADAM_B1 * m_ref[...] + (1.0 - ADAM_B1) * g
        v = ADAM_B2 * v_ref[...] + (1.0 - ADAM_B2) * jnp.square(g)
        m_hat = m / (1.0 - ADAM_B1 ** ADAM_STEP)
        v_hat = v / (1.0 - ADAM_B2 ** ADAM_STEP)
        d_ref[...] = -ADAM_LR * (m_hat / (jnp.sqrt(v_hat) + ADAM_EPS) + ADAM_WD * w_ref[...])
        nm_ref[...] = m
        nv_ref[...] = v

    spec = pl.BlockSpec((tr, cols), lambda i: (i, 0))
    out = jax.ShapeDtypeStruct((rows, cols), F32)
    d, nm, nv = pl.pallas_call(
        body, name=name,
        out_shape=(out, out, out),
        grid=(rows // tr,),
        in_specs=[spec] * 4, out_specs=(spec,) * 3,
        compiler_params=_params("parallel"),
    )(*(a.reshape(rows, cols) for a in (w, g, m, v)))
    return d.reshape(shape), nm.reshape(shape), nv.reshape(shape)


PACK_ALIGN = 16 * LANE
COL_SHARDED = ("ffn1_up", "w_in", "w_pool_proj", "w_uq", "w_ukv", "ffn2_up")


def _pack(pieces, lead):
    out = []
    for p in pieces:
        keep = p.shape[:lead]
        flat = p.reshape(*keep, -1)
        pad = (-flat.shape[-1]) % PACK_ALIGN
        if pad:
            flat = jnp.pad(flat, [(0, 0)] * lead + [(0, pad)])
        out.append(flat.reshape(*keep, -1, LANE))
    return jnp.concatenate(out, axis=lead)


def _unpack(buf, shapes, lead):
    keep = buf.shape[:lead]
    out, row = [], 0
    for shape in shapes:
        size = int(np.prod(shape))
        rows = -(-size // PACK_ALIGN) * (PACK_ALIGN // LANE)
        piece = lax.slice_in_dim(buf, row, row + rows, axis=lead).reshape(*keep, rows * LANE)
        out.append(lax.slice_in_dim(piece, 0, size, axis=lead).reshape(*keep, *shape))
        row += rows
    return out


def _stacked_to_full(name, stacked):
    if name in COL_SHARDED:
        n, k, cols = stacked.shape
        return stacked.transpose(1, 0, 2).reshape(k, n * cols)
    n, rows, cols = stacked.shape
    return stacked.reshape(n * rows, cols)


def _full_to_stacked(name, full):
    if name in COL_SHARDED:
        k, cols = full.shape
        return full.reshape(k, N_DEV, cols // N_DEV).transpose(1, 0, 2)
    rows, cols = full.shape
    return full.reshape(N_DEV, rows // N_DEV, cols)


def kernel(x, positions, norm_ffn1, ffn1_up, ffn1_down, norm_mix, w_in, b_gate, pool_maps, pool_scale, w_pool_proj, q_latent_norm, w_uq, kv_latent_norm, w_ukv, w_attn_proj, w_out, norm_ffn2, ffn2_up, ffn2_down, final_norm, loss_target, m_norm_ffn1, m_ffn1_up, m_ffn1_down, m_norm_mix, m_w_in, m_b_gate, m_pool_maps, m_pool_scale, m_w_pool_proj, m_q_latent_norm, m_w_uq, m_kv_latent_norm, m_w_ukv, m_w_attn_proj, m_w_out, m_norm_ffn2, m_ffn2_up, m_ffn2_down, m_final_norm, v_norm_ffn1, v_ffn1_up, v_ffn1_down, v_norm_mix, v_w_in, v_b_gate, v_pool_maps, v_pool_scale, v_w_pool_proj, v_q_latent_norm, v_w_uq, v_kv_latent_norm, v_w_ukv, v_w_attn_proj, v_w_out, v_norm_ffn2, v_ffn2_up, v_ffn2_down, v_final_norm):
    order = ("norm_ffn1", "ffn1_up", "ffn1_down", "norm_mix", "w_in", "b_gate", "pool_maps", "pool_scale",
             "w_pool_proj", "q_latent_norm", "w_uq", "kv_latent_norm", "w_ukv", "w_attn_proj", "w_out",
             "norm_ffn2", "ffn2_up", "ffn2_down", "final_norm")
    w = dict(zip(order, (norm_ffn1, ffn1_up, ffn1_down, norm_mix, w_in, b_gate, pool_maps, pool_scale, w_pool_proj,
                         q_latent_norm, w_uq, kv_latent_norm, w_ukv, w_attn_proj, w_out, norm_ffn2, ffn2_up,
                         ffn2_down, final_norm)))
    m = dict(zip(order, (m_norm_ffn1, m_ffn1_up, m_ffn1_down, m_norm_mix, m_w_in, m_b_gate, m_pool_maps, m_pool_scale,
                         m_w_pool_proj, m_q_latent_norm, m_w_uq, m_kv_latent_norm, m_w_ukv, m_w_attn_proj, m_w_out,
                         m_norm_ffn2, m_ffn2_up, m_ffn2_down, m_final_norm)))
    v = dict(zip(order, (v_norm_ffn1, v_ffn1_up, v_ffn1_down, v_norm_mix, v_w_in, v_b_gate, v_pool_maps, v_pool_scale,
                         v_w_pool_proj, v_q_latent_norm, v_w_uq, v_kv_latent_norm, v_w_ukv, v_w_attn_proj, v_w_out,
                         v_norm_ffn2, v_ffn2_up, v_ffn2_down, v_final_norm)))
    L = norm_ffn1.shape[0]
    shard_shapes = [w[n].shape[1:] for n in BIG]

    full, small = [], []
    for l in range(L):
        packed = _pack([w[n][l].astype(BF16) for n in BIG], 0)
        gathered = _unpack(_all_gather(packed, name="gather_weights"), shard_shapes, 1)
        full.append(_layout_full({n: _stacked_to_full(n, s) for n, s in zip(BIG, gathered)}))
        small.append({n: w[n][l] for n in SMALL})

    loss, grad_x, grads, dfinal = _local_step(x, positions, small, full, final_norm, loss_target)

    core = jnp.stack([lax.axis_index("c")]).astype(jnp.int32)
    chip = jnp.stack([2 * lax.axis_index("x") + lax.axis_index("y")]).astype(jnp.int32)
    big_grads = {n: [] for n in BIG}
    small_parts = []
    for l in range(L):
        nat = _grads_natural(grads[l])
        parts = _pack([_full_to_stacked(n, nat[n]) for n in BIG], 1)
        parts = parts.reshape(4, 2, *parts.shape[1:])
        sums = _rs_add(parts, _rs_pair(parts, name="rs_pair"), core, name="rs_add")
        mine = _rs_final(sums, _rs_chips(sums, name="rs_chips"), chip, name="rs_final")
        for n, g in zip(BIG, _unpack(mine, shard_shapes, 0)):
            big_grads[n].append(g)
        small_parts += [nat[n] for n in SMALL]

    small_parts += [dfinal, loss[0, :1]]
    small_shapes = [p.shape for p in small_parts]
    vec = _pack([jnp.concatenate([p.reshape(-1) for p in small_parts])], 0)
    total = _sum_devices(_all_gather(vec, name="gather_small"), name="sum_small")
    flat = total.reshape(-1)
    small_grads, at = [], 0
    for shape in small_shapes:
        size = int(np.prod(shape))
        small_grads.append(lax.slice_in_dim(flat, at, at + size).reshape(shape))
        at += size
    loss_total = small_grads[-1].reshape(())
    grad = {n: jnp.stack(big_grads[n]) for n in BIG}
    for i, n in enumerate(SMALL):
        grad[n] = jnp.stack([small_grads[l * len(SMALL) + i] for l in range(L)]).reshape(w[n].shape)
    grad["final_norm"] = small_grads[-2].reshape(final_norm.shape)

    delta, new_m, new_v = {}, {}, {}
    for n in order:
        wn, gn, mn, vn = (a.reshape(1, -1) if a.ndim == 1 else a for a in (w[n], grad[n], m[n], v[n]))
        d, nm, nv = _adamw(wn, gn, mn, vn, name="adamw_" + n)
        delta[n], new_m[n], new_v[n] = (a.reshape(w[n].shape) for a in (d, nm, nv))
    return (loss_total, grad_x, *[grad[n] for n in order], *[delta[n] for n in order],
            *[new_m[n] for n in order], *[new_v[n] for n in order])
```

```python
import functools

import numpy as np
import jax
import jax.numpy as jnp
from jax import lax
from jax.experimental import pallas as pl
from jax.experimental.pallas import tpu as pltpu

F32 = jnp.float32
BF16 = jnp.bfloat16

NORM_EPS = 1e-6
ROPE_THETA = 10000.0
QK_NOPE = 128
QK_ROPE = 64
V_DIM = 128
HEAD_W = 256
POOL_WINDOWS = (2, 4, 8, 16)
POOL_G = 128
POOL_DIM = 512
LANE = 128
ATTN_SCALE = float((QK_NOPE + QK_ROPE) ** -0.5)
MASK_VALUE = -1e30
ATTN_TILE = 512

ADAM_LR = 0.001
ADAM_B1 = 0.9
ADAM_B2 = 0.999
ADAM_EPS = 1e-08
ADAM_WD = 0.01
ADAM_STEP = 10

N_DEV = 8
VMEM_LIMIT = 52 * 1024 * 1024

MESH = pl.DeviceIdType.MESH
ANY = pl.BlockSpec(memory_space=pl.ANY)


def _tile(dim, target, align=LANE):
    if dim <= target:
        return dim
    t = (target // align) * align
    while t >= align:
        if dim % t == 0:
            return t
        t -= align
    return dim


def _params(*sem):
    return pltpu.CompilerParams(dimension_semantics=sem, vmem_limit_bytes=VMEM_LIMIT)


def _rstd(x):
    return lax.rsqrt(jnp.mean(x * x, axis=-1, keepdims=True) + NORM_EPS)


def _mm(a, b, *, name, ta=False, tb=False, out_dtype=F32, res=None, alpha=1.0, tm=512, tn=1024, tk=1024):
    if ta:
        K, M = a.shape
    else:
        M, K = a.shape
    if tb:
        N, K2 = b.shape
    else:
        K2, N = b.shape
    assert K == K2, (a.shape, b.shape, ta, tb)
    tm, tn, tk = _tile(M, tm), _tile(N, tn), _tile(K, tk)
    nk = K // tk
    dims = (((0 if ta else 1,), (1 if tb else 0,)), ((), ()))
    has_res = res is not None

    def body(*refs):
        a_ref, b_ref = refs[0], refs[1]
        res_ref = refs[2] if has_res else None
        o_ref = refs[2 + has_res]
        acc_ref = refs[3 + has_res] if nk > 1 else None
        part = lax.dot_general(a_ref[...].astype(BF16), b_ref[...].astype(BF16), dims,
                               preferred_element_type=F32)

        def finish(acc):
            r = acc * alpha if alpha != 1.0 else acc
            if has_res:
                r = res_ref[...].astype(F32) + r
            o_ref[...] = r.astype(out_dtype)

        if nk == 1:
            finish(part)
        else:
            k = pl.program_id(2)

            @pl.when(k == 0)
            def _():
                acc_ref[...] = part

            @pl.when(k > 0)
            def _():
                acc_ref[...] += part

            @pl.when(k == nk - 1)
            def _():
                finish(acc_ref[...])

    a_spec = pl.BlockSpec((tk, tm), lambda i, j, k: (k, i)) if ta else pl.BlockSpec((tm, tk), lambda i, j, k: (i, k))
    b_spec = pl.BlockSpec((tn, tk), lambda i, j, k: (j, k)) if tb else pl.BlockSpec((tk, tn), lambda i, j, k: (k, j))
    in_specs = [a_spec, b_spec]
    operands = [a, b]
    if has_res:
        in_specs.append(pl.BlockSpec((tm, tn), lambda i, j, k: (i, j)))
        operands.append(res)
    return pl.pallas_call(
        body, name=name,
        out_shape=jax.ShapeDtypeStruct((M, N), out_dtype),
        grid=(M // tm, N // tn, nk),
        in_specs=in_specs,
        out_specs=pl.BlockSpec((tm, tn), lambda i, j, k: (i, j)),
        scratch_shapes=[pltpu.VMEM((tm, tn), F32)] if nk > 1 else [],
        compiler_params=_params("parallel", "parallel", "arbitrary"),
    )(*operands)


def _rms_fwd(x, g, *, name):
    T, D = x.shape
    tm = _tile(T, 512, 16)

    def body(x_ref, g_ref, h_ref):
        x = x_ref[...]
        h_ref[...] = (x * _rstd(x) * g_ref[...]).astype(BF16)

    return pl.pallas_call(
        body, name=name,
        out_shape=jax.ShapeDtypeStruct((T, D), BF16),
        grid=(T // tm,),
        in_specs=[pl.BlockSpec((tm, D), lambda i: (i, 0)), pl.BlockSpec((1, D), lambda i: (0, 0))],
        out_specs=pl.BlockSpec((tm, D), lambda i: (i, 0)),
        compiler_params=_params("parallel"),
    )(x, g.reshape(1, D))


def _rms_bwd(x, g, dh, dxo, *, name):
    T, D = x.shape
    tm = _tile(T, 512, 16)

    def body(x_ref, g_ref, dh_ref, dxo_ref, dx_ref, dg_ref):
        x = x_ref[...]
        r = _rstd(x)
        xhat = x * r
        dh = dh_ref[...]
        dxh = dh * g_ref[...]
        dx_ref[...] = dxo_ref[...] + r * (dxh - xhat * jnp.mean(dxh * xhat, axis=-1, keepdims=True))
        part = jnp.sum(dh * xhat, axis=0, keepdims=True)

        @pl.when(pl.program_id(0) == 0)
        def _():
            dg_ref[...] = part

        @pl.when(pl.program_id(0) > 0)
        def _():
            dg_ref[...] += part

    row = pl.BlockSpec((tm, D), lambda i: (i, 0))
    vec = pl.BlockSpec((1, D), lambda i: (0, 0))
    return pl.pallas_call(
        body, name=name,
        out_shape=(jax.ShapeDtypeStruct((T, D), F32), jax.ShapeDtypeStruct((1, D), F32)),
        grid=(T // tm,),
        in_specs=[row, vec, row, row],
        out_specs=(row, vec),
        compiler_params=_params("arbitrary"),
    )(x, g.reshape(1, D), dh, dxo)


def _loss_head(x, g, target, *, name):
    T, D = x.shape
    tm = _tile(T, 512, 16)

    def body(x_ref, g_ref, t_ref, dx_ref, dg_ref, loss_ref):
        x = x_ref[...]
        gain = g_ref[...]
        r = _rstd(x)
        xhat = x * r
        err = xhat * gain - t_ref[...]
        dy = err * (1.0 / D)
        dxh = dy * gain
        dx_ref[...] = r * (dxh - xhat * jnp.mean(dxh * xhat, axis=-1, keepdims=True))
        dg_part = jnp.sum(dy * xhat, axis=0, keepdims=True)
        loss_part = jnp.full((1, LANE), 0.5 / D, F32) * jnp.sum(err * err)

        @pl.when(pl.program_id(0) == 0)
        def _():
            dg_ref[...] = dg_part
            loss_ref[...] = loss_part

        @pl.when(pl.program_id(0) > 0)
        def _():
            dg_ref[...] += dg_part
            loss_ref[...] += loss_part

    row = pl.BlockSpec((tm, D), lambda i: (i, 0))
    vec = pl.BlockSpec((1, D), lambda i: (0, 0))
    return pl.pallas_call(
        body, name=name,
        out_shape=(jax.ShapeDtypeStruct((T, D), F32), jax.ShapeDtypeStruct((1, D), F32),
                   jax.ShapeDtypeStruct((1, LANE), F32)),
        grid=(T // tm,),
        in_specs=[row, vec, row],
        out_specs=(row, vec, pl.BlockSpec((1, LANE), lambda i: (0, 0))),
        compiler_params=_params("arbitrary"),
    )(x, g.reshape(1, D), target)


def _ffn_up(h, wg, wu, *, name):
    T, D = h.shape
    F = wg.shape[1]
    tm, tn = _tile(T, 512, 16), _tile(F, 1408)

    def body(h_ref, wg_ref, wu_ref, gate_ref, up_ref, a_ref):
        h = h_ref[...]
        gate = jnp.dot(h, wg_ref[...], preferred_element_type=F32)
        up = jnp.dot(h, wu_ref[...], preferred_element_type=F32)
        gate_ref[...] = gate.astype(BF16)
        up_ref[...] = up.astype(BF16)
        a_ref[...] = (gate * jax.nn.sigmoid(gate) * up).astype(BF16)

    w_spec = pl.BlockSpec((D, tn), lambda j, i: (0, j))
    o_spec = pl.BlockSpec((tm, tn), lambda j, i: (i, j))
    out = jax.ShapeDtypeStruct((T, F), BF16)
    return pl.pallas_call(
        body, name=name,
        out_shape=(out, out, out),
        grid=(F // tn, T // tm),
        in_specs=[pl.BlockSpec((tm, D), lambda j, i: (i, 0)), w_spec, w_spec],
        out_specs=(o_spec, o_spec, o_spec),
        compiler_params=_params("parallel", "parallel"),
    )(h, wg, wu)


def _dep_spec(dep):
    return [] if dep is None else [pl.BlockSpec(dep.shape, lambda *_: (0,) * dep.ndim)]


def _ffn_bwd_act(dxo, wd, gate, up, *, alpha, name, dep=None):
    T, D = dxo.shape
    F = wd.shape[0]
    tm, tn = _tile(T, 512, 16), _tile(F, 1408)

    def body(dxo_ref, wd_ref, gate_ref, up_ref, *rest):
        dgate_ref, dup_ref = rest[-2:]
        da = lax.dot_general(dxo_ref[...].astype(BF16), wd_ref[...], (((1,), (1,)), ((), ())),
                             preferred_element_type=F32) * alpha
        gate = gate_ref[...].astype(F32)
        up = up_ref[...].astype(F32)
        sig = jax.nn.sigmoid(gate)
        dgate_ref[...] = (da * up * (sig * (1.0 + gate * (1.0 - sig)))).astype(BF16)
        dup_ref[...] = (da * (gate * sig)).astype(BF16)

    t_spec = pl.BlockSpec((tm, tn), lambda j, i: (i, j))
    out = jax.ShapeDtypeStruct((T, F), BF16)
    return pl.pallas_call(
        body, name=name,
        out_shape=(out, out),
        grid=(F // tn, T // tm),
        in_specs=[pl.BlockSpec((tm, D), lambda j, i: (i, 0)), pl.BlockSpec((tn, D), lambda j, i: (j, 0)),
                  t_spec, t_spec] + _dep_spec(dep),
        out_specs=(t_spec, t_spec),
        compiler_params=_params("parallel", "parallel"),
    )(dxo, wd, gate, up, *([] if dep is None else [dep]))


def _rope_tables(positions):
    half = QK_ROPE // 2
    inv_freq = ROPE_THETA ** (-jnp.arange(0, QK_ROPE, 2, dtype=F32) / QK_ROPE)
    ang = positions.astype(F32)[:, None] * inv_freq
    cos, sin = jnp.cos(ang), jnp.sin(ang)
    z = jnp.zeros_like(cos)
    zz = jnp.zeros((positions.shape[0], LANE - QK_ROPE), F32)
    c = jnp.concatenate([cos, cos, zz], axis=1)
    sa = jnp.concatenate([z, sin, zz], axis=1)
    sb = jnp.concatenate([-sin, z, zz], axis=1)
    return c, sa, sb


def _rotate(seg, c, sa, sb, sign):
    half = QK_ROPE // 2
    mix = pltpu.roll(seg, half, 1) * sa + pltpu.roll(seg, LANE - half, 1) * sb
    return seg * c + mix if sign > 0 else seg * c - mix


def _mixer_in(h, wa, wuq, wukv, gq, gkv, tabs, *, name):
    T, D = h.shape
    PA = wa.shape[1]
    QL, HQ = wuq.shape
    KVL = wukv.shape[0]
    H = HQ // HEAD_W
    o_q, o_kv, o_kr = POOL_DIM, POOL_DIM + QL, POOL_DIM + QL + KVL
    assert PA == o_kr + LANE
    tm = _tile(T, 256, 16)

    def body(h_ref, wa_ref, wuq_ref, wukv_ref, gq_ref, gkv_ref, c_ref, sa_ref, sb_ref,
             xp_ref, ql_ref, kvl_ref, qn_ref, kvn_ref, q_ref, kv_ref, kr_ref):
        proj = jnp.dot(h_ref[...], wa_ref[...], preferred_element_type=F32)
        xp_ref[...] = proj[:, :POOL_DIM]
        ql = proj[:, o_q:o_kv]
        kvl = proj[:, o_kv:o_kr]
        ql_ref[...] = ql
        kvl_ref[...] = kvl
        qn = (ql * _rstd(ql) * gq_ref[...]).astype(BF16)
        kvn = (kvl * _rstd(kvl) * gkv_ref[...]).astype(BF16)
        qn_ref[...] = qn
        kvn_ref[...] = kvn
        c, sa, sb = c_ref[...], sa_ref[...], sb_ref[...]
        q = jnp.dot(qn, wuq_ref[...], preferred_element_type=F32)
        for hh in range(H):
            base = hh * HEAD_W
            q_ref[:, base:base + QK_NOPE] = q[:, base:base + QK_NOPE].astype(BF16)
            q_ref[:, base + QK_NOPE:base + HEAD_W] = _rotate(
                q[:, base + QK_NOPE:base + HEAD_W], c, sa, sb, 1).astype(BF16)
        kv_ref[...] = jnp.dot(kvn, wukv_ref[...], preferred_element_type=F32).astype(BF16)
        kr_ref[...] = _rotate(proj[:, o_kr:o_kr + LANE], c, sa, sb, 1).astype(BF16)

    def row(w):
        return pl.BlockSpec((tm, w), lambda i: (i, 0))

    def whole(arr):
        return pl.BlockSpec(arr.shape, lambda i: (0,) * arr.ndim)

    gq2, gkv2 = gq.reshape(1, QL), gkv.reshape(1, KVL)
    outs = [(POOL_DIM, F32), (QL, F32), (KVL, F32), (QL, BF16), (KVL, BF16), (HQ, BF16), (HQ, BF16), (LANE, BF16)]
    return pl.pallas_call(
        body, name=name,
        out_shape=tuple(jax.ShapeDtypeStruct((T, w), dt) for w, dt in outs),
        grid=(T // tm,),
        in_specs=[row(D), whole(wa), whole(wuq), whole(wukv), whole(gq2), whole(gkv2),
                  row(LANE), row(LANE), row(LANE)],
        out_specs=tuple(row(w) for w, _ in outs),
        compiler_params=_params("parallel"),
    )(h, wa, wuq, wukv, gq2, gkv2, *tabs)


def _mixer_in_bwd(dq, dkv, dkr, ql, kvl, dxp, wuq, wukv, gq, gkv, tabs, *, name):
    T, HQ = dq.shape
    QL, KVL = wuq.shape[0], wukv.shape[0]
    H = HQ // HEAD_W
    PA = POOL_DIM + QL + KVL + LANE
    o_q, o_kv, o_kr = POOL_DIM, POOL_DIM + QL, POOL_DIM + QL + KVL
    tm = _tile(T, 256, 16)

    def norm_bwd(lat, gain, dn):
        r = _rstd(lat)
        xhat = lat * r
        dxh = dn * gain
        dlat = r * (dxh - xhat * jnp.mean(dxh * xhat, axis=-1, keepdims=True))
        return dlat, jnp.sum(dn * xhat, axis=0, keepdims=True)

    def body(dq_ref, dkv_ref, dkr_ref, ql_ref, kvl_ref, dxp_ref, wuq_ref, wukv_ref, gq_ref, gkv_ref,
             c_ref, sa_ref, sb_ref, dproj_ref, dqp_ref, dgq_ref, dgkv_ref):
        c, sa, sb = c_ref[...], sa_ref[...], sb_ref[...]
        dkr_sum = dkr_ref[:, :LANE]
        for hh in range(H):
            base = hh * HEAD_W
            dqp_ref[:, base:base + QK_NOPE] = dq_ref[:, base:base + QK_NOPE]
            dqp_ref[:, base + QK_NOPE:base + HEAD_W] = _rotate(
                dq_ref[:, base + QK_NOPE:base + HEAD_W].astype(F32), c, sa, sb, -1).astype(BF16)
            if hh:
                dkr_sum = dkr_sum + dkr_ref[:, hh * LANE:(hh + 1) * LANE]
        contract_last = (((1,), (1,)), ((), ()))
        dqn = lax.dot_general(dqp_ref[...], wuq_ref[...], contract_last, preferred_element_type=F32)
        dkvn = lax.dot_general(dkv_ref[...], wukv_ref[...], contract_last, preferred_element_type=F32)
        dql, dgq = norm_bwd(ql_ref[...], gq_ref[...], dqn)
        dkvl, dgkv = norm_bwd(kvl_ref[...], gkv_ref[...], dkvn)
        dproj_ref[:, :POOL_DIM] = dxp_ref[...].astype(BF16)
        dproj_ref[:, o_q:o_kv] = dql.astype(BF16)
        dproj_ref[:, o_kv:o_kr] = dkvl.astype(BF16)
        dproj_ref[:, o_kr:PA] = _rotate(dkr_sum, c, sa, sb, -1).astype(BF16)

        @pl.when(pl.program_id(0) == 0)
        def _():
            dgq_ref[...] = dgq
            dgkv_ref[...] = dgkv

        @pl.when(pl.program_id(0) > 0)
        def _():
            dgq_ref[...] += dgq
            dgkv_ref[...] += dgkv

    def row(w):
        return pl.BlockSpec((tm, w), lambda i: (i, 0))

    def whole(arr):
        return pl.BlockSpec(arr.shape, lambda i: (0,) * arr.ndim)

    gq2, gkv2 = gq.reshape(1, QL), gkv.reshape(1, KVL)
    return pl.pallas_call(
        body, name=name,
        out_shape=(jax.ShapeDtypeStruct((T, PA), BF16), jax.ShapeDtypeStruct((T, HQ), BF16),
                   jax.ShapeDtypeStruct((1, QL), F32), jax.ShapeDtypeStruct((1, KVL), F32)),
        grid=(T // tm,),
        in_specs=[row(HQ), row(HQ), row(H * LANE), row(QL), row(KVL), row(POOL_DIM), whole(wuq), whole(wukv),
                  whole(gq2), whole(gkv2), row(LANE), row(LANE), row(LANE)],
        out_specs=(row(PA), row(HQ), whole(gq2), whole(gkv2)),
        compiler_params=_params("arbitrary"),
    )(dq, dkv, dkr, ql, kvl, dxp, wuq, wukv, gq2, gkv2, *tabs)


def _pool_groups(x_of, S):
    row = lax.broadcasted_iota(jnp.int32, (S, POOL_G), 0)
    for g, w in enumerate(POOL_WINDOWS):
        x = x_of(g)
        s = x
        d = 1
        while d < w:
            s = s + jnp.where(row >= d, pltpu.roll(s, d, 0), 0.0)
            d *= 2
        cnt = jnp.minimum(row + 1, w).astype(F32)
        yield g, w, x, s / cnt - x, cnt, row


def _pool_fwd(xp, maps, scale, *, S, name):
    T = xp.shape[0]

    def body(xp_ref, maps_ref, scale_ref, ms_ref):
        for g, _, _, pooled, _, _ in _pool_groups(lambda g: xp_ref[:, g * POOL_G:(g + 1) * POOL_G], S):
            mixed = jnp.dot(pooled.astype(BF16), maps_ref[g].astype(BF16), preferred_element_type=F32)
            ms_ref[:, g * POOL_G:(g + 1) * POOL_G] = (mixed * scale_ref[:, g * POOL_G:(g + 1) * POOL_G]).astype(BF16)

    return pl.pallas_call(
        body, name=name,
        out_shape=jax.ShapeDtypeStruct((T, POOL_DIM), BF16),
        grid=(T // S,),
        in_specs=[pl.BlockSpec((S, POOL_DIM), lambda b: (b, 0)),
                  pl.BlockSpec(maps.shape, lambda b: (0, 0, 0)),
                  pl.BlockSpec((1, POOL_DIM), lambda b: (0, 0))],
        out_specs=pl.BlockSpec((S, POOL_DIM), lambda b: (b, 0)),
        compiler_params=_params("parallel"),
    )(xp, maps, scale.reshape(1, POOL_DIM))


def _pool_bwd(xp, dms, maps, scale, *, S, name):
    T = xp.shape[0]

    def body(xp_ref, dms_ref, maps_ref, scale_ref, dxp_ref, dmaps_ref, dscale_ref):
        first = pl.program_id(0) == 0
        for g, w, _, pooled, cnt, row in _pool_groups(lambda g: xp_ref[:, g * POOL_G:(g + 1) * POOL_G], S):
            cols = slice(g * POOL_G, (g + 1) * POOL_G)
            pooled_b = pooled.astype(BF16)
            maps_b = maps_ref[g].astype(BF16)
            mixed = jnp.dot(pooled_b, maps_b, preferred_element_type=F32)
            dms = dms_ref[:, cols]
            dscale = jnp.sum(dms * mixed, axis=0, keepdims=True)
            dmixed = (dms * scale_ref[:, cols]).astype(BF16)
            dmaps = lax.dot_general(pooled_b, dmixed, (((0,), (0,)), ((), ())), preferred_element_type=F32)
            dpooled = lax.dot_general(dmixed, maps_b, (((1,), (1,)), ((), ())), preferred_element_type=F32)
            z = dpooled / cnt
            d = 1
            while d < w:
                z = z + jnp.where(row < S - d, pltpu.roll(z, S - d, 0), 0.0)
                d *= 2
            dxp_ref[:, cols] = z - dpooled

            @pl.when(first)
            def _():
                dmaps_ref[g] = dmaps
                dscale_ref[:, cols] = dscale

            @pl.when(jnp.logical_not(first))
            def _():
                dmaps_ref[g] += dmaps
                dscale_ref[:, cols] += dscale

    seq = pl.BlockSpec((S, POOL_DIM), lambda b: (b, 0))
    maps_spec = pl.BlockSpec(maps.shape, lambda b: (0, 0, 0))
    vec = pl.BlockSpec((1, POOL_DIM), lambda b: (0, 0))
    return pl.pallas_call(
        body, name=name,
        out_shape=(jax.ShapeDtypeStruct((T, POOL_DIM), F32), jax.ShapeDtypeStruct(maps.shape, F32),
                   jax.ShapeDtypeStruct((1, POOL_DIM), F32)),
        grid=(T // S,),
        in_specs=[seq, seq, maps_spec, vec],
        out_specs=(seq, maps_spec, vec),
        compiler_params=_params("arbitrary"),
    )(xp, dms, maps, scale.reshape(1, POOL_DIM))


def _causal_mask(s, t):
    r = lax.broadcasted_iota(jnp.int32, (t, t), 0)
    c = lax.broadcasted_iota(jnp.int32, (t, t), 1)
    return jnp.where(r >= c, s, MASK_VALUE)


_NT = (((1,), (1,)), ((), ()))
_TN = (((0,), (0,)), ((), ()))


def _attn_fwd(q, kv, kr, *, S, name):
    T, HQ = q.shape
    H = HQ // HEAD_W
    B = T // S
    t = _tile(S, ATTN_TILE)
    n = S // t

    def body(q_ref, k_ref, v_ref, kr_ref, o_ref, lse_ref, kcat):
        kcat[:, :QK_NOPE] = k_ref[...]
        kcat[:, QK_NOPE:] = kr_ref[...]
        for i in range(n):
            rows = slice(i * t, (i + 1) * t)
            qt = q_ref[rows, :]
            m = jnp.full((t, 1), MASK_VALUE, F32)
            l = jnp.zeros((t, 1), F32)
            acc = jnp.zeros((t, V_DIM), F32)
            for j in range(i + 1):
                cols = slice(j * t, (j + 1) * t)
                s = lax.dot_general(qt, kcat[cols, :], _NT, preferred_element_type=F32) * ATTN_SCALE
                if j == i:
                    s = _causal_mask(s, t)
                m_new = jnp.maximum(m, jnp.max(s, axis=1, keepdims=True))
                p = jnp.exp(s - m_new)
                corr = jnp.exp(m - m_new)
                l = corr * l + jnp.sum(p, axis=1, keepdims=True)
                acc = corr * acc + jnp.dot(p.astype(BF16), v_ref[cols, :], preferred_element_type=F32)
                m = m_new
            o_ref[rows, :] = (acc / l).astype(BF16)
            lse_ref[rows, :] = jnp.broadcast_to(m + jnp.log(l), (t, LANE))

    seq_h = pl.BlockSpec((S, LANE), lambda b, h: (b, h))
    return pl.pallas_call(
        body, name=name,
        out_shape=(jax.ShapeDtypeStruct((T, H * V_DIM), BF16), jax.ShapeDtypeStruct((T, H * LANE), F32)),
        grid=(B, H),
        in_specs=[pl.BlockSpec((S, HEAD_W), lambda b, h: (b, h)),
                  pl.BlockSpec((S, QK_NOPE), lambda b, h: (b, 2 * h)),
                  pl.BlockSpec((S, V_DIM), lambda b, h: (b, 2 * h + 1)),
                  pl.BlockSpec((S, LANE), lambda b, h: (b, 0))],
        out_specs=(seq_h, seq_h),
        scratch_shapes=[pltpu.VMEM((S, HEAD_W), BF16)],
        compiler_params=_params("parallel", "parallel"),
    )(q, kv, kv, kr)


def _attn_bwd(q, kv, kr, o, do, lse, *, S, name):
    T, HQ = q.shape
    H = HQ // HEAD_W
    B = T // S
    t = _tile(S, ATTN_TILE)
    n = S // t

    def body(q_ref, k_ref, v_ref, kr_ref, o_ref, do_ref, lse_ref, dq_ref, dkv_ref, dkr_ref, kcat, dq_acc):
        kcat[:, :QK_NOPE] = k_ref[...]
        kcat[:, QK_NOPE:] = kr_ref[...]
        delta = [jnp.sum(do_ref[i * t:(i + 1) * t, :].astype(F32) * o_ref[i * t:(i + 1) * t, :].astype(F32),
                         axis=1, keepdims=True) for i in range(n)]
        for j in range(n):
            cols = slice(j * t, (j + 1) * t)
            kc = kcat[cols, :]
            vt = v_ref[cols, :]
            dk = jnp.zeros((t, HEAD_W), F32)
            dv = jnp.zeros((t, V_DIM), F32)
            for i in range(j, n):
                rows = slice(i * t, (i + 1) * t)
                qt = q_ref[rows, :]
                dot_ = do_ref[rows, :]
                s = lax.dot_general(qt, kc, _NT, preferred_element_type=F32) * ATTN_SCALE
                if i == j:
                    s = _causal_mask(s, t)
                p = jnp.exp(s - lse_ref[rows, :][:, :1])
                dv = dv + lax.dot_general(p.astype(BF16), dot_, _TN, preferred_element_type=F32)
                dp = lax.dot_general(dot_, vt, _NT, preferred_element_type=F32)
                ds = (p * (dp - delta[i]) * ATTN_SCALE).astype(BF16)
                dk = dk + lax.dot_general(ds, qt, _TN, preferred_element_type=F32)
                dq_part = jnp.dot(ds, kc, preferred_element_type=F32)
                if j == 0:
                    dq_acc[rows, :] = dq_part
                else:
                    dq_acc[rows, :] += dq_part
            dkv_ref[cols, :QK_NOPE] = dk[:, :QK_NOPE].astype(BF16)
            dkv_ref[cols, QK_NOPE:] = dv.astype(BF16)
            dkr_ref[cols, :] = dk[:, QK_NOPE:]
        dq_ref[...] = dq_acc[...].astype(BF16)

    seq_q = pl.BlockSpec((S, HEAD_W), lambda b, h: (b, h))
    seq_h = pl.BlockSpec((S, LANE), lambda b, h: (b, h))
    return pl.pallas_call(
        body, name=name,
        out_shape=(jax.ShapeDtypeStruct((T, HQ), BF16), jax.ShapeDtypeStruct((T, HQ), BF16),
                   jax.ShapeDtypeStruct((T, H * LANE), F32)),
        grid=(B, H),
        in_specs=[seq_q,
                  pl.BlockSpec((S, QK_NOPE), lambda b, h: (b, 2 * h)),
                  pl.BlockSpec((S, V_DIM), lambda b, h: (b, 2 * h + 1)),
                  pl.BlockSpec((S, LANE), lambda b, h: (b, 0)),
                  seq_h, seq_h, seq_h],
        out_specs=(seq_q, seq_q, seq_h),
        scratch_shapes=[pltpu.VMEM((S, HEAD_W), BF16), pltpu.VMEM((S, HEAD_W), F32)],
        compiler_params=_params("parallel", "parallel"),
    )(q, kv, kv, kr, o, do, lse)


def _merge_out(h, ms, o, x, wgate, bgate, wpp, wap, wout, *, name):
    T, D = x.shape
    tm = _tile(T, 256, 16)

    def body(h_ref, ms_ref, o_ref, x_ref, wgate_ref, bgate_ref, wpp_ref, wap_ref, wout_ref,
             gates_ref, ba_ref, bb_ref, merged_ref, xn_ref):
        logits = jnp.dot(h_ref[...], wgate_ref[...], preferred_element_type=F32) + bgate_ref[...]
        gates = jax.nn.sigmoid(logits)
        ba = jnp.dot(ms_ref[...], wpp_ref[...], preferred_element_type=F32)
        bb = jnp.dot(o_ref[...], wap_ref[...], preferred_element_type=F32)
        merged = (gates[:, :D] * ba + gates[:, D:] * bb).astype(BF16)
        gates_ref[...] = gates.astype(BF16)
        ba_ref[...] = ba.astype(BF16)
        bb_ref[...] = bb.astype(BF16)
        merged_ref[...] = merged
        xn_ref[...] = x_ref[...] + jnp.dot(merged, wout_ref[...], preferred_element_type=F32)

    def row(w):
        return pl.BlockSpec((tm, w), lambda i: (i, 0))

    def whole(arr):
        return pl.BlockSpec(arr.shape, lambda i: (0,) * arr.ndim)

    bg2 = bgate.reshape(1, 2 * D)
    act = jax.ShapeDtypeStruct((T, D), BF16)
    return pl.pallas_call(
        body, name=name,
        out_shape=(jax.ShapeDtypeStruct((T, 2 * D), BF16), act, act, act, jax.ShapeDtypeStruct((T, D), F32)),
        grid=(T // tm,),
        in_specs=[row(D), row(ms.shape[1]), row(o.shape[1]), row(D), whole(wgate), whole(bg2), whole(wpp),
                  whole(wap), whole(wout)],
        out_specs=(row(2 * D), row(D), row(D), row(D), row(D)),
        compiler_params=_params("parallel"),
    )(h, ms, o, x, wgate, bg2, wpp, wap, wout)


def _merge_bwd(dxo, wout, gates, ba, bb, *, name, dep=None):
    T, D = dxo.shape
    tm = _tile(T, 256, 16)

    def body(dxo_ref, wout_ref, gates_ref, ba_ref, bb_ref, *rest):
        dba_ref, dbb_ref, dgl_ref, dbg_ref = rest[-4:]
        dm = lax.dot_general(dxo_ref[...].astype(BF16), wout_ref[...], _NT, preferred_element_type=F32)
        ga = gates_ref[:, :D].astype(F32)
        gb = gates_ref[:, D:].astype(F32)
        dba_ref[...] = (dm * ga).astype(BF16)
        dbb_ref[...] = (dm * gb).astype(BF16)
        dgl_a = dm * ba_ref[...].astype(F32) * (ga * (1.0 - ga))
        dgl_b = dm * bb_ref[...].astype(F32) * (gb * (1.0 - gb))
        dgl_ref[:, :D] = dgl_a.astype(BF16)
        dgl_ref[:, D:] = dgl_b.astype(BF16)
        sa = jnp.sum(dgl_a, axis=0, keepdims=True)
        sb = jnp.sum(dgl_b, axis=0, keepdims=True)

        @pl.when(pl.program_id(0) == 0)
        def _():
            dbg_ref[:, :D] = sa
            dbg_ref[:, D:] = sb

        @pl.when(pl.program_id(0) > 0)
        def _():
            dbg_ref[:, :D] += sa
            dbg_ref[:, D:] += sb

    def row(w):
        return pl.BlockSpec((tm, w), lambda i: (i, 0))

    act = jax.ShapeDtypeStruct((T, D), BF16)
    return pl.pallas_call(
        body, name=name,
        out_shape=(act, act, jax.ShapeDtypeStruct((T, 2 * D), BF16), jax.ShapeDtypeStruct((1, 2 * D), F32)),
        grid=(T // tm,),
        in_specs=[row(D), pl.BlockSpec(wout.shape, lambda i: (0, 0)), row(2 * D), row(D), row(D)] + _dep_spec(dep),
        out_specs=(row(D), row(D), row(2 * D), pl.BlockSpec((1, 2 * D), lambda i: (0, 0))),
        compiler_params=_params("arbitrary"),
    )(dxo, wout, gates, ba, bb, *([] if dep is None else [dep]))


def _ffn_fwd(x, gain, w, tag):
    h = _rms_fwd(x, gain, name=f"{tag}_norm")
    gate, up, a = _ffn_up(h, w["wg"], w["wu"], name=f"{tag}_up")
    xn = _mm(a, w["wd"], res=x, alpha=0.5, name=f"{tag}_down", tk=2816)
    return xn, (x, h, gate, up, a)


def _ffn_bwd(dxo, gain, w, saved, tag, dep=None):
    x, h, gate, up, a = saved
    dgate, dup = _ffn_bwd_act(dxo, w["wd"], gate, up, alpha=0.5, name=f"{tag}_bwd_act", dep=dep)
    dwd = _mm(a, dxo, ta=True, alpha=0.5, out_dtype=BF16, name=f"{tag}_dwd", tm=1408, tn=1024, tk=1024)
    dwg = _mm(h, dgate, ta=True, out_dtype=BF16, name=f"{tag}_dwg", tm=1024, tn=1408, tk=2048)
    dwu = _mm(h, dup, ta=True, out_dtype=BF16, name=f"{tag}_dwu", tm=1024, tn=1408, tk=2048)
    dh = _mm(dgate, w["wg"], tb=True, name=f"{tag}_dh_gate", tk=2816)
    dh = _mm(dup, w["wu"], tb=True, res=dh, name=f"{tag}_dh_up", tk=2816)
    dx, dgain = _rms_bwd(x, gain, dh, dxo, name=f"{tag}_norm_bwd")
    return dx, dgain, dwg, dwu, dwd


def _mixer_fwd(x, p, w, tabs, S):
    h = _rms_fwd(x, p["norm_mix"], name="mix_norm")
    xp, ql, kvl, qn, kvn, q, kv, kr = _mixer_in(h, w["wa"], w["wuq"], w["wukv"], p["q_latent_norm"],
                                                 p["kv_latent_norm"], tabs, name="mix_in")
    ms = _pool_fwd(xp, p["pool_maps"], p["pool_scale"], S=S, name="pool_fwd")
    o, lse = _attn_fwd(q, kv, kr, S=S, name="attn_fwd")
    gates, ba, bb, merged, xn = _merge_out(h, ms, o, x, w["wgate"], p["b_gate"], w["wpp"], w["wap"], w["wout"],
                                           name="merge_out")
    return xn, (x, h, xp, ql, kvl, qn, kvn, q, kv, kr, ms, o, lse, gates, ba, bb, merged)


def _mixer_bwd(dxo, p, w, tabs, saved, S, dep=None):
    x, h, xp, ql, kvl, qn, kvn, q, kv, kr, ms, o, lse, gates, ba, bb, merged = saved
    dba, dbb, dgl, dbg = _merge_bwd(dxo, w["wout"], gates, ba, bb, name="merge_bwd", dep=dep)
    g = {}
    g["wout"] = _mm(merged, dxo, ta=True, out_dtype=BF16, name="d_wout", tm=1024, tk=1024)
    g["wpp"] = _mm(ms, dba, ta=True, out_dtype=BF16, name="d_wpp", tk=2048)
    g["wap"] = _mm(o, dbb, ta=True, out_dtype=BF16, name="d_wap", tm=1024, tk=2048)
    dms = _mm(dba, w["wpp"], tb=True, name="d_ms")
    do = _mm(dbb, w["wap"], tb=True, out_dtype=BF16, name="d_o")
    dxp, g["pool_maps"], g["pool_scale"] = _pool_bwd(xp, dms, p["pool_maps"], p["pool_scale"], S=S, name="pool_bwd")
    dq, dkv, dkr = _attn_bwd(q, kv, kr, o, do, lse, S=S, name="attn_bwd")
    dproj, dqp, g["q_latent_norm"], g["kv_latent_norm"] = _mixer_in_bwd(
        dq, dkv, dkr, ql, kvl, dxp, w["wuq"], w["wukv"], p["q_latent_norm"], p["kv_latent_norm"], tabs,
        name="mix_in_bwd")
    g["wuq"] = _mm(qn, dqp, ta=True, out_dtype=BF16, name="d_wuq", tn=2048, tk=2048)
    g["wukv"] = _mm(kvn, dkv, ta=True, out_dtype=BF16, name="d_wukv", tn=2048, tk=2048)
    g["wa"] = _mm(h, dproj, ta=True, out_dtype=BF16, name="d_wa", tm=1024, tn=1280, tk=2048)
    g["wgate"] = _mm(h, dgl, ta=True, out_dtype=BF16, name="d_wgate", tm=1024, tn=2048, tk=1024)
    dh = _mm(dproj, w["wa"], tb=True, name="dh_mix_a", tk=1280)
    dh = _mm(dgl, w["wgate"], tb=True, res=dh, name="dh_mix_gate", tk=2048)
    dx, g["norm_mix"] = _rms_bwd(x, p["norm_mix"], dh, dxo, name="mix_norm_bwd")
    g["b_gate"] = dbg
    return dx, g


BIG = ("ffn1_up", "ffn1_down", "w_in", "w_pool_proj", "w_uq", "w_ukv", "w_attn_proj", "w_out", "ffn2_up", "ffn2_down")
SMALL = ("norm_ffn1", "norm_mix", "b_gate", "pool_maps", "pool_scale", "q_latent_norm", "kv_latent_norm", "norm_ffn2")
PACKED = ("w_pool_proj", "w_uq", "w_ukv")
DIRECT = tuple(n for n in BIG if n not in PACKED)


def _layout_full(nat):
    w_in, w_uq = nat["w_in"], nat["w_uq"]
    D = w_in.shape[0]
    QL = w_uq.shape[0]
    H = w_uq.shape[1] // (QK_NOPE + QK_ROPE)
    n_a = w_in.shape[1] - 2 * D
    wa = jnp.pad(w_in[:, :n_a], ((0, 0), (0, LANE - QK_ROPE)))
    wuq = jnp.pad(w_uq.reshape(QL, H, QK_NOPE + QK_ROPE), ((0, 0), (0, 0), (0, HEAD_W - QK_NOPE - QK_ROPE)))
    full = {"wa": wa, "wgate": w_in[:, n_a:], "wuq": wuq.reshape(QL, H * HEAD_W), "wukv": nat["w_ukv"],
            "wpp": nat["w_pool_proj"], "wap": nat["w_attn_proj"], "wout": nat["w_out"]}
    for tag in ("ffn1", "ffn2"):
        up = nat[tag + "_up"]
        F = up.shape[1] // 2
        full[tag] = {"wg": up[:, :F], "wu": up[:, F:], "wd": nat[tag + "_down"]}
    return full


def _mixer_grads_natural(g):
    QL = g["wuq"].shape[0]
    H = g["wuq"].shape[1] // HEAD_W
    n_a = g["wa"].shape[1] - (LANE - QK_ROPE)
    return {
        "norm_mix": g["norm_mix"],
        "w_in": jnp.concatenate([g["wa"][:, :n_a], g["wgate"]], axis=1),
        "b_gate": g["b_gate"],
        "pool_maps": g["pool_maps"],
        "pool_scale": g["pool_scale"],
        "w_pool_proj": g["wpp"],
        "q_latent_norm": g["q_latent_norm"],
        "w_uq": g["wuq"].reshape(QL, H, HEAD_W)[:, :, :QK_NOPE + QK_ROPE].reshape(QL, H * (QK_NOPE + QK_ROPE)),
        "kv_latent_norm": g["kv_latent_norm"],
        "w_ukv": g["wukv"],
        "w_attn_proj": g["wap"],
        "w_out": g["wout"],
    }


def _mesh_place():
    x, y, c = lax.axis_index("x"), lax.axis_index("y"), lax.axis_index("c")
    chips = [(1 - x, y), (x, 1 - y), (1 - x, 1 - y)]
    return x, y, c, chips


def _all_gather(block, *, name):
    R, C = block.shape

    def body(x_ref, out_ref, send_sems, recv_sems, local_sem):
        x, y, c, chips = _mesh_place()
        me, sibling = (x, y, c), (x, y, 1 - c)

        def slot(px, py, pc):
            return out_ref.at[4 * px + 2 * py + pc]

        def copy(k, blk, to, src=None):
            return pltpu.make_async_remote_copy(
                src_ref=slot(*blk) if src is None else src, dst_ref=slot(*blk),
                send_sem=send_sems.at[k], recv_sem=recv_sems.at[k], device_id=to, device_id_type=MESH)

        mine = pltpu.make_async_copy(x_ref, slot(*me), local_sem)
        mine.start()
        first = [copy(0, me, sibling, src=x_ref)]
        first += [copy(1 + j, me, (*chip, c), src=x_ref) for j, chip in enumerate(chips)]
        for cp in first:
            cp.start()
        passed = [copy(4 + j, (*chip, c), sibling) for j, chip in enumerate(chips)]
        for j, chip in enumerate(chips):
            copy(1 + j, (*chip, c), me).wait_recv()
            passed[j].start()
        copy(0, sibling, me).wait_recv()
        for j, chip in enumerate(chips):
            copy(4 + j, (*chip, 1 - c), me).wait_recv()
        for cp in first + passed:
            cp.wait_send()
        mine.wait()

    return pl.pallas_call(
        body, name=name,
        out_shape=jax.ShapeDtypeStruct((N_DEV, R, C), block.dtype),
        in_specs=[ANY], out_specs=ANY,
        scratch_shapes=[pltpu.SemaphoreType.DMA((7,)), pltpu.SemaphoreType.DMA((7,)), pltpu.SemaphoreType.DMA],
    )(block)


HBM = pl.BlockSpec(memory_space=pltpu.HBM)
SEMAPHORES = pl.BlockSpec(memory_space=pltpu.SEMAPHORE)
DATAFLOW = pltpu.SideEffectType.DATAFLOW_SIDE_EFFECTING
GATHER_PEERS = 4
SCATTER_PEERS = 7


def _in_hbm(a):
    return pltpu.with_memory_space_constraint(a, pltpu.HBM)


def _gather_plan(src_refs, land_refs):
    x, y, c, chips = _mesh_place()
    me = 4 * x + 2 * y + c
    targets = [(x, y, 1 - c)] + [(cx, cy, c) for cx, cy in chips]
    return [(s, land.at[me], to) for s, land in zip(src_refs, land_refs) for to in targets]


def _scatter_plan(src_refs, land_refs):
    x, y, c, _ = _mesh_place()
    peers = [(x, y, 1 - c), (1 - x, y, c), (x, 1 - y, c), (1 - x, 1 - y, c),
             (1 - x, y, 1 - c), (x, 1 - y, 1 - c), (1 - x, 1 - y, 1 - c)]
    return [(s.at[4 * px + 2 * py + pc], land.at[k], (px, py, pc))
            for s, land in zip(src_refs, land_refs) for k, (px, py, pc) in enumerate(peers)]


def _descriptors(plan, src_refs, land_refs, send_sems, recv_sems):
    return [pltpu.make_async_remote_copy(src_ref=s, dst_ref=d, send_sem=send_sems.at[k], recv_sem=recv_sems.at[k],
                                         device_id=to, device_id_type=MESH)
            for k, (s, d, to) in enumerate(plan(src_refs, land_refs))]


def _exchange(srcs, land_shapes, plan, per_src, *, name):
    n = len(srcs)

    def body(*refs):
        copies = _descriptors(plan, refs[:n], refs[n:2 * n], refs[2 * n], refs[2 * n + 1])
        for cp in copies:
            cp.start()
        for cp in copies:
            cp.wait()

    return pl.pallas_call(
        body, name=name,
        out_shape=tuple(jax.ShapeDtypeStruct(shape, s.dtype) for shape, s in zip(land_shapes, srcs)),
        in_specs=[ANY] * n, out_specs=(ANY,) * n,
        scratch_shapes=[pltpu.SemaphoreType.DMA((per_src * n,)), pltpu.SemaphoreType.DMA((per_src * n,))],
    )(*srcs)


def _exchange_start(srcs, land_shapes, plan, per_src, *, name):
    n = len(srcs)
    lands = [lax.empty(shape, s.dtype) for shape, s in zip(land_shapes, srcs)]

    def body(*refs):
        for cp in _descriptors(plan, refs[:n], refs[n:2 * n], refs[2 * n], refs[2 * n + 1]):
            cp.start()
        refs[-1][...] = jnp.zeros_like(refs[-1])

    sems = pltpu.SemaphoreType.DMA((per_src * n,))
    out = pl.pallas_call(
        body, name=name,
        out_shape=(sems, sems, *[pltpu.HBM(a.shape, a.dtype) for a in srcs + lands],
                   jax.ShapeDtypeStruct((8, LANE), F32)),
        in_specs=(HBM,) * (2 * n),
        out_specs=(SEMAPHORES, SEMAPHORES, *[HBM] * (2 * n), pl.BlockSpec(memory_space=pltpu.VMEM)),
        input_output_aliases={i: 2 + i for i in range(2 * n)},
        compiler_params=pltpu.CompilerParams(has_side_effects=DATAFLOW),
    )(*[_in_hbm(a) for a in srcs + lands])
    return out[0], out[1], list(out[2:2 + n]), list(out[2 + n:2 + 2 * n]), out[-1]


def _exchange_wait(send_sems, recv_sems, srcs, lands, plan, after, *, name):
    n = len(srcs)

    def body(*refs):
        for cp in _descriptors(plan, refs[:n], refs[n:2 * n], refs[2 * n], refs[2 * n + 1]):
            cp.wait_send()
            cp.wait_recv()

    out = pl.pallas_call(
        body, name=name,
        out_shape=tuple(pltpu.HBM(a.shape, a.dtype) for a in srcs + lands),
        in_specs=(*[HBM] * (2 * n), SEMAPHORES, SEMAPHORES, ANY),
        out_specs=(HBM,) * (2 * n),
        input_output_aliases={i: i for i in range(2 * n)},
        compiler_params=pltpu.CompilerParams(has_side_effects=DATAFLOW),
    )(*srcs, *lands, send_sems, recv_sems, after)
    return list(out[n:])


def _gather_forward(srcs, lands, *, name):
    n = len(srcs)

    def body(*refs):
        src_refs, in_refs, out_refs = refs[:n], refs[n:2 * n], refs[2 * n:3 * n]
        token, send_sems, recv_sems, local_sems = refs[3 * n:3 * n + 4]
        x, y, c, chips = _mesh_place()
        me = 4 * x + 2 * y + c
        own = [pltpu.make_async_copy(s, o.at[me], local_sems.at[b]) for b, (s, o) in enumerate(zip(src_refs, out_refs))]
        passed = [pltpu.make_async_remote_copy(
            src_ref=i.at[4 * cx + 2 * cy + c], dst_ref=o.at[4 * cx + 2 * cy + c],
            send_sem=send_sems.at[3 * b + j], recv_sem=recv_sems.at[3 * b + j],
            device_id=(x, y, 1 - c), device_id_type=MESH)
            for b, (i, o) in enumerate(zip(in_refs, out_refs)) for j, (cx, cy) in enumerate(chips)]
        for cp in own + passed:
            cp.start()
        for cp in own + passed:
            cp.wait()
        token[...] = jnp.zeros_like(token)

    out = pl.pallas_call(
        body, name=name,
        out_shape=(*[jax.ShapeDtypeStruct(a.shape, a.dtype) for a in lands], jax.ShapeDtypeStruct((8, LANE), F32)),
        in_specs=[ANY] * (2 * n),
        out_specs=(*[ANY] * n, pl.BlockSpec(memory_space=pltpu.VMEM)),
        input_output_aliases={n + i: i for i in range(n)},
        scratch_shapes=[pltpu.SemaphoreType.DMA((3 * n,)), pltpu.SemaphoreType.DMA((3 * n,)),
                        pltpu.SemaphoreType.DMA((n,))],
    )(*srcs, *lands)
    return list(out[:n]), out[n]


def _scatter_sum(parts, got, me, *, name):
    shard = parts.shape[1:]
    cols = shard[-1]
    rows = int(np.prod(shard[:-1]))
    tr = _tile(rows, 256, 16)

    def body(me_ref, p_ref, g_ref, o_ref):
        acc = p_ref[...].astype(F32)
        for k in range(SCATTER_PEERS):
            acc = acc + g_ref[k].astype(F32)
        o_ref[...] = acc

    out = pl.pallas_call(
        body, name=name,
        out_shape=jax.ShapeDtypeStruct((rows, cols), F32),
        grid_spec=pltpu.PrefetchScalarGridSpec(
            num_scalar_prefetch=1, grid=(rows // tr,),
            in_specs=[pl.BlockSpec((None, tr, cols), lambda r, me_ref: (me_ref[0], r, 0)),
                      pl.BlockSpec((SCATTER_PEERS, tr, cols), lambda r, me_ref: (0, r, 0))],
            out_specs=pl.BlockSpec((tr, cols), lambda r, me_ref: (r, 0))),
        compiler_params=_params("parallel"),
    )(me, parts.reshape(N_DEV, rows, cols), got.reshape(SCATTER_PEERS, rows, cols))
    return out.reshape(shard)


def _sum_devices(parts, *, name):
    _, R, C = parts.shape
    tr = _tile(R, 512, 8)

    def body(p_ref, o_ref):
        acc = p_ref[0]
        for d in range(1, N_DEV):
            acc = acc + p_ref[d]
        o_ref[...] = acc

    return pl.pallas_call(
        body, name=name,
        out_shape=jax.ShapeDtypeStruct((R, C), F32),
        grid=(R // tr,),
        in_specs=[pl.BlockSpec((N_DEV, tr, C), lambda r: (0, r, 0))],
        out_specs=pl.BlockSpec((tr, C), lambda r: (r, 0)),
        compiler_params=_params("parallel"),
    )(parts)


def _adamw(w, g, m, v, *, name):
    shape = w.shape
    cols = shape[-1]
    rows = w.size // cols
    tr = _tile(rows, 256, 8)

    def body(w_ref, g_ref, m_ref, v_ref, d_ref, nm_ref, nv_ref):
        g = g_ref[...]
        m = ADAM_B1 * m_ref[...] + (1.0 - ADAM_B1) * g
        v = ADAM_B2 * v_ref[...] + (1.0 - ADAM_B2) * jnp.square(g)
        m_hat = m / (1.0 - ADAM_B1 ** ADAM_STEP)
        v_hat = v / (1.0 - ADAM_B2 ** ADAM_STEP)
        d_ref[...] = -ADAM_LR * (m_hat / (jnp.sqrt(v_hat) + ADAM_EPS) + ADAM_WD * w_ref[...])
        nm_ref[...] = m
        nv_ref[...] = v

    spec = pl.BlockSpec((tr, cols), lambda i: (i, 0))
    out = jax.ShapeDtypeStruct((rows, cols), F32)
    d, nm, nv = pl.pallas_call(
        body, name=name,
        out_shape=(out, out, out),
        grid=(rows // tr,),
        in_specs=[spec] * 4, out_specs=(spec,) * 3,
        compiler_params=_params("parallel"),
    )(*(a.reshape(rows, cols) for a in (w, g, m, v)))
    return d.reshape(shape), nm.reshape(shape), nv.reshape(shape)


PACK_ALIGN = 16 * LANE
COL_SHARDED = ("ffn1_up", "w_in", "w_pool_proj", "w_uq", "w_ukv", "ffn2_up")


def _pack(pieces, lead):
    out = []
    for p in pieces:
        keep = p.shape[:lead]
        flat = p.reshape(*keep, -1)
        pad = (-flat.shape[-1]) % PACK_ALIGN
        if pad:
            flat = jnp.pad(flat, [(0, 0)] * lead + [(0, pad)])
        out.append(flat.reshape(*keep, -1, LANE))
    return jnp.concatenate(out, axis=lead)


def _unpack(buf, shapes, lead):
    keep = buf.shape[:lead]
    out, row = [], 0
    for shape in shapes:
        size = int(np.prod(shape))
        rows = -(-size // PACK_ALIGN) * (PACK_ALIGN // LANE)
        piece = lax.slice_in_dim(buf, row, row + rows, axis=lead).reshape(*keep, rows * LANE)
        out.append(lax.slice_in_dim(piece, 0, size, axis=lead).reshape(*keep, *shape))
        row += rows
    return out


def _stacked_to_full(name, stacked):
    if name in COL_SHARDED:
        n, k, cols = stacked.shape
        return stacked.transpose(1, 0, 2).reshape(k, n * cols)
    n, rows, cols = stacked.shape
    return stacked.reshape(n * rows, cols)


def _full_to_stacked(name, full):
    if name in COL_SHARDED:
        k, cols = full.shape
        return full.reshape(k, N_DEV, cols // N_DEV).transpose(1, 0, 2)
    rows, cols = full.shape
    return full.reshape(N_DEV, rows // N_DEV, cols)


def kernel(x, positions, norm_ffn1, ffn1_up, ffn1_down, norm_mix, w_in, b_gate, pool_maps, pool_scale, w_pool_proj, q_latent_norm, w_uq, kv_latent_norm, w_ukv, w_attn_proj, w_out, norm_ffn2, ffn2_up, ffn2_down, final_norm, loss_target, m_norm_ffn1, m_ffn1_up, m_ffn1_down, m_norm_mix, m_w_in, m_b_gate, m_pool_maps, m_pool_scale, m_w_pool_proj, m_q_latent_norm, m_w_uq, m_kv_latent_norm, m_w_ukv, m_w_attn_proj, m_w_out, m_norm_ffn2, m_ffn2_up, m_ffn2_down, m_final_norm, v_norm_ffn1, v_ffn1_up, v_ffn1_down, v_norm_mix, v_w_in, v_b_gate, v_pool_maps, v_pool_scale, v_w_pool_proj, v_q_latent_norm, v_w_uq, v_kv_latent_norm, v_w_ukv, v_w_attn_proj, v_w_out, v_norm_ffn2, v_ffn2_up, v_ffn2_down, v_final_norm):
    order = ("norm_ffn1", "ffn1_up", "ffn1_down", "norm_mix", "w_in", "b_gate", "pool_maps", "pool_scale",
             "w_pool_proj", "q_latent_norm", "w_uq", "kv_latent_norm", "w_ukv", "w_attn_proj", "w_out",
             "norm_ffn2", "ffn2_up", "ffn2_down", "final_norm")
    w = dict(zip(order, (norm_ffn1, ffn1_up, ffn1_down, norm_mix, w_in, b_gate, pool_maps, pool_scale, w_pool_proj,
                         q_latent_norm, w_uq, kv_latent_norm, w_ukv, w_attn_proj, w_out, norm_ffn2, ffn2_up,
                         ffn2_down, final_norm)))
    m = dict(zip(order, (m_norm_ffn1, m_ffn1_up, m_ffn1_down, m_norm_mix, m_w_in, m_b_gate, m_pool_maps, m_pool_scale,
                         m_w_pool_proj, m_q_latent_norm, m_w_uq, m_kv_latent_norm, m_w_ukv, m_w_attn_proj, m_w_out,
                         m_norm_ffn2, m_ffn2_up, m_ffn2_down, m_final_norm)))
    v = dict(zip(order, (v_norm_ffn1, v_ffn1_up, v_ffn1_down, v_norm_mix, v_w_in, v_b_gate, v_pool_maps, v_pool_scale,
                         v_w_pool_proj, v_q_latent_norm, v_w_uq, v_kv_latent_norm, v_w_ukv, v_w_attn_proj, v_w_out,
                         v_norm_ffn2, v_ffn2_up, v_ffn2_down, v_final_norm)))
    L = norm_ffn1.shape[0]
    B, S, D = x.shape
    T = B * S
    packed_shapes = [w[n].shape[1:] for n in PACKED]
    me = jnp.stack([4 * lax.axis_index("x") + 2 * lax.axis_index("y") + lax.axis_index("c")]).astype(jnp.int32)

    def weight_blocks(l, token):
        zero = token[0, 0].astype(BF16)
        blocks = [w[n][l].astype(BF16) + zero for n in DIRECT]
        return blocks + [_pack([w[n][l].astype(BF16) + zero for n in PACKED], 0)]

    def kernel_weights(lands):
        stacked = dict(zip(DIRECT, lands[:len(DIRECT)]))
        stacked.update(zip(PACKED, _unpack(lands[-1], packed_shapes, 1)))
        return _layout_full({n: _stacked_to_full(n, s) for n, s in stacked.items()})

    tabs = _rope_tables(positions.reshape(T))
    xs = x.reshape(T, D)
    blocks = weight_blocks(0, jnp.zeros((8, LANE), F32))
    lands = _exchange(blocks, [(N_DEV, *b.shape) for b in blocks], _gather_plan, GATHER_PEERS, name="gather_first")
    lands, token = _gather_forward(blocks, lands, name="gather_forward")
    full, saved = [], []
    for l in range(L):
        full.append(kernel_weights(lands))
        gain1 = w["norm_ffn1"][l]
        if l + 1 < L:
            blocks = weight_blocks(l + 1, token)
            send_sems, recv_sems, srcs, lands, token = _exchange_start(
                blocks, [(N_DEV, *b.shape) for b in blocks], _gather_plan, GATHER_PEERS, name=f"gather_start_{l + 1}")
            gain1 = gain1 + token[0, 0]
        p = {n: w[n][l] for n in SMALL}
        xs, s1 = _ffn_fwd(xs, gain1, full[l]["ffn1"], "ffn1")
        xs, s2 = _mixer_fwd(xs, p, full[l], tabs, S)
        xs, s3 = _ffn_fwd(xs, p["norm_ffn2"], full[l]["ffn2"], "ffn2")
        saved.append((s1, s2, s3))
        if l + 1 < L:
            lands = _exchange_wait(send_sems, recv_sems, srcs, lands, _gather_plan, xs, name=f"gather_wait_{l + 1}")
            lands, token = _gather_forward(blocks, lands, name="gather_forward")
    dx, dfinal, loss = _loss_head(xs, final_norm, loss_target.reshape(T, D), name="loss_head")

    big_grads = {n: [None] * L for n in BIG}
    small_grads_of = [None] * L
    pending = None

    def scatter_start(names, stacked, tag):
        srcs = [stacked[n] for n in names if n not in PACKED]
        if any(n in PACKED for n in names):
            srcs.append(_pack([stacked[n] for n in PACKED], 1))
        shapes = [(SCATTER_PEERS, *s.shape[1:]) for s in srcs]
        send_sems, recv_sems, srcs, lands, token = _exchange_start(srcs, shapes, _scatter_plan, SCATTER_PEERS,
                                                                   name=f"scatter_start_{tag}")
        return (names, send_sems, recv_sems, srcs, lands, tag), token

    def scatter_finish(state, after, l):
        names, send_sems, recv_sems, srcs, lands, tag = state
        got = _exchange_wait(send_sems, recv_sems, srcs, lands, _scatter_plan, after, name=f"scatter_wait_{tag}")
        sums = [_scatter_sum(s, g, me, name="scatter_sum") for s, g in zip(srcs, got)]
        direct = [n for n in names if n not in PACKED]
        for n, g in zip(direct, sums):
            big_grads[n][l] = g
        if len(sums) > len(direct):
            for n, g in zip(PACKED, _unpack(sums[-1], packed_shapes, 0)):
                big_grads[n][l] = g

    dep = None
    for l in reversed(range(L)):
        p = {n: w[n][l] for n in SMALL}
        s1, s2, s3 = saved[l]
        small_g = {}
        dx, small_g["norm_ffn2"], dwg, dwu, dwd = _ffn_bwd(dx, p["norm_ffn2"], full[l]["ffn2"], s3, "ffn2", dep=dep)
        if pending is not None:
            scatter_finish(pending[0], dx, pending[1])
        names = ("ffn2_up", "ffn2_down")
        stacked = {"ffn2_up": _full_to_stacked("ffn2_up", jnp.concatenate([dwg, dwu], axis=1)),
                   "ffn2_down": _full_to_stacked("ffn2_down", dwd)}
        state, dep = scatter_start(names, stacked, f"ffn2_{l}")
        pending = (state, l)

        dx, gm = _mixer_bwd(dx, p, full[l], tabs, s2, S, dep=dep)
        scatter_finish(pending[0], dx, pending[1])
        nat = _mixer_grads_natural(gm)
        names = ("w_in", "w_attn_proj", "w_out") + PACKED
        state, dep = scatter_start(names, {n: _full_to_stacked(n, nat[n]) for n in names}, f"mix_{l}")
        pending = (state, l)
        small_g.update({n: nat[n] for n in SMALL if n in nat})

        dx, small_g["norm_ffn1"], dwg, dwu, dwd = _ffn_bwd(dx, p["norm_ffn1"], full[l]["ffn1"], s1, "ffn1", dep=dep)
        scatter_finish(pending[0], dx, pending[1])
        names = ("ffn1_up", "ffn1_down")
        stacked = {"ffn1_up": _full_to_stacked("ffn1_up", jnp.concatenate([dwg, dwu], axis=1)),
                   "ffn1_down": _full_to_stacked("ffn1_down", dwd)}
        state, dep = scatter_start(names, stacked, f"ffn1_{l}")
        pending = (state, l)
        small_grads_of[l] = small_g
    grad_x = dx.reshape(B, S, D)

    small_parts = [small_grads_of[l][n] for l in range(L) for n in SMALL] + [dfinal, loss[0, :1]]
    small_shapes = [p.shape for p in small_parts]
    vec = _pack([jnp.concatenate([p.reshape(-1) for p in small_parts])], 0)
    total = _sum_devices(_all_gather(vec, name="gather_small"), name="sum_small")
    scatter_finish(pending[0], total, pending[1])
    flat = total.reshape(-1)
    small_grads, at = [], 0
    for shape in small_shapes:
        size = int(np.prod(shape))
        small_grads.append(lax.slice_in_dim(flat, at, at + size).reshape(shape))
        at += size
    loss_total = small_grads[-1].reshape(())
    grad = {n: jnp.stack(big_grads[n]) for n in BIG}
    for i, n in enumerate(SMALL):
        grad[n] = jnp.stack([small_grads[l * len(SMALL) + i] for l in range(L)]).reshape(w[n].shape)
    grad["final_norm"] = small_grads[-2].reshape(final_norm.shape)

    delta, new_m, new_v = {}, {}, {}
    for n in order:
        wn, gn, mn, vn = (a.reshape(1, -1) if a.ndim == 1 else a for a in (w[n], grad[n], m[n], v[n]))
        d, nm, nv = _adamw(wn, gn, mn, vn, name="adamw_" + n)
        delta[n], new_m[n], new_v[n] = (a.reshape(w[n].shape) for a in (d, nm, nv))
    return (loss_total, grad_x, *[grad[n] for n in order], *[delta[n] for n in order],
            *[new_m[n] for n in order], *[new_v[n] for n in order])
```

```python
import functools

import numpy as np
import jax
import jax.numpy as jnp
from jax import lax
from jax.experimental import pallas as pl
from jax.experimental.pallas import tpu as pltpu

F32 = jnp.float32
BF16 = jnp.bfloat16

NORM_EPS = 1e-6
ROPE_THETA = 10000.0
QK_NOPE = 128
QK_ROPE = 64
V_DIM = 128
HEAD_W = 256
POOL_WINDOWS = (2, 4, 8, 16)
POOL_G = 128
POOL_DIM = 512
LANE = 128
ATTN_SCALE = float((QK_NOPE + QK_ROPE) ** -0.5)
MASK_VALUE = -1e30
ATTN_TILE = 512

ADAM_LR = 0.001
ADAM_B1 = 0.9
ADAM_B2 = 0.999
ADAM_EPS = 1e-08
ADAM_WD = 0.01
ADAM_STEP = 10

N_DEV = 8
VMEM_LIMIT = 52 * 1024 * 1024

MESH = pl.DeviceIdType.MESH
ANY = pl.BlockSpec(memory_space=pl.ANY)


def _tile(dim, target, align=LANE):
    if dim <= target:
        return dim
    t = (target // align) * align
    while t >= align:
        if dim % t == 0:
            return t
        t -= align
    return dim


def _params(*sem):
    return pltpu.CompilerParams(dimension_semantics=sem, vmem_limit_bytes=VMEM_LIMIT)


def _rstd(x):
    return lax.rsqrt(jnp.mean(x * x, axis=-1, keepdims=True) + NORM_EPS)


def _mm(a, b, *, name, ta=False, tb=False, out_dtype=F32, res=None, alpha=1.0, tm=512, tn=1024, tk=1024, b_row0=0):
    if ta:
        K, M = a.shape
    else:
        M, K = a.shape
    if tb:
        N, K2 = b.shape
    else:
        K2, N = b.shape
    assert K == K2 or (not tb and K2 >= b_row0 + K), (a.shape, b.shape, ta, tb)
    tm, tn, tk = _tile(M, tm), _tile(N, tn), _tile(K, tk)
    nk = K // tk
    assert b_row0 % tk == 0
    kb0 = b_row0 // tk
    dims = (((0 if ta else 1,), (1 if tb else 0,)), ((), ()))
    has_res = res is not None

    def body(*refs):
        a_ref, b_ref = refs[0], refs[1]
        res_ref = refs[2] if has_res else None
        o_ref = refs[2 + has_res]
        acc_ref = refs[3 + has_res] if nk > 1 else None
        part = lax.dot_general(a_ref[...].astype(BF16), b_ref[...].astype(BF16), dims,
                               preferred_element_type=F32)

        def finish(acc):
            r = acc * alpha if alpha != 1.0 else acc
            if has_res:
                r = res_ref[...].astype(F32) + r
            o_ref[...] = r.astype(out_dtype)

        if nk == 1:
            finish(part)
        else:
            k = pl.program_id(2)

            @pl.when(k == 0)
            def _():
                acc_ref[...] = part

            @pl.when(k > 0)
            def _():
                acc_ref[...] += part

            @pl.when(k == nk - 1)
            def _():
                finish(acc_ref[...])

    a_spec = pl.BlockSpec((tk, tm), lambda i, j, k: (k, i)) if ta else pl.BlockSpec((tm, tk), lambda i, j, k: (i, k))
    b_spec = (pl.BlockSpec((tn, tk), lambda i, j, k: (j, k)) if tb
              else pl.BlockSpec((tk, tn), lambda i, j, k: (k + kb0, j)))
    in_specs = [a_spec, b_spec]
    operands = [a, b]
    if has_res:
        in_specs.append(pl.BlockSpec((tm, tn), lambda i, j, k: (i, j)))
        operands.append(res)
    return pl.pallas_call(
        body, name=name,
        out_shape=jax.ShapeDtypeStruct((M, N), out_dtype),
        grid=(M // tm, N // tn, nk),
        in_specs=in_specs,
        out_specs=pl.BlockSpec((tm, tn), lambda i, j, k: (i, j)),
        scratch_shapes=[pltpu.VMEM((tm, tn), F32)] if nk > 1 else [],
        compiler_params=_params("parallel", "parallel", "arbitrary"),
    )(*operands)


def _rms_fwd(x, g, *, name):
    T, D = x.shape
    tm = _tile(T, 512, 16)

    def body(x_ref, g_ref, h_ref):
        x = x_ref[...]
        h_ref[...] = (x * _rstd(x) * g_ref[...]).astype(BF16)

    return pl.pallas_call(
        body, name=name,
        out_shape=jax.ShapeDtypeStruct((T, D), BF16),
        grid=(T // tm,),
        in_specs=[pl.BlockSpec((tm, D), lambda i: (i, 0)), pl.BlockSpec((1, D), lambda i: (0, 0))],
        out_specs=pl.BlockSpec((tm, D), lambda i: (i, 0)),
        compiler_params=_params("parallel"),
    )(x, g.reshape(1, D))


def _rms_bwd(x, g, dh, dxo, *, name):
    T, D = x.shape
    tm = _tile(T, 512, 16)

    def body(x_ref, g_ref, dh_ref, dxo_ref, dx_ref, dg_ref):
        x = x_ref[...]
        r = _rstd(x)
        xhat = x * r
        dh = dh_ref[...]
        dxh = dh * g_ref[...]
        dx_ref[...] = dxo_ref[...] + r * (dxh - xhat * jnp.mean(dxh * xhat, axis=-1, keepdims=True))
        part = jnp.sum(dh * xhat, axis=0, keepdims=True)

        @pl.when(pl.program_id(0) == 0)
        def _():
            dg_ref[...] = part

        @pl.when(pl.program_id(0) > 0)
        def _():
            dg_ref[...] += part

    row = pl.BlockSpec((tm, D), lambda i: (i, 0))
    vec = pl.BlockSpec((1, D), lambda i: (0, 0))
    return pl.pallas_call(
        body, name=name,
        out_shape=(jax.ShapeDtypeStruct((T, D), F32), jax.ShapeDtypeStruct((1, D), F32)),
        grid=(T // tm,),
        in_specs=[row, vec, row, row],
        out_specs=(row, vec),
        compiler_params=_params("arbitrary"),
    )(x, g.reshape(1, D), dh, dxo)


def _loss_head(x, g, target, *, name):
    T, D = x.shape
    tm = _tile(T, 512, 16)

    def body(x_ref, g_ref, t_ref, dx_ref, dg_ref, loss_ref):
        x = x_ref[...]
        gain = g_ref[...]
        r = _rstd(x)
        xhat = x * r
        err = xhat * gain - t_ref[...]
        dy = err * (1.0 / D)
        dxh = dy * gain
        dx_ref[...] = r * (dxh - xhat * jnp.mean(dxh * xhat, axis=-1, keepdims=True))
        dg_part = jnp.sum(dy * xhat, axis=0, keepdims=True)
        loss_part = jnp.full((1, LANE), 0.5 / D, F32) * jnp.sum(err * err)

        @pl.when(pl.program_id(0) == 0)
        def _():
            dg_ref[...] = dg_part
            loss_ref[...] = loss_part

        @pl.when(pl.program_id(0) > 0)
        def _():
            dg_ref[...] += dg_part
            loss_ref[...] += loss_part

    row = pl.BlockSpec((tm, D), lambda i: (i, 0))
    vec = pl.BlockSpec((1, D), lambda i: (0, 0))
    return pl.pallas_call(
        body, name=name,
        out_shape=(jax.ShapeDtypeStruct((T, D), F32), jax.ShapeDtypeStruct((1, D), F32),
                   jax.ShapeDtypeStruct((1, LANE), F32)),
        grid=(T // tm,),
        in_specs=[row, vec, row],
        out_specs=(row, vec, pl.BlockSpec((1, LANE), lambda i: (0, 0))),
        compiler_params=_params("arbitrary"),
    )(x, g.reshape(1, D), target)


def _ffn_up(h, w_up_t, *, name):
    T, D = h.shape
    F = w_up_t.shape[0] // 2
    tm, tn = _tile(T, 512, 16), _tile(F, 1408)
    nj = F // tn

    def body(h_ref, wg_ref, wu_ref, gate_ref, up_ref, a_ref):
        h = h_ref[...]
        gate = lax.dot_general(h, wg_ref[...], _NT, preferred_element_type=F32)
        up = lax.dot_general(h, wu_ref[...], _NT, preferred_element_type=F32)
        gate_ref[...] = gate.astype(BF16)
        up_ref[...] = up.astype(BF16)
        a_ref[...] = (gate * jax.nn.sigmoid(gate) * up).astype(BF16)

    o_spec = pl.BlockSpec((tm, tn), lambda j, i: (i, j))
    out = jax.ShapeDtypeStruct((T, F), BF16)
    return pl.pallas_call(
        body, name=name,
        out_shape=(out, out, out),
        grid=(nj, T // tm),
        in_specs=[pl.BlockSpec((tm, D), lambda j, i: (i, 0)),
                  pl.BlockSpec((tn, D), lambda j, i: (j, 0)),
                  pl.BlockSpec((tn, D), lambda j, i: (j + nj, 0))],
        out_specs=(o_spec, o_spec, o_spec),
        compiler_params=_params("parallel", "parallel"),
    )(h, w_up_t, w_up_t)


def _dep_spec(dep):
    return [] if dep is None else [pl.BlockSpec(dep.shape, lambda *_: (0,) * dep.ndim)]


def _ffn_bwd_act(dxo, wd, gate, up, *, alpha, name, dep=None):
    T, D = dxo.shape
    F = wd.shape[0]
    tm, tn = _tile(T, 512, 16), _tile(F, 1408)

    def body(dxo_ref, wd_ref, gate_ref, up_ref, *rest):
        dgate_ref, dup_ref = rest[-2:]
        da = lax.dot_general(dxo_ref[...].astype(BF16), wd_ref[...], (((1,), (1,)), ((), ())),
                             preferred_element_type=F32) * alpha
        gate = gate_ref[...].astype(F32)
        up = up_ref[...].astype(F32)
        sig = jax.nn.sigmoid(gate)
        dgate_ref[...] = (da * up * (sig * (1.0 + gate * (1.0 - sig)))).astype(BF16)
        dup_ref[...] = (da * (gate * sig)).astype(BF16)

    t_spec = pl.BlockSpec((tm, tn), lambda j, i: (i, j))
    out = jax.ShapeDtypeStruct((T, F), BF16)
    return pl.pallas_call(
        body, name=name,
        out_shape=(out, out),
        grid=(F // tn, T // tm),
        in_specs=[pl.BlockSpec((tm, D), lambda j, i: (i, 0)), pl.BlockSpec((tn, D), lambda j, i: (j, 0)),
                  t_spec, t_spec] + _dep_spec(dep),
        out_specs=(t_spec, t_spec),
        compiler_params=_params("parallel", "parallel"),
    )(dxo, wd, gate, up, *([] if dep is None else [dep]))


def _rope_tables(positions):
    half = QK_ROPE // 2
    inv_freq = ROPE_THETA ** (-jnp.arange(0, QK_ROPE, 2, dtype=F32) / QK_ROPE)
    ang = positions.astype(F32)[:, None] * inv_freq
    cos, sin = jnp.cos(ang), jnp.sin(ang)
    z = jnp.zeros_like(cos)
    zz = jnp.zeros((positions.shape[0], LANE - QK_ROPE), F32)
    c = jnp.concatenate([cos, cos, zz], axis=1)
    sa = jnp.concatenate([z, sin, zz], axis=1)
    sb = jnp.concatenate([-sin, z, zz], axis=1)
    return c, sa, sb


def _rotate(seg, c, sa, sb, sign):
    half = QK_ROPE // 2
    mix = pltpu.roll(seg, half, 1) * sa + pltpu.roll(seg, LANE - half, 1) * sb
    return seg * c + mix if sign > 0 else seg * c - mix


def _mixer_in(h, wa, wuq, wukv, gq, gkv, tabs, *, name):
    T, D = h.shape
    HQ, QL = wuq.shape
    KVL = wukv.shape[0]
    H = HQ // HEAD_W
    o_q, o_kv, o_kr = POOL_DIM, POOL_DIM + QL, POOL_DIM + QL + KVL
    PA = o_kr + LANE
    assert wa.shape[0] >= PA
    tm = _tile(T, 256, 16)

    def body(h_ref, wa_ref, wuq_ref, wukv_ref, gq_ref, gkv_ref, c_ref, sa_ref, sb_ref,
             xp_ref, ql_ref, kvl_ref, qn_ref, kvn_ref, q_ref, kv_ref, kr_ref):
        proj = lax.dot_general(h_ref[...], wa_ref[...], _NT, preferred_element_type=F32)
        xp_ref[...] = proj[:, :POOL_DIM]
        ql = proj[:, o_q:o_kv]
        kvl = proj[:, o_kv:o_kr]
        ql_ref[...] = ql
        kvl_ref[...] = kvl
        qn = (ql * _rstd(ql) * gq_ref[...]).astype(BF16)
        kvn = (kvl * _rstd(kvl) * gkv_ref[...]).astype(BF16)
        qn_ref[...] = qn
        kvn_ref[...] = kvn
        c, sa, sb = c_ref[...], sa_ref[...], sb_ref[...]
        q = lax.dot_general(qn, wuq_ref[...], _NT, preferred_element_type=F32)
        for hh in range(H):
            base = hh * HEAD_W
            q_ref[:, base:base + QK_NOPE] = q[:, base:base + QK_NOPE].astype(BF16)
            q_ref[:, base + QK_NOPE:base + HEAD_W] = _rotate(
                q[:, base + QK_NOPE:base + HEAD_W], c, sa, sb, 1).astype(BF16)
        kv_ref[...] = jnp.dot(kvn, wukv_ref[...], preferred_element_type=F32).astype(BF16)
        kr_ref[...] = _rotate(proj[:, o_kr:o_kr + LANE], c, sa, sb, 1).astype(BF16)

    def row(w):
        return pl.BlockSpec((tm, w), lambda i: (i, 0))

    def whole(arr):
        return pl.BlockSpec(arr.shape, lambda i: (0,) * arr.ndim)

    gq2, gkv2 = gq.reshape(1, QL), gkv.reshape(1, KVL)
    outs = [(POOL_DIM, F32), (QL, F32), (KVL, F32), (QL, BF16), (KVL, BF16), (HQ, BF16), (HQ, BF16), (LANE, BF16)]
    return pl.pallas_call(
        body, name=name,
        out_shape=tuple(jax.ShapeDtypeStruct((T, w), dt) for w, dt in outs),
        grid=(T // tm,),
        in_specs=[row(D), pl.BlockSpec((PA, D), lambda i: (0, 0)), whole(wuq), whole(wukv), whole(gq2), whole(gkv2),
                  row(LANE), row(LANE), row(LANE)],
        out_specs=tuple(row(w) for w, _ in outs),
        compiler_params=_params("parallel"),
    )(h, wa, wuq, wukv, gq2, gkv2, *tabs)


def _mixer_in_bwd(dq, dkv, dkr, ql, kvl, dxp, wuq, wukv, gq, gkv, tabs, *, name):
    T, HQ = dq.shape
    QL, KVL = wuq.shape[1], wukv.shape[0]
    H = HQ // HEAD_W
    PA = POOL_DIM + QL + KVL + LANE
    o_q, o_kv, o_kr = POOL_DIM, POOL_DIM + QL, POOL_DIM + QL + KVL
    tm = _tile(T, 256, 16)

    def norm_bwd(lat, gain, dn):
        r = _rstd(lat)
        xhat = lat * r
        dxh = dn * gain
        dlat = r * (dxh - xhat * jnp.mean(dxh * xhat, axis=-1, keepdims=True))
        return dlat, jnp.sum(dn * xhat, axis=0, keepdims=True)

    def body(dq_ref, dkv_ref, dkr_ref, ql_ref, kvl_ref, dxp_ref, wuq_ref, wukv_ref, gq_ref, gkv_ref,
             c_ref, sa_ref, sb_ref, dproj_ref, dqp_ref, dgq_ref, dgkv_ref):
        c, sa, sb = c_ref[...], sa_ref[...], sb_ref[...]
        dkr_sum = dkr_ref[:, :LANE]
        for hh in range(H):
            base = hh * HEAD_W
            dqp_ref[:, base:base + QK_NOPE] = dq_ref[:, base:base + QK_NOPE]
            dqp_ref[:, base + QK_NOPE:base + HEAD_W] = _rotate(
                dq_ref[:, base + QK_NOPE:base + HEAD_W].astype(F32), c, sa, sb, -1).astype(BF16)
            if hh:
                dkr_sum = dkr_sum + dkr_ref[:, hh * LANE:(hh + 1) * LANE]
        dqn = jnp.dot(dqp_ref[...], wuq_ref[...], preferred_element_type=F32)
        dkvn = lax.dot_general(dkv_ref[...], wukv_ref[...], _NT, preferred_element_type=F32)
        dql, dgq = norm_bwd(ql_ref[...], gq_ref[...], dqn)
        dkvl, dgkv = norm_bwd(kvl_ref[...], gkv_ref[...], dkvn)
        dproj_ref[:, :POOL_DIM] = dxp_ref[...].astype(BF16)
        dproj_ref[:, o_q:o_kv] = dql.astype(BF16)
        dproj_ref[:, o_kv:o_kr] = dkvl.astype(BF16)
        dproj_ref[:, o_kr:PA] = _rotate(dkr_sum, c, sa, sb, -1).astype(BF16)

        @pl.when(pl.program_id(0) == 0)
        def _():
            dgq_ref[...] = dgq
            dgkv_ref[...] = dgkv

        @pl.when(pl.program_id(0) > 0)
        def _():
            dgq_ref[...] += dgq
            dgkv_ref[...] += dgkv

    def row(w):
        return pl.BlockSpec((tm, w), lambda i: (i, 0))

    def whole(arr):
        return pl.BlockSpec(arr.shape, lambda i: (0,) * arr.ndim)

    gq2, gkv2 = gq.reshape(1, QL), gkv.reshape(1, KVL)
    return pl.pallas_call(
        body, name=name,
        out_shape=(jax.ShapeDtypeStruct((T, PA), BF16), jax.ShapeDtypeStruct((T, HQ), BF16),
                   jax.ShapeDtypeStruct((1, QL), F32), jax.ShapeDtypeStruct((1, KVL), F32)),
        grid=(T // tm,),
        in_specs=[row(HQ), row(HQ), row(H * LANE), row(QL), row(KVL), row(POOL_DIM), whole(wuq), whole(wukv),
                  whole(gq2), whole(gkv2), row(LANE), row(LANE), row(LANE)],
        out_specs=(row(PA), row(HQ), whole(gq2), whole(gkv2)),
        compiler_params=_params("arbitrary"),
    )(dq, dkv, dkr, ql, kvl, dxp, wuq, wukv, gq2, gkv2, *tabs)


def _pool_groups(x_of, S):
    row = lax.broadcasted_iota(jnp.int32, (S, POOL_G), 0)
    for g, w in enumerate(POOL_WINDOWS):
        x = x_of(g)
        s = x
        d = 1
        while d < w:
            s = s + jnp.where(row >= d, pltpu.roll(s, d, 0), 0.0)
            d *= 2
        cnt = jnp.minimum(row + 1, w).astype(F32)
        yield g, w, x, s / cnt - x, cnt, row


def _pool_fwd(xp, maps, scale, *, S, name):
    T = xp.shape[0]

    def body(xp_ref, maps_ref, scale_ref, ms_ref):
        for g, _, _, pooled, _, _ in _pool_groups(lambda g: xp_ref[:, g * POOL_G:(g + 1) * POOL_G], S):
            mixed = jnp.dot(pooled.astype(BF16), maps_ref[g].astype(BF16), preferred_element_type=F32)
            ms_ref[:, g * POOL_G:(g + 1) * POOL_G] = (mixed * scale_ref[:, g * POOL_G:(g + 1) * POOL_G]).astype(BF16)

    return pl.pallas_call(
        body, name=name,
        out_shape=jax.ShapeDtypeStruct((T, POOL_DIM), BF16),
        grid=(T // S,),
        in_specs=[pl.BlockSpec((S, POOL_DIM), lambda b: (b, 0)),
                  pl.BlockSpec(maps.shape, lambda b: (0, 0, 0)),
                  pl.BlockSpec((1, POOL_DIM), lambda b: (0, 0))],
        out_specs=pl.BlockSpec((S, POOL_DIM), lambda b: (b, 0)),
        compiler_params=_params("parallel"),
    )(xp, maps, scale.reshape(1, POOL_DIM))


def _pool_bwd(xp, dms, maps, scale, *, S, name):
    T = xp.shape[0]

    def body(xp_ref, dms_ref, maps_ref, scale_ref, dxp_ref, dmaps_ref, dscale_ref):
        first = pl.program_id(0) == 0
        for g, w, _, pooled, cnt, row in _pool_groups(lambda g: xp_ref[:, g * POOL_G:(g + 1) * POOL_G], S):
            cols = slice(g * POOL_G, (g + 1) * POOL_G)
            pooled_b = pooled.astype(BF16)
            maps_b = maps_ref[g].astype(BF16)
            mixed = jnp.dot(pooled_b, maps_b, preferred_element_type=F32)
            dms = dms_ref[:, cols]
            dscale = jnp.sum(dms * mixed, axis=0, keepdims=True)
            dmixed = (dms * scale_ref[:, cols]).astype(BF16)
            dmaps = lax.dot_general(pooled_b, dmixed, (((0,), (0,)), ((), ())), preferred_element_type=F32)
            dpooled = lax.dot_general(dmixed, maps_b, (((1,), (1,)), ((), ())), preferred_element_type=F32)
            z = dpooled / cnt
            d = 1
            while d < w:
                z = z + jnp.where(row < S - d, pltpu.roll(z, S - d, 0), 0.0)
                d *= 2
            dxp_ref[:, cols] = z - dpooled

            @pl.when(first)
            def _():
                dmaps_ref[g] = dmaps
                dscale_ref[:, cols] = dscale

            @pl.when(jnp.logical_not(first))
            def _():
                dmaps_ref[g] += dmaps
                dscale_ref[:, cols] += dscale

    seq = pl.BlockSpec((S, POOL_DIM), lambda b: (b, 0))
    maps_spec = pl.BlockSpec(maps.shape, lambda b: (0, 0, 0))
    vec = pl.BlockSpec((1, POOL_DIM), lambda b: (0, 0))
    return pl.pallas_call(
        body, name=name,
        out_shape=(jax.ShapeDtypeStruct((T, POOL_DIM), F32), jax.ShapeDtypeStruct(maps.shape, F32),
                   jax.ShapeDtypeStruct((1, POOL_DIM), F32)),
        grid=(T // S,),
        in_specs=[seq, seq, maps_spec, vec],
        out_specs=(seq, maps_spec, vec),
        compiler_params=_params("arbitrary"),
    )(xp, dms, maps, scale.reshape(1, POOL_DIM))


def _causal_mask(s, t):
    r = lax.broadcasted_iota(jnp.int32, (t, t), 0)
    c = lax.broadcasted_iota(jnp.int32, (t, t), 1)
    return jnp.where(r >= c, s, MASK_VALUE)


_NT = (((1,), (1,)), ((), ()))
_TN = (((0,), (0,)), ((), ()))


def _attn_fwd(q, kv, kr, *, S, name):
    T, HQ = q.shape
    H = HQ // HEAD_W
    B = T // S
    t = _tile(S, ATTN_TILE)
    n = S // t

    def body(q_ref, k_ref, v_ref, kr_ref, o_ref, lse_ref, kcat):
        kcat[:, :QK_NOPE] = k_ref[...]
        kcat[:, QK_NOPE:] = kr_ref[...]
        for i in range(n):
            rows = slice(i * t, (i + 1) * t)
            qt = q_ref[rows, :]
            m = jnp.full((t, 1), MASK_VALUE, F32)
            l = jnp.zeros((t, 1), F32)
            acc = jnp.zeros((t, V_DIM), F32)
            for j in range(i + 1):
                cols = slice(j * t, (j + 1) * t)
                s = lax.dot_general(qt, kcat[cols, :], _NT, preferred_element_type=F32) * ATTN_SCALE
                if j == i:
                    s = _causal_mask(s, t)
                m_new = jnp.maximum(m, jnp.max(s, axis=1, keepdims=True))
                p = jnp.exp(s - m_new)
                corr = jnp.exp(m - m_new)
                l = corr * l + jnp.sum(p, axis=1, keepdims=True)
                acc = corr * acc + jnp.dot(p.astype(BF16), v_ref[cols, :], preferred_element_type=F32)
                m = m_new
            o_ref[rows, :] = (acc / l).astype(BF16)
            lse_ref[rows, :] = jnp.broadcast_to(m + jnp.log(l), (t, LANE))

    seq_h = pl.BlockSpec((S, LANE), lambda b, h: (b, h))
    return pl.pallas_call(
        body, name=name,
        out_shape=(jax.ShapeDtypeStruct((T, H * V_DIM), BF16), jax.ShapeDtypeStruct((T, H * LANE), F32)),
        grid=(B, H),
        in_specs=[pl.BlockSpec((S, HEAD_W), lambda b, h: (b, h)),
                  pl.BlockSpec((S, QK_NOPE), lambda b, h: (b, 2 * h)),
                  pl.BlockSpec((S, V_DIM), lambda b, h: (b, 2 * h + 1)),
                  pl.BlockSpec((S, LANE), lambda b, h: (b, 0))],
        out_specs=(seq_h, seq_h),
        scratch_shapes=[pltpu.VMEM((S, HEAD_W), BF16)],
        compiler_params=_params("parallel", "parallel"),
    )(q, kv, kv, kr)


def _attn_bwd(q, kv, kr, o, do, lse, *, S, name):
    T, HQ = q.shape
    H = HQ // HEAD_W
    B = T // S
    t = _tile(S, ATTN_TILE)
    n = S // t

    def body(q_ref, k_ref, v_ref, kr_ref, o_ref, do_ref, lse_ref, dq_ref, dkv_ref, dkr_ref, kcat, dq_acc):
        kcat[:, :QK_NOPE] = k_ref[...]
        kcat[:, QK_NOPE:] = kr_ref[...]
        delta = [jnp.sum(do_ref[i * t:(i + 1) * t, :].astype(F32) * o_ref[i * t:(i + 1) * t, :].astype(F32),
                         axis=1, keepdims=True) for i in range(n)]
        for j in range(n):
            cols = slice(j * t, (j + 1) * t)
            kc = kcat[cols, :]
            vt = v_ref[cols, :]
            dk = jnp.zeros((t, HEAD_W), F32)
            dv = jnp.zeros((t, V_DIM), F32)
            for i in range(j, n):
                rows = slice(i * t, (i + 1) * t)
                qt = q_ref[rows, :]
                dot_ = do_ref[rows, :]
                s = lax.dot_general(qt, kc, _NT, preferred_element_type=F32) * ATTN_SCALE
                if i == j:
                    s = _causal_mask(s, t)
                p = jnp.exp(s - lse_ref[rows, :][:, :1])
                dv = dv + lax.dot_general(p.astype(BF16), dot_, _TN, preferred_element_type=F32)
                dp = lax.dot_general(dot_, vt, _NT, preferred_element_type=F32)
                ds = (p * (dp - delta[i]) * ATTN_SCALE).astype(BF16)
                dk = dk + lax.dot_general(ds, qt, _TN, preferred_element_type=F32)
                dq_part = jnp.dot(ds, kc, preferred_element_type=F32)
                if j == 0:
                    dq_acc[rows, :] = dq_part
                else:
                    dq_acc[rows, :] += dq_part
            dkv_ref[cols, :QK_NOPE] = dk[:, :QK_NOPE].astype(BF16)
            dkv_ref[cols, QK_NOPE:] = dv.astype(BF16)
            dkr_ref[cols, :] = dk[:, QK_NOPE:]
        dq_ref[...] = dq_acc[...].astype(BF16)

    seq_q = pl.BlockSpec((S, HEAD_W), lambda b, h: (b, h))
    seq_h = pl.BlockSpec((S, LANE), lambda b, h: (b, h))
    return pl.pallas_call(
        body, name=name,
        out_shape=(jax.ShapeDtypeStruct((T, HQ), BF16), jax.ShapeDtypeStruct((T, HQ), BF16),
                   jax.ShapeDtypeStruct((T, H * LANE), F32)),
        grid=(B, H),
        in_specs=[seq_q,
                  pl.BlockSpec((S, QK_NOPE), lambda b, h: (b, 2 * h)),
                  pl.BlockSpec((S, V_DIM), lambda b, h: (b, 2 * h + 1)),
                  pl.BlockSpec((S, LANE), lambda b, h: (b, 0)),
                  seq_h, seq_h, seq_h],
        out_specs=(seq_q, seq_q, seq_h),
        scratch_shapes=[pltpu.VMEM((S, HEAD_W), BF16), pltpu.VMEM((S, HEAD_W), F32)],
        compiler_params=_params("parallel", "parallel"),
    )(q, kv, kv, kr, o, do, lse)


def _merge_out(h, ms, o, x, wgate, bgate, wpp, wap, wout, *, name):
    T, D = x.shape
    tm = _tile(T, 256, 16)

    def body(h_ref, ms_ref, o_ref, x_ref, wgate_ref, bgate_ref, wpp_ref, wap_ref, wout_ref,
             gates_ref, ba_ref, bb_ref, merged_ref, xn_ref):
        logits = lax.dot_general(h_ref[...], wgate_ref[...], _NT, preferred_element_type=F32) + bgate_ref[...]
        gates = jax.nn.sigmoid(logits)
        ba = jnp.dot(ms_ref[...], wpp_ref[...], preferred_element_type=F32)
        bb = jnp.dot(o_ref[...], wap_ref[...], preferred_element_type=F32)
        merged = (gates[:, :D] * ba + gates[:, D:] * bb).astype(BF16)
        gates_ref[...] = gates.astype(BF16)
        ba_ref[...] = ba.astype(BF16)
        bb_ref[...] = bb.astype(BF16)
        merged_ref[...] = merged
        xn_ref[...] = x_ref[...] + jnp.dot(merged, wout_ref[...], preferred_element_type=F32)

    def row(w):
        return pl.BlockSpec((tm, w), lambda i: (i, 0))

    def whole(arr):
        return pl.BlockSpec(arr.shape, lambda i: (0,) * arr.ndim)

    bg2 = bgate.reshape(1, 2 * D)
    act = jax.ShapeDtypeStruct((T, D), BF16)
    return pl.pallas_call(
        body, name=name,
        out_shape=(jax.ShapeDtypeStruct((T, 2 * D), BF16), act, act, act, jax.ShapeDtypeStruct((T, D), F32)),
        grid=(T // tm,),
        in_specs=[row(D), row(ms.shape[1]), row(o.shape[1]), row(D), whole(wgate), whole(bg2), whole(wpp),
                  whole(wap), whole(wout)],
        out_specs=(row(2 * D), row(D), row(D), row(D), row(D)),
        compiler_params=_params("parallel"),
    )(h, ms, o, x, wgate, bg2, wpp, wap, wout)


def _merge_bwd(dxo, wout, gates, ba, bb, *, name, dep=None):
    T, D = dxo.shape
    tm = _tile(T, 256, 16)

    def body(dxo_ref, wout_ref, gates_ref, ba_ref, bb_ref, *rest):
        dba_ref, dbb_ref, dgl_ref, dbg_ref = rest[-4:]
        dm = lax.dot_general(dxo_ref[...].astype(BF16), wout_ref[...], _NT, preferred_element_type=F32)
        ga = gates_ref[:, :D].astype(F32)
        gb = gates_ref[:, D:].astype(F32)
        dba_ref[...] = (dm * ga).astype(BF16)
        dbb_ref[...] = (dm * gb).astype(BF16)
        dgl_a = dm * ba_ref[...].astype(F32) * (ga * (1.0 - ga))
        dgl_b = dm * bb_ref[...].astype(F32) * (gb * (1.0 - gb))
        dgl_ref[:, :D] = dgl_a.astype(BF16)
        dgl_ref[:, D:] = dgl_b.astype(BF16)
        sa = jnp.sum(dgl_a, axis=0, keepdims=True)
        sb = jnp.sum(dgl_b, axis=0, keepdims=True)

        @pl.when(pl.program_id(0) == 0)
        def _():
            dbg_ref[:, :D] = sa
            dbg_ref[:, D:] = sb

        @pl.when(pl.program_id(0) > 0)
        def _():
            dbg_ref[:, :D] += sa
            dbg_ref[:, D:] += sb

    def row(w):
        return pl.BlockSpec((tm, w), lambda i: (i, 0))

    act = jax.ShapeDtypeStruct((T, D), BF16)
    return pl.pallas_call(
        body, name=name,
        out_shape=(act, act, jax.ShapeDtypeStruct((T, 2 * D), BF16), jax.ShapeDtypeStruct((1, 2 * D), F32)),
        grid=(T // tm,),
        in_specs=[row(D), pl.BlockSpec(wout.shape, lambda i: (0, 0)), row(2 * D), row(D), row(D)] + _dep_spec(dep),
        out_specs=(row(D), row(D), row(2 * D), pl.BlockSpec((1, 2 * D), lambda i: (0, 0))),
        compiler_params=_params("arbitrary"),
    )(dxo, wout, gates, ba, bb, *([] if dep is None else [dep]))


def _ffn_fwd(x, gain, w, tag):
    h = _rms_fwd(x, gain, name=f"{tag}_norm")
    gate, up, a = _ffn_up(h, w["up_t"], name=f"{tag}_up")
    xn = _mm(a, w["wd"], res=x, alpha=0.5, name=f"{tag}_down", tk=2816)
    return xn, (x, h, gate, up, a)


def _ffn_bwd(dxo, gain, w, saved, tag, dep=None):
    x, h, gate, up, a = saved
    F = gate.shape[1]
    dgate, dup = _ffn_bwd_act(dxo, w["wd"], gate, up, alpha=0.5, name=f"{tag}_bwd_act", dep=dep)
    dwd = _mm(a, dxo, ta=True, alpha=0.5, out_dtype=BF16, name=f"{tag}_dwd", tm=1408, tn=1024, tk=1024)
    dwg_t = _mm(dgate, h, ta=True, out_dtype=BF16, name=f"{tag}_dwg", tm=1408, tn=1024, tk=2048)
    dwu_t = _mm(dup, h, ta=True, out_dtype=BF16, name=f"{tag}_dwu", tm=1408, tn=1024, tk=2048)
    dh = _mm(dgate, w["up_t"], name=f"{tag}_dh_gate", tk=2816)
    dh = _mm(dup, w["up_t"], b_row0=F, res=dh, name=f"{tag}_dh_up", tk=2816)
    dx, dgain = _rms_bwd(x, gain, dh, dxo, name=f"{tag}_norm_bwd")
    return dx, dgain, jnp.concatenate([dwg_t, dwu_t], axis=0), dwd


def _mixer_fwd(x, p, w, tabs, S):
    h = _rms_fwd(x, p["norm_mix"], name="mix_norm")
    xp, ql, kvl, qn, kvn, q, kv, kr = _mixer_in(h, w["win_t"], w["wuq_t"], w["wukv"], p["q_latent_norm"],
                                                 p["kv_latent_norm"], tabs, name="mix_in")
    ms = _pool_fwd(xp, p["pool_maps"], p["pool_scale"], S=S, name="pool_fwd")
    o, lse = _attn_fwd(q, kv, kr, S=S, name="attn_fwd")
    gates, ba, bb, merged, xn = _merge_out(h, ms, o, x, w["wgate_t"], p["b_gate"], w["wpp"], w["wap"], w["wout"],
                                           name="merge_out")
    return xn, (x, h, xp, ql, kvl, qn, kvn, q, kv, kr, ms, o, lse, gates, ba, bb, merged)


def _mixer_bwd(dxo, p, w, tabs, saved, S, dep=None):
    x, h, xp, ql, kvl, qn, kvn, q, kv, kr, ms, o, lse, gates, ba, bb, merged = saved
    dba, dbb, dgl, dbg = _merge_bwd(dxo, w["wout"], gates, ba, bb, name="merge_bwd", dep=dep)
    g = {}
    g["wout"] = _mm(merged, dxo, ta=True, out_dtype=BF16, name="d_wout", tm=1024, tk=1024)
    g["wpp"] = _mm(ms, dba, ta=True, out_dtype=BF16, name="d_wpp", tk=2048)
    g["wap"] = _mm(o, dbb, ta=True, out_dtype=BF16, name="d_wap", tm=1024, tk=2048)
    dms = _mm(dba, w["wpp"], tb=True, name="d_ms")
    do = _mm(dbb, w["wap"], tb=True, out_dtype=BF16, name="d_o")
    dxp, g["pool_maps"], g["pool_scale"] = _pool_bwd(xp, dms, p["pool_maps"], p["pool_scale"], S=S, name="pool_bwd")
    dq, dkv, dkr = _attn_bwd(q, kv, kr, o, do, lse, S=S, name="attn_bwd")
    dproj, dqp, g["q_latent_norm"], g["kv_latent_norm"] = _mixer_in_bwd(
        dq, dkv, dkr, ql, kvl, dxp, w["wuq_t"], w["wukv"], p["q_latent_norm"], p["kv_latent_norm"], tabs,
        name="mix_in_bwd")
    g["wuq_t"] = _mm(dqp, qn, ta=True, out_dtype=BF16, name="d_wuq", tm=2048, tk=2048)
    g["wukv"] = _mm(kvn, dkv, ta=True, out_dtype=BF16, name="d_wukv", tn=2048, tk=2048)
    g["wa_t"] = _mm(dproj, h, ta=True, out_dtype=BF16, name="d_wa", tm=1280, tn=1024, tk=2048)
    g["wgate_t"] = _mm(dgl, h, ta=True, out_dtype=BF16, name="d_wgate", tm=2048, tn=1024, tk=1024)
    dh = _mm(dproj, w["win_t"], name="dh_mix_a", tk=1280)
    dh = _mm(dgl, w["wgate_t"], res=dh, name="dh_mix_gate", tk=2048)
    dx, g["norm_mix"] = _rms_bwd(x, p["norm_mix"], dh, dxo, name="mix_norm_bwd")
    g["b_gate"] = dbg
    return dx, g


BIG = ("ffn1_up", "ffn1_down", "w_in", "w_pool_proj", "w_uq", "w_ukv", "w_attn_proj", "w_out", "ffn2_up", "ffn2_down")
SMALL = ("norm_ffn1", "norm_mix", "b_gate", "pool_maps", "pool_scale", "q_latent_norm", "kv_latent_norm", "norm_ffn2")
PACKED = ("w_pool_proj", "w_uq", "w_ukv")
DIRECT = tuple(n for n in BIG if n not in PACKED)
TRANSPOSED = ("ffn1_up", "ffn2_up", "w_in", "w_uq")
COL_SHARDED = ("w_pool_proj", "w_ukv")
QK_HEAD = QK_NOPE + QK_ROPE


def _rows(stacked):
    n, r, c = stacked.shape
    return stacked.reshape(n * r, c)


def _cols(stacked):
    n, k, c = stacked.shape
    return stacked.transpose(1, 0, 2).reshape(k, n * c)


def _kernel_weights(stacked):
    win_t = _rows(stacked["w_in"])
    D = win_t.shape[1]
    wuq_t = _rows(stacked["w_uq"])
    QL = wuq_t.shape[1]
    H = wuq_t.shape[0] // QK_HEAD
    wuq_t = jnp.pad(wuq_t.reshape(H, QK_HEAD, QL), ((0, 0), (0, HEAD_W - QK_HEAD), (0, 0))).reshape(H * HEAD_W, QL)
    full = {"win_t": win_t, "wgate_t": win_t[win_t.shape[0] - 2 * D:], "wuq_t": wuq_t,
            "wukv": _cols(stacked["w_ukv"]), "wpp": _cols(stacked["w_pool_proj"]),
            "wap": _rows(stacked["w_attn_proj"]), "wout": _rows(stacked["w_out"])}
    for tag in ("ffn1", "ffn2"):
        full[tag] = {"up_t": _rows(stacked[tag + "_up"]), "wd": _rows(stacked[tag + "_down"])}
    return full


def _split_rows(full):
    return full.reshape(N_DEV, full.shape[0] // N_DEV, full.shape[1])


def _split_cols(full):
    k, cols = full.shape
    return full.reshape(k, N_DEV, cols // N_DEV).transpose(1, 0, 2)


def _mixer_grads_stacked(g):
    n_a = g["wa_t"].shape[0] - (LANE - QK_ROPE)
    HQ, QL = g["wuq_t"].shape
    H = HQ // HEAD_W
    wuq_t = g["wuq_t"].reshape(H, HEAD_W, QL)[:, :QK_HEAD].reshape(H * QK_HEAD, QL)
    return {"w_in": _split_rows(jnp.concatenate([g["wa_t"][:n_a], g["wgate_t"]], axis=0)),
            "w_uq": _split_rows(wuq_t),
            "w_pool_proj": _split_cols(g["wpp"]), "w_ukv": _split_cols(g["wukv"]),
            "w_attn_proj": _split_rows(g["wap"]), "w_out": _split_rows(g["wout"])}


def _mesh_place():
    x, y, c = lax.axis_index("x"), lax.axis_index("y"), lax.axis_index("c")
    chips = [(1 - x, y), (x, 1 - y), (1 - x, 1 - y)]
    return x, y, c, chips


def _all_gather(block, *, name):
    R, C = block.shape

    def body(x_ref, out_ref, send_sems, recv_sems, local_sem):
        x, y, c, chips = _mesh_place()
        me, sibling = (x, y, c), (x, y, 1 - c)

        def slot(px, py, pc):
            return out_ref.at[4 * px + 2 * py + pc]

        def copy(k, blk, to, src=None):
            return pltpu.make_async_remote_copy(
                src_ref=slot(*blk) if src is None else src, dst_ref=slot(*blk),
                send_sem=send_sems.at[k], recv_sem=recv_sems.at[k], device_id=to, device_id_type=MESH)

        mine = pltpu.make_async_copy(x_ref, slot(*me), local_sem)
        mine.start()
        first = [copy(0, me, sibling, src=x_ref)]
        first += [copy(1 + j, me, (*chip, c), src=x_ref) for j, chip in enumerate(chips)]
        for cp in first:
            cp.start()
        passed = [copy(4 + j, (*chip, c), sibling) for j, chip in enumerate(chips)]
        for j, chip in enumerate(chips):
            copy(1 + j, (*chip, c), me).wait_recv()
            passed[j].start()
        copy(0, sibling, me).wait_recv()
        for j, chip in enumerate(chips):
            copy(4 + j, (*chip, 1 - c), me).wait_recv()
        for cp in first + passed:
            cp.wait_send()
        mine.wait()

    return pl.pallas_call(
        body, name=name,
        out_shape=jax.ShapeDtypeStruct((N_DEV, R, C), block.dtype),
        in_specs=[ANY], out_specs=ANY,
        scratch_shapes=[pltpu.SemaphoreType.DMA((7,)), pltpu.SemaphoreType.DMA((7,)), pltpu.SemaphoreType.DMA],
    )(block)


HBM = pl.BlockSpec(memory_space=pltpu.HBM)
SEMAPHORES = pl.BlockSpec(memory_space=pltpu.SEMAPHORE)
DATAFLOW = pltpu.SideEffectType.DATAFLOW_SIDE_EFFECTING
GATHER_PEERS = 4
SCATTER_PEERS = 7


def _in_hbm(a):
    return pltpu.with_memory_space_constraint(a, pltpu.HBM)


def _gather_plan(src_refs, land_refs):
    x, y, c, chips = _mesh_place()
    me = 4 * x + 2 * y + c
    targets = [(x, y, 1 - c)] + [(cx, cy, c) for cx, cy in chips]
    return [(s, land.at[me], to) for s, land in zip(src_refs, land_refs) for to in targets]


def _scatter_plan(src_refs, land_refs):
    x, y, c, _ = _mesh_place()
    peers = [(x, y, 1 - c), (1 - x, y, c), (x, 1 - y, c), (1 - x, 1 - y, c),
             (1 - x, y, 1 - c), (x, 1 - y, 1 - c), (1 - x, 1 - y, 1 - c)]
    return [(s.at[4 * px + 2 * py + pc], land.at[k], (px, py, pc))
            for s, land in zip(src_refs, land_refs) for k, (px, py, pc) in enumerate(peers)]


def _descriptors(plan, src_refs, land_refs, send_sems, recv_sems):
    return [pltpu.make_async_remote_copy(src_ref=s, dst_ref=d, send_sem=send_sems.at[k], recv_sem=recv_sems.at[k],
                                         device_id=to, device_id_type=MESH)
            for k, (s, d, to) in enumerate(plan(src_refs, land_refs))]


def _exchange(srcs, land_shapes, plan, per_src, *, name):
    n = len(srcs)

    def body(*refs):
        copies = _descriptors(plan, refs[:n], refs[n:2 * n], refs[2 * n], refs[2 * n + 1])
        for cp in copies:
            cp.start()
        for cp in copies:
            cp.wait()

    return pl.pallas_call(
        body, name=name,
        out_shape=tuple(jax.ShapeDtypeStruct(shape, s.dtype) for shape, s in zip(land_shapes, srcs)),
        in_specs=[ANY] * n, out_specs=(ANY,) * n,
        scratch_shapes=[pltpu.SemaphoreType.DMA((per_src * n,)), pltpu.SemaphoreType.DMA((per_src * n,))],
    )(*srcs)


def _exchange_start(srcs, land_shapes, plan, per_src, *, name):
    n = len(srcs)
    lands = [lax.empty(shape, s.dtype) for shape, s in zip(land_shapes, srcs)]

    def body(*refs):
        for cp in _descriptors(plan, refs[:n], refs[n:2 * n], refs[2 * n], refs[2 * n + 1]):
            cp.start()
        refs[-1][...] = jnp.zeros_like(refs[-1])

    sems = pltpu.SemaphoreType.DMA((per_src * n,))
    out = pl.pallas_call(
        body, name=name,
        out_shape=(sems, sems, *[pltpu.HBM(a.shape, a.dtype) for a in srcs + lands],
                   jax.ShapeDtypeStruct((8, LANE), F32)),
        in_specs=(HBM,) * (2 * n),
        out_specs=(SEMAPHORES, SEMAPHORES, *[HBM] * (2 * n), pl.BlockSpec(memory_space=pltpu.VMEM)),
        input_output_aliases={i: 2 + i for i in range(2 * n)},
        compiler_params=pltpu.CompilerParams(has_side_effects=DATAFLOW),
    )(*[_in_hbm(a) for a in srcs + lands])
    return out[0], out[1], list(out[2:2 + n]), list(out[2 + n:2 + 2 * n]), out[-1]


def _exchange_wait(send_sems, recv_sems, srcs, lands, plan, after, *, name):
    n = len(srcs)

    def body(*refs):
        for cp in _descriptors(plan, refs[:n], refs[n:2 * n], refs[2 * n], refs[2 * n + 1]):
            cp.wait_send()
            cp.wait_recv()

    out = pl.pallas_call(
        body, name=name,
        out_shape=tuple(pltpu.HBM(a.shape, a.dtype) for a in srcs + lands),
        in_specs=(*[HBM] * (2 * n), SEMAPHORES, SEMAPHORES, ANY),
        out_specs=(HBM,) * (2 * n),
        input_output_aliases={i: i for i in range(2 * n)},
        compiler_params=pltpu.CompilerParams(has_side_effects=DATAFLOW),
    )(*srcs, *lands, send_sems, recv_sems, after)
    return list(out[:n]), list(out[n:])


def _gather_forward(lands, *, name):
    n = len(lands)

    def body(*refs):
        in_refs, out_refs = refs[:n], refs[n:2 * n]
        token, send_sems, recv_sems = refs[2 * n:2 * n + 3]
        x, y, c, chips = _mesh_place()
        passed = [pltpu.make_async_remote_copy(
            src_ref=i.at[4 * cx + 2 * cy + c], dst_ref=o.at[4 * cx + 2 * cy + c],
            send_sem=send_sems.at[3 * b + j], recv_sem=recv_sems.at[3 * b + j],
            device_id=(x, y, 1 - c), device_id_type=MESH)
            for b, (i, o) in enumerate(zip(in_refs, out_refs)) for j, (cx, cy) in enumerate(chips)]
        for cp in passed:
            cp.start()
        for cp in passed:
            cp.wait()
        token[...] = jnp.zeros_like(token)

    out = pl.pallas_call(
        body, name=name,
        out_shape=(*[jax.ShapeDtypeStruct(a.shape, a.dtype) for a in lands], jax.ShapeDtypeStruct((8, LANE), F32)),
        in_specs=[ANY] * n,
        out_specs=(*[ANY] * n, pl.BlockSpec(memory_space=pltpu.VMEM)),
        input_output_aliases={i: i for i in range(n)},
        scratch_shapes=[pltpu.SemaphoreType.DMA((3 * n,)), pltpu.SemaphoreType.DMA((3 * n,))],
    )(*lands)
    return list(out[:n]), out[n]


def _scatter_sum(parts, got, me, *, name):
    shard = parts.shape[1:]
    cols = shard[-1]
    rows = int(np.prod(shard[:-1]))
    tr = _tile(rows, 256, 16)

    def body(me_ref, p_ref, g_ref, o_ref):
        acc = p_ref[...].astype(F32)
        for k in range(SCATTER_PEERS):
            acc = acc + g_ref[k].astype(F32)
        o_ref[...] = acc

    out = pl.pallas_call(
        body, name=name,
        out_shape=jax.ShapeDtypeStruct((rows, cols), F32),
        grid_spec=pltpu.PrefetchScalarGridSpec(
            num_scalar_prefetch=1, grid=(rows // tr,),
            in_specs=[pl.BlockSpec((None, tr, cols), lambda r, me_ref: (me_ref[0], r, 0)),
                      pl.BlockSpec((SCATTER_PEERS, tr, cols), lambda r, me_ref: (0, r, 0))],
            out_specs=pl.BlockSpec((tr, cols), lambda r, me_ref: (r, 0))),
        compiler_params=_params("parallel"),
    )(me, parts.reshape(N_DEV, rows, cols), got.reshape(SCATTER_PEERS, rows, cols))
    return out.reshape(shard)


def _sum_devices(parts, *, name):
    _, R, C = parts.shape
    tr = _tile(R, 512, 8)

    def body(p_ref, o_ref):
        acc = p_ref[0]
        for d in range(1, N_DEV):
            acc = acc + p_ref[d]
        o_ref[...] = acc

    return pl.pallas_call(
        body, name=name,
        out_shape=jax.ShapeDtypeStruct((R, C), F32),
        grid=(R // tr,),
        in_specs=[pl.BlockSpec((N_DEV, tr, C), lambda r: (0, r, 0))],
        out_specs=pl.BlockSpec((tr, C), lambda r: (r, 0)),
        compiler_params=_params("parallel"),
    )(parts)


def _adamw(w, g, m, v, *, name):
    shape = w.shape
    cols = shape[-1]
    rows = w.size // cols
    tr = _tile(rows, 256, 8)

    def body(w_ref, g_ref, m_ref, v_ref, d_ref, nm_ref, nv_ref):
        g = g_ref[...]
        m = ADAM_B1 * m_ref[...] + (1.0 - ADAM_B1) * g
        v = ADAM_B2 * v_ref[...] + (1.0 - ADAM_B2) * jnp.square(g)
        m_hat = m / (1.0 - ADAM_B1 ** ADAM_STEP)
        v_hat = v / (1.0 - ADAM_B2 ** ADAM_STEP)
        d_ref[...] = -ADAM_LR * (m_hat / (jnp.sqrt(v_hat) + ADAM_EPS) + ADAM_WD * w_ref[...])
        nm_ref[...] = m
        nv_ref[...] = v

    spec = pl.BlockSpec((tr, cols), lambda i: (i, 0))
    out = jax.ShapeDtypeStruct((rows, cols), F32)
    d, nm, nv = pl.pallas_call(
        body, name=name,
        out_shape=(out, out, out),
        grid=(rows // tr,),
        in_specs=[spec] * 4, out_specs=(spec,) * 3,
        compiler_params=_params("parallel"),
    )(*(a.reshape(rows, cols) for a in (w, g, m, v)))
    return d.reshape(shape), nm.reshape(shape), nv.reshape(shape)


PACK_ALIGN = 16 * LANE


def _pack(pieces, lead):
    out = []
    for p in pieces:
        keep = p.shape[:lead]
        flat = p.reshape(*keep, -1)
        pad = (-flat.shape[-1]) % PACK_ALIGN
        if pad:
            flat = jnp.pad(flat, [(0, 0)] * lead + [(0, pad)])
        out.append(flat.reshape(*keep, -1, LANE))
    return jnp.concatenate(out, axis=lead)


def _unpack(buf, shapes, lead):
    keep = buf.shape[:lead]
    out, row = [], 0
    for shape in shapes:
        size = int(np.prod(shape))
        rows = -(-size // PACK_ALIGN) * (PACK_ALIGN // LANE)
        piece = lax.slice_in_dim(buf, row, row + rows, axis=lead).reshape(*keep, rows * LANE)
        out.append(lax.slice_in_dim(piece, 0, size, axis=lead).reshape(*keep, *shape))
        row += rows
    return out


def kernel(x, positions, norm_ffn1, ffn1_up, ffn1_down, norm_mix, w_in, b_gate, pool_maps, pool_scale, w_pool_proj, q_latent_norm, w_uq, kv_latent_norm, w_ukv, w_attn_proj, w_out, norm_ffn2, ffn2_up, ffn2_down, final_norm, loss_target, m_norm_ffn1, m_ffn1_up, m_ffn1_down, m_norm_mix, m_w_in, m_b_gate, m_pool_maps, m_pool_scale, m_w_pool_proj, m_q_latent_norm, m_w_uq, m_kv_latent_norm, m_w_ukv, m_w_attn_proj, m_w_out, m_norm_ffn2, m_ffn2_up, m_ffn2_down, m_final_norm, v_norm_ffn1, v_ffn1_up, v_ffn1_down, v_norm_mix, v_w_in, v_b_gate, v_pool_maps, v_pool_scale, v_w_pool_proj, v_q_latent_norm, v_w_uq, v_kv_latent_norm, v_w_ukv, v_w_attn_proj, v_w_out, v_norm_ffn2, v_ffn2_up, v_ffn2_down, v_final_norm):
    order = ("norm_ffn1", "ffn1_up", "ffn1_down", "norm_mix", "w_in", "b_gate", "pool_maps", "pool_scale",
             "w_pool_proj", "q_latent_norm", "w_uq", "kv_latent_norm", "w_ukv", "w_attn_proj", "w_out",
             "norm_ffn2", "ffn2_up", "ffn2_down", "final_norm")
    w = dict(zip(order, (norm_ffn1, ffn1_up, ffn1_down, norm_mix, w_in, b_gate, pool_maps, pool_scale, w_pool_proj,
                         q_latent_norm, w_uq, kv_latent_norm, w_ukv, w_attn_proj, w_out, norm_ffn2, ffn2_up,
                         ffn2_down, final_norm)))
    m = dict(zip(order, (m_norm_ffn1, m_ffn1_up, m_ffn1_down, m_norm_mix, m_w_in, m_b_gate, m_pool_maps, m_pool_scale,
                         m_w_pool_proj, m_q_latent_norm, m_w_uq, m_kv_latent_norm, m_w_ukv, m_w_attn_proj, m_w_out,
                         m_norm_ffn2, m_ffn2_up, m_ffn2_down, m_final_norm)))
    v = dict(zip(order, (v_norm_ffn1, v_ffn1_up, v_ffn1_down, v_norm_mix, v_w_in, v_b_gate, v_pool_maps, v_pool_scale,
                         v_w_pool_proj, v_q_latent_norm, v_w_uq, v_kv_latent_norm, v_w_ukv, v_w_attn_proj, v_w_out,
                         v_norm_ffn2, v_ffn2_up, v_ffn2_down, v_final_norm)))
    L = norm_ffn1.shape[0]
    B, S, D = x.shape
    T = B * S

    def turned(a, n):
        return a.transpose(0, 2, 1) if n in TRANSPOSED else a

    wk, mk, vk = ({n: turned(d[n], n) for n in order} for d in (w, m, v))
    packed_shapes = [wk[n].shape[1:] for n in PACKED]
    my_slot = 4 * lax.axis_index("x") + 2 * lax.axis_index("y") + lax.axis_index("c")
    me = jnp.stack([my_slot]).astype(jnp.int32)

    def weight_blocks(l, token):
        zero = token[0, 0].astype(BF16)
        blocks = [wk[n][l].astype(BF16) + zero for n in DIRECT]
        return blocks + [_pack([wk[n][l].astype(BF16) + zero for n in PACKED], 0)]

    def kernel_weights(blocks, lands):
        lands = [lax.dynamic_update_index_in_dim(land, b, my_slot, 0) for land, b in zip(lands, blocks)]
        stacked = dict(zip(DIRECT, lands[:len(DIRECT)]))
        stacked.update(zip(PACKED, _unpack(lands[-1], packed_shapes, 1)))
        return _kernel_weights(stacked)

    tabs = _rope_tables(positions.reshape(T))
    xs = x.reshape(T, D)
    blocks = weight_blocks(0, jnp.zeros((8, LANE), F32))
    lands = _exchange(blocks, [(N_DEV, *b.shape) for b in blocks], _gather_plan, GATHER_PEERS, name="gather_first")
    lands, token = _gather_forward(lands, name="gather_forward")
    full, saved = [], []
    for l in range(L):
        full.append(kernel_weights(blocks, lands))
        gain1 = w["norm_ffn1"][l]
        if l + 1 < L:
            blocks = weight_blocks(l + 1, token)
            send_sems, recv_sems, srcs, lands, token = _exchange_start(
                blocks, [(N_DEV, *b.shape) for b in blocks], _gather_plan, GATHER_PEERS, name=f"gather_start_{l + 1}")
            gain1 = gain1 + token[0, 0]
        p = {n: w[n][l] for n in SMALL}
        xs, s1 = _ffn_fwd(xs, gain1, full[l]["ffn1"], "ffn1")
        xs, s2 = _mixer_fwd(xs, p, full[l], tabs, S)
        xs, s3 = _ffn_fwd(xs, p["norm_ffn2"], full[l]["ffn2"], "ffn2")
        saved.append((s1, s2, s3))
        if l + 1 < L:
            blocks, lands = _exchange_wait(send_sems, recv_sems, srcs, lands, _gather_plan, xs,
                                           name=f"gather_wait_{l + 1}")
            lands, token = _gather_forward(lands, name="gather_forward")
    dx, dfinal, loss = _loss_head(xs, final_norm, loss_target.reshape(T, D), name="loss_head")

    big_grads = {n: [None] * L for n in BIG}
    small_grads_of = [None] * L
    pending = None

    def scatter_start(names, stacked, tag):
        srcs = [stacked[n] for n in names if n not in PACKED]
        if any(n in PACKED for n in names):
            srcs.append(_pack([stacked[n] for n in PACKED], 1))
        shapes = [(SCATTER_PEERS, *s.shape[1:]) for s in srcs]
        send_sems, recv_sems, srcs, lands, token = _exchange_start(srcs, shapes, _scatter_plan, SCATTER_PEERS,
                                                                   name=f"scatter_start_{tag}")
        return (names, send_sems, recv_sems, srcs, lands, tag), token

    def scatter_finish(state, after, l):
        names, send_sems, recv_sems, srcs, lands, tag = state
        srcs, got = _exchange_wait(send_sems, recv_sems, srcs, lands, _scatter_plan, after, name=f"scatter_wait_{tag}")
        sums = [_scatter_sum(s, g, me, name="scatter_sum") for s, g in zip(srcs, got)]
        direct = [n for n in names if n not in PACKED]
        for n, g in zip(direct, sums):
            big_grads[n][l] = g
        if len(sums) > len(direct):
            for n, g in zip(PACKED, _unpack(sums[-1], packed_shapes, 0)):
                big_grads[n][l] = g

    dep = None
    for l in reversed(range(L)):
        p = {n: w[n][l] for n in SMALL}
        s1, s2, s3 = saved[l]
        small_g = {}
        dx, small_g["norm_ffn2"], dup_t, dwd = _ffn_bwd(dx, p["norm_ffn2"], full[l]["ffn2"], s3, "ffn2", dep=dep)
        if pending is not None:
            scatter_finish(pending[0], dx, pending[1])
        stacked = {"ffn2_up": _split_rows(dup_t), "ffn2_down": _split_rows(dwd)}
        state, dep = scatter_start(("ffn2_up", "ffn2_down"), stacked, f"ffn2_{l}")
        pending = (state, l)

        dx, gm = _mixer_bwd(dx, p, full[l], tabs, s2, S, dep=dep)
        scatter_finish(pending[0], dx, pending[1])
        names = ("w_in", "w_attn_proj", "w_out") + PACKED
        state, dep = scatter_start(names, _mixer_grads_stacked(gm), f"mix_{l}")
        pending = (state, l)
        small_g.update({n: gm[n] for n in SMALL if n in gm})

        dx, small_g["norm_ffn1"], dup_t, dwd = _ffn_bwd(dx, p["norm_ffn1"], full[l]["ffn1"], s1, "ffn1", dep=dep)
        scatter_finish(pending[0], dx, pending[1])
        stacked = {"ffn1_up": _split_rows(dup_t), "ffn1_down": _split_rows(dwd)}
        state, dep = scatter_start(("ffn1_up", "ffn1_down"), stacked, f"ffn1_{l}")
        pending = (state, l)
        small_grads_of[l] = small_g
    grad_x = dx.reshape(B, S, D)

    small_parts = [small_grads_of[l][n] for l in range(L) for n in SMALL] + [dfinal, loss[0, :1]]
    small_shapes = [p.shape for p in small_parts]
    vec = _pack([jnp.concatenate([p.reshape(-1) for p in small_parts])], 0)
    total = _sum_devices(_all_gather(vec, name="gather_small"), name="sum_small")
    scatter_finish(pending[0], total, pending[1])
    flat = total.reshape(-1)
    small_grads, at = [], 0
    for shape in small_shapes:
        size = int(np.prod(shape))
        small_grads.append(lax.slice_in_dim(flat, at, at + size).reshape(shape))
        at += size
    loss_total = small_grads[-1].reshape(())
    gk = {n: jnp.stack(big_grads[n]) for n in BIG}
    for i, n in enumerate(SMALL):
        gk[n] = jnp.stack([small_grads[l * len(SMALL) + i] for l in range(L)]).reshape(w[n].shape)
    gk["final_norm"] = small_grads[-2].reshape(final_norm.shape)

    grad, delta, new_m, new_v = {}, {}, {}, {}
    for n in order:
        wn, gn, mn, vn = (a.reshape(1, -1) if a.ndim == 1 else a for a in (wk[n], gk[n], mk[n], vk[n]))
        d, nm, nv = _adamw(wn, gn, mn, vn, name="adamw_" + n)
        grad[n] = turned(gk[n], n)
        delta[n], new_m[n], new_v[n] = (turned(a.reshape(wk[n].shape), n) for a in (d, nm, nv))
    return (loss_total, grad_x, *[grad[n] for n in order], *[delta[n] for n in order],
            *[new_m[n] for n in order], *[new_v[n] for n in order])
```

```python
import functools

import numpy as np
import jax
import jax.numpy as jnp
from jax import lax
from jax.experimental import pallas as pl
from jax.experimental.pallas import tpu as pltpu

F32 = jnp.float32
BF16 = jnp.bfloat16

NORM_EPS = 1e-6
ROPE_THETA = 10000.0
QK_NOPE = 128
QK_ROPE = 64
V_DIM = 128
HEAD_W = 256
POOL_WINDOWS = (2, 4, 8, 16)
POOL_G = 128
POOL_DIM = 512
LANE = 128
ATTN_SCALE = float((QK_NOPE + QK_ROPE) ** -0.5)
MASK_VALUE = -1e30
ATTN_TILE = 512

ADAM_LR = 0.001
ADAM_B1 = 0.9
ADAM_B2 = 0.999
ADAM_EPS = 1e-08
ADAM_WD = 0.01
ADAM_STEP = 10

N_DEV = 8
VMEM_LIMIT = 52 * 1024 * 1024

MESH = pl.DeviceIdType.MESH
ANY = pl.BlockSpec(memory_space=pl.ANY)


def _tile(dim, target, align=LANE):
    if dim <= target:
        return dim
    t = (target // align) * align
    while t >= align:
        if dim % t == 0:
            return t
        t -= align
    return dim


def _params(*sem):
    return pltpu.CompilerParams(dimension_semantics=sem, vmem_limit_bytes=VMEM_LIMIT)


def _rstd(x):
    return lax.rsqrt(jnp.mean(x * x, axis=-1, keepdims=True) + NORM_EPS)


def _mm(a, b, *, name, ta=False, tb=False, out_dtype=F32, res=None, alpha=1.0, tm=512, tn=1024, tk=1024, b_row0=0,
        norm_gain=None, dep=None):
    if ta:
        K, M = a.shape
    else:
        M, K = a.shape
    if tb:
        N, K2 = b.shape
    else:
        K2, N = b.shape
    assert K == K2 or (not tb and K2 >= b_row0 + K), (a.shape, b.shape, ta, tb)
    tm, tn, tk = _tile(M, tm), _tile(N, tn), _tile(K, tk)
    nk = K // tk
    assert b_row0 % tk == 0
    kb0 = b_row0 // tk
    dims = (((0 if ta else 1,), (1 if tb else 0,)), ((), ()))
    has_res = res is not None
    has_norm = norm_gain is not None
    assert not has_norm or tn == N
    n_in = 2 + has_res + has_norm + (dep is not None)

    def body(*refs):
        a_ref, b_ref = refs[0], refs[1]
        res_ref = refs[2] if has_res else None
        gain_ref = refs[2 + has_res] if has_norm else None
        o_ref = refs[n_in]
        h_ref = refs[n_in + 1] if has_norm else None
        acc_ref = refs[n_in + 1 + has_norm] if nk > 1 else None
        part = lax.dot_general(a_ref[...].astype(BF16), b_ref[...].astype(BF16), dims,
                               preferred_element_type=F32)

        def finish(acc):
            r = acc * alpha if alpha != 1.0 else acc
            if has_res:
                r = res_ref[...].astype(F32) + r
            o_ref[...] = r.astype(out_dtype)
            if has_norm:
                h_ref[...] = (r * _rstd(r) * gain_ref[...]).astype(BF16)

        if nk == 1:
            finish(part)
        else:
            k = pl.program_id(2)

            @pl.when(k == 0)
            def _():
                acc_ref[...] = part

            @pl.when(k > 0)
            def _():
                acc_ref[...] += part

            @pl.when(k == nk - 1)
            def _():
                finish(acc_ref[...])

    a_spec = pl.BlockSpec((tk, tm), lambda i, j, k: (k, i)) if ta else pl.BlockSpec((tm, tk), lambda i, j, k: (i, k))
    b_spec = (pl.BlockSpec((tn, tk), lambda i, j, k: (j, k)) if tb
              else pl.BlockSpec((tk, tn), lambda i, j, k: (k + kb0, j)))
    in_specs = [a_spec, b_spec]
    operands = [a, b]
    tile_spec = pl.BlockSpec((tm, tn), lambda i, j, k: (i, j))
    if has_res:
        in_specs.append(tile_spec)
        operands.append(res)
    if has_norm:
        in_specs.append(pl.BlockSpec((1, tn), lambda i, j, k: (0, j)))
        operands.append(norm_gain.reshape(1, N))
    if dep is not None:
        in_specs += _dep_spec(dep)
        operands.append(dep)
    out = pl.pallas_call(
        body, name=name,
        out_shape=(jax.ShapeDtypeStruct((M, N), out_dtype),) + ((jax.ShapeDtypeStruct((M, N), BF16),) if has_norm else ()),
        grid=(M // tm, N // tn, nk),
        in_specs=in_specs,
        out_specs=(tile_spec,) + ((tile_spec,) if has_norm else ()),
        scratch_shapes=[pltpu.VMEM((tm, tn), F32)] if nk > 1 else [],
        compiler_params=_params("parallel", "parallel", "arbitrary"),
    )(*operands)
    return out if has_norm else out[0]


def _rms_fwd(x, g, *, name):
    T, D = x.shape
    tm = _tile(T, 512, 16)

    def body(x_ref, g_ref, h_ref):
        x = x_ref[...]
        h_ref[...] = (x * _rstd(x) * g_ref[...]).astype(BF16)

    return pl.pallas_call(
        body, name=name,
        out_shape=jax.ShapeDtypeStruct((T, D), BF16),
        grid=(T // tm,),
        in_specs=[pl.BlockSpec((tm, D), lambda i: (i, 0)), pl.BlockSpec((1, D), lambda i: (0, 0))],
        out_specs=pl.BlockSpec((tm, D), lambda i: (i, 0)),
        compiler_params=_params("parallel"),
    )(x, g.reshape(1, D))


def _dh_norm_bwd(a1, b1, a2, b2, b2_row0, x, g, dxo, *, name):
    T, D = x.shape
    K1, K2 = a1.shape[1], a2.shape[1]
    assert b2_row0 % K2 == 0 and b1.shape[0] >= K1 and b2.shape[0] >= b2_row0 + K2
    tm = _tile(T, 256, 16)

    def body(a1_ref, b1_ref, a2_ref, b2_ref, x_ref, g_ref, dxo_ref, dx_ref, dg_ref):
        x = x_ref[...]
        r = _rstd(x)
        xhat = x * r
        dh = (jnp.dot(a1_ref[...], b1_ref[...], preferred_element_type=F32)
              + jnp.dot(a2_ref[...], b2_ref[...], preferred_element_type=F32))
        dxh = dh * g_ref[...]
        dx_ref[...] = dxo_ref[...] + r * (dxh - xhat * jnp.mean(dxh * xhat, axis=-1, keepdims=True))
        part = jnp.sum(dh * xhat, axis=0, keepdims=True)

        @pl.when(pl.program_id(0) == 0)
        def _():
            dg_ref[...] = part

        @pl.when(pl.program_id(0) > 0)
        def _():
            dg_ref[...] += part

    row = pl.BlockSpec((tm, D), lambda i: (i, 0))
    vec = pl.BlockSpec((1, D), lambda i: (0, 0))
    return pl.pallas_call(
        body, name=name,
        out_shape=(jax.ShapeDtypeStruct((T, D), F32), jax.ShapeDtypeStruct((1, D), F32)),
        grid=(T // tm,),
        in_specs=[pl.BlockSpec((tm, K1), lambda i: (i, 0)), pl.BlockSpec((K1, D), lambda i: (0, 0)),
                  pl.BlockSpec((tm, K2), lambda i: (i, 0)), pl.BlockSpec((K2, D), lambda i: (b2_row0 // K2, 0)),
                  row, vec, row],
        out_specs=(row, vec),
        compiler_params=_params("arbitrary"),
    )(a1, b1, a2, b2, x, g.reshape(1, D), dxo)


def _loss_head(x, g, target, *, name):
    T, D = x.shape
    tm = _tile(T, 512, 16)

    def body(x_ref, g_ref, t_ref, dx_ref, dg_ref, loss_ref):
        x = x_ref[...]
        gain = g_ref[...]
        r = _rstd(x)
        xhat = x * r
        err = xhat * gain - t_ref[...]
        dy = err * (1.0 / D)
        dxh = dy * gain
        dx_ref[...] = r * (dxh - xhat * jnp.mean(dxh * xhat, axis=-1, keepdims=True))
        dg_part = jnp.sum(dy * xhat, axis=0, keepdims=True)
        loss_part = jnp.full((1, LANE), 0.5 / D, F32) * jnp.sum(err * err)

        @pl.when(pl.program_id(0) == 0)
        def _():
            dg_ref[...] = dg_part
            loss_ref[...] = loss_part

        @pl.when(pl.program_id(0) > 0)
        def _():
            dg_ref[...] += dg_part
            loss_ref[...] += loss_part

    row = pl.BlockSpec((tm, D), lambda i: (i, 0))
    vec = pl.BlockSpec((1, D), lambda i: (0, 0))
    return pl.pallas_call(
        body, name=name,
        out_shape=(jax.ShapeDtypeStruct((T, D), F32), jax.ShapeDtypeStruct((1, D), F32),
                   jax.ShapeDtypeStruct((1, LANE), F32)),
        grid=(T // tm,),
        in_specs=[row, vec, row],
        out_specs=(row, vec, pl.BlockSpec((1, LANE), lambda i: (0, 0))),
        compiler_params=_params("arbitrary"),
    )(x, g.reshape(1, D), target)


def _ffn_up(h, w_up_t, *, name, dep=None):
    T, D = h.shape
    F = w_up_t.shape[0] // 2
    tm, tn = _tile(T, 512, 16), _tile(F, 1408)
    nj = F // tn

    def body(h_ref, wg_ref, wu_ref, *rest):
        gate_ref, up_ref, a_ref = rest[-3:]
        h = h_ref[...]
        gate = lax.dot_general(h, wg_ref[...], _NT, preferred_element_type=F32)
        up = lax.dot_general(h, wu_ref[...], _NT, preferred_element_type=F32)
        gate_ref[...] = gate.astype(BF16)
        up_ref[...] = up.astype(BF16)
        a_ref[...] = (gate * jax.nn.sigmoid(gate) * up).astype(BF16)

    o_spec = pl.BlockSpec((tm, tn), lambda j, i: (i, j))
    out = jax.ShapeDtypeStruct((T, F), BF16)
    return pl.pallas_call(
        body, name=name,
        out_shape=(out, out, out),
        grid=(nj, T // tm),
        in_specs=[pl.BlockSpec((tm, D), lambda j, i: (i, 0)),
                  pl.BlockSpec((tn, D), lambda j, i: (j, 0)),
                  pl.BlockSpec((tn, D), lambda j, i: (j + nj, 0))] + _dep_spec(dep),
        out_specs=(o_spec, o_spec, o_spec),
        compiler_params=_params("parallel", "parallel"),
    )(h, w_up_t, w_up_t, *([] if dep is None else [dep]))


def _dep_spec(dep):
    return [] if dep is None else [pl.BlockSpec(dep.shape, lambda *_: (0,) * dep.ndim)]


def _ffn_bwd_act(dxo, wd, gate, up, *, alpha, name, dep=None):
    T, D = dxo.shape
    F = wd.shape[0]
    tm, tn = _tile(T, 512, 16), _tile(F, 1408)

    def body(dxo_ref, wd_ref, gate_ref, up_ref, *rest):
        dgate_ref, dup_ref = rest[-2:]
        da = lax.dot_general(dxo_ref[...].astype(BF16), wd_ref[...], (((1,), (1,)), ((), ())),
                             preferred_element_type=F32) * alpha
        gate = gate_ref[...].astype(F32)
        up = up_ref[...].astype(F32)
        sig = jax.nn.sigmoid(gate)
        dgate_ref[...] = (da * up * (sig * (1.0 + gate * (1.0 - sig)))).astype(BF16)
        dup_ref[...] = (da * (gate * sig)).astype(BF16)

    t_spec = pl.BlockSpec((tm, tn), lambda j, i: (i, j))
    out = jax.ShapeDtypeStruct((T, F), BF16)
    return pl.pallas_call(
        body, name=name,
        out_shape=(out, out),
        grid=(F // tn, T // tm),
        in_specs=[pl.BlockSpec((tm, D), lambda j, i: (i, 0)), pl.BlockSpec((tn, D), lambda j, i: (j, 0)),
                  t_spec, t_spec] + _dep_spec(dep),
        out_specs=(t_spec, t_spec),
        compiler_params=_params("parallel", "parallel"),
    )(dxo, wd, gate, up, *([] if dep is None else [dep]))


def _rope_tables(positions):
    half = QK_ROPE // 2
    inv_freq = ROPE_THETA ** (-jnp.arange(0, QK_ROPE, 2, dtype=F32) / QK_ROPE)
    ang = positions.astype(F32)[:, None] * inv_freq
    cos, sin = jnp.cos(ang), jnp.sin(ang)
    z = jnp.zeros_like(cos)
    zz = jnp.zeros((positions.shape[0], LANE - QK_ROPE), F32)
    c = jnp.concatenate([cos, cos, zz], axis=1)
    sa = jnp.concatenate([z, sin, zz], axis=1)
    sb = jnp.concatenate([-sin, z, zz], axis=1)
    return c, sa, sb


def _rotate(seg, c, sa, sb, sign):
    half = QK_ROPE // 2
    mix = pltpu.roll(seg, half, 1) * sa + pltpu.roll(seg, LANE - half, 1) * sb
    return seg * c + mix if sign > 0 else seg * c - mix


def _mixer_in(h, wa, wuq, wukv, gq, gkv, tabs, *, name):
    T, D = h.shape
    HQ, QL = wuq.shape
    KVL = wukv.shape[0]
    H = HQ // HEAD_W
    o_q, o_kv, o_kr = POOL_DIM, POOL_DIM + QL, POOL_DIM + QL + KVL
    PA = o_kr + LANE
    assert wa.shape[0] >= PA
    tm = _tile(T, 256, 16)

    def body(h_ref, wa_ref, wuq_ref, wukv_ref, gq_ref, gkv_ref, c_ref, sa_ref, sb_ref,
             xp_ref, ql_ref, kvl_ref, qn_ref, kvn_ref, q_ref, kv_ref, kr_ref):
        proj = lax.dot_general(h_ref[...], wa_ref[...], _NT, preferred_element_type=F32)
        xp_ref[...] = proj[:, :POOL_DIM]
        ql = proj[:, o_q:o_kv]
        kvl = proj[:, o_kv:o_kr]
        ql_ref[...] = ql
        kvl_ref[...] = kvl
        qn = (ql * _rstd(ql) * gq_ref[...]).astype(BF16)
        kvn = (kvl * _rstd(kvl) * gkv_ref[...]).astype(BF16)
        qn_ref[...] = qn
        kvn_ref[...] = kvn
        c, sa, sb = c_ref[...], sa_ref[...], sb_ref[...]
        q = lax.dot_general(qn, wuq_ref[...], _NT, preferred_element_type=F32)
        for hh in range(H):
            base = hh * HEAD_W
            q_ref[:, base:base + QK_NOPE] = q[:, base:base + QK_NOPE].astype(BF16)
            q_ref[:, base + QK_NOPE:base + HEAD_W] = _rotate(
                q[:, base + QK_NOPE:base + HEAD_W], c, sa, sb, 1).astype(BF16)
        kv_ref[...] = jnp.dot(kvn, wukv_ref[...], preferred_element_type=F32).astype(BF16)
        kr_ref[...] = _rotate(proj[:, o_kr:o_kr + LANE], c, sa, sb, 1).astype(BF16)

    def row(w):
        return pl.BlockSpec((tm, w), lambda i: (i, 0))

    def whole(arr):
        return pl.BlockSpec(arr.shape, lambda i: (0,) * arr.ndim)

    gq2, gkv2 = gq.reshape(1, QL), gkv.reshape(1, KVL)
    outs = [(POOL_DIM, F32), (QL, F32), (KVL, F32), (QL, BF16), (KVL, BF16), (HQ, BF16), (HQ, BF16), (LANE, BF16)]
    return pl.pallas_call(
        body, name=name,
        out_shape=tuple(jax.ShapeDtypeStruct((T, w), dt) for w, dt in outs),
        grid=(T // tm,),
        in_specs=[row(D), pl.BlockSpec((PA, D), lambda i: (0, 0)), whole(wuq), whole(wukv), whole(gq2), whole(gkv2),
                  row(LANE), row(LANE), row(LANE)],
        out_specs=tuple(row(w) for w, _ in outs),
        compiler_params=_params("parallel"),
    )(h, wa, wuq, wukv, gq2, gkv2, *tabs)


def _mixer_in_bwd(dq, dkv, dkr, ql, kvl, dxp, wuq, wukv, gq, gkv, tabs, *, name):
    T, HQ = dq.shape
    QL, KVL = wuq.shape[1], wukv.shape[0]
    H = HQ // HEAD_W
    PA = POOL_DIM + QL + KVL + LANE
    o_q, o_kv, o_kr = POOL_DIM, POOL_DIM + QL, POOL_DIM + QL + KVL
    tm = _tile(T, 256, 16)

    def norm_bwd(lat, gain, dn):
        r = _rstd(lat)
        xhat = lat * r
        dxh = dn * gain
        dlat = r * (dxh - xhat * jnp.mean(dxh * xhat, axis=-1, keepdims=True))
        return dlat, jnp.sum(dn * xhat, axis=0, keepdims=True)

    def body(dq_ref, dkv_ref, dkr_ref, ql_ref, kvl_ref, dxp_ref, wuq_ref, wukv_ref, gq_ref, gkv_ref,
             c_ref, sa_ref, sb_ref, dproj_ref, dqp_ref, dgq_ref, dgkv_ref):
        c, sa, sb = c_ref[...], sa_ref[...], sb_ref[...]
        dkr_sum = dkr_ref[:, :LANE]
        for hh in range(H):
            base = hh * HEAD_W
            dqp_ref[:, base:base + QK_NOPE] = dq_ref[:, base:base + QK_NOPE]
            dqp_ref[:, base + QK_NOPE:base + HEAD_W] = _rotate(
                dq_ref[:, base + QK_NOPE:base + HEAD_W].astype(F32), c, sa, sb, -1).astype(BF16)
            if hh:
                dkr_sum = dkr_sum + dkr_ref[:, hh * LANE:(hh + 1) * LANE]
        dqn = jnp.dot(dqp_ref[...], wuq_ref[...], preferred_element_type=F32)
        dkvn = lax.dot_general(dkv_ref[...], wukv_ref[...], _NT, preferred_element_type=F32)
        dql, dgq = norm_bwd(ql_ref[...], gq_ref[...], dqn)
        dkvl, dgkv = norm_bwd(kvl_ref[...], gkv_ref[...], dkvn)
        dproj_ref[:, :POOL_DIM] = dxp_ref[...].astype(BF16)
        dproj_ref[:, o_q:o_kv] = dql.astype(BF16)
        dproj_ref[:, o_kv:o_kr] = dkvl.astype(BF16)
        dproj_ref[:, o_kr:PA] = _rotate(dkr_sum, c, sa, sb, -1).astype(BF16)

        @pl.when(pl.program_id(0) == 0)
        def _():
            dgq_ref[...] = dgq
            dgkv_ref[...] = dgkv

        @pl.when(pl.program_id(0) > 0)
        def _():
            dgq_ref[...] += dgq
            dgkv_ref[...] += dgkv

    def row(w):
        return pl.BlockSpec((tm, w), lambda i: (i, 0))

    def whole(arr):
        return pl.BlockSpec(arr.shape, lambda i: (0,) * arr.ndim)

    gq2, gkv2 = gq.reshape(1, QL), gkv.reshape(1, KVL)
    return pl.pallas_call(
        body, name=name,
        out_shape=(jax.ShapeDtypeStruct((T, PA), BF16), jax.ShapeDtypeStruct((T, HQ), BF16),
                   jax.ShapeDtypeStruct((1, QL), F32), jax.ShapeDtypeStruct((1, KVL), F32)),
        grid=(T // tm,),
        in_specs=[row(HQ), row(HQ), row(H * LANE), row(QL), row(KVL), row(POOL_DIM), whole(wuq), whole(wukv),
                  whole(gq2), whole(gkv2), row(LANE), row(LANE), row(LANE)],
        out_specs=(row(PA), row(HQ), whole(gq2), whole(gkv2)),
        compiler_params=_params("arbitrary"),
    )(dq, dkv, dkr, ql, kvl, dxp, wuq, wukv, gq2, gkv2, *tabs)


def _pool_groups(x_of, S):
    row = lax.broadcasted_iota(jnp.int32, (S, POOL_G), 0)
    for g, w in enumerate(POOL_WINDOWS):
        x = x_of(g)
        s = x
        d = 1
        while d < w:
            s = s + jnp.where(row >= d, pltpu.roll(s, d, 0), 0.0)
            d *= 2
        cnt = jnp.minimum(row + 1, w).astype(F32)
        yield g, w, x, s / cnt - x, cnt, row


def _pool_fwd(xp, maps, scale, *, S, name):
    T = xp.shape[0]

    def body(xp_ref, maps_ref, scale_ref, ms_ref):
        for g, _, _, pooled, _, _ in _pool_groups(lambda g: xp_ref[:, g * POOL_G:(g + 1) * POOL_G], S):
            mixed = jnp.dot(pooled.astype(BF16), maps_ref[g].astype(BF16), preferred_element_type=F32)
            ms_ref[:, g * POOL_G:(g + 1) * POOL_G] = (mixed * scale_ref[:, g * POOL_G:(g + 1) * POOL_G]).astype(BF16)

    return pl.pallas_call(
        body, name=name,
        out_shape=jax.ShapeDtypeStruct((T, POOL_DIM), BF16),
        grid=(T // S,),
        in_specs=[pl.BlockSpec((S, POOL_DIM), lambda b: (b, 0)),
                  pl.BlockSpec(maps.shape, lambda b: (0, 0, 0)),
                  pl.BlockSpec((1, POOL_DIM), lambda b: (0, 0))],
        out_specs=pl.BlockSpec((S, POOL_DIM), lambda b: (b, 0)),
        compiler_params=_params("parallel"),
    )(xp, maps, scale.reshape(1, POOL_DIM))


def _pool_bwd(xp, dms, maps, scale, *, S, name):
    T = xp.shape[0]

    def body(xp_ref, dms_ref, maps_ref, scale_ref, dxp_ref, dmaps_ref, dscale_ref):
        first = pl.program_id(0) == 0
        for g, w, _, pooled, cnt, row in _pool_groups(lambda g: xp_ref[:, g * POOL_G:(g + 1) * POOL_G], S):
            cols = slice(g * POOL_G, (g + 1) * POOL_G)
            pooled_b = pooled.astype(BF16)
            maps_b = maps_ref[g].astype(BF16)
            mixed = jnp.dot(pooled_b, maps_b, preferred_element_type=F32)
            dms = dms_ref[:, cols]
            dscale = jnp.sum(dms * mixed, axis=0, keepdims=True)
            dmixed = (dms * scale_ref[:, cols]).astype(BF16)
            dmaps = lax.dot_general(pooled_b, dmixed, (((0,), (0,)), ((), ())), preferred_element_type=F32)
            dpooled = lax.dot_general(dmixed, maps_b, (((1,), (1,)), ((), ())), preferred_element_type=F32)
            z = dpooled / cnt
            d = 1
            while d < w:
                z = z + jnp.where(row < S - d, pltpu.roll(z, S - d, 0), 0.0)
                d *= 2
            dxp_ref[:, cols] = z - dpooled

            @pl.when(first)
            def _():
                dmaps_ref[g] = dmaps
                dscale_ref[:, cols] = dscale

            @pl.when(jnp.logical_not(first))
            def _():
                dmaps_ref[g] += dmaps
                dscale_ref[:, cols] += dscale

    seq = pl.BlockSpec((S, POOL_DIM), lambda b: (b, 0))
    maps_spec = pl.BlockSpec(maps.shape, lambda b: (0, 0, 0))
    vec = pl.BlockSpec((1, POOL_DIM), lambda b: (0, 0))
    return pl.pallas_call(
        body, name=name,
        out_shape=(jax.ShapeDtypeStruct((T, POOL_DIM), F32), jax.ShapeDtypeStruct(maps.shape, F32),
                   jax.ShapeDtypeStruct((1, POOL_DIM), F32)),
        grid=(T // S,),
        in_specs=[seq, seq, maps_spec, vec],
        out_specs=(seq, maps_spec, vec),
        compiler_params=_params("arbitrary"),
    )(xp, dms, maps, scale.reshape(1, POOL_DIM))


def _causal_mask(s, t):
    r = lax.broadcasted_iota(jnp.int32, (t, t), 0)
    c = lax.broadcasted_iota(jnp.int32, (t, t), 1)
    return jnp.where(r >= c, s, MASK_VALUE)


_NT = (((1,), (1,)), ((), ()))
_TN = (((0,), (0,)), ((), ()))


def _attn_fwd(q, kv, kr, *, S, name):
    T, HQ = q.shape
    H = HQ // HEAD_W
    B = T // S
    t = _tile(S, ATTN_TILE)
    n = S // t

    def body(q_ref, k_ref, v_ref, kr_ref, o_ref, lse_ref, kcat):
        kcat[:, :QK_NOPE] = k_ref[...]
        kcat[:, QK_NOPE:] = kr_ref[...]
        for i in range(n):
            rows = slice(i * t, (i + 1) * t)
            qt = q_ref[rows, :]
            m = jnp.full((t, 1), MASK_VALUE, F32)
            l = jnp.zeros((t, 1), F32)
            acc = jnp.zeros((t, V_DIM), F32)
            for j in range(i + 1):
                cols = slice(j * t, (j + 1) * t)
                s = lax.dot_general(qt, kcat[cols, :], _NT, preferred_element_type=F32) * ATTN_SCALE
                if j == i:
                    s = _causal_mask(s, t)
                m_new = jnp.maximum(m, jnp.max(s, axis=1, keepdims=True))
                p = jnp.exp(s - m_new)
                corr = jnp.exp(m - m_new)
                l = corr * l + jnp.sum(p, axis=1, keepdims=True)
                acc = corr * acc + jnp.dot(p.astype(BF16), v_ref[cols, :], preferred_element_type=F32)
                m = m_new
            o_ref[rows, :] = (acc / l).astype(BF16)
            lse_ref[rows, :] = jnp.broadcast_to(m + jnp.log(l), (t, LANE))

    seq_h = pl.BlockSpec((S, LANE), lambda b, h: (b, h))
    return pl.pallas_call(
        body, name=name,
        out_shape=(jax.ShapeDtypeStruct((T, H * V_DIM), BF16), jax.ShapeDtypeStruct((T, H * LANE), F32)),
        grid=(B, H),
        in_specs=[pl.BlockSpec((S, HEAD_W), lambda b, h: (b, h)),
                  pl.BlockSpec((S, QK_NOPE), lambda b, h: (b, 2 * h)),
                  pl.BlockSpec((S, V_DIM), lambda b, h: (b, 2 * h + 1)),
                  pl.BlockSpec((S, LANE), lambda b, h: (b, 0))],
        out_specs=(seq_h, seq_h),
        scratch_shapes=[pltpu.VMEM((S, HEAD_W), BF16)],
        compiler_params=_params("parallel", "parallel"),
    )(q, kv, kv, kr)


def _attn_bwd(q, kv, kr, o, do, lse, *, S, name):
    T, HQ = q.shape
    H = HQ // HEAD_W
    B = T // S
    t = _tile(S, ATTN_TILE)
    n = S // t

    def body(q_ref, k_ref, v_ref, kr_ref, o_ref, do_ref, lse_ref, dq_ref, dkv_ref, dkr_ref, kcat, dq_acc):
        kcat[:, :QK_NOPE] = k_ref[...]
        kcat[:, QK_NOPE:] = kr_ref[...]
        delta = [jnp.sum(do_ref[i * t:(i + 1) * t, :].astype(F32) * o_ref[i * t:(i + 1) * t, :].astype(F32),
                         axis=1, keepdims=True) for i in range(n)]
        for j in range(n):
            cols = slice(j * t, (j + 1) * t)
            kc = kcat[cols, :]
            vt = v_ref[cols, :]
            dk = jnp.zeros((t, HEAD_W), F32)
            dv = jnp.zeros((t, V_DIM), F32)
            for i in range(j, n):
                rows = slice(i * t, (i + 1) * t)
                qt = q_ref[rows, :]
                dot_ = do_ref[rows, :]
                s = lax.dot_general(qt, kc, _NT, preferred_element_type=F32) * ATTN_SCALE
                if i == j:
                    s = _causal_mask(s, t)
                p = jnp.exp(s - lse_ref[rows, :][:, :1])
                dv = dv + lax.dot_general(p.astype(BF16), dot_, _TN, preferred_element_type=F32)
                dp = lax.dot_general(dot_, vt, _NT, preferred_element_type=F32)
                ds = (p * (dp - delta[i]) * ATTN_SCALE).astype(BF16)
                dk = dk + lax.dot_general(ds, qt, _TN, preferred_element_type=F32)
                dq_part = jnp.dot(ds, kc, preferred_element_type=F32)
                if j == 0:
                    dq_acc[rows, :] = dq_part
                else:
                    dq_acc[rows, :] += dq_part
            dkv_ref[cols, :QK_NOPE] = dk[:, :QK_NOPE].astype(BF16)
            dkv_ref[cols, QK_NOPE:] = dv.astype(BF16)
            dkr_ref[cols, :] = dk[:, QK_NOPE:]
        dq_ref[...] = dq_acc[...].astype(BF16)

    seq_q = pl.BlockSpec((S, HEAD_W), lambda b, h: (b, h))
    seq_h = pl.BlockSpec((S, LANE), lambda b, h: (b, h))
    return pl.pallas_call(
        body, name=name,
        out_shape=(jax.ShapeDtypeStruct((T, HQ), BF16), jax.ShapeDtypeStruct((T, HQ), BF16),
                   jax.ShapeDtypeStruct((T, H * LANE), F32)),
        grid=(B, H),
        in_specs=[seq_q,
                  pl.BlockSpec((S, QK_NOPE), lambda b, h: (b, 2 * h)),
                  pl.BlockSpec((S, V_DIM), lambda b, h: (b, 2 * h + 1)),
                  pl.BlockSpec((S, LANE), lambda b, h: (b, 0)),
                  seq_h, seq_h, seq_h],
        out_specs=(seq_q, seq_q, seq_h),
        scratch_shapes=[pltpu.VMEM((S, HEAD_W), BF16), pltpu.VMEM((S, HEAD_W), F32)],
        compiler_params=_params("parallel", "parallel"),
    )(q, kv, kv, kr, o, do, lse)


def _merge_out(h, ms, o, x, wgate, bgate, wpp, wap, wout, next_gain, *, name):
    T, D = x.shape
    tm = _tile(T, 256, 16)

    def body(h_ref, ms_ref, o_ref, x_ref, wgate_ref, bgate_ref, wpp_ref, wap_ref, wout_ref, ng_ref,
             gates_ref, ba_ref, bb_ref, merged_ref, xn_ref, hn_ref):
        logits = lax.dot_general(h_ref[...], wgate_ref[...], _NT, preferred_element_type=F32) + bgate_ref[...]
        gates = jax.nn.sigmoid(logits)
        ba = jnp.dot(ms_ref[...], wpp_ref[...], preferred_element_type=F32)
        bb = jnp.dot(o_ref[...], wap_ref[...], preferred_element_type=F32)
        merged = (gates[:, :D] * ba + gates[:, D:] * bb).astype(BF16)
        gates_ref[...] = gates.astype(BF16)
        ba_ref[...] = ba.astype(BF16)
        bb_ref[...] = bb.astype(BF16)
        merged_ref[...] = merged
        xn = x_ref[...] + jnp.dot(merged, wout_ref[...], preferred_element_type=F32)
        xn_ref[...] = xn
        hn_ref[...] = (xn * _rstd(xn) * ng_ref[...]).astype(BF16)

    def row(w):
        return pl.BlockSpec((tm, w), lambda i: (i, 0))

    def whole(arr):
        return pl.BlockSpec(arr.shape, lambda i: (0,) * arr.ndim)

    bg2, ng2 = bgate.reshape(1, 2 * D), next_gain.reshape(1, D)
    act = jax.ShapeDtypeStruct((T, D), BF16)
    return pl.pallas_call(
        body, name=name,
        out_shape=(jax.ShapeDtypeStruct((T, 2 * D), BF16), act, act, act, jax.ShapeDtypeStruct((T, D), F32), act),
        grid=(T // tm,),
        in_specs=[row(D), row(ms.shape[1]), row(o.shape[1]), row(D), whole(wgate), whole(bg2), whole(wpp),
                  whole(wap), whole(wout), whole(ng2)],
        out_specs=(row(2 * D), row(D), row(D), row(D), row(D), row(D)),
        compiler_params=_params("parallel"),
    )(h, ms, o, x, wgate, bg2, wpp, wap, wout, ng2)


def _merge_bwd(dxo, wout, gates, ba, bb, *, name, dep=None):
    T, D = dxo.shape
    tm = _tile(T, 256, 16)

    def body(dxo_ref, wout_ref, gates_ref, ba_ref, bb_ref, *rest):
        dba_ref, dbb_ref, dgl_ref, dbg_ref = rest[-4:]
        dm = lax.dot_general(dxo_ref[...].astype(BF16), wout_ref[...], _NT, preferred_element_type=F32)
        ga = gates_ref[:, :D].astype(F32)
        gb = gates_ref[:, D:].astype(F32)
        dba_ref[...] = (dm * ga).astype(BF16)
        dbb_ref[...] = (dm * gb).astype(BF16)
        dgl_a = dm * ba_ref[...].astype(F32) * (ga * (1.0 - ga))
        dgl_b = dm * bb_ref[...].astype(F32) * (gb * (1.0 - gb))
        dgl_ref[:, :D] = dgl_a.astype(BF16)
        dgl_ref[:, D:] = dgl_b.astype(BF16)
        sa = jnp.sum(dgl_a, axis=0, keepdims=True)
        sb = jnp.sum(dgl_b, axis=0, keepdims=True)

        @pl.when(pl.program_id(0) == 0)
        def _():
            dbg_ref[:, :D] = sa
            dbg_ref[:, D:] = sb

        @pl.when(pl.program_id(0) > 0)
        def _():
            dbg_ref[:, :D] += sa
            dbg_ref[:, D:] += sb

    def row(w):
        return pl.BlockSpec((tm, w), lambda i: (i, 0))

    act = jax.ShapeDtypeStruct((T, D), BF16)
    return pl.pallas_call(
        body, name=name,
        out_shape=(act, act, jax.ShapeDtypeStruct((T, 2 * D), BF16), jax.ShapeDtypeStruct((1, 2 * D), F32)),
        grid=(T // tm,),
        in_specs=[row(D), pl.BlockSpec(wout.shape, lambda i: (0, 0)), row(2 * D), row(D), row(D)] + _dep_spec(dep),
        out_specs=(row(D), row(D), row(2 * D), pl.BlockSpec((1, 2 * D), lambda i: (0, 0))),
        compiler_params=_params("arbitrary"),
    )(dxo, wout, gates, ba, bb, *([] if dep is None else [dep]))


def _ffn_fwd(x, h, w, tag, next_gain, dep=None):
    gate, up, a = _ffn_up(h, w["up_t"], name=f"{tag}_up", dep=dep)
    if next_gain is None:
        xn, hn = _mm(a, w["wd"], res=x, alpha=0.5, name=f"{tag}_down_last", tk=2816), None
    else:
        xn, hn = _mm(a, w["wd"], res=x, alpha=0.5, norm_gain=next_gain, name=f"{tag}_down", tk=2816)
    return xn, hn, (x, h, gate, up, a)


def _ffn_bwd(dxo, gain, w, saved, tag, dep=None):
    x, h, gate, up, a = saved
    F = gate.shape[1]
    dgate, dup = _ffn_bwd_act(dxo, w["wd"], gate, up, alpha=0.5, name=f"{tag}_bwd_act", dep=dep)
    dwd = _mm(a, dxo, ta=True, alpha=0.5, out_dtype=BF16, name=f"{tag}_dwd", tm=1408, tn=1024, tk=1024)
    dwg_t = _mm(dgate, h, ta=True, out_dtype=BF16, name=f"{tag}_dwg", tm=1408, tn=1024, tk=2048)
    dwu_t = _mm(dup, h, ta=True, out_dtype=BF16, name=f"{tag}_dwu", tm=1408, tn=1024, tk=2048)
    dx, dgain = _dh_norm_bwd(dgate, w["up_t"], dup, w["up_t"], F, x, gain, dxo, name=f"{tag}_dh_norm_bwd")
    return dx, dgain, jnp.concatenate([dwg_t, dwu_t], axis=0), dwd


def _mixer_fwd(x, h, p, w, tabs, S, next_gain):
    xp, ql, kvl, qn, kvn, q, kv, kr = _mixer_in(h, w["win_t"], w["wuq_t"], w["wukv"], p["q_latent_norm"],
                                                 p["kv_latent_norm"], tabs, name="mix_in")
    ms = _pool_fwd(xp, p["pool_maps"], p["pool_scale"], S=S, name="pool_fwd")
    o, lse = _attn_fwd(q, kv, kr, S=S, name="attn_fwd")
    gates, ba, bb, merged, xn, hn = _merge_out(h, ms, o, x, w["wgate_t"], p["b_gate"], w["wpp"], w["wap"], w["wout"],
                                               next_gain, name="merge_out")
    return xn, hn, (x, h, xp, ql, kvl, qn, kvn, q, kv, kr, ms, o, lse, gates, ba, bb, merged)


def _mixer_bwd(dxo, p, w, tabs, saved, S, dep=None):
    x, h, xp, ql, kvl, qn, kvn, q, kv, kr, ms, o, lse, gates, ba, bb, merged = saved
    dba, dbb, dgl, dbg = _merge_bwd(dxo, w["wout"], gates, ba, bb, name="merge_bwd", dep=dep)
    g = {}
    g["wout"] = _mm(merged, dxo, ta=True, out_dtype=BF16, name="d_wout", tm=1024, tk=1024)
    g["wpp"] = _mm(ms, dba, ta=True, out_dtype=BF16, name="d_wpp", tk=2048)
    g["wap"] = _mm(o, dbb, ta=True, out_dtype=BF16, name="d_wap", tm=1024, tk=2048)
    dms = _mm(dba, w["wpp"], tb=True, name="d_ms")
    do = _mm(dbb, w["wap"], tb=True, out_dtype=BF16, name="d_o")
    dxp, g["pool_maps"], g["pool_scale"] = _pool_bwd(xp, dms, p["pool_maps"], p["pool_scale"], S=S, name="pool_bwd")
    dq, dkv, dkr = _attn_bwd(q, kv, kr, o, do, lse, S=S, name="attn_bwd")
    dproj, dqp, g["q_latent_norm"], g["kv_latent_norm"] = _mixer_in_bwd(
        dq, dkv, dkr, ql, kvl, dxp, w["wuq_t"], w["wukv"], p["q_latent_norm"], p["kv_latent_norm"], tabs,
        name="mix_in_bwd")
    g["wuq_t"] = _mm(dqp, qn, ta=True, out_dtype=BF16, name="d_wuq", tm=2048, tk=2048)
    g["wukv"] = _mm(kvn, dkv, ta=True, out_dtype=BF16, name="d_wukv", tn=2048, tk=2048)
    g["wa_t"] = _mm(dproj, h, ta=True, out_dtype=BF16, name="d_wa", tm=1280, tn=1024, tk=2048)
    g["wgate_t"] = _mm(dgl, h, ta=True, out_dtype=BF16, name="d_wgate", tm=2048, tn=1024, tk=1024)
    dx, g["norm_mix"] = _dh_norm_bwd(dproj, w["win_t"], dgl, w["wgate_t"], 0, x, p["norm_mix"], dxo,
                                     name="mix_dh_norm_bwd")
    g["b_gate"] = dbg
    return dx, g


BIG = ("ffn1_up", "ffn1_down", "w_in", "w_pool_proj", "w_uq", "w_ukv", "w_attn_proj", "w_out", "ffn2_up", "ffn2_down")
SMALL = ("norm_ffn1", "norm_mix", "b_gate", "pool_maps", "pool_scale", "q_latent_norm", "kv_latent_norm", "norm_ffn2")
PACKED = ("w_pool_proj", "w_uq", "w_ukv")
DIRECT = tuple(n for n in BIG if n not in PACKED)
TRANSPOSED = ("ffn1_up", "ffn2_up", "w_in", "w_uq")
COL_SHARDED = ("w_pool_proj", "w_ukv")
QK_HEAD = QK_NOPE + QK_ROPE


def _rows(stacked):
    n, r, c = stacked.shape
    return stacked.reshape(n * r, c)


def _cols(stacked):
    n, k, c = stacked.shape
    return stacked.transpose(1, 0, 2).reshape(k, n * c)


def _kernel_weights(stacked):
    win_t = _rows(stacked["w_in"])
    D = win_t.shape[1]
    wuq_t = _rows(stacked["w_uq"])
    QL = wuq_t.shape[1]
    H = wuq_t.shape[0] // QK_HEAD
    wuq_t = jnp.pad(wuq_t.reshape(H, QK_HEAD, QL), ((0, 0), (0, HEAD_W - QK_HEAD), (0, 0))).reshape(H * HEAD_W, QL)
    full = {"win_t": win_t, "wgate_t": win_t[win_t.shape[0] - 2 * D:], "wuq_t": wuq_t,
            "wukv": _cols(stacked["w_ukv"]), "wpp": _cols(stacked["w_pool_proj"]),
            "wap": _rows(stacked["w_attn_proj"]), "wout": _rows(stacked["w_out"])}
    for tag in ("ffn1", "ffn2"):
        full[tag] = {"up_t": _rows(stacked[tag + "_up"]), "wd": _rows(stacked[tag + "_down"])}
    return full


def _split_rows(full):
    return full.reshape(N_DEV, full.shape[0] // N_DEV, full.shape[1])


def _split_cols(full):
    k, cols = full.shape
    return full.reshape(k, N_DEV, cols // N_DEV).transpose(1, 0, 2)


def _mixer_grads_stacked(g):
    n_a = g["wa_t"].shape[0] - (LANE - QK_ROPE)
    HQ, QL = g["wuq_t"].shape
    H = HQ // HEAD_W
    wuq_t = g["wuq_t"].reshape(H, HEAD_W, QL)[:, :QK_HEAD].reshape(H * QK_HEAD, QL)
    return {"w_in": _split_rows(jnp.concatenate([g["wa_t"][:n_a], g["wgate_t"]], axis=0)),
            "w_uq": _split_rows(wuq_t),
            "w_pool_proj": _split_cols(g["wpp"]), "w_ukv": _split_cols(g["wukv"]),
            "w_attn_proj": _split_rows(g["wap"]), "w_out": _split_rows(g["wout"])}


def _mesh_place():
    x, y, c = lax.axis_index("x"), lax.axis_index("y"), lax.axis_index("c")
    chips = [(1 - x, y), (x, 1 - y), (1 - x, 1 - y)]
    return x, y, c, chips


HBM = pl.BlockSpec(memory_space=pltpu.HBM)
SEMAPHORES = pl.BlockSpec(memory_space=pltpu.SEMAPHORE)
DATAFLOW = pltpu.SideEffectType.DATAFLOW_SIDE_EFFECTING
GATHER_PEERS = 4
SCATTER_PEERS = 7


def _in_hbm(a):
    return pltpu.with_memory_space_constraint(a, pltpu.HBM)


def _gather_plan(src_refs, land_refs):
    x, y, c, chips = _mesh_place()
    me = 4 * x + 2 * y + c
    targets = [(x, y, 1 - c)] + [(cx, cy, c) for cx, cy in chips]
    return [(s, land.at[me], to) for s, land in zip(src_refs, land_refs) for to in targets]


def _scatter_plan(src_refs, land_refs):
    x, y, c, _ = _mesh_place()
    peers = [(x, y, 1 - c), (1 - x, y, c), (x, 1 - y, c), (1 - x, 1 - y, c),
             (1 - x, y, 1 - c), (x, 1 - y, 1 - c), (1 - x, 1 - y, 1 - c)]
    return [(s.at[4 * px + 2 * py + pc], land.at[k], (px, py, pc))
            for s, land in zip(src_refs, land_refs) for k, (px, py, pc) in enumerate(peers)]


def _descriptors(plan, src_refs, land_refs, send_sems, recv_sems):
    return [pltpu.make_async_remote_copy(src_ref=s, dst_ref=d, send_sem=send_sems.at[k], recv_sem=recv_sems.at[k],
                                         device_id=to, device_id_type=MESH)
            for k, (s, d, to) in enumerate(plan(src_refs, land_refs))]


def _exchange(srcs, land_shapes, plan, per_src, *, name):
    n = len(srcs)

    def body(*refs):
        copies = _descriptors(plan, refs[:n], refs[n:2 * n], refs[2 * n], refs[2 * n + 1])
        for cp in copies:
            cp.start()
        for cp in copies:
            cp.wait()

    return pl.pallas_call(
        body, name=name,
        out_shape=tuple(jax.ShapeDtypeStruct(shape, s.dtype) for shape, s in zip(land_shapes, srcs)),
        in_specs=[ANY] * n, out_specs=(ANY,) * n,
        scratch_shapes=[pltpu.SemaphoreType.DMA((per_src * n,)), pltpu.SemaphoreType.DMA((per_src * n,))],
    )(*srcs)


def _forward_plan(src_refs, land_refs):
    x, y, c, chips = _mesh_place()
    return [(land.at[4 * cx + 2 * cy + c], land.at[4 * cx + 2 * cy + c], (x, y, 1 - c))
            for land in land_refs for cx, cy in chips]


def _gather_all_plan(src_refs, land_refs):
    x, y, c, _ = _mesh_place()
    me = 4 * x + 2 * y + c
    peers = [(x, y, 1 - c), (1 - x, y, c), (x, 1 - y, c), (1 - x, 1 - y, c),
             (1 - x, y, 1 - c), (x, 1 - y, 1 - c), (1 - x, 1 - y, 1 - c)]
    return [(s, land.at[me], to) for s, land in zip(src_refs, land_refs) for to in peers]


def _exchange_start(srcs, lands, plan, n_copies, *, name):
    ns, n = len(srcs), len(srcs) + len(lands)

    def body(*refs):
        for cp in _descriptors(plan, refs[:ns], refs[ns:n], refs[n], refs[n + 1]):
            cp.start()
        refs[-1][...] = jnp.zeros_like(refs[-1])

    sems = pltpu.SemaphoreType.DMA((n_copies,))
    out = pl.pallas_call(
        body, name=name,
        out_shape=(sems, sems, *[pltpu.HBM(a.shape, a.dtype) for a in srcs + lands],
                   jax.ShapeDtypeStruct((8, LANE), F32)),
        in_specs=(HBM,) * n,
        out_specs=(SEMAPHORES, SEMAPHORES, *[HBM] * n, pl.BlockSpec(memory_space=pltpu.VMEM)),
        input_output_aliases={i: 2 + i for i in range(n)},
        compiler_params=pltpu.CompilerParams(has_side_effects=DATAFLOW),
    )(*[_in_hbm(a) for a in srcs + lands])
    return out[0], out[1], list(out[2:2 + ns]), list(out[2 + ns:2 + n]), out[-1]


def _exchange_wait(send_sems, recv_sems, srcs, lands, plan, after, *, name):
    ns, n = len(srcs), len(srcs) + len(lands)

    def body(*refs):
        for cp in _descriptors(plan, refs[:ns], refs[ns:n], refs[n], refs[n + 1]):
            cp.wait_send()
            cp.wait_recv()

    out = pl.pallas_call(
        body, name=name,
        out_shape=tuple(pltpu.HBM(a.shape, a.dtype) for a in srcs + lands),
        in_specs=(*[HBM] * n, SEMAPHORES, SEMAPHORES, ANY),
        out_specs=(HBM,) * n,
        input_output_aliases={i: i for i in range(n)},
        compiler_params=pltpu.CompilerParams(has_side_effects=DATAFLOW),
    )(*srcs, *lands, send_sems, recv_sems, after)
    return list(out[:ns]), list(out[ns:])


def _gather_forward(lands, *, name):
    n = len(lands)

    def body(*refs):
        in_refs, out_refs = refs[:n], refs[n:2 * n]
        token, send_sems, recv_sems = refs[2 * n:2 * n + 3]
        x, y, c, chips = _mesh_place()
        passed = [pltpu.make_async_remote_copy(
            src_ref=i.at[4 * cx + 2 * cy + c], dst_ref=o.at[4 * cx + 2 * cy + c],
            send_sem=send_sems.at[3 * b + j], recv_sem=recv_sems.at[3 * b + j],
            device_id=(x, y, 1 - c), device_id_type=MESH)
            for b, (i, o) in enumerate(zip(in_refs, out_refs)) for j, (cx, cy) in enumerate(chips)]
        for cp in passed:
            cp.start()
        for cp in passed:
            cp.wait()
        token[...] = jnp.zeros_like(token)

    out = pl.pallas_call(
        body, name=name,
        out_shape=(*[jax.ShapeDtypeStruct(a.shape, a.dtype) for a in lands], jax.ShapeDtypeStruct((8, LANE), F32)),
        in_specs=[ANY] * n,
        out_specs=(*[ANY] * n, pl.BlockSpec(memory_space=pltpu.VMEM)),
        input_output_aliases={i: i for i in range(n)},
        scratch_shapes=[pltpu.SemaphoreType.DMA((3 * n,)), pltpu.SemaphoreType.DMA((3 * n,))],
    )(*lands)
    return list(out[:n]), out[n]


def _scatter_sum(parts, got, me, *, name):
    shard = parts.shape[1:]
    cols = shard[-1]
    rows = int(np.prod(shard[:-1]))
    tr = _tile(rows, 256, 16)

    def body(me_ref, p_ref, g_ref, o_ref):
        acc = p_ref[...].astype(F32)
        for k in range(SCATTER_PEERS):
            acc = acc + g_ref[k].astype(F32)
        o_ref[...] = acc

    out = pl.pallas_call(
        body, name=name,
        out_shape=jax.ShapeDtypeStruct((rows, cols), F32),
        grid_spec=pltpu.PrefetchScalarGridSpec(
            num_scalar_prefetch=1, grid=(rows // tr,),
            in_specs=[pl.BlockSpec((None, tr, cols), lambda r, me_ref: (me_ref[0], r, 0)),
                      pl.BlockSpec((SCATTER_PEERS, tr, cols), lambda r, me_ref: (0, r, 0))],
            out_specs=pl.BlockSpec((tr, cols), lambda r, me_ref: (r, 0))),
        compiler_params=_params("parallel"),
    )(me, parts.reshape(N_DEV, rows, cols), got.reshape(SCATTER_PEERS, rows, cols))
    return out.reshape(shard)


def _sum_devices(parts, *, name):
    _, R, C = parts.shape
    tr = _tile(R, 512, 8)

    def body(p_ref, o_ref):
        acc = p_ref[0]
        for d in range(1, N_DEV):
            acc = acc + p_ref[d]
        o_ref[...] = acc

    return pl.pallas_call(
        body, name=name,
        out_shape=jax.ShapeDtypeStruct((R, C), F32),
        grid=(R // tr,),
        in_specs=[pl.BlockSpec((N_DEV, tr, C), lambda r: (0, r, 0))],
        out_specs=pl.BlockSpec((tr, C), lambda r: (r, 0)),
        compiler_params=_params("parallel"),
    )(parts)


def _adamw(w, g, m, v, *, name, dep=None):
    shape = w.shape
    cols = shape[-1]
    rows = w.size // cols
    tr = _tile(rows, 256, 8)

    def body(w_ref, g_ref, m_ref, v_ref, *rest):
        d_ref, nm_ref, nv_ref = rest[-3:]
        g = g_ref[...]
        m = ADAM_B1 * m_ref[...] + (1.0 - ADAM_B1) * g
        v = ADAM_B2 * v_ref[...] + (1.0 - ADAM_B2) * jnp.square(g)
        m_hat = m / (1.0 - ADAM_B1 ** ADAM_STEP)
        v_hat = v / (1.0 - ADAM_B2 ** ADAM_STEP)
        d_ref[...] = -ADAM_LR * (m_hat / (jnp.sqrt(v_hat) + ADAM_EPS) + ADAM_WD * w_ref[...])
        nm_ref[...] = m
        nv_ref[...] = v

    spec = pl.BlockSpec((tr, cols), lambda i: (i, 0))
    out = jax.ShapeDtypeStruct((rows, cols), F32)
    d, nm, nv = pl.pallas_call(
        body, name=name,
        out_shape=(out, out, out),
        grid=(rows // tr,),
        in_specs=[spec] * 4 + _dep_spec(dep), out_specs=(spec,) * 3,
        compiler_params=_params("parallel"),
    )(*(a.reshape(rows, cols) for a in (w, g, m, v)), *([] if dep is None else [dep]))
    return d.reshape(shape), nm.reshape(shape), nv.reshape(shape)


PACK_ALIGN = 16 * LANE


def _pack(pieces, lead):
    out = []
    for p in pieces:
        keep = p.shape[:lead]
        flat = p.reshape(*keep, -1)
        pad = (-flat.shape[-1]) % PACK_ALIGN
        if pad:
            flat = jnp.pad(flat, [(0, 0)] * lead + [(0, pad)])
        out.append(flat.reshape(*keep, -1, LANE))
    return jnp.concatenate(out, axis=lead)


def _unpack(buf, shapes, lead):
    keep = buf.shape[:lead]
    out, row = [], 0
    for shape in shapes:
        size = int(np.prod(shape))
        rows = -(-size // PACK_ALIGN) * (PACK_ALIGN // LANE)
        piece = lax.slice_in_dim(buf, row, row + rows, axis=lead).reshape(*keep, rows * LANE)
        out.append(lax.slice_in_dim(piece, 0, size, axis=lead).reshape(*keep, *shape))
        row += rows
    return out


def kernel(x, positions, norm_ffn1, ffn1_up, ffn1_down, norm_mix, w_in, b_gate, pool_maps, pool_scale, w_pool_proj, q_latent_norm, w_uq, kv_latent_norm, w_ukv, w_attn_proj, w_out, norm_ffn2, ffn2_up, ffn2_down, final_norm, loss_target, m_norm_ffn1, m_ffn1_up, m_ffn1_down, m_norm_mix, m_w_in, m_b_gate, m_pool_maps, m_pool_scale, m_w_pool_proj, m_q_latent_norm, m_w_uq, m_kv_latent_norm, m_w_ukv, m_w_attn_proj, m_w_out, m_norm_ffn2, m_ffn2_up, m_ffn2_down, m_final_norm, v_norm_ffn1, v_ffn1_up, v_ffn1_down, v_norm_mix, v_w_in, v_b_gate, v_pool_maps, v_pool_scale, v_w_pool_proj, v_q_latent_norm, v_w_uq, v_kv_latent_norm, v_w_ukv, v_w_attn_proj, v_w_out, v_norm_ffn2, v_ffn2_up, v_ffn2_down, v_final_norm):
    order = ("norm_ffn1", "ffn1_up", "ffn1_down", "norm_mix", "w_in", "b_gate", "pool_maps", "pool_scale",
             "w_pool_proj", "q_latent_norm", "w_uq", "kv_latent_norm", "w_ukv", "w_attn_proj", "w_out",
             "norm_ffn2", "ffn2_up", "ffn2_down", "final_norm")
    w = dict(zip(order, (norm_ffn1, ffn1_up, ffn1_down, norm_mix, w_in, b_gate, pool_maps, pool_scale, w_pool_proj,
                         q_latent_norm, w_uq, kv_latent_norm, w_ukv, w_attn_proj, w_out, norm_ffn2, ffn2_up,
                         ffn2_down, final_norm)))
    m = dict(zip(order, (m_norm_ffn1, m_ffn1_up, m_ffn1_down, m_norm_mix, m_w_in, m_b_gate, m_pool_maps, m_pool_scale,
                         m_w_pool_proj, m_q_latent_norm, m_w_uq, m_kv_latent_norm, m_w_ukv, m_w_attn_proj, m_w_out,
                         m_norm_ffn2, m_ffn2_up, m_ffn2_down, m_final_norm)))
    v = dict(zip(order, (v_norm_ffn1, v_ffn1_up, v_ffn1_down, v_norm_mix, v_w_in, v_b_gate, v_pool_maps, v_pool_scale,
                         v_w_pool_proj, v_q_latent_norm, v_w_uq, v_kv_latent_norm, v_w_ukv, v_w_attn_proj, v_w_out,
                         v_norm_ffn2, v_ffn2_up, v_ffn2_down, v_final_norm)))
    L = norm_ffn1.shape[0]
    B, S, D = x.shape
    T = B * S

    def turned(a, n):
        return a.transpose(0, 2, 1) if n in TRANSPOSED else a

    wk, mk, vk = ({n: turned(d[n], n) for n in order} for d in (w, m, v))
    packed_shapes = [wk[n].shape[1:] for n in PACKED]
    my_slot = 4 * lax.axis_index("x") + 2 * lax.axis_index("y") + lax.axis_index("c")
    me = jnp.stack([my_slot]).astype(jnp.int32)

    def weight_blocks(l, token):
        zero = token[0, 0].astype(BF16)
        blocks = [wk[n][l].astype(BF16) + zero for n in DIRECT]
        return blocks + [_pack([wk[n][l].astype(BF16) + zero for n in PACKED], 0)]

    def kernel_weights(blocks, lands):
        lands = [lax.dynamic_update_index_in_dim(land, b, my_slot, 0) for land, b in zip(lands, blocks)]
        stacked = dict(zip(DIRECT, lands[:len(DIRECT)]))
        stacked.update(zip(PACKED, _unpack(lands[-1], packed_shapes, 1)))
        return _kernel_weights(stacked)

    tabs = _rope_tables(positions.reshape(T))
    xs = x.reshape(T, D)
    blocks = weight_blocks(0, jnp.zeros((8, LANE), F32))
    lands = _exchange(blocks, [(N_DEV, *b.shape) for b in blocks], _gather_plan, GATHER_PEERS, name="gather_first")
    lands, token = _gather_forward(lands, name="gather_forward")
    h = _rms_fwd(xs, w["norm_ffn1"][0], name="first_norm")
    full, saved = [], []
    for l in range(L):
        full.append(kernel_weights(blocks, lands))
        more = l + 1 < L
        p = {n: w[n][l] for n in SMALL}
        if more:
            blocks = weight_blocks(l + 1, token)
            lands = [lax.empty((N_DEV, *b.shape), b.dtype) for b in blocks]
            send_sems, recv_sems, blocks, lands, token = _exchange_start(
                blocks, lands, _gather_plan, GATHER_PEERS * len(blocks), name=f"gather_start_{l + 1}")
        xs, h, s1 = _ffn_fwd(xs, h, full[l]["ffn1"], "ffn1", p["norm_mix"], dep=token if more else None)
        xs, h, s2 = _mixer_fwd(xs, h, p, full[l], tabs, S, p["norm_ffn2"])
        if more:
            blocks, lands = _exchange_wait(send_sems, recv_sems, blocks, lands, _gather_plan, xs,
                                           name=f"gather_wait_{l + 1}")
            send_sems, recv_sems, _, lands, token = _exchange_start(
                [], lands, _forward_plan, 3 * len(lands), name=f"forward_start_{l + 1}")
        xs, h, s3 = _ffn_fwd(xs, h, full[l]["ffn2"], "ffn2", w["norm_ffn1"][l + 1] if more else None,
                             dep=token if more else None)
        if more:
            _, lands = _exchange_wait(send_sems, recv_sems, [], lands, _forward_plan, xs, name=f"forward_wait_{l + 1}")
        saved.append((s1, s2, s3))
    dx, dfinal, loss = _loss_head(xs, final_norm, loss_target.reshape(T, D), name="loss_head")

    big_grads = {n: [None] * L for n in BIG}
    small_grads_of = [None] * L
    pending = None

    def scatter_start(names, stacked, tag):
        srcs = [stacked[n] for n in names if n not in PACKED]
        if any(n in PACKED for n in names):
            srcs.append(_pack([stacked[n] for n in PACKED], 1))
        lands = [lax.empty((SCATTER_PEERS, *s.shape[1:]), s.dtype) for s in srcs]
        send_sems, recv_sems, srcs, lands, token = _exchange_start(
            srcs, lands, _scatter_plan, SCATTER_PEERS * len(srcs), name=f"scatter_start_{tag}")
        return (names, send_sems, recv_sems, srcs, lands, tag), token

    def scatter_finish(state, after, l):
        names, send_sems, recv_sems, srcs, lands, tag = state
        srcs, got = _exchange_wait(send_sems, recv_sems, srcs, lands, _scatter_plan, after, name=f"scatter_wait_{tag}")
        sums = [_scatter_sum(s, g, me, name="scatter_sum") for s, g in zip(srcs, got)]
        direct = [n for n in names if n not in PACKED]
        for n, g in zip(direct, sums):
            big_grads[n][l] = g
        if len(sums) > len(direct):
            for n, g in zip(PACKED, _unpack(sums[-1], packed_shapes, 0)):
                big_grads[n][l] = g

    dep = None
    for l in reversed(range(L)):
        p = {n: w[n][l] for n in SMALL}
        s1, s2, s3 = saved[l]
        small_g = {}
        dx, small_g["norm_ffn2"], dup_t, dwd = _ffn_bwd(dx, p["norm_ffn2"], full[l]["ffn2"], s3, "ffn2", dep=dep)
        if pending is not None:
            scatter_finish(pending[0], dx, pending[1])
        stacked = {"ffn2_up": _split_rows(dup_t), "ffn2_down": _split_rows(dwd)}
        state, dep = scatter_start(("ffn2_up", "ffn2_down"), stacked, f"ffn2_{l}")
        pending = (state, l)

        dx, gm = _mixer_bwd(dx, p, full[l], tabs, s2, S, dep=dep)
        scatter_finish(pending[0], dx, pending[1])
        names = ("w_in", "w_attn_proj", "w_out") + PACKED
        state, dep = scatter_start(names, _mixer_grads_stacked(gm), f"mix_{l}")
        pending = (state, l)
        small_g.update({n: gm[n] for n in SMALL if n in gm})

        dx, small_g["norm_ffn1"], dup_t, dwd = _ffn_bwd(dx, p["norm_ffn1"], full[l]["ffn1"], s1, "ffn1", dep=dep)
        scatter_finish(pending[0], dx, pending[1])
        stacked = {"ffn1_up": _split_rows(dup_t), "ffn1_down": _split_rows(dwd)}
        state, dep = scatter_start(("ffn1_up", "ffn1_down"), stacked, f"ffn1_{l}")
        pending = (state, l)
        small_grads_of[l] = small_g
    grad_x = dx.reshape(B, S, D)

    small_parts = [small_grads_of[l][n] for l in range(L) for n in SMALL] + [dfinal, loss[0, :1]]
    small_shapes = [p.shape for p in small_parts]
    vec = _pack([jnp.concatenate([p.reshape(-1) for p in small_parts])], 0)
    small_send, small_recv, vec_thru, small_land, small_token = _exchange_start(
        [vec], [lax.empty((N_DEV, *vec.shape), F32)], _gather_all_plan, SCATTER_PEERS, name="small_start")

    gk, grad, delta, new_m, new_v = {}, {}, {}, {}, {}

    def update(n, dep=None):
        wn, gn, mn, vn = (a.reshape(1, -1) if a.ndim == 1 else a for a in (wk[n], gk[n], mk[n], vk[n]))
        d, nm, nv = _adamw(wn, gn, mn, vn, name="adamw_" + n, dep=dep)
        grad[n] = turned(gk[n], n)
        delta[n], new_m[n], new_v[n] = (turned(a.reshape(wk[n].shape), n) for a in (d, nm, nv))

    last_block = ("ffn1_up", "ffn1_down")
    deps = [dep, small_token]
    for n in BIG:
        if n not in last_block:
            gk[n] = jnp.stack(big_grads[n])
            update(n, deps.pop(0) if deps else None)
    scatter_finish(pending[0], new_v["ffn2_down"], pending[1])
    for n in last_block:
        gk[n] = jnp.stack(big_grads[n])
        update(n)

    vec_thru, small_land = _exchange_wait(small_send, small_recv, vec_thru, small_land, _gather_all_plan,
                                          new_v["ffn1_down"], name="small_wait")
    parts = lax.dynamic_update_index_in_dim(small_land[0], vec_thru[0], my_slot, 0)
    flat = _sum_devices(parts, name="sum_small").reshape(-1)
    small_grads, at = [], 0
    for shape in small_shapes:
        size = int(np.prod(shape))
        small_grads.append(lax.slice_in_dim(flat, at, at + size).reshape(shape))
        at += size
    loss_total = small_grads[-1].reshape(())
    for i, n in enumerate(SMALL):
        gk[n] = jnp.stack([small_grads[l * len(SMALL) + i] for l in range(L)]).reshape(w[n].shape)
        update(n)
    gk["final_norm"] = small_grads[-2].reshape(final_norm.shape)
    update("final_norm")
    return (loss_total, grad_x, *[grad[n] for n in order], *[delta[n] for n in order],
            *[new_m[n] for n in order], *[new_v[n] for n in order])
```

```python
import functools

import numpy as np
import jax
import jax.numpy as jnp
from jax import lax
from jax.experimental import pallas as pl
from jax.experimental.pallas import tpu as pltpu

F32 = jnp.float32
BF16 = jnp.bfloat16

NORM_EPS = 1e-6
ROPE_THETA = 10000.0
QK_NOPE = 128
QK_ROPE = 64
V_DIM = 128
HEAD_W = 256
POOL_WINDOWS = (2, 4, 8, 16)
POOL_G = 128
POOL_DIM = 512
LANE = 128
ATTN_SCALE = float((QK_NOPE + QK_ROPE) ** -0.5)
ATTN_SCALE_LOG2 = ATTN_SCALE * float(np.log2(np.e))
MASK_VALUE = -1e30
ATTN_TILE = 512

ADAM_LR = 0.001
ADAM_B1 = 0.9
ADAM_B2 = 0.999
ADAM_EPS = 1e-08
ADAM_WD = 0.01
ADAM_STEP = 10

N_DEV = 8
VMEM_LIMIT = 52 * 1024 * 1024

MESH = pl.DeviceIdType.MESH
ANY = pl.BlockSpec(memory_space=pl.ANY)


def _tile(dim, target, align=LANE):
    if dim <= target:
        return dim
    t = (target // align) * align
    while t >= align:
        if dim % t == 0:
            return t
        t -= align
    return dim


def _params(*sem):
    return pltpu.CompilerParams(dimension_semantics=sem, vmem_limit_bytes=VMEM_LIMIT)


def _rstd(x):
    return lax.rsqrt(jnp.mean(x * x, axis=-1, keepdims=True) + NORM_EPS)


def _mm(a, b, *, name, ta=False, tb=False, out_dtype=F32, res=None, alpha=1.0, tm=512, tn=1024, tk=1024, b_row0=0,
        norm_gain=None, dep=None):
    if ta:
        K, M = a.shape
    else:
        M, K = a.shape
    if tb:
        N, K2 = b.shape
    else:
        K2, N = b.shape
    assert K == K2 or (not tb and K2 >= b_row0 + K), (a.shape, b.shape, ta, tb)
    tm, tn, tk = _tile(M, tm), _tile(N, tn), _tile(K, tk)
    nk = K // tk
    assert b_row0 % tk == 0
    kb0 = b_row0 // tk
    dims = (((0 if ta else 1,), (1 if tb else 0,)), ((), ()))
    has_res = res is not None
    has_norm = norm_gain is not None
    assert not has_norm or tn == N
    n_in = 2 + has_res + has_norm + (dep is not None)

    def body(*refs):
        a_ref, b_ref = refs[0], refs[1]
        res_ref = refs[2] if has_res else None
        gain_ref = refs[2 + has_res] if has_norm else None
        o_ref = refs[n_in]
        h_ref = refs[n_in + 1] if has_norm else None
        acc_ref = refs[n_in + 1 + has_norm] if nk > 1 else None
        part = lax.dot_general(a_ref[...].astype(BF16), b_ref[...].astype(BF16), dims,
                               preferred_element_type=F32)

        def finish(acc):
            r = acc * alpha if alpha != 1.0 else acc
            if has_res:
                r = res_ref[...].astype(F32) + r
            o_ref[...] = r.astype(out_dtype)
            if has_norm:
                h_ref[...] = (r * _rstd(r) * gain_ref[...]).astype(BF16)

        if nk == 1:
            finish(part)
        else:
            k = pl.program_id(2)

            @pl.when(k == 0)
            def _():
                acc_ref[...] = part

            @pl.when(k > 0)
            def _():
                acc_ref[...] += part

            @pl.when(k == nk - 1)
            def _():
                finish(acc_ref[...])

    a_spec = pl.BlockSpec((tk, tm), lambda i, j, k: (k, i)) if ta else pl.BlockSpec((tm, tk), lambda i, j, k: (i, k))
    b_spec = (pl.BlockSpec((tn, tk), lambda i, j, k: (j, k)) if tb
              else pl.BlockSpec((tk, tn), lambda i, j, k: (k + kb0, j)))
    in_specs = [a_spec, b_spec]
    operands = [a, b]
    tile_spec = pl.BlockSpec((tm, tn), lambda i, j, k: (i, j))
    if has_res:
        in_specs.append(tile_spec)
        operands.append(res)
    if has_norm:
        in_specs.append(pl.BlockSpec((1, tn), lambda i, j, k: (0, j)))
        operands.append(norm_gain.reshape(1, N))
    if dep is not None:
        in_specs += _dep_spec(dep)
        operands.append(dep)
    out = pl.pallas_call(
        body, name=name,
        out_shape=(jax.ShapeDtypeStruct((M, N), out_dtype),) + ((jax.ShapeDtypeStruct((M, N), BF16),) if has_norm else ()),
        grid=(M // tm, N // tn, nk),
        in_specs=in_specs,
        out_specs=(tile_spec,) + ((tile_spec,) if has_norm else ()),
        scratch_shapes=[pltpu.VMEM((tm, tn), F32)] if nk > 1 else [],
        compiler_params=_params("parallel", "parallel", "arbitrary"),
    )(*operands)
    return out if has_norm else out[0]


def _rms_fwd(x, g, *, name):
    T, D = x.shape
    tm = _tile(T, 512, 16)

    def body(x_ref, g_ref, h_ref):
        x = x_ref[...]
        h_ref[...] = (x * _rstd(x) * g_ref[...]).astype(BF16)

    return pl.pallas_call(
        body, name=name,
        out_shape=jax.ShapeDtypeStruct((T, D), BF16),
        grid=(T // tm,),
        in_specs=[pl.BlockSpec((tm, D), lambda i: (i, 0)), pl.BlockSpec((1, D), lambda i: (0, 0))],
        out_specs=pl.BlockSpec((tm, D), lambda i: (i, 0)),
        compiler_params=_params("parallel"),
    )(x, g.reshape(1, D))


def _dh_norm_bwd(a1, b1, a2, b2, b2_row0, x, g, dxo, *, name):
    T, D = x.shape
    K1, K2 = a1.shape[1], a2.shape[1]
    assert b2_row0 % K2 == 0 and b1.shape[0] >= K1 and b2.shape[0] >= b2_row0 + K2
    tm = _tile(T, 256, 16)

    def body(a1_ref, b1_ref, a2_ref, b2_ref, x_ref, g_ref, dxo_ref, dx_ref, dg_ref):
        x = x_ref[...]
        r = _rstd(x)
        xhat = x * r
        dh = (jnp.dot(a1_ref[...], b1_ref[...], preferred_element_type=F32)
              + jnp.dot(a2_ref[...], b2_ref[...], preferred_element_type=F32))
        dxh = dh * g_ref[...]
        dx_ref[...] = dxo_ref[...] + r * (dxh - xhat * jnp.mean(dxh * xhat, axis=-1, keepdims=True))
        part = jnp.sum(dh * xhat, axis=0, keepdims=True)

        @pl.when(pl.program_id(0) == 0)
        def _():
            dg_ref[...] = part

        @pl.when(pl.program_id(0) > 0)
        def _():
            dg_ref[...] += part

    row = pl.BlockSpec((tm, D), lambda i: (i, 0))
    vec = pl.BlockSpec((1, D), lambda i: (0, 0))
    return pl.pallas_call(
        body, name=name,
        out_shape=(jax.ShapeDtypeStruct((T, D), F32), jax.ShapeDtypeStruct((1, D), F32)),
        grid=(T // tm,),
        in_specs=[pl.BlockSpec((tm, K1), lambda i: (i, 0)), pl.BlockSpec((K1, D), lambda i: (0, 0)),
                  pl.BlockSpec((tm, K2), lambda i: (i, 0)), pl.BlockSpec((K2, D), lambda i: (b2_row0 // K2, 0)),
                  row, vec, row],
        out_specs=(row, vec),
        compiler_params=_params("arbitrary"),
    )(a1, b1, a2, b2, x, g.reshape(1, D), dxo)


def _loss_head(x, g, target, *, name):
    T, D = x.shape
    tm = _tile(T, 512, 16)

    def body(x_ref, g_ref, t_ref, dx_ref, dg_ref, loss_ref):
        x = x_ref[...]
        gain = g_ref[...]
        r = _rstd(x)
        xhat = x * r
        err = xhat * gain - t_ref[...]
        dy = err * (1.0 / D)
        dxh = dy * gain
        dx_ref[...] = r * (dxh - xhat * jnp.mean(dxh * xhat, axis=-1, keepdims=True))
        dg_part = jnp.sum(dy * xhat, axis=0, keepdims=True)
        loss_part = jnp.full((1, LANE), 0.5 / D, F32) * jnp.sum(err * err)

        @pl.when(pl.program_id(0) == 0)
        def _():
            dg_ref[...] = dg_part
            loss_ref[...] = loss_part

        @pl.when(pl.program_id(0) > 0)
        def _():
            dg_ref[...] += dg_part
            loss_ref[...] += loss_part

    row = pl.BlockSpec((tm, D), lambda i: (i, 0))
    vec = pl.BlockSpec((1, D), lambda i: (0, 0))
    return pl.pallas_call(
        body, name=name,
        out_shape=(jax.ShapeDtypeStruct((T, D), F32), jax.ShapeDtypeStruct((1, D), F32),
                   jax.ShapeDtypeStruct((1, LANE), F32)),
        grid=(T // tm,),
        in_specs=[row, vec, row],
        out_specs=(row, vec, pl.BlockSpec((1, LANE), lambda i: (0, 0))),
        compiler_params=_params("arbitrary"),
    )(x, g.reshape(1, D), target)


def _ffn_up(h, w_up_t, *, name, dep=None):
    T, D = h.shape
    F = w_up_t.shape[0] // 2
    tm, tn = _tile(T, 512, 16), _tile(F, 1408)
    nj = F // tn

    def body(h_ref, wg_ref, wu_ref, *rest):
        gate_ref, up_ref, a_ref = rest[-3:]
        h = h_ref[...]
        gate = lax.dot_general(h, wg_ref[...], _NT, preferred_element_type=F32)
        up = lax.dot_general(h, wu_ref[...], _NT, preferred_element_type=F32)
        gate_ref[...] = gate.astype(BF16)
        up_ref[...] = up.astype(BF16)
        a_ref[...] = (gate * jax.nn.sigmoid(gate) * up).astype(BF16)

    o_spec = pl.BlockSpec((tm, tn), lambda j, i: (i, j))
    out = jax.ShapeDtypeStruct((T, F), BF16)
    return pl.pallas_call(
        body, name=name,
        out_shape=(out, out, out),
        grid=(nj, T // tm),
        in_specs=[pl.BlockSpec((tm, D), lambda j, i: (i, 0)),
                  pl.BlockSpec((tn, D), lambda j, i: (j, 0)),
                  pl.BlockSpec((tn, D), lambda j, i: (j + nj, 0))] + _dep_spec(dep),
        out_specs=(o_spec, o_spec, o_spec),
        compiler_params=_params("parallel", "parallel"),
    )(h, w_up_t, w_up_t, *([] if dep is None else [dep]))


def _ffn_dw_up(dgate, dup, h, *, name):
    T, F = dgate.shape
    D = h.shape[1]
    tm, tk = _tile(F, 1408), _tile(T, 1024, 16)
    nf, nk = F // tm, T // tk

    def body(dgate_ref, dup_ref, h_ref, o_ref, acc_ref):
        i, k = pl.program_id(0), pl.program_id(1)

        def accumulate(part):
            @pl.when(k == 0)
            def _():
                acc_ref[...] = part

            @pl.when(k > 0)
            def _():
                acc_ref[...] += part

        @pl.when(i < nf)
        def _():
            accumulate(lax.dot_general(dgate_ref[...], h_ref[...], _TN, preferred_element_type=F32))

        @pl.when(i >= nf)
        def _():
            accumulate(lax.dot_general(dup_ref[...], h_ref[...], _TN, preferred_element_type=F32))

        @pl.when(k == nk - 1)
        def _():
            o_ref[...] = acc_ref[...].astype(BF16)

    return pl.pallas_call(
        body, name=name,
        out_shape=jax.ShapeDtypeStruct((2 * F, D), BF16),
        grid=(2 * nf, nk),
        in_specs=[pl.BlockSpec((tk, tm), lambda i, k: (jnp.where(i < nf, k, nk - 1), jnp.minimum(i, nf - 1))),
                  pl.BlockSpec((tk, tm), lambda i, k: (jnp.where(i < nf, 0, k), jnp.maximum(i - nf, 0))),
                  pl.BlockSpec((tk, D), lambda i, k: (k, 0))],
        out_specs=pl.BlockSpec((tm, D), lambda i, k: (i, 0)),
        scratch_shapes=[pltpu.VMEM((tm, D), F32)],
        compiler_params=_params("parallel", "arbitrary"),
    )(dgate, dup, h)


def _dep_spec(dep):
    return [] if dep is None else [pl.BlockSpec(dep.shape, lambda *_: (0,) * dep.ndim)]


def _ffn_bwd_act(dxo, wd, gate, up, *, alpha, name, dep=None):
    T, D = dxo.shape
    F = wd.shape[0]
    tm, tn = _tile(T, 256, 16), _tile(F, 2816)

    def body(dxo_ref, wd_ref, gate_ref, up_ref, *rest):
        dgate_ref, dup_ref = rest[-2:]
        da = lax.dot_general(dxo_ref[...].astype(BF16), wd_ref[...], (((1,), (1,)), ((), ())),
                             preferred_element_type=F32) * alpha
        gate = gate_ref[...].astype(F32)
        up = up_ref[...].astype(F32)
        sig = jax.nn.sigmoid(gate)
        dgate_ref[...] = (da * up * (sig * (1.0 + gate * (1.0 - sig)))).astype(BF16)
        dup_ref[...] = (da * (gate * sig)).astype(BF16)

    t_spec = pl.BlockSpec((tm, tn), lambda j, i: (i, j))
    out = jax.ShapeDtypeStruct((T, F), BF16)
    return pl.pallas_call(
        body, name=name,
        out_shape=(out, out),
        grid=(F // tn, T // tm),
        in_specs=[pl.BlockSpec((tm, D), lambda j, i: (i, 0)), pl.BlockSpec((tn, D), lambda j, i: (j, 0)),
                  t_spec, t_spec] + _dep_spec(dep),
        out_specs=(t_spec, t_spec),
        compiler_params=_params("parallel", "parallel"),
    )(dxo, wd, gate, up, *([] if dep is None else [dep]))


def _rope_tables(positions):
    half = QK_ROPE // 2
    inv_freq = ROPE_THETA ** (-jnp.arange(0, QK_ROPE, 2, dtype=F32) / QK_ROPE)
    ang = positions.astype(F32)[:, None] * inv_freq
    cos, sin = jnp.cos(ang), jnp.sin(ang)
    z = jnp.zeros_like(cos)
    zz = jnp.zeros((positions.shape[0], LANE - QK_ROPE), F32)
    c = jnp.concatenate([cos, cos, zz], axis=1)
    sa = jnp.concatenate([z, sin, zz], axis=1)
    sb = jnp.concatenate([-sin, z, zz], axis=1)
    return c, sa, sb


def _rotate(seg, c, sa, sb, sign):
    half = QK_ROPE // 2
    mix = pltpu.roll(seg, half, 1) * sa + pltpu.roll(seg, LANE - half, 1) * sb
    return seg * c + mix if sign > 0 else seg * c - mix


def _mixer_in(h, wa, wuq, wukv, gq, gkv, tabs, *, name):
    T, D = h.shape
    HQ, QL = wuq.shape
    KVL = wukv.shape[0]
    H = HQ // HEAD_W
    o_q, o_kv, o_kr = POOL_DIM, POOL_DIM + QL, POOL_DIM + QL + KVL
    PA = o_kr + LANE
    assert wa.shape[0] >= PA
    tm = _tile(T, 256, 16)

    def body(h_ref, wa_ref, wuq_ref, wukv_ref, gq_ref, gkv_ref, c_ref, sa_ref, sb_ref,
             xp_ref, ql_ref, kvl_ref, qn_ref, kvn_ref, q_ref, kv_ref, kr_ref):
        proj = lax.dot_general(h_ref[...], wa_ref[...], _NT, preferred_element_type=F32)
        xp_ref[...] = proj[:, :POOL_DIM]
        ql = proj[:, o_q:o_kv]
        kvl = proj[:, o_kv:o_kr]
        ql_ref[...] = ql
        kvl_ref[...] = kvl
        qn = (ql * _rstd(ql) * gq_ref[...]).astype(BF16)
        kvn = (kvl * _rstd(kvl) * gkv_ref[...]).astype(BF16)
        qn_ref[...] = qn
        kvn_ref[...] = kvn
        c, sa, sb = c_ref[...], sa_ref[...], sb_ref[...]
        q = lax.dot_general(qn, wuq_ref[...], _NT, preferred_element_type=F32)
        for hh in range(H):
            base = hh * HEAD_W
            q_ref[:, base:base + QK_NOPE] = q[:, base:base + QK_NOPE].astype(BF16)
            q_ref[:, base + QK_NOPE:base + HEAD_W] = _rotate(
                q[:, base + QK_NOPE:base + HEAD_W], c, sa, sb, 1).astype(BF16)
        kv_ref[...] = jnp.dot(kvn, wukv_ref[...], preferred_element_type=F32).astype(BF16)
        kr_ref[...] = _rotate(proj[:, o_kr:o_kr + LANE], c, sa, sb, 1).astype(BF16)

    def row(w):
        return pl.BlockSpec((tm, w), lambda i: (i, 0))

    def whole(arr):
        return pl.BlockSpec(arr.shape, lambda i: (0,) * arr.ndim)

    gq2, gkv2 = gq.reshape(1, QL), gkv.reshape(1, KVL)
    outs = [(POOL_DIM, F32), (QL, F32), (KVL, F32), (QL, BF16), (KVL, BF16), (HQ, BF16), (HQ, BF16), (LANE, BF16)]
    return pl.pallas_call(
        body, name=name,
        out_shape=tuple(jax.ShapeDtypeStruct((T, w), dt) for w, dt in outs),
        grid=(T // tm,),
        in_specs=[row(D), pl.BlockSpec((PA, D), lambda i: (0, 0)), whole(wuq), whole(wukv), whole(gq2), whole(gkv2),
                  row(LANE), row(LANE), row(LANE)],
        out_specs=tuple(row(w) for w, _ in outs),
        compiler_params=_params("parallel"),
    )(h, wa, wuq, wukv, gq2, gkv2, *tabs)


def _mixer_in_bwd(dq, dkv, dkr, ql, kvl, dxp, wuq, wukv, gq, gkv, tabs, *, name):
    T, HQ = dq.shape
    QL, KVL = wuq.shape[1], wukv.shape[0]
    H = HQ // HEAD_W
    PA = POOL_DIM + QL + KVL + LANE
    o_q, o_kv, o_kr = POOL_DIM, POOL_DIM + QL, POOL_DIM + QL + KVL
    tm = _tile(T, 256, 16)

    def norm_bwd(lat, gain, dn):
        r = _rstd(lat)
        xhat = lat * r
        dxh = dn * gain
        dlat = r * (dxh - xhat * jnp.mean(dxh * xhat, axis=-1, keepdims=True))
        return dlat, jnp.sum(dn * xhat, axis=0, keepdims=True)

    def body(dq_ref, dkv_ref, dkr_ref, ql_ref, kvl_ref, dxp_ref, wuq_ref, wukv_ref, gq_ref, gkv_ref,
             c_ref, sa_ref, sb_ref, dproj_ref, dqp_ref, dgq_ref, dgkv_ref):
        c, sa, sb = c_ref[...], sa_ref[...], sb_ref[...]
        dkr_sum = dkr_ref[:, :LANE]
        for hh in range(H):
            base = hh * HEAD_W
            dqp_ref[:, base:base + QK_NOPE] = dq_ref[:, base:base + QK_NOPE]
            dqp_ref[:, base + QK_NOPE:base + HEAD_W] = _rotate(
                dq_ref[:, base + QK_NOPE:base + HEAD_W].astype(F32), c, sa, sb, -1).astype(BF16)
            if hh:
                dkr_sum = dkr_sum + dkr_ref[:, hh * LANE:(hh + 1) * LANE]
        dqn = jnp.dot(dqp_ref[...], wuq_ref[...], preferred_element_type=F32)
        dkvn = lax.dot_general(dkv_ref[...], wukv_ref[...], _NT, preferred_element_type=F32)
        dql, dgq = norm_bwd(ql_ref[...], gq_ref[...], dqn)
        dkvl, dgkv = norm_bwd(kvl_ref[...], gkv_ref[...], dkvn)
        dproj_ref[:, :POOL_DIM] = dxp_ref[...].astype(BF16)
        dproj_ref[:, o_q:o_kv] = dql.astype(BF16)
        dproj_ref[:, o_kv:o_kr] = dkvl.astype(BF16)
        dproj_ref[:, o_kr:PA] = _rotate(dkr_sum, c, sa, sb, -1).astype(BF16)

        @pl.when(pl.program_id(0) == 0)
        def _():
            dgq_ref[...] = dgq
            dgkv_ref[...] = dgkv

        @pl.when(pl.program_id(0) > 0)
        def _():
            dgq_ref[...] += dgq
            dgkv_ref[...] += dgkv

    def row(w):
        return pl.BlockSpec((tm, w), lambda i: (i, 0))

    def whole(arr):
        return pl.BlockSpec(arr.shape, lambda i: (0,) * arr.ndim)

    gq2, gkv2 = gq.reshape(1, QL), gkv.reshape(1, KVL)
    return pl.pallas_call(
        body, name=name,
        out_shape=(jax.ShapeDtypeStruct((T, PA), BF16), jax.ShapeDtypeStruct((T, HQ), BF16),
                   jax.ShapeDtypeStruct((1, QL), F32), jax.ShapeDtypeStruct((1, KVL), F32)),
        grid=(T // tm,),
        in_specs=[row(HQ), row(HQ), row(H * LANE), row(QL), row(KVL), row(POOL_DIM), whole(wuq), whole(wukv),
                  whole(gq2), whole(gkv2), row(LANE), row(LANE), row(LANE)],
        out_specs=(row(PA), row(HQ), whole(gq2), whole(gkv2)),
        compiler_params=_params("arbitrary"),
    )(dq, dkv, dkr, ql, kvl, dxp, wuq, wukv, gq2, gkv2, *tabs)


def _pool_groups(x_of, S):
    row = lax.broadcasted_iota(jnp.int32, (S, POOL_G), 0)
    for g, w in enumerate(POOL_WINDOWS):
        x = x_of(g)
        s = x
        d = 1
        while d < w:
            s = s + jnp.where(row >= d, pltpu.roll(s, d, 0), 0.0)
            d *= 2
        cnt = jnp.minimum(row + 1, w).astype(F32)
        yield g, w, x, s / cnt - x, cnt, row


def _pool_fwd(xp, maps, scale, *, S, name):
    T = xp.shape[0]

    def body(xp_ref, maps_ref, scale_ref, ms_ref):
        for g, _, _, pooled, _, _ in _pool_groups(lambda g: xp_ref[:, g * POOL_G:(g + 1) * POOL_G], S):
            mixed = jnp.dot(pooled.astype(BF16), maps_ref[g].astype(BF16), preferred_element_type=F32)
            ms_ref[:, g * POOL_G:(g + 1) * POOL_G] = (mixed * scale_ref[:, g * POOL_G:(g + 1) * POOL_G]).astype(BF16)

    return pl.pallas_call(
        body, name=name,
        out_shape=jax.ShapeDtypeStruct((T, POOL_DIM), BF16),
        grid=(T // S,),
        in_specs=[pl.BlockSpec((S, POOL_DIM), lambda b: (b, 0)),
                  pl.BlockSpec(maps.shape, lambda b: (0, 0, 0)),
                  pl.BlockSpec((1, POOL_DIM), lambda b: (0, 0))],
        out_specs=pl.BlockSpec((S, POOL_DIM), lambda b: (b, 0)),
        compiler_params=_params("parallel"),
    )(xp, maps, scale.reshape(1, POOL_DIM))


def _pool_bwd(xp, dms, maps, scale, *, S, name):
    T = xp.shape[0]

    def body(xp_ref, dms_ref, maps_ref, scale_ref, dxp_ref, dmaps_ref, dscale_ref):
        first = pl.program_id(0) == 0
        for g, w, _, pooled, cnt, row in _pool_groups(lambda g: xp_ref[:, g * POOL_G:(g + 1) * POOL_G], S):
            cols = slice(g * POOL_G, (g + 1) * POOL_G)
            pooled_b = pooled.astype(BF16)
            maps_b = maps_ref[g].astype(BF16)
            mixed = jnp.dot(pooled_b, maps_b, preferred_element_type=F32)
            dms = dms_ref[:, cols]
            dscale = jnp.sum(dms * mixed, axis=0, keepdims=True)
            dmixed = (dms * scale_ref[:, cols]).astype(BF16)
            dmaps = lax.dot_general(pooled_b, dmixed, (((0,), (0,)), ((), ())), preferred_element_type=F32)
            dpooled = lax.dot_general(dmixed, maps_b, (((1,), (1,)), ((), ())), preferred_element_type=F32)
            z = dpooled / cnt
            d = 1
            while d < w:
                z = z + jnp.where(row < S - d, pltpu.roll(z, S - d, 0), 0.0)
                d *= 2
            dxp_ref[:, cols] = z - dpooled

            @pl.when(first)
            def _():
                dmaps_ref[g] = dmaps
                dscale_ref[:, cols] = dscale

            @pl.when(jnp.logical_not(first))
            def _():
                dmaps_ref[g] += dmaps
                dscale_ref[:, cols] += dscale

    seq = pl.BlockSpec((S, POOL_DIM), lambda b: (b, 0))
    maps_spec = pl.BlockSpec(maps.shape, lambda b: (0, 0, 0))
    vec = pl.BlockSpec((1, POOL_DIM), lambda b: (0, 0))
    return pl.pallas_call(
        body, name=name,
        out_shape=(jax.ShapeDtypeStruct((T, POOL_DIM), F32), jax.ShapeDtypeStruct(maps.shape, F32),
                   jax.ShapeDtypeStruct((1, POOL_DIM), F32)),
        grid=(T // S,),
        in_specs=[seq, seq, maps_spec, vec],
        out_specs=(seq, maps_spec, vec),
        compiler_params=_params("arbitrary"),
    )(xp, dms, maps, scale.reshape(1, POOL_DIM))


def _causal_mask(s, t):
    r = lax.broadcasted_iota(jnp.int32, (t, t), 0)
    c = lax.broadcasted_iota(jnp.int32, (t, t), 1)
    return jnp.where(r >= c, s, MASK_VALUE)


_NT = (((1,), (1,)), ((), ()))
_TN = (((0,), (0,)), ((), ()))


def _attn_fwd(q, kv, kr, *, S, name):
    T, HQ = q.shape
    H = HQ // HEAD_W
    B = T // S
    t = _tile(S, ATTN_TILE)
    n = S // t

    def body(q_ref, k_ref, v_ref, kr_ref, o_ref, lse_ref, kcat):
        kcat[:, :QK_NOPE] = k_ref[...]
        kcat[:, QK_NOPE:] = kr_ref[...]
        for i in range(n):
            rows = slice(i * t, (i + 1) * t)
            qt = q_ref[rows, :]
            m = jnp.full((t, 1), MASK_VALUE, F32)
            l = jnp.zeros((t, 1), F32)
            acc = jnp.zeros((t, V_DIM), F32)
            for j in range(i + 1):
                cols = slice(j * t, (j + 1) * t)
                s = lax.dot_general(qt, kcat[cols, :], _NT, preferred_element_type=F32) * ATTN_SCALE_LOG2
                if j == i:
                    s = _causal_mask(s, t)
                m_new = jnp.maximum(m, jnp.max(s, axis=1, keepdims=True))
                p = jnp.exp2(s - m_new)
                corr = jnp.exp2(m - m_new)
                l = corr * l + jnp.sum(p, axis=1, keepdims=True)
                acc = corr * acc + jnp.dot(p.astype(BF16), v_ref[cols, :], preferred_element_type=F32)
                m = m_new
            o_ref[rows, :] = (acc / l).astype(BF16)
            lse_ref[rows, :] = jnp.broadcast_to(m + jnp.log2(l), (t, LANE))

    seq_h = pl.BlockSpec((S, LANE), lambda b, h: (b, h))
    return pl.pallas_call(
        body, name=name,
        out_shape=(jax.ShapeDtypeStruct((T, H * V_DIM), BF16), jax.ShapeDtypeStruct((T, H * LANE), F32)),
        grid=(B, H),
        in_specs=[pl.BlockSpec((S, HEAD_W), lambda b, h: (b, h)),
                  pl.BlockSpec((S, QK_NOPE), lambda b, h: (b, 2 * h)),
                  pl.BlockSpec((S, V_DIM), lambda b, h: (b, 2 * h + 1)),
                  pl.BlockSpec((S, LANE), lambda b, h: (b, 0))],
        out_specs=(seq_h, seq_h),
        scratch_shapes=[pltpu.VMEM((S, HEAD_W), BF16)],
        compiler_params=_params("parallel", "parallel"),
    )(q, kv, kv, kr)


def _attn_bwd(q, kv, kr, o, do, lse, *, S, name):
    T, HQ = q.shape
    H = HQ // HEAD_W
    B = T // S
    t = _tile(S, ATTN_TILE)
    n = S // t

    def body(q_ref, k_ref, v_ref, kr_ref, o_ref, do_ref, lse_ref, dq_ref, dkv_ref, dkr_ref, kcat, dq_acc):
        kcat[:, :QK_NOPE] = k_ref[...]
        kcat[:, QK_NOPE:] = kr_ref[...]
        delta = [jnp.sum(do_ref[i * t:(i + 1) * t, :].astype(F32) * o_ref[i * t:(i + 1) * t, :].astype(F32),
                         axis=1, keepdims=True) for i in range(n)]
        for j in range(n):
            cols = slice(j * t, (j + 1) * t)
            kc = kcat[cols, :]
            vt = v_ref[cols, :]
            dk = jnp.zeros((t, HEAD_W), F32)
            dv = jnp.zeros((t, V_DIM), F32)
            for i in range(j, n):
                rows = slice(i * t, (i + 1) * t)
                qt = q_ref[rows, :]
                dot_ = do_ref[rows, :]
                s = lax.dot_general(qt, kc, _NT, preferred_element_type=F32) * ATTN_SCALE_LOG2
                if i == j:
                    s = _causal_mask(s, t)
                p = jnp.exp2(s - lse_ref[rows, :][:, :1])
                dv = dv + lax.dot_general(p.astype(BF16), dot_, _TN, preferred_element_type=F32)
                dp = lax.dot_general(dot_, vt, _NT, preferred_element_type=F32)
                ds = (p * (dp - delta[i]) * ATTN_SCALE).astype(BF16)
                dk = dk + lax.dot_general(ds, qt, _TN, preferred_element_type=F32)
                dq_part = jnp.dot(ds, kc, preferred_element_type=F32)
                if j == 0:
                    dq_acc[rows, :] = dq_part
                else:
                    dq_acc[rows, :] += dq_part
            dkv_ref[cols, :QK_NOPE] = dk[:, :QK_NOPE].astype(BF16)
            dkv_ref[cols, QK_NOPE:] = dv.astype(BF16)
            dkr_ref[cols, :] = dk[:, QK_NOPE:]
        dq_ref[...] = dq_acc[...].astype(BF16)

    seq_q = pl.BlockSpec((S, HEAD_W), lambda b, h: (b, h))
    seq_h = pl.BlockSpec((S, LANE), lambda b, h: (b, h))
    return pl.pallas_call(
        body, name=name,
        out_shape=(jax.ShapeDtypeStruct((T, HQ), BF16), jax.ShapeDtypeStruct((T, HQ), BF16),
                   jax.ShapeDtypeStruct((T, H * LANE), F32)),
        grid=(B, H),
        in_specs=[seq_q,
                  pl.BlockSpec((S, QK_NOPE), lambda b, h: (b, 2 * h)),
                  pl.BlockSpec((S, V_DIM), lambda b, h: (b, 2 * h + 1)),
                  pl.BlockSpec((S, LANE), lambda b, h: (b, 0)),
                  seq_h, seq_h, seq_h],
        out_specs=(seq_q, seq_q, seq_h),
        scratch_shapes=[pltpu.VMEM((S, HEAD_W), BF16), pltpu.VMEM((S, HEAD_W), F32)],
        compiler_params=_params("parallel", "parallel"),
    )(q, kv, kv, kr, o, do, lse)


def _merge_out(h, ms, o, x, wgate, bgate, wpp, wap, wout, next_gain, *, name):
    T, D = x.shape
    tm = _tile(T, 256, 16)

    def body(h_ref, ms_ref, o_ref, x_ref, wgate_ref, bgate_ref, wpp_ref, wap_ref, wout_ref, ng_ref,
             gates_ref, ba_ref, bb_ref, merged_ref, xn_ref, hn_ref):
        logits = lax.dot_general(h_ref[...], wgate_ref[...], _NT, preferred_element_type=F32) + bgate_ref[...]
        gates = jax.nn.sigmoid(logits)
        ba = jnp.dot(ms_ref[...], wpp_ref[...], preferred_element_type=F32)
        bb = jnp.dot(o_ref[...], wap_ref[...], preferred_element_type=F32)
        merged = (gates[:, :D] * ba + gates[:, D:] * bb).astype(BF16)
        gates_ref[...] = gates.astype(BF16)
        ba_ref[...] = ba.astype(BF16)
        bb_ref[...] = bb.astype(BF16)
        merged_ref[...] = merged
        xn = x_ref[...] + jnp.dot(merged, wout_ref[...], preferred_element_type=F32)
        xn_ref[...] = xn
        hn_ref[...] = (xn * _rstd(xn) * ng_ref[...]).astype(BF16)

    def row(w):
        return pl.BlockSpec((tm, w), lambda i: (i, 0))

    def whole(arr):
        return pl.BlockSpec(arr.shape, lambda i: (0,) * arr.ndim)

    bg2, ng2 = bgate.reshape(1, 2 * D), next_gain.reshape(1, D)
    act = jax.ShapeDtypeStruct((T, D), BF16)
    return pl.pallas_call(
        body, name=name,
        out_shape=(jax.ShapeDtypeStruct((T, 2 * D), BF16), act, act, act, jax.ShapeDtypeStruct((T, D), F32), act),
        grid=(T // tm,),
        in_specs=[row(D), row(ms.shape[1]), row(o.shape[1]), row(D), whole(wgate), whole(bg2), whole(wpp),
                  whole(wap), whole(wout), whole(ng2)],
        out_specs=(row(2 * D), row(D), row(D), row(D), row(D), row(D)),
        compiler_params=_params("parallel"),
    )(h, ms, o, x, wgate, bg2, wpp, wap, wout, ng2)


def _merge_bwd(dxo, wout, gates, ba, bb, *, name, dep=None):
    T, D = dxo.shape
    tm = _tile(T, 256, 16)

    def body(dxo_ref, wout_ref, gates_ref, ba_ref, bb_ref, *rest):
        dba_ref, dbb_ref, dgl_ref, dbg_ref = rest[-4:]
        dm = lax.dot_general(dxo_ref[...].astype(BF16), wout_ref[...], _NT, preferred_element_type=F32)
        ga = gates_ref[:, :D].astype(F32)
        gb = gates_ref[:, D:].astype(F32)
        dba_ref[...] = (dm * ga).astype(BF16)
        dbb_ref[...] = (dm * gb).astype(BF16)
        dgl_a = dm * ba_ref[...].astype(F32) * (ga * (1.0 - ga))
        dgl_b = dm * bb_ref[...].astype(F32) * (gb * (1.0 - gb))
        dgl_ref[:, :D] = dgl_a.astype(BF16)
        dgl_ref[:, D:] = dgl_b.astype(BF16)
        sa = jnp.sum(dgl_a, axis=0, keepdims=True)
        sb = jnp.sum(dgl_b, axis=0, keepdims=True)

        @pl.when(pl.program_id(0) == 0)
        def _():
            dbg_ref[:, :D] = sa
            dbg_ref[:, D:] = sb

        @pl.when(pl.program_id(0) > 0)
        def _():
            dbg_ref[:, :D] += sa
            dbg_ref[:, D:] += sb

    def row(w):
        return pl.BlockSpec((tm, w), lambda i: (i, 0))

    act = jax.ShapeDtypeStruct((T, D), BF16)
    return pl.pallas_call(
        body, name=name,
        out_shape=(act, act, jax.ShapeDtypeStruct((T, 2 * D), BF16), jax.ShapeDtypeStruct((1, 2 * D), F32)),
        grid=(T // tm,),
        in_specs=[row(D), pl.BlockSpec(wout.shape, lambda i: (0, 0)), row(2 * D), row(D), row(D)] + _dep_spec(dep),
        out_specs=(row(D), row(D), row(2 * D), pl.BlockSpec((1, 2 * D), lambda i: (0, 0))),
        compiler_params=_params("arbitrary"),
    )(dxo, wout, gates, ba, bb, *([] if dep is None else [dep]))


def _ffn_fwd(x, h, w, tag, next_gain, dep=None):
    gate, up, a = _ffn_up(h, w["up_t"], name=f"{tag}_up", dep=dep)
    if next_gain is None:
        xn, hn = _mm(a, w["wd"], res=x, alpha=0.5, name=f"{tag}_down_last", tk=2816), None
    else:
        xn, hn = _mm(a, w["wd"], res=x, alpha=0.5, norm_gain=next_gain, name=f"{tag}_down", tk=2816)
    return xn, hn, (x, h, gate, up, a)


def _ffn_bwd(dxo, gain, w, saved, tag, dep=None):
    x, h, gate, up, a = saved
    F = gate.shape[1]
    dgate, dup = _ffn_bwd_act(dxo, w["wd"], gate, up, alpha=0.5, name=f"{tag}_bwd_act", dep=dep)
    dwd = _mm(a, dxo, ta=True, alpha=0.5, out_dtype=BF16, name=f"{tag}_dwd", tm=1408, tn=1024, tk=1024)
    dup_t = _ffn_dw_up(dgate, dup, h, name=f"{tag}_dw_up")
    dx, dgain = _dh_norm_bwd(dgate, w["up_t"], dup, w["up_t"], F, x, gain, dxo, name=f"{tag}_dh_norm_bwd")
    return dx, dgain, dup_t, dwd


def _mixer_fwd(x, h, p, w, tabs, S, next_gain):
    xp, ql, kvl, qn, kvn, q, kv, kr = _mixer_in(h, w["win_t"], w["wuq_t"], w["wukv"], p["q_latent_norm"],
                                                 p["kv_latent_norm"], tabs, name="mix_in")
    ms = _pool_fwd(xp, p["pool_maps"], p["pool_scale"], S=S, name="pool_fwd")
    o, lse = _attn_fwd(q, kv, kr, S=S, name="attn_fwd")
    gates, ba, bb, merged, xn, hn = _merge_out(h, ms, o, x, w["wgate_t"], p["b_gate"], w["wpp"], w["wap"], w["wout"],
                                               next_gain, name="merge_out")
    return xn, hn, (x, h, xp, ql, kvl, qn, kvn, q, kv, kr, ms, o, lse, gates, ba, bb, merged)


def _mixer_bwd(dxo, p, w, tabs, saved, S, dep=None):
    x, h, xp, ql, kvl, qn, kvn, q, kv, kr, ms, o, lse, gates, ba, bb, merged = saved
    dba, dbb, dgl, dbg = _merge_bwd(dxo, w["wout"], gates, ba, bb, name="merge_bwd", dep=dep)
    g = {}
    g["wout"] = _mm(merged, dxo, ta=True, out_dtype=BF16, name="d_wout", tm=1024, tk=1024)
    g["wpp"] = _mm(ms, dba, ta=True, out_dtype=BF16, name="d_wpp", tk=2048)
    g["wap"] = _mm(o, dbb, ta=True, out_dtype=BF16, name="d_wap", tm=1024, tk=2048)
    dms = _mm(dba, w["wpp"], tb=True, name="d_ms")
    do = _mm(dbb, w["wap"], tb=True, out_dtype=BF16, name="d_o")
    dxp, g["pool_maps"], g["pool_scale"] = _pool_bwd(xp, dms, p["pool_maps"], p["pool_scale"], S=S, name="pool_bwd")
    dq, dkv, dkr = _attn_bwd(q, kv, kr, o, do, lse, S=S, name="attn_bwd")
    dproj, dqp, g["q_latent_norm"], g["kv_latent_norm"] = _mixer_in_bwd(
        dq, dkv, dkr, ql, kvl, dxp, w["wuq_t"], w["wukv"], p["q_latent_norm"], p["kv_latent_norm"], tabs,
        name="mix_in_bwd")
    g["wuq_t"] = _mm(dqp, qn, ta=True, out_dtype=BF16, name="d_wuq", tm=2048, tk=2048)
    g["wukv"] = _mm(kvn, dkv, ta=True, out_dtype=BF16, name="d_wukv", tn=2048, tk=2048)
    g["wa_t"] = _mm(dproj, h, ta=True, out_dtype=BF16, name="d_wa", tm=1280, tn=1024, tk=2048)
    g["wgate_t"] = _mm(dgl, h, ta=True, out_dtype=BF16, name="d_wgate", tm=2048, tn=1024, tk=1024)
    dx, g["norm_mix"] = _dh_norm_bwd(dproj, w["win_t"], dgl, w["wgate_t"], 0, x, p["norm_mix"], dxo,
                                     name="mix_dh_norm_bwd")
    g["b_gate"] = dbg
    return dx, g


BIG = ("ffn1_up", "ffn1_down", "w_in", "w_pool_proj", "w_uq", "w_ukv", "w_attn_proj", "w_out", "ffn2_up", "ffn2_down")
SMALL = ("norm_ffn1", "norm_mix", "b_gate", "pool_maps", "pool_scale", "q_latent_norm", "kv_latent_norm", "norm_ffn2")
PACKED = ("w_pool_proj", "w_uq", "w_ukv")
DIRECT = tuple(n for n in BIG if n not in PACKED)
TRANSPOSED = ("ffn1_up", "ffn2_up", "w_in", "w_uq")
COL_SHARDED = ("w_pool_proj", "w_ukv")
QK_HEAD = QK_NOPE + QK_ROPE


def _rows(stacked):
    n, r, c = stacked.shape
    return stacked.reshape(n * r, c)


def _cols(stacked):
    n, k, c = stacked.shape
    return stacked.transpose(1, 0, 2).reshape(k, n * c)


def _kernel_weights(stacked):
    win_t = _rows(stacked["w_in"])
    D = win_t.shape[1]
    wuq_t = _rows(stacked["w_uq"])
    QL = wuq_t.shape[1]
    H = wuq_t.shape[0] // QK_HEAD
    wuq_t = jnp.pad(wuq_t.reshape(H, QK_HEAD, QL), ((0, 0), (0, HEAD_W - QK_HEAD), (0, 0))).reshape(H * HEAD_W, QL)
    full = {"win_t": win_t, "wgate_t": win_t[win_t.shape[0] - 2 * D:], "wuq_t": wuq_t,
            "wukv": _cols(stacked["w_ukv"]), "wpp": _cols(stacked["w_pool_proj"]),
            "wap": _rows(stacked["w_attn_proj"]), "wout": _rows(stacked["w_out"])}
    for tag in ("ffn1", "ffn2"):
        full[tag] = {"up_t": _rows(stacked[tag + "_up"]), "wd": _rows(stacked[tag + "_down"])}
    return full


def _split_rows(full):
    return full.reshape(N_DEV, full.shape[0] // N_DEV, full.shape[1])


def _split_cols(full):
    k, cols = full.shape
    return full.reshape(k, N_DEV, cols // N_DEV).transpose(1, 0, 2)


def _mixer_grads_stacked(g):
    n_a = g["wa_t"].shape[0] - (LANE - QK_ROPE)
    HQ, QL = g["wuq_t"].shape
    H = HQ // HEAD_W
    wuq_t = g["wuq_t"].reshape(H, HEAD_W, QL)[:, :QK_HEAD].reshape(H * QK_HEAD, QL)
    return {"w_in": _split_rows(jnp.concatenate([g["wa_t"][:n_a], g["wgate_t"]], axis=0)),
            "w_uq": _split_rows(wuq_t),
            "w_pool_proj": _split_cols(g["wpp"]), "w_ukv": _split_cols(g["wukv"]),
            "w_attn_proj": _split_rows(g["wap"]), "w_out": _split_rows(g["wout"])}


def _mesh_place():
    x, y, c = lax.axis_index("x"), lax.axis_index("y"), lax.axis_index("c")
    chips = [(1 - x, y), (x, 1 - y), (1 - x, 1 - y)]
    return x, y, c, chips


HBM = pl.BlockSpec(memory_space=pltpu.HBM)
SEMAPHORES = pl.BlockSpec(memory_space=pltpu.SEMAPHORE)
DATAFLOW = pltpu.SideEffectType.DATAFLOW_SIDE_EFFECTING
GATHER_PEERS = 4
SCATTER_PEERS = 7


def _in_hbm(a):
    return pltpu.with_memory_space_constraint(a, pltpu.HBM)


def _gather_plan(src_refs, land_refs):
    x, y, c, chips = _mesh_place()
    me = 4 * x + 2 * y + c
    targets = [(x, y, 1 - c)] + [(cx, cy, c) for cx, cy in chips]
    return [(s, land.at[me], to) for s, land in zip(src_refs, land_refs) for to in targets]


def _scatter_plan(src_refs, land_refs):
    x, y, c, _ = _mesh_place()
    peers = [(x, y, 1 - c), (1 - x, y, c), (x, 1 - y, c), (1 - x, 1 - y, c),
             (1 - x, y, 1 - c), (x, 1 - y, 1 - c), (1 - x, 1 - y, 1 - c)]
    return [(s.at[4 * px + 2 * py + pc], land.at[k], (px, py, pc))
            for s, land in zip(src_refs, land_refs) for k, (px, py, pc) in enumerate(peers)]


def _descriptors(plan, src_refs, land_refs, send_sems, recv_sems):
    return [pltpu.make_async_remote_copy(src_ref=s, dst_ref=d, send_sem=send_sems.at[k], recv_sem=recv_sems.at[k],
                                         device_id=to, device_id_type=MESH)
            for k, (s, d, to) in enumerate(plan(src_refs, land_refs))]


def _exchange(srcs, land_shapes, plan, per_src, *, name):
    n = len(srcs)

    def body(*refs):
        copies = _descriptors(plan, refs[:n], refs[n:2 * n], refs[2 * n], refs[2 * n + 1])
        for cp in copies:
            cp.start()
        for cp in copies:
            cp.wait()

    return pl.pallas_call(
        body, name=name,
        out_shape=tuple(jax.ShapeDtypeStruct(shape, s.dtype) for shape, s in zip(land_shapes, srcs)),
        in_specs=[ANY] * n, out_specs=(ANY,) * n,
        scratch_shapes=[pltpu.SemaphoreType.DMA((per_src * n,)), pltpu.SemaphoreType.DMA((per_src * n,))],
    )(*srcs)


FORWARD_COPIES = 4


def _forward_slots():
    x, y, c, chips = _mesh_place()
    return [4 * cx + 2 * cy + c for cx, cy in chips] + [4 * x + 2 * y + (1 - c)], (x, y, 1 - c)


def _forward_plan(src_refs, land_refs):
    slots, sibling = _forward_slots()
    return [(land.at[s], land.at[s], sibling) for land in land_refs for s in slots]


def _gather_all_plan(src_refs, land_refs):
    x, y, c, _ = _mesh_place()
    me = 4 * x + 2 * y + c
    peers = [(x, y, 1 - c), (1 - x, y, c), (x, 1 - y, c), (1 - x, 1 - y, c),
             (1 - x, y, 1 - c), (x, 1 - y, 1 - c), (1 - x, 1 - y, 1 - c)]
    return [(s, land.at[me], to) for s, land in zip(src_refs, land_refs) for to in peers]


def _exchange_start(srcs, lands, plan, n_copies, *, name):
    ns, n = len(srcs), len(srcs) + len(lands)

    def body(*refs):
        for cp in _descriptors(plan, refs[:ns], refs[ns:n], refs[n], refs[n + 1]):
            cp.start()
        refs[-1][...] = jnp.zeros_like(refs[-1])

    sems = pltpu.SemaphoreType.DMA((n_copies,))
    out = pl.pallas_call(
        body, name=name,
        out_shape=(sems, sems, *[pltpu.HBM(a.shape, a.dtype) for a in srcs + lands],
                   jax.ShapeDtypeStruct((8, LANE), F32)),
        in_specs=(HBM,) * n,
        out_specs=(SEMAPHORES, SEMAPHORES, *[HBM] * n, pl.BlockSpec(memory_space=pltpu.VMEM)),
        input_output_aliases={i: 2 + i for i in range(n)},
        compiler_params=pltpu.CompilerParams(has_side_effects=DATAFLOW),
    )(*[_in_hbm(a) for a in srcs + lands])
    return out[0], out[1], list(out[2:2 + ns]), list(out[2 + ns:2 + n]), out[-1]


def _exchange_wait(send_sems, recv_sems, srcs, lands, plan, after, *, name):
    ns, n = len(srcs), len(srcs) + len(lands)

    def body(*refs):
        for cp in _descriptors(plan, refs[:ns], refs[ns:n], refs[n], refs[n + 1]):
            cp.wait_send()
            cp.wait_recv()

    out = pl.pallas_call(
        body, name=name,
        out_shape=tuple(pltpu.HBM(a.shape, a.dtype) for a in srcs + lands),
        in_specs=(*[HBM] * n, SEMAPHORES, SEMAPHORES, ANY),
        out_specs=(HBM,) * n,
        input_output_aliases={i: i for i in range(n)},
        compiler_params=pltpu.CompilerParams(has_side_effects=DATAFLOW),
    )(*srcs, *lands, send_sems, recv_sems, after)
    return list(out[:ns]), list(out[ns:])


def _gather_forward(lands, *, name):
    n = len(lands)

    def body(*refs):
        in_refs, out_refs = refs[:n], refs[n:2 * n]
        token, send_sems, recv_sems = refs[2 * n:2 * n + 3]
        slots, sibling = _forward_slots()
        passed = [pltpu.make_async_remote_copy(
            src_ref=i.at[s], dst_ref=o.at[s],
            send_sem=send_sems.at[FORWARD_COPIES * b + j], recv_sem=recv_sems.at[FORWARD_COPIES * b + j],
            device_id=sibling, device_id_type=MESH)
            for b, (i, o) in enumerate(zip(in_refs, out_refs)) for j, s in enumerate(slots)]
        for cp in passed:
            cp.start()
        for cp in passed:
            cp.wait()
        token[...] = jnp.zeros_like(token)

    out = pl.pallas_call(
        body, name=name,
        out_shape=(*[jax.ShapeDtypeStruct(a.shape, a.dtype) for a in lands], jax.ShapeDtypeStruct((8, LANE), F32)),
        in_specs=[ANY] * n,
        out_specs=(*[ANY] * n, pl.BlockSpec(memory_space=pltpu.VMEM)),
        input_output_aliases={i: i for i in range(n)},
        scratch_shapes=[pltpu.SemaphoreType.DMA((FORWARD_COPIES * n,)), pltpu.SemaphoreType.DMA((FORWARD_COPIES * n,))],
    )(*lands)
    return list(out[:n]), out[n]


def _scatter_sum(parts, got, me, *, name):
    shard = parts.shape[1:]
    cols = shard[-1]
    rows = int(np.prod(shard[:-1]))
    tr = _tile(rows, 256, 16)

    def body(me_ref, p_ref, g_ref, o_ref):
        acc = p_ref[...].astype(F32)
        for k in range(SCATTER_PEERS):
            acc = acc + g_ref[k].astype(F32)
        o_ref[...] = acc

    out = pl.pallas_call(
        body, name=name,
        out_shape=jax.ShapeDtypeStruct((rows, cols), F32),
        grid_spec=pltpu.PrefetchScalarGridSpec(
            num_scalar_prefetch=1, grid=(rows // tr,),
            in_specs=[pl.BlockSpec((None, tr, cols), lambda r, me_ref: (me_ref[0], r, 0)),
                      pl.BlockSpec((SCATTER_PEERS, tr, cols), lambda r, me_ref: (0, r, 0))],
            out_specs=pl.BlockSpec((tr, cols), lambda r, me_ref: (r, 0))),
        compiler_params=_params("parallel"),
    )(me, parts.reshape(N_DEV, rows, cols), got.reshape(SCATTER_PEERS, rows, cols))
    return out.reshape(shard)


def _sum_devices(parts, *, name):
    _, R, C = parts.shape
    tr = _tile(R, 512, 8)

    def body(p_ref, o_ref):
        acc = p_ref[0]
        for d in range(1, N_DEV):
            acc = acc + p_ref[d]
        o_ref[...] = acc

    return pl.pallas_call(
        body, name=name,
        out_shape=jax.ShapeDtypeStruct((R, C), F32),
        grid=(R // tr,),
        in_specs=[pl.BlockSpec((N_DEV, tr, C), lambda r: (0, r, 0))],
        out_specs=pl.BlockSpec((tr, C), lambda r: (r, 0)),
        compiler_params=_params("parallel"),
    )(parts)


def _adamw(w, g, m, v, *, name, dep=None):
    shape = w.shape
    cols = shape[-1]
    rows = w.size // cols
    tr = _tile(rows, 256, 8)

    def body(w_ref, g_ref, m_ref, v_ref, *rest):
        d_ref, nm_ref, nv_ref = rest[-3:]
        g = g_ref[...]
        m = ADAM_B1 * m_ref[...] + (1.0 - ADAM_B1) * g
        v = ADAM_B2 * v_ref[...] + (1.0 - ADAM_B2) * jnp.square(g)
        m_hat = m / (1.0 - ADAM_B1 ** ADAM_STEP)
        v_hat = v / (1.0 - ADAM_B2 ** ADAM_STEP)
        d_ref[...] = -ADAM_LR * (m_hat / (jnp.sqrt(v_hat) + ADAM_EPS) + ADAM_WD * w_ref[...])
        nm_ref[...] = m
        nv_ref[...] = v

    spec = pl.BlockSpec((tr, cols), lambda i: (i, 0))
    out = jax.ShapeDtypeStruct((rows, cols), F32)
    d, nm, nv = pl.pallas_call(
        body, name=name,
        out_shape=(out, out, out),
        grid=(rows // tr,),
        in_specs=[spec] * 4 + _dep_spec(dep), out_specs=(spec,) * 3,
        compiler_params=_params("parallel"),
    )(*(a.reshape(rows, cols) for a in (w, g, m, v)), *([] if dep is None else [dep]))
    return d.reshape(shape), nm.reshape(shape), nv.reshape(shape)


PACK_ALIGN = 16 * LANE


def _pack(pieces, lead):
    out = []
    for p in pieces:
        keep = p.shape[:lead]
        flat = p.reshape(*keep, -1)
        pad = (-flat.shape[-1]) % PACK_ALIGN
        if pad:
            flat = jnp.pad(flat, [(0, 0)] * lead + [(0, pad)])
        out.append(flat.reshape(*keep, -1, LANE))
    return jnp.concatenate(out, axis=lead)


def _unpack(buf, shapes, lead):
    keep = buf.shape[:lead]
    out, row = [], 0
    for shape in shapes:
        size = int(np.prod(shape))
        rows = -(-size // PACK_ALIGN) * (PACK_ALIGN // LANE)
        piece = lax.slice_in_dim(buf, row, row + rows, axis=lead).reshape(*keep, rows * LANE)
        out.append(lax.slice_in_dim(piece, 0, size, axis=lead).reshape(*keep, *shape))
        row += rows
    return out


def kernel(x, positions, norm_ffn1, ffn1_up, ffn1_down, norm_mix, w_in, b_gate, pool_maps, pool_scale, w_pool_proj, q_latent_norm, w_uq, kv_latent_norm, w_ukv, w_attn_proj, w_out, norm_ffn2, ffn2_up, ffn2_down, final_norm, loss_target, m_norm_ffn1, m_ffn1_up, m_ffn1_down, m_norm_mix, m_w_in, m_b_gate, m_pool_maps, m_pool_scale, m_w_pool_proj, m_q_latent_norm, m_w_uq, m_kv_latent_norm, m_w_ukv, m_w_attn_proj, m_w_out, m_norm_ffn2, m_ffn2_up, m_ffn2_down, m_final_norm, v_norm_ffn1, v_ffn1_up, v_ffn1_down, v_norm_mix, v_w_in, v_b_gate, v_pool_maps, v_pool_scale, v_w_pool_proj, v_q_latent_norm, v_w_uq, v_kv_latent_norm, v_w_ukv, v_w_attn_proj, v_w_out, v_norm_ffn2, v_ffn2_up, v_ffn2_down, v_final_norm):
    order = ("norm_ffn1", "ffn1_up", "ffn1_down", "norm_mix", "w_in", "b_gate", "pool_maps", "pool_scale",
             "w_pool_proj", "q_latent_norm", "w_uq", "kv_latent_norm", "w_ukv", "w_attn_proj", "w_out",
             "norm_ffn2", "ffn2_up", "ffn2_down", "final_norm")
    w = dict(zip(order, (norm_ffn1, ffn1_up, ffn1_down, norm_mix, w_in, b_gate, pool_maps, pool_scale, w_pool_proj,
                         q_latent_norm, w_uq, kv_latent_norm, w_ukv, w_attn_proj, w_out, norm_ffn2, ffn2_up,
                         ffn2_down, final_norm)))
    m = dict(zip(order, (m_norm_ffn1, m_ffn1_up, m_ffn1_down, m_norm_mix, m_w_in, m_b_gate, m_pool_maps, m_pool_scale,
                         m_w_pool_proj, m_q_latent_norm, m_w_uq, m_kv_latent_norm, m_w_ukv, m_w_attn_proj, m_w_out,
                         m_norm_ffn2, m_ffn2_up, m_ffn2_down, m_final_norm)))
    v = dict(zip(order, (v_norm_ffn1, v_ffn1_up, v_ffn1_down, v_norm_mix, v_w_in, v_b_gate, v_pool_maps, v_pool_scale,
                         v_w_pool_proj, v_q_latent_norm, v_w_uq, v_kv_latent_norm, v_w_ukv, v_w_attn_proj, v_w_out,
                         v_norm_ffn2, v_ffn2_up, v_ffn2_down, v_final_norm)))
    L = norm_ffn1.shape[0]
    B, S, D = x.shape
    T = B * S

    def turned(a, n):
        return a.transpose(0, 2, 1) if n in TRANSPOSED else a

    wk, mk, vk = ({n: turned(d[n], n) for n in order} for d in (w, m, v))
    packed_shapes = [wk[n].shape[1:] for n in PACKED]
    my_slot = 4 * lax.axis_index("x") + 2 * lax.axis_index("y") + lax.axis_index("c")
    me = jnp.stack([my_slot]).astype(jnp.int32)

    def weight_blocks(l, token):
        zero = token[0, 0].astype(BF16)
        blocks = [wk[n][l].astype(BF16) + zero for n in DIRECT]
        return blocks + [_pack([wk[n][l].astype(BF16) + zero for n in PACKED], 0)]

    def kernel_weights(lands):
        stacked = dict(zip(DIRECT, lands[:len(DIRECT)]))
        stacked.update(zip(PACKED, _unpack(lands[-1], packed_shapes, 1)))
        return _kernel_weights(stacked)

    tabs = _rope_tables(positions.reshape(T))
    xs = x.reshape(T, D)
    blocks = weight_blocks(0, jnp.zeros((8, LANE), F32))
    lands = _exchange(blocks, [(N_DEV, *b.shape) for b in blocks], _gather_plan, GATHER_PEERS, name="gather_first")
    lands, token = _gather_forward(lands, name="gather_forward")
    h = _rms_fwd(xs, w["norm_ffn1"][0], name="first_norm")
    full, saved = [], []
    for l in range(L):
        full.append(kernel_weights(lands))
        more = l + 1 < L
        p = {n: w[n][l] for n in SMALL}
        if more:
            blocks = weight_blocks(l + 1, token)
            lands = [lax.empty((N_DEV, *b.shape), b.dtype) for b in blocks]
            send_sems, recv_sems, blocks, lands, token = _exchange_start(
                blocks, lands, _gather_plan, GATHER_PEERS * len(blocks), name=f"gather_start_{l + 1}")
        xs, h, s1 = _ffn_fwd(xs, h, full[l]["ffn1"], "ffn1", p["norm_mix"], dep=token if more else None)
        xs, h, s2 = _mixer_fwd(xs, h, p, full[l], tabs, S, p["norm_ffn2"])
        if more:
            _, lands = _exchange_wait(send_sems, recv_sems, blocks, lands, _gather_plan, xs,
                                      name=f"gather_wait_{l + 1}")
            send_sems, recv_sems, _, lands, token = _exchange_start(
                [], lands, _forward_plan, FORWARD_COPIES * len(lands), name=f"forward_start_{l + 1}")
        xs, h, s3 = _ffn_fwd(xs, h, full[l]["ffn2"], "ffn2", w["norm_ffn1"][l + 1] if more else None,
                             dep=token if more else None)
        if more:
            _, lands = _exchange_wait(send_sems, recv_sems, [], lands, _forward_plan, xs, name=f"forward_wait_{l + 1}")
        saved.append((s1, s2, s3))
    dx, dfinal, loss = _loss_head(xs, final_norm, loss_target.reshape(T, D), name="loss_head")

    big_grads = {n: [None] * L for n in BIG}
    small_grads_of = [None] * L
    pending = None

    def scatter_start(names, stacked, tag):
        srcs = [stacked[n] for n in names if n not in PACKED]
        if any(n in PACKED for n in names):
            srcs.append(_pack([stacked[n] for n in PACKED], 1))
        lands = [lax.empty((SCATTER_PEERS, *s.shape[1:]), s.dtype) for s in srcs]
        send_sems, recv_sems, srcs, lands, token = _exchange_start(
            srcs, lands, _scatter_plan, SCATTER_PEERS * len(srcs), name=f"scatter_start_{tag}")
        return (names, send_sems, recv_sems, srcs, lands, tag), token

    def scatter_finish(state, after, l):
        names, send_sems, recv_sems, srcs, lands, tag = state
        srcs, got = _exchange_wait(send_sems, recv_sems, srcs, lands, _scatter_plan, after, name=f"scatter_wait_{tag}")
        sums = [_scatter_sum(s, g, me, name="scatter_sum") for s, g in zip(srcs, got)]
        direct = [n for n in names if n not in PACKED]
        for n, g in zip(direct, sums):
            big_grads[n][l] = g
        if len(sums) > len(direct):
            for n, g in zip(PACKED, _unpack(sums[-1], packed_shapes, 0)):
                big_grads[n][l] = g

    dep = None
    for l in reversed(range(L)):
        p = {n: w[n][l] for n in SMALL}
        s1, s2, s3 = saved[l]
        small_g = {}
        dx, small_g["norm_ffn2"], dup_t, dwd = _ffn_bwd(dx, p["norm_ffn2"], full[l]["ffn2"], s3, "ffn2", dep=dep)
        if pending is not None:
            scatter_finish(pending[0], dx, pending[1])
        stacked = {"ffn2_up": _split_rows(dup_t), "ffn2_down": _split_rows(dwd)}
        state, dep = scatter_start(("ffn2_up", "ffn2_down"), stacked, f"ffn2_{l}")
        pending = (state, l)

        dx, gm = _mixer_bwd(dx, p, full[l], tabs, s2, S, dep=dep)
        scatter_finish(pending[0], dx, pending[1])
        names = ("w_in", "w_attn_proj", "w_out") + PACKED
        state, dep = scatter_start(names, _mixer_grads_stacked(gm), f"mix_{l}")
        pending = (state, l)
        small_g.update({n: gm[n] for n in SMALL if n in gm})

        dx, small_g["norm_ffn1"], dup_t, dwd = _ffn_bwd(dx, p["norm_ffn1"], full[l]["ffn1"], s1, "ffn1", dep=dep)
        scatter_finish(pending[0], dx, pending[1])
        stacked = {"ffn1_up": _split_rows(dup_t), "ffn1_down": _split_rows(dwd)}
        state, dep = scatter_start(("ffn1_up", "ffn1_down"), stacked, f"ffn1_{l}")
        pending = (state, l)
        small_grads_of[l] = small_g
    grad_x = dx.reshape(B, S, D)

    small_parts = [small_grads_of[l][n] for l in range(L) for n in SMALL] + [dfinal, loss[0, :1]]
    small_shapes = [p.shape for p in small_parts]
    vec = _pack([jnp.concatenate([p.reshape(-1) for p in small_parts])], 0)
    small_send, small_recv, vec_thru, small_land, small_token = _exchange_start(
        [vec], [lax.empty((N_DEV, *vec.shape), F32)], _gather_all_plan, SCATTER_PEERS, name="small_start")

    gk, grad, delta, new_m, new_v = {}, {}, {}, {}, {}

    def update(n, dep=None):
        wn, gn, mn, vn = (a.reshape(1, -1) if a.ndim == 1 else a for a in (wk[n], gk[n], mk[n], vk[n]))
        d, nm, nv = _adamw(wn, gn, mn, vn, name="adamw_" + n, dep=dep)
        grad[n] = turned(gk[n], n)
        delta[n], new_m[n], new_v[n] = (turned(a.reshape(wk[n].shape), n) for a in (d, nm, nv))

    last_block = ("ffn1_up", "ffn1_down")
    deps = [dep, small_token]
    for n in BIG:
        if n not in last_block:
            gk[n] = jnp.stack(big_grads[n])
            update(n, deps.pop(0) if deps else None)
    scatter_finish(pending[0], new_v["ffn2_down"], pending[1])
    for n in last_block:
        gk[n] = jnp.stack(big_grads[n])
        update(n)

    vec_thru, small_land = _exchange_wait(small_send, small_recv, vec_thru, small_land, _gather_all_plan,
                                          new_v["ffn1_down"], name="small_wait")
    parts = lax.dynamic_update_index_in_dim(small_land[0], vec_thru[0], my_slot, 0)
    flat = _sum_devices(parts, name="sum_small").reshape(-1)
    small_grads, at = [], 0
    for shape in small_shapes:
        size = int(np.prod(shape))
        small_grads.append(lax.slice_in_dim(flat, at, at + size).reshape(shape))
        at += size
    loss_total = small_grads[-1].reshape(())
    for i, n in enumerate(SMALL):
        gk[n] = jnp.stack([small_grads[l * len(SMALL) + i] for l in range(L)]).reshape(w[n].shape)
        update(n)
    gk["final_norm"] = small_grads[-2].reshape(final_norm.shape)
    update("final_norm")
    return (loss_total, grad_x, *[grad[n] for n in order], *[delta[n] for n in order],
            *[new_m[n] for n in order], *[new_v[n] for n in order])
```

```python
import functools

import numpy as np
import jax
import jax.numpy as jnp
from jax import lax
from jax.experimental import pallas as pl
from jax.experimental.pallas import tpu as pltpu

F32 = jnp.float32
BF16 = jnp.bfloat16

NORM_EPS = 1e-6
ROPE_THETA = 10000.0
QK_NOPE = 128
QK_ROPE = 64
V_DIM = 128
HEAD_W = 256
POOL_WINDOWS = (2, 4, 8, 16)
POOL_G = 128
POOL_DIM = 512
LANE = 128
ATTN_SCALE = float((QK_NOPE + QK_ROPE) ** -0.5)
ATTN_SCALE_LOG2 = ATTN_SCALE * float(np.log2(np.e))
MASK_VALUE = -1e30
ATTN_TILE = 512

ADAM_LR = 0.001
ADAM_B1 = 0.9
ADAM_B2 = 0.999
ADAM_EPS = 1e-08
ADAM_WD = 0.01
ADAM_STEP = 10

N_DEV = 8
VMEM_LIMIT = 52 * 1024 * 1024

MESH = pl.DeviceIdType.MESH
ANY = pl.BlockSpec(memory_space=pl.ANY)


def _tile(dim, target, align=LANE):
    if dim <= target:
        return dim
    t = (target // align) * align
    while t >= align:
        if dim % t == 0:
            return t
        t -= align
    return dim


def _params(*sem):
    return pltpu.CompilerParams(dimension_semantics=sem, vmem_limit_bytes=VMEM_LIMIT)


def _rstd(x):
    return lax.rsqrt(jnp.mean(x * x, axis=-1, keepdims=True) + NORM_EPS)


def _mm(a, b, *, name, ta=False, tb=False, out_dtype=F32, res=None, alpha=1.0, tm=512, tn=1024, tk=1024, b_row0=0,
        norm_gain=None, dep=None):
    if ta:
        K, M = a.shape
    else:
        M, K = a.shape
    if tb:
        N, K2 = b.shape
    else:
        K2, N = b.shape
    assert K == K2 or (not tb and K2 >= b_row0 + K), (a.shape, b.shape, ta, tb)
    tm, tn, tk = _tile(M, tm), _tile(N, tn), _tile(K, tk)
    nk = K // tk
    assert b_row0 % tk == 0
    kb0 = b_row0 // tk
    dims = (((0 if ta else 1,), (1 if tb else 0,)), ((), ()))
    has_res = res is not None
    has_norm = norm_gain is not None
    assert not has_norm or tn == N
    n_in = 2 + has_res + has_norm + (dep is not None)

    def body(*refs):
        a_ref, b_ref = refs[0], refs[1]
        res_ref = refs[2] if has_res else None
        gain_ref = refs[2 + has_res] if has_norm else None
        o_ref = refs[n_in]
        h_ref = refs[n_in + 1] if has_norm else None
        acc_ref = refs[n_in + 1 + has_norm] if nk > 1 else None
        part = lax.dot_general(a_ref[...].astype(BF16), b_ref[...].astype(BF16), dims,
                               preferred_element_type=F32)

        def finish(acc):
            r = acc * alpha if alpha != 1.0 else acc
            if has_res:
                r = res_ref[...].astype(F32) + r
            o_ref[...] = r.astype(out_dtype)
            if has_norm:
                h_ref[...] = (r * _rstd(r) * gain_ref[...]).astype(BF16)

        if nk == 1:
            finish(part)
        else:
            k = pl.program_id(2)

            @pl.when(k == 0)
            def _():
                acc_ref[...] = part

            @pl.when(k > 0)
            def _():
                acc_ref[...] += part

            @pl.when(k == nk - 1)
            def _():
                finish(acc_ref[...])

    a_spec = pl.BlockSpec((tk, tm), lambda i, j, k: (k, i)) if ta else pl.BlockSpec((tm, tk), lambda i, j, k: (i, k))
    b_spec = (pl.BlockSpec((tn, tk), lambda i, j, k: (j, k)) if tb
              else pl.BlockSpec((tk, tn), lambda i, j, k: (k + kb0, j)))
    in_specs = [a_spec, b_spec]
    operands = [a, b]
    tile_spec = pl.BlockSpec((tm, tn), lambda i, j, k: (i, j))
    if has_res:
        in_specs.append(tile_spec)
        operands.append(res)
    if has_norm:
        in_specs.append(pl.BlockSpec((1, tn), lambda i, j, k: (0, j)))
        operands.append(norm_gain.reshape(1, N))
    if dep is not None:
        in_specs += _dep_spec(dep)
        operands.append(dep)
    out = pl.pallas_call(
        body, name=name,
        out_shape=(jax.ShapeDtypeStruct((M, N), out_dtype),) + ((jax.ShapeDtypeStruct((M, N), BF16),) if has_norm else ()),
        grid=(M // tm, N // tn, nk),
        in_specs=in_specs,
        out_specs=(tile_spec,) + ((tile_spec,) if has_norm else ()),
        scratch_shapes=[pltpu.VMEM((tm, tn), F32)] if nk > 1 else [],
        compiler_params=_params("parallel", "parallel", "arbitrary"),
    )(*operands)
    return out if has_norm else out[0]


def _rms_fwd(x, g, *, name):
    T, D = x.shape
    tm = _tile(T, 512, 16)

    def body(x_ref, g_ref, h_ref):
        x = x_ref[...]
        h_ref[...] = (x * _rstd(x) * g_ref[...]).astype(BF16)

    return pl.pallas_call(
        body, name=name,
        out_shape=jax.ShapeDtypeStruct((T, D), BF16),
        grid=(T // tm,),
        in_specs=[pl.BlockSpec((tm, D), lambda i: (i, 0)), pl.BlockSpec((1, D), lambda i: (0, 0))],
        out_specs=pl.BlockSpec((tm, D), lambda i: (i, 0)),
        compiler_params=_params("parallel"),
    )(x, g.reshape(1, D))


def _dh_norm_bwd(a1, b1, a2, b2, b2_row0, x, g, dxo, *, name):
    T, D = x.shape
    K1, K2 = a1.shape[1], a2.shape[1]
    assert b2_row0 % K2 == 0 and b1.shape[0] >= K1 and b2.shape[0] >= b2_row0 + K2
    tm = _tile(T, 256, 16)

    def body(a1_ref, b1_ref, a2_ref, b2_ref, x_ref, g_ref, dxo_ref, dx_ref, dg_ref):
        x = x_ref[...]
        r = _rstd(x)
        xhat = x * r
        dh = (jnp.dot(a1_ref[...], b1_ref[...], preferred_element_type=F32)
              + jnp.dot(a2_ref[...], b2_ref[...], preferred_element_type=F32))
        dxh = dh * g_ref[...]
        dx_ref[...] = dxo_ref[...] + r * (dxh - xhat * jnp.mean(dxh * xhat, axis=-1, keepdims=True))
        part = jnp.sum(dh * xhat, axis=0, keepdims=True)

        @pl.when(pl.program_id(0) == 0)
        def _():
            dg_ref[...] = part

        @pl.when(pl.program_id(0) > 0)
        def _():
            dg_ref[...] += part

    row = pl.BlockSpec((tm, D), lambda i: (i, 0))
    vec = pl.BlockSpec((1, D), lambda i: (0, 0))
    return pl.pallas_call(
        body, name=name,
        out_shape=(jax.ShapeDtypeStruct((T, D), F32), jax.ShapeDtypeStruct((1, D), F32)),
        grid=(T // tm,),
        in_specs=[pl.BlockSpec((tm, K1), lambda i: (i, 0)), pl.BlockSpec((K1, D), lambda i: (0, 0)),
                  pl.BlockSpec((tm, K2), lambda i: (i, 0)), pl.BlockSpec((K2, D), lambda i: (b2_row0 // K2, 0)),
                  row, vec, row],
        out_specs=(row, vec),
        compiler_params=_params("arbitrary"),
    )(a1, b1, a2, b2, x, g.reshape(1, D), dxo)


def _loss_head(x, g, target, *, name):
    T, D = x.shape
    tm = _tile(T, 512, 16)

    def body(x_ref, g_ref, t_ref, dx_ref, dg_ref, loss_ref):
        x = x_ref[...]
        gain = g_ref[...]
        r = _rstd(x)
        xhat = x * r
        err = xhat * gain - t_ref[...]
        dy = err * (1.0 / D)
        dxh = dy * gain
        dx_ref[...] = r * (dxh - xhat * jnp.mean(dxh * xhat, axis=-1, keepdims=True))
        dg_part = jnp.sum(dy * xhat, axis=0, keepdims=True)
        loss_part = jnp.full((1, LANE), 0.5 / D, F32) * jnp.sum(err * err)

        @pl.when(pl.program_id(0) == 0)
        def _():
            dg_ref[...] = dg_part
            loss_ref[...] = loss_part

        @pl.when(pl.program_id(0) > 0)
        def _():
            dg_ref[...] += dg_part
            loss_ref[...] += loss_part

    row = pl.BlockSpec((tm, D), lambda i: (i, 0))
    vec = pl.BlockSpec((1, D), lambda i: (0, 0))
    return pl.pallas_call(
        body, name=name,
        out_shape=(jax.ShapeDtypeStruct((T, D), F32), jax.ShapeDtypeStruct((1, D), F32),
                   jax.ShapeDtypeStruct((1, LANE), F32)),
        grid=(T // tm,),
        in_specs=[row, vec, row],
        out_specs=(row, vec, pl.BlockSpec((1, LANE), lambda i: (0, 0))),
        compiler_params=_params("arbitrary"),
    )(x, g.reshape(1, D), target)


def _ffn_up(h, w_up_t, *, name, dep=None):
    T, D = h.shape
    F = w_up_t.shape[0] // 2
    tm, tn = _tile(T, 512, 16), _tile(F, 1408)
    nj = F // tn

    def body(h_ref, wg_ref, wu_ref, *rest):
        gate_ref, up_ref, a_ref = rest[-3:]
        h = h_ref[...]
        gate = lax.dot_general(h, wg_ref[...], _NT, preferred_element_type=F32)
        up = lax.dot_general(h, wu_ref[...], _NT, preferred_element_type=F32)
        gate_ref[...] = gate.astype(BF16)
        up_ref[...] = up.astype(BF16)
        a_ref[...] = (gate * jax.nn.sigmoid(gate) * up).astype(BF16)

    o_spec = pl.BlockSpec((tm, tn), lambda j, i: (i, j))
    out = jax.ShapeDtypeStruct((T, F), BF16)
    return pl.pallas_call(
        body, name=name,
        out_shape=(out, out, out),
        grid=(nj, T // tm),
        in_specs=[pl.BlockSpec((tm, D), lambda j, i: (i, 0)),
                  pl.BlockSpec((tn, D), lambda j, i: (j, 0)),
                  pl.BlockSpec((tn, D), lambda j, i: (j + nj, 0))] + _dep_spec(dep),
        out_specs=(o_spec, o_spec, o_spec),
        compiler_params=_params("parallel", "parallel"),
    )(h, w_up_t, w_up_t, *([] if dep is None else [dep]))


def _ffn_dw_up(dgate, dup, h, *, name):
    T, F = dgate.shape
    D = h.shape[1]
    tm, tk = _tile(F, 1408), _tile(T, 1024, 16)
    nf, nk = F // tm, T // tk

    def body(dgate_ref, dup_ref, h_ref, o_ref, acc_ref):
        i, k = pl.program_id(0), pl.program_id(1)

        def accumulate(part):
            @pl.when(k == 0)
            def _():
                acc_ref[...] = part

            @pl.when(k > 0)
            def _():
                acc_ref[...] += part

        @pl.when(i < nf)
        def _():
            accumulate(lax.dot_general(dgate_ref[...], h_ref[...], _TN, preferred_element_type=F32))

        @pl.when(i >= nf)
        def _():
            accumulate(lax.dot_general(dup_ref[...], h_ref[...], _TN, preferred_element_type=F32))

        @pl.when(k == nk - 1)
        def _():
            o_ref[...] = acc_ref[...].astype(BF16)

    return pl.pallas_call(
        body, name=name,
        out_shape=jax.ShapeDtypeStruct((2 * F, D), BF16),
        grid=(2 * nf, nk),
        in_specs=[pl.BlockSpec((tk, tm), lambda i, k: (jnp.where(i < nf, k, nk - 1), jnp.minimum(i, nf - 1))),
                  pl.BlockSpec((tk, tm), lambda i, k: (jnp.where(i < nf, 0, k), jnp.maximum(i - nf, 0))),
                  pl.BlockSpec((tk, D), lambda i, k: (k, 0))],
        out_specs=pl.BlockSpec((tm, D), lambda i, k: (i, 0)),
        scratch_shapes=[pltpu.VMEM((tm, D), F32)],
        compiler_params=_params("parallel", "arbitrary"),
    )(dgate, dup, h)


def _dep_spec(dep):
    return [] if dep is None else [pl.BlockSpec(dep.shape, lambda *_: (0,) * dep.ndim)]


def _ffn_bwd_act(dxo, wd, gate, up, *, alpha, name, dep=None):
    T, D = dxo.shape
    F = wd.shape[0]
    tm, tn = _tile(T, 256, 16), _tile(F, 2816)

    def body(dxo_ref, wd_ref, gate_ref, up_ref, *rest):
        dgate_ref, dup_ref = rest[-2:]
        da = lax.dot_general(dxo_ref[...].astype(BF16), wd_ref[...], (((1,), (1,)), ((), ())),
                             preferred_element_type=F32) * alpha
        gate = gate_ref[...].astype(F32)
        up = up_ref[...].astype(F32)
        sig = jax.nn.sigmoid(gate)
        dgate_ref[...] = (da * up * (sig * (1.0 + gate * (1.0 - sig)))).astype(BF16)
        dup_ref[...] = (da * (gate * sig)).astype(BF16)

    t_spec = pl.BlockSpec((tm, tn), lambda j, i: (i, j))
    out = jax.ShapeDtypeStruct((T, F), BF16)
    return pl.pallas_call(
        body, name=name,
        out_shape=(out, out),
        grid=(F // tn, T // tm),
        in_specs=[pl.BlockSpec((tm, D), lambda j, i: (i, 0)), pl.BlockSpec((tn, D), lambda j, i: (j, 0)),
                  t_spec, t_spec] + _dep_spec(dep),
        out_specs=(t_spec, t_spec),
        compiler_params=_params("parallel", "parallel"),
    )(dxo, wd, gate, up, *([] if dep is None else [dep]))


def _rope_tables(positions):
    half = QK_ROPE // 2
    inv_freq = ROPE_THETA ** (-jnp.arange(0, QK_ROPE, 2, dtype=F32) / QK_ROPE)
    ang = positions.astype(F32)[:, None] * inv_freq
    cos, sin = jnp.cos(ang), jnp.sin(ang)
    z = jnp.zeros_like(cos)
    zz = jnp.zeros((positions.shape[0], LANE - QK_ROPE), F32)
    c = jnp.concatenate([cos, cos, zz], axis=1)
    sa = jnp.concatenate([z, sin, zz], axis=1)
    sb = jnp.concatenate([-sin, z, zz], axis=1)
    return c, sa, sb


def _rotate(seg, c, sa, sb, sign):
    half = QK_ROPE // 2
    mix = pltpu.roll(seg, half, 1) * sa + pltpu.roll(seg, LANE - half, 1) * sb
    return seg * c + mix if sign > 0 else seg * c - mix


def _mixer_in(h, wa, wuq, wukv, gq, gkv, tabs, *, name, dep=None):
    T, D = h.shape
    HQ, QL = wuq.shape
    KVL = wukv.shape[0]
    H = HQ // HEAD_W
    o_q, o_kv, o_kr = POOL_DIM, POOL_DIM + QL, POOL_DIM + QL + KVL
    PA = o_kr + LANE
    assert wa.shape[0] >= PA
    tm = _tile(T, 256, 16)

    def body(h_ref, wa_ref, wuq_ref, wukv_ref, gq_ref, gkv_ref, c_ref, sa_ref, sb_ref, *rest):
        xp_ref, ql_ref, kvl_ref, qn_ref, kvn_ref, q_ref, kv_ref, kr_ref = rest[-8:]
        proj = lax.dot_general(h_ref[...], wa_ref[...], _NT, preferred_element_type=F32)
        xp_ref[...] = proj[:, :POOL_DIM]
        ql = proj[:, o_q:o_kv]
        kvl = proj[:, o_kv:o_kr]
        ql_ref[...] = ql
        kvl_ref[...] = kvl
        qn = (ql * _rstd(ql) * gq_ref[...]).astype(BF16)
        kvn = (kvl * _rstd(kvl) * gkv_ref[...]).astype(BF16)
        qn_ref[...] = qn
        kvn_ref[...] = kvn
        c, sa, sb = c_ref[...], sa_ref[...], sb_ref[...]
        q = lax.dot_general(qn, wuq_ref[...], _NT, preferred_element_type=F32)
        for hh in range(H):
            base = hh * HEAD_W
            q_ref[:, base:base + QK_NOPE] = q[:, base:base + QK_NOPE].astype(BF16)
            q_ref[:, base + QK_NOPE:base + HEAD_W] = _rotate(
                q[:, base + QK_NOPE:base + HEAD_W], c, sa, sb, 1).astype(BF16)
        kv_ref[...] = jnp.dot(kvn, wukv_ref[...], preferred_element_type=F32).astype(BF16)
        kr_ref[...] = _rotate(proj[:, o_kr:o_kr + LANE], c, sa, sb, 1).astype(BF16)

    def row(w):
        return pl.BlockSpec((tm, w), lambda i: (i, 0))

    def whole(arr):
        return pl.BlockSpec(arr.shape, lambda i: (0,) * arr.ndim)

    gq2, gkv2 = gq.reshape(1, QL), gkv.reshape(1, KVL)
    outs = [(POOL_DIM, F32), (QL, F32), (KVL, F32), (QL, BF16), (KVL, BF16), (HQ, BF16), (HQ, BF16), (LANE, BF16)]
    return pl.pallas_call(
        body, name=name,
        out_shape=tuple(jax.ShapeDtypeStruct((T, w), dt) for w, dt in outs),
        grid=(T // tm,),
        in_specs=[row(D), pl.BlockSpec((PA, D), lambda i: (0, 0)), whole(wuq), whole(wukv), whole(gq2), whole(gkv2),
                  row(LANE), row(LANE), row(LANE)] + _dep_spec(dep),
        out_specs=tuple(row(w) for w, _ in outs),
        compiler_params=_params("parallel"),
    )(h, wa, wuq, wukv, gq2, gkv2, *tabs, *([] if dep is None else [dep]))


def _mixer_in_bwd(dq, dkv, dkr, ql, kvl, dxp, wuq, wukv, gq, gkv, tabs, *, name):
    T, HQ = dq.shape
    QL, KVL = wuq.shape[1], wukv.shape[0]
    H = HQ // HEAD_W
    PA = POOL_DIM + QL + KVL + LANE
    o_q, o_kv, o_kr = POOL_DIM, POOL_DIM + QL, POOL_DIM + QL + KVL
    tm = _tile(T, 256, 16)

    def norm_bwd(lat, gain, dn):
        r = _rstd(lat)
        xhat = lat * r
        dxh = dn * gain
        dlat = r * (dxh - xhat * jnp.mean(dxh * xhat, axis=-1, keepdims=True))
        return dlat, jnp.sum(dn * xhat, axis=0, keepdims=True)

    def body(dq_ref, dkv_ref, dkr_ref, ql_ref, kvl_ref, dxp_ref, wuq_ref, wukv_ref, gq_ref, gkv_ref,
             c_ref, sa_ref, sb_ref, dproj_ref, dqp_ref, dgq_ref, dgkv_ref):
        c, sa, sb = c_ref[...], sa_ref[...], sb_ref[...]
        dkr_sum = dkr_ref[:, :LANE]
        for hh in range(H):
            base = hh * HEAD_W
            dqp_ref[:, base:base + QK_NOPE] = dq_ref[:, base:base + QK_NOPE]
            dqp_ref[:, base + QK_NOPE:base + HEAD_W] = _rotate(
                dq_ref[:, base + QK_NOPE:base + HEAD_W].astype(F32), c, sa, sb, -1).astype(BF16)
            if hh:
                dkr_sum = dkr_sum + dkr_ref[:, hh * LANE:(hh + 1) * LANE]
        dqn = jnp.dot(dqp_ref[...], wuq_ref[...], preferred_element_type=F32)
        dkvn = lax.dot_general(dkv_ref[...], wukv_ref[...], _NT, preferred_element_type=F32)
        dql, dgq = norm_bwd(ql_ref[...], gq_ref[...], dqn)
        dkvl, dgkv = norm_bwd(kvl_ref[...], gkv_ref[...], dkvn)
        dproj_ref[:, :POOL_DIM] = dxp_ref[...].astype(BF16)
        dproj_ref[:, o_q:o_kv] = dql.astype(BF16)
        dproj_ref[:, o_kv:o_kr] = dkvl.astype(BF16)
        dproj_ref[:, o_kr:PA] = _rotate(dkr_sum, c, sa, sb, -1).astype(BF16)

        @pl.when(pl.program_id(0) == 0)
        def _():
            dgq_ref[...] = dgq
            dgkv_ref[...] = dgkv

        @pl.when(pl.program_id(0) > 0)
        def _():
            dgq_ref[...] += dgq
            dgkv_ref[...] += dgkv

    def row(w):
        return pl.BlockSpec((tm, w), lambda i: (i, 0))

    def whole(arr):
        return pl.BlockSpec(arr.shape, lambda i: (0,) * arr.ndim)

    gq2, gkv2 = gq.reshape(1, QL), gkv.reshape(1, KVL)
    return pl.pallas_call(
        body, name=name,
        out_shape=(jax.ShapeDtypeStruct((T, PA), BF16), jax.ShapeDtypeStruct((T, HQ), BF16),
                   jax.ShapeDtypeStruct((1, QL), F32), jax.ShapeDtypeStruct((1, KVL), F32)),
        grid=(T // tm,),
        in_specs=[row(HQ), row(HQ), row(H * LANE), row(QL), row(KVL), row(POOL_DIM), whole(wuq), whole(wukv),
                  whole(gq2), whole(gkv2), row(LANE), row(LANE), row(LANE)],
        out_specs=(row(PA), row(HQ), whole(gq2), whole(gkv2)),
        compiler_params=_params("arbitrary"),
    )(dq, dkv, dkr, ql, kvl, dxp, wuq, wukv, gq2, gkv2, *tabs)


def _pool_groups(x_of, S):
    row = lax.broadcasted_iota(jnp.int32, (S, POOL_G), 0)
    for g, w in enumerate(POOL_WINDOWS):
        x = x_of(g)
        s = x
        d = 1
        while d < w:
            s = s + jnp.where(row >= d, pltpu.roll(s, d, 0), 0.0)
            d *= 2
        cnt = jnp.minimum(row + 1, w).astype(F32)
        yield g, w, x, s / cnt - x, cnt, row


def _pool_fwd(xp, maps, scale, *, S, name):
    T = xp.shape[0]

    def body(xp_ref, maps_ref, scale_ref, ms_ref):
        for g, _, _, pooled, _, _ in _pool_groups(lambda g: xp_ref[:, g * POOL_G:(g + 1) * POOL_G], S):
            mixed = jnp.dot(pooled.astype(BF16), maps_ref[g].astype(BF16), preferred_element_type=F32)
            ms_ref[:, g * POOL_G:(g + 1) * POOL_G] = (mixed * scale_ref[:, g * POOL_G:(g + 1) * POOL_G]).astype(BF16)

    return pl.pallas_call(
        body, name=name,
        out_shape=jax.ShapeDtypeStruct((T, POOL_DIM), BF16),
        grid=(T // S,),
        in_specs=[pl.BlockSpec((S, POOL_DIM), lambda b: (b, 0)),
                  pl.BlockSpec(maps.shape, lambda b: (0, 0, 0)),
                  pl.BlockSpec((1, POOL_DIM), lambda b: (0, 0))],
        out_specs=pl.BlockSpec((S, POOL_DIM), lambda b: (b, 0)),
        compiler_params=_params("parallel"),
    )(xp, maps, scale.reshape(1, POOL_DIM))


def _pool_bwd(xp, dms, maps, scale, *, S, name):
    T = xp.shape[0]

    def body(xp_ref, dms_ref, maps_ref, scale_ref, dxp_ref, dmaps_ref, dscale_ref):
        first = pl.program_id(0) == 0
        for g, w, _, pooled, cnt, row in _pool_groups(lambda g: xp_ref[:, g * POOL_G:(g + 1) * POOL_G], S):
            cols = slice(g * POOL_G, (g + 1) * POOL_G)
            pooled_b = pooled.astype(BF16)
            maps_b = maps_ref[g].astype(BF16)
            mixed = jnp.dot(pooled_b, maps_b, preferred_element_type=F32)
            dms = dms_ref[:, cols]
            dscale = jnp.sum(dms * mixed, axis=0, keepdims=True)
            dmixed = (dms * scale_ref[:, cols]).astype(BF16)
            dmaps = lax.dot_general(pooled_b, dmixed, (((0,), (0,)), ((), ())), preferred_element_type=F32)
            dpooled = lax.dot_general(dmixed, maps_b, (((1,), (1,)), ((), ())), preferred_element_type=F32)
            z = dpooled / cnt
            d = 1
            while d < w:
                z = z + jnp.where(row < S - d, pltpu.roll(z, S - d, 0), 0.0)
                d *= 2
            dxp_ref[:, cols] = z - dpooled

            @pl.when(first)
            def _():
                dmaps_ref[g] = dmaps
                dscale_ref[:, cols] = dscale

            @pl.when(jnp.logical_not(first))
            def _():
                dmaps_ref[g] += dmaps
                dscale_ref[:, cols] += dscale

    seq = pl.BlockSpec((S, POOL_DIM), lambda b: (b, 0))
    maps_spec = pl.BlockSpec(maps.shape, lambda b: (0, 0, 0))
    vec = pl.BlockSpec((1, POOL_DIM), lambda b: (0, 0))
    return pl.pallas_call(
        body, name=name,
        out_shape=(jax.ShapeDtypeStruct((T, POOL_DIM), F32), jax.ShapeDtypeStruct(maps.shape, F32),
                   jax.ShapeDtypeStruct((1, POOL_DIM), F32)),
        grid=(T // S,),
        in_specs=[seq, seq, maps_spec, vec],
        out_specs=(seq, maps_spec, vec),
        compiler_params=_params("arbitrary"),
    )(xp, dms, maps, scale.reshape(1, POOL_DIM))


def _causal_mask(s, t):
    r = lax.broadcasted_iota(jnp.int32, (t, t), 0)
    c = lax.broadcasted_iota(jnp.int32, (t, t), 1)
    return jnp.where(r >= c, s, MASK_VALUE)


_NT = (((1,), (1,)), ((), ()))
_TN = (((0,), (0,)), ((), ()))


def _attn_fwd(q, kv, kr, *, S, name):
    T, HQ = q.shape
    H = HQ // HEAD_W
    B = T // S
    t = _tile(S, ATTN_TILE)
    n = S // t

    def body(q_ref, k_ref, v_ref, kr_ref, o_ref, lse_ref, kcat):
        kcat[:, :QK_NOPE] = k_ref[...]
        kcat[:, QK_NOPE:] = kr_ref[...]
        for i in range(n):
            rows = slice(i * t, (i + 1) * t)
            qt = q_ref[rows, :]
            m = jnp.full((t, 1), MASK_VALUE, F32)
            l = jnp.zeros((t, 1), F32)
            acc = jnp.zeros((t, V_DIM), F32)
            for j in range(i + 1):
                cols = slice(j * t, (j + 1) * t)
                s = lax.dot_general(qt, kcat[cols, :], _NT, preferred_element_type=F32) * ATTN_SCALE_LOG2
                if j == i:
                    s = _causal_mask(s, t)
                m_new = jnp.maximum(m, jnp.max(s, axis=1, keepdims=True))
                p = jnp.exp2(s - m_new)
                corr = jnp.exp2(m - m_new)
                l = corr * l + jnp.sum(p, axis=1, keepdims=True)
                acc = corr * acc + jnp.dot(p.astype(BF16), v_ref[cols, :], preferred_element_type=F32)
                m = m_new
            o_ref[rows, :] = (acc / l).astype(BF16)
            lse_ref[rows, :] = jnp.broadcast_to(m + jnp.log2(l), (t, LANE))

    seq_h = pl.BlockSpec((S, LANE), lambda b, h: (b, h))
    return pl.pallas_call(
        body, name=name,
        out_shape=(jax.ShapeDtypeStruct((T, H * V_DIM), BF16), jax.ShapeDtypeStruct((T, H * LANE), F32)),
        grid=(B, H),
        in_specs=[pl.BlockSpec((S, HEAD_W), lambda b, h: (b, h)),
                  pl.BlockSpec((S, QK_NOPE), lambda b, h: (b, 2 * h)),
                  pl.BlockSpec((S, V_DIM), lambda b, h: (b, 2 * h + 1)),
                  pl.BlockSpec((S, LANE), lambda b, h: (b, 0))],
        out_specs=(seq_h, seq_h),
        scratch_shapes=[pltpu.VMEM((S, HEAD_W), BF16)],
        compiler_params=_params("parallel", "parallel"),
    )(q, kv, kv, kr)


def _attn_bwd(q, kv, kr, o, do, lse, *, S, name):
    T, HQ = q.shape
    H = HQ // HEAD_W
    B = T // S
    t = _tile(S, ATTN_TILE)
    n = S // t

    def body(q_ref, k_ref, v_ref, kr_ref, o_ref, do_ref, lse_ref, dq_ref, dkv_ref, dkr_ref, kcat, dq_acc):
        kcat[:, :QK_NOPE] = k_ref[...]
        kcat[:, QK_NOPE:] = kr_ref[...]
        delta = [jnp.sum(do_ref[i * t:(i + 1) * t, :].astype(F32) * o_ref[i * t:(i + 1) * t, :].astype(F32),
                         axis=1, keepdims=True) for i in range(n)]
        for j in range(n):
            cols = slice(j * t, (j + 1) * t)
            kc = kcat[cols, :]
            vt = v_ref[cols, :]
            dk = jnp.zeros((t, HEAD_W), F32)
            dv = jnp.zeros((t, V_DIM), F32)
            for i in range(j, n):
                rows = slice(i * t, (i + 1) * t)
                qt = q_ref[rows, :]
                dot_ = do_ref[rows, :]
                s = lax.dot_general(qt, kc, _NT, preferred_element_type=F32) * ATTN_SCALE_LOG2
                if i == j:
                    s = _causal_mask(s, t)
                p = jnp.exp2(s - lse_ref[rows, :][:, :1])
                dv = dv + lax.dot_general(p.astype(BF16), dot_, _TN, preferred_element_type=F32)
                dp = lax.dot_general(dot_, vt, _NT, preferred_element_type=F32)
                ds = (p * (dp - delta[i]) * ATTN_SCALE).astype(BF16)
                dk = dk + lax.dot_general(ds, qt, _TN, preferred_element_type=F32)
                dq_part = jnp.dot(ds, kc, preferred_element_type=F32)
                if j == 0:
                    dq_acc[rows, :] = dq_part
                else:
                    dq_acc[rows, :] += dq_part
            dkv_ref[cols, :QK_NOPE] = dk[:, :QK_NOPE].astype(BF16)
            dkv_ref[cols, QK_NOPE:] = dv.astype(BF16)
            dkr_ref[cols, :] = dk[:, QK_NOPE:]
        dq_ref[...] = dq_acc[...].astype(BF16)

    seq_q = pl.BlockSpec((S, HEAD_W), lambda b, h: (b, h))
    seq_h = pl.BlockSpec((S, LANE), lambda b, h: (b, h))
    return pl.pallas_call(
        body, name=name,
        out_shape=(jax.ShapeDtypeStruct((T, HQ), BF16), jax.ShapeDtypeStruct((T, HQ), BF16),
                   jax.ShapeDtypeStruct((T, H * LANE), F32)),
        grid=(B, H),
        in_specs=[seq_q,
                  pl.BlockSpec((S, QK_NOPE), lambda b, h: (b, 2 * h)),
                  pl.BlockSpec((S, V_DIM), lambda b, h: (b, 2 * h + 1)),
                  pl.BlockSpec((S, LANE), lambda b, h: (b, 0)),
                  seq_h, seq_h, seq_h],
        out_specs=(seq_q, seq_q, seq_h),
        scratch_shapes=[pltpu.VMEM((S, HEAD_W), BF16), pltpu.VMEM((S, HEAD_W), F32)],
        compiler_params=_params("parallel", "parallel"),
    )(q, kv, kv, kr, o, do, lse)


def _merge_out(h, ms, o, x, wgate, bgate, wpp, wap, wout, next_gain, *, name):
    T, D = x.shape
    tm = _tile(T, 256, 16)

    def body(h_ref, ms_ref, o_ref, x_ref, wgate_ref, bgate_ref, wpp_ref, wap_ref, wout_ref, ng_ref,
             gates_ref, ba_ref, bb_ref, merged_ref, xn_ref, hn_ref):
        logits = lax.dot_general(h_ref[...], wgate_ref[...], _NT, preferred_element_type=F32) + bgate_ref[...]
        gates = jax.nn.sigmoid(logits)
        ba = jnp.dot(ms_ref[...], wpp_ref[...], preferred_element_type=F32)
        bb = jnp.dot(o_ref[...], wap_ref[...], preferred_element_type=F32)
        merged = (gates[:, :D] * ba + gates[:, D:] * bb).astype(BF16)
        gates_ref[...] = gates.astype(BF16)
        ba_ref[...] = ba.astype(BF16)
        bb_ref[...] = bb.astype(BF16)
        merged_ref[...] = merged
        xn = x_ref[...] + jnp.dot(merged, wout_ref[...], preferred_element_type=F32)
        xn_ref[...] = xn
        hn_ref[...] = (xn * _rstd(xn) * ng_ref[...]).astype(BF16)

    def row(w):
        return pl.BlockSpec((tm, w), lambda i: (i, 0))

    def whole(arr):
        return pl.BlockSpec(arr.shape, lambda i: (0,) * arr.ndim)

    bg2, ng2 = bgate.reshape(1, 2 * D), next_gain.reshape(1, D)
    act = jax.ShapeDtypeStruct((T, D), BF16)
    return pl.pallas_call(
        body, name=name,
        out_shape=(jax.ShapeDtypeStruct((T, 2 * D), BF16), act, act, act, jax.ShapeDtypeStruct((T, D), F32), act),
        grid=(T // tm,),
        in_specs=[row(D), row(ms.shape[1]), row(o.shape[1]), row(D), whole(wgate), whole(bg2), whole(wpp),
                  whole(wap), whole(wout), whole(ng2)],
        out_specs=(row(2 * D), row(D), row(D), row(D), row(D), row(D)),
        compiler_params=_params("parallel"),
    )(h, ms, o, x, wgate, bg2, wpp, wap, wout, ng2)


def _merge_bwd(dxo, wout, gates, ba, bb, *, name, dep=None):
    T, D = dxo.shape
    tm = _tile(T, 256, 16)

    def body(dxo_ref, wout_ref, gates_ref, ba_ref, bb_ref, *rest):
        dba_ref, dbb_ref, dgl_ref, dbg_ref = rest[-4:]
        dm = lax.dot_general(dxo_ref[...].astype(BF16), wout_ref[...], _NT, preferred_element_type=F32)
        ga = gates_ref[:, :D].astype(F32)
        gb = gates_ref[:, D:].astype(F32)
        dba_ref[...] = (dm * ga).astype(BF16)
        dbb_ref[...] = (dm * gb).astype(BF16)
        dgl_a = dm * ba_ref[...].astype(F32) * (ga * (1.0 - ga))
        dgl_b = dm * bb_ref[...].astype(F32) * (gb * (1.0 - gb))
        dgl_ref[:, :D] = dgl_a.astype(BF16)
        dgl_ref[:, D:] = dgl_b.astype(BF16)
        sa = jnp.sum(dgl_a, axis=0, keepdims=True)
        sb = jnp.sum(dgl_b, axis=0, keepdims=True)

        @pl.when(pl.program_id(0) == 0)
        def _():
            dbg_ref[:, :D] = sa
            dbg_ref[:, D:] = sb

        @pl.when(pl.program_id(0) > 0)
        def _():
            dbg_ref[:, :D] += sa
            dbg_ref[:, D:] += sb

    def row(w):
        return pl.BlockSpec((tm, w), lambda i: (i, 0))

    act = jax.ShapeDtypeStruct((T, D), BF16)
    return pl.pallas_call(
        body, name=name,
        out_shape=(act, act, jax.ShapeDtypeStruct((T, 2 * D), BF16), jax.ShapeDtypeStruct((1, 2 * D), F32)),
        grid=(T // tm,),
        in_specs=[row(D), pl.BlockSpec(wout.shape, lambda i: (0, 0)), row(2 * D), row(D), row(D)] + _dep_spec(dep),
        out_specs=(row(D), row(D), row(2 * D), pl.BlockSpec((1, 2 * D), lambda i: (0, 0))),
        compiler_params=_params("arbitrary"),
    )(dxo, wout, gates, ba, bb, *([] if dep is None else [dep]))


def _ffn_fwd(x, h, w, tag, next_gain, dep=None):
    gate, up, a = _ffn_up(h, w["up_t"], name=f"{tag}_up", dep=dep)
    if next_gain is None:
        xn, hn = _mm(a, w["wd"], res=x, alpha=0.5, name=f"{tag}_down_last", tk=2816), None
    else:
        xn, hn = _mm(a, w["wd"], res=x, alpha=0.5, norm_gain=next_gain, name=f"{tag}_down", tk=2816)
    return xn, hn, (x, h, gate, up, a)


def _ffn_bwd(dxo, gain, w, saved, tag, dep=None):
    x, h, gate, up, a = saved
    F = gate.shape[1]
    dgate, dup = _ffn_bwd_act(dxo, w["wd"], gate, up, alpha=0.5, name=f"{tag}_bwd_act", dep=dep)
    dwd = _mm(a, dxo, ta=True, alpha=0.5, out_dtype=BF16, name=f"{tag}_dwd", tm=1408, tn=1024, tk=1024)
    dup_t = _ffn_dw_up(dgate, dup, h, name=f"{tag}_dw_up")
    dx, dgain = _dh_norm_bwd(dgate, w["up_t"], dup, w["up_t"], F, x, gain, dxo, name=f"{tag}_dh_norm_bwd")
    return dx, dgain, dup_t, dwd


def _mixer_fwd(x, h, p, w, tabs, S, next_gain, dep=None):
    xp, ql, kvl, qn, kvn, q, kv, kr = _mixer_in(h, w["win_t"], w["wuq_t"], w["wukv"], p["q_latent_norm"],
                                                 p["kv_latent_norm"], tabs, name="mix_in", dep=dep)
    ms = _pool_fwd(xp, p["pool_maps"], p["pool_scale"], S=S, name="pool_fwd")
    o, lse = _attn_fwd(q, kv, kr, S=S, name="attn_fwd")
    gates, ba, bb, merged, xn, hn = _merge_out(h, ms, o, x, w["wgate_t"], p["b_gate"], w["wpp"], w["wap"], w["wout"],
                                               next_gain, name="merge_out")
    return xn, hn, (x, h, xp, ql, kvl, qn, kvn, q, kv, kr, ms, o, lse, gates, ba, bb, merged)


def _mixer_bwd(dxo, p, w, tabs, saved, S, dep=None):
    x, h, xp, ql, kvl, qn, kvn, q, kv, kr, ms, o, lse, gates, ba, bb, merged = saved
    dba, dbb, dgl, dbg = _merge_bwd(dxo, w["wout"], gates, ba, bb, name="merge_bwd", dep=dep)
    g = {}
    g["wout"] = _mm(merged, dxo, ta=True, out_dtype=BF16, name="d_wout", tm=1024, tk=1024)
    g["wpp"] = _mm(ms, dba, ta=True, out_dtype=BF16, name="d_wpp", tk=2048)
    g["wap"] = _mm(o, dbb, ta=True, out_dtype=BF16, name="d_wap", tm=1024, tk=2048)
    dms = _mm(dba, w["wpp"], tb=True, name="d_ms")
    do = _mm(dbb, w["wap"], tb=True, out_dtype=BF16, name="d_o")
    dxp, g["pool_maps"], g["pool_scale"] = _pool_bwd(xp, dms, p["pool_maps"], p["pool_scale"], S=S, name="pool_bwd")
    dq, dkv, dkr = _attn_bwd(q, kv, kr, o, do, lse, S=S, name="attn_bwd")
    dproj, dqp, g["q_latent_norm"], g["kv_latent_norm"] = _mixer_in_bwd(
        dq, dkv, dkr, ql, kvl, dxp, w["wuq_t"], w["wukv"], p["q_latent_norm"], p["kv_latent_norm"], tabs,
        name="mix_in_bwd")
    g["wuq_t"] = _mm(dqp, qn, ta=True, out_dtype=BF16, name="d_wuq", tm=2048, tk=2048)
    g["wukv"] = _mm(kvn, dkv, ta=True, out_dtype=BF16, name="d_wukv", tn=2048, tk=2048)
    g["wa_t"] = _mm(dproj, h, ta=True, out_dtype=BF16, name="d_wa", tm=1280, tn=1024, tk=2048)
    g["wgate_t"] = _mm(dgl, h, ta=True, out_dtype=BF16, name="d_wgate", tm=2048, tn=1024, tk=1024)
    dx, g["norm_mix"] = _dh_norm_bwd(dproj, w["win_t"], dgl, w["wgate_t"], 0, x, p["norm_mix"], dxo,
                                     name="mix_dh_norm_bwd")
    g["b_gate"] = dbg
    return dx, g


BIG = ("ffn1_up", "ffn1_down", "w_in", "w_pool_proj", "w_uq", "w_ukv", "w_attn_proj", "w_out", "ffn2_up", "ffn2_down")
SMALL = ("norm_ffn1", "norm_mix", "b_gate", "pool_maps", "pool_scale", "q_latent_norm", "kv_latent_norm", "norm_ffn2")
PACKED = ("w_pool_proj", "w_uq", "w_ukv")
TRANSPOSED = ("ffn1_up", "ffn2_up", "w_in", "w_uq")
COL_SHARDED = ("w_pool_proj", "w_ukv")
QK_HEAD = QK_NOPE + QK_ROPE


def _rows(stacked):
    n, r, c = stacked.shape
    return stacked.reshape(n * r, c)


def _cols(stacked):
    n, k, c = stacked.shape
    return stacked.transpose(1, 0, 2).reshape(k, n * c)


FFN1_PART = ("ffn1_up", "ffn1_down")
MIXER_PART = ("w_in", "w_attn_proj", "w_out") + PACKED
FFN2_PART = ("ffn2_up", "ffn2_down")


def _kernel_weights(stacked):
    full = {}
    for tag in ("ffn1", "ffn2"):
        if tag + "_up" in stacked:
            full[tag] = {"up_t": _rows(stacked[tag + "_up"]), "wd": _rows(stacked[tag + "_down"])}
    if "w_in" in stacked:
        win_t = _rows(stacked["w_in"])
        D = win_t.shape[1]
        wuq_t = _rows(stacked["w_uq"])
        QL = wuq_t.shape[1]
        H = wuq_t.shape[0] // QK_HEAD
        wuq_t = jnp.pad(wuq_t.reshape(H, QK_HEAD, QL), ((0, 0), (0, HEAD_W - QK_HEAD), (0, 0)))
        full.update({"win_t": win_t, "wgate_t": win_t[win_t.shape[0] - 2 * D:], "wuq_t": wuq_t.reshape(H * HEAD_W, QL),
                     "wukv": _cols(stacked["w_ukv"]), "wpp": _cols(stacked["w_pool_proj"]),
                     "wap": _rows(stacked["w_attn_proj"]), "wout": _rows(stacked["w_out"])})
    return full


def _split_rows(full):
    return full.reshape(N_DEV, full.shape[0] // N_DEV, full.shape[1])


def _split_cols(full):
    k, cols = full.shape
    return full.reshape(k, N_DEV, cols // N_DEV).transpose(1, 0, 2)


def _mixer_grads_stacked(g):
    n_a = g["wa_t"].shape[0] - (LANE - QK_ROPE)
    HQ, QL = g["wuq_t"].shape
    H = HQ // HEAD_W
    wuq_t = g["wuq_t"].reshape(H, HEAD_W, QL)[:, :QK_HEAD].reshape(H * QK_HEAD, QL)
    return {"w_in": _split_rows(jnp.concatenate([g["wa_t"][:n_a], g["wgate_t"]], axis=0)),
            "w_uq": _split_rows(wuq_t),
            "w_pool_proj": _split_cols(g["wpp"]), "w_ukv": _split_cols(g["wukv"]),
            "w_attn_proj": _split_rows(g["wap"]), "w_out": _split_rows(g["wout"])}


def _mesh_place():
    x, y, c = lax.axis_index("x"), lax.axis_index("y"), lax.axis_index("c")
    chips = [(1 - x, y), (x, 1 - y), (1 - x, 1 - y)]
    return x, y, c, chips


HBM = pl.BlockSpec(memory_space=pltpu.HBM)
SEMAPHORES = pl.BlockSpec(memory_space=pltpu.SEMAPHORE)
DATAFLOW = pltpu.SideEffectType.DATAFLOW_SIDE_EFFECTING
GATHER_PEERS = 4
SCATTER_PEERS = 7


def _in_hbm(a):
    return pltpu.with_memory_space_constraint(a, pltpu.HBM)


def _gather_plan(src_refs, land_refs):
    x, y, c, chips = _mesh_place()
    me = 4 * x + 2 * y + c
    targets = [(x, y, 1 - c)] + [(cx, cy, c) for cx, cy in chips]
    return [(s, land.at[me], to) for s, land in zip(src_refs, land_refs) for to in targets]


def _scatter_plan(src_refs, land_refs):
    x, y, c, _ = _mesh_place()
    peers = [(x, y, 1 - c), (1 - x, y, c), (x, 1 - y, c), (1 - x, 1 - y, c),
             (1 - x, y, 1 - c), (x, 1 - y, 1 - c), (1 - x, 1 - y, 1 - c)]
    return [(s.at[4 * px + 2 * py + pc], land.at[k], (px, py, pc))
            for s, land in zip(src_refs, land_refs) for k, (px, py, pc) in enumerate(peers)]


def _descriptors(plan, src_refs, land_refs, send_sems, recv_sems):
    return [pltpu.make_async_remote_copy(src_ref=s, dst_ref=d, send_sem=send_sems.at[k], recv_sem=recv_sems.at[k],
                                         device_id=to, device_id_type=MESH)
            for k, (s, d, to) in enumerate(plan(src_refs, land_refs))]


def _exchange(srcs, land_shapes, plan, per_src, *, name):
    n = len(srcs)

    def body(*refs):
        copies = _descriptors(plan, refs[:n], refs[n:2 * n], refs[2 * n], refs[2 * n + 1])
        for cp in copies:
            cp.start()
        for cp in copies:
            cp.wait()

    return pl.pallas_call(
        body, name=name,
        out_shape=tuple(jax.ShapeDtypeStruct(shape, s.dtype) for shape, s in zip(land_shapes, srcs)),
        in_specs=[ANY] * n, out_specs=(ANY,) * n,
        scratch_shapes=[pltpu.SemaphoreType.DMA((per_src * n,)), pltpu.SemaphoreType.DMA((per_src * n,))],
    )(*srcs)


FORWARD_COPIES = 4


def _forward_slots():
    x, y, c, chips = _mesh_place()
    return [4 * cx + 2 * cy + c for cx, cy in chips] + [4 * x + 2 * y + (1 - c)], (x, y, 1 - c)


def _forward_plan(src_refs, land_refs):
    slots, sibling = _forward_slots()
    return [(land.at[s], land.at[s], sibling) for land in land_refs for s in slots]


def _gather_all_plan(src_refs, land_refs):
    x, y, c, _ = _mesh_place()
    me = 4 * x + 2 * y + c
    peers = [(x, y, 1 - c), (1 - x, y, c), (x, 1 - y, c), (1 - x, 1 - y, c),
             (1 - x, y, 1 - c), (x, 1 - y, 1 - c), (1 - x, 1 - y, 1 - c)]
    return [(s, land.at[me], to) for s, land in zip(src_refs, land_refs) for to in peers]


def _exchange_start(srcs, lands, plan, n_copies, *, name):
    ns, n = len(srcs), len(srcs) + len(lands)

    def body(*refs):
        for cp in _descriptors(plan, refs[:ns], refs[ns:n], refs[n], refs[n + 1]):
            cp.start()
        refs[-1][...] = jnp.zeros_like(refs[-1])

    sems = pltpu.SemaphoreType.DMA((n_copies,))
    out = pl.pallas_call(
        body, name=name,
        out_shape=(sems, sems, *[pltpu.HBM(a.shape, a.dtype) for a in srcs + lands],
                   jax.ShapeDtypeStruct((8, LANE), F32)),
        in_specs=(HBM,) * n,
        out_specs=(SEMAPHORES, SEMAPHORES, *[HBM] * n, pl.BlockSpec(memory_space=pltpu.VMEM)),
        input_output_aliases={i: 2 + i for i in range(n)},
        compiler_params=pltpu.CompilerParams(has_side_effects=DATAFLOW),
    )(*[_in_hbm(a) for a in srcs + lands])
    return out[0], out[1], list(out[2:2 + ns]), list(out[2 + ns:2 + n]), out[-1]


def _exchange_wait(send_sems, recv_sems, srcs, lands, plan, after, *, name):
    ns, n = len(srcs), len(srcs) + len(lands)

    def body(*refs):
        for cp in _descriptors(plan, refs[:ns], refs[ns:n], refs[n], refs[n + 1]):
            cp.wait_send()
            cp.wait_recv()

    out = pl.pallas_call(
        body, name=name,
        out_shape=tuple(pltpu.HBM(a.shape, a.dtype) for a in srcs + lands),
        in_specs=(*[HBM] * n, SEMAPHORES, SEMAPHORES, ANY),
        out_specs=(HBM,) * n,
        input_output_aliases={i: i for i in range(n)},
        compiler_params=pltpu.CompilerParams(has_side_effects=DATAFLOW),
    )(*srcs, *lands, send_sems, recv_sems, after)
    return list(out[:ns]), list(out[ns:])


def _gather_forward(lands, *, name):
    n = len(lands)

    def body(*refs):
        in_refs, out_refs = refs[:n], refs[n:2 * n]
        token, send_sems, recv_sems = refs[2 * n:2 * n + 3]
        slots, sibling = _forward_slots()
        passed = [pltpu.make_async_remote_copy(
            src_ref=i.at[s], dst_ref=o.at[s],
            send_sem=send_sems.at[FORWARD_COPIES * b + j], recv_sem=recv_sems.at[FORWARD_COPIES * b + j],
            device_id=sibling, device_id_type=MESH)
            for b, (i, o) in enumerate(zip(in_refs, out_refs)) for j, s in enumerate(slots)]
        for cp in passed:
            cp.start()
        for cp in passed:
            cp.wait()
        token[...] = jnp.zeros_like(token)

    out = pl.pallas_call(
        body, name=name,
        out_shape=(*[jax.ShapeDtypeStruct(a.shape, a.dtype) for a in lands], jax.ShapeDtypeStruct((8, LANE), F32)),
        in_specs=[ANY] * n,
        out_specs=(*[ANY] * n, pl.BlockSpec(memory_space=pltpu.VMEM)),
        input_output_aliases={i: i for i in range(n)},
        scratch_shapes=[pltpu.SemaphoreType.DMA((FORWARD_COPIES * n,)), pltpu.SemaphoreType.DMA((FORWARD_COPIES * n,))],
    )(*lands)
    return list(out[:n]), out[n]


def _scatter_sum(parts, got, me, *, name):
    shard = parts.shape[1:]
    cols = shard[-1]
    rows = int(np.prod(shard[:-1]))
    tr = _tile(rows, 256, 16)

    def body(me_ref, p_ref, g_ref, o_ref):
        acc = p_ref[...].astype(F32)
        for k in range(SCATTER_PEERS):
            acc = acc + g_ref[k].astype(F32)
        o_ref[...] = acc

    out = pl.pallas_call(
        body, name=name,
        out_shape=jax.ShapeDtypeStruct((rows, cols), F32),
        grid_spec=pltpu.PrefetchScalarGridSpec(
            num_scalar_prefetch=1, grid=(rows // tr,),
            in_specs=[pl.BlockSpec((None, tr, cols), lambda r, me_ref: (me_ref[0], r, 0)),
                      pl.BlockSpec((SCATTER_PEERS, tr, cols), lambda r, me_ref: (0, r, 0))],
            out_specs=pl.BlockSpec((tr, cols), lambda r, me_ref: (r, 0))),
        compiler_params=_params("parallel"),
    )(me, parts.reshape(N_DEV, rows, cols), got.reshape(SCATTER_PEERS, rows, cols))
    return out.reshape(shard)


def _sum_devices(parts, *, name):
    _, R, C = parts.shape
    tr = _tile(R, 512, 8)

    def body(p_ref, o_ref):
        acc = p_ref[0]
        for d in range(1, N_DEV):
            acc = acc + p_ref[d]
        o_ref[...] = acc

    return pl.pallas_call(
        body, name=name,
        out_shape=jax.ShapeDtypeStruct((R, C), F32),
        grid=(R // tr,),
        in_specs=[pl.BlockSpec((N_DEV, tr, C), lambda r: (0, r, 0))],
        out_specs=pl.BlockSpec((tr, C), lambda r: (r, 0)),
        compiler_params=_params("parallel"),
    )(parts)


def _adamw(w, g, m, v, *, name, dep=None):
    shape = w.shape
    cols = shape[-1]
    rows = w.size // cols
    tr = _tile(rows, 256, 8)

    def body(w_ref, g_ref, m_ref, v_ref, *rest):
        d_ref, nm_ref, nv_ref = rest[-3:]
        g = g_ref[...]
        m = ADAM_B1 * m_ref[...] + (1.0 - ADAM_B1) * g
        v = ADAM_B2 * v_ref[...] + (1.0 - ADAM_B2) * jnp.square(g)
        m_hat = m / (1.0 - ADAM_B1 ** ADAM_STEP)
        v_hat = v / (1.0 - ADAM_B2 ** ADAM_STEP)
        d_ref[...] = -ADAM_LR * (m_hat / (jnp.sqrt(v_hat) + ADAM_EPS) + ADAM_WD * w_ref[...])
        nm_ref[...] = m
        nv_ref[...] = v

    spec = pl.BlockSpec((tr, cols), lambda i: (i, 0))
    out = jax.ShapeDtypeStruct((rows, cols), F32)
    d, nm, nv = pl.pallas_call(
        body, name=name,
        out_shape=(out, out, out),
        grid=(rows // tr,),
        in_specs=[spec] * 4 + _dep_spec(dep), out_specs=(spec,) * 3,
        compiler_params=_params("parallel"),
    )(*(a.reshape(rows, cols) for a in (w, g, m, v)), *([] if dep is None else [dep]))
    return d.reshape(shape), nm.reshape(shape), nv.reshape(shape)


PACK_ALIGN = 16 * LANE


def _pack(pieces, lead):
    out = []
    for p in pieces:
        keep = p.shape[:lead]
        flat = p.reshape(*keep, -1)
        pad = (-flat.shape[-1]) % PACK_ALIGN
        if pad:
            flat = jnp.pad(flat, [(0, 0)] * lead + [(0, pad)])
        out.append(flat.reshape(*keep, -1, LANE))
    return jnp.concatenate(out, axis=lead)


def _unpack(buf, shapes, lead):
    keep = buf.shape[:lead]
    out, row = [], 0
    for shape in shapes:
        size = int(np.prod(shape))
        rows = -(-size // PACK_ALIGN) * (PACK_ALIGN // LANE)
        piece = lax.slice_in_dim(buf, row, row + rows, axis=lead).reshape(*keep, rows * LANE)
        out.append(lax.slice_in_dim(piece, 0, size, axis=lead).reshape(*keep, *shape))
        row += rows
    return out


def kernel(x, positions, norm_ffn1, ffn1_up, ffn1_down, norm_mix, w_in, b_gate, pool_maps, pool_scale, w_pool_proj, q_latent_norm, w_uq, kv_latent_norm, w_ukv, w_attn_proj, w_out, norm_ffn2, ffn2_up, ffn2_down, final_norm, loss_target, m_norm_ffn1, m_ffn1_up, m_ffn1_down, m_norm_mix, m_w_in, m_b_gate, m_pool_maps, m_pool_scale, m_w_pool_proj, m_q_latent_norm, m_w_uq, m_kv_latent_norm, m_w_ukv, m_w_attn_proj, m_w_out, m_norm_ffn2, m_ffn2_up, m_ffn2_down, m_final_norm, v_norm_ffn1, v_ffn1_up, v_ffn1_down, v_norm_mix, v_w_in, v_b_gate, v_pool_maps, v_pool_scale, v_w_pool_proj, v_q_latent_norm, v_w_uq, v_kv_latent_norm, v_w_ukv, v_w_attn_proj, v_w_out, v_norm_ffn2, v_ffn2_up, v_ffn2_down, v_final_norm):
    order = ("norm_ffn1", "ffn1_up", "ffn1_down", "norm_mix", "w_in", "b_gate", "pool_maps", "pool_scale",
             "w_pool_proj", "q_latent_norm", "w_uq", "kv_latent_norm", "w_ukv", "w_attn_proj", "w_out",
             "norm_ffn2", "ffn2_up", "ffn2_down", "final_norm")
    w = dict(zip(order, (norm_ffn1, ffn1_up, ffn1_down, norm_mix, w_in, b_gate, pool_maps, pool_scale, w_pool_proj,
                         q_latent_norm, w_uq, kv_latent_norm, w_ukv, w_attn_proj, w_out, norm_ffn2, ffn2_up,
                         ffn2_down, final_norm)))
    m = dict(zip(order, (m_norm_ffn1, m_ffn1_up, m_ffn1_down, m_norm_mix, m_w_in, m_b_gate, m_pool_maps, m_pool_scale,
                         m_w_pool_proj, m_q_latent_norm, m_w_uq, m_kv_latent_norm, m_w_ukv, m_w_attn_proj, m_w_out,
                         m_norm_ffn2, m_ffn2_up, m_ffn2_down, m_final_norm)))
    v = dict(zip(order, (v_norm_ffn1, v_ffn1_up, v_ffn1_down, v_norm_mix, v_w_in, v_b_gate, v_pool_maps, v_pool_scale,
                         v_w_pool_proj, v_q_latent_norm, v_w_uq, v_kv_latent_norm, v_w_ukv, v_w_attn_proj, v_w_out,
                         v_norm_ffn2, v_ffn2_up, v_ffn2_down, v_final_norm)))
    L = norm_ffn1.shape[0]
    B, S, D = x.shape
    T = B * S

    def turned(a, n):
        return a.transpose(0, 2, 1) if n in TRANSPOSED else a

    wk, mk, vk = ({n: turned(d[n], n) for n in order} for d in (w, m, v))
    packed_shapes = [wk[n].shape[1:] for n in PACKED]
    my_slot = 4 * lax.axis_index("x") + 2 * lax.axis_index("y") + lax.axis_index("c")
    me = jnp.stack([my_slot]).astype(jnp.int32)

    def weight_blocks(l, names, token):
        zero = token[0, 0].astype(BF16)
        blocks = [wk[n][l].astype(BF16) + zero for n in names if n not in PACKED]
        if any(n in PACKED for n in names):
            blocks.append(_pack([wk[n][l].astype(BF16) + zero for n in PACKED], 0))
        return blocks

    def kernel_weights(names, lands):
        direct = [n for n in names if n not in PACKED]
        stacked = dict(zip(direct, lands))
        if len(lands) > len(direct):
            stacked.update(zip(PACKED, _unpack(lands[-1], packed_shapes, 1)))
        return _kernel_weights(stacked)

    def gather_start(l, names, token, tag):
        blocks = weight_blocks(l, names, token)
        lands = [lax.empty((N_DEV, *b.shape), b.dtype) for b in blocks]
        send_sems, recv_sems, blocks, lands, token = _exchange_start(
            blocks, lands, _gather_plan, GATHER_PEERS * len(blocks), name=f"gather_start_{tag}")
        return (send_sems, recv_sems, blocks, lands, tag), token

    def gather_wait(state, after):
        send_sems, recv_sems, blocks, lands, tag = state
        return _exchange_wait(send_sems, recv_sems, blocks, lands, _gather_plan, after, name=f"gather_wait_{tag}")[1]

    layer_part = FFN1_PART + MIXER_PART + FFN2_PART
    tabs = _rope_tables(positions.reshape(T))
    xs = x.reshape(T, D)
    h = _rms_fwd(xs, w["norm_ffn1"][0], name="first_norm")
    full, saved = [], []

    p = {n: w[n][0] for n in SMALL}
    blocks = weight_blocks(0, FFN1_PART, jnp.zeros((8, LANE), F32))
    lands = _exchange(blocks, [(N_DEV, *b.shape) for b in blocks], _gather_plan, GATHER_PEERS, name="gather_first")
    lands, token = _gather_forward(lands, name="gather_forward")
    w0 = kernel_weights(FFN1_PART, lands)
    state, token = gather_start(0, MIXER_PART, token, "0_mix")
    xs, h, s1 = _ffn_fwd(xs, h, w0["ffn1"], "ffn1", p["norm_mix"], dep=token)
    lands, token = _gather_forward(gather_wait(state, xs), name="gather_forward")
    w0.update(kernel_weights(MIXER_PART, lands))
    state, token = gather_start(0, FFN2_PART, token, "0_ffn2")
    if L > 1:
        next_state, token = gather_start(1, layer_part, token, "1")
    xs, h, s2 = _mixer_fwd(xs, h, p, w0, tabs, S, p["norm_ffn2"], dep=token)
    lands, token = _gather_forward(gather_wait(state, xs), name="gather_forward")
    w0.update(kernel_weights(FFN2_PART, lands))
    xs, h, s3 = _ffn_fwd(xs, h, w0["ffn2"], "ffn2", w["norm_ffn1"][1] if L > 1 else None, dep=token)
    if L > 1:
        lands, token = _gather_forward(gather_wait(next_state, xs), name="gather_forward")
    full.append(w0)
    saved.append((s1, s2, s3))

    for l in range(1, L):
        full.append(kernel_weights(layer_part, lands))
        more = l + 1 < L
        p = {n: w[n][l] for n in SMALL}
        if more:
            state, token = gather_start(l + 1, layer_part, token, f"{l + 1}")
        xs, h, s1 = _ffn_fwd(xs, h, full[l]["ffn1"], "ffn1", p["norm_mix"], dep=token if more else None)
        xs, h, s2 = _mixer_fwd(xs, h, p, full[l], tabs, S, p["norm_ffn2"])
        if more:
            lands = gather_wait(state, xs)
            send_sems, recv_sems, _, lands, token = _exchange_start(
                [], lands, _forward_plan, FORWARD_COPIES * len(lands), name=f"forward_start_{l + 1}")
        xs, h, s3 = _ffn_fwd(xs, h, full[l]["ffn2"], "ffn2", w["norm_ffn1"][l + 1] if more else None,
                             dep=token if more else None)
        if more:
            _, lands = _exchange_wait(send_sems, recv_sems, [], lands, _forward_plan, xs, name=f"forward_wait_{l + 1}")
        saved.append((s1, s2, s3))
    dx, dfinal, loss = _loss_head(xs, final_norm, loss_target.reshape(T, D), name="loss_head")

    big_grads = {n: [None] * L for n in BIG}
    small_grads_of = [None] * L
    pending = None

    def scatter_start(names, stacked, tag):
        srcs = [stacked[n] for n in names if n not in PACKED]
        if any(n in PACKED for n in names):
            srcs.append(_pack([stacked[n] for n in PACKED], 1))
        lands = [lax.empty((SCATTER_PEERS, *s.shape[1:]), s.dtype) for s in srcs]
        send_sems, recv_sems, srcs, lands, token = _exchange_start(
            srcs, lands, _scatter_plan, SCATTER_PEERS * len(srcs), name=f"scatter_start_{tag}")
        return (names, send_sems, recv_sems, srcs, lands, tag), token

    def scatter_finish(state, after, l):
        names, send_sems, recv_sems, srcs, lands, tag = state
        srcs, got = _exchange_wait(send_sems, recv_sems, srcs, lands, _scatter_plan, after, name=f"scatter_wait_{tag}")
        sums = [_scatter_sum(s, g, me, name="scatter_sum") for s, g in zip(srcs, got)]
        direct = [n for n in names if n not in PACKED]
        for n, g in zip(direct, sums):
            big_grads[n][l] = g
        if len(sums) > len(direct):
            for n, g in zip(PACKED, _unpack(sums[-1], packed_shapes, 0)):
                big_grads[n][l] = g

    dep = None
    for l in reversed(range(L)):
        p = {n: w[n][l] for n in SMALL}
        s1, s2, s3 = saved[l]
        small_g = {}
        dx, small_g["norm_ffn2"], dup_t, dwd = _ffn_bwd(dx, p["norm_ffn2"], full[l]["ffn2"], s3, "ffn2", dep=dep)
        if pending is not None:
            scatter_finish(pending[0], dx, pending[1])
        stacked = {"ffn2_up": _split_rows(dup_t), "ffn2_down": _split_rows(dwd)}
        state, dep = scatter_start(("ffn2_up", "ffn2_down"), stacked, f"ffn2_{l}")
        pending = (state, l)

        dx, gm = _mixer_bwd(dx, p, full[l], tabs, s2, S, dep=dep)
        scatter_finish(pending[0], dx, pending[1])
        names = ("w_in", "w_attn_proj", "w_out") + PACKED
        state, dep = scatter_start(names, _mixer_grads_stacked(gm), f"mix_{l}")
        pending = (state, l)
        small_g.update({n: gm[n] for n in SMALL if n in gm})

        dx, small_g["norm_ffn1"], dup_t, dwd = _ffn_bwd(dx, p["norm_ffn1"], full[l]["ffn1"], s1, "ffn1", dep=dep)
        scatter_finish(pending[0], dx, pending[1])
        stacked = {"ffn1_up": _split_rows(dup_t), "ffn1_down": _split_rows(dwd)}
        state, dep = scatter_start(("ffn1_up", "ffn1_down"), stacked, f"ffn1_{l}")
        pending = (state, l)
        small_grads_of[l] = small_g
    grad_x = dx.reshape(B, S, D)

    small_parts = [small_grads_of[l][n] for l in range(L) for n in SMALL] + [dfinal, loss[0, :1]]
    small_shapes = [p.shape for p in small_parts]
    vec = _pack([jnp.concatenate([p.reshape(-1) for p in small_parts])], 0)
    small_send, small_recv, vec_thru, small_land, small_token = _exchange_start(
        [vec], [lax.empty((N_DEV, *vec.shape), F32)], _gather_all_plan, SCATTER_PEERS, name="small_start")

    gk, grad, delta, new_m, new_v = {}, {}, {}, {}, {}

    def update(n, dep=None):
        wn, gn, mn, vn = (a.reshape(1, -1) if a.ndim == 1 else a for a in (wk[n], gk[n], mk[n], vk[n]))
        d, nm, nv = _adamw(wn, gn, mn, vn, name="adamw_" + n, dep=dep)
        grad[n] = turned(gk[n], n)
        delta[n], new_m[n], new_v[n] = (turned(a.reshape(wk[n].shape), n) for a in (d, nm, nv))

    last_block = ("ffn1_up", "ffn1_down")
    deps = [dep, small_token]
    for n in BIG:
        if n not in last_block:
            gk[n] = jnp.stack(big_grads[n])
            update(n, deps.pop(0) if deps else None)
    scatter_finish(pending[0], new_v["ffn2_down"], pending[1])
    for n in last_block:
        gk[n] = jnp.stack(big_grads[n])
        update(n)

    vec_thru, small_land = _exchange_wait(small_send, small_recv, vec_thru, small_land, _gather_all_plan,
                                          new_v["ffn1_down"], name="small_wait")
    parts = lax.dynamic_update_index_in_dim(small_land[0], vec_thru[0], my_slot, 0)
    flat = _sum_devices(parts, name="sum_small").reshape(-1)
    small_grads, at = [], 0
    for shape in small_shapes:
        size = int(np.prod(shape))
        small_grads.append(lax.slice_in_dim(flat, at, at + size).reshape(shape))
        at += size
    loss_total = small_grads[-1].reshape(())
    for i, n in enumerate(SMALL):
        gk[n] = jnp.stack([small_grads[l * len(SMALL) + i] for l in range(L)]).reshape(w[n].shape)
        update(n)
    gk["final_norm"] = small_grads[-2].reshape(final_norm.shape)
    update("final_norm")
    return (loss_total, grad_x, *[grad[n] for n in order], *[delta[n] for n in order],
            *[new_m[n] for n in order], *[new_v[n] for n in order])
```

```python
import functools

import numpy as np
import jax
import jax.numpy as jnp
from jax import lax
from jax.experimental import pallas as pl
from jax.experimental.pallas import tpu as pltpu

F32 = jnp.float32
BF16 = jnp.bfloat16

NORM_EPS = 1e-6
ROPE_THETA = 10000.0
QK_NOPE = 128
QK_ROPE = 64
V_DIM = 128
HEAD_W = 256
POOL_WINDOWS = (2, 4, 8, 16)
POOL_G = 128
POOL_DIM = 512
LANE = 128
ATTN_SCALE = float((QK_NOPE + QK_ROPE) ** -0.5)
ATTN_SCALE_LOG2 = ATTN_SCALE * float(np.log2(np.e))
MASK_VALUE = -1e30
ATTN_TILE = 512

ADAM_LR = 0.001
ADAM_B1 = 0.9
ADAM_B2 = 0.999
ADAM_EPS = 1e-08
ADAM_WD = 0.01
ADAM_STEP = 10

N_DEV = 8
VMEM_LIMIT = 52 * 1024 * 1024

MESH = pl.DeviceIdType.MESH
ANY = pl.BlockSpec(memory_space=pl.ANY)


def _tile(dim, target, align=LANE):
    if dim <= target:
        return dim
    t = (target // align) * align
    while t >= align:
        if dim % t == 0:
            return t
        t -= align
    return dim


def _params(*sem):
    return pltpu.CompilerParams(dimension_semantics=sem, vmem_limit_bytes=VMEM_LIMIT)


def _rstd(x):
    return lax.rsqrt(jnp.mean(x * x, axis=-1, keepdims=True) + NORM_EPS)


def _mm(a, b, *, name, ta=False, tb=False, out_dtype=F32, res=None, alpha=1.0, tm=512, tn=1024, tk=1024, b_row0=0,
        norm_gain=None, dep=None):
    if ta:
        K, M = a.shape
    else:
        M, K = a.shape
    if tb:
        N, K2 = b.shape
    else:
        K2, N = b.shape
    assert K == K2 or (not tb and K2 >= b_row0 + K), (a.shape, b.shape, ta, tb)
    tm, tn, tk = _tile(M, tm), _tile(N, tn), _tile(K, tk)
    nk = K // tk
    assert b_row0 % tk == 0
    kb0 = b_row0 // tk
    dims = (((0 if ta else 1,), (1 if tb else 0,)), ((), ()))
    has_res = res is not None
    has_norm = norm_gain is not None
    assert not has_norm or tn == N
    n_in = 2 + has_res + has_norm + (dep is not None)

    def body(*refs):
        a_ref, b_ref = refs[0], refs[1]
        res_ref = refs[2] if has_res else None
        gain_ref = refs[2 + has_res] if has_norm else None
        o_ref = refs[n_in]
        h_ref = refs[n_in + 1] if has_norm else None
        acc_ref = refs[n_in + 1 + has_norm] if nk > 1 else None
        part = lax.dot_general(a_ref[...].astype(BF16), b_ref[...].astype(BF16), dims,
                               preferred_element_type=F32)

        def finish(acc):
            r = acc * alpha if alpha != 1.0 else acc
            if has_res:
                r = res_ref[...].astype(F32) + r
            o_ref[...] = r.astype(out_dtype)
            if has_norm:
                h_ref[...] = (r * _rstd(r) * gain_ref[...]).astype(BF16)

        if nk == 1:
            finish(part)
        else:
            k = pl.program_id(2)

            @pl.when(k == 0)
            def _():
                acc_ref[...] = part

            @pl.when(k > 0)
            def _():
                acc_ref[...] += part

            @pl.when(k == nk - 1)
            def _():
                finish(acc_ref[...])

    a_spec = pl.BlockSpec((tk, tm), lambda i, j, k: (k, i)) if ta else pl.BlockSpec((tm, tk), lambda i, j, k: (i, k))
    b_spec = (pl.BlockSpec((tn, tk), lambda i, j, k: (j, k)) if tb
              else pl.BlockSpec((tk, tn), lambda i, j, k: (k + kb0, j)))
    in_specs = [a_spec, b_spec]
    operands = [a, b]
    tile_spec = pl.BlockSpec((tm, tn), lambda i, j, k: (i, j))
    if has_res:
        in_specs.append(tile_spec)
        operands.append(res)
    if has_norm:
        in_specs.append(pl.BlockSpec((1, tn), lambda i, j, k: (0, j)))
        operands.append(norm_gain.reshape(1, N))
    if dep is not None:
        in_specs += _dep_spec(dep)
        operands.append(dep)
    out = pl.pallas_call(
        body, name=name,
        out_shape=(jax.ShapeDtypeStruct((M, N), out_dtype),) + ((jax.ShapeDtypeStruct((M, N), BF16),) if has_norm else ()),
        grid=(M // tm, N // tn, nk),
        in_specs=in_specs,
        out_specs=(tile_spec,) + ((tile_spec,) if has_norm else ()),
        scratch_shapes=[pltpu.VMEM((tm, tn), F32)] if nk > 1 else [],
        compiler_params=_params("parallel", "parallel", "arbitrary"),
    )(*operands)
    return out if has_norm else out[0]


def _rms_fwd(x, g, *, name):
    T, D = x.shape
    tm = _tile(T, 512, 16)

    def body(x_ref, g_ref, h_ref):
        x = x_ref[...]
        h_ref[...] = (x * _rstd(x) * g_ref[...]).astype(BF16)

    return pl.pallas_call(
        body, name=name,
        out_shape=jax.ShapeDtypeStruct((T, D), BF16),
        grid=(T // tm,),
        in_specs=[pl.BlockSpec((tm, D), lambda i: (i, 0)), pl.BlockSpec((1, D), lambda i: (0, 0))],
        out_specs=pl.BlockSpec((tm, D), lambda i: (i, 0)),
        compiler_params=_params("parallel"),
    )(x, g.reshape(1, D))


def _dh_norm_bwd(a1, b1, a2, b2, b2_row0, x, g, dxo, *, name):
    T, D = x.shape
    K1, K2 = a1.shape[1], a2.shape[1]
    assert b2_row0 % K2 == 0 and b1.shape[0] >= K1 and b2.shape[0] >= b2_row0 + K2
    tm = _tile(T, 256, 16)

    def body(a1_ref, b1_ref, a2_ref, b2_ref, x_ref, g_ref, dxo_ref, dx_ref, dg_ref):
        x = x_ref[...]
        r = _rstd(x)
        xhat = x * r
        dh = (jnp.dot(a1_ref[...], b1_ref[...], preferred_element_type=F32)
              + jnp.dot(a2_ref[...], b2_ref[...], preferred_element_type=F32))
        dxh = dh * g_ref[...]
        dx_ref[...] = dxo_ref[...] + r * (dxh - xhat * jnp.mean(dxh * xhat, axis=-1, keepdims=True))
        part = jnp.sum(dh * xhat, axis=0, keepdims=True)

        @pl.when(pl.program_id(0) == 0)
        def _():
            dg_ref[...] = part

        @pl.when(pl.program_id(0) > 0)
        def _():
            dg_ref[...] += part

    row = pl.BlockSpec((tm, D), lambda i: (i, 0))
    vec = pl.BlockSpec((1, D), lambda i: (0, 0))
    return pl.pallas_call(
        body, name=name,
        out_shape=(jax.ShapeDtypeStruct((T, D), F32), jax.ShapeDtypeStruct((1, D), F32)),
        grid=(T // tm,),
        in_specs=[pl.BlockSpec((tm, K1), lambda i: (i, 0)), pl.BlockSpec((K1, D), lambda i: (0, 0)),
                  pl.BlockSpec((tm, K2), lambda i: (i, 0)), pl.BlockSpec((K2, D), lambda i: (b2_row0 // K2, 0)),
                  row, vec, row],
        out_specs=(row, vec),
        compiler_params=_params("arbitrary"),
    )(a1, b1, a2, b2, x, g.reshape(1, D), dxo)


def _loss_head(x, g, target, *, name):
    T, D = x.shape
    tm = _tile(T, 512, 16)

    def body(x_ref, g_ref, t_ref, dx_ref, dg_ref, loss_ref):
        x = x_ref[...]
        gain = g_ref[...]
        r = _rstd(x)
        xhat = x * r
        err = xhat * gain - t_ref[...]
        dy = err * (1.0 / D)
        dxh = dy * gain
        dx_ref[...] = r * (dxh - xhat * jnp.mean(dxh * xhat, axis=-1, keepdims=True))
        dg_part = jnp.sum(dy * xhat, axis=0, keepdims=True)
        loss_part = jnp.full((1, LANE), 0.5 / D, F32) * jnp.sum(err * err)

        @pl.when(pl.program_id(0) == 0)
        def _():
            dg_ref[...] = dg_part
            loss_ref[...] = loss_part

        @pl.when(pl.program_id(0) > 0)
        def _():
            dg_ref[...] += dg_part
            loss_ref[...] += loss_part

    row = pl.BlockSpec((tm, D), lambda i: (i, 0))
    vec = pl.BlockSpec((1, D), lambda i: (0, 0))
    return pl.pallas_call(
        body, name=name,
        out_shape=(jax.ShapeDtypeStruct((T, D), F32), jax.ShapeDtypeStruct((1, D), F32),
                   jax.ShapeDtypeStruct((1, LANE), F32)),
        grid=(T // tm,),
        in_specs=[row, vec, row],
        out_specs=(row, vec, pl.BlockSpec((1, LANE), lambda i: (0, 0))),
        compiler_params=_params("arbitrary"),
    )(x, g.reshape(1, D), target)


def _ffn_up(h, w_up_t, *, name, dep=None):
    T, D = h.shape
    F = w_up_t.shape[0] // 2
    tm, tn = _tile(T, 512, 16), _tile(F, 1408)
    nj = F // tn

    def body(h_ref, wg_ref, wu_ref, *rest):
        gate_ref, up_ref, a_ref = rest[-3:]
        h = h_ref[...]
        gate = lax.dot_general(h, wg_ref[...], _NT, preferred_element_type=F32)
        up = lax.dot_general(h, wu_ref[...], _NT, preferred_element_type=F32)
        gate_ref[...] = gate.astype(BF16)
        up_ref[...] = up.astype(BF16)
        a_ref[...] = (gate * jax.nn.sigmoid(gate) * up).astype(BF16)

    o_spec = pl.BlockSpec((tm, tn), lambda j, i: (i, j))
    out = jax.ShapeDtypeStruct((T, F), BF16)
    return pl.pallas_call(
        body, name=name,
        out_shape=(out, out, out),
        grid=(nj, T // tm),
        in_specs=[pl.BlockSpec((tm, D), lambda j, i: (i, 0)),
                  pl.BlockSpec((tn, D), lambda j, i: (j, 0)),
                  pl.BlockSpec((tn, D), lambda j, i: (j + nj, 0))] + _dep_spec(dep),
        out_specs=(o_spec, o_spec, o_spec),
        compiler_params=_params("parallel", "parallel"),
    )(h, w_up_t, w_up_t, *([] if dep is None else [dep]))


def _ffn_dw_up(dgate, dup, h, *, name):
    T, F = dgate.shape
    D = h.shape[1]
    tm, tk = _tile(F, 1408), _tile(T, 1024, 16)
    nf, nk = F // tm, T // tk

    def body(dgate_ref, dup_ref, h_ref, o_ref, acc_ref):
        i, k = pl.program_id(0), pl.program_id(1)

        def accumulate(part):
            @pl.when(k == 0)
            def _():
                acc_ref[...] = part

            @pl.when(k > 0)
            def _():
                acc_ref[...] += part

        @pl.when(i < nf)
        def _():
            accumulate(lax.dot_general(dgate_ref[...], h_ref[...], _TN, preferred_element_type=F32))

        @pl.when(i >= nf)
        def _():
            accumulate(lax.dot_general(dup_ref[...], h_ref[...], _TN, preferred_element_type=F32))

        @pl.when(k == nk - 1)
        def _():
            o_ref[...] = acc_ref[...].astype(BF16)

    return pl.pallas_call(
        body, name=name,
        out_shape=jax.ShapeDtypeStruct((2 * F, D), BF16),
        grid=(2 * nf, nk),
        in_specs=[pl.BlockSpec((tk, tm), lambda i, k: (jnp.where(i < nf, k, nk - 1), jnp.minimum(i, nf - 1))),
                  pl.BlockSpec((tk, tm), lambda i, k: (jnp.where(i < nf, 0, k), jnp.maximum(i - nf, 0))),
                  pl.BlockSpec((tk, D), lambda i, k: (k, 0))],
        out_specs=pl.BlockSpec((tm, D), lambda i, k: (i, 0)),
        scratch_shapes=[pltpu.VMEM((tm, D), F32)],
        compiler_params=_params("parallel", "arbitrary"),
    )(dgate, dup, h)


def _dep_spec(dep):
    return [] if dep is None else [pl.BlockSpec(dep.shape, lambda *_: (0,) * dep.ndim)]


def _ffn_bwd_act(dxo, wd, gate, up, *, alpha, name, dep=None):
    T, D = dxo.shape
    F = wd.shape[0]
    tm, tn = _tile(T, 256, 16), _tile(F, 2816)

    def body(dxo_ref, wd_ref, gate_ref, up_ref, *rest):
        dgate_ref, dup_ref = rest[-2:]
        da = lax.dot_general(dxo_ref[...].astype(BF16), wd_ref[...], (((1,), (1,)), ((), ())),
                             preferred_element_type=F32) * alpha
        gate = gate_ref[...].astype(F32)
        up = up_ref[...].astype(F32)
        sig = jax.nn.sigmoid(gate)
        dgate_ref[...] = (da * up * (sig * (1.0 + gate * (1.0 - sig)))).astype(BF16)
        dup_ref[...] = (da * (gate * sig)).astype(BF16)

    t_spec = pl.BlockSpec((tm, tn), lambda j, i: (i, j))
    out = jax.ShapeDtypeStruct((T, F), BF16)
    return pl.pallas_call(
        body, name=name,
        out_shape=(out, out),
        grid=(F // tn, T // tm),
        in_specs=[pl.BlockSpec((tm, D), lambda j, i: (i, 0)), pl.BlockSpec((tn, D), lambda j, i: (j, 0)),
                  t_spec, t_spec] + _dep_spec(dep),
        out_specs=(t_spec, t_spec),
        compiler_params=_params("parallel", "parallel"),
    )(dxo, wd, gate, up, *([] if dep is None else [dep]))


def _rope_tables(positions):
    half = QK_ROPE // 2
    inv_freq = ROPE_THETA ** (-jnp.arange(0, QK_ROPE, 2, dtype=F32) / QK_ROPE)
    ang = positions.astype(F32)[:, None] * inv_freq
    cos, sin = jnp.cos(ang), jnp.sin(ang)
    z = jnp.zeros_like(cos)
    zz = jnp.zeros((positions.shape[0], LANE - QK_ROPE), F32)
    c = jnp.concatenate([cos, cos, zz], axis=1)
    sa = jnp.concatenate([z, sin, zz], axis=1)
    sb = jnp.concatenate([-sin, z, zz], axis=1)
    return c, sa, sb


def _rotate(seg, c, sa, sb, sign):
    half = QK_ROPE // 2
    mix = pltpu.roll(seg, half, 1) * sa + pltpu.roll(seg, LANE - half, 1) * sb
    return seg * c + mix if sign > 0 else seg * c - mix


def _mixer_in(h, wa, wuq, wukv, gq, gkv, tabs, *, name, dep=None):
    T, D = h.shape
    HQ, QL = wuq.shape
    KVL = wukv.shape[0]
    H = HQ // HEAD_W
    o_q, o_kv, o_kr = POOL_DIM, POOL_DIM + QL, POOL_DIM + QL + KVL
    PA = o_kr + LANE
    assert wa.shape[0] >= PA
    tm = _tile(T, 512, 16)

    def body(h_ref, wa_ref, wuq_ref, wukv_ref, gq_ref, gkv_ref, c_ref, sa_ref, sb_ref, *rest):
        xp_ref, ql_ref, kvl_ref, qn_ref, kvn_ref, q_ref, kv_ref, kr_ref = rest[-8:]
        proj = lax.dot_general(h_ref[...], wa_ref[...], _NT, preferred_element_type=F32)
        xp_ref[...] = proj[:, :POOL_DIM]
        ql = proj[:, o_q:o_kv]
        kvl = proj[:, o_kv:o_kr]
        ql_ref[...] = ql
        kvl_ref[...] = kvl
        qn = (ql * _rstd(ql) * gq_ref[...]).astype(BF16)
        kvn = (kvl * _rstd(kvl) * gkv_ref[...]).astype(BF16)
        qn_ref[...] = qn
        kvn_ref[...] = kvn
        c, sa, sb = c_ref[...], sa_ref[...], sb_ref[...]
        q = lax.dot_general(qn, wuq_ref[...], _NT, preferred_element_type=F32)
        for hh in range(H):
            base = hh * HEAD_W
            q_ref[:, base:base + QK_NOPE] = q[:, base:base + QK_NOPE].astype(BF16)
            q_ref[:, base + QK_NOPE:base + HEAD_W] = _rotate(
                q[:, base + QK_NOPE:base + HEAD_W], c, sa, sb, 1).astype(BF16)
        kv_ref[...] = jnp.dot(kvn, wukv_ref[...], preferred_element_type=F32).astype(BF16)
        kr_ref[...] = _rotate(proj[:, o_kr:o_kr + LANE], c, sa, sb, 1).astype(BF16)

    def row(w):
        return pl.BlockSpec((tm, w), lambda i: (i, 0))

    def whole(arr):
        return pl.BlockSpec(arr.shape, lambda i: (0,) * arr.ndim)

    gq2, gkv2 = gq.reshape(1, QL), gkv.reshape(1, KVL)
    outs = [(POOL_DIM, F32), (QL, F32), (KVL, F32), (QL, BF16), (KVL, BF16), (HQ, BF16), (HQ, BF16), (LANE, BF16)]
    return pl.pallas_call(
        body, name=name,
        out_shape=tuple(jax.ShapeDtypeStruct((T, w), dt) for w, dt in outs),
        grid=(T // tm,),
        in_specs=[row(D), pl.BlockSpec((PA, D), lambda i: (0, 0)), whole(wuq), whole(wukv), whole(gq2), whole(gkv2),
                  row(LANE), row(LANE), row(LANE)] + _dep_spec(dep),
        out_specs=tuple(row(w) for w, _ in outs),
        compiler_params=_params("parallel"),
    )(h, wa, wuq, wukv, gq2, gkv2, *tabs, *([] if dep is None else [dep]))


def _mixer_in_bwd(dq, dkv, dkr, ql, kvl, dxp, wuq, wukv, gq, gkv, tabs, *, name):
    T, HQ = dq.shape
    QL, KVL = wuq.shape[1], wukv.shape[0]
    H = HQ // HEAD_W
    PA = POOL_DIM + QL + KVL + LANE
    o_q, o_kv, o_kr = POOL_DIM, POOL_DIM + QL, POOL_DIM + QL + KVL
    tm = _tile(T, 512, 16)

    def norm_bwd(lat, gain, dn):
        r = _rstd(lat)
        xhat = lat * r
        dxh = dn * gain
        dlat = r * (dxh - xhat * jnp.mean(dxh * xhat, axis=-1, keepdims=True))
        return dlat, jnp.sum(dn * xhat, axis=0, keepdims=True)

    def body(dq_ref, dkv_ref, dkr_ref, ql_ref, kvl_ref, dxp_ref, wuq_ref, wukv_ref, gq_ref, gkv_ref,
             c_ref, sa_ref, sb_ref, dproj_ref, dqp_ref, dgq_ref, dgkv_ref):
        c, sa, sb = c_ref[...], sa_ref[...], sb_ref[...]
        for hh in range(H):
            base = hh * HEAD_W
            dqp_ref[:, base:base + QK_NOPE] = dq_ref[:, base:base + QK_NOPE]
            dqp_ref[:, base + QK_NOPE:base + HEAD_W] = _rotate(
                dq_ref[:, base + QK_NOPE:base + HEAD_W].astype(F32), c, sa, sb, -1).astype(BF16)
        dqn = jnp.dot(dqp_ref[...], wuq_ref[...], preferred_element_type=F32)
        dkvn = lax.dot_general(dkv_ref[...], wukv_ref[...], _NT, preferred_element_type=F32)
        dql, dgq = norm_bwd(ql_ref[...], gq_ref[...], dqn)
        dkvl, dgkv = norm_bwd(kvl_ref[...], gkv_ref[...], dkvn)
        dproj_ref[:, :POOL_DIM] = dxp_ref[...].astype(BF16)
        dproj_ref[:, o_q:o_kv] = dql.astype(BF16)
        dproj_ref[:, o_kv:o_kr] = dkvl.astype(BF16)
        dproj_ref[:, o_kr:PA] = _rotate(dkr_ref[...], c, sa, sb, -1).astype(BF16)

        @pl.when(pl.program_id(0) == 0)
        def _():
            dgq_ref[...] = dgq
            dgkv_ref[...] = dgkv

        @pl.when(pl.program_id(0) > 0)
        def _():
            dgq_ref[...] += dgq
            dgkv_ref[...] += dgkv

    def row(w):
        return pl.BlockSpec((tm, w), lambda i: (i, 0))

    def whole(arr):
        return pl.BlockSpec(arr.shape, lambda i: (0,) * arr.ndim)

    gq2, gkv2 = gq.reshape(1, QL), gkv.reshape(1, KVL)
    return pl.pallas_call(
        body, name=name,
        out_shape=(jax.ShapeDtypeStruct((T, PA), BF16), jax.ShapeDtypeStruct((T, HQ), BF16),
                   jax.ShapeDtypeStruct((1, QL), F32), jax.ShapeDtypeStruct((1, KVL), F32)),
        grid=(T // tm,),
        in_specs=[row(HQ), row(HQ), row(LANE), row(QL), row(KVL), row(POOL_DIM), whole(wuq), whole(wukv),
                  whole(gq2), whole(gkv2), row(LANE), row(LANE), row(LANE)],
        out_specs=(row(PA), row(HQ), whole(gq2), whole(gkv2)),
        compiler_params=_params("arbitrary"),
    )(dq, dkv, dkr, ql, kvl, dxp, wuq, wukv, gq2, gkv2, *tabs)


def _pool_groups(x_of, S):
    row = lax.broadcasted_iota(jnp.int32, (S, POOL_G), 0)
    for g, w in enumerate(POOL_WINDOWS):
        x = x_of(g)
        s = x
        d = 1
        while d < w:
            s = s + jnp.where(row >= d, pltpu.roll(s, d, 0), 0.0)
            d *= 2
        cnt = jnp.minimum(row + 1, w).astype(F32)
        yield g, w, x, s / cnt - x, cnt, row


def _pool_fwd(xp, maps, scale, *, S, name):
    T = xp.shape[0]

    def body(xp_ref, maps_ref, scale_ref, ms_ref):
        for g, _, _, pooled, _, _ in _pool_groups(lambda g: xp_ref[:, g * POOL_G:(g + 1) * POOL_G], S):
            mixed = jnp.dot(pooled.astype(BF16), maps_ref[g].astype(BF16), preferred_element_type=F32)
            ms_ref[:, g * POOL_G:(g + 1) * POOL_G] = (mixed * scale_ref[:, g * POOL_G:(g + 1) * POOL_G]).astype(BF16)

    return pl.pallas_call(
        body, name=name,
        out_shape=jax.ShapeDtypeStruct((T, POOL_DIM), BF16),
        grid=(T // S,),
        in_specs=[pl.BlockSpec((S, POOL_DIM), lambda b: (b, 0)),
                  pl.BlockSpec(maps.shape, lambda b: (0, 0, 0)),
                  pl.BlockSpec((1, POOL_DIM), lambda b: (0, 0))],
        out_specs=pl.BlockSpec((S, POOL_DIM), lambda b: (b, 0)),
        compiler_params=_params("parallel"),
    )(xp, maps, scale.reshape(1, POOL_DIM))


def _pool_bwd(xp, dms, maps, scale, *, S, name):
    T = xp.shape[0]

    def body(xp_ref, dms_ref, maps_ref, scale_ref, dxp_ref, dmaps_ref, dscale_ref):
        first = pl.program_id(0) == 0
        for g, w, _, pooled, cnt, row in _pool_groups(lambda g: xp_ref[:, g * POOL_G:(g + 1) * POOL_G], S):
            cols = slice(g * POOL_G, (g + 1) * POOL_G)
            pooled_b = pooled.astype(BF16)
            maps_b = maps_ref[g].astype(BF16)
            mixed = jnp.dot(pooled_b, maps_b, preferred_element_type=F32)
            dms = dms_ref[:, cols]
            dscale = jnp.sum(dms * mixed, axis=0, keepdims=True)
            dmixed = (dms * scale_ref[:, cols]).astype(BF16)
            dmaps = lax.dot_general(pooled_b, dmixed, (((0,), (0,)), ((), ())), preferred_element_type=F32)
            dpooled = lax.dot_general(dmixed, maps_b, (((1,), (1,)), ((), ())), preferred_element_type=F32)
            z = dpooled / cnt
            d = 1
            while d < w:
                z = z + jnp.where(row < S - d, pltpu.roll(z, S - d, 0), 0.0)
                d *= 2
            dxp_ref[:, cols] = z - dpooled

            @pl.when(first)
            def _():
                dmaps_ref[g] = dmaps
                dscale_ref[:, cols] = dscale

            @pl.when(jnp.logical_not(first))
            def _():
                dmaps_ref[g] += dmaps
                dscale_ref[:, cols] += dscale

    seq = pl.BlockSpec((S, POOL_DIM), lambda b: (b, 0))
    maps_spec = pl.BlockSpec(maps.shape, lambda b: (0, 0, 0))
    vec = pl.BlockSpec((1, POOL_DIM), lambda b: (0, 0))
    return pl.pallas_call(
        body, name=name,
        out_shape=(jax.ShapeDtypeStruct((T, POOL_DIM), F32), jax.ShapeDtypeStruct(maps.shape, F32),
                   jax.ShapeDtypeStruct((1, POOL_DIM), F32)),
        grid=(T // S,),
        in_specs=[seq, seq, maps_spec, vec],
        out_specs=(seq, maps_spec, vec),
        compiler_params=_params("arbitrary"),
    )(xp, dms, maps, scale.reshape(1, POOL_DIM))


def _causal_mask(s, t):
    r = lax.broadcasted_iota(jnp.int32, (t, t), 0)
    c = lax.broadcasted_iota(jnp.int32, (t, t), 1)
    return jnp.where(r >= c, s, MASK_VALUE)


_NT = (((1,), (1,)), ((), ()))
_TN = (((0,), (0,)), ((), ()))


def _attn_fwd(q, kv, kr, *, S, name):
    T, HQ = q.shape
    H = HQ // HEAD_W
    B = T // S
    t = _tile(S, ATTN_TILE)
    n = S // t

    def body(q_ref, k_ref, v_ref, kr_ref, o_ref, lse_ref, kcat):
        kcat[:, :QK_NOPE] = k_ref[...]
        kcat[:, QK_NOPE:] = kr_ref[...]
        for i in range(n):
            rows = slice(i * t, (i + 1) * t)
            qt = q_ref[rows, :]
            m = jnp.full((t, 1), MASK_VALUE, F32)
            l = jnp.zeros((t, 1), F32)
            acc = jnp.zeros((t, V_DIM), F32)
            for j in range(i + 1):
                cols = slice(j * t, (j + 1) * t)
                s = lax.dot_general(qt, kcat[cols, :], _NT, preferred_element_type=F32) * ATTN_SCALE_LOG2
                if j == i:
                    s = _causal_mask(s, t)
                m_new = jnp.maximum(m, jnp.max(s, axis=1, keepdims=True))
                p = jnp.exp2(s - m_new)
                corr = jnp.exp2(m - m_new)
                l = corr * l + jnp.sum(p, axis=1, keepdims=True)
                acc = corr * acc + jnp.dot(p.astype(BF16), v_ref[cols, :], preferred_element_type=F32)
                m = m_new
            o_ref[rows, :] = (acc / l).astype(BF16)
            lse_ref[rows, :] = jnp.broadcast_to(m + jnp.log2(l), (t, LANE))

    seq_h = pl.BlockSpec((S, LANE), lambda b, h: (b, h))
    return pl.pallas_call(
        body, name=name,
        out_shape=(jax.ShapeDtypeStruct((T, H * V_DIM), BF16), jax.ShapeDtypeStruct((T, H * LANE), F32)),
        grid=(B, H),
        in_specs=[pl.BlockSpec((S, HEAD_W), lambda b, h: (b, h)),
                  pl.BlockSpec((S, QK_NOPE), lambda b, h: (b, 2 * h)),
                  pl.BlockSpec((S, V_DIM), lambda b, h: (b, 2 * h + 1)),
                  pl.BlockSpec((S, LANE), lambda b, h: (b, 0))],
        out_specs=(seq_h, seq_h),
        scratch_shapes=[pltpu.VMEM((S, HEAD_W), BF16)],
        compiler_params=_params("parallel", "parallel"),
    )(q, kv, kv, kr)


def _attn_bwd(q, kv, kr, o, do, lse, *, S, name):
    T, HQ = q.shape
    H = HQ // HEAD_W
    B = T // S
    t = _tile(S, ATTN_TILE)
    n = S // t

    def body(q_ref, k_ref, v_ref, kr_ref, o_ref, do_ref, lse_ref, dq_ref, dkv_ref, dkr_ref, kcat, dq_acc):
        kcat[:, :QK_NOPE] = k_ref[...]
        kcat[:, QK_NOPE:] = kr_ref[...]
        delta = [jnp.sum(do_ref[i * t:(i + 1) * t, :].astype(F32) * o_ref[i * t:(i + 1) * t, :].astype(F32),
                         axis=1, keepdims=True) for i in range(n)]
        for j in range(n):
            cols = slice(j * t, (j + 1) * t)
            kc = kcat[cols, :]
            vt = v_ref[cols, :]
            dk = jnp.zeros((t, HEAD_W), F32)
            dv = jnp.zeros((t, V_DIM), F32)
            for i in range(j, n):
                rows = slice(i * t, (i + 1) * t)
                qt = q_ref[rows, :]
                dot_ = do_ref[rows, :]
                s = lax.dot_general(qt, kc, _NT, preferred_element_type=F32) * ATTN_SCALE_LOG2
                if i == j:
                    s = _causal_mask(s, t)
                p = jnp.exp2(s - lse_ref[rows, :][:, :1])
                dv = dv + lax.dot_general(p.astype(BF16), dot_, _TN, preferred_element_type=F32)
                dp = lax.dot_general(dot_, vt, _NT, preferred_element_type=F32)
                ds = (p * (dp - delta[i]) * ATTN_SCALE).astype(BF16)
                dk = dk + lax.dot_general(ds, qt, _TN, preferred_element_type=F32)
                dq_part = jnp.dot(ds, kc, preferred_element_type=F32)
                if j == 0:
                    dq_acc[rows, :] = dq_part
                else:
                    dq_acc[rows, :] += dq_part
            dkv_ref[cols, :QK_NOPE] = dk[:, :QK_NOPE].astype(BF16)
            dkv_ref[cols, QK_NOPE:] = dv.astype(BF16)

            @pl.when(pl.program_id(1) == 0)
            def _():
                dkr_ref[cols, :] = dk[:, QK_NOPE:]

            @pl.when(pl.program_id(1) > 0)
            def _():
                dkr_ref[cols, :] += dk[:, QK_NOPE:]
        dq_ref[...] = dq_acc[...].astype(BF16)

    seq_q = pl.BlockSpec((S, HEAD_W), lambda b, h: (b, h))
    seq_h = pl.BlockSpec((S, LANE), lambda b, h: (b, h))
    seq_shared = pl.BlockSpec((S, LANE), lambda b, h: (b, 0))
    return pl.pallas_call(
        body, name=name,
        out_shape=(jax.ShapeDtypeStruct((T, HQ), BF16), jax.ShapeDtypeStruct((T, HQ), BF16),
                   jax.ShapeDtypeStruct((T, LANE), F32)),
        grid=(B, H),
        in_specs=[seq_q,
                  pl.BlockSpec((S, QK_NOPE), lambda b, h: (b, 2 * h)),
                  pl.BlockSpec((S, V_DIM), lambda b, h: (b, 2 * h + 1)),
                  seq_shared, seq_h, seq_h, seq_h],
        out_specs=(seq_q, seq_q, seq_shared),
        scratch_shapes=[pltpu.VMEM((S, HEAD_W), BF16), pltpu.VMEM((S, HEAD_W), F32)],
        compiler_params=_params("parallel", "arbitrary"),
    )(q, kv, kv, kr, o, do, lse)


def _merge_out(h, ms, o, x, wgate, bgate, wpp, wap, wout, next_gain, *, name):
    T, D = x.shape
    tm = _tile(T, 256, 16)

    def body(h_ref, ms_ref, o_ref, x_ref, wgate_ref, bgate_ref, wpp_ref, wap_ref, wout_ref, ng_ref,
             gates_ref, ba_ref, bb_ref, merged_ref, xn_ref, hn_ref):
        logits = lax.dot_general(h_ref[...], wgate_ref[...], _NT, preferred_element_type=F32) + bgate_ref[...]
        gates = jax.nn.sigmoid(logits)
        ba = jnp.dot(ms_ref[...], wpp_ref[...], preferred_element_type=F32)
        bb = jnp.dot(o_ref[...], wap_ref[...], preferred_element_type=F32)
        merged = (gates[:, :D] * ba + gates[:, D:] * bb).astype(BF16)
        gates_ref[...] = gates.astype(BF16)
        ba_ref[...] = ba.astype(BF16)
        bb_ref[...] = bb.astype(BF16)
        merged_ref[...] = merged
        xn = x_ref[...] + jnp.dot(merged, wout_ref[...], preferred_element_type=F32)
        xn_ref[...] = xn
        hn_ref[...] = (xn * _rstd(xn) * ng_ref[...]).astype(BF16)

    def row(w):
        return pl.BlockSpec((tm, w), lambda i: (i, 0))

    def whole(arr):
        return pl.BlockSpec(arr.shape, lambda i: (0,) * arr.ndim)

    bg2, ng2 = bgate.reshape(1, 2 * D), next_gain.reshape(1, D)
    act = jax.ShapeDtypeStruct((T, D), BF16)
    return pl.pallas_call(
        body, name=name,
        out_shape=(jax.ShapeDtypeStruct((T, 2 * D), BF16), act, act, act, jax.ShapeDtypeStruct((T, D), F32), act),
        grid=(T // tm,),
        in_specs=[row(D), row(ms.shape[1]), row(o.shape[1]), row(D), whole(wgate), whole(bg2), whole(wpp),
                  whole(wap), whole(wout), whole(ng2)],
        out_specs=(row(2 * D), row(D), row(D), row(D), row(D), row(D)),
        compiler_params=_params("parallel"),
    )(h, ms, o, x, wgate, bg2, wpp, wap, wout, ng2)


def _merge_bwd(dxo, wout, gates, ba, bb, *, name, dep=None):
    T, D = dxo.shape
    tm = _tile(T, 512, 16)

    def body(dxo_ref, wout_ref, gates_ref, ba_ref, bb_ref, *rest):
        dba_ref, dbb_ref, dgl_ref, dbg_ref = rest[-4:]
        dm = lax.dot_general(dxo_ref[...].astype(BF16), wout_ref[...], _NT, preferred_element_type=F32)
        ga = gates_ref[:, :D].astype(F32)
        gb = gates_ref[:, D:].astype(F32)
        dba_ref[...] = (dm * ga).astype(BF16)
        dbb_ref[...] = (dm * gb).astype(BF16)
        dgl_a = dm * ba_ref[...].astype(F32) * (ga * (1.0 - ga))
        dgl_b = dm * bb_ref[...].astype(F32) * (gb * (1.0 - gb))
        dgl_ref[:, :D] = dgl_a.astype(BF16)
        dgl_ref[:, D:] = dgl_b.astype(BF16)
        sa = jnp.sum(dgl_a, axis=0, keepdims=True)
        sb = jnp.sum(dgl_b, axis=0, keepdims=True)

        @pl.when(pl.program_id(0) == 0)
        def _():
            dbg_ref[:, :D] = sa
            dbg_ref[:, D:] = sb

        @pl.when(pl.program_id(0) > 0)
        def _():
            dbg_ref[:, :D] += sa
            dbg_ref[:, D:] += sb

    def row(w):
        return pl.BlockSpec((tm, w), lambda i: (i, 0))

    act = jax.ShapeDtypeStruct((T, D), BF16)
    return pl.pallas_call(
        body, name=name,
        out_shape=(act, act, jax.ShapeDtypeStruct((T, 2 * D), BF16), jax.ShapeDtypeStruct((1, 2 * D), F32)),
        grid=(T // tm,),
        in_specs=[row(D), pl.BlockSpec(wout.shape, lambda i: (0, 0)), row(2 * D), row(D), row(D)] + _dep_spec(dep),
        out_specs=(row(D), row(D), row(2 * D), pl.BlockSpec((1, 2 * D), lambda i: (0, 0))),
        compiler_params=_params("arbitrary"),
    )(dxo, wout, gates, ba, bb, *([] if dep is None else [dep]))


def _ffn_fwd(x, h, w, tag, next_gain, dep=None):
    gate, up, a = _ffn_up(h, w["up_t"], name=f"{tag}_up", dep=dep)
    if next_gain is None:
        xn, hn = _mm(a, w["wd"], res=x, alpha=0.5, name=f"{tag}_down_last", tk=2816), None
    else:
        xn, hn = _mm(a, w["wd"], res=x, alpha=0.5, norm_gain=next_gain, name=f"{tag}_down", tk=2816)
    return xn, hn, (x, h, gate, up, a)


def _ffn_bwd(dxo, gain, w, saved, tag, dep=None):
    x, h, gate, up, a = saved
    F = gate.shape[1]
    dgate, dup = _ffn_bwd_act(dxo, w["wd"], gate, up, alpha=0.5, name=f"{tag}_bwd_act", dep=dep)
    dwd = _mm(a, dxo, ta=True, alpha=0.5, out_dtype=BF16, name=f"{tag}_dwd", tm=1408, tn=1024, tk=1024)
    dup_t = _ffn_dw_up(dgate, dup, h, name=f"{tag}_dw_up")
    dx, dgain = _dh_norm_bwd(dgate, w["up_t"], dup, w["up_t"], F, x, gain, dxo, name=f"{tag}_dh_norm_bwd")
    return dx, dgain, dup_t, dwd


def _mixer_fwd(x, h, p, w, tabs, S, next_gain, dep=None):
    xp, ql, kvl, qn, kvn, q, kv, kr = _mixer_in(h, w["win_t"], w["wuq_t"], w["wukv"], p["q_latent_norm"],
                                                 p["kv_latent_norm"], tabs, name="mix_in", dep=dep)
    ms = _pool_fwd(xp, p["pool_maps"], p["pool_scale"], S=S, name="pool_fwd")
    o, lse = _attn_fwd(q, kv, kr, S=S, name="attn_fwd")
    gates, ba, bb, merged, xn, hn = _merge_out(h, ms, o, x, w["wgate_t"], p["b_gate"], w["wpp"], w["wap"], w["wout"],
                                               next_gain, name="merge_out")
    return xn, hn, (x, h, xp, ql, kvl, qn, kvn, q, kv, kr, ms, o, lse, gates, ba, bb, merged)


def _mixer_bwd(dxo, p, w, tabs, saved, S, dep=None):
    x, h, xp, ql, kvl, qn, kvn, q, kv, kr, ms, o, lse, gates, ba, bb, merged = saved
    dba, dbb, dgl, dbg = _merge_bwd(dxo, w["wout"], gates, ba, bb, name="merge_bwd", dep=dep)
    g = {}
    g["wout"] = _mm(merged, dxo, ta=True, out_dtype=BF16, name="d_wout", tm=1024, tk=1024)
    g["wpp"] = _mm(ms, dba, ta=True, out_dtype=BF16, name="d_wpp", tk=2048)
    g["wap"] = _mm(o, dbb, ta=True, out_dtype=BF16, name="d_wap", tm=1024, tk=2048)
    dms = _mm(dba, w["wpp"], tb=True, name="d_ms")
    do = _mm(dbb, w["wap"], tb=True, out_dtype=BF16, name="d_o")
    dxp, g["pool_maps"], g["pool_scale"] = _pool_bwd(xp, dms, p["pool_maps"], p["pool_scale"], S=S, name="pool_bwd")
    dq, dkv, dkr = _attn_bwd(q, kv, kr, o, do, lse, S=S, name="attn_bwd")
    dproj, dqp, g["q_latent_norm"], g["kv_latent_norm"] = _mixer_in_bwd(
        dq, dkv, dkr, ql, kvl, dxp, w["wuq_t"], w["wukv"], p["q_latent_norm"], p["kv_latent_norm"], tabs,
        name="mix_in_bwd")
    g["wuq_t"] = _mm(dqp, qn, ta=True, out_dtype=BF16, name="d_wuq", tm=2048, tk=2048)
    g["wukv"] = _mm(kvn, dkv, ta=True, out_dtype=BF16, name="d_wukv", tn=2048, tk=2048)
    g["wa_t"] = _mm(dproj, h, ta=True, out_dtype=BF16, name="d_wa", tm=1280, tn=1024, tk=2048)
    g["wgate_t"] = _mm(dgl, h, ta=True, out_dtype=BF16, name="d_wgate", tm=2048, tn=1024, tk=1024)
    dx, g["norm_mix"] = _dh_norm_bwd(dproj, w["win_t"], dgl, w["wgate_t"], 0, x, p["norm_mix"], dxo,
                                     name="mix_dh_norm_bwd")
    g["b_gate"] = dbg
    return dx, g


BIG = ("ffn1_up", "ffn1_down", "w_in", "w_pool_proj", "w_uq", "w_ukv", "w_attn_proj", "w_out", "ffn2_up", "ffn2_down")
SMALL = ("norm_ffn1", "norm_mix", "b_gate", "pool_maps", "pool_scale", "q_latent_norm", "kv_latent_norm", "norm_ffn2")
PACKED = ("w_pool_proj", "w_uq", "w_ukv")
TRANSPOSED = ("ffn1_up", "ffn2_up", "w_in", "w_uq")
COL_SHARDED = ("w_pool_proj", "w_ukv")
QK_HEAD = QK_NOPE + QK_ROPE


def _rows(stacked):
    n, r, c = stacked.shape
    return stacked.reshape(n * r, c)


def _cols(stacked):
    n, k, c = stacked.shape
    return stacked.transpose(1, 0, 2).reshape(k, n * c)


FFN1_PART = ("ffn1_up", "ffn1_down")
MIXER_PART = ("w_in", "w_attn_proj", "w_out") + PACKED
FFN2_PART = ("ffn2_up", "ffn2_down")


def _kernel_weights(stacked):
    full = {}
    for tag in ("ffn1", "ffn2"):
        if tag + "_up" in stacked:
            full[tag] = {"up_t": _rows(stacked[tag + "_up"]), "wd": _rows(stacked[tag + "_down"])}
    if "w_in" in stacked:
        win_t = _rows(stacked["w_in"])
        D = win_t.shape[1]
        wuq_t = _rows(stacked["w_uq"])
        QL = wuq_t.shape[1]
        H = wuq_t.shape[0] // QK_HEAD
        wuq_t = jnp.pad(wuq_t.reshape(H, QK_HEAD, QL), ((0, 0), (0, HEAD_W - QK_HEAD), (0, 0)))
        full.update({"win_t": win_t, "wgate_t": win_t[win_t.shape[0] - 2 * D:], "wuq_t": wuq_t.reshape(H * HEAD_W, QL),
                     "wukv": _cols(stacked["w_ukv"]), "wpp": _cols(stacked["w_pool_proj"]),
                     "wap": _rows(stacked["w_attn_proj"]), "wout": _rows(stacked["w_out"])})
    return full


def _split_rows(full):
    return full.reshape(N_DEV, full.shape[0] // N_DEV, full.shape[1])


def _split_cols(full):
    k, cols = full.shape
    return full.reshape(k, N_DEV, cols // N_DEV).transpose(1, 0, 2)


def _mixer_grads_stacked(g):
    n_a = g["wa_t"].shape[0] - (LANE - QK_ROPE)
    HQ, QL = g["wuq_t"].shape
    H = HQ // HEAD_W
    wuq_t = g["wuq_t"].reshape(H, HEAD_W, QL)[:, :QK_HEAD].reshape(H * QK_HEAD, QL)
    return {"w_in": _split_rows(jnp.concatenate([g["wa_t"][:n_a], g["wgate_t"]], axis=0)),
            "w_uq": _split_rows(wuq_t),
            "w_pool_proj": _split_cols(g["wpp"]), "w_ukv": _split_cols(g["wukv"]),
            "w_attn_proj": _split_rows(g["wap"]), "w_out": _split_rows(g["wout"])}


def _mesh_place():
    x, y, c = lax.axis_index("x"), lax.axis_index("y"), lax.axis_index("c")
    chips = [(1 - x, y), (x, 1 - y), (1 - x, 1 - y)]
    return x, y, c, chips


HBM = pl.BlockSpec(memory_space=pltpu.HBM)
SEMAPHORES = pl.BlockSpec(memory_space=pltpu.SEMAPHORE)
DATAFLOW = pltpu.SideEffectType.DATAFLOW_SIDE_EFFECTING
GATHER_PEERS = 4
SCATTER_PEERS = 7


def _in_hbm(a):
    return pltpu.with_memory_space_constraint(a, pltpu.HBM)


def _gather_plan(src_refs, land_refs):
    x, y, c, chips = _mesh_place()
    me = 4 * x + 2 * y + c
    targets = [(x, y, 1 - c)] + [(cx, cy, c) for cx, cy in chips]
    return [(s, land.at[me], to) for s, land in zip(src_refs, land_refs) for to in targets]


def _scatter_plan(src_refs, land_refs):
    x, y, c, _ = _mesh_place()
    peers = [(x, y, 1 - c), (1 - x, y, c), (x, 1 - y, c), (1 - x, 1 - y, c),
             (1 - x, y, 1 - c), (x, 1 - y, 1 - c), (1 - x, 1 - y, 1 - c)]
    return [(s.at[4 * px + 2 * py + pc], land.at[k], (px, py, pc))
            for s, land in zip(src_refs, land_refs) for k, (px, py, pc) in enumerate(peers)]


def _descriptors(plan, src_refs, land_refs, send_sems, recv_sems):
    return [pltpu.make_async_remote_copy(src_ref=s, dst_ref=d, send_sem=send_sems.at[k], recv_sem=recv_sems.at[k],
                                         device_id=to, device_id_type=MESH)
            for k, (s, d, to) in enumerate(plan(src_refs, land_refs))]


def _exchange(srcs, land_shapes, plan, per_src, *, name):
    n = len(srcs)

    def body(*refs):
        copies = _descriptors(plan, refs[:n], refs[n:2 * n], refs[2 * n], refs[2 * n + 1])
        for cp in copies:
            cp.start()
        for cp in copies:
            cp.wait()

    return pl.pallas_call(
        body, name=name,
        out_shape=tuple(jax.ShapeDtypeStruct(shape, s.dtype) for shape, s in zip(land_shapes, srcs)),
        in_specs=[ANY] * n, out_specs=(ANY,) * n,
        scratch_shapes=[pltpu.SemaphoreType.DMA((per_src * n,)), pltpu.SemaphoreType.DMA((per_src * n,))],
    )(*srcs)


FORWARD_COPIES = 4


def _forward_slots():
    x, y, c, chips = _mesh_place()
    return [4 * cx + 2 * cy + c for cx, cy in chips] + [4 * x + 2 * y + (1 - c)], (x, y, 1 - c)


def _forward_plan(src_refs, land_refs):
    slots, sibling = _forward_slots()
    return [(land.at[s], land.at[s], sibling) for land in land_refs for s in slots]


def _gather_all_plan(src_refs, land_refs):
    x, y, c, _ = _mesh_place()
    me = 4 * x + 2 * y + c
    peers = [(x, y, 1 - c), (1 - x, y, c), (x, 1 - y, c), (1 - x, 1 - y, c),
             (1 - x, y, 1 - c), (x, 1 - y, 1 - c), (1 - x, 1 - y, 1 - c)]
    return [(s, land.at[me], to) for s, land in zip(src_refs, land_refs) for to in peers]


def _exchange_start(srcs, lands, plan, n_copies, *, name):
    ns, n = len(srcs), len(srcs) + len(lands)

    def body(*refs):
        for cp in _descriptors(plan, refs[:ns], refs[ns:n], refs[n], refs[n + 1]):
            cp.start()
        refs[-1][...] = jnp.zeros_like(refs[-1])

    sems = pltpu.SemaphoreType.DMA((n_copies,))
    out = pl.pallas_call(
        body, name=name,
        out_shape=(sems, sems, *[pltpu.HBM(a.shape, a.dtype) for a in srcs + lands],
                   jax.ShapeDtypeStruct((8, LANE), F32)),
        in_specs=(HBM,) * n,
        out_specs=(SEMAPHORES, SEMAPHORES, *[HBM] * n, pl.BlockSpec(memory_space=pltpu.VMEM)),
        input_output_aliases={i: 2 + i for i in range(n)},
        compiler_params=pltpu.CompilerParams(has_side_effects=DATAFLOW),
    )(*[_in_hbm(a) for a in srcs + lands])
    return out[0], out[1], list(out[2:2 + ns]), list(out[2 + ns:2 + n]), out[-1]


def _exchange_wait(send_sems, recv_sems, srcs, lands, plan, after, *, name):
    ns, n = len(srcs), len(srcs) + len(lands)

    def body(*refs):
        for cp in _descriptors(plan, refs[:ns], refs[ns:n], refs[n], refs[n + 1]):
            cp.wait_send()
            cp.wait_recv()

    out = pl.pallas_call(
        body, name=name,
        out_shape=tuple(pltpu.HBM(a.shape, a.dtype) for a in srcs + lands),
        in_specs=(*[HBM] * n, SEMAPHORES, SEMAPHORES, ANY),
        out_specs=(HBM,) * n,
        input_output_aliases={i: i for i in range(n)},
        compiler_params=pltpu.CompilerParams(has_side_effects=DATAFLOW),
    )(*srcs, *lands, send_sems, recv_sems, after)
    return list(out[:ns]), list(out[ns:])


def _gather_forward(lands, *, name):
    n = len(lands)

    def body(*refs):
        in_refs, out_refs = refs[:n], refs[n:2 * n]
        token, send_sems, recv_sems = refs[2 * n:2 * n + 3]
        slots, sibling = _forward_slots()
        passed = [pltpu.make_async_remote_copy(
            src_ref=i.at[s], dst_ref=o.at[s],
            send_sem=send_sems.at[FORWARD_COPIES * b + j], recv_sem=recv_sems.at[FORWARD_COPIES * b + j],
            device_id=sibling, device_id_type=MESH)
            for b, (i, o) in enumerate(zip(in_refs, out_refs)) for j, s in enumerate(slots)]
        for cp in passed:
            cp.start()
        for cp in passed:
            cp.wait()
        token[...] = jnp.zeros_like(token)

    out = pl.pallas_call(
        body, name=name,
        out_shape=(*[jax.ShapeDtypeStruct(a.shape, a.dtype) for a in lands], jax.ShapeDtypeStruct((8, LANE), F32)),
        in_specs=[ANY] * n,
        out_specs=(*[ANY] * n, pl.BlockSpec(memory_space=pltpu.VMEM)),
        input_output_aliases={i: i for i in range(n)},
        scratch_shapes=[pltpu.SemaphoreType.DMA((FORWARD_COPIES * n,)), pltpu.SemaphoreType.DMA((FORWARD_COPIES * n,))],
    )(*lands)
    return list(out[:n]), out[n]


def _scatter_sum(parts, got, me, *, name):
    shard = parts.shape[1:]
    cols = shard[-1]
    rows = int(np.prod(shard[:-1]))
    tr = _tile(rows, 256, 16)

    def body(me_ref, p_ref, g_ref, o_ref):
        acc = p_ref[...].astype(F32)
        for k in range(SCATTER_PEERS):
            acc = acc + g_ref[k].astype(F32)
        o_ref[...] = acc

    out = pl.pallas_call(
        body, name=name,
        out_shape=jax.ShapeDtypeStruct((rows, cols), F32),
        grid_spec=pltpu.PrefetchScalarGridSpec(
            num_scalar_prefetch=1, grid=(rows // tr,),
            in_specs=[pl.BlockSpec((None, tr, cols), lambda r, me_ref: (me_ref[0], r, 0)),
                      pl.BlockSpec((SCATTER_PEERS, tr, cols), lambda r, me_ref: (0, r, 0))],
            out_specs=pl.BlockSpec((tr, cols), lambda r, me_ref: (r, 0))),
        compiler_params=_params("parallel"),
    )(me, parts.reshape(N_DEV, rows, cols), got.reshape(SCATTER_PEERS, rows, cols))
    return out.reshape(shard)


def _sum_devices(parts, *, name):
    _, R, C = parts.shape
    tr = _tile(R, 512, 8)

    def body(p_ref, o_ref):
        acc = p_ref[0]
        for d in range(1, N_DEV):
            acc = acc + p_ref[d]
        o_ref[...] = acc

    return pl.pallas_call(
        body, name=name,
        out_shape=jax.ShapeDtypeStruct((R, C), F32),
        grid=(R // tr,),
        in_specs=[pl.BlockSpec((N_DEV, tr, C), lambda r: (0, r, 0))],
        out_specs=pl.BlockSpec((tr, C), lambda r: (r, 0)),
        compiler_params=_params("parallel"),
    )(parts)


def _adamw(w, g, m, v, *, name, dep=None):
    shape = w.shape
    cols = shape[-1]
    rows = w.size // cols
    tr = _tile(rows, 256, 8)

    def body(w_ref, g_ref, m_ref, v_ref, *rest):
        d_ref, nm_ref, nv_ref = rest[-3:]
        g = g_ref[...]
        m = ADAM_B1 * m_ref[...] + (1.0 - ADAM_B1) * g
        v = ADAM_B2 * v_ref[...] + (1.0 - ADAM_B2) * jnp.square(g)
        m_hat = m / (1.0 - ADAM_B1 ** ADAM_STEP)
        v_hat = v / (1.0 - ADAM_B2 ** ADAM_STEP)
        d_ref[...] = -ADAM_LR * (m_hat / (jnp.sqrt(v_hat) + ADAM_EPS) + ADAM_WD * w_ref[...])
        nm_ref[...] = m
        nv_ref[...] = v

    spec = pl.BlockSpec((tr, cols), lambda i: (i, 0))
    out = jax.ShapeDtypeStruct((rows, cols), F32)
    d, nm, nv = pl.pallas_call(
        body, name=name,
        out_shape=(out, out, out),
        grid=(rows // tr,),
        in_specs=[spec] * 4 + _dep_spec(dep), out_specs=(spec,) * 3,
        compiler_params=_params("parallel"),
    )(*(a.reshape(rows, cols) for a in (w, g, m, v)), *([] if dep is None else [dep]))
    return d.reshape(shape), nm.reshape(shape), nv.reshape(shape)


PACK_ALIGN = 16 * LANE


def _pack(pieces, lead):
    out = []
    for p in pieces:
        keep = p.shape[:lead]
        flat = p.reshape(*keep, -1)
        pad = (-flat.shape[-1]) % PACK_ALIGN
        if pad:
            flat = jnp.pad(flat, [(0, 0)] * lead + [(0, pad)])
        out.append(flat.reshape(*keep, -1, LANE))
    return jnp.concatenate(out, axis=lead)


def _unpack(buf, shapes, lead):
    keep = buf.shape[:lead]
    out, row = [], 0
    for shape in shapes:
        size = int(np.prod(shape))
        rows = -(-size // PACK_ALIGN) * (PACK_ALIGN // LANE)
        piece = lax.slice_in_dim(buf, row, row + rows, axis=lead).reshape(*keep, rows * LANE)
        out.append(lax.slice_in_dim(piece, 0, size, axis=lead).reshape(*keep, *shape))
        row += rows
    return out


def kernel(x, positions, norm_ffn1, ffn1_up, ffn1_down, norm_mix, w_in, b_gate, pool_maps, pool_scale, w_pool_proj, q_latent_norm, w_uq, kv_latent_norm, w_ukv, w_attn_proj, w_out, norm_ffn2, ffn2_up, ffn2_down, final_norm, loss_target, m_norm_ffn1, m_ffn1_up, m_ffn1_down, m_norm_mix, m_w_in, m_b_gate, m_pool_maps, m_pool_scale, m_w_pool_proj, m_q_latent_norm, m_w_uq, m_kv_latent_norm, m_w_ukv, m_w_attn_proj, m_w_out, m_norm_ffn2, m_ffn2_up, m_ffn2_down, m_final_norm, v_norm_ffn1, v_ffn1_up, v_ffn1_down, v_norm_mix, v_w_in, v_b_gate, v_pool_maps, v_pool_scale, v_w_pool_proj, v_q_latent_norm, v_w_uq, v_kv_latent_norm, v_w_ukv, v_w_attn_proj, v_w_out, v_norm_ffn2, v_ffn2_up, v_ffn2_down, v_final_norm):
    order = ("norm_ffn1", "ffn1_up", "ffn1_down", "norm_mix", "w_in", "b_gate", "pool_maps", "pool_scale",
             "w_pool_proj", "q_latent_norm", "w_uq", "kv_latent_norm", "w_ukv", "w_attn_proj", "w_out",
             "norm_ffn2", "ffn2_up", "ffn2_down", "final_norm")
    w = dict(zip(order, (norm_ffn1, ffn1_up, ffn1_down, norm_mix, w_in, b_gate, pool_maps, pool_scale, w_pool_proj,
                         q_latent_norm, w_uq, kv_latent_norm, w_ukv, w_attn_proj, w_out, norm_ffn2, ffn2_up,
                         ffn2_down, final_norm)))
    m = dict(zip(order, (m_norm_ffn1, m_ffn1_up, m_ffn1_down, m_norm_mix, m_w_in, m_b_gate, m_pool_maps, m_pool_scale,
                         m_w_pool_proj, m_q_latent_norm, m_w_uq, m_kv_latent_norm, m_w_ukv, m_w_attn_proj, m_w_out,
                         m_norm_ffn2, m_ffn2_up, m_ffn2_down, m_final_norm)))
    v = dict(zip(order, (v_norm_ffn1, v_ffn1_up, v_ffn1_down, v_norm_mix, v_w_in, v_b_gate, v_pool_maps, v_pool_scale,
                         v_w_pool_proj, v_q_latent_norm, v_w_uq, v_kv_latent_norm, v_w_ukv, v_w_attn_proj, v_w_out,
                         v_norm_ffn2, v_ffn2_up, v_ffn2_down, v_final_norm)))
    L = norm_ffn1.shape[0]
    B, S, D = x.shape
    T = B * S

    def turned(a, n):
        return a.transpose(0, 2, 1) if n in TRANSPOSED else a

    wk, mk, vk = ({n: turned(d[n], n) for n in order} for d in (w, m, v))
    packed_shapes = [wk[n].shape[1:] for n in PACKED]
    my_slot = 4 * lax.axis_index("x") + 2 * lax.axis_index("y") + lax.axis_index("c")
    me = jnp.stack([my_slot]).astype(jnp.int32)

    def weight_blocks(l, names, token):
        zero = token[0, 0].astype(BF16)
        blocks = [wk[n][l].astype(BF16) + zero for n in names if n not in PACKED]
        if any(n in PACKED for n in names):
            blocks.append(_pack([wk[n][l].astype(BF16) + zero for n in PACKED], 0))
        return blocks

    def kernel_weights(names, lands):
        direct = [n for n in names if n not in PACKED]
        stacked = dict(zip(direct, lands))
        if len(lands) > len(direct):
            stacked.update(zip(PACKED, _unpack(lands[-1], packed_shapes, 1)))
        return _kernel_weights(stacked)

    def gather_start(l, names, token, tag):
        blocks = weight_blocks(l, names, token)
        lands = [lax.empty((N_DEV, *b.shape), b.dtype) for b in blocks]
        send_sems, recv_sems, blocks, lands, token = _exchange_start(
            blocks, lands, _gather_plan, GATHER_PEERS * len(blocks), name=f"gather_start_{tag}")
        return (send_sems, recv_sems, blocks, lands, tag), token

    def gather_wait(state, after):
        send_sems, recv_sems, blocks, lands, tag = state
        return _exchange_wait(send_sems, recv_sems, blocks, lands, _gather_plan, after, name=f"gather_wait_{tag}")[1]

    layer_part = FFN1_PART + MIXER_PART + FFN2_PART
    tabs = _rope_tables(positions.reshape(T))
    xs = x.reshape(T, D)
    h = _rms_fwd(xs, w["norm_ffn1"][0], name="first_norm")
    full, saved = [], []

    p = {n: w[n][0] for n in SMALL}
    blocks = weight_blocks(0, FFN1_PART, jnp.zeros((8, LANE), F32))
    lands = _exchange(blocks, [(N_DEV, *b.shape) for b in blocks], _gather_plan, GATHER_PEERS, name="gather_first")
    lands, token = _gather_forward(lands, name="gather_forward")
    w0 = kernel_weights(FFN1_PART, lands)
    state, token = gather_start(0, MIXER_PART, token, "0_mix")
    xs, h, s1 = _ffn_fwd(xs, h, w0["ffn1"], "ffn1", p["norm_mix"], dep=token)
    lands, token = _gather_forward(gather_wait(state, xs), name="gather_forward")
    w0.update(kernel_weights(MIXER_PART, lands))
    state, token = gather_start(0, FFN2_PART, token, "0_ffn2")
    if L > 1:
        next_state, token = gather_start(1, layer_part, token, "1")
    xs, h, s2 = _mixer_fwd(xs, h, p, w0, tabs, S, p["norm_ffn2"], dep=token)
    lands, token = _gather_forward(gather_wait(state, xs), name="gather_forward")
    w0.update(kernel_weights(FFN2_PART, lands))
    xs, h, s3 = _ffn_fwd(xs, h, w0["ffn2"], "ffn2", w["norm_ffn1"][1] if L > 1 else None, dep=token)
    if L > 1:
        lands, token = _gather_forward(gather_wait(next_state, xs), name="gather_forward")
    full.append(w0)
    saved.append((s1, s2, s3))

    for l in range(1, L):
        full.append(kernel_weights(layer_part, lands))
        more = l + 1 < L
        p = {n: w[n][l] for n in SMALL}
        if more:
            state, token = gather_start(l + 1, layer_part, token, f"{l + 1}")
        xs, h, s1 = _ffn_fwd(xs, h, full[l]["ffn1"], "ffn1", p["norm_mix"], dep=token if more else None)
        xs, h, s2 = _mixer_fwd(xs, h, p, full[l], tabs, S, p["norm_ffn2"])
        if more:
            lands = gather_wait(state, xs)
            send_sems, recv_sems, _, lands, token = _exchange_start(
                [], lands, _forward_plan, FORWARD_COPIES * len(lands), name=f"forward_start_{l + 1}")
        xs, h, s3 = _ffn_fwd(xs, h, full[l]["ffn2"], "ffn2", w["norm_ffn1"][l + 1] if more else None,
                             dep=token if more else None)
        if more:
            _, lands = _exchange_wait(send_sems, recv_sems, [], lands, _forward_plan, xs, name=f"forward_wait_{l + 1}")
        saved.append((s1, s2, s3))
    dx, dfinal, loss = _loss_head(xs, final_norm, loss_target.reshape(T, D), name="loss_head")

    big_grads = {n: [None] * L for n in BIG}
    small_grads_of = [None] * L
    pending = None

    def scatter_start(names, stacked, tag):
        srcs = [stacked[n] for n in names if n not in PACKED]
        if any(n in PACKED for n in names):
            srcs.append(_pack([stacked[n] for n in PACKED], 1))
        lands = [lax.empty((SCATTER_PEERS, *s.shape[1:]), s.dtype) for s in srcs]
        send_sems, recv_sems, srcs, lands, token = _exchange_start(
            srcs, lands, _scatter_plan, SCATTER_PEERS * len(srcs), name=f"scatter_start_{tag}")
        return (names, send_sems, recv_sems, srcs, lands, tag), token

    def scatter_finish(state, after, l):
        names, send_sems, recv_sems, srcs, lands, tag = state
        srcs, got = _exchange_wait(send_sems, recv_sems, srcs, lands, _scatter_plan, after, name=f"scatter_wait_{tag}")
        sums = [_scatter_sum(s, g, me, name="scatter_sum") for s, g in zip(srcs, got)]
        direct = [n for n in names if n not in PACKED]
        for n, g in zip(direct, sums):
            big_grads[n][l] = g
        if len(sums) > len(direct):
            for n, g in zip(PACKED, _unpack(sums[-1], packed_shapes, 0)):
                big_grads[n][l] = g

    dep = None
    for l in reversed(range(L)):
        p = {n: w[n][l] for n in SMALL}
        s1, s2, s3 = saved[l]
        small_g = {}
        dx, small_g["norm_ffn2"], dup_t, dwd = _ffn_bwd(dx, p["norm_ffn2"], full[l]["ffn2"], s3, "ffn2", dep=dep)
        if pending is not None:
            scatter_finish(pending[0], dx, pending[1])
        stacked = {"ffn2_up": _split_rows(dup_t), "ffn2_down": _split_rows(dwd)}
        state, dep = scatter_start(("ffn2_up", "ffn2_down"), stacked, f"ffn2_{l}")
        pending = (state, l)

        dx, gm = _mixer_bwd(dx, p, full[l], tabs, s2, S, dep=dep)
        scatter_finish(pending[0], dx, pending[1])
        names = ("w_in", "w_attn_proj", "w_out") + PACKED
        state, dep = scatter_start(names, _mixer_grads_stacked(gm), f"mix_{l}")
        pending = (state, l)
        small_g.update({n: gm[n] for n in SMALL if n in gm})

        dx, small_g["norm_ffn1"], dup_t, dwd = _ffn_bwd(dx, p["norm_ffn1"], full[l]["ffn1"], s1, "ffn1", dep=dep)
        scatter_finish(pending[0], dx, pending[1])
        stacked = {"ffn1_up": _split_rows(dup_t), "ffn1_down": _split_rows(dwd)}
        state, dep = scatter_start(("ffn1_up", "ffn1_down"), stacked, f"ffn1_{l}")
        pending = (state, l)
        small_grads_of[l] = small_g
    grad_x = dx.reshape(B, S, D)

    small_parts = [small_grads_of[l][n] for l in range(L) for n in SMALL] + [dfinal, loss[0, :1]]
    small_shapes = [p.shape for p in small_parts]
    vec = _pack([jnp.concatenate([p.reshape(-1) for p in small_parts])], 0)
    small_send, small_recv, vec_thru, small_land, small_token = _exchange_start(
        [vec], [lax.empty((N_DEV, *vec.shape), F32)], _gather_all_plan, SCATTER_PEERS, name="small_start")

    gk, grad, delta, new_m, new_v = {}, {}, {}, {}, {}

    def update(n, dep=None):
        wn, gn, mn, vn = (a.reshape(1, -1) if a.ndim == 1 else a for a in (wk[n], gk[n], mk[n], vk[n]))
        d, nm, nv = _adamw(wn, gn, mn, vn, name="adamw_" + n, dep=dep)
        grad[n] = turned(gk[n], n)
        delta[n], new_m[n], new_v[n] = (turned(a.reshape(wk[n].shape), n) for a in (d, nm, nv))

    last_block = ("ffn1_up", "ffn1_down")
    deps = [dep, small_token]
    for n in BIG:
        if n not in last_block:
            gk[n] = jnp.stack(big_grads[n])
            update(n, deps.pop(0) if deps else None)
    scatter_finish(pending[0], new_v["ffn2_down"], pending[1])
    for n in last_block:
        gk[n] = jnp.stack(big_grads[n])
        update(n)

    vec_thru, small_land = _exchange_wait(small_send, small_recv, vec_thru, small_land, _gather_all_plan,
                                          new_v["ffn1_down"], name="small_wait")
    parts = lax.dynamic_update_index_in_dim(small_land[0], vec_thru[0], my_slot, 0)
    flat = _sum_devices(parts, name="sum_small").reshape(-1)
    small_grads, at = [], 0
    for shape in small_shapes:
        size = int(np.prod(shape))
        small_grads.append(lax.slice_in_dim(flat, at, at + size).reshape(shape))
        at += size
    loss_total = small_grads[-1].reshape(())
    for i, n in enumerate(SMALL):
        gk[n] = jnp.stack([small_grads[l * len(SMALL) + i] for l in range(L)]).reshape(w[n].shape)
        update(n)
    gk["final_norm"] = small_grads[-2].reshape(final_norm.shape)
    update("final_norm")
    return (loss_total, grad_x, *[grad[n] for n in order], *[delta[n] for n in order],
            *[new_m[n] for n in order], *[new_v[n] for n in order])
```

```python
import functools

import numpy as np
import jax
import jax.numpy as jnp
from jax import lax
from jax.experimental import pallas as pl
from jax.experimental.pallas import tpu as pltpu

F32 = jnp.float32
BF16 = jnp.bfloat16

NORM_EPS = 1e-6
ROPE_THETA = 10000.0
QK_NOPE = 128
QK_ROPE = 64
V_DIM = 128
HEAD_W = 256
POOL_WINDOWS = (2, 4, 8, 16)
POOL_G = 128
POOL_DIM = 512
LANE = 128
ATTN_SCALE = float((QK_NOPE + QK_ROPE) ** -0.5)
ATTN_SCALE_LOG2 = ATTN_SCALE * float(np.log2(np.e))
MASK_VALUE = -1e30
ATTN_TILE = 512

ADAM_LR = 0.001
ADAM_B1 = 0.9
ADAM_B2 = 0.999
ADAM_EPS = 1e-08
ADAM_WD = 0.01
ADAM_STEP = 10

N_DEV = 8
VMEM_LIMIT = 52 * 1024 * 1024

MESH = pl.DeviceIdType.MESH
ANY = pl.BlockSpec(memory_space=pl.ANY)


def _tile(dim, target, align=LANE):
    if dim <= target:
        return dim
    t = (target // align) * align
    while t >= align:
        if dim % t == 0:
            return t
        t -= align
    return dim


def _params(*sem):
    return pltpu.CompilerParams(dimension_semantics=sem, vmem_limit_bytes=VMEM_LIMIT)


def _rstd(x):
    return lax.rsqrt(jnp.mean(x * x, axis=-1, keepdims=True) + NORM_EPS)


def _mm(a, b, *, name, ta=False, tb=False, out_dtype=F32, res=None, alpha=1.0, tm=512, tn=1024, tk=1024, b_row0=0,
        norm_gain=None, dep=None):
    if ta:
        K, M = a.shape
    else:
        M, K = a.shape
    if tb:
        N, K2 = b.shape
    else:
        K2, N = b.shape
    assert K == K2 or (not tb and K2 >= b_row0 + K), (a.shape, b.shape, ta, tb)
    tm, tn, tk = _tile(M, tm), _tile(N, tn), _tile(K, tk)
    nk = K // tk
    assert b_row0 % tk == 0
    kb0 = b_row0 // tk
    dims = (((0 if ta else 1,), (1 if tb else 0,)), ((), ()))
    has_res = res is not None
    has_norm = norm_gain is not None
    assert not has_norm or tn == N
    n_in = 2 + has_res + has_norm + (dep is not None)

    def body(*refs):
        a_ref, b_ref = refs[0], refs[1]
        res_ref = refs[2] if has_res else None
        gain_ref = refs[2 + has_res] if has_norm else None
        o_ref = refs[n_in]
        h_ref = refs[n_in + 1] if has_norm else None
        acc_ref = refs[n_in + 1 + has_norm] if nk > 1 else None
        part = lax.dot_general(a_ref[...].astype(BF16), b_ref[...].astype(BF16), dims,
                               preferred_element_type=F32)

        def finish(acc):
            r = acc * alpha if alpha != 1.0 else acc
            if has_res:
                r = res_ref[...].astype(F32) + r
            o_ref[...] = r.astype(out_dtype)
            if has_norm:
                h_ref[...] = (r * _rstd(r) * gain_ref[...]).astype(BF16)

        if nk == 1:
            finish(part)
        else:
            k = pl.program_id(2)

            @pl.when(k == 0)
            def _():
                acc_ref[...] = part

            @pl.when(k > 0)
            def _():
                acc_ref[...] += part

            @pl.when(k == nk - 1)
            def _():
                finish(acc_ref[...])

    a_spec = pl.BlockSpec((tk, tm), lambda i, j, k: (k, i)) if ta else pl.BlockSpec((tm, tk), lambda i, j, k: (i, k))
    b_spec = (pl.BlockSpec((tn, tk), lambda i, j, k: (j, k)) if tb
              else pl.BlockSpec((tk, tn), lambda i, j, k: (k + kb0, j)))
    in_specs = [a_spec, b_spec]
    operands = [a, b]
    tile_spec = pl.BlockSpec((tm, tn), lambda i, j, k: (i, j))
    if has_res:
        in_specs.append(tile_spec)
        operands.append(res)
    if has_norm:
        in_specs.append(pl.BlockSpec((1, tn), lambda i, j, k: (0, j)))
        operands.append(norm_gain.reshape(1, N))
    if dep is not None:
        in_specs += _dep_spec(dep)
        operands.append(dep)
    out = pl.pallas_call(
        body, name=name,
        out_shape=(jax.ShapeDtypeStruct((M, N), out_dtype),) + ((jax.ShapeDtypeStruct((M, N), BF16),) if has_norm else ()),
        grid=(M // tm, N // tn, nk),
        in_specs=in_specs,
        out_specs=(tile_spec,) + ((tile_spec,) if has_norm else ()),
        scratch_shapes=[pltpu.VMEM((tm, tn), F32)] if nk > 1 else [],
        compiler_params=_params("parallel", "parallel", "arbitrary"),
    )(*operands)
    return out if has_norm else out[0]


def _rms_fwd(x, g, *, name):
    T, D = x.shape
    tm = _tile(T, 512, 16)

    def body(x_ref, g_ref, h_ref):
        x = x_ref[...]
        h_ref[...] = (x * _rstd(x) * g_ref[...]).astype(BF16)

    return pl.pallas_call(
        body, name=name,
        out_shape=jax.ShapeDtypeStruct((T, D), BF16),
        grid=(T // tm,),
        in_specs=[pl.BlockSpec((tm, D), lambda i: (i, 0)), pl.BlockSpec((1, D), lambda i: (0, 0))],
        out_specs=pl.BlockSpec((tm, D), lambda i: (i, 0)),
        compiler_params=_params("parallel"),
    )(x, g.reshape(1, D))


def _dh_norm_bwd(a1, b1, a2, b2, b2_row0, x, g, dxo, *, name):
    T, D = x.shape
    K1, K2 = a1.shape[1], a2.shape[1]
    assert b2_row0 % K2 == 0 and b1.shape[0] >= K1 and b2.shape[0] >= b2_row0 + K2
    tm = _tile(T, 256, 16)

    def body(a1_ref, b1_ref, a2_ref, b2_ref, x_ref, g_ref, dxo_ref, dx_ref, dg_ref):
        x = x_ref[...]
        r = _rstd(x)
        xhat = x * r
        dh = (jnp.dot(a1_ref[...], b1_ref[...], preferred_element_type=F32)
              + jnp.dot(a2_ref[...], b2_ref[...], preferred_element_type=F32))
        dxh = dh * g_ref[...]
        dx_ref[...] = dxo_ref[...] + r * (dxh - xhat * jnp.mean(dxh * xhat, axis=-1, keepdims=True))
        part = jnp.sum(dh * xhat, axis=0, keepdims=True)

        @pl.when(pl.program_id(0) == 0)
        def _():
            dg_ref[...] = part

        @pl.when(pl.program_id(0) > 0)
        def _():
            dg_ref[...] += part

    row = pl.BlockSpec((tm, D), lambda i: (i, 0))
    vec = pl.BlockSpec((1, D), lambda i: (0, 0))
    return pl.pallas_call(
        body, name=name,
        out_shape=(jax.ShapeDtypeStruct((T, D), F32), jax.ShapeDtypeStruct((1, D), F32)),
        grid=(T // tm,),
        in_specs=[pl.BlockSpec((tm, K1), lambda i: (i, 0)), pl.BlockSpec((K1, D), lambda i: (0, 0)),
                  pl.BlockSpec((tm, K2), lambda i: (i, 0)), pl.BlockSpec((K2, D), lambda i: (b2_row0 // K2, 0)),
                  row, vec, row],
        out_specs=(row, vec),
        compiler_params=_params("arbitrary"),
    )(a1, b1, a2, b2, x, g.reshape(1, D), dxo)


def _loss_head(x, g, target, *, name):
    T, D = x.shape
    tm = _tile(T, 512, 16)

    def body(x_ref, g_ref, t_ref, dx_ref, dg_ref, loss_ref):
        x = x_ref[...]
        gain = g_ref[...]
        r = _rstd(x)
        xhat = x * r
        err = xhat * gain - t_ref[...]
        dy = err * (1.0 / D)
        dxh = dy * gain
        dx_ref[...] = r * (dxh - xhat * jnp.mean(dxh * xhat, axis=-1, keepdims=True))
        dg_part = jnp.sum(dy * xhat, axis=0, keepdims=True)
        loss_part = jnp.full((1, LANE), 0.5 / D, F32) * jnp.sum(err * err)

        @pl.when(pl.program_id(0) == 0)
        def _():
            dg_ref[...] = dg_part
            loss_ref[...] = loss_part

        @pl.when(pl.program_id(0) > 0)
        def _():
            dg_ref[...] += dg_part
            loss_ref[...] += loss_part

    row = pl.BlockSpec((tm, D), lambda i: (i, 0))
    vec = pl.BlockSpec((1, D), lambda i: (0, 0))
    return pl.pallas_call(
        body, name=name,
        out_shape=(jax.ShapeDtypeStruct((T, D), F32), jax.ShapeDtypeStruct((1, D), F32),
                   jax.ShapeDtypeStruct((1, LANE), F32)),
        grid=(T // tm,),
        in_specs=[row, vec, row],
        out_specs=(row, vec, pl.BlockSpec((1, LANE), lambda i: (0, 0))),
        compiler_params=_params("arbitrary"),
    )(x, g.reshape(1, D), target)


def _ffn_up(h, w_up_t, *, name, dep=None):
    T, D = h.shape
    F = w_up_t.shape[0] // 2
    tm, tn = _tile(T, 512, 16), _tile(F, 1408)
    nj = F // tn

    def body(h_ref, wg_ref, wu_ref, *rest):
        gate_ref, up_ref, a_ref = rest[-3:]
        h = h_ref[...]
        gate = lax.dot_general(h, wg_ref[...], _NT, preferred_element_type=F32)
        up = lax.dot_general(h, wu_ref[...], _NT, preferred_element_type=F32)
        gate_ref[...] = gate.astype(BF16)
        up_ref[...] = up.astype(BF16)
        a_ref[...] = (gate * jax.nn.sigmoid(gate) * up).astype(BF16)

    o_spec = pl.BlockSpec((tm, tn), lambda j, i: (i, j))
    out = jax.ShapeDtypeStruct((T, F), BF16)
    return pl.pallas_call(
        body, name=name,
        out_shape=(out, out, out),
        grid=(nj, T // tm),
        in_specs=[pl.BlockSpec((tm, D), lambda j, i: (i, 0)),
                  pl.BlockSpec((tn, D), lambda j, i: (j, 0)),
                  pl.BlockSpec((tn, D), lambda j, i: (j + nj, 0))] + _dep_spec(dep),
        out_specs=(o_spec, o_spec, o_spec),
        compiler_params=_params("parallel", "parallel"),
    )(h, w_up_t, w_up_t, *([] if dep is None else [dep]))


def _ffn_dw_up(dgate, dup, h, *, name):
    T, F = dgate.shape
    D = h.shape[1]
    tm, tk = _tile(F, 1408), _tile(T, 1024, 16)
    nf, nk = F // tm, T // tk

    def body(dgate_ref, dup_ref, h_ref, o_ref, acc_ref):
        i, k = pl.program_id(0), pl.program_id(1)

        def accumulate(part):
            @pl.when(k == 0)
            def _():
                acc_ref[...] = part

            @pl.when(k > 0)
            def _():
                acc_ref[...] += part

        @pl.when(i < nf)
        def _():
            accumulate(lax.dot_general(dgate_ref[...], h_ref[...], _TN, preferred_element_type=F32))

        @pl.when(i >= nf)
        def _():
            accumulate(lax.dot_general(dup_ref[...], h_ref[...], _TN, preferred_element_type=F32))

        @pl.when(k == nk - 1)
        def _():
            o_ref[...] = acc_ref[...].astype(BF16)

    return pl.pallas_call(
        body, name=name,
        out_shape=jax.ShapeDtypeStruct((2 * F, D), BF16),
        grid=(2 * nf, nk),
        in_specs=[pl.BlockSpec((tk, tm), lambda i, k: (jnp.where(i < nf, k, nk - 1), jnp.minimum(i, nf - 1))),
                  pl.BlockSpec((tk, tm), lambda i, k: (jnp.where(i < nf, 0, k), jnp.maximum(i - nf, 0))),
                  pl.BlockSpec((tk, D), lambda i, k: (k, 0))],
        out_specs=pl.BlockSpec((tm, D), lambda i, k: (i, 0)),
        scratch_shapes=[pltpu.VMEM((tm, D), F32)],
        compiler_params=_params("parallel", "arbitrary"),
    )(dgate, dup, h)


def _dep_spec(dep):
    return [] if dep is None else [pl.BlockSpec(dep.shape, lambda *_: (0,) * dep.ndim)]


def _ffn_bwd_act(dxo, wd, gate, up, *, alpha, name, dep=None):
    T, D = dxo.shape
    F = wd.shape[0]
    tm, tn = _tile(T, 256, 16), _tile(F, 2816)

    def body(dxo_ref, wd_ref, gate_ref, up_ref, *rest):
        dgate_ref, dup_ref = rest[-2:]
        da = lax.dot_general(dxo_ref[...].astype(BF16), wd_ref[...], (((1,), (1,)), ((), ())),
                             preferred_element_type=F32) * alpha
        gate = gate_ref[...].astype(F32)
        up = up_ref[...].astype(F32)
        sig = jax.nn.sigmoid(gate)
        dgate_ref[...] = (da * up * (sig * (1.0 + gate * (1.0 - sig)))).astype(BF16)
        dup_ref[...] = (da * (gate * sig)).astype(BF16)

    t_spec = pl.BlockSpec((tm, tn), lambda j, i: (i, j))
    out = jax.ShapeDtypeStruct((T, F), BF16)
    return pl.pallas_call(
        body, name=name,
        out_shape=(out, out),
        grid=(F // tn, T // tm),
        in_specs=[pl.BlockSpec((tm, D), lambda j, i: (i, 0)), pl.BlockSpec((tn, D), lambda j, i: (j, 0)),
                  t_spec, t_spec] + _dep_spec(dep),
        out_specs=(t_spec, t_spec),
        compiler_params=_params("parallel", "parallel"),
    )(dxo, wd, gate, up, *([] if dep is None else [dep]))


def _rope_tables(positions):
    half = QK_ROPE // 2
    inv_freq = ROPE_THETA ** (-jnp.arange(0, QK_ROPE, 2, dtype=F32) / QK_ROPE)
    ang = positions.astype(F32)[:, None] * inv_freq
    cos, sin = jnp.cos(ang), jnp.sin(ang)
    z = jnp.zeros_like(cos)
    zz = jnp.zeros((positions.shape[0], LANE - QK_ROPE), F32)
    c = jnp.concatenate([cos, cos, zz], axis=1)
    sa = jnp.concatenate([z, sin, zz], axis=1)
    sb = jnp.concatenate([-sin, z, zz], axis=1)
    return c, sa, sb


def _rotate(seg, c, sa, sb, sign):
    half = QK_ROPE // 2
    mix = pltpu.roll(seg, half, 1) * sa + pltpu.roll(seg, LANE - half, 1) * sb
    return seg * c + mix if sign > 0 else seg * c - mix


def _mixer_in(h, wa, wuq, wukv, gq, gkv, tabs, *, name, dep=None):
    T, D = h.shape
    HQ, QL = wuq.shape
    KVL = wukv.shape[0]
    H = HQ // HEAD_W
    o_q, o_kv, o_kr = POOL_DIM, POOL_DIM + QL, POOL_DIM + QL + KVL
    PA = o_kr + LANE
    assert wa.shape[0] >= PA
    tm = _tile(T, 512, 16)

    def body(h_ref, wa_ref, wuq_ref, wukv_ref, gq_ref, gkv_ref, c_ref, sa_ref, sb_ref, *rest):
        xp_ref, ql_ref, kvl_ref, qn_ref, kvn_ref, q_ref, kv_ref, kr_ref = rest[-8:]
        proj = lax.dot_general(h_ref[...], wa_ref[...], _NT, preferred_element_type=F32)
        xp_ref[...] = proj[:, :POOL_DIM]
        ql = proj[:, o_q:o_kv]
        kvl = proj[:, o_kv:o_kr]
        ql_ref[...] = ql
        kvl_ref[...] = kvl
        qn = (ql * _rstd(ql) * gq_ref[...]).astype(BF16)
        kvn = (kvl * _rstd(kvl) * gkv_ref[...]).astype(BF16)
        qn_ref[...] = qn
        kvn_ref[...] = kvn
        c, sa, sb = c_ref[...], sa_ref[...], sb_ref[...]
        q = lax.dot_general(qn, wuq_ref[...], _NT, preferred_element_type=F32)
        for hh in range(H):
            base = hh * HEAD_W
            q_ref[:, base:base + QK_NOPE] = q[:, base:base + QK_NOPE].astype(BF16)
            q_ref[:, base + QK_NOPE:base + HEAD_W] = _rotate(
                q[:, base + QK_NOPE:base + HEAD_W], c, sa, sb, 1).astype(BF16)
        kv_ref[...] = jnp.dot(kvn, wukv_ref[...], preferred_element_type=F32).astype(BF16)
        kr_ref[...] = _rotate(proj[:, o_kr:o_kr + LANE], c, sa, sb, 1).astype(BF16)

    def row(w):
        return pl.BlockSpec((tm, w), lambda i: (i, 0))

    def whole(arr):
        return pl.BlockSpec(arr.shape, lambda i: (0,) * arr.ndim)

    gq2, gkv2 = gq.reshape(1, QL), gkv.reshape(1, KVL)
    outs = [(POOL_DIM, F32), (QL, F32), (KVL, F32), (QL, BF16), (KVL, BF16), (HQ, BF16), (HQ, BF16), (LANE, BF16)]
    return pl.pallas_call(
        body, name=name,
        out_shape=tuple(jax.ShapeDtypeStruct((T, w), dt) for w, dt in outs),
        grid=(T // tm,),
        in_specs=[row(D), pl.BlockSpec((PA, D), lambda i: (0, 0)), whole(wuq), whole(wukv), whole(gq2), whole(gkv2),
                  row(LANE), row(LANE), row(LANE)] + _dep_spec(dep),
        out_specs=tuple(row(w) for w, _ in outs),
        compiler_params=_params("parallel"),
    )(h, wa, wuq, wukv, gq2, gkv2, *tabs, *([] if dep is None else [dep]))


def _mixer_in_bwd(dq, dkv, dkr, ql, kvl, dxp, wuq, wukv, gq, gkv, tabs, *, name):
    T, HQ = dq.shape
    QL, KVL = wuq.shape[1], wukv.shape[0]
    H = HQ // HEAD_W
    PA = POOL_DIM + QL + KVL + LANE
    o_q, o_kv, o_kr = POOL_DIM, POOL_DIM + QL, POOL_DIM + QL + KVL
    tm = _tile(T, 512, 16)

    def norm_bwd(lat, gain, dn):
        r = _rstd(lat)
        xhat = lat * r
        dxh = dn * gain
        dlat = r * (dxh - xhat * jnp.mean(dxh * xhat, axis=-1, keepdims=True))
        return dlat, jnp.sum(dn * xhat, axis=0, keepdims=True)

    def body(dq_ref, dkv_ref, dkr_ref, ql_ref, kvl_ref, dxp_ref, wuq_ref, wukv_ref, gq_ref, gkv_ref,
             c_ref, sa_ref, sb_ref, dproj_ref, dqp_ref, dgq_ref, dgkv_ref):
        c, sa, sb = c_ref[...], sa_ref[...], sb_ref[...]
        for hh in range(H):
            base = hh * HEAD_W
            dqp_ref[:, base:base + QK_NOPE] = dq_ref[:, base:base + QK_NOPE]
            dqp_ref[:, base + QK_NOPE:base + HEAD_W] = _rotate(
                dq_ref[:, base + QK_NOPE:base + HEAD_W].astype(F32), c, sa, sb, -1).astype(BF16)
        dqn = jnp.dot(dqp_ref[...], wuq_ref[...], preferred_element_type=F32)
        dkvn = lax.dot_general(dkv_ref[...], wukv_ref[...], _NT, preferred_element_type=F32)
        dql, dgq = norm_bwd(ql_ref[...], gq_ref[...], dqn)
        dkvl, dgkv = norm_bwd(kvl_ref[...], gkv_ref[...], dkvn)
        dproj_ref[:, :POOL_DIM] = dxp_ref[...].astype(BF16)
        dproj_ref[:, o_q:o_kv] = dql.astype(BF16)
        dproj_ref[:, o_kv:o_kr] = dkvl.astype(BF16)
        dproj_ref[:, o_kr:PA] = _rotate(dkr_ref[...], c, sa, sb, -1).astype(BF16)

        @pl.when(pl.program_id(0) == 0)
        def _():
            dgq_ref[...] = dgq
            dgkv_ref[...] = dgkv

        @pl.when(pl.program_id(0) > 0)
        def _():
            dgq_ref[...] += dgq
            dgkv_ref[...] += dgkv

    def row(w):
        return pl.BlockSpec((tm, w), lambda i: (i, 0))

    def whole(arr):
        return pl.BlockSpec(arr.shape, lambda i: (0,) * arr.ndim)

    gq2, gkv2 = gq.reshape(1, QL), gkv.reshape(1, KVL)
    return pl.pallas_call(
        body, name=name,
        out_shape=(jax.ShapeDtypeStruct((T, PA), BF16), jax.ShapeDtypeStruct((T, HQ), BF16),
                   jax.ShapeDtypeStruct((1, QL), F32), jax.ShapeDtypeStruct((1, KVL), F32)),
        grid=(T // tm,),
        in_specs=[row(HQ), row(HQ), row(LANE), row(QL), row(KVL), row(POOL_DIM), whole(wuq), whole(wukv),
                  whole(gq2), whole(gkv2), row(LANE), row(LANE), row(LANE)],
        out_specs=(row(PA), row(HQ), whole(gq2), whole(gkv2)),
        compiler_params=_params("arbitrary"),
    )(dq, dkv, dkr, ql, kvl, dxp, wuq, wukv, gq2, gkv2, *tabs)


def _pool_groups(x_of, S):
    row = lax.broadcasted_iota(jnp.int32, (S, POOL_G), 0)
    for g, w in enumerate(POOL_WINDOWS):
        x = x_of(g)
        s = x
        d = 1
        while d < w:
            s = s + jnp.where(row >= d, pltpu.roll(s, d, 0), 0.0)
            d *= 2
        cnt = jnp.minimum(row + 1, w).astype(F32)
        yield g, w, x, s / cnt - x, cnt, row


def _pool_fwd(xp, maps, scale, *, S, name):
    T = xp.shape[0]

    def body(xp_ref, maps_ref, scale_ref, ms_ref):
        for g, _, _, pooled, _, _ in _pool_groups(lambda g: xp_ref[:, g * POOL_G:(g + 1) * POOL_G], S):
            mixed = jnp.dot(pooled.astype(BF16), maps_ref[g].astype(BF16), preferred_element_type=F32)
            ms_ref[:, g * POOL_G:(g + 1) * POOL_G] = (mixed * scale_ref[:, g * POOL_G:(g + 1) * POOL_G]).astype(BF16)

    return pl.pallas_call(
        body, name=name,
        out_shape=jax.ShapeDtypeStruct((T, POOL_DIM), BF16),
        grid=(T // S,),
        in_specs=[pl.BlockSpec((S, POOL_DIM), lambda b: (b, 0)),
                  pl.BlockSpec(maps.shape, lambda b: (0, 0, 0)),
                  pl.BlockSpec((1, POOL_DIM), lambda b: (0, 0))],
        out_specs=pl.BlockSpec((S, POOL_DIM), lambda b: (b, 0)),
        compiler_params=_params("parallel"),
    )(xp, maps, scale.reshape(1, POOL_DIM))


def _pool_bwd(xp, dms, maps, scale, *, S, name):
    T = xp.shape[0]

    def body(xp_ref, dms_ref, maps_ref, scale_ref, dxp_ref, dmaps_ref, dscale_ref):
        first = pl.program_id(0) == 0
        for g, w, _, pooled, cnt, row in _pool_groups(lambda g: xp_ref[:, g * POOL_G:(g + 1) * POOL_G], S):
            cols = slice(g * POOL_G, (g + 1) * POOL_G)
            pooled_b = pooled.astype(BF16)
            maps_b = maps_ref[g].astype(BF16)
            mixed = jnp.dot(pooled_b, maps_b, preferred_element_type=F32)
            dms = dms_ref[:, cols]
            dscale = jnp.sum(dms * mixed, axis=0, keepdims=True)
            dmixed = (dms * scale_ref[:, cols]).astype(BF16)
            dmaps = lax.dot_general(pooled_b, dmixed, (((0,), (0,)), ((), ())), preferred_element_type=F32)
            dpooled = lax.dot_general(dmixed, maps_b, (((1,), (1,)), ((), ())), preferred_element_type=F32)
            z = dpooled / cnt
            d = 1
            while d < w:
                z = z + jnp.where(row < S - d, pltpu.roll(z, S - d, 0), 0.0)
                d *= 2
            dxp_ref[:, cols] = z - dpooled

            @pl.when(first)
            def _():
                dmaps_ref[g] = dmaps
                dscale_ref[:, cols] = dscale

            @pl.when(jnp.logical_not(first))
            def _():
                dmaps_ref[g] += dmaps
                dscale_ref[:, cols] += dscale

    seq = pl.BlockSpec((S, POOL_DIM), lambda b: (b, 0))
    maps_spec = pl.BlockSpec(maps.shape, lambda b: (0, 0, 0))
    vec = pl.BlockSpec((1, POOL_DIM), lambda b: (0, 0))
    return pl.pallas_call(
        body, name=name,
        out_shape=(jax.ShapeDtypeStruct((T, POOL_DIM), F32), jax.ShapeDtypeStruct(maps.shape, F32),
                   jax.ShapeDtypeStruct((1, POOL_DIM), F32)),
        grid=(T // S,),
        in_specs=[seq, seq, maps_spec, vec],
        out_specs=(seq, maps_spec, vec),
        compiler_params=_params("arbitrary"),
    )(xp, dms, maps, scale.reshape(1, POOL_DIM))


def _causal_mask(s, t):
    r = lax.broadcasted_iota(jnp.int32, (t, t), 0)
    c = lax.broadcasted_iota(jnp.int32, (t, t), 1)
    return jnp.where(r >= c, s, MASK_VALUE)


_NT = (((1,), (1,)), ((), ()))
_TN = (((0,), (0,)), ((), ()))


def _attn_fwd(q, kv, kr, *, S, name):
    T, HQ = q.shape
    H = HQ // HEAD_W
    B = T // S
    t = _tile(S, ATTN_TILE)
    n = S // t

    def body(q_ref, k_ref, v_ref, kr_ref, o_ref, lse_ref, kcat):
        kcat[:, :QK_NOPE] = k_ref[...]
        kcat[:, QK_NOPE:] = kr_ref[...]
        for i in range(n):
            rows = slice(i * t, (i + 1) * t)
            qt = q_ref[rows, :]
            m = jnp.full((t, 1), MASK_VALUE, F32)
            l = jnp.zeros((t, 1), F32)
            acc = jnp.zeros((t, V_DIM), F32)
            for j in range(i + 1):
                cols = slice(j * t, (j + 1) * t)
                s = lax.dot_general(qt, kcat[cols, :], _NT, preferred_element_type=F32) * ATTN_SCALE_LOG2
                if j == i:
                    s = _causal_mask(s, t)
                m_new = jnp.maximum(m, jnp.max(s, axis=1, keepdims=True))
                p = jnp.exp2(s - m_new)
                corr = jnp.exp2(m - m_new)
                l = corr * l + jnp.sum(p, axis=1, keepdims=True)
                acc = corr * acc + jnp.dot(p.astype(BF16), v_ref[cols, :], preferred_element_type=F32)
                m = m_new
            o_ref[rows, :] = (acc / l).astype(BF16)
            lse_ref[rows, :] = jnp.broadcast_to(m + jnp.log2(l), (t, LANE))

    seq_h = pl.BlockSpec((S, LANE), lambda b, h: (b, h))
    return pl.pallas_call(
        body, name=name,
        out_shape=(jax.ShapeDtypeStruct((T, H * V_DIM), BF16), jax.ShapeDtypeStruct((T, H * LANE), F32)),
        grid=(B, H),
        in_specs=[pl.BlockSpec((S, HEAD_W), lambda b, h: (b, h)),
                  pl.BlockSpec((S, QK_NOPE), lambda b, h: (b, 2 * h)),
                  pl.BlockSpec((S, V_DIM), lambda b, h: (b, 2 * h + 1)),
                  pl.BlockSpec((S, LANE), lambda b, h: (b, 0))],
        out_specs=(seq_h, seq_h),
        scratch_shapes=[pltpu.VMEM((S, HEAD_W), BF16)],
        compiler_params=_params("parallel", "parallel"),
    )(q, kv, kv, kr)


def _attn_bwd(q, kv, kr, o, do, lse, *, S, name):
    T, HQ = q.shape
    H = HQ // HEAD_W
    B = T // S
    t = _tile(S, ATTN_TILE)
    n = S // t

    def body(q_ref, k_ref, v_ref, kr_ref, o_ref, do_ref, lse_ref, dq_ref, dkv_ref, dkr_ref, kcat, dq_acc):
        @pl.when(pl.program_id(1) == 0)
        def _():
            dkr_ref[...] = jnp.zeros_like(dkr_ref)

        kcat[:, :QK_NOPE] = k_ref[...]
        kcat[:, QK_NOPE:] = kr_ref[...]
        delta = [jnp.sum(do_ref[i * t:(i + 1) * t, :].astype(F32) * o_ref[i * t:(i + 1) * t, :].astype(F32),
                         axis=1, keepdims=True) for i in range(n)]
        for j in range(n):
            cols = slice(j * t, (j + 1) * t)
            kc = kcat[cols, :]
            vt = v_ref[cols, :]
            dk = jnp.zeros((t, HEAD_W), F32)
            dv = jnp.zeros((t, V_DIM), F32)
            for i in range(j, n):
                rows = slice(i * t, (i + 1) * t)
                qt = q_ref[rows, :]
                dot_ = do_ref[rows, :]
                s = lax.dot_general(qt, kc, _NT, preferred_element_type=F32) * ATTN_SCALE_LOG2
                if i == j:
                    s = _causal_mask(s, t)
                p = jnp.exp2(s - lse_ref[rows, :][:, :1])
                dv = dv + lax.dot_general(p.astype(BF16), dot_, _TN, preferred_element_type=F32)
                dp = lax.dot_general(dot_, vt, _NT, preferred_element_type=F32)
                ds = (p * (dp - delta[i]) * ATTN_SCALE).astype(BF16)
                dk = dk + lax.dot_general(ds, qt, _TN, preferred_element_type=F32)
                dq_part = jnp.dot(ds, kc, preferred_element_type=F32)
                if j == 0:
                    dq_acc[rows, :] = dq_part
                else:
                    dq_acc[rows, :] += dq_part
            dkv_ref[cols, :QK_NOPE] = dk[:, :QK_NOPE].astype(BF16)
            dkv_ref[cols, QK_NOPE:] = dv.astype(BF16)
            dkr_ref[cols, :] += dk[:, QK_NOPE:]
        dq_ref[...] = dq_acc[...].astype(BF16)

    seq_q = pl.BlockSpec((S, HEAD_W), lambda b, h: (b, h))
    seq_h = pl.BlockSpec((S, LANE), lambda b, h: (b, h))
    seq_shared = pl.BlockSpec((S, LANE), lambda b, h: (b, 0))
    return pl.pallas_call(
        body, name=name,
        out_shape=(jax.ShapeDtypeStruct((T, HQ), BF16), jax.ShapeDtypeStruct((T, HQ), BF16),
                   jax.ShapeDtypeStruct((T, LANE), F32)),
        grid=(B, H),
        in_specs=[seq_q,
                  pl.BlockSpec((S, QK_NOPE), lambda b, h: (b, 2 * h)),
                  pl.BlockSpec((S, V_DIM), lambda b, h: (b, 2 * h + 1)),
                  seq_shared, seq_h, seq_h, seq_h],
        out_specs=(seq_q, seq_q, seq_shared),
        scratch_shapes=[pltpu.VMEM((S, HEAD_W), BF16), pltpu.VMEM((S, HEAD_W), F32)],
        compiler_params=_params("parallel", "arbitrary"),
    )(q, kv, kv, kr, o, do, lse)


def _merge_out(h, ms, o, x, wgate, bgate, wpp, wap, wout, next_gain, *, name):
    T, D = x.shape
    tm = _tile(T, 256, 16)

    def body(h_ref, ms_ref, o_ref, x_ref, wgate_ref, bgate_ref, wpp_ref, wap_ref, wout_ref, ng_ref,
             gates_ref, ba_ref, bb_ref, merged_ref, xn_ref, hn_ref):
        logits = lax.dot_general(h_ref[...], wgate_ref[...], _NT, preferred_element_type=F32) + bgate_ref[...]
        gates = jax.nn.sigmoid(logits)
        ba = jnp.dot(ms_ref[...], wpp_ref[...], preferred_element_type=F32)
        bb = jnp.dot(o_ref[...], wap_ref[...], preferred_element_type=F32)
        merged = (gates[:, :D] * ba + gates[:, D:] * bb).astype(BF16)
        gates_ref[...] = gates.astype(BF16)
        ba_ref[...] = ba.astype(BF16)
        bb_ref[...] = bb.astype(BF16)
        merged_ref[...] = merged
        xn = x_ref[...] + jnp.dot(merged, wout_ref[...], preferred_element_type=F32)
        xn_ref[...] = xn
        hn_ref[...] = (xn * _rstd(xn) * ng_ref[...]).astype(BF16)

    def row(w):
        return pl.BlockSpec((tm, w), lambda i: (i, 0))

    def whole(arr):
        return pl.BlockSpec(arr.shape, lambda i: (0,) * arr.ndim)

    bg2, ng2 = bgate.reshape(1, 2 * D), next_gain.reshape(1, D)
    act = jax.ShapeDtypeStruct((T, D), BF16)
    return pl.pallas_call(
        body, name=name,
        out_shape=(jax.ShapeDtypeStruct((T, 2 * D), BF16), act, act, act, jax.ShapeDtypeStruct((T, D), F32), act),
        grid=(T // tm,),
        in_specs=[row(D), row(ms.shape[1]), row(o.shape[1]), row(D), whole(wgate), whole(bg2), whole(wpp),
                  whole(wap), whole(wout), whole(ng2)],
        out_specs=(row(2 * D), row(D), row(D), row(D), row(D), row(D)),
        compiler_params=_params("parallel"),
    )(h, ms, o, x, wgate, bg2, wpp, wap, wout, ng2)


def _merge_bwd(dxo, wout, gates, ba, bb, *, name, dep=None):
    T, D = dxo.shape
    tm = _tile(T, 512, 16)

    def body(dxo_ref, wout_ref, gates_ref, ba_ref, bb_ref, *rest):
        dba_ref, dbb_ref, dgl_ref, dbg_ref = rest[-4:]
        dm = lax.dot_general(dxo_ref[...].astype(BF16), wout_ref[...], _NT, preferred_element_type=F32)
        ga = gates_ref[:, :D].astype(F32)
        gb = gates_ref[:, D:].astype(F32)
        dba_ref[...] = (dm * ga).astype(BF16)
        dbb_ref[...] = (dm * gb).astype(BF16)
        dgl_a = dm * ba_ref[...].astype(F32) * (ga * (1.0 - ga))
        dgl_b = dm * bb_ref[...].astype(F32) * (gb * (1.0 - gb))
        dgl_ref[:, :D] = dgl_a.astype(BF16)
        dgl_ref[:, D:] = dgl_b.astype(BF16)
        sa = jnp.sum(dgl_a, axis=0, keepdims=True)
        sb = jnp.sum(dgl_b, axis=0, keepdims=True)

        @pl.when(pl.program_id(0) == 0)
        def _():
            dbg_ref[:, :D] = sa
            dbg_ref[:, D:] = sb

        @pl.when(pl.program_id(0) > 0)
        def _():
            dbg_ref[:, :D] += sa
            dbg_ref[:, D:] += sb

    def row(w):
        return pl.BlockSpec((tm, w), lambda i: (i, 0))

    act = jax.ShapeDtypeStruct((T, D), BF16)
    return pl.pallas_call(
        body, name=name,
        out_shape=(act, act, jax.ShapeDtypeStruct((T, 2 * D), BF16), jax.ShapeDtypeStruct((1, 2 * D), F32)),
        grid=(T // tm,),
        in_specs=[row(D), pl.BlockSpec(wout.shape, lambda i: (0, 0)), row(2 * D), row(D), row(D)] + _dep_spec(dep),
        out_specs=(row(D), row(D), row(2 * D), pl.BlockSpec((1, 2 * D), lambda i: (0, 0))),
        compiler_params=_params("arbitrary"),
    )(dxo, wout, gates, ba, bb, *([] if dep is None else [dep]))


def _ffn_fwd(x, h, w, tag, next_gain, dep=None):
    gate, up, a = _ffn_up(h, w["up_t"], name=f"{tag}_up", dep=dep)
    if next_gain is None:
        xn, hn = _mm(a, w["wd"], res=x, alpha=0.5, name=f"{tag}_down_last", tk=2816), None
    else:
        xn, hn = _mm(a, w["wd"], res=x, alpha=0.5, norm_gain=next_gain, name=f"{tag}_down", tk=2816)
    return xn, hn, (x, h, gate, up, a)


def _ffn_bwd(dxo, gain, w, saved, tag, dep=None):
    x, h, gate, up, a = saved
    F = gate.shape[1]
    dgate, dup = _ffn_bwd_act(dxo, w["wd"], gate, up, alpha=0.5, name=f"{tag}_bwd_act", dep=dep)
    dwd = _mm(a, dxo, ta=True, alpha=0.5, out_dtype=BF16, name=f"{tag}_dwd", tm=1408, tn=1024, tk=1024)
    dup_t = _ffn_dw_up(dgate, dup, h, name=f"{tag}_dw_up")
    dx, dgain = _dh_norm_bwd(dgate, w["up_t"], dup, w["up_t"], F, x, gain, dxo, name=f"{tag}_dh_norm_bwd")
    return dx, dgain, dup_t, dwd


def _mixer_fwd(x, h, p, w, tabs, S, next_gain, dep=None):
    xp, ql, kvl, qn, kvn, q, kv, kr = _mixer_in(h, w["win_t"], w["wuq_t"], w["wukv"], p["q_latent_norm"],
                                                 p["kv_latent_norm"], tabs, name="mix_in", dep=dep)
    ms = _pool_fwd(xp, p["pool_maps"], p["pool_scale"], S=S, name="pool_fwd")
    o, lse = _attn_fwd(q, kv, kr, S=S, name="attn_fwd")
    gates, ba, bb, merged, xn, hn = _merge_out(h, ms, o, x, w["wgate_t"], p["b_gate"], w["wpp"], w["wap"], w["wout"],
                                               next_gain, name="merge_out")
    return xn, hn, (x, h, xp, ql, kvl, qn, kvn, q, kv, kr, ms, o, lse, gates, ba, bb, merged)


def _mixer_bwd(dxo, p, w, tabs, saved, S, dep=None):
    x, h, xp, ql, kvl, qn, kvn, q, kv, kr, ms, o, lse, gates, ba, bb, merged = saved
    dba, dbb, dgl, dbg = _merge_bwd(dxo, w["wout"], gates, ba, bb, name="merge_bwd", dep=dep)
    g = {}
    g["wout"] = _mm(merged, dxo, ta=True, out_dtype=BF16, name="d_wout", tm=1024, tk=1024)
    g["wpp"] = _mm(ms, dba, ta=True, out_dtype=BF16, name="d_wpp", tk=2048)
    g["wap"] = _mm(o, dbb, ta=True, out_dtype=BF16, name="d_wap", tm=1024, tk=2048)
    dms = _mm(dba, w["wpp"], tb=True, name="d_ms")
    do = _mm(dbb, w["wap"], tb=True, out_dtype=BF16, name="d_o")
    dxp, g["pool_maps"], g["pool_scale"] = _pool_bwd(xp, dms, p["pool_maps"], p["pool_scale"], S=S, name="pool_bwd")
    dq, dkv, dkr = _attn_bwd(q, kv, kr, o, do, lse, S=S, name="attn_bwd")
    dproj, dqp, g["q_latent_norm"], g["kv_latent_norm"] = _mixer_in_bwd(
        dq, dkv, dkr, ql, kvl, dxp, w["wuq_t"], w["wukv"], p["q_latent_norm"], p["kv_latent_norm"], tabs,
        name="mix_in_bwd")
    g["wuq_t"] = _mm(dqp, qn, ta=True, out_dtype=BF16, name="d_wuq", tm=2048, tk=2048)
    g["wukv"] = _mm(kvn, dkv, ta=True, out_dtype=BF16, name="d_wukv", tn=2048, tk=2048)
    g["wa_t"] = _mm(dproj, h, ta=True, out_dtype=BF16, name="d_wa", tm=1280, tn=1024, tk=2048)
    g["wgate_t"] = _mm(dgl, h, ta=True, out_dtype=BF16, name="d_wgate", tm=2048, tn=1024, tk=1024)
    dx, g["norm_mix"] = _dh_norm_bwd(dproj, w["win_t"], dgl, w["wgate_t"], 0, x, p["norm_mix"], dxo,
                                     name="mix_dh_norm_bwd")
    g["b_gate"] = dbg
    return dx, g


BIG = ("ffn1_up", "ffn1_down", "w_in", "w_pool_proj", "w_uq", "w_ukv", "w_attn_proj", "w_out", "ffn2_up", "ffn2_down")
SMALL = ("norm_ffn1", "norm_mix", "b_gate", "pool_maps", "pool_scale", "q_latent_norm", "kv_latent_norm", "norm_ffn2")
PACKED = ("w_pool_proj", "w_uq", "w_ukv")
TRANSPOSED = ("ffn1_up", "ffn2_up", "w_in", "w_uq")
COL_SHARDED = ("w_pool_proj", "w_ukv")
QK_HEAD = QK_NOPE + QK_ROPE


def _rows(stacked):
    n, r, c = stacked.shape
    return stacked.reshape(n * r, c)


def _cols(stacked):
    n, k, c = stacked.shape
    return stacked.transpose(1, 0, 2).reshape(k, n * c)


FFN1_PART = ("ffn1_up", "ffn1_down")
MIXER_PART = ("w_in", "w_attn_proj", "w_out") + PACKED
FFN2_PART = ("ffn2_up", "ffn2_down")


def _kernel_weights(stacked):
    full = {}
    for tag in ("ffn1", "ffn2"):
        if tag + "_up" in stacked:
            full[tag] = {"up_t": _rows(stacked[tag + "_up"]), "wd": _rows(stacked[tag + "_down"])}
    if "w_in" in stacked:
        win_t = _rows(stacked["w_in"])
        D = win_t.shape[1]
        wuq_t = _rows(stacked["w_uq"])
        QL = wuq_t.shape[1]
        H = wuq_t.shape[0] // QK_HEAD
        wuq_t = jnp.pad(wuq_t.reshape(H, QK_HEAD, QL), ((0, 0), (0, HEAD_W - QK_HEAD), (0, 0)))
        full.update({"win_t": win_t, "wgate_t": win_t[win_t.shape[0] - 2 * D:], "wuq_t": wuq_t.reshape(H * HEAD_W, QL),
                     "wukv": _cols(stacked["w_ukv"]), "wpp": _cols(stacked["w_pool_proj"]),
                     "wap": _rows(stacked["w_attn_proj"]), "wout": _rows(stacked["w_out"])})
    return full


def _split_rows(full):
    return full.reshape(N_DEV, full.shape[0] // N_DEV, full.shape[1])


def _split_cols(full):
    k, cols = full.shape
    return full.reshape(k, N_DEV, cols // N_DEV).transpose(1, 0, 2)


def _mixer_grads_stacked(g):
    n_a = g["wa_t"].shape[0] - (LANE - QK_ROPE)
    HQ, QL = g["wuq_t"].shape
    H = HQ // HEAD_W
    wuq_t = g["wuq_t"].reshape(H, HEAD_W, QL)[:, :QK_HEAD].reshape(H * QK_HEAD, QL)
    return {"w_in": _split_rows(jnp.concatenate([g["wa_t"][:n_a], g["wgate_t"]], axis=0)),
            "w_uq": _split_rows(wuq_t),
            "w_pool_proj": _split_cols(g["wpp"]), "w_ukv": _split_cols(g["wukv"]),
            "w_attn_proj": _split_rows(g["wap"]), "w_out": _split_rows(g["wout"])}


def _mesh_place():
    x, y, c = lax.axis_index("x"), lax.axis_index("y"), lax.axis_index("c")
    chips = [(1 - x, y), (x, 1 - y), (1 - x, 1 - y)]
    return x, y, c, chips


HBM = pl.BlockSpec(memory_space=pltpu.HBM)
SEMAPHORES = pl.BlockSpec(memory_space=pltpu.SEMAPHORE)
DATAFLOW = pltpu.SideEffectType.DATAFLOW_SIDE_EFFECTING
GATHER_PEERS = 4
SCATTER_PEERS = 7


def _in_hbm(a):
    return pltpu.with_memory_space_constraint(a, pltpu.HBM)


def _gather_plan(src_refs, land_refs):
    x, y, c, chips = _mesh_place()
    me = 4 * x + 2 * y + c
    targets = [(x, y, 1 - c)] + [(cx, cy, c) for cx, cy in chips]
    return [(s, land.at[me], to) for s, land in zip(src_refs, land_refs) for to in targets]


def _scatter_plan(src_refs, land_refs):
    x, y, c, _ = _mesh_place()
    peers = [(x, y, 1 - c), (1 - x, y, c), (x, 1 - y, c), (1 - x, 1 - y, c),
             (1 - x, y, 1 - c), (x, 1 - y, 1 - c), (1 - x, 1 - y, 1 - c)]
    return [(s.at[4 * px + 2 * py + pc], land.at[k], (px, py, pc))
            for s, land in zip(src_refs, land_refs) for k, (px, py, pc) in enumerate(peers)]


def _descriptors(plan, src_refs, land_refs, send_sems, recv_sems):
    return [pltpu.make_async_remote_copy(src_ref=s, dst_ref=d, send_sem=send_sems.at[k], recv_sem=recv_sems.at[k],
                                         device_id=to, device_id_type=MESH)
            for k, (s, d, to) in enumerate(plan(src_refs, land_refs))]


def _exchange(srcs, land_shapes, plan, per_src, *, name):
    n = len(srcs)

    def body(*refs):
        copies = _descriptors(plan, refs[:n], refs[n:2 * n], refs[2 * n], refs[2 * n + 1])
        for cp in copies:
            cp.start()
        for cp in copies:
            cp.wait()

    return pl.pallas_call(
        body, name=name,
        out_shape=tuple(jax.ShapeDtypeStruct(shape, s.dtype) for shape, s in zip(land_shapes, srcs)),
        in_specs=[ANY] * n, out_specs=(ANY,) * n,
        scratch_shapes=[pltpu.SemaphoreType.DMA((per_src * n,)), pltpu.SemaphoreType.DMA((per_src * n,))],
    )(*srcs)


FORWARD_COPIES = 4


def _forward_slots():
    x, y, c, chips = _mesh_place()
    return [4 * cx + 2 * cy + c for cx, cy in chips] + [4 * x + 2 * y + (1 - c)], (x, y, 1 - c)


def _forward_plan(src_refs, land_refs):
    slots, sibling = _forward_slots()
    return [(land.at[s], land.at[s], sibling) for land in land_refs for s in slots]


def _gather_all_plan(src_refs, land_refs):
    x, y, c, _ = _mesh_place()
    me = 4 * x + 2 * y + c
    peers = [(x, y, 1 - c), (1 - x, y, c), (x, 1 - y, c), (1 - x, 1 - y, c),
             (1 - x, y, 1 - c), (x, 1 - y, 1 - c), (1 - x, 1 - y, 1 - c)]
    return [(s, land.at[me], to) for s, land in zip(src_refs, land_refs) for to in peers]


def _exchange_start(srcs, lands, plan, n_copies, *, name):
    ns, n = len(srcs), len(srcs) + len(lands)

    def body(*refs):
        for cp in _descriptors(plan, refs[:ns], refs[ns:n], refs[n], refs[n + 1]):
            cp.start()
        refs[-1][...] = jnp.zeros_like(refs[-1])

    sems = pltpu.SemaphoreType.DMA((n_copies,))
    out = pl.pallas_call(
        body, name=name,
        out_shape=(sems, sems, *[pltpu.HBM(a.shape, a.dtype) for a in srcs + lands],
                   jax.ShapeDtypeStruct((8, LANE), F32)),
        in_specs=(HBM,) * n,
        out_specs=(SEMAPHORES, SEMAPHORES, *[HBM] * n, pl.BlockSpec(memory_space=pltpu.VMEM)),
        input_output_aliases={i: 2 + i for i in range(n)},
        compiler_params=pltpu.CompilerParams(has_side_effects=DATAFLOW),
    )(*[_in_hbm(a) for a in srcs + lands])
    return out[0], out[1], list(out[2:2 + ns]), list(out[2 + ns:2 + n]), out[-1]


def _exchange_wait(send_sems, recv_sems, srcs, lands, plan, after, *, name):
    ns, n = len(srcs), len(srcs) + len(lands)

    def body(*refs):
        for cp in _descriptors(plan, refs[:ns], refs[ns:n], refs[n], refs[n + 1]):
            cp.wait_send()
            cp.wait_recv()

    out = pl.pallas_call(
        body, name=name,
        out_shape=tuple(pltpu.HBM(a.shape, a.dtype) for a in srcs + lands),
        in_specs=(*[HBM] * n, SEMAPHORES, SEMAPHORES, ANY),
        out_specs=(HBM,) * n,
        input_output_aliases={i: i for i in range(n)},
        compiler_params=pltpu.CompilerParams(has_side_effects=DATAFLOW),
    )(*srcs, *lands, send_sems, recv_sems, after)
    return list(out[:ns]), list(out[ns:])


def _gather_forward(lands, *, name):
    n = len(lands)

    def body(*refs):
        in_refs, out_refs = refs[:n], refs[n:2 * n]
        token, send_sems, recv_sems = refs[2 * n:2 * n + 3]
        slots, sibling = _forward_slots()
        passed = [pltpu.make_async_remote_copy(
            src_ref=i.at[s], dst_ref=o.at[s],
            send_sem=send_sems.at[FORWARD_COPIES * b + j], recv_sem=recv_sems.at[FORWARD_COPIES * b + j],
            device_id=sibling, device_id_type=MESH)
            for b, (i, o) in enumerate(zip(in_refs, out_refs)) for j, s in enumerate(slots)]
        for cp in passed:
            cp.start()
        for cp in passed:
            cp.wait()
        token[...] = jnp.zeros_like(token)

    out = pl.pallas_call(
        body, name=name,
        out_shape=(*[jax.ShapeDtypeStruct(a.shape, a.dtype) for a in lands], jax.ShapeDtypeStruct((8, LANE), F32)),
        in_specs=[ANY] * n,
        out_specs=(*[ANY] * n, pl.BlockSpec(memory_space=pltpu.VMEM)),
        input_output_aliases={i: i for i in range(n)},
        scratch_shapes=[pltpu.SemaphoreType.DMA((FORWARD_COPIES * n,)), pltpu.SemaphoreType.DMA((FORWARD_COPIES * n,))],
    )(*lands)
    return list(out[:n]), out[n]


def _scatter_sum(parts, got, me, *, name):
    shard = parts.shape[1:]
    cols = shard[-1]
    rows = int(np.prod(shard[:-1]))
    tr = _tile(rows, 256, 16)

    def body(me_ref, p_ref, g_ref, o_ref):
        acc = p_ref[...].astype(F32)
        for k in range(SCATTER_PEERS):
            acc = acc + g_ref[k].astype(F32)
        o_ref[...] = acc

    out = pl.pallas_call(
        body, name=name,
        out_shape=jax.ShapeDtypeStruct((rows, cols), F32),
        grid_spec=pltpu.PrefetchScalarGridSpec(
            num_scalar_prefetch=1, grid=(rows // tr,),
            in_specs=[pl.BlockSpec((None, tr, cols), lambda r, me_ref: (me_ref[0], r, 0)),
                      pl.BlockSpec((SCATTER_PEERS, tr, cols), lambda r, me_ref: (0, r, 0))],
            out_specs=pl.BlockSpec((tr, cols), lambda r, me_ref: (r, 0))),
        compiler_params=_params("parallel"),
    )(me, parts.reshape(N_DEV, rows, cols), got.reshape(SCATTER_PEERS, rows, cols))
    return out.reshape(shard)


def _sum_devices(parts, *, name):
    _, R, C = parts.shape
    tr = _tile(R, 512, 8)

    def body(p_ref, o_ref):
        acc = p_ref[0]
        for d in range(1, N_DEV):
            acc = acc + p_ref[d]
        o_ref[...] = acc

    return pl.pallas_call(
        body, name=name,
        out_shape=jax.ShapeDtypeStruct((R, C), F32),
        grid=(R // tr,),
        in_specs=[pl.BlockSpec((N_DEV, tr, C), lambda r: (0, r, 0))],
        out_specs=pl.BlockSpec((tr, C), lambda r: (r, 0)),
        compiler_params=_params("parallel"),
    )(parts)


def _adamw(w, g, m, v, *, name, dep=None):
    shape = w.shape
    cols = shape[-1]
    rows = w.size // cols
    tr = _tile(rows, 256, 8)

    def body(w_ref, g_ref, m_ref, v_ref, *rest):
        d_ref, nm_ref, nv_ref = rest[-3:]
        g = g_ref[...]
        m = ADAM_B1 * m_ref[...] + (1.0 - ADAM_B1) * g
        v = ADAM_B2 * v_ref[...] + (1.0 - ADAM_B2) * jnp.square(g)
        m_hat = m / (1.0 - ADAM_B1 ** ADAM_STEP)
        v_hat = v / (1.0 - ADAM_B2 ** ADAM_STEP)
        d_ref[...] = -ADAM_LR * (m_hat / (jnp.sqrt(v_hat) + ADAM_EPS) + ADAM_WD * w_ref[...])
        nm_ref[...] = m
        nv_ref[...] = v

    spec = pl.BlockSpec((tr, cols), lambda i: (i, 0))
    out = jax.ShapeDtypeStruct((rows, cols), F32)
    d, nm, nv = pl.pallas_call(
        body, name=name,
        out_shape=(out, out, out),
        grid=(rows // tr,),
        in_specs=[spec] * 4 + _dep_spec(dep), out_specs=(spec,) * 3,
        compiler_params=_params("parallel"),
    )(*(a.reshape(rows, cols) for a in (w, g, m, v)), *([] if dep is None else [dep]))
    return d.reshape(shape), nm.reshape(shape), nv.reshape(shape)


PACK_ALIGN = 16 * LANE


def _pack(pieces, lead):
    out = []
    for p in pieces:
        keep = p.shape[:lead]
        flat = p.reshape(*keep, -1)
        pad = (-flat.shape[-1]) % PACK_ALIGN
        if pad:
            flat = jnp.pad(flat, [(0, 0)] * lead + [(0, pad)])
        out.append(flat.reshape(*keep, -1, LANE))
    return jnp.concatenate(out, axis=lead)


def _unpack(buf, shapes, lead):
    keep = buf.shape[:lead]
    out, row = [], 0
    for shape in shapes:
        size = int(np.prod(shape))
        rows = -(-size // PACK_ALIGN) * (PACK_ALIGN // LANE)
        piece = lax.slice_in_dim(buf, row, row + rows, axis=lead).reshape(*keep, rows * LANE)
        out.append(lax.slice_in_dim(piece, 0, size, axis=lead).reshape(*keep, *shape))
        row += rows
    return out


def kernel(x, positions, norm_ffn1, ffn1_up, ffn1_down, norm_mix, w_in, b_gate, pool_maps, pool_scale, w_pool_proj, q_latent_norm, w_uq, kv_latent_norm, w_ukv, w_attn_proj, w_out, norm_ffn2, ffn2_up, ffn2_down, final_norm, loss_target, m_norm_ffn1, m_ffn1_up, m_ffn1_down, m_norm_mix, m_w_in, m_b_gate, m_pool_maps, m_pool_scale, m_w_pool_proj, m_q_latent_norm, m_w_uq, m_kv_latent_norm, m_w_ukv, m_w_attn_proj, m_w_out, m_norm_ffn2, m_ffn2_up, m_ffn2_down, m_final_norm, v_norm_ffn1, v_ffn1_up, v_ffn1_down, v_norm_mix, v_w_in, v_b_gate, v_pool_maps, v_pool_scale, v_w_pool_proj, v_q_latent_norm, v_w_uq, v_kv_latent_norm, v_w_ukv, v_w_attn_proj, v_w_out, v_norm_ffn2, v_ffn2_up, v_ffn2_down, v_final_norm):
    order = ("norm_ffn1", "ffn1_up", "ffn1_down", "norm_mix", "w_in", "b_gate", "pool_maps", "pool_scale",
             "w_pool_proj", "q_latent_norm", "w_uq", "kv_latent_norm", "w_ukv", "w_attn_proj", "w_out",
             "norm_ffn2", "ffn2_up", "ffn2_down", "final_norm")
    w = dict(zip(order, (norm_ffn1, ffn1_up, ffn1_down, norm_mix, w_in, b_gate, pool_maps, pool_scale, w_pool_proj,
                         q_latent_norm, w_uq, kv_latent_norm, w_ukv, w_attn_proj, w_out, norm_ffn2, ffn2_up,
                         ffn2_down, final_norm)))
    m = dict(zip(order, (m_norm_ffn1, m_ffn1_up, m_ffn1_down, m_norm_mix, m_w_in, m_b_gate, m_pool_maps, m_pool_scale,
                         m_w_pool_proj, m_q_latent_norm, m_w_uq, m_kv_latent_norm, m_w_ukv, m_w_attn_proj, m_w_out,
                         m_norm_ffn2, m_ffn2_up, m_ffn2_down, m_final_norm)))
    v = dict(zip(order, (v_norm_ffn1, v_ffn1_up, v_ffn1_down, v_norm_mix, v_w_in, v_b_gate, v_pool_maps, v_pool_scale,
                         v_w_pool_proj, v_q_latent_norm, v_w_uq, v_kv_latent_norm, v_w_ukv, v_w_attn_proj, v_w_out,
                         v_norm_ffn2, v_ffn2_up, v_ffn2_down, v_final_norm)))
    L = norm_ffn1.shape[0]
    B, S, D = x.shape
    T = B * S

    def turned(a, n):
        return a.transpose(0, 2, 1) if n in TRANSPOSED else a

    wk, mk, vk = ({n: turned(d[n], n) for n in order} for d in (w, m, v))
    packed_shapes = [wk[n].shape[1:] for n in PACKED]
    my_slot = 4 * lax.axis_index("x") + 2 * lax.axis_index("y") + lax.axis_index("c")
    me = jnp.stack([my_slot]).astype(jnp.int32)

    def weight_blocks(l, names, token):
        zero = token[0, 0].astype(BF16)
        blocks = [wk[n][l].astype(BF16) + zero for n in names if n not in PACKED]
        if any(n in PACKED for n in names):
            blocks.append(_pack([wk[n][l].astype(BF16) + zero for n in PACKED], 0))
        return blocks

    def kernel_weights(names, lands):
        direct = [n for n in names if n not in PACKED]
        stacked = dict(zip(direct, lands))
        if len(lands) > len(direct):
            stacked.update(zip(PACKED, _unpack(lands[-1], packed_shapes, 1)))
        return _kernel_weights(stacked)

    def gather_start(l, names, token, tag):
        blocks = weight_blocks(l, names, token)
        lands = [lax.empty((N_DEV, *b.shape), b.dtype) for b in blocks]
        send_sems, recv_sems, blocks, lands, token = _exchange_start(
            blocks, lands, _gather_plan, GATHER_PEERS * len(blocks), name=f"gather_start_{tag}")
        return (send_sems, recv_sems, blocks, lands, tag), token

    def gather_wait(state, after):
        send_sems, recv_sems, blocks, lands, tag = state
        return _exchange_wait(send_sems, recv_sems, blocks, lands, _gather_plan, after, name=f"gather_wait_{tag}")[1]

    layer_part = FFN1_PART + MIXER_PART + FFN2_PART
    tabs = _rope_tables(positions.reshape(T))
    xs = x.reshape(T, D)
    h = _rms_fwd(xs, w["norm_ffn1"][0], name="first_norm")
    full, saved = [], []

    p = {n: w[n][0] for n in SMALL}
    blocks = weight_blocks(0, FFN1_PART, jnp.zeros((8, LANE), F32))
    lands = _exchange(blocks, [(N_DEV, *b.shape) for b in blocks], _gather_plan, GATHER_PEERS, name="gather_first")
    lands, token = _gather_forward(lands, name="gather_forward")
    w0 = kernel_weights(FFN1_PART, lands)
    state, token = gather_start(0, MIXER_PART, token, "0_mix")
    xs, h, s1 = _ffn_fwd(xs, h, w0["ffn1"], "ffn1", p["norm_mix"], dep=token)
    lands, token = _gather_forward(gather_wait(state, xs), name="gather_forward")
    w0.update(kernel_weights(MIXER_PART, lands))
    state, token = gather_start(0, FFN2_PART, token, "0_ffn2")
    if L > 1:
        next_state, token = gather_start(1, layer_part, token, "1")
    xs, h, s2 = _mixer_fwd(xs, h, p, w0, tabs, S, p["norm_ffn2"], dep=token)
    lands, token = _gather_forward(gather_wait(state, xs), name="gather_forward")
    w0.update(kernel_weights(FFN2_PART, lands))
    xs, h, s3 = _ffn_fwd(xs, h, w0["ffn2"], "ffn2", w["norm_ffn1"][1] if L > 1 else None, dep=token)
    if L > 1:
        lands, token = _gather_forward(gather_wait(next_state, xs), name="gather_forward")
    full.append(w0)
    saved.append((s1, s2, s3))

    for l in range(1, L):
        full.append(kernel_weights(layer_part, lands))
        more = l + 1 < L
        p = {n: w[n][l] for n in SMALL}
        if more:
            state, token = gather_start(l + 1, layer_part, token, f"{l + 1}")
        xs, h, s1 = _ffn_fwd(xs, h, full[l]["ffn1"], "ffn1", p["norm_mix"], dep=token if more else None)
        xs, h, s2 = _mixer_fwd(xs, h, p, full[l], tabs, S, p["norm_ffn2"])
        if more:
            lands = gather_wait(state, xs)
            send_sems, recv_sems, _, lands, token = _exchange_start(
                [], lands, _forward_plan, FORWARD_COPIES * len(lands), name=f"forward_start_{l + 1}")
        xs, h, s3 = _ffn_fwd(xs, h, full[l]["ffn2"], "ffn2", w["norm_ffn1"][l + 1] if more else None,
                             dep=token if more else None)
        if more:
            _, lands = _exchange_wait(send_sems, recv_sems, [], lands, _forward_plan, xs, name=f"forward_wait_{l + 1}")
        saved.append((s1, s2, s3))
    dx, dfinal, loss = _loss_head(xs, final_norm, loss_target.reshape(T, D), name="loss_head")

    big_grads = {n: [None] * L for n in BIG}
    small_grads_of = [None] * L
    pending = None

    def scatter_start(names, stacked, tag):
        srcs = [stacked[n] for n in names if n not in PACKED]
        if any(n in PACKED for n in names):
            srcs.append(_pack([stacked[n] for n in PACKED], 1))
        lands = [lax.empty((SCATTER_PEERS, *s.shape[1:]), s.dtype) for s in srcs]
        send_sems, recv_sems, srcs, lands, token = _exchange_start(
            srcs, lands, _scatter_plan, SCATTER_PEERS * len(srcs), name=f"scatter_start_{tag}")
        return (names, send_sems, recv_sems, srcs, lands, tag), token

    def scatter_finish(state, after, l):
        names, send_sems, recv_sems, srcs, lands, tag = state
        srcs, got = _exchange_wait(send_sems, recv_sems, srcs, lands, _scatter_plan, after, name=f"scatter_wait_{tag}")
        sums = [_scatter_sum(s, g, me, name="scatter_sum") for s, g in zip(srcs, got)]
        direct = [n for n in names if n not in PACKED]
        for n, g in zip(direct, sums):
            big_grads[n][l] = g
        if len(sums) > len(direct):
            for n, g in zip(PACKED, _unpack(sums[-1], packed_shapes, 0)):
                big_grads[n][l] = g

    dep = None
    for l in reversed(range(L)):
        p = {n: w[n][l] for n in SMALL}
        s1, s2, s3 = saved[l]
        small_g = {}
        dx, small_g["norm_ffn2"], dup_t, dwd = _ffn_bwd(dx, p["norm_ffn2"], full[l]["ffn2"], s3, "ffn2", dep=dep)
        if pending is not None:
            scatter_finish(pending[0], dx, pending[1])
        stacked = {"ffn2_up": _split_rows(dup_t), "ffn2_down": _split_rows(dwd)}
        state, dep = scatter_start(("ffn2_up", "ffn2_down"), stacked, f"ffn2_{l}")
        pending = (state, l)

        dx, gm = _mixer_bwd(dx, p, full[l], tabs, s2, S, dep=dep)
        scatter_finish(pending[0], dx, pending[1])
        names = ("w_in", "w_attn_proj", "w_out") + PACKED
        state, dep = scatter_start(names, _mixer_grads_stacked(gm), f"mix_{l}")
        pending = (state, l)
        small_g.update({n: gm[n] for n in SMALL if n in gm})

        dx, small_g["norm_ffn1"], dup_t, dwd = _ffn_bwd(dx, p["norm_ffn1"], full[l]["ffn1"], s1, "ffn1", dep=dep)
        scatter_finish(pending[0], dx, pending[1])
        stacked = {"ffn1_up": _split_rows(dup_t), "ffn1_down": _split_rows(dwd)}
        state, dep = scatter_start(("ffn1_up", "ffn1_down"), stacked, f"ffn1_{l}")
        pending = (state, l)
        small_grads_of[l] = small_g
    grad_x = dx.reshape(B, S, D)

    small_parts = [small_grads_of[l][n] for l in range(L) for n in SMALL] + [dfinal, loss[0, :1]]
    small_shapes = [p.shape for p in small_parts]
    vec = _pack([jnp.concatenate([p.reshape(-1) for p in small_parts])], 0)
    small_send, small_recv, vec_thru, small_land, small_token = _exchange_start(
        [vec], [lax.empty((N_DEV, *vec.shape), F32)], _gather_all_plan, SCATTER_PEERS, name="small_start")

    gk, grad, delta, new_m, new_v = {}, {}, {}, {}, {}

    def update(n, dep=None):
        wn, gn, mn, vn = (a.reshape(1, -1) if a.ndim == 1 else a for a in (wk[n], gk[n], mk[n], vk[n]))
        d, nm, nv = _adamw(wn, gn, mn, vn, name="adamw_" + n, dep=dep)
        grad[n] = turned(gk[n], n)
        delta[n], new_m[n], new_v[n] = (turned(a.reshape(wk[n].shape), n) for a in (d, nm, nv))

    last_block = ("ffn1_up", "ffn1_down")
    deps = [dep, small_token]
    for n in BIG:
        if n not in last_block:
            gk[n] = jnp.stack(big_grads[n])
            update(n, deps.pop(0) if deps else None)
    scatter_finish(pending[0], new_v["ffn2_down"], pending[1])
    for n in last_block:
        gk[n] = jnp.stack(big_grads[n])
        update(n)

    vec_thru, small_land = _exchange_wait(small_send, small_recv, vec_thru, small_land, _gather_all_plan,
                                          new_v["ffn1_down"], name="small_wait")
    parts = lax.dynamic_update_index_in_dim(small_land[0], vec_thru[0], my_slot, 0)
    flat = _sum_devices(parts, name="sum_small").reshape(-1)
    small_grads, at = [], 0
    for shape in small_shapes:
        size = int(np.prod(shape))
        small_grads.append(lax.slice_in_dim(flat, at, at + size).reshape(shape))
        at += size
    loss_total = small_grads[-1].reshape(())
    for i, n in enumerate(SMALL):
        gk[n] = jnp.stack([small_grads[l * len(SMALL) + i] for l in range(L)]).reshape(w[n].shape)
        update(n)
    gk["final_norm"] = small_grads[-2].reshape(final_norm.shape)
    update("final_norm")
    return (loss_total, grad_x, *[grad[n] for n in order], *[delta[n] for n in order],
            *[new_m[n] for n in order], *[new_v[n] for n in order])
```

```python
import functools

import numpy as np
import jax
import jax.numpy as jnp
from jax import lax
from jax.experimental import pallas as pl
from jax.experimental.pallas import tpu as pltpu

F32 = jnp.float32
BF16 = jnp.bfloat16

NORM_EPS = 1e-6
ROPE_THETA = 10000.0
QK_NOPE = 128
QK_ROPE = 64
V_DIM = 128
HEAD_W = 256
POOL_WINDOWS = (2, 4, 8, 16)
POOL_G = 128
POOL_DIM = 512
LANE = 128
ATTN_SCALE = float((QK_NOPE + QK_ROPE) ** -0.5)
ATTN_SCALE_LOG2 = ATTN_SCALE * float(np.log2(np.e))
MASK_VALUE = -1e30
ATTN_TILE = 512

ADAM_LR = 0.001
ADAM_B1 = 0.9
ADAM_B2 = 0.999
ADAM_EPS = 1e-08
ADAM_WD = 0.01
ADAM_STEP = 10

N_DEV = 8
VMEM_LIMIT = 52 * 1024 * 1024

MESH = pl.DeviceIdType.MESH
ANY = pl.BlockSpec(memory_space=pl.ANY)


def _tile(dim, target, align=LANE):
    if dim <= target:
        return dim
    t = (target // align) * align
    while t >= align:
        if dim % t == 0:
            return t
        t -= align
    return dim


def _params(*sem):
    return pltpu.CompilerParams(dimension_semantics=sem, vmem_limit_bytes=VMEM_LIMIT)


def _rstd(x):
    return lax.rsqrt(jnp.mean(x * x, axis=-1, keepdims=True) + NORM_EPS)


def _mm(a, b, *, name, ta=False, tb=False, out_dtype=F32, alpha=1.0, tm=512, tn=1024, tk=1024):
    if ta:
        K, M = a.shape
    else:
        M, K = a.shape
    if tb:
        N, K2 = b.shape
    else:
        K2, N = b.shape
    assert K == K2, (a.shape, b.shape, ta, tb)
    tm, tn, tk = _tile(M, tm), _tile(N, tn), _tile(K, tk)
    nk = K // tk
    dims = (((0 if ta else 1,), (1 if tb else 0,)), ((), ()))

    def body(a_ref, b_ref, o_ref, *scratch):
        acc_ref = scratch[0] if nk > 1 else None
        part = lax.dot_general(a_ref[...].astype(BF16), b_ref[...].astype(BF16), dims,
                               preferred_element_type=F32)

        def finish(acc):
            o_ref[...] = (acc * alpha if alpha != 1.0 else acc).astype(out_dtype)

        if nk == 1:
            finish(part)
        else:
            k = pl.program_id(2)

            @pl.when(k == 0)
            def _():
                acc_ref[...] = part

            @pl.when(k > 0)
            def _():
                acc_ref[...] += part

            @pl.when(k == nk - 1)
            def _():
                finish(acc_ref[...])

    a_spec = pl.BlockSpec((tk, tm), lambda i, j, k: (k, i)) if ta else pl.BlockSpec((tm, tk), lambda i, j, k: (i, k))
    b_spec = pl.BlockSpec((tn, tk), lambda i, j, k: (j, k)) if tb else pl.BlockSpec((tk, tn), lambda i, j, k: (k, j))
    return pl.pallas_call(
        body, name=name,
        out_shape=jax.ShapeDtypeStruct((M, N), out_dtype),
        grid=(M // tm, N // tn, nk),
        in_specs=[a_spec, b_spec],
        out_specs=pl.BlockSpec((tm, tn), lambda i, j, k: (i, j)),
        scratch_shapes=[pltpu.VMEM((tm, tn), F32)] if nk > 1 else [],
        compiler_params=_params("parallel", "parallel", "arbitrary"),
    )(a, b)


def _rms_fwd(x, g, *, name):
    T, D = x.shape
    tm = _tile(T, 512, 16)

    def body(x_ref, g_ref, h_ref):
        x = x_ref[...]
        h_ref[...] = (x * _rstd(x) * g_ref[...]).astype(BF16)

    return pl.pallas_call(
        body, name=name,
        out_shape=jax.ShapeDtypeStruct((T, D), BF16),
        grid=(T // tm,),
        in_specs=[pl.BlockSpec((tm, D), lambda i: (i, 0)), pl.BlockSpec((1, D), lambda i: (0, 0))],
        out_specs=pl.BlockSpec((tm, D), lambda i: (i, 0)),
        compiler_params=_params("parallel"),
    )(x, g.reshape(1, D))


def _dh_norm_bwd(a1, b1, a2, b2, b2_row0, x, g, dxo, *, name):
    T, D = x.shape
    K1, K2 = a1.shape[1], a2.shape[1]
    assert b2_row0 % K2 == 0 and b1.shape[0] >= K1 and b2.shape[0] >= b2_row0 + K2
    tm = _tile(T, 256, 16)

    def body(a1_ref, b1_ref, a2_ref, b2_ref, x_ref, g_ref, dxo_ref, dx_ref, dg_ref):
        x = x_ref[...]
        r = _rstd(x)
        xhat = x * r
        dh = (jnp.dot(a1_ref[...], b1_ref[...], preferred_element_type=F32)
              + jnp.dot(a2_ref[...], b2_ref[...], preferred_element_type=F32))
        dxh = dh * g_ref[...]
        dx_ref[...] = dxo_ref[...] + r * (dxh - xhat * jnp.mean(dxh * xhat, axis=-1, keepdims=True))
        part = jnp.sum(dh * xhat, axis=0, keepdims=True)

        @pl.when(pl.program_id(0) == 0)
        def _():
            dg_ref[...] = part

        @pl.when(pl.program_id(0) > 0)
        def _():
            dg_ref[...] += part

    row = pl.BlockSpec((tm, D), lambda i: (i, 0))
    vec = pl.BlockSpec((1, D), lambda i: (0, 0))
    return pl.pallas_call(
        body, name=name,
        out_shape=(jax.ShapeDtypeStruct((T, D), F32), jax.ShapeDtypeStruct((1, D), F32)),
        grid=(T // tm,),
        in_specs=[pl.BlockSpec((tm, K1), lambda i: (i, 0)), pl.BlockSpec((K1, D), lambda i: (0, 0)),
                  pl.BlockSpec((tm, K2), lambda i: (i, 0)), pl.BlockSpec((K2, D), lambda i: (b2_row0 // K2, 0)),
                  row, vec, row],
        out_specs=(row, vec),
        compiler_params=_params("arbitrary"),
    )(a1, b1, a2, b2, x, g.reshape(1, D), dxo)


def _loss_head(x, g, target, *, name):
    T, D = x.shape
    tm = _tile(T, 512, 16)

    def body(x_ref, g_ref, t_ref, dx_ref, dg_ref, loss_ref):
        x = x_ref[...]
        gain = g_ref[...]
        r = _rstd(x)
        xhat = x * r
        err = xhat * gain - t_ref[...]
        dy = err * (1.0 / D)
        dxh = dy * gain
        dx_ref[...] = r * (dxh - xhat * jnp.mean(dxh * xhat, axis=-1, keepdims=True))
        dg_part = jnp.sum(dy * xhat, axis=0, keepdims=True)
        loss_part = jnp.full((1, LANE), 0.5 / D, F32) * jnp.sum(err * err)

        @pl.when(pl.program_id(0) == 0)
        def _():
            dg_ref[...] = dg_part
            loss_ref[...] = loss_part

        @pl.when(pl.program_id(0) > 0)
        def _():
            dg_ref[...] += dg_part
            loss_ref[...] += loss_part

    row = pl.BlockSpec((tm, D), lambda i: (i, 0))
    vec = pl.BlockSpec((1, D), lambda i: (0, 0))
    return pl.pallas_call(
        body, name=name,
        out_shape=(jax.ShapeDtypeStruct((T, D), F32), jax.ShapeDtypeStruct((1, D), F32),
                   jax.ShapeDtypeStruct((1, LANE), F32)),
        grid=(T // tm,),
        in_specs=[row, vec, row],
        out_specs=(row, vec, pl.BlockSpec((1, LANE), lambda i: (0, 0))),
        compiler_params=_params("arbitrary"),
    )(x, g.reshape(1, D), target)


def _ffn_fwd_core(x, h, w_up_t, wd, next_gain, *, alpha, name, dep=None):
    T, D = x.shape
    F = wd.shape[0]
    tm = _tile(T, 256, 16)
    has_norm = next_gain is not None

    def body(x_ref, h_ref, wg_ref, wu_ref, wd_ref, *rest):
        outs = rest[len(rest) - (5 if has_norm else 4):]
        gate_ref, up_ref, a_ref, xn_ref = outs[:4]
        h = h_ref[...]
        gate = lax.dot_general(h, wg_ref[...], _NT, preferred_element_type=F32)
        up = lax.dot_general(h, wu_ref[...], _NT, preferred_element_type=F32)
        a = (gate * jax.nn.sigmoid(gate) * up).astype(BF16)
        gate_ref[...] = gate.astype(BF16)
        up_ref[...] = up.astype(BF16)
        a_ref[...] = a
        xn = x_ref[...] + alpha * jnp.dot(a, wd_ref[...], preferred_element_type=F32)
        xn_ref[...] = xn
        if has_norm:
            outs[4][...] = (xn * _rstd(xn) * rest[0][...]).astype(BF16)

    once = pl.Buffered(1)
    row_d = pl.BlockSpec((tm, D), lambda i: (i, 0))
    row_f = pl.BlockSpec((tm, F), lambda i: (i, 0))
    vec = pl.BlockSpec((1, D), lambda i: (0, 0))
    act = jax.ShapeDtypeStruct((T, F), BF16)
    operands = [x, h, w_up_t, w_up_t, wd] + ([next_gain.reshape(1, D)] if has_norm else [])
    out = pl.pallas_call(
        body, name=name,
        out_shape=(act, act, act, jax.ShapeDtypeStruct((T, D), F32)) + ((jax.ShapeDtypeStruct((T, D), BF16),) if has_norm else ()),
        grid=(T // tm,),
        in_specs=[row_d, row_d,
                  pl.BlockSpec((F, D), lambda i: (0, 0), pipeline_mode=once),
                  pl.BlockSpec((F, D), lambda i: (1, 0), pipeline_mode=once),
                  pl.BlockSpec((F, D), lambda i: (0, 0), pipeline_mode=once)] + ([vec] if has_norm else []) + _dep_spec(dep),
        out_specs=(row_f, row_f, row_f, row_d) + ((row_d,) if has_norm else ()),
        compiler_params=_params("parallel"),
    )(*operands, *([] if dep is None else [dep]))
    return out if has_norm else (*out, None)


def _ffn_bwd_core(dxo, wd, w_up_t, gate, up, x, gain, *, alpha, name, dep=None):
    T, D = dxo.shape
    F = wd.shape[0]
    tm = _tile(T, 256, 16)

    def body(dxo_ref, wd_ref, wg_ref, wu_ref, gate_ref, up_ref, x_ref, g_ref, *rest):
        dgate_ref, dup_ref, dx_ref, dg_ref = rest[-4:]
        dxo = dxo_ref[...]
        da = lax.dot_general(dxo.astype(BF16), wd_ref[...], _NT, preferred_element_type=F32) * alpha
        gate = gate_ref[...].astype(F32)
        up = up_ref[...].astype(F32)
        sig = jax.nn.sigmoid(gate)
        dgate = (da * up * (sig * (1.0 + gate * (1.0 - sig)))).astype(BF16)
        dup = (da * (gate * sig)).astype(BF16)
        dgate_ref[...] = dgate
        dup_ref[...] = dup
        dh = (jnp.dot(dgate, wg_ref[...], preferred_element_type=F32)
              + jnp.dot(dup, wu_ref[...], preferred_element_type=F32))
        x = x_ref[...]
        r = _rstd(x)
        xhat = x * r
        dxh = dh * g_ref[...]
        dx_ref[...] = dxo + r * (dxh - xhat * jnp.mean(dxh * xhat, axis=-1, keepdims=True))
        part = jnp.sum(dh * xhat, axis=0, keepdims=True)

        @pl.when(pl.program_id(0) == 0)
        def _():
            dg_ref[...] = part

        @pl.when(pl.program_id(0) > 0)
        def _():
            dg_ref[...] += part

    once = pl.Buffered(1)
    row_d = pl.BlockSpec((tm, D), lambda i: (i, 0))
    row_f = pl.BlockSpec((tm, F), lambda i: (i, 0))
    vec = pl.BlockSpec((1, D), lambda i: (0, 0))
    act = jax.ShapeDtypeStruct((T, F), BF16)
    return pl.pallas_call(
        body, name=name,
        out_shape=(act, act, jax.ShapeDtypeStruct((T, D), F32), jax.ShapeDtypeStruct((1, D), F32)),
        grid=(T // tm,),
        in_specs=[row_d,
                  pl.BlockSpec((F, D), lambda i: (0, 0), pipeline_mode=once),
                  pl.BlockSpec((F, D), lambda i: (0, 0), pipeline_mode=once),
                  pl.BlockSpec((F, D), lambda i: (1, 0), pipeline_mode=once),
                  row_f, row_f, row_d, vec] + _dep_spec(dep),
        out_specs=(row_f, row_f, row_d, vec),
        compiler_params=_params("arbitrary"),
    )(dxo, wd, w_up_t, w_up_t, gate, up, x, gain.reshape(1, D), *([] if dep is None else [dep]))


def _ffn_dw_up(dgate, dup, h, *, name):
    T, F = dgate.shape
    D = h.shape[1]
    tm, tk = _tile(F, 1408), _tile(T, 1024, 16)
    nf, nk = F // tm, T // tk

    def body(dgate_ref, dup_ref, h_ref, o_ref, acc_ref):
        i, k = pl.program_id(0), pl.program_id(1)

        def accumulate(part):
            @pl.when(k == 0)
            def _():
                acc_ref[...] = part

            @pl.when(k > 0)
            def _():
                acc_ref[...] += part

        @pl.when(i < nf)
        def _():
            accumulate(lax.dot_general(dgate_ref[...], h_ref[...], _TN, preferred_element_type=F32))

        @pl.when(i >= nf)
        def _():
            accumulate(lax.dot_general(dup_ref[...], h_ref[...], _TN, preferred_element_type=F32))

        @pl.when(k == nk - 1)
        def _():
            o_ref[...] = acc_ref[...].astype(BF16)

    return pl.pallas_call(
        body, name=name,
        out_shape=jax.ShapeDtypeStruct((2 * F, D), BF16),
        grid=(2 * nf, nk),
        in_specs=[pl.BlockSpec((tk, tm), lambda i, k: (jnp.where(i < nf, k, nk - 1), jnp.minimum(i, nf - 1))),
                  pl.BlockSpec((tk, tm), lambda i, k: (jnp.where(i < nf, 0, k), jnp.maximum(i - nf, 0))),
                  pl.BlockSpec((tk, D), lambda i, k: (k, 0))],
        out_specs=pl.BlockSpec((tm, D), lambda i, k: (i, 0)),
        scratch_shapes=[pltpu.VMEM((tm, D), F32)],
        compiler_params=_params("parallel", "arbitrary"),
    )(dgate, dup, h)


def _dep_spec(dep):
    return [] if dep is None else [pl.BlockSpec(dep.shape, lambda *_: (0,) * dep.ndim)]


def _rope_tables(positions):
    half = QK_ROPE // 2
    inv_freq = ROPE_THETA ** (-jnp.arange(0, QK_ROPE, 2, dtype=F32) / QK_ROPE)
    ang = positions.astype(F32)[:, None] * inv_freq
    cos, sin = jnp.cos(ang), jnp.sin(ang)
    z = jnp.zeros_like(cos)
    zz = jnp.zeros((positions.shape[0], LANE - QK_ROPE), F32)
    c = jnp.concatenate([cos, cos, zz], axis=1)
    sa = jnp.concatenate([z, sin, zz], axis=1)
    sb = jnp.concatenate([-sin, z, zz], axis=1)
    return c, sa, sb


def _rotate(seg, c, sa, sb, sign):
    half = QK_ROPE // 2
    mix = pltpu.roll(seg, half, 1) * sa + pltpu.roll(seg, LANE - half, 1) * sb
    return seg * c + mix if sign > 0 else seg * c - mix


def _mixer_in(h, wa, wuq, wukv, gq, gkv, tabs, *, name, dep=None):
    T, D = h.shape
    HQ, QL = wuq.shape
    KVL = wukv.shape[0]
    H = HQ // HEAD_W
    o_q, o_kv, o_kr = POOL_DIM, POOL_DIM + QL, POOL_DIM + QL + KVL
    PA = o_kr + LANE
    assert wa.shape[0] >= PA
    tm = _tile(T, 512, 16)

    def body(h_ref, wa_ref, wuq_ref, wukv_ref, gq_ref, gkv_ref, c_ref, sa_ref, sb_ref, *rest):
        xp_ref, ql_ref, kvl_ref, qn_ref, kvn_ref, q_ref, kv_ref, kr_ref = rest[-8:]
        proj = lax.dot_general(h_ref[...], wa_ref[...], _NT, preferred_element_type=F32)
        xp_ref[...] = proj[:, :POOL_DIM]
        ql = proj[:, o_q:o_kv]
        kvl = proj[:, o_kv:o_kr]
        ql_ref[...] = ql
        kvl_ref[...] = kvl
        qn = (ql * _rstd(ql) * gq_ref[...]).astype(BF16)
        kvn = (kvl * _rstd(kvl) * gkv_ref[...]).astype(BF16)
        qn_ref[...] = qn
        kvn_ref[...] = kvn
        c, sa, sb = c_ref[...], sa_ref[...], sb_ref[...]
        q = lax.dot_general(qn, wuq_ref[...], _NT, preferred_element_type=F32)
        for hh in range(H):
            base = hh * HEAD_W
            q_ref[:, base:base + QK_NOPE] = q[:, base:base + QK_NOPE].astype(BF16)
            q_ref[:, base + QK_NOPE:base + HEAD_W] = _rotate(
                q[:, base + QK_NOPE:base + HEAD_W], c, sa, sb, 1).astype(BF16)
        kv_ref[...] = jnp.dot(kvn, wukv_ref[...], preferred_element_type=F32).astype(BF16)
        kr_ref[...] = _rotate(proj[:, o_kr:o_kr + LANE], c, sa, sb, 1).astype(BF16)

    def row(w):
        return pl.BlockSpec((tm, w), lambda i: (i, 0))

    def whole(arr):
        return pl.BlockSpec(arr.shape, lambda i: (0,) * arr.ndim)

    gq2, gkv2 = gq.reshape(1, QL), gkv.reshape(1, KVL)
    outs = [(POOL_DIM, F32), (QL, F32), (KVL, F32), (QL, BF16), (KVL, BF16), (HQ, BF16), (HQ, BF16), (LANE, BF16)]
    return pl.pallas_call(
        body, name=name,
        out_shape=tuple(jax.ShapeDtypeStruct((T, w), dt) for w, dt in outs),
        grid=(T // tm,),
        in_specs=[row(D), pl.BlockSpec((PA, D), lambda i: (0, 0)), whole(wuq), whole(wukv), whole(gq2), whole(gkv2),
                  row(LANE), row(LANE), row(LANE)] + _dep_spec(dep),
        out_specs=tuple(row(w) for w, _ in outs),
        compiler_params=_params("parallel"),
    )(h, wa, wuq, wukv, gq2, gkv2, *tabs, *([] if dep is None else [dep]))


def _mixer_in_bwd(dq, dkv, dkr, ql, kvl, dxp, wuq, wukv, gq, gkv, tabs, *, name):
    T, HQ = dq.shape
    QL, KVL = wuq.shape[1], wukv.shape[0]
    H = HQ // HEAD_W
    PA = POOL_DIM + QL + KVL + LANE
    o_q, o_kv, o_kr = POOL_DIM, POOL_DIM + QL, POOL_DIM + QL + KVL
    tm = _tile(T, 512, 16)

    def norm_bwd(lat, gain, dn):
        r = _rstd(lat)
        xhat = lat * r
        dxh = dn * gain
        dlat = r * (dxh - xhat * jnp.mean(dxh * xhat, axis=-1, keepdims=True))
        return dlat, jnp.sum(dn * xhat, axis=0, keepdims=True)

    def body(dq_ref, dkv_ref, dkr_ref, ql_ref, kvl_ref, dxp_ref, wuq_ref, wukv_ref, gq_ref, gkv_ref,
             c_ref, sa_ref, sb_ref, dproj_ref, dqp_ref, dgq_ref, dgkv_ref):
        c, sa, sb = c_ref[...], sa_ref[...], sb_ref[...]
        for hh in range(H):
            base = hh * HEAD_W
            dqp_ref[:, base:base + QK_NOPE] = dq_ref[:, base:base + QK_NOPE]
            dqp_ref[:, base + QK_NOPE:base + HEAD_W] = _rotate(
                dq_ref[:, base + QK_NOPE:base + HEAD_W].astype(F32), c, sa, sb, -1).astype(BF16)
        dqn = jnp.dot(dqp_ref[...], wuq_ref[...], preferred_element_type=F32)
        dkvn = lax.dot_general(dkv_ref[...], wukv_ref[...], _NT, preferred_element_type=F32)
        dql, dgq = norm_bwd(ql_ref[...], gq_ref[...], dqn)
        dkvl, dgkv = norm_bwd(kvl_ref[...], gkv_ref[...], dkvn)
        dproj_ref[:, :POOL_DIM] = dxp_ref[...].astype(BF16)
        dproj_ref[:, o_q:o_kv] = dql.astype(BF16)
        dproj_ref[:, o_kv:o_kr] = dkvl.astype(BF16)
        dproj_ref[:, o_kr:PA] = _rotate(dkr_ref[...], c, sa, sb, -1).astype(BF16)

        @pl.when(pl.program_id(0) == 0)
        def _():
            dgq_ref[...] = dgq
            dgkv_ref[...] = dgkv

        @pl.when(pl.program_id(0) > 0)
        def _():
            dgq_ref[...] += dgq
            dgkv_ref[...] += dgkv

    def row(w):
        return pl.BlockSpec((tm, w), lambda i: (i, 0))

    def whole(arr):
        return pl.BlockSpec(arr.shape, lambda i: (0,) * arr.ndim)

    gq2, gkv2 = gq.reshape(1, QL), gkv.reshape(1, KVL)
    return pl.pallas_call(
        body, name=name,
        out_shape=(jax.ShapeDtypeStruct((T, PA), BF16), jax.ShapeDtypeStruct((T, HQ), BF16),
                   jax.ShapeDtypeStruct((1, QL), F32), jax.ShapeDtypeStruct((1, KVL), F32)),
        grid=(T // tm,),
        in_specs=[row(HQ), row(HQ), row(LANE), row(QL), row(KVL), row(POOL_DIM), whole(wuq), whole(wukv),
                  whole(gq2), whole(gkv2), row(LANE), row(LANE), row(LANE)],
        out_specs=(row(PA), row(HQ), whole(gq2), whole(gkv2)),
        compiler_params=_params("arbitrary"),
    )(dq, dkv, dkr, ql, kvl, dxp, wuq, wukv, gq2, gkv2, *tabs)


def _pool_groups(x_of, S):
    row = lax.broadcasted_iota(jnp.int32, (S, POOL_G), 0)
    for g, w in enumerate(POOL_WINDOWS):
        x = x_of(g)
        s = x
        d = 1
        while d < w:
            s = s + jnp.where(row >= d, pltpu.roll(s, d, 0), 0.0)
            d *= 2
        cnt = jnp.minimum(row + 1, w).astype(F32)
        yield g, w, x, s / cnt - x, cnt, row


def _pool_fwd(xp, maps, scale, *, S, name):
    T = xp.shape[0]

    def body(xp_ref, maps_ref, scale_ref, ms_ref):
        for g, _, _, pooled, _, _ in _pool_groups(lambda g: xp_ref[:, g * POOL_G:(g + 1) * POOL_G], S):
            mixed = jnp.dot(pooled.astype(BF16), maps_ref[g].astype(BF16), preferred_element_type=F32)
            ms_ref[:, g * POOL_G:(g + 1) * POOL_G] = (mixed * scale_ref[:, g * POOL_G:(g + 1) * POOL_G]).astype(BF16)

    return pl.pallas_call(
        body, name=name,
        out_shape=jax.ShapeDtypeStruct((T, POOL_DIM), BF16),
        grid=(T // S,),
        in_specs=[pl.BlockSpec((S, POOL_DIM), lambda b: (b, 0)),
                  pl.BlockSpec(maps.shape, lambda b: (0, 0, 0)),
                  pl.BlockSpec((1, POOL_DIM), lambda b: (0, 0))],
        out_specs=pl.BlockSpec((S, POOL_DIM), lambda b: (b, 0)),
        compiler_params=_params("parallel"),
    )(xp, maps, scale.reshape(1, POOL_DIM))


def _pool_bwd(xp, dms, maps, scale, *, S, name):
    T = xp.shape[0]

    def body(xp_ref, dms_ref, maps_ref, scale_ref, dxp_ref, dmaps_ref, dscale_ref):
        first = pl.program_id(0) == 0
        for g, w, _, pooled, cnt, row in _pool_groups(lambda g: xp_ref[:, g * POOL_G:(g + 1) * POOL_G], S):
            cols = slice(g * POOL_G, (g + 1) * POOL_G)
            pooled_b = pooled.astype(BF16)
            maps_b = maps_ref[g].astype(BF16)
            mixed = jnp.dot(pooled_b, maps_b, preferred_element_type=F32)
            dms = dms_ref[:, cols]
            dscale = jnp.sum(dms * mixed, axis=0, keepdims=True)
            dmixed = (dms * scale_ref[:, cols]).astype(BF16)
            dmaps = lax.dot_general(pooled_b, dmixed, (((0,), (0,)), ((), ())), preferred_element_type=F32)
            dpooled = lax.dot_general(dmixed, maps_b, (((1,), (1,)), ((), ())), preferred_element_type=F32)
            z = dpooled / cnt
            d = 1
            while d < w:
                z = z + jnp.where(row < S - d, pltpu.roll(z, S - d, 0), 0.0)
                d *= 2
            dxp_ref[:, cols] = z - dpooled

            @pl.when(first)
            def _():
                dmaps_ref[g] = dmaps
                dscale_ref[:, cols] = dscale

            @pl.when(jnp.logical_not(first))
            def _():
                dmaps_ref[g] += dmaps
                dscale_ref[:, cols] += dscale

    seq = pl.BlockSpec((S, POOL_DIM), lambda b: (b, 0))
    maps_spec = pl.BlockSpec(maps.shape, lambda b: (0, 0, 0))
    vec = pl.BlockSpec((1, POOL_DIM), lambda b: (0, 0))
    return pl.pallas_call(
        body, name=name,
        out_shape=(jax.ShapeDtypeStruct((T, POOL_DIM), F32), jax.ShapeDtypeStruct(maps.shape, F32),
                   jax.ShapeDtypeStruct((1, POOL_DIM), F32)),
        grid=(T // S,),
        in_specs=[seq, seq, maps_spec, vec],
        out_specs=(seq, maps_spec, vec),
        compiler_params=_params("arbitrary"),
    )(xp, dms, maps, scale.reshape(1, POOL_DIM))


def _causal_mask(s, t):
    r = lax.broadcasted_iota(jnp.int32, (t, t), 0)
    c = lax.broadcasted_iota(jnp.int32, (t, t), 1)
    return jnp.where(r >= c, s, MASK_VALUE)


_NT = (((1,), (1,)), ((), ()))
_TN = (((0,), (0,)), ((), ()))


def _attn_fwd(q, kv, kr, *, S, name):
    T, HQ = q.shape
    H = HQ // HEAD_W
    B = T // S
    t = _tile(S, ATTN_TILE)
    n = S // t

    def body(q_ref, k_ref, v_ref, kr_ref, o_ref, lse_ref, kcat):
        kcat[:, :QK_NOPE] = k_ref[...]
        kcat[:, QK_NOPE:] = kr_ref[...]
        for i in range(n):
            rows = slice(i * t, (i + 1) * t)
            qt = q_ref[rows, :]
            m = jnp.full((t, 1), MASK_VALUE, F32)
            l = jnp.zeros((t, 1), F32)
            acc = jnp.zeros((t, V_DIM), F32)
            for j in range(i + 1):
                cols = slice(j * t, (j + 1) * t)
                s = lax.dot_general(qt, kcat[cols, :], _NT, preferred_element_type=F32) * ATTN_SCALE_LOG2
                if j == i:
                    s = _causal_mask(s, t)
                m_new = jnp.maximum(m, jnp.max(s, axis=1, keepdims=True))
                p = jnp.exp2(s - m_new)
                corr = jnp.exp2(m - m_new)
                l = corr * l + jnp.sum(p, axis=1, keepdims=True)
                acc = corr * acc + jnp.dot(p.astype(BF16), v_ref[cols, :], preferred_element_type=F32)
                m = m_new
            o_ref[rows, :] = (acc / l).astype(BF16)
            lse_ref[rows, :] = jnp.broadcast_to(m + jnp.log2(l), (t, LANE))

    seq_h = pl.BlockSpec((S, LANE), lambda b, h: (b, h))
    return pl.pallas_call(
        body, name=name,
        out_shape=(jax.ShapeDtypeStruct((T, H * V_DIM), BF16), jax.ShapeDtypeStruct((T, H * LANE), F32)),
        grid=(B, H),
        in_specs=[pl.BlockSpec((S, HEAD_W), lambda b, h: (b, h)),
                  pl.BlockSpec((S, QK_NOPE), lambda b, h: (b, 2 * h)),
                  pl.BlockSpec((S, V_DIM), lambda b, h: (b, 2 * h + 1)),
                  pl.BlockSpec((S, LANE), lambda b, h: (b, 0))],
        out_specs=(seq_h, seq_h),
        scratch_shapes=[pltpu.VMEM((S, HEAD_W), BF16)],
        compiler_params=_params("parallel", "parallel"),
    )(q, kv, kv, kr)


def _attn_bwd(q, kv, kr, o, do, lse, *, S, name):
    T, HQ = q.shape
    H = HQ // HEAD_W
    B = T // S
    t = _tile(S, ATTN_TILE)
    n = S // t

    def body(q_ref, k_ref, v_ref, kr_ref, o_ref, do_ref, lse_ref, dq_ref, dkv_ref, dkr_ref, kcat, dq_acc):
        @pl.when(pl.program_id(1) == 0)
        def _():
            dkr_ref[...] = jnp.zeros_like(dkr_ref)

        kcat[:, :QK_NOPE] = k_ref[...]
        kcat[:, QK_NOPE:] = kr_ref[...]
        delta = [jnp.sum(do_ref[i * t:(i + 1) * t, :].astype(F32) * o_ref[i * t:(i + 1) * t, :].astype(F32),
                         axis=1, keepdims=True) for i in range(n)]
        for j in range(n):
            cols = slice(j * t, (j + 1) * t)
            kc = kcat[cols, :]
            vt = v_ref[cols, :]
            dk = jnp.zeros((t, HEAD_W), F32)
            dv = jnp.zeros((t, V_DIM), F32)
            for i in range(j, n):
                rows = slice(i * t, (i + 1) * t)
                qt = q_ref[rows, :]
                dot_ = do_ref[rows, :]
                s = lax.dot_general(qt, kc, _NT, preferred_element_type=F32) * ATTN_SCALE_LOG2
                if i == j:
                    s = _causal_mask(s, t)
                p = jnp.exp2(s - lse_ref[rows, :][:, :1])
                dv = dv + lax.dot_general(p.astype(BF16), dot_, _TN, preferred_element_type=F32)
                dp = lax.dot_general(dot_, vt, _NT, preferred_element_type=F32)
                ds = (p * (dp - delta[i]) * ATTN_SCALE).astype(BF16)
                dk = dk + lax.dot_general(ds, qt, _TN, preferred_element_type=F32)
                dq_part = jnp.dot(ds, kc, preferred_element_type=F32)
                if j == 0:
                    dq_acc[rows, :] = dq_part
                else:
                    dq_acc[rows, :] += dq_part
            dkv_ref[cols, :QK_NOPE] = dk[:, :QK_NOPE].astype(BF16)
            dkv_ref[cols, QK_NOPE:] = dv.astype(BF16)
            dkr_ref[cols, :] += dk[:, QK_NOPE:]
        dq_ref[...] = dq_acc[...].astype(BF16)

    seq_q = pl.BlockSpec((S, HEAD_W), lambda b, h: (b, h))
    seq_h = pl.BlockSpec((S, LANE), lambda b, h: (b, h))
    seq_shared = pl.BlockSpec((S, LANE), lambda b, h: (b, 0))
    return pl.pallas_call(
        body, name=name,
        out_shape=(jax.ShapeDtypeStruct((T, HQ), BF16), jax.ShapeDtypeStruct((T, HQ), BF16),
                   jax.ShapeDtypeStruct((T, LANE), F32)),
        grid=(B, H),
        in_specs=[seq_q,
                  pl.BlockSpec((S, QK_NOPE), lambda b, h: (b, 2 * h)),
                  pl.BlockSpec((S, V_DIM), lambda b, h: (b, 2 * h + 1)),
                  seq_shared, seq_h, seq_h, seq_h],
        out_specs=(seq_q, seq_q, seq_shared),
        scratch_shapes=[pltpu.VMEM((S, HEAD_W), BF16), pltpu.VMEM((S, HEAD_W), F32)],
        compiler_params=_params("parallel", "arbitrary"),
    )(q, kv, kv, kr, o, do, lse)


def _merge_out(h, ms, o, x, wgate, bgate, wpp, wap, wout, next_gain, *, name):
    T, D = x.shape
    tm = _tile(T, 256, 16)

    def body(h_ref, ms_ref, o_ref, x_ref, wgate_ref, bgate_ref, wpp_ref, wap_ref, wout_ref, ng_ref,
             gates_ref, ba_ref, bb_ref, merged_ref, xn_ref, hn_ref):
        logits = lax.dot_general(h_ref[...], wgate_ref[...], _NT, preferred_element_type=F32) + bgate_ref[...]
        gates = jax.nn.sigmoid(logits)
        ba = jnp.dot(ms_ref[...], wpp_ref[...], preferred_element_type=F32)
        bb = jnp.dot(o_ref[...], wap_ref[...], preferred_element_type=F32)
        merged = (gates[:, :D] * ba + gates[:, D:] * bb).astype(BF16)
        gates_ref[...] = gates.astype(BF16)
        ba_ref[...] = ba.astype(BF16)
        bb_ref[...] = bb.astype(BF16)
        merged_ref[...] = merged
        xn = x_ref[...] + jnp.dot(merged, wout_ref[...], preferred_element_type=F32)
        xn_ref[...] = xn
        hn_ref[...] = (xn * _rstd(xn) * ng_ref[...]).astype(BF16)

    def row(w):
        return pl.BlockSpec((tm, w), lambda i: (i, 0))

    def whole(arr):
        return pl.BlockSpec(arr.shape, lambda i: (0,) * arr.ndim)

    bg2, ng2 = bgate.reshape(1, 2 * D), next_gain.reshape(1, D)
    act = jax.ShapeDtypeStruct((T, D), BF16)
    return pl.pallas_call(
        body, name=name,
        out_shape=(jax.ShapeDtypeStruct((T, 2 * D), BF16), act, act, act, jax.ShapeDtypeStruct((T, D), F32), act),
        grid=(T // tm,),
        in_specs=[row(D), row(ms.shape[1]), row(o.shape[1]), row(D), whole(wgate), whole(bg2), whole(wpp),
                  whole(wap), whole(wout), whole(ng2)],
        out_specs=(row(2 * D), row(D), row(D), row(D), row(D), row(D)),
        compiler_params=_params("parallel"),
    )(h, ms, o, x, wgate, bg2, wpp, wap, wout, ng2)


def _merge_bwd(dxo, wout, gates, ba, bb, *, name, dep=None):
    T, D = dxo.shape
    tm = _tile(T, 512, 16)

    def body(dxo_ref, wout_ref, gates_ref, ba_ref, bb_ref, *rest):
        dba_ref, dbb_ref, dgl_ref, dbg_ref = rest[-4:]
        dm = lax.dot_general(dxo_ref[...].astype(BF16), wout_ref[...], _NT, preferred_element_type=F32)
        ga = gates_ref[:, :D].astype(F32)
        gb = gates_ref[:, D:].astype(F32)
        dba_ref[...] = (dm * ga).astype(BF16)
        dbb_ref[...] = (dm * gb).astype(BF16)
        dgl_a = dm * ba_ref[...].astype(F32) * (ga * (1.0 - ga))
        dgl_b = dm * bb_ref[...].astype(F32) * (gb * (1.0 - gb))
        dgl_ref[:, :D] = dgl_a.astype(BF16)
        dgl_ref[:, D:] = dgl_b.astype(BF16)
        sa = jnp.sum(dgl_a, axis=0, keepdims=True)
        sb = jnp.sum(dgl_b, axis=0, keepdims=True)

        @pl.when(pl.program_id(0) == 0)
        def _():
            dbg_ref[:, :D] = sa
            dbg_ref[:, D:] = sb

        @pl.when(pl.program_id(0) > 0)
        def _():
            dbg_ref[:, :D] += sa
            dbg_ref[:, D:] += sb

    def row(w):
        return pl.BlockSpec((tm, w), lambda i: (i, 0))

    act = jax.ShapeDtypeStruct((T, D), BF16)
    return pl.pallas_call(
        body, name=name,
        out_shape=(act, act, jax.ShapeDtypeStruct((T, 2 * D), BF16), jax.ShapeDtypeStruct((1, 2 * D), F32)),
        grid=(T // tm,),
        in_specs=[row(D), pl.BlockSpec(wout.shape, lambda i: (0, 0)), row(2 * D), row(D), row(D)] + _dep_spec(dep),
        out_specs=(row(D), row(D), row(2 * D), pl.BlockSpec((1, 2 * D), lambda i: (0, 0))),
        compiler_params=_params("arbitrary"),
    )(dxo, wout, gates, ba, bb, *([] if dep is None else [dep]))


def _ffn_fwd(x, h, w, tag, next_gain, dep=None):
    gate, up, a, xn, hn = _ffn_fwd_core(x, h, w["up_t"], w["wd"], next_gain, alpha=0.5,
                                        name=f"{tag}_fwd" if next_gain is not None else f"{tag}_fwd_last", dep=dep)
    return xn, hn, (x, h, gate, up, a)


def _ffn_bwd(dxo, gain, w, saved, tag, dep=None):
    x, h, gate, up, a = saved
    F = gate.shape[1]
    dgate, dup, dx, dgain = _ffn_bwd_core(dxo, w["wd"], w["up_t"], gate, up, x, gain, alpha=0.5,
                                          name=f"{tag}_bwd_core", dep=dep)
    dwd = _mm(a, dxo, ta=True, alpha=0.5, out_dtype=BF16, name=f"{tag}_dwd", tm=1408, tn=1024, tk=1024)
    dup_t = _ffn_dw_up(dgate, dup, h, name=f"{tag}_dw_up")
    return dx, dgain, dup_t, dwd


def _mixer_fwd(x, h, p, w, tabs, S, next_gain, dep=None):
    xp, ql, kvl, qn, kvn, q, kv, kr = _mixer_in(h, w["win_t"], w["wuq_t"], w["wukv"], p["q_latent_norm"],
                                                 p["kv_latent_norm"], tabs, name="mix_in", dep=dep)
    ms = _pool_fwd(xp, p["pool_maps"], p["pool_scale"], S=S, name="pool_fwd")
    o, lse = _attn_fwd(q, kv, kr, S=S, name="attn_fwd")
    gates, ba, bb, merged, xn, hn = _merge_out(h, ms, o, x, w["wgate_t"], p["b_gate"], w["wpp"], w["wap"], w["wout"],
                                               next_gain, name="merge_out")
    return xn, hn, (x, h, xp, ql, kvl, qn, kvn, q, kv, kr, ms, o, lse, gates, ba, bb, merged)


def _mixer_bwd(dxo, p, w, tabs, saved, S, dep=None):
    x, h, xp, ql, kvl, qn, kvn, q, kv, kr, ms, o, lse, gates, ba, bb, merged = saved
    dba, dbb, dgl, dbg = _merge_bwd(dxo, w["wout"], gates, ba, bb, name="merge_bwd", dep=dep)
    g = {}
    g["wout"] = _mm(merged, dxo, ta=True, out_dtype=BF16, name="d_wout", tm=1024, tk=1024)
    g["wpp"] = _mm(ms, dba, ta=True, out_dtype=BF16, name="d_wpp", tk=2048)
    g["wap"] = _mm(o, dbb, ta=True, out_dtype=BF16, name="d_wap", tm=1024, tk=2048)
    dms = _mm(dba, w["wpp"], tb=True, name="d_ms")
    do = _mm(dbb, w["wap"], tb=True, out_dtype=BF16, name="d_o")
    dxp, g["pool_maps"], g["pool_scale"] = _pool_bwd(xp, dms, p["pool_maps"], p["pool_scale"], S=S, name="pool_bwd")
    dq, dkv, dkr = _attn_bwd(q, kv, kr, o, do, lse, S=S, name="attn_bwd")
    dproj, dqp, g["q_latent_norm"], g["kv_latent_norm"] = _mixer_in_bwd(
        dq, dkv, dkr, ql, kvl, dxp, w["wuq_t"], w["wukv"], p["q_latent_norm"], p["kv_latent_norm"], tabs,
        name="mix_in_bwd")
    g["wuq_t"] = _mm(dqp, qn, ta=True, out_dtype=BF16, name="d_wuq", tm=2048, tk=2048)
    g["wukv"] = _mm(kvn, dkv, ta=True, out_dtype=BF16, name="d_wukv", tn=2048, tk=2048)
    g["wa_t"] = _mm(dproj, h, ta=True, out_dtype=BF16, name="d_wa", tm=1280, tn=1024, tk=2048)
    g["wgate_t"] = _mm(dgl, h, ta=True, out_dtype=BF16, name="d_wgate", tm=2048, tn=1024, tk=1024)
    dx, g["norm_mix"] = _dh_norm_bwd(dproj, w["win_t"], dgl, w["wgate_t"], 0, x, p["norm_mix"], dxo,
                                     name="mix_dh_norm_bwd")
    g["b_gate"] = dbg
    return dx, g


BIG = ("ffn1_up", "ffn1_down", "w_in", "w_pool_proj", "w_uq", "w_ukv", "w_attn_proj", "w_out", "ffn2_up", "ffn2_down")
SMALL = ("norm_ffn1", "norm_mix", "b_gate", "pool_maps", "pool_scale", "q_latent_norm", "kv_latent_norm", "norm_ffn2")
PACKED = ("w_pool_proj", "w_uq", "w_ukv")
TRANSPOSED = ("ffn1_up", "ffn2_up", "w_in", "w_uq")
COL_SHARDED = ("w_pool_proj", "w_ukv")
QK_HEAD = QK_NOPE + QK_ROPE


def _rows(stacked):
    n, r, c = stacked.shape
    return stacked.reshape(n * r, c)


def _cols(stacked):
    n, k, c = stacked.shape
    return stacked.transpose(1, 0, 2).reshape(k, n * c)


FFN1_PART = ("ffn1_up", "ffn1_down")
MIXER_PART = ("w_in", "w_attn_proj", "w_out") + PACKED
FFN2_PART = ("ffn2_up", "ffn2_down")


def _kernel_weights(stacked):
    full = {}
    for tag in ("ffn1", "ffn2"):
        if tag + "_up" in stacked:
            full[tag] = {"up_t": _rows(stacked[tag + "_up"]), "wd": _rows(stacked[tag + "_down"])}
    if "w_in" in stacked:
        win_t = _rows(stacked["w_in"])
        D = win_t.shape[1]
        wuq_t = _rows(stacked["w_uq"])
        QL = wuq_t.shape[1]
        H = wuq_t.shape[0] // QK_HEAD
        wuq_t = jnp.pad(wuq_t.reshape(H, QK_HEAD, QL), ((0, 0), (0, HEAD_W - QK_HEAD), (0, 0)))
        full.update({"win_t": win_t, "wgate_t": win_t[win_t.shape[0] - 2 * D:], "wuq_t": wuq_t.reshape(H * HEAD_W, QL),
                     "wukv": _cols(stacked["w_ukv"]), "wpp": _cols(stacked["w_pool_proj"]),
                     "wap": _rows(stacked["w_attn_proj"]), "wout": _rows(stacked["w_out"])})
    return full


def _split_rows(full):
    return full.reshape(N_DEV, full.shape[0] // N_DEV, full.shape[1])


def _split_cols(full):
    k, cols = full.shape
    return full.reshape(k, N_DEV, cols // N_DEV).transpose(1, 0, 2)


def _mixer_grads_stacked(g):
    n_a = g["wa_t"].shape[0] - (LANE - QK_ROPE)
    HQ, QL = g["wuq_t"].shape
    H = HQ // HEAD_W
    wuq_t = g["wuq_t"].reshape(H, HEAD_W, QL)[:, :QK_HEAD].reshape(H * QK_HEAD, QL)
    return {"w_in": _split_rows(jnp.concatenate([g["wa_t"][:n_a], g["wgate_t"]], axis=0)),
            "w_uq": _split_rows(wuq_t),
            "w_pool_proj": _split_cols(g["wpp"]), "w_ukv": _split_cols(g["wukv"]),
            "w_attn_proj": _split_rows(g["wap"]), "w_out": _split_rows(g["wout"])}


def _mesh_place():
    x, y, c = lax.axis_index("x"), lax.axis_index("y"), lax.axis_index("c")
    chips = [(1 - x, y), (x, 1 - y), (1 - x, 1 - y)]
    return x, y, c, chips


HBM = pl.BlockSpec(memory_space=pltpu.HBM)
SEMAPHORES = pl.BlockSpec(memory_space=pltpu.SEMAPHORE)
DATAFLOW = pltpu.SideEffectType.DATAFLOW_SIDE_EFFECTING
GATHER_PEERS = 4
SCATTER_PEERS = 7


def _in_hbm(a):
    return pltpu.with_memory_space_constraint(a, pltpu.HBM)


def _gather_plan(src_refs, land_refs):
    x, y, c, chips = _mesh_place()
    me = 4 * x + 2 * y + c
    targets = [(x, y, 1 - c)] + [(cx, cy, c) for cx, cy in chips]
    return [(s, land.at[me], to) for s, land in zip(src_refs, land_refs) for to in targets]


def _scatter_plan(src_refs, land_refs):
    x, y, c, _ = _mesh_place()
    peers = [(x, y, 1 - c), (1 - x, y, c), (x, 1 - y, c), (1 - x, 1 - y, c),
             (1 - x, y, 1 - c), (x, 1 - y, 1 - c), (1 - x, 1 - y, 1 - c)]
    return [(s.at[4 * px + 2 * py + pc], land.at[k], (px, py, pc))
            for s, land in zip(src_refs, land_refs) for k, (px, py, pc) in enumerate(peers)]


def _descriptors(plan, src_refs, land_refs, send_sems, recv_sems):
    return [pltpu.make_async_remote_copy(src_ref=s, dst_ref=d, send_sem=send_sems.at[k], recv_sem=recv_sems.at[k],
                                         device_id=to, device_id_type=MESH)
            for k, (s, d, to) in enumerate(plan(src_refs, land_refs))]


def _exchange(srcs, land_shapes, plan, per_src, *, name):
    n = len(srcs)

    def body(*refs):
        copies = _descriptors(plan, refs[:n], refs[n:2 * n], refs[2 * n], refs[2 * n + 1])
        for cp in copies:
            cp.start()
        for cp in copies:
            cp.wait()

    return pl.pallas_call(
        body, name=name,
        out_shape=tuple(jax.ShapeDtypeStruct(shape, s.dtype) for shape, s in zip(land_shapes, srcs)),
        in_specs=[ANY] * n, out_specs=(ANY,) * n,
        scratch_shapes=[pltpu.SemaphoreType.DMA((per_src * n,)), pltpu.SemaphoreType.DMA((per_src * n,))],
    )(*srcs)


FORWARD_COPIES = 4


def _forward_slots():
    x, y, c, chips = _mesh_place()
    return [4 * cx + 2 * cy + c for cx, cy in chips] + [4 * x + 2 * y + (1 - c)], (x, y, 1 - c)


def _forward_plan(src_refs, land_refs):
    slots, sibling = _forward_slots()
    return [(land.at[s], land.at[s], sibling) for land in land_refs for s in slots]


def _gather_all_plan(src_refs, land_refs):
    x, y, c, _ = _mesh_place()
    me = 4 * x + 2 * y + c
    peers = [(x, y, 1 - c), (1 - x, y, c), (x, 1 - y, c), (1 - x, 1 - y, c),
             (1 - x, y, 1 - c), (x, 1 - y, 1 - c), (1 - x, 1 - y, 1 - c)]
    return [(s, land.at[me], to) for s, land in zip(src_refs, land_refs) for to in peers]


def _exchange_start(srcs, lands, plan, n_copies, *, name):
    ns, n = len(srcs), len(srcs) + len(lands)

    def body(*refs):
        for cp in _descriptors(plan, refs[:ns], refs[ns:n], refs[n], refs[n + 1]):
            cp.start()
        refs[-1][...] = jnp.zeros_like(refs[-1])

    sems = pltpu.SemaphoreType.DMA((n_copies,))
    out = pl.pallas_call(
        body, name=name,
        out_shape=(sems, sems, *[pltpu.HBM(a.shape, a.dtype) for a in srcs + lands],
                   jax.ShapeDtypeStruct((8, LANE), F32)),
        in_specs=(HBM,) * n,
        out_specs=(SEMAPHORES, SEMAPHORES, *[HBM] * n, pl.BlockSpec(memory_space=pltpu.VMEM)),
        input_output_aliases={i: 2 + i for i in range(n)},
        compiler_params=pltpu.CompilerParams(has_side_effects=DATAFLOW),
    )(*[_in_hbm(a) for a in srcs + lands])
    return out[0], out[1], list(out[2:2 + ns]), list(out[2 + ns:2 + n]), out[-1]


def _exchange_wait(send_sems, recv_sems, srcs, lands, plan, after, *, name):
    ns, n = len(srcs), len(srcs) + len(lands)

    def body(*refs):
        for cp in _descriptors(plan, refs[:ns], refs[ns:n], refs[n], refs[n + 1]):
            cp.wait_send()
            cp.wait_recv()

    out = pl.pallas_call(
        body, name=name,
        out_shape=tuple(pltpu.HBM(a.shape, a.dtype) for a in srcs + lands),
        in_specs=(*[HBM] * n, SEMAPHORES, SEMAPHORES, ANY),
        out_specs=(HBM,) * n,
        input_output_aliases={i: i for i in range(n)},
        compiler_params=pltpu.CompilerParams(has_side_effects=DATAFLOW),
    )(*srcs, *lands, send_sems, recv_sems, after)
    return list(out[:ns]), list(out[ns:])


def _gather_forward(lands, *, name):
    n = len(lands)

    def body(*refs):
        in_refs, out_refs = refs[:n], refs[n:2 * n]
        token, send_sems, recv_sems = refs[2 * n:2 * n + 3]
        slots, sibling = _forward_slots()
        passed = [pltpu.make_async_remote_copy(
            src_ref=i.at[s], dst_ref=o.at[s],
            send_sem=send_sems.at[FORWARD_COPIES * b + j], recv_sem=recv_sems.at[FORWARD_COPIES * b + j],
            device_id=sibling, device_id_type=MESH)
            for b, (i, o) in enumerate(zip(in_refs, out_refs)) for j, s in enumerate(slots)]
        for cp in passed:
            cp.start()
        for cp in passed:
            cp.wait()
        token[...] = jnp.zeros_like(token)

    out = pl.pallas_call(
        body, name=name,
        out_shape=(*[jax.ShapeDtypeStruct(a.shape, a.dtype) for a in lands], jax.ShapeDtypeStruct((8, LANE), F32)),
        in_specs=[ANY] * n,
        out_specs=(*[ANY] * n, pl.BlockSpec(memory_space=pltpu.VMEM)),
        input_output_aliases={i: i for i in range(n)},
        scratch_shapes=[pltpu.SemaphoreType.DMA((FORWARD_COPIES * n,)), pltpu.SemaphoreType.DMA((FORWARD_COPIES * n,))],
    )(*lands)
    return list(out[:n]), out[n]


def _scatter_sum(parts, got, me, *, name):
    shard = parts.shape[1:]
    cols = shard[-1]
    rows = int(np.prod(shard[:-1]))
    tr = _tile(rows, 256, 16)

    def body(me_ref, p_ref, g_ref, o_ref):
        acc = p_ref[...].astype(F32)
        for k in range(SCATTER_PEERS):
            acc = acc + g_ref[k].astype(F32)
        o_ref[...] = acc

    out = pl.pallas_call(
        body, name=name,
        out_shape=jax.ShapeDtypeStruct((rows, cols), F32),
        grid_spec=pltpu.PrefetchScalarGridSpec(
            num_scalar_prefetch=1, grid=(rows // tr,),
            in_specs=[pl.BlockSpec((None, tr, cols), lambda r, me_ref: (me_ref[0], r, 0)),
                      pl.BlockSpec((SCATTER_PEERS, tr, cols), lambda r, me_ref: (0, r, 0))],
            out_specs=pl.BlockSpec((tr, cols), lambda r, me_ref: (r, 0))),
        compiler_params=_params("parallel"),
    )(me, parts.reshape(N_DEV, rows, cols), got.reshape(SCATTER_PEERS, rows, cols))
    return out.reshape(shard)


def _sum_devices(parts, *, name):
    _, R, C = parts.shape
    tr = _tile(R, 512, 8)

    def body(p_ref, o_ref):
        acc = p_ref[0]
        for d in range(1, N_DEV):
            acc = acc + p_ref[d]
        o_ref[...] = acc

    return pl.pallas_call(
        body, name=name,
        out_shape=jax.ShapeDtypeStruct((R, C), F32),
        grid=(R // tr,),
        in_specs=[pl.BlockSpec((N_DEV, tr, C), lambda r: (0, r, 0))],
        out_specs=pl.BlockSpec((tr, C), lambda r: (r, 0)),
        compiler_params=_params("parallel"),
    )(parts)


def _adamw(w, g, m, v, *, name, dep=None):
    shape = w.shape
    cols = shape[-1]
    rows = w.size // cols
    tr = _tile(rows, 256, 8)

    def body(w_ref, g_ref, m_ref, v_ref, *rest):
        d_ref, nm_ref, nv_ref = rest[-3:]
        g = g_ref[...]
        m = ADAM_B1 * m_ref[...] + (1.0 - ADAM_B1) * g
        v = ADAM_B2 * v_ref[...] + (1.0 - ADAM_B2) * jnp.square(g)
        m_hat = m / (1.0 - ADAM_B1 ** ADAM_STEP)
        v_hat = v / (1.0 - ADAM_B2 ** ADAM_STEP)
        d_ref[...] = -ADAM_LR * (m_hat / (jnp.sqrt(v_hat) + ADAM_EPS) + ADAM_WD * w_ref[...])
        nm_ref[...] = m
        nv_ref[...] = v

    spec = pl.BlockSpec((tr, cols), lambda i: (i, 0))
    out = jax.ShapeDtypeStruct((rows, cols), F32)
    d, nm, nv = pl.pallas_call(
        body, name=name,
        out_shape=(out, out, out),
        grid=(rows // tr,),
        in_specs=[spec] * 4 + _dep_spec(dep), out_specs=(spec,) * 3,
        compiler_params=_params("parallel"),
    )(*(a.reshape(rows, cols) for a in (w, g, m, v)), *([] if dep is None else [dep]))
    return d.reshape(shape), nm.reshape(shape), nv.reshape(shape)


PACK_ALIGN = 16 * LANE


def _pack(pieces, lead):
    out = []
    for p in pieces:
        keep = p.shape[:lead]
        flat = p.reshape(*keep, -1)
        pad = (-flat.shape[-1]) % PACK_ALIGN
        if pad:
            flat = jnp.pad(flat, [(0, 0)] * lead + [(0, pad)])
        out.append(flat.reshape(*keep, -1, LANE))
    return jnp.concatenate(out, axis=lead)


def _unpack(buf, shapes, lead):
    keep = buf.shape[:lead]
    out, row = [], 0
    for shape in shapes:
        size = int(np.prod(shape))
        rows = -(-size // PACK_ALIGN) * (PACK_ALIGN // LANE)
        piece = lax.slice_in_dim(buf, row, row + rows, axis=lead).reshape(*keep, rows * LANE)
        out.append(lax.slice_in_dim(piece, 0, size, axis=lead).reshape(*keep, *shape))
        row += rows
    return out


def kernel(x, positions, norm_ffn1, ffn1_up, ffn1_down, norm_mix, w_in, b_gate, pool_maps, pool_scale, w_pool_proj, q_latent_norm, w_uq, kv_latent_norm, w_ukv, w_attn_proj, w_out, norm_ffn2, ffn2_up, ffn2_down, final_norm, loss_target, m_norm_ffn1, m_ffn1_up, m_ffn1_down, m_norm_mix, m_w_in, m_b_gate, m_pool_maps, m_pool_scale, m_w_pool_proj, m_q_latent_norm, m_w_uq, m_kv_latent_norm, m_w_ukv, m_w_attn_proj, m_w_out, m_norm_ffn2, m_ffn2_up, m_ffn2_down, m_final_norm, v_norm_ffn1, v_ffn1_up, v_ffn1_down, v_norm_mix, v_w_in, v_b_gate, v_pool_maps, v_pool_scale, v_w_pool_proj, v_q_latent_norm, v_w_uq, v_kv_latent_norm, v_w_ukv, v_w_attn_proj, v_w_out, v_norm_ffn2, v_ffn2_up, v_ffn2_down, v_final_norm):
    order = ("norm_ffn1", "ffn1_up", "ffn1_down", "norm_mix", "w_in", "b_gate", "pool_maps", "pool_scale",
             "w_pool_proj", "q_latent_norm", "w_uq", "kv_latent_norm", "w_ukv", "w_attn_proj", "w_out",
             "norm_ffn2", "ffn2_up", "ffn2_down", "final_norm")
    w = dict(zip(order, (norm_ffn1, ffn1_up, ffn1_down, norm_mix, w_in, b_gate, pool_maps, pool_scale, w_pool_proj,
                         q_latent_norm, w_uq, kv_latent_norm, w_ukv, w_attn_proj, w_out, norm_ffn2, ffn2_up,
                         ffn2_down, final_norm)))
    m = dict(zip(order, (m_norm_ffn1, m_ffn1_up, m_ffn1_down, m_norm_mix, m_w_in, m_b_gate, m_pool_maps, m_pool_scale,
                         m_w_pool_proj, m_q_latent_norm, m_w_uq, m_kv_latent_norm, m_w_ukv, m_w_attn_proj, m_w_out,
                         m_norm_ffn2, m_ffn2_up, m_ffn2_down, m_final_norm)))
    v = dict(zip(order, (v_norm_ffn1, v_ffn1_up, v_ffn1_down, v_norm_mix, v_w_in, v_b_gate, v_pool_maps, v_pool_scale,
                         v_w_pool_proj, v_q_latent_norm, v_w_uq, v_kv_latent_norm, v_w_ukv, v_w_attn_proj, v_w_out,
                         v_norm_ffn2, v_ffn2_up, v_ffn2_down, v_final_norm)))
    L = norm_ffn1.shape[0]
    B, S, D = x.shape
    T = B * S

    def turned(a, n):
        return a.transpose(0, 2, 1) if n in TRANSPOSED else a

    wk, mk, vk = ({n: turned(d[n], n) for n in order} for d in (w, m, v))
    packed_shapes = [wk[n].shape[1:] for n in PACKED]
    my_slot = 4 * lax.axis_index("x") + 2 * lax.axis_index("y") + lax.axis_index("c")
    me = jnp.stack([my_slot]).astype(jnp.int32)

    def weight_blocks(l, names, token):
        zero = token[0, 0].astype(BF16)
        blocks = [wk[n][l].astype(BF16) + zero for n in names if n not in PACKED]
        if any(n in PACKED for n in names):
            blocks.append(_pack([wk[n][l].astype(BF16) + zero for n in PACKED], 0))
        return blocks

    def kernel_weights(names, lands):
        direct = [n for n in names if n not in PACKED]
        stacked = dict(zip(direct, lands))
        if len(lands) > len(direct):
            stacked.update(zip(PACKED, _unpack(lands[-1], packed_shapes, 1)))
        return _kernel_weights(stacked)

    def gather_start(l, names, token, tag):
        blocks = weight_blocks(l, names, token)
        lands = [lax.empty((N_DEV, *b.shape), b.dtype) for b in blocks]
        send_sems, recv_sems, blocks, lands, token = _exchange_start(
            blocks, lands, _gather_plan, GATHER_PEERS * len(blocks), name=f"gather_start_{tag}")
        return (send_sems, recv_sems, blocks, lands, tag), token

    def gather_wait(state, after):
        send_sems, recv_sems, blocks, lands, tag = state
        return _exchange_wait(send_sems, recv_sems, blocks, lands, _gather_plan, after, name=f"gather_wait_{tag}")[1]

    layer_part = FFN1_PART + MIXER_PART + FFN2_PART
    tabs = _rope_tables(positions.reshape(T))
    xs = x.reshape(T, D)
    h = _rms_fwd(xs, w["norm_ffn1"][0], name="first_norm")
    full, saved = [], []

    p = {n: w[n][0] for n in SMALL}
    blocks = weight_blocks(0, FFN1_PART, jnp.zeros((8, LANE), F32))
    lands = _exchange(blocks, [(N_DEV, *b.shape) for b in blocks], _gather_plan, GATHER_PEERS, name="gather_first")
    lands, token = _gather_forward(lands, name="gather_forward")
    w0 = kernel_weights(FFN1_PART, lands)
    state, token = gather_start(0, MIXER_PART, token, "0_mix")
    xs, h, s1 = _ffn_fwd(xs, h, w0["ffn1"], "ffn1", p["norm_mix"], dep=token)
    lands, token = _gather_forward(gather_wait(state, xs), name="gather_forward")
    w0.update(kernel_weights(MIXER_PART, lands))
    state, token = gather_start(0, FFN2_PART, token, "0_ffn2")
    if L > 1:
        next_state, token = gather_start(1, layer_part, token, "1")
    xs, h, s2 = _mixer_fwd(xs, h, p, w0, tabs, S, p["norm_ffn2"], dep=token)
    lands, token = _gather_forward(gather_wait(state, xs), name="gather_forward")
    w0.update(kernel_weights(FFN2_PART, lands))
    xs, h, s3 = _ffn_fwd(xs, h, w0["ffn2"], "ffn2", w["norm_ffn1"][1] if L > 1 else None, dep=token)
    if L > 1:
        lands, token = _gather_forward(gather_wait(next_state, xs), name="gather_forward")
    full.append(w0)
    saved.append((s1, s2, s3))

    for l in range(1, L):
        full.append(kernel_weights(layer_part, lands))
        more = l + 1 < L
        p = {n: w[n][l] for n in SMALL}
        if more:
            state, token = gather_start(l + 1, layer_part, token, f"{l + 1}")
        xs, h, s1 = _ffn_fwd(xs, h, full[l]["ffn1"], "ffn1", p["norm_mix"], dep=token if more else None)
        xs, h, s2 = _mixer_fwd(xs, h, p, full[l], tabs, S, p["norm_ffn2"])
        if more:
            lands = gather_wait(state, xs)
            send_sems, recv_sems, _, lands, token = _exchange_start(
                [], lands, _forward_plan, FORWARD_COPIES * len(lands), name=f"forward_start_{l + 1}")
        xs, h, s3 = _ffn_fwd(xs, h, full[l]["ffn2"], "ffn2", w["norm_ffn1"][l + 1] if more else None,
                             dep=token if more else None)
        if more:
            _, lands = _exchange_wait(send_sems, recv_sems, [], lands, _forward_plan, xs, name=f"forward_wait_{l + 1}")
        saved.append((s1, s2, s3))
    dx, dfinal, loss = _loss_head(xs, final_norm, loss_target.reshape(T, D), name="loss_head")

    big_grads = {n: [None] * L for n in BIG}
    small_grads_of = [None] * L
    pending = None

    def scatter_start(names, stacked, tag):
        srcs = [stacked[n] for n in names if n not in PACKED]
        if any(n in PACKED for n in names):
            srcs.append(_pack([stacked[n] for n in PACKED], 1))
        lands = [lax.empty((SCATTER_PEERS, *s.shape[1:]), s.dtype) for s in srcs]
        send_sems, recv_sems, srcs, lands, token = _exchange_start(
            srcs, lands, _scatter_plan, SCATTER_PEERS * len(srcs), name=f"scatter_start_{tag}")
        return (names, send_sems, recv_sems, srcs, lands, tag), token

    def scatter_finish(state, after, l):
        names, send_sems, recv_sems, srcs, lands, tag = state
        srcs, got = _exchange_wait(send_sems, recv_sems, srcs, lands, _scatter_plan, after, name=f"scatter_wait_{tag}")
        sums = [_scatter_sum(s, g, me, name="scatter_sum") for s, g in zip(srcs, got)]
        direct = [n for n in names if n not in PACKED]
        for n, g in zip(direct, sums):
            big_grads[n][l] = g
        if len(sums) > len(direct):
            for n, g in zip(PACKED, _unpack(sums[-1], packed_shapes, 0)):
                big_grads[n][l] = g

    dep = None
    for l in reversed(range(L)):
        p = {n: w[n][l] for n in SMALL}
        s1, s2, s3 = saved[l]
        small_g = {}
        dx, small_g["norm_ffn2"], dup_t, dwd = _ffn_bwd(dx, p["norm_ffn2"], full[l]["ffn2"], s3, "ffn2", dep=dep)
        if pending is not None:
            scatter_finish(pending[0], dx, pending[1])
        stacked = {"ffn2_up": _split_rows(dup_t), "ffn2_down": _split_rows(dwd)}
        state, dep = scatter_start(("ffn2_up", "ffn2_down"), stacked, f"ffn2_{l}")
        pending = (state, l)

        dx, gm = _mixer_bwd(dx, p, full[l], tabs, s2, S, dep=dep)
        scatter_finish(pending[0], dx, pending[1])
        names = ("w_in", "w_attn_proj", "w_out") + PACKED
        state, dep = scatter_start(names, _mixer_grads_stacked(gm), f"mix_{l}")
        pending = (state, l)
        small_g.update({n: gm[n] for n in SMALL if n in gm})

        dx, small_g["norm_ffn1"], dup_t, dwd = _ffn_bwd(dx, p["norm_ffn1"], full[l]["ffn1"], s1, "ffn1", dep=dep)
        scatter_finish(pending[0], dx, pending[1])
        stacked = {"ffn1_up": _split_rows(dup_t), "ffn1_down": _split_rows(dwd)}
        state, dep = scatter_start(("ffn1_up", "ffn1_down"), stacked, f"ffn1_{l}")
        pending = (state, l)
        small_grads_of[l] = small_g
    grad_x = dx.reshape(B, S, D)

    small_parts = [small_grads_of[l][n] for l in range(L) for n in SMALL] + [dfinal, loss[0, :1]]
    small_shapes = [p.shape for p in small_parts]
    vec = _pack([jnp.concatenate([p.reshape(-1) for p in small_parts])], 0)
    small_send, small_recv, vec_thru, small_land, small_token = _exchange_start(
        [vec], [lax.empty((N_DEV, *vec.shape), F32)], _gather_all_plan, SCATTER_PEERS, name="small_start")

    gk, grad, delta, new_m, new_v = {}, {}, {}, {}, {}

    def update(n, dep=None):
        wn, gn, mn, vn = (a.reshape(1, -1) if a.ndim == 1 else a for a in (wk[n], gk[n], mk[n], vk[n]))
        d, nm, nv = _adamw(wn, gn, mn, vn, name="adamw_" + n, dep=dep)
        grad[n] = turned(gk[n], n)
        delta[n], new_m[n], new_v[n] = (turned(a.reshape(wk[n].shape), n) for a in (d, nm, nv))

    last_block = ("ffn1_up", "ffn1_down")
    deps = [dep, small_token]
    for n in BIG:
        if n not in last_block:
            gk[n] = jnp.stack(big_grads[n])
            update(n, deps.pop(0) if deps else None)
    scatter_finish(pending[0], new_v["ffn2_down"], pending[1])
    for n in last_block:
        gk[n] = jnp.stack(big_grads[n])
        update(n)

    vec_thru, small_land = _exchange_wait(small_send, small_recv, vec_thru, small_land, _gather_all_plan,
                                          new_v["ffn1_down"], name="small_wait")
    parts = lax.dynamic_update_index_in_dim(small_land[0], vec_thru[0], my_slot, 0)
    flat = _sum_devices(parts, name="sum_small").reshape(-1)
    small_grads, at = [], 0
    for shape in small_shapes:
        size = int(np.prod(shape))
        small_grads.append(lax.slice_in_dim(flat, at, at + size).reshape(shape))
        at += size
    loss_total = small_grads[-1].reshape(())
    for i, n in enumerate(SMALL):
        gk[n] = jnp.stack([small_grads[l * len(SMALL) + i] for l in range(L)]).reshape(w[n].shape)
        update(n)
    gk["final_norm"] = small_grads[-2].reshape(final_norm.shape)
    update("final_norm")
    return (loss_total, grad_x, *[grad[n] for n in order], *[delta[n] for n in order],
            *[new_m[n] for n in order], *[new_v[n] for n in order])
```

```python
import functools

import numpy as np
import jax
import jax.numpy as jnp
from jax import lax
from jax.experimental import pallas as pl
from jax.experimental.pallas import tpu as pltpu

F32 = jnp.float32
BF16 = jnp.bfloat16

NORM_EPS = 1e-6
ROPE_THETA = 10000.0
QK_NOPE = 128
QK_ROPE = 64
V_DIM = 128
HEAD_W = 256
POOL_WINDOWS = (2, 4, 8, 16)
POOL_G = 128
POOL_DIM = 512
LANE = 128
ATTN_SCALE = float((QK_NOPE + QK_ROPE) ** -0.5)
ATTN_SCALE_LOG2 = ATTN_SCALE * float(np.log2(np.e))
MASK_VALUE = -1e30
ATTN_TILE = 512

ADAM_LR = 0.001
ADAM_B1 = 0.9
ADAM_B2 = 0.999
ADAM_EPS = 1e-08
ADAM_WD = 0.01
ADAM_STEP = 10

N_DEV = 8
VMEM_LIMIT = 52 * 1024 * 1024

MESH = pl.DeviceIdType.MESH
ANY = pl.BlockSpec(memory_space=pl.ANY)


def _tile(dim, target, align=LANE):
    if dim <= target:
        return dim
    t = (target // align) * align
    while t >= align:
        if dim % t == 0:
            return t
        t -= align
    return dim


def _params(*sem):
    return pltpu.CompilerParams(dimension_semantics=sem, vmem_limit_bytes=VMEM_LIMIT)


def _rstd(x):
    return lax.rsqrt(jnp.mean(x * x, axis=-1, keepdims=True) + NORM_EPS)


def _mm(a, b, *, name, ta=False, tb=False, out_dtype=F32, alpha=1.0, tm=512, tn=1024, tk=1024):
    if ta:
        K, M = a.shape
    else:
        M, K = a.shape
    if tb:
        N, K2 = b.shape
    else:
        K2, N = b.shape
    assert K == K2, (a.shape, b.shape, ta, tb)
    tm, tn, tk = _tile(M, tm), _tile(N, tn), _tile(K, tk)
    nk = K // tk
    dims = (((0 if ta else 1,), (1 if tb else 0,)), ((), ()))

    def body(a_ref, b_ref, o_ref, *scratch):
        acc_ref = scratch[0] if nk > 1 else None
        part = lax.dot_general(a_ref[...].astype(BF16), b_ref[...].astype(BF16), dims,
                               preferred_element_type=F32)

        def finish(acc):
            o_ref[...] = (acc * alpha if alpha != 1.0 else acc).astype(out_dtype)

        if nk == 1:
            finish(part)
        else:
            k = pl.program_id(2)

            @pl.when(k == 0)
            def _():
                acc_ref[...] = part

            @pl.when(k > 0)
            def _():
                acc_ref[...] += part

            @pl.when(k == nk - 1)
            def _():
                finish(acc_ref[...])

    a_spec = pl.BlockSpec((tk, tm), lambda i, j, k: (k, i)) if ta else pl.BlockSpec((tm, tk), lambda i, j, k: (i, k))
    b_spec = pl.BlockSpec((tn, tk), lambda i, j, k: (j, k)) if tb else pl.BlockSpec((tk, tn), lambda i, j, k: (k, j))
    return pl.pallas_call(
        body, name=name,
        out_shape=jax.ShapeDtypeStruct((M, N), out_dtype),
        grid=(M // tm, N // tn, nk),
        in_specs=[a_spec, b_spec],
        out_specs=pl.BlockSpec((tm, tn), lambda i, j, k: (i, j)),
        scratch_shapes=[pltpu.VMEM((tm, tn), F32)] if nk > 1 else [],
        compiler_params=_params("parallel", "parallel", "arbitrary"),
    )(a, b)


def _dw_multi(pairs, *, name, tk=512):
    n = len(pairs)
    T = pairs[0][0].shape[0]
    tk = _tile(T, tk, 16)
    nk = T // tk
    shapes = [(a.shape[1], b.shape[1]) for a, b in pairs]

    def body(*refs):
        ins, outs, accs = refs[:2 * n], refs[2 * n:3 * n], refs[3 * n:]
        k = pl.program_id(0)
        parts = [lax.dot_general(ins[2 * i][...].astype(BF16), ins[2 * i + 1][...].astype(BF16), _TN,
                                 preferred_element_type=F32) for i in range(n)]

        @pl.when(k == 0)
        def _():
            for acc, part in zip(accs, parts):
                acc[...] = part

        @pl.when(k > 0)
        def _():
            for acc, part in zip(accs, parts):
                acc[...] += part

        @pl.when(k == nk - 1)
        def _():
            for out, acc in zip(outs, accs):
                out[...] = acc[...].astype(BF16)

    return pl.pallas_call(
        body, name=name,
        out_shape=tuple(jax.ShapeDtypeStruct(s, BF16) for s in shapes),
        grid=(nk,),
        in_specs=[pl.BlockSpec((tk, x.shape[1]), lambda k: (k, 0)) for pair in pairs for x in pair],
        out_specs=tuple(pl.BlockSpec(s, lambda k: (0, 0)) for s in shapes),
        scratch_shapes=[pltpu.VMEM(s, F32) for s in shapes],
        compiler_params=_params("arbitrary"),
    )(*[x for pair in pairs for x in pair])


def _rms_fwd(x, g, *, name):
    T, D = x.shape
    tm = _tile(T, 512, 16)

    def body(x_ref, g_ref, h_ref):
        x = x_ref[...]
        h_ref[...] = (x * _rstd(x) * g_ref[...]).astype(BF16)

    return pl.pallas_call(
        body, name=name,
        out_shape=jax.ShapeDtypeStruct((T, D), BF16),
        grid=(T // tm,),
        in_specs=[pl.BlockSpec((tm, D), lambda i: (i, 0)), pl.BlockSpec((1, D), lambda i: (0, 0))],
        out_specs=pl.BlockSpec((tm, D), lambda i: (i, 0)),
        compiler_params=_params("parallel"),
    )(x, g.reshape(1, D))


def _dh_norm_bwd(a1, b1, a2, b2, b2_row0, x, g, dxo, *, name):
    T, D = x.shape
    K1, K2 = a1.shape[1], a2.shape[1]
    assert b2_row0 % K2 == 0 and b1.shape[0] >= K1 and b2.shape[0] >= b2_row0 + K2
    tm = _tile(T, 256, 16)

    def body(a1_ref, b1_ref, a2_ref, b2_ref, x_ref, g_ref, dxo_ref, dx_ref, dg_ref):
        x = x_ref[...]
        r = _rstd(x)
        xhat = x * r
        dh = (jnp.dot(a1_ref[...], b1_ref[...], preferred_element_type=F32)
              + jnp.dot(a2_ref[...], b2_ref[...], preferred_element_type=F32))
        dxh = dh * g_ref[...]
        dx_ref[...] = dxo_ref[...] + r * (dxh - xhat * jnp.mean(dxh * xhat, axis=-1, keepdims=True))
        part = jnp.sum(dh * xhat, axis=0, keepdims=True)

        @pl.when(pl.program_id(0) == 0)
        def _():
            dg_ref[...] = part

        @pl.when(pl.program_id(0) > 0)
        def _():
            dg_ref[...] += part

    row = pl.BlockSpec((tm, D), lambda i: (i, 0))
    vec = pl.BlockSpec((1, D), lambda i: (0, 0))
    return pl.pallas_call(
        body, name=name,
        out_shape=(jax.ShapeDtypeStruct((T, D), F32), jax.ShapeDtypeStruct((1, D), F32)),
        grid=(T // tm,),
        in_specs=[pl.BlockSpec((tm, K1), lambda i: (i, 0)), pl.BlockSpec((K1, D), lambda i: (0, 0)),
                  pl.BlockSpec((tm, K2), lambda i: (i, 0)), pl.BlockSpec((K2, D), lambda i: (b2_row0 // K2, 0)),
                  row, vec, row],
        out_specs=(row, vec),
        compiler_params=_params("arbitrary"),
    )(a1, b1, a2, b2, x, g.reshape(1, D), dxo)


def _loss_head(x, g, target, *, name):
    T, D = x.shape
    tm = _tile(T, 512, 16)

    def body(x_ref, g_ref, t_ref, dx_ref, dg_ref, loss_ref):
        x = x_ref[...]
        gain = g_ref[...]
        r = _rstd(x)
        xhat = x * r
        err = xhat * gain - t_ref[...]
        dy = err * (1.0 / D)
        dxh = dy * gain
        dx_ref[...] = r * (dxh - xhat * jnp.mean(dxh * xhat, axis=-1, keepdims=True))
        dg_part = jnp.sum(dy * xhat, axis=0, keepdims=True)
        loss_part = jnp.full((1, LANE), 0.5 / D, F32) * jnp.sum(err * err)

        @pl.when(pl.program_id(0) == 0)
        def _():
            dg_ref[...] = dg_part
            loss_ref[...] = loss_part

        @pl.when(pl.program_id(0) > 0)
        def _():
            dg_ref[...] += dg_part
            loss_ref[...] += loss_part

    row = pl.BlockSpec((tm, D), lambda i: (i, 0))
    vec = pl.BlockSpec((1, D), lambda i: (0, 0))
    return pl.pallas_call(
        body, name=name,
        out_shape=(jax.ShapeDtypeStruct((T, D), F32), jax.ShapeDtypeStruct((1, D), F32),
                   jax.ShapeDtypeStruct((1, LANE), F32)),
        grid=(T // tm,),
        in_specs=[row, vec, row],
        out_specs=(row, vec, pl.BlockSpec((1, LANE), lambda i: (0, 0))),
        compiler_params=_params("arbitrary"),
    )(x, g.reshape(1, D), target)


def _ffn_fwd_core(x, h, w_up_t, wd, next_gain, *, alpha, name, dep=None):
    T, D = x.shape
    F = wd.shape[0]
    tm = _tile(T, 256, 16)
    has_norm = next_gain is not None

    def body(x_ref, h_ref, wg_ref, wu_ref, wd_ref, *rest):
        outs = rest[len(rest) - (5 if has_norm else 4):]
        gate_ref, up_ref, a_ref, xn_ref = outs[:4]
        h = h_ref[...]
        gate = lax.dot_general(h, wg_ref[...], _NT, preferred_element_type=F32)
        up = lax.dot_general(h, wu_ref[...], _NT, preferred_element_type=F32)
        a = (gate * jax.nn.sigmoid(gate) * up).astype(BF16)
        gate_ref[...] = gate.astype(BF16)
        up_ref[...] = up.astype(BF16)
        a_ref[...] = a
        xn = x_ref[...] + alpha * jnp.dot(a, wd_ref[...], preferred_element_type=F32)
        xn_ref[...] = xn
        if has_norm:
            outs[4][...] = (xn * _rstd(xn) * rest[0][...]).astype(BF16)

    once = pl.Buffered(1)
    row_d = pl.BlockSpec((tm, D), lambda i: (i, 0))
    row_f = pl.BlockSpec((tm, F), lambda i: (i, 0))
    vec = pl.BlockSpec((1, D), lambda i: (0, 0))
    act = jax.ShapeDtypeStruct((T, F), BF16)
    operands = [x, h, w_up_t, w_up_t, wd] + ([next_gain.reshape(1, D)] if has_norm else [])
    out = pl.pallas_call(
        body, name=name,
        out_shape=(act, act, act, jax.ShapeDtypeStruct((T, D), F32)) + ((jax.ShapeDtypeStruct((T, D), BF16),) if has_norm else ()),
        grid=(T // tm,),
        in_specs=[row_d, row_d,
                  pl.BlockSpec((F, D), lambda i: (0, 0), pipeline_mode=once),
                  pl.BlockSpec((F, D), lambda i: (1, 0), pipeline_mode=once),
                  pl.BlockSpec((F, D), lambda i: (0, 0), pipeline_mode=once)] + ([vec] if has_norm else []) + _dep_spec(dep),
        out_specs=(row_f, row_f, row_f, row_d) + ((row_d,) if has_norm else ()),
        compiler_params=_params("parallel"),
    )(*operands, *([] if dep is None else [dep]))
    return out if has_norm else (*out, None)


def _ffn_bwd_core(dxo, wd, w_up_t, gate, up, x, gain, *, alpha, name, dep=None):
    T, D = dxo.shape
    F = wd.shape[0]
    tm = _tile(T, 256, 16)

    def body(dxo_ref, wd_ref, wg_ref, wu_ref, gate_ref, up_ref, x_ref, g_ref, *rest):
        dgate_ref, dup_ref, dx_ref, dg_ref = rest[-4:]
        dxo = dxo_ref[...]
        da = lax.dot_general(dxo.astype(BF16), wd_ref[...], _NT, preferred_element_type=F32) * alpha
        gate = gate_ref[...].astype(F32)
        up = up_ref[...].astype(F32)
        sig = jax.nn.sigmoid(gate)
        dgate = (da * up * (sig * (1.0 + gate * (1.0 - sig)))).astype(BF16)
        dup = (da * (gate * sig)).astype(BF16)
        dgate_ref[...] = dgate
        dup_ref[...] = dup
        dh = (jnp.dot(dgate, wg_ref[...], preferred_element_type=F32)
              + jnp.dot(dup, wu_ref[...], preferred_element_type=F32))
        x = x_ref[...]
        r = _rstd(x)
        xhat = x * r
        dxh = dh * g_ref[...]
        dx_ref[...] = dxo + r * (dxh - xhat * jnp.mean(dxh * xhat, axis=-1, keepdims=True))
        part = jnp.sum(dh * xhat, axis=0, keepdims=True)

        @pl.when(pl.program_id(0) == 0)
        def _():
            dg_ref[...] = part

        @pl.when(pl.program_id(0) > 0)
        def _():
            dg_ref[...] += part

    once = pl.Buffered(1)
    row_d = pl.BlockSpec((tm, D), lambda i: (i, 0))
    row_f = pl.BlockSpec((tm, F), lambda i: (i, 0))
    vec = pl.BlockSpec((1, D), lambda i: (0, 0))
    act = jax.ShapeDtypeStruct((T, F), BF16)
    return pl.pallas_call(
        body, name=name,
        out_shape=(act, act, jax.ShapeDtypeStruct((T, D), F32), jax.ShapeDtypeStruct((1, D), F32)),
        grid=(T // tm,),
        in_specs=[row_d,
                  pl.BlockSpec((F, D), lambda i: (0, 0), pipeline_mode=once),
                  pl.BlockSpec((F, D), lambda i: (0, 0), pipeline_mode=once),
                  pl.BlockSpec((F, D), lambda i: (1, 0), pipeline_mode=once),
                  row_f, row_f, row_d, vec] + _dep_spec(dep),
        out_specs=(row_f, row_f, row_d, vec),
        compiler_params=_params("arbitrary"),
    )(dxo, wd, w_up_t, w_up_t, gate, up, x, gain.reshape(1, D), *([] if dep is None else [dep]))


def _ffn_dw_up(dgate, dup, h, *, name):
    T, F = dgate.shape
    D = h.shape[1]
    tm, tk = _tile(F, 1408), _tile(T, 1024, 16)
    nf, nk = F // tm, T // tk

    def body(dgate_ref, dup_ref, h_ref, o_ref, acc_ref):
        i, k = pl.program_id(0), pl.program_id(1)

        def accumulate(part):
            @pl.when(k == 0)
            def _():
                acc_ref[...] = part

            @pl.when(k > 0)
            def _():
                acc_ref[...] += part

        @pl.when(i < nf)
        def _():
            accumulate(lax.dot_general(dgate_ref[...], h_ref[...], _TN, preferred_element_type=F32))

        @pl.when(i >= nf)
        def _():
            accumulate(lax.dot_general(dup_ref[...], h_ref[...], _TN, preferred_element_type=F32))

        @pl.when(k == nk - 1)
        def _():
            o_ref[...] = acc_ref[...].astype(BF16)

    return pl.pallas_call(
        body, name=name,
        out_shape=jax.ShapeDtypeStruct((2 * F, D), BF16),
        grid=(2 * nf, nk),
        in_specs=[pl.BlockSpec((tk, tm), lambda i, k: (jnp.where(i < nf, k, nk - 1), jnp.minimum(i, nf - 1))),
                  pl.BlockSpec((tk, tm), lambda i, k: (jnp.where(i < nf, 0, k), jnp.maximum(i - nf, 0))),
                  pl.BlockSpec((tk, D), lambda i, k: (k, 0))],
        out_specs=pl.BlockSpec((tm, D), lambda i, k: (i, 0)),
        scratch_shapes=[pltpu.VMEM((tm, D), F32)],
        compiler_params=_params("parallel", "arbitrary"),
    )(dgate, dup, h)


def _dep_spec(dep):
    return [] if dep is None else [pl.BlockSpec(dep.shape, lambda *_: (0,) * dep.ndim)]


def _rope_tables(positions):
    half = QK_ROPE // 2
    inv_freq = ROPE_THETA ** (-jnp.arange(0, QK_ROPE, 2, dtype=F32) / QK_ROPE)
    ang = positions.astype(F32)[:, None] * inv_freq
    cos, sin = jnp.cos(ang), jnp.sin(ang)
    z = jnp.zeros_like(cos)
    zz = jnp.zeros((positions.shape[0], LANE - QK_ROPE), F32)
    c = jnp.concatenate([cos, cos, zz], axis=1)
    sa = jnp.concatenate([z, sin, zz], axis=1)
    sb = jnp.concatenate([-sin, z, zz], axis=1)
    return c, sa, sb


def _rotate(seg, c, sa, sb, sign):
    half = QK_ROPE // 2
    mix = pltpu.roll(seg, half, 1) * sa + pltpu.roll(seg, LANE - half, 1) * sb
    return seg * c + mix if sign > 0 else seg * c - mix


def _mixer_in(h, wa, wuq, wukv, gq, gkv, tabs, *, name, dep=None):
    T, D = h.shape
    HQ, QL = wuq.shape
    KVL = wukv.shape[0]
    H = HQ // HEAD_W
    o_q, o_kv, o_kr = POOL_DIM, POOL_DIM + QL, POOL_DIM + QL + KVL
    PA = o_kr + LANE
    assert wa.shape[0] >= PA
    tm = _tile(T, 512, 16)

    def body(h_ref, wa_ref, wuq_ref, wukv_ref, gq_ref, gkv_ref, c_ref, sa_ref, sb_ref, *rest):
        xp_ref, ql_ref, kvl_ref, qn_ref, kvn_ref, q_ref, kv_ref, kr_ref = rest[-8:]
        proj = lax.dot_general(h_ref[...], wa_ref[...], _NT, preferred_element_type=F32)
        xp_ref[...] = proj[:, :POOL_DIM]
        ql = proj[:, o_q:o_kv]
        kvl = proj[:, o_kv:o_kr]
        ql_ref[...] = ql
        kvl_ref[...] = kvl
        qn = (ql * _rstd(ql) * gq_ref[...]).astype(BF16)
        kvn = (kvl * _rstd(kvl) * gkv_ref[...]).astype(BF16)
        qn_ref[...] = qn
        kvn_ref[...] = kvn
        c, sa, sb = c_ref[...], sa_ref[...], sb_ref[...]
        q = lax.dot_general(qn, wuq_ref[...], _NT, preferred_element_type=F32)
        for hh in range(H):
            base = hh * HEAD_W
            q_ref[:, base:base + QK_NOPE] = q[:, base:base + QK_NOPE].astype(BF16)
            q_ref[:, base + QK_NOPE:base + HEAD_W] = _rotate(
                q[:, base + QK_NOPE:base + HEAD_W], c, sa, sb, 1).astype(BF16)
        kv_ref[...] = jnp.dot(kvn, wukv_ref[...], preferred_element_type=F32).astype(BF16)
        kr_ref[...] = _rotate(proj[:, o_kr:o_kr + LANE], c, sa, sb, 1).astype(BF16)

    def row(w):
        return pl.BlockSpec((tm, w), lambda i: (i, 0))

    def whole(arr):
        return pl.BlockSpec(arr.shape, lambda i: (0,) * arr.ndim)

    gq2, gkv2 = gq.reshape(1, QL), gkv.reshape(1, KVL)
    outs = [(POOL_DIM, F32), (QL, F32), (KVL, F32), (QL, BF16), (KVL, BF16), (HQ, BF16), (HQ, BF16), (LANE, BF16)]
    return pl.pallas_call(
        body, name=name,
        out_shape=tuple(jax.ShapeDtypeStruct((T, w), dt) for w, dt in outs),
        grid=(T // tm,),
        in_specs=[row(D), pl.BlockSpec((PA, D), lambda i: (0, 0)), whole(wuq), whole(wukv), whole(gq2), whole(gkv2),
                  row(LANE), row(LANE), row(LANE)] + _dep_spec(dep),
        out_specs=tuple(row(w) for w, _ in outs),
        compiler_params=_params("parallel"),
    )(h, wa, wuq, wukv, gq2, gkv2, *tabs, *([] if dep is None else [dep]))


def _mixer_in_bwd(dq, dkv, dkr, ql, kvl, dxp, wuq, wukv, gq, gkv, tabs, *, name):
    T, HQ = dq.shape
    QL, KVL = wuq.shape[1], wukv.shape[0]
    H = HQ // HEAD_W
    PA = POOL_DIM + QL + KVL + LANE
    o_q, o_kv, o_kr = POOL_DIM, POOL_DIM + QL, POOL_DIM + QL + KVL
    tm = _tile(T, 512, 16)

    def norm_bwd(lat, gain, dn):
        r = _rstd(lat)
        xhat = lat * r
        dxh = dn * gain
        dlat = r * (dxh - xhat * jnp.mean(dxh * xhat, axis=-1, keepdims=True))
        return dlat, jnp.sum(dn * xhat, axis=0, keepdims=True)

    def body(dq_ref, dkv_ref, dkr_ref, ql_ref, kvl_ref, dxp_ref, wuq_ref, wukv_ref, gq_ref, gkv_ref,
             c_ref, sa_ref, sb_ref, dproj_ref, dqp_ref, dgq_ref, dgkv_ref):
        c, sa, sb = c_ref[...], sa_ref[...], sb_ref[...]
        for hh in range(H):
            base = hh * HEAD_W
            dqp_ref[:, base:base + QK_NOPE] = dq_ref[:, base:base + QK_NOPE]
            dqp_ref[:, base + QK_NOPE:base + HEAD_W] = _rotate(
                dq_ref[:, base + QK_NOPE:base + HEAD_W].astype(F32), c, sa, sb, -1).astype(BF16)
        dqn = jnp.dot(dqp_ref[...], wuq_ref[...], preferred_element_type=F32)
        dkvn = lax.dot_general(dkv_ref[...], wukv_ref[...], _NT, preferred_element_type=F32)
        dql, dgq = norm_bwd(ql_ref[...], gq_ref[...], dqn)
        dkvl, dgkv = norm_bwd(kvl_ref[...], gkv_ref[...], dkvn)
        dproj_ref[:, :POOL_DIM] = dxp_ref[...].astype(BF16)
        dproj_ref[:, o_q:o_kv] = dql.astype(BF16)
        dproj_ref[:, o_kv:o_kr] = dkvl.astype(BF16)
        dproj_ref[:, o_kr:PA] = _rotate(dkr_ref[...], c, sa, sb, -1).astype(BF16)

        @pl.when(pl.program_id(0) == 0)
        def _():
            dgq_ref[...] = dgq
            dgkv_ref[...] = dgkv

        @pl.when(pl.program_id(0) > 0)
        def _():
            dgq_ref[...] += dgq
            dgkv_ref[...] += dgkv

    def row(w):
        return pl.BlockSpec((tm, w), lambda i: (i, 0))

    def whole(arr):
        return pl.BlockSpec(arr.shape, lambda i: (0,) * arr.ndim)

    gq2, gkv2 = gq.reshape(1, QL), gkv.reshape(1, KVL)
    return pl.pallas_call(
        body, name=name,
        out_shape=(jax.ShapeDtypeStruct((T, PA), BF16), jax.ShapeDtypeStruct((T, HQ), BF16),
                   jax.ShapeDtypeStruct((1, QL), F32), jax.ShapeDtypeStruct((1, KVL), F32)),
        grid=(T // tm,),
        in_specs=[row(HQ), row(HQ), row(LANE), row(QL), row(KVL), row(POOL_DIM), whole(wuq), whole(wukv),
                  whole(gq2), whole(gkv2), row(LANE), row(LANE), row(LANE)],
        out_specs=(row(PA), row(HQ), whole(gq2), whole(gkv2)),
        compiler_params=_params("arbitrary"),
    )(dq, dkv, dkr, ql, kvl, dxp, wuq, wukv, gq2, gkv2, *tabs)


def _pool_groups(x_of, S):
    row = lax.broadcasted_iota(jnp.int32, (S, POOL_G), 0)
    for g, w in enumerate(POOL_WINDOWS):
        x = x_of(g)
        s = x
        d = 1
        while d < w:
            s = s + jnp.where(row >= d, pltpu.roll(s, d, 0), 0.0)
            d *= 2
        cnt = jnp.minimum(row + 1, w).astype(F32)
        yield g, w, x, s / cnt - x, cnt, row


def _pool_fwd(xp, maps, scale, *, S, name):
    T = xp.shape[0]

    def body(xp_ref, maps_ref, scale_ref, ms_ref):
        for g, _, _, pooled, _, _ in _pool_groups(lambda g: xp_ref[:, g * POOL_G:(g + 1) * POOL_G], S):
            mixed = jnp.dot(pooled.astype(BF16), maps_ref[g].astype(BF16), preferred_element_type=F32)
            ms_ref[:, g * POOL_G:(g + 1) * POOL_G] = (mixed * scale_ref[:, g * POOL_G:(g + 1) * POOL_G]).astype(BF16)

    return pl.pallas_call(
        body, name=name,
        out_shape=jax.ShapeDtypeStruct((T, POOL_DIM), BF16),
        grid=(T // S,),
        in_specs=[pl.BlockSpec((S, POOL_DIM), lambda b: (b, 0)),
                  pl.BlockSpec(maps.shape, lambda b: (0, 0, 0)),
                  pl.BlockSpec((1, POOL_DIM), lambda b: (0, 0))],
        out_specs=pl.BlockSpec((S, POOL_DIM), lambda b: (b, 0)),
        compiler_params=_params("parallel"),
    )(xp, maps, scale.reshape(1, POOL_DIM))


def _pool_bwd(xp, dms, maps, scale, *, S, name):
    T = xp.shape[0]

    def body(xp_ref, dms_ref, maps_ref, scale_ref, dxp_ref, dmaps_ref, dscale_ref):
        first = pl.program_id(0) == 0
        for g, w, _, pooled, cnt, row in _pool_groups(lambda g: xp_ref[:, g * POOL_G:(g + 1) * POOL_G], S):
            cols = slice(g * POOL_G, (g + 1) * POOL_G)
            pooled_b = pooled.astype(BF16)
            maps_b = maps_ref[g].astype(BF16)
            mixed = jnp.dot(pooled_b, maps_b, preferred_element_type=F32)
            dms = dms_ref[:, cols]
            dscale = jnp.sum(dms * mixed, axis=0, keepdims=True)
            dmixed = (dms * scale_ref[:, cols]).astype(BF16)
            dmaps = lax.dot_general(pooled_b, dmixed, (((0,), (0,)), ((), ())), preferred_element_type=F32)
            dpooled = lax.dot_general(dmixed, maps_b, (((1,), (1,)), ((), ())), preferred_element_type=F32)
            z = dpooled / cnt
            d = 1
            while d < w:
                z = z + jnp.where(row < S - d, pltpu.roll(z, S - d, 0), 0.0)
                d *= 2
            dxp_ref[:, cols] = z - dpooled

            @pl.when(first)
            def _():
                dmaps_ref[g] = dmaps
                dscale_ref[:, cols] = dscale

            @pl.when(jnp.logical_not(first))
            def _():
                dmaps_ref[g] += dmaps
                dscale_ref[:, cols] += dscale

    seq = pl.BlockSpec((S, POOL_DIM), lambda b: (b, 0))
    maps_spec = pl.BlockSpec(maps.shape, lambda b: (0, 0, 0))
    vec = pl.BlockSpec((1, POOL_DIM), lambda b: (0, 0))
    return pl.pallas_call(
        body, name=name,
        out_shape=(jax.ShapeDtypeStruct((T, POOL_DIM), F32), jax.ShapeDtypeStruct(maps.shape, F32),
                   jax.ShapeDtypeStruct((1, POOL_DIM), F32)),
        grid=(T // S,),
        in_specs=[seq, seq, maps_spec, vec],
        out_specs=(seq, maps_spec, vec),
        compiler_params=_params("arbitrary"),
    )(xp, dms, maps, scale.reshape(1, POOL_DIM))


def _causal_mask(s, t):
    r = lax.broadcasted_iota(jnp.int32, (t, t), 0)
    c = lax.broadcasted_iota(jnp.int32, (t, t), 1)
    return jnp.where(r >= c, s, MASK_VALUE)


_NT = (((1,), (1,)), ((), ()))
_TN = (((0,), (0,)), ((), ()))


def _attn_fwd(q, kv, kr, *, S, name):
    T, HQ = q.shape
    H = HQ // HEAD_W
    B = T // S
    t = _tile(S, ATTN_TILE)
    n = S // t

    def body(q_ref, k_ref, v_ref, kr_ref, o_ref, lse_ref, kcat):
        kcat[:, :QK_NOPE] = k_ref[...]
        kcat[:, QK_NOPE:] = kr_ref[...]
        for i in range(n):
            rows = slice(i * t, (i + 1) * t)
            qt = q_ref[rows, :]
            m = jnp.full((t, 1), MASK_VALUE, F32)
            l = jnp.zeros((t, 1), F32)
            acc = jnp.zeros((t, V_DIM), F32)
            for j in range(i + 1):
                cols = slice(j * t, (j + 1) * t)
                s = lax.dot_general(qt, kcat[cols, :], _NT, preferred_element_type=F32) * ATTN_SCALE_LOG2
                if j == i:
                    s = _causal_mask(s, t)
                m_new = jnp.maximum(m, jnp.max(s, axis=1, keepdims=True))
                p = jnp.exp2(s - m_new)
                corr = jnp.exp2(m - m_new)
                l = corr * l + jnp.sum(p, axis=1, keepdims=True)
                acc = corr * acc + jnp.dot(p.astype(BF16), v_ref[cols, :], preferred_element_type=F32)
                m = m_new
            o_ref[rows, :] = (acc / l).astype(BF16)
            lse_ref[rows, :] = jnp.broadcast_to(m + jnp.log2(l), (t, LANE))

    seq_h = pl.BlockSpec((S, LANE), lambda b, h: (b, h))
    return pl.pallas_call(
        body, name=name,
        out_shape=(jax.ShapeDtypeStruct((T, H * V_DIM), BF16), jax.ShapeDtypeStruct((T, H * LANE), F32)),
        grid=(B, H),
        in_specs=[pl.BlockSpec((S, HEAD_W), lambda b, h: (b, h)),
                  pl.BlockSpec((S, QK_NOPE), lambda b, h: (b, 2 * h)),
                  pl.BlockSpec((S, V_DIM), lambda b, h: (b, 2 * h + 1)),
                  pl.BlockSpec((S, LANE), lambda b, h: (b, 0))],
        out_specs=(seq_h, seq_h),
        scratch_shapes=[pltpu.VMEM((S, HEAD_W), BF16)],
        compiler_params=_params("parallel", "parallel"),
    )(q, kv, kv, kr)


def _attn_bwd(q, kv, kr, o, do, lse, *, S, name):
    T, HQ = q.shape
    H = HQ // HEAD_W
    B = T // S
    t = _tile(S, ATTN_TILE)
    n = S // t

    def body(q_ref, k_ref, v_ref, kr_ref, o_ref, do_ref, lse_ref, dq_ref, dkv_ref, dkr_ref, kcat, dq_acc):
        @pl.when(pl.program_id(1) == 0)
        def _():
            dkr_ref[...] = jnp.zeros_like(dkr_ref)

        kcat[:, :QK_NOPE] = k_ref[...]
        kcat[:, QK_NOPE:] = kr_ref[...]
        delta = [jnp.sum(do_ref[i * t:(i + 1) * t, :].astype(F32) * o_ref[i * t:(i + 1) * t, :].astype(F32),
                         axis=1, keepdims=True) for i in range(n)]
        for j in range(n):
            cols = slice(j * t, (j + 1) * t)
            kc = kcat[cols, :]
            vt = v_ref[cols, :]
            dk = jnp.zeros((t, HEAD_W), F32)
            dv = jnp.zeros((t, V_DIM), F32)
            for i in range(j, n):
                rows = slice(i * t, (i + 1) * t)
                qt = q_ref[rows, :]
                dot_ = do_ref[rows, :]
                s = lax.dot_general(qt, kc, _NT, preferred_element_type=F32) * ATTN_SCALE_LOG2
                if i == j:
                    s = _causal_mask(s, t)
                p = jnp.exp2(s - lse_ref[rows, :][:, :1])
                dv = dv + lax.dot_general(p.astype(BF16), dot_, _TN, preferred_element_type=F32)
                dp = lax.dot_general(dot_, vt, _NT, preferred_element_type=F32)
                ds = (p * (dp - delta[i]) * ATTN_SCALE).astype(BF16)
                dk = dk + lax.dot_general(ds, qt, _TN, preferred_element_type=F32)
                dq_part = jnp.dot(ds, kc, preferred_element_type=F32)
                if j == 0:
                    dq_acc[rows, :] = dq_part
                else:
                    dq_acc[rows, :] += dq_part
            dkv_ref[cols, :QK_NOPE] = dk[:, :QK_NOPE].astype(BF16)
            dkv_ref[cols, QK_NOPE:] = dv.astype(BF16)
            dkr_ref[cols, :] += dk[:, QK_NOPE:]
        dq_ref[...] = dq_acc[...].astype(BF16)

    seq_q = pl.BlockSpec((S, HEAD_W), lambda b, h: (b, h))
    seq_h = pl.BlockSpec((S, LANE), lambda b, h: (b, h))
    seq_shared = pl.BlockSpec((S, LANE), lambda b, h: (b, 0))
    return pl.pallas_call(
        body, name=name,
        out_shape=(jax.ShapeDtypeStruct((T, HQ), BF16), jax.ShapeDtypeStruct((T, HQ), BF16),
                   jax.ShapeDtypeStruct((T, LANE), F32)),
        grid=(B, H),
        in_specs=[seq_q,
                  pl.BlockSpec((S, QK_NOPE), lambda b, h: (b, 2 * h)),
                  pl.BlockSpec((S, V_DIM), lambda b, h: (b, 2 * h + 1)),
                  seq_shared, seq_h, seq_h, seq_h],
        out_specs=(seq_q, seq_q, seq_shared),
        scratch_shapes=[pltpu.VMEM((S, HEAD_W), BF16), pltpu.VMEM((S, HEAD_W), F32)],
        compiler_params=_params("parallel", "arbitrary"),
    )(q, kv, kv, kr, o, do, lse)


def _merge_out(h, ms, o, x, wgate, bgate, wpp, wap, wout, next_gain, *, name):
    T, D = x.shape
    tm = _tile(T, 256, 16)

    def body(h_ref, ms_ref, o_ref, x_ref, wgate_ref, bgate_ref, wpp_ref, wap_ref, wout_ref, ng_ref,
             gates_ref, ba_ref, bb_ref, merged_ref, xn_ref, hn_ref):
        logits = lax.dot_general(h_ref[...], wgate_ref[...], _NT, preferred_element_type=F32) + bgate_ref[...]
        gates = jax.nn.sigmoid(logits)
        ba = jnp.dot(ms_ref[...], wpp_ref[...], preferred_element_type=F32)
        bb = jnp.dot(o_ref[...], wap_ref[...], preferred_element_type=F32)
        merged = (gates[:, :D] * ba + gates[:, D:] * bb).astype(BF16)
        gates_ref[...] = gates.astype(BF16)
        ba_ref[...] = ba.astype(BF16)
        bb_ref[...] = bb.astype(BF16)
        merged_ref[...] = merged
        xn = x_ref[...] + jnp.dot(merged, wout_ref[...], preferred_element_type=F32)
        xn_ref[...] = xn
        hn_ref[...] = (xn * _rstd(xn) * ng_ref[...]).astype(BF16)

    def row(w):
        return pl.BlockSpec((tm, w), lambda i: (i, 0))

    def whole(arr):
        return pl.BlockSpec(arr.shape, lambda i: (0,) * arr.ndim)

    bg2, ng2 = bgate.reshape(1, 2 * D), next_gain.reshape(1, D)
    act = jax.ShapeDtypeStruct((T, D), BF16)
    return pl.pallas_call(
        body, name=name,
        out_shape=(jax.ShapeDtypeStruct((T, 2 * D), BF16), act, act, act, jax.ShapeDtypeStruct((T, D), F32), act),
        grid=(T // tm,),
        in_specs=[row(D), row(ms.shape[1]), row(o.shape[1]), row(D), whole(wgate), whole(bg2), whole(wpp),
                  whole(wap), whole(wout), whole(ng2)],
        out_specs=(row(2 * D), row(D), row(D), row(D), row(D), row(D)),
        compiler_params=_params("parallel"),
    )(h, ms, o, x, wgate, bg2, wpp, wap, wout, ng2)


def _merge_bwd(dxo, wout, wpp, wap, gates, ba, bb, *, name, dep=None):
    T, D = dxo.shape
    tm = _tile(T, 512, 16)

    def body(dxo_ref, wout_ref, wpp_ref, wap_ref, gates_ref, ba_ref, bb_ref, *rest):
        dba_ref, dbb_ref, dgl_ref, dbg_ref, dms_ref, do_ref = rest[-6:]
        dm = lax.dot_general(dxo_ref[...].astype(BF16), wout_ref[...], _NT, preferred_element_type=F32)
        ga = gates_ref[:, :D].astype(F32)
        gb = gates_ref[:, D:].astype(F32)
        dba = (dm * ga).astype(BF16)
        dbb = (dm * gb).astype(BF16)
        dba_ref[...] = dba
        dbb_ref[...] = dbb
        dms_ref[...] = lax.dot_general(dba, wpp_ref[...], _NT, preferred_element_type=F32)
        do_ref[...] = lax.dot_general(dbb, wap_ref[...], _NT, preferred_element_type=F32).astype(BF16)
        dgl_a = dm * ba_ref[...].astype(F32) * (ga * (1.0 - ga))
        dgl_b = dm * bb_ref[...].astype(F32) * (gb * (1.0 - gb))
        dgl_ref[:, :D] = dgl_a.astype(BF16)
        dgl_ref[:, D:] = dgl_b.astype(BF16)
        sa = jnp.sum(dgl_a, axis=0, keepdims=True)
        sb = jnp.sum(dgl_b, axis=0, keepdims=True)

        @pl.when(pl.program_id(0) == 0)
        def _():
            dbg_ref[:, :D] = sa
            dbg_ref[:, D:] = sb

        @pl.when(pl.program_id(0) > 0)
        def _():
            dbg_ref[:, :D] += sa
            dbg_ref[:, D:] += sb

    def row(w):
        return pl.BlockSpec((tm, w), lambda i: (i, 0))

    def whole(arr):
        return pl.BlockSpec(arr.shape, lambda i: (0, 0))

    P, HV = wpp.shape[0], wap.shape[0]
    act = jax.ShapeDtypeStruct((T, D), BF16)
    return pl.pallas_call(
        body, name=name,
        out_shape=(act, act, jax.ShapeDtypeStruct((T, 2 * D), BF16), jax.ShapeDtypeStruct((1, 2 * D), F32),
                   jax.ShapeDtypeStruct((T, P), F32), jax.ShapeDtypeStruct((T, HV), BF16)),
        grid=(T // tm,),
        in_specs=[row(D), whole(wout), whole(wpp), whole(wap), row(2 * D), row(D), row(D)] + _dep_spec(dep),
        out_specs=(row(D), row(D), row(2 * D), pl.BlockSpec((1, 2 * D), lambda i: (0, 0)), row(P), row(HV)),
        compiler_params=_params("arbitrary"),
    )(dxo, wout, wpp, wap, gates, ba, bb, *([] if dep is None else [dep]))


def _ffn_fwd(x, h, w, tag, next_gain, dep=None):
    gate, up, a, xn, hn = _ffn_fwd_core(x, h, w["up_t"], w["wd"], next_gain, alpha=0.5,
                                        name=f"{tag}_fwd" if next_gain is not None else f"{tag}_fwd_last", dep=dep)
    return xn, hn, (x, h, gate, up, a)


def _ffn_bwd(dxo, gain, w, saved, tag, dep=None):
    x, h, gate, up, a = saved
    F = gate.shape[1]
    dgate, dup, dx, dgain = _ffn_bwd_core(dxo, w["wd"], w["up_t"], gate, up, x, gain, alpha=0.5,
                                          name=f"{tag}_bwd_core", dep=dep)
    dwd = _mm(a, dxo, ta=True, alpha=0.5, out_dtype=BF16, name=f"{tag}_dwd", tm=1408, tn=1024, tk=1024)
    dup_t = _ffn_dw_up(dgate, dup, h, name=f"{tag}_dw_up")
    return dx, dgain, dup_t, dwd


def _mixer_fwd(x, h, p, w, tabs, S, next_gain, dep=None):
    xp, ql, kvl, qn, kvn, q, kv, kr = _mixer_in(h, w["win_t"], w["wuq_t"], w["wukv"], p["q_latent_norm"],
                                                 p["kv_latent_norm"], tabs, name="mix_in", dep=dep)
    ms = _pool_fwd(xp, p["pool_maps"], p["pool_scale"], S=S, name="pool_fwd")
    o, lse = _attn_fwd(q, kv, kr, S=S, name="attn_fwd")
    gates, ba, bb, merged, xn, hn = _merge_out(h, ms, o, x, w["wgate_t"], p["b_gate"], w["wpp"], w["wap"], w["wout"],
                                               next_gain, name="merge_out")
    return xn, hn, (x, h, xp, ql, kvl, qn, kvn, q, kv, kr, ms, o, lse, gates, ba, bb, merged)


def _mixer_bwd(dxo, p, w, tabs, saved, S, dep=None):
    x, h, xp, ql, kvl, qn, kvn, q, kv, kr, ms, o, lse, gates, ba, bb, merged = saved
    dba, dbb, dgl, dbg, dms, do = _merge_bwd(dxo, w["wout"], w["wpp"], w["wap"], gates, ba, bb, name="merge_bwd",
                                             dep=dep)
    g = {}
    g["wout"], g["wpp"], g["wap"] = _dw_multi([(merged, dxo), (ms, dba), (o, dbb)], name="d_w_merge")
    dxp, g["pool_maps"], g["pool_scale"] = _pool_bwd(xp, dms, p["pool_maps"], p["pool_scale"], S=S, name="pool_bwd")
    dq, dkv, dkr = _attn_bwd(q, kv, kr, o, do, lse, S=S, name="attn_bwd")
    dproj, dqp, g["q_latent_norm"], g["kv_latent_norm"] = _mixer_in_bwd(
        dq, dkv, dkr, ql, kvl, dxp, w["wuq_t"], w["wukv"], p["q_latent_norm"], p["kv_latent_norm"], tabs,
        name="mix_in_bwd")
    g["wuq_t"], g["wukv"] = _dw_multi([(dqp, qn), (kvn, dkv)], name="d_w_qkv", tk=1024)
    g["wa_t"], g["wgate_t"] = _dw_multi([(dproj, h), (dgl, h)], name="d_w_in")
    dx, g["norm_mix"] = _dh_norm_bwd(dproj, w["win_t"], dgl, w["wgate_t"], 0, x, p["norm_mix"], dxo,
                                     name="mix_dh_norm_bwd")
    g["b_gate"] = dbg
    return dx, g


BIG = ("ffn1_up", "ffn1_down", "w_in", "w_pool_proj", "w_uq", "w_ukv", "w_attn_proj", "w_out", "ffn2_up", "ffn2_down")
SMALL = ("norm_ffn1", "norm_mix", "b_gate", "pool_maps", "pool_scale", "q_latent_norm", "kv_latent_norm", "norm_ffn2")
PACKED = ("w_pool_proj", "w_uq", "w_ukv")
TRANSPOSED = ("ffn1_up", "ffn2_up", "w_in", "w_uq")
COL_SHARDED = ("w_pool_proj", "w_ukv")
QK_HEAD = QK_NOPE + QK_ROPE


def _rows(stacked):
    n, r, c = stacked.shape
    return stacked.reshape(n * r, c)


def _cols(stacked):
    n, k, c = stacked.shape
    return stacked.transpose(1, 0, 2).reshape(k, n * c)


FFN1_PART = ("ffn1_up", "ffn1_down")
MIXER_PART = ("w_in", "w_attn_proj", "w_out") + PACKED
FFN2_PART = ("ffn2_up", "ffn2_down")


def _kernel_weights(stacked):
    full = {}
    for tag in ("ffn1", "ffn2"):
        if tag + "_up" in stacked:
            full[tag] = {"up_t": _rows(stacked[tag + "_up"]), "wd": _rows(stacked[tag + "_down"])}
    if "w_in" in stacked:
        win_t = _rows(stacked["w_in"])
        D = win_t.shape[1]
        wuq_t = _rows(stacked["w_uq"])
        QL = wuq_t.shape[1]
        H = wuq_t.shape[0] // QK_HEAD
        wuq_t = jnp.pad(wuq_t.reshape(H, QK_HEAD, QL), ((0, 0), (0, HEAD_W - QK_HEAD), (0, 0)))
        full.update({"win_t": win_t, "wgate_t": win_t[win_t.shape[0] - 2 * D:], "wuq_t": wuq_t.reshape(H * HEAD_W, QL),
                     "wukv": _cols(stacked["w_ukv"]), "wpp": _cols(stacked["w_pool_proj"]),
                     "wap": _rows(stacked["w_attn_proj"]), "wout": _rows(stacked["w_out"])})
    return full


def _split_rows(full):
    return full.reshape(N_DEV, full.shape[0] // N_DEV, full.shape[1])


def _split_cols(full):
    k, cols = full.shape
    return full.reshape(k, N_DEV, cols // N_DEV).transpose(1, 0, 2)


def _mixer_grads_stacked(g):
    n_a = g["wa_t"].shape[0] - (LANE - QK_ROPE)
    HQ, QL = g["wuq_t"].shape
    H = HQ // HEAD_W
    wuq_t = g["wuq_t"].reshape(H, HEAD_W, QL)[:, :QK_HEAD].reshape(H * QK_HEAD, QL)
    return {"w_in": _split_rows(jnp.concatenate([g["wa_t"][:n_a], g["wgate_t"]], axis=0)),
            "w_uq": _split_rows(wuq_t),
            "w_pool_proj": _split_cols(g["wpp"]), "w_ukv": _split_cols(g["wukv"]),
            "w_attn_proj": _split_rows(g["wap"]), "w_out": _split_rows(g["wout"])}


def _mesh_place():
    x, y, c = lax.axis_index("x"), lax.axis_index("y"), lax.axis_index("c")
    chips = [(1 - x, y), (x, 1 - y), (1 - x, 1 - y)]
    return x, y, c, chips


HBM = pl.BlockSpec(memory_space=pltpu.HBM)
SEMAPHORES = pl.BlockSpec(memory_space=pltpu.SEMAPHORE)
DATAFLOW = pltpu.SideEffectType.DATAFLOW_SIDE_EFFECTING
GATHER_PEERS = 4
SCATTER_PEERS = 7


def _in_hbm(a):
    return pltpu.with_memory_space_constraint(a, pltpu.HBM)


def _gather_plan(src_refs, land_refs):
    x, y, c, chips = _mesh_place()
    me = 4 * x + 2 * y + c
    targets = [(x, y, 1 - c)] + [(cx, cy, c) for cx, cy in chips]
    return [(s, land.at[me], to) for s, land in zip(src_refs, land_refs) for to in targets]


def _scatter_plan(src_refs, land_refs):
    x, y, c, _ = _mesh_place()
    peers = [(x, y, 1 - c), (1 - x, y, c), (x, 1 - y, c), (1 - x, 1 - y, c),
             (1 - x, y, 1 - c), (x, 1 - y, 1 - c), (1 - x, 1 - y, 1 - c)]
    return [(s.at[4 * px + 2 * py + pc], land.at[k], (px, py, pc))
            for s, land in zip(src_refs, land_refs) for k, (px, py, pc) in enumerate(peers)]


def _descriptors(plan, src_refs, land_refs, send_sems, recv_sems):
    return [pltpu.make_async_remote_copy(src_ref=s, dst_ref=d, send_sem=send_sems.at[k], recv_sem=recv_sems.at[k],
                                         device_id=to, device_id_type=MESH)
            for k, (s, d, to) in enumerate(plan(src_refs, land_refs))]


def _exchange(srcs, land_shapes, plan, per_src, *, name):
    n = len(srcs)

    def body(*refs):
        copies = _descriptors(plan, refs[:n], refs[n:2 * n], refs[2 * n], refs[2 * n + 1])
        for cp in copies:
            cp.start()
        for cp in copies:
            cp.wait()

    return pl.pallas_call(
        body, name=name,
        out_shape=tuple(jax.ShapeDtypeStruct(shape, s.dtype) for shape, s in zip(land_shapes, srcs)),
        in_specs=[ANY] * n, out_specs=(ANY,) * n,
        scratch_shapes=[pltpu.SemaphoreType.DMA((per_src * n,)), pltpu.SemaphoreType.DMA((per_src * n,))],
    )(*srcs)


FORWARD_COPIES = 4


def _forward_slots():
    x, y, c, chips = _mesh_place()
    return [4 * cx + 2 * cy + c for cx, cy in chips] + [4 * x + 2 * y + (1 - c)], (x, y, 1 - c)


def _forward_plan(src_refs, land_refs):
    slots, sibling = _forward_slots()
    return [(land.at[s], land.at[s], sibling) for land in land_refs for s in slots]


def _gather_all_plan(src_refs, land_refs):
    x, y, c, _ = _mesh_place()
    me = 4 * x + 2 * y + c
    peers = [(x, y, 1 - c), (1 - x, y, c), (x, 1 - y, c), (1 - x, 1 - y, c),
             (1 - x, y, 1 - c), (x, 1 - y, 1 - c), (1 - x, 1 - y, 1 - c)]
    return [(s, land.at[me], to) for s, land in zip(src_refs, land_refs) for to in peers]


def _exchange_start(srcs, lands, plan, n_copies, *, name):
    ns, n = len(srcs), len(srcs) + len(lands)

    def body(*refs):
        for cp in _descriptors(plan, refs[:ns], refs[ns:n], refs[n], refs[n + 1]):
            cp.start()
        refs[-1][...] = jnp.zeros_like(refs[-1])

    sems = pltpu.SemaphoreType.DMA((n_copies,))
    out = pl.pallas_call(
        body, name=name,
        out_shape=(sems, sems, *[pltpu.HBM(a.shape, a.dtype) for a in srcs + lands],
                   jax.ShapeDtypeStruct((8, LANE), F32)),
        in_specs=(HBM,) * n,
        out_specs=(SEMAPHORES, SEMAPHORES, *[HBM] * n, pl.BlockSpec(memory_space=pltpu.VMEM)),
        input_output_aliases={i: 2 + i for i in range(n)},
        compiler_params=pltpu.CompilerParams(has_side_effects=DATAFLOW),
    )(*[_in_hbm(a) for a in srcs + lands])
    return out[0], out[1], list(out[2:2 + ns]), list(out[2 + ns:2 + n]), out[-1]


def _exchange_wait(send_sems, recv_sems, srcs, lands, plan, after, *, name):
    ns, n = len(srcs), len(srcs) + len(lands)

    def body(*refs):
        for cp in _descriptors(plan, refs[:ns], refs[ns:n], refs[n], refs[n + 1]):
            cp.wait_send()
            cp.wait_recv()

    out = pl.pallas_call(
        body, name=name,
        out_shape=tuple(pltpu.HBM(a.shape, a.dtype) for a in srcs + lands),
        in_specs=(*[HBM] * n, SEMAPHORES, SEMAPHORES, ANY),
        out_specs=(HBM,) * n,
        input_output_aliases={i: i for i in range(n)},
        compiler_params=pltpu.CompilerParams(has_side_effects=DATAFLOW),
    )(*srcs, *lands, send_sems, recv_sems, after)
    return list(out[:ns]), list(out[ns:])


def _gather_forward(lands, *, name):
    n = len(lands)

    def body(*refs):
        in_refs, out_refs = refs[:n], refs[n:2 * n]
        token, send_sems, recv_sems = refs[2 * n:2 * n + 3]
        slots, sibling = _forward_slots()
        passed = [pltpu.make_async_remote_copy(
            src_ref=i.at[s], dst_ref=o.at[s],
            send_sem=send_sems.at[FORWARD_COPIES * b + j], recv_sem=recv_sems.at[FORWARD_COPIES * b + j],
            device_id=sibling, device_id_type=MESH)
            for b, (i, o) in enumerate(zip(in_refs, out_refs)) for j, s in enumerate(slots)]
        for cp in passed:
            cp.start()
        for cp in passed:
            cp.wait()
        token[...] = jnp.zeros_like(token)

    out = pl.pallas_call(
        body, name=name,
        out_shape=(*[jax.ShapeDtypeStruct(a.shape, a.dtype) for a in lands], jax.ShapeDtypeStruct((8, LANE), F32)),
        in_specs=[ANY] * n,
        out_specs=(*[ANY] * n, pl.BlockSpec(memory_space=pltpu.VMEM)),
        input_output_aliases={i: i for i in range(n)},
        scratch_shapes=[pltpu.SemaphoreType.DMA((FORWARD_COPIES * n,)), pltpu.SemaphoreType.DMA((FORWARD_COPIES * n,))],
    )(*lands)
    return list(out[:n]), out[n]


def _scatter_sum(parts, got, me, *, name):
    shard = parts.shape[1:]
    cols = shard[-1]
    rows = int(np.prod(shard[:-1]))
    tr = _tile(rows, 256, 16)

    def body(me_ref, p_ref, g_ref, o_ref):
        acc = p_ref[...].astype(F32)
        for k in range(SCATTER_PEERS):
            acc = acc + g_ref[k].astype(F32)
        o_ref[...] = acc

    out = pl.pallas_call(
        body, name=name,
        out_shape=jax.ShapeDtypeStruct((rows, cols), F32),
        grid_spec=pltpu.PrefetchScalarGridSpec(
            num_scalar_prefetch=1, grid=(rows // tr,),
            in_specs=[pl.BlockSpec((None, tr, cols), lambda r, me_ref: (me_ref[0], r, 0)),
                      pl.BlockSpec((SCATTER_PEERS, tr, cols), lambda r, me_ref: (0, r, 0))],
            out_specs=pl.BlockSpec((tr, cols), lambda r, me_ref: (r, 0))),
        compiler_params=_params("parallel"),
    )(me, parts.reshape(N_DEV, rows, cols), got.reshape(SCATTER_PEERS, rows, cols))
    return out.reshape(shard)


def _sum_devices(parts, *, name):
    _, R, C = parts.shape
    tr = _tile(R, 512, 8)

    def body(p_ref, o_ref):
        acc = p_ref[0]
        for d in range(1, N_DEV):
            acc = acc + p_ref[d]
        o_ref[...] = acc

    return pl.pallas_call(
        body, name=name,
        out_shape=jax.ShapeDtypeStruct((R, C), F32),
        grid=(R // tr,),
        in_specs=[pl.BlockSpec((N_DEV, tr, C), lambda r: (0, r, 0))],
        out_specs=pl.BlockSpec((tr, C), lambda r: (r, 0)),
        compiler_params=_params("parallel"),
    )(parts)


def _adamw(w, g, m, v, *, name, dep=None):
    shape = w.shape
    cols = shape[-1]
    rows = w.size // cols
    tr = _tile(rows, 256, 8)

    def body(w_ref, g_ref, m_ref, v_ref, *rest):
        d_ref, nm_ref, nv_ref = rest[-3:]
        g = g_ref[...]
        m = ADAM_B1 * m_ref[...] + (1.0 - ADAM_B1) * g
        v = ADAM_B2 * v_ref[...] + (1.0 - ADAM_B2) * jnp.square(g)
        m_hat = m / (1.0 - ADAM_B1 ** ADAM_STEP)
        v_hat = v / (1.0 - ADAM_B2 ** ADAM_STEP)
        d_ref[...] = -ADAM_LR * (m_hat / (jnp.sqrt(v_hat) + ADAM_EPS) + ADAM_WD * w_ref[...])
        nm_ref[...] = m
        nv_ref[...] = v

    spec = pl.BlockSpec((tr, cols), lambda i: (i, 0))
    out = jax.ShapeDtypeStruct((rows, cols), F32)
    d, nm, nv = pl.pallas_call(
        body, name=name,
        out_shape=(out, out, out),
        grid=(rows // tr,),
        in_specs=[spec] * 4 + _dep_spec(dep), out_specs=(spec,) * 3,
        compiler_params=_params("parallel"),
    )(*(a.reshape(rows, cols) for a in (w, g, m, v)), *([] if dep is None else [dep]))
    return d.reshape(shape), nm.reshape(shape), nv.reshape(shape)


PACK_ALIGN = 16 * LANE


def _pack(pieces, lead):
    out = []
    for p in pieces:
        keep = p.shape[:lead]
        flat = p.reshape(*keep, -1)
        pad = (-flat.shape[-1]) % PACK_ALIGN
        if pad:
            flat = jnp.pad(flat, [(0, 0)] * lead + [(0, pad)])
        out.append(flat.reshape(*keep, -1, LANE))
    return jnp.concatenate(out, axis=lead)


def _unpack(buf, shapes, lead):
    keep = buf.shape[:lead]
    out, row = [], 0
    for shape in shapes:
        size = int(np.prod(shape))
        rows = -(-size // PACK_ALIGN) * (PACK_ALIGN // LANE)
        piece = lax.slice_in_dim(buf, row, row + rows, axis=lead).reshape(*keep, rows * LANE)
        out.append(lax.slice_in_dim(piece, 0, size, axis=lead).reshape(*keep, *shape))
        row += rows
    return out


def kernel(x, positions, norm_ffn1, ffn1_up, ffn1_down, norm_mix, w_in, b_gate, pool_maps, pool_scale, w_pool_proj, q_latent_norm, w_uq, kv_latent_norm, w_ukv, w_attn_proj, w_out, norm_ffn2, ffn2_up, ffn2_down, final_norm, loss_target, m_norm_ffn1, m_ffn1_up, m_ffn1_down, m_norm_mix, m_w_in, m_b_gate, m_pool_maps, m_pool_scale, m_w_pool_proj, m_q_latent_norm, m_w_uq, m_kv_latent_norm, m_w_ukv, m_w_attn_proj, m_w_out, m_norm_ffn2, m_ffn2_up, m_ffn2_down, m_final_norm, v_norm_ffn1, v_ffn1_up, v_ffn1_down, v_norm_mix, v_w_in, v_b_gate, v_pool_maps, v_pool_scale, v_w_pool_proj, v_q_latent_norm, v_w_uq, v_kv_latent_norm, v_w_ukv, v_w_attn_proj, v_w_out, v_norm_ffn2, v_ffn2_up, v_ffn2_down, v_final_norm):
    order = ("norm_ffn1", "ffn1_up", "ffn1_down", "norm_mix", "w_in", "b_gate", "pool_maps", "pool_scale",
             "w_pool_proj", "q_latent_norm", "w_uq", "kv_latent_norm", "w_ukv", "w_attn_proj", "w_out",
             "norm_ffn2", "ffn2_up", "ffn2_down", "final_norm")
    w = dict(zip(order, (norm_ffn1, ffn1_up, ffn1_down, norm_mix, w_in, b_gate, pool_maps, pool_scale, w_pool_proj,
                         q_latent_norm, w_uq, kv_latent_norm, w_ukv, w_attn_proj, w_out, norm_ffn2, ffn2_up,
                         ffn2_down, final_norm)))
    m = dict(zip(order, (m_norm_ffn1, m_ffn1_up, m_ffn1_down, m_norm_mix, m_w_in, m_b_gate, m_pool_maps, m_pool_scale,
                         m_w_pool_proj, m_q_latent_norm, m_w_uq, m_kv_latent_norm, m_w_ukv, m_w_attn_proj, m_w_out,
                         m_norm_ffn2, m_ffn2_up, m_ffn2_down, m_final_norm)))
    v = dict(zip(order, (v_norm_ffn1, v_ffn1_up, v_ffn1_down, v_norm_mix, v_w_in, v_b_gate, v_pool_maps, v_pool_scale,
                         v_w_pool_proj, v_q_latent_norm, v_w_uq, v_kv_latent_norm, v_w_ukv, v_w_attn_proj, v_w_out,
                         v_norm_ffn2, v_ffn2_up, v_ffn2_down, v_final_norm)))
    L = norm_ffn1.shape[0]
    B, S, D = x.shape
    T = B * S

    def turned(a, n):
        return a.transpose(0, 2, 1) if n in TRANSPOSED else a

    wk, mk, vk = ({n: turned(d[n], n) for n in order} for d in (w, m, v))
    packed_shapes = [wk[n].shape[1:] for n in PACKED]
    my_slot = 4 * lax.axis_index("x") + 2 * lax.axis_index("y") + lax.axis_index("c")
    me = jnp.stack([my_slot]).astype(jnp.int32)

    def weight_blocks(l, names, token):
        zero = token[0, 0].astype(BF16)
        blocks = [wk[n][l].astype(BF16) + zero for n in names if n not in PACKED]
        if any(n in PACKED for n in names):
            blocks.append(_pack([wk[n][l].astype(BF16) + zero for n in PACKED], 0))
        return blocks

    def kernel_weights(names, lands):
        direct = [n for n in names if n not in PACKED]
        stacked = dict(zip(direct, lands))
        if len(lands) > len(direct):
            stacked.update(zip(PACKED, _unpack(lands[-1], packed_shapes, 1)))
        return _kernel_weights(stacked)

    def gather_start(l, names, token, tag):
        blocks = weight_blocks(l, names, token)
        lands = [lax.empty((N_DEV, *b.shape), b.dtype) for b in blocks]
        send_sems, recv_sems, blocks, lands, token = _exchange_start(
            blocks, lands, _gather_plan, GATHER_PEERS * len(blocks), name=f"gather_start_{tag}")
        return (send_sems, recv_sems, blocks, lands, tag), token

    def gather_wait(state, after):
        send_sems, recv_sems, blocks, lands, tag = state
        return _exchange_wait(send_sems, recv_sems, blocks, lands, _gather_plan, after, name=f"gather_wait_{tag}")[1]

    layer_part = FFN1_PART + MIXER_PART + FFN2_PART
    tabs = _rope_tables(positions.reshape(T))
    xs = x.reshape(T, D)
    h = _rms_fwd(xs, w["norm_ffn1"][0], name="first_norm")
    full, saved = [], []

    p = {n: w[n][0] for n in SMALL}
    blocks = weight_blocks(0, FFN1_PART, jnp.zeros((8, LANE), F32))
    lands = _exchange(blocks, [(N_DEV, *b.shape) for b in blocks], _gather_plan, GATHER_PEERS, name="gather_first")
    lands, token = _gather_forward(lands, name="gather_forward")
    w0 = kernel_weights(FFN1_PART, lands)
    state, token = gather_start(0, MIXER_PART, token, "0_mix")
    xs, h, s1 = _ffn_fwd(xs, h, w0["ffn1"], "ffn1", p["norm_mix"], dep=token)
    lands, token = _gather_forward(gather_wait(state, xs), name="gather_forward")
    w0.update(kernel_weights(MIXER_PART, lands))
    state, token = gather_start(0, FFN2_PART, token, "0_ffn2")
    if L > 1:
        next_state, token = gather_start(1, layer_part, token, "1")
    xs, h, s2 = _mixer_fwd(xs, h, p, w0, tabs, S, p["norm_ffn2"], dep=token)
    lands, token = _gather_forward(gather_wait(state, xs), name="gather_forward")
    w0.update(kernel_weights(FFN2_PART, lands))
    xs, h, s3 = _ffn_fwd(xs, h, w0["ffn2"], "ffn2", w["norm_ffn1"][1] if L > 1 else None, dep=token)
    if L > 1:
        lands, token = _gather_forward(gather_wait(next_state, xs), name="gather_forward")
    full.append(w0)
    saved.append((s1, s2, s3))

    for l in range(1, L):
        full.append(kernel_weights(layer_part, lands))
        more = l + 1 < L
        p = {n: w[n][l] for n in SMALL}
        if more:
            state, token = gather_start(l + 1, layer_part, token, f"{l + 1}")
        xs, h, s1 = _ffn_fwd(xs, h, full[l]["ffn1"], "ffn1", p["norm_mix"], dep=token if more else None)
        xs, h, s2 = _mixer_fwd(xs, h, p, full[l], tabs, S, p["norm_ffn2"])
        if more:
            lands = gather_wait(state, xs)
            send_sems, recv_sems, _, lands, token = _exchange_start(
                [], lands, _forward_plan, FORWARD_COPIES * len(lands), name=f"forward_start_{l + 1}")
        xs, h, s3 = _ffn_fwd(xs, h, full[l]["ffn2"], "ffn2", w["norm_ffn1"][l + 1] if more else None,
                             dep=token if more else None)
        if more:
            _, lands = _exchange_wait(send_sems, recv_sems, [], lands, _forward_plan, xs, name=f"forward_wait_{l + 1}")
        saved.append((s1, s2, s3))
    dx, dfinal, loss = _loss_head(xs, final_norm, loss_target.reshape(T, D), name="loss_head")

    big_grads = {n: [None] * L for n in BIG}
    small_grads_of = [None] * L
    pending = None

    def scatter_start(names, stacked, tag):
        srcs = [stacked[n] for n in names if n not in PACKED]
        if any(n in PACKED for n in names):
            srcs.append(_pack([stacked[n] for n in PACKED], 1))
        lands = [lax.empty((SCATTER_PEERS, *s.shape[1:]), s.dtype) for s in srcs]
        send_sems, recv_sems, srcs, lands, token = _exchange_start(
            srcs, lands, _scatter_plan, SCATTER_PEERS * len(srcs), name=f"scatter_start_{tag}")
        return (names, send_sems, recv_sems, srcs, lands, tag), token

    def scatter_finish(state, after, l):
        names, send_sems, recv_sems, srcs, lands, tag = state
        srcs, got = _exchange_wait(send_sems, recv_sems, srcs, lands, _scatter_plan, after, name=f"scatter_wait_{tag}")
        sums = [_scatter_sum(s, g, me, name="scatter_sum") for s, g in zip(srcs, got)]
        direct = [n for n in names if n not in PACKED]
        for n, g in zip(direct, sums):
            big_grads[n][l] = g
        if len(sums) > len(direct):
            for n, g in zip(PACKED, _unpack(sums[-1], packed_shapes, 0)):
                big_grads[n][l] = g

    dep = None
    for l in reversed(range(L)):
        p = {n: w[n][l] for n in SMALL}
        s1, s2, s3 = saved[l]
        small_g = {}
        dx, small_g["norm_ffn2"], dup_t, dwd = _ffn_bwd(dx, p["norm_ffn2"], full[l]["ffn2"], s3, "ffn2", dep=dep)
        if pending is not None:
            scatter_finish(pending[0], dx, pending[1])
        stacked = {"ffn2_up": _split_rows(dup_t), "ffn2_down": _split_rows(dwd)}
        state, dep = scatter_start(("ffn2_up", "ffn2_down"), stacked, f"ffn2_{l}")
        pending = (state, l)

        dx, gm = _mixer_bwd(dx, p, full[l], tabs, s2, S, dep=dep)
        scatter_finish(pending[0], dx, pending[1])
        names = ("w_in", "w_attn_proj", "w_out") + PACKED
        state, dep = scatter_start(names, _mixer_grads_stacked(gm), f"mix_{l}")
        pending = (state, l)
        small_g.update({n: gm[n] for n in SMALL if n in gm})

        dx, small_g["norm_ffn1"], dup_t, dwd = _ffn_bwd(dx, p["norm_ffn1"], full[l]["ffn1"], s1, "ffn1", dep=dep)
        scatter_finish(pending[0], dx, pending[1])
        stacked = {"ffn1_up": _split_rows(dup_t), "ffn1_down": _split_rows(dwd)}
        state, dep = scatter_start(("ffn1_up", "ffn1_down"), stacked, f"ffn1_{l}")
        pending = (state, l)
        small_grads_of[l] = small_g
    grad_x = dx.reshape(B, S, D)

    small_parts = [small_grads_of[l][n] for l in range(L) for n in SMALL] + [dfinal, loss[0, :1]]
    small_shapes = [p.shape for p in small_parts]
    vec = _pack([jnp.concatenate([p.reshape(-1) for p in small_parts])], 0)
    small_send, small_recv, vec_thru, small_land, small_token = _exchange_start(
        [vec], [lax.empty((N_DEV, *vec.shape), F32)], _gather_all_plan, SCATTER_PEERS, name="small_start")

    gk, grad, delta, new_m, new_v = {}, {}, {}, {}, {}

    def update(n, dep=None):
        wn, gn, mn, vn = (a.reshape(1, -1) if a.ndim == 1 else a for a in (wk[n], gk[n], mk[n], vk[n]))
        d, nm, nv = _adamw(wn, gn, mn, vn, name="adamw_" + n, dep=dep)
        grad[n] = turned(gk[n], n)
        delta[n], new_m[n], new_v[n] = (turned(a.reshape(wk[n].shape), n) for a in (d, nm, nv))

    last_block = ("ffn1_up", "ffn1_down")
    deps = [dep, small_token]
    for n in BIG:
        if n not in last_block:
            gk[n] = jnp.stack(big_grads[n])
            update(n, deps.pop(0) if deps else None)
    scatter_finish(pending[0], new_v["ffn2_down"], pending[1])
    for n in last_block:
        gk[n] = jnp.stack(big_grads[n])
        update(n)

    vec_thru, small_land = _exchange_wait(small_send, small_recv, vec_thru, small_land, _gather_all_plan,
                                          new_v["ffn1_down"], name="small_wait")
    parts = lax.dynamic_update_index_in_dim(small_land[0], vec_thru[0], my_slot, 0)
    flat = _sum_devices(parts, name="sum_small").reshape(-1)
    small_grads, at = [], 0
    for shape in small_shapes:
        size = int(np.prod(shape))
        small_grads.append(lax.slice_in_dim(flat, at, at + size).reshape(shape))
        at += size
    loss_total = small_grads[-1].reshape(())
    for i, n in enumerate(SMALL):
        gk[n] = jnp.stack([small_grads[l * len(SMALL) + i] for l in range(L)]).reshape(w[n].shape)
        update(n)
    gk["final_norm"] = small_grads[-2].reshape(final_norm.shape)
    update("final_norm")
    return (loss_total, grad_x, *[grad[n] for n in order], *[delta[n] for n in order],
            *[new_m[n] for n in order], *[new_v[n] for n in order])
```

```python
import functools

import numpy as np
import jax
import jax.numpy as jnp
from jax import lax
from jax.experimental import pallas as pl
from jax.experimental.pallas import tpu as pltpu

F32 = jnp.float32
BF16 = jnp.bfloat16

NORM_EPS = 1e-6
ROPE_THETA = 10000.0
QK_NOPE = 128
QK_ROPE = 64
V_DIM = 128
HEAD_W = 256
POOL_WINDOWS = (2, 4, 8, 16)
POOL_G = 128
POOL_DIM = 512
LANE = 128
ATTN_SCALE = float((QK_NOPE + QK_ROPE) ** -0.5)
ATTN_SCALE_LOG2 = ATTN_SCALE * float(np.log2(np.e))
MASK_VALUE = -1e30
ATTN_TILE = 512

ADAM_LR = 0.001
ADAM_B1 = 0.9
ADAM_B2 = 0.999
ADAM_EPS = 1e-08
ADAM_WD = 0.01
ADAM_STEP = 10

N_DEV = 8
VMEM_LIMIT = 52 * 1024 * 1024

MESH = pl.DeviceIdType.MESH
ANY = pl.BlockSpec(memory_space=pl.ANY)


def _tile(dim, target, align=LANE):
    if dim <= target:
        return dim
    t = (target // align) * align
    while t >= align:
        if dim % t == 0:
            return t
        t -= align
    return dim


def _params(*sem):
    return pltpu.CompilerParams(dimension_semantics=sem, vmem_limit_bytes=VMEM_LIMIT)


def _rstd(x):
    return lax.rsqrt(jnp.mean(x * x, axis=-1, keepdims=True) + NORM_EPS)


def _mm(a, b, *, name, ta=False, tb=False, out_dtype=F32, alpha=1.0, tm=512, tn=1024, tk=1024):
    if ta:
        K, M = a.shape
    else:
        M, K = a.shape
    if tb:
        N, K2 = b.shape
    else:
        K2, N = b.shape
    assert K == K2, (a.shape, b.shape, ta, tb)
    tm, tn, tk = _tile(M, tm), _tile(N, tn), _tile(K, tk)
    nk = K // tk
    dims = (((0 if ta else 1,), (1 if tb else 0,)), ((), ()))

    def body(a_ref, b_ref, o_ref, *scratch):
        acc_ref = scratch[0] if nk > 1 else None
        part = lax.dot_general(a_ref[...].astype(BF16), b_ref[...].astype(BF16), dims,
                               preferred_element_type=F32)

        def finish(acc):
            o_ref[...] = (acc * alpha if alpha != 1.0 else acc).astype(out_dtype)

        if nk == 1:
            finish(part)
        else:
            k = pl.program_id(2)

            @pl.when(k == 0)
            def _():
                acc_ref[...] = part

            @pl.when(k > 0)
            def _():
                acc_ref[...] += part

            @pl.when(k == nk - 1)
            def _():
                finish(acc_ref[...])

    a_spec = pl.BlockSpec((tk, tm), lambda i, j, k: (k, i)) if ta else pl.BlockSpec((tm, tk), lambda i, j, k: (i, k))
    b_spec = pl.BlockSpec((tn, tk), lambda i, j, k: (j, k)) if tb else pl.BlockSpec((tk, tn), lambda i, j, k: (k, j))
    return pl.pallas_call(
        body, name=name,
        out_shape=jax.ShapeDtypeStruct((M, N), out_dtype),
        grid=(M // tm, N // tn, nk),
        in_specs=[a_spec, b_spec],
        out_specs=pl.BlockSpec((tm, tn), lambda i, j, k: (i, j)),
        scratch_shapes=[pltpu.VMEM((tm, tn), F32)] if nk > 1 else [],
        compiler_params=_params("parallel", "parallel", "arbitrary"),
    )(a, b)


def _dw_multi(pairs, *, name, tk=512):
    n = len(pairs)
    T = pairs[0][0].shape[0]
    tk = _tile(T, tk, 16)
    nk = T // tk
    shapes = [(a.shape[1], b.shape[1]) for a, b in pairs]

    def body(*refs):
        ins, outs, accs = refs[:2 * n], refs[2 * n:3 * n], refs[3 * n:]
        k = pl.program_id(0)
        parts = [lax.dot_general(ins[2 * i][...].astype(BF16), ins[2 * i + 1][...].astype(BF16), _TN,
                                 preferred_element_type=F32) for i in range(n)]

        @pl.when(k == 0)
        def _():
            for acc, part in zip(accs, parts):
                acc[...] = part

        @pl.when(k > 0)
        def _():
            for acc, part in zip(accs, parts):
                acc[...] += part

        @pl.when(k == nk - 1)
        def _():
            for out, acc in zip(outs, accs):
                out[...] = acc[...].astype(BF16)

    return pl.pallas_call(
        body, name=name,
        out_shape=tuple(jax.ShapeDtypeStruct(s, BF16) for s in shapes),
        grid=(nk,),
        in_specs=[pl.BlockSpec((tk, x.shape[1]), lambda k: (k, 0)) for pair in pairs for x in pair],
        out_specs=tuple(pl.BlockSpec(s, lambda k: (0, 0)) for s in shapes),
        scratch_shapes=[pltpu.VMEM(s, F32) for s in shapes],
        compiler_params=_params("arbitrary"),
    )(*[x for pair in pairs for x in pair])


def _rms_fwd(x, g, *, name):
    T, D = x.shape
    tm = _tile(T, 512, 16)

    def body(x_ref, g_ref, h_ref):
        x = x_ref[...]
        h_ref[...] = (x * _rstd(x) * g_ref[...]).astype(BF16)

    return pl.pallas_call(
        body, name=name,
        out_shape=jax.ShapeDtypeStruct((T, D), BF16),
        grid=(T // tm,),
        in_specs=[pl.BlockSpec((tm, D), lambda i: (i, 0)), pl.BlockSpec((1, D), lambda i: (0, 0))],
        out_specs=pl.BlockSpec((tm, D), lambda i: (i, 0)),
        compiler_params=_params("parallel"),
    )(x, g.reshape(1, D))


def _loss_head(x, g, target, *, name):
    T, D = x.shape
    tm = _tile(T, 512, 16)

    def body(x_ref, g_ref, t_ref, dx_ref, dg_ref, loss_ref):
        x = x_ref[...]
        gain = g_ref[...]
        r = _rstd(x)
        xhat = x * r
        err = xhat * gain - t_ref[...]
        dy = err * (1.0 / D)
        dxh = dy * gain
        dx_ref[...] = r * (dxh - xhat * jnp.mean(dxh * xhat, axis=-1, keepdims=True))
        dg_part = jnp.sum(dy * xhat, axis=0, keepdims=True)
        loss_part = jnp.full((1, LANE), 0.5 / D, F32) * jnp.sum(err * err)

        @pl.when(pl.program_id(0) == 0)
        def _():
            dg_ref[...] = dg_part
            loss_ref[...] = loss_part

        @pl.when(pl.program_id(0) > 0)
        def _():
            dg_ref[...] += dg_part
            loss_ref[...] += loss_part

    row = pl.BlockSpec((tm, D), lambda i: (i, 0))
    vec = pl.BlockSpec((1, D), lambda i: (0, 0))
    return pl.pallas_call(
        body, name=name,
        out_shape=(jax.ShapeDtypeStruct((T, D), F32), jax.ShapeDtypeStruct((1, D), F32),
                   jax.ShapeDtypeStruct((1, LANE), F32)),
        grid=(T // tm,),
        in_specs=[row, vec, row],
        out_specs=(row, vec, pl.BlockSpec((1, LANE), lambda i: (0, 0))),
        compiler_params=_params("arbitrary"),
    )(x, g.reshape(1, D), target)


def _ffn_fwd_core(x, h, w_up_t, wd, next_gain, *, alpha, name, dep=None):
    T, D = x.shape
    F = wd.shape[0]
    tm = _tile(T, 256, 16)
    has_norm = next_gain is not None

    def body(x_ref, h_ref, wg_ref, wu_ref, wd_ref, *rest):
        outs = rest[len(rest) - (5 if has_norm else 4):]
        gate_ref, up_ref, a_ref, xn_ref = outs[:4]
        h = h_ref[...]
        gate = lax.dot_general(h, wg_ref[...], _NT, preferred_element_type=F32)
        up = lax.dot_general(h, wu_ref[...], _NT, preferred_element_type=F32)
        a = (gate * jax.nn.sigmoid(gate) * up).astype(BF16)
        gate_ref[...] = gate.astype(BF16)
        up_ref[...] = up.astype(BF16)
        a_ref[...] = a
        xn = x_ref[...] + alpha * jnp.dot(a, wd_ref[...], preferred_element_type=F32)
        xn_ref[...] = xn
        if has_norm:
            outs[4][...] = (xn * _rstd(xn) * rest[0][...]).astype(BF16)

    once = pl.Buffered(1)
    row_d = pl.BlockSpec((tm, D), lambda i: (i, 0))
    row_f = pl.BlockSpec((tm, F), lambda i: (i, 0))
    vec = pl.BlockSpec((1, D), lambda i: (0, 0))
    act = jax.ShapeDtypeStruct((T, F), BF16)
    operands = [x, h, w_up_t, w_up_t, wd] + ([next_gain.reshape(1, D)] if has_norm else [])
    out = pl.pallas_call(
        body, name=name,
        out_shape=(act, act, act, jax.ShapeDtypeStruct((T, D), F32)) + ((jax.ShapeDtypeStruct((T, D), BF16),) if has_norm else ()),
        grid=(T // tm,),
        in_specs=[row_d, row_d,
                  pl.BlockSpec((F, D), lambda i: (0, 0), pipeline_mode=once),
                  pl.BlockSpec((F, D), lambda i: (1, 0), pipeline_mode=once),
                  pl.BlockSpec((F, D), lambda i: (0, 0), pipeline_mode=once)] + ([vec] if has_norm else []) + _dep_spec(dep),
        out_specs=(row_f, row_f, row_f, row_d) + ((row_d,) if has_norm else ()),
        compiler_params=_params("parallel"),
    )(*operands, *([] if dep is None else [dep]))
    return out if has_norm else (*out, None)


def _ffn_bwd_core(dxo, wd, w_up_t, gate, up, x, gain, *, alpha, name, dep=None):
    T, D = dxo.shape
    F = wd.shape[0]
    tm = _tile(T, 256, 16)

    def body(dxo_ref, wd_ref, wg_ref, wu_ref, gate_ref, up_ref, x_ref, g_ref, *rest):
        dgate_ref, dup_ref, dx_ref, dg_ref = rest[-4:]
        dxo = dxo_ref[...]
        da = lax.dot_general(dxo.astype(BF16), wd_ref[...], _NT, preferred_element_type=F32) * alpha
        gate = gate_ref[...].astype(F32)
        up = up_ref[...].astype(F32)
        sig = jax.nn.sigmoid(gate)
        dgate = (da * up * (sig * (1.0 + gate * (1.0 - sig)))).astype(BF16)
        dup = (da * (gate * sig)).astype(BF16)
        dgate_ref[...] = dgate
        dup_ref[...] = dup
        dh = (jnp.dot(dgate, wg_ref[...], preferred_element_type=F32)
              + jnp.dot(dup, wu_ref[...], preferred_element_type=F32))
        x = x_ref[...]
        r = _rstd(x)
        xhat = x * r
        dxh = dh * g_ref[...]
        dx_ref[...] = dxo + r * (dxh - xhat * jnp.mean(dxh * xhat, axis=-1, keepdims=True))
        part = jnp.sum(dh * xhat, axis=0, keepdims=True)

        @pl.when(pl.program_id(0) == 0)
        def _():
            dg_ref[...] = part

        @pl.when(pl.program_id(0) > 0)
        def _():
            dg_ref[...] += part

    once = pl.Buffered(1)
    row_d = pl.BlockSpec((tm, D), lambda i: (i, 0))
    row_f = pl.BlockSpec((tm, F), lambda i: (i, 0))
    vec = pl.BlockSpec((1, D), lambda i: (0, 0))
    act = jax.ShapeDtypeStruct((T, F), BF16)
    return pl.pallas_call(
        body, name=name,
        out_shape=(act, act, jax.ShapeDtypeStruct((T, D), F32), jax.ShapeDtypeStruct((1, D), F32)),
        grid=(T // tm,),
        in_specs=[row_d,
                  pl.BlockSpec((F, D), lambda i: (0, 0), pipeline_mode=once),
                  pl.BlockSpec((F, D), lambda i: (0, 0), pipeline_mode=once),
                  pl.BlockSpec((F, D), lambda i: (1, 0), pipeline_mode=once),
                  row_f, row_f, row_d, vec] + _dep_spec(dep),
        out_specs=(row_f, row_f, row_d, vec),
        compiler_params=_params("arbitrary"),
    )(dxo, wd, w_up_t, w_up_t, gate, up, x, gain.reshape(1, D), *([] if dep is None else [dep]))


def _ffn_dw_up(dgate, dup, h, *, name):
    T, F = dgate.shape
    D = h.shape[1]
    tm, tk = _tile(F, 1408), _tile(T, 1024, 16)
    nf, nk = F // tm, T // tk

    def body(dgate_ref, dup_ref, h_ref, o_ref, acc_ref):
        i, k = pl.program_id(0), pl.program_id(1)

        def accumulate(part):
            @pl.when(k == 0)
            def _():
                acc_ref[...] = part

            @pl.when(k > 0)
            def _():
                acc_ref[...] += part

        @pl.when(i < nf)
        def _():
            accumulate(lax.dot_general(dgate_ref[...], h_ref[...], _TN, preferred_element_type=F32))

        @pl.when(i >= nf)
        def _():
            accumulate(lax.dot_general(dup_ref[...], h_ref[...], _TN, preferred_element_type=F32))

        @pl.when(k == nk - 1)
        def _():
            o_ref[...] = acc_ref[...].astype(BF16)

    return pl.pallas_call(
        body, name=name,
        out_shape=jax.ShapeDtypeStruct((2 * F, D), BF16),
        grid=(2 * nf, nk),
        in_specs=[pl.BlockSpec((tk, tm), lambda i, k: (jnp.where(i < nf, k, nk - 1), jnp.minimum(i, nf - 1))),
                  pl.BlockSpec((tk, tm), lambda i, k: (jnp.where(i < nf, 0, k), jnp.maximum(i - nf, 0))),
                  pl.BlockSpec((tk, D), lambda i, k: (k, 0))],
        out_specs=pl.BlockSpec((tm, D), lambda i, k: (i, 0)),
        scratch_shapes=[pltpu.VMEM((tm, D), F32)],
        compiler_params=_params("parallel", "arbitrary"),
    )(dgate, dup, h)


def _dep_spec(dep):
    return [] if dep is None else [pl.BlockSpec(dep.shape, lambda *_: (0,) * dep.ndim)]


def _rope_tables(positions):
    half = QK_ROPE // 2
    inv_freq = ROPE_THETA ** (-jnp.arange(0, QK_ROPE, 2, dtype=F32) / QK_ROPE)
    ang = positions.astype(F32)[:, None] * inv_freq
    cos, sin = jnp.cos(ang), jnp.sin(ang)
    z = jnp.zeros_like(cos)
    zz = jnp.zeros((positions.shape[0], LANE - QK_ROPE), F32)
    c = jnp.concatenate([cos, cos, zz], axis=1)
    sa = jnp.concatenate([z, sin, zz], axis=1)
    sb = jnp.concatenate([-sin, z, zz], axis=1)
    return c, sa, sb


def _rotate(seg, c, sa, sb, sign):
    half = QK_ROPE // 2
    mix = pltpu.roll(seg, half, 1) * sa + pltpu.roll(seg, LANE - half, 1) * sb
    return seg * c + mix if sign > 0 else seg * c - mix


def _mixer_in(h, wa, wuq, wukv, gq, gkv, tabs, *, name, dep=None):
    T, D = h.shape
    HQ, QL = wuq.shape
    KVL = wukv.shape[0]
    H = HQ // HEAD_W
    o_q, o_kv, o_kr = POOL_DIM, POOL_DIM + QL, POOL_DIM + QL + KVL
    PA = o_kr + LANE
    assert wa.shape[0] >= PA
    tm = _tile(T, 512, 16)

    def body(h_ref, wa_ref, wuq_ref, wukv_ref, gq_ref, gkv_ref, c_ref, sa_ref, sb_ref, *rest):
        xp_ref, ql_ref, kvl_ref, qn_ref, kvn_ref, q_ref, kv_ref, kr_ref = rest[-8:]
        proj = lax.dot_general(h_ref[...], wa_ref[...], _NT, preferred_element_type=F32)
        xp_ref[...] = proj[:, :POOL_DIM]
        ql = proj[:, o_q:o_kv]
        kvl = proj[:, o_kv:o_kr]
        ql_ref[...] = ql
        kvl_ref[...] = kvl
        qn = (ql * _rstd(ql) * gq_ref[...]).astype(BF16)
        kvn = (kvl * _rstd(kvl) * gkv_ref[...]).astype(BF16)
        qn_ref[...] = qn
        kvn_ref[...] = kvn
        c, sa, sb = c_ref[...], sa_ref[...], sb_ref[...]
        q = lax.dot_general(qn, wuq_ref[...], _NT, preferred_element_type=F32)
        for hh in range(H):
            base = hh * HEAD_W
            q_ref[:, base:base + QK_NOPE] = q[:, base:base + QK_NOPE].astype(BF16)
            q_ref[:, base + QK_NOPE:base + HEAD_W] = _rotate(
                q[:, base + QK_NOPE:base + HEAD_W], c, sa, sb, 1).astype(BF16)
        kv_ref[...] = jnp.dot(kvn, wukv_ref[...], preferred_element_type=F32).astype(BF16)
        kr_ref[...] = _rotate(proj[:, o_kr:o_kr + LANE], c, sa, sb, 1).astype(BF16)

    def row(w):
        return pl.BlockSpec((tm, w), lambda i: (i, 0))

    def whole(arr):
        return pl.BlockSpec(arr.shape, lambda i: (0,) * arr.ndim)

    gq2, gkv2 = gq.reshape(1, QL), gkv.reshape(1, KVL)
    outs = [(POOL_DIM, F32), (QL, F32), (KVL, F32), (QL, BF16), (KVL, BF16), (HQ, BF16), (HQ, BF16), (LANE, BF16)]
    return pl.pallas_call(
        body, name=name,
        out_shape=tuple(jax.ShapeDtypeStruct((T, w), dt) for w, dt in outs),
        grid=(T // tm,),
        in_specs=[row(D), pl.BlockSpec((PA, D), lambda i: (0, 0)), whole(wuq), whole(wukv), whole(gq2), whole(gkv2),
                  row(LANE), row(LANE), row(LANE)] + _dep_spec(dep),
        out_specs=tuple(row(w) for w, _ in outs),
        compiler_params=_params("parallel"),
    )(h, wa, wuq, wukv, gq2, gkv2, *tabs, *([] if dep is None else [dep]))


def _mixer_in_bwd(dq, dkv, dkr, ql, kvl, dxp, dgl, x, dxo, win_t, wgate_t, wuq, wukv, g_mix, gq, gkv, tabs, *, name):
    T, HQ = dq.shape
    D = x.shape[1]
    QL, KVL = wuq.shape[1], wukv.shape[0]
    H = HQ // HEAD_W
    PA = POOL_DIM + QL + KVL + LANE
    o_q, o_kv, o_kr = POOL_DIM, POOL_DIM + QL, POOL_DIM + QL + KVL
    tm = _tile(T, 256, 16)

    def norm_bwd(lat, gain, dn):
        r = _rstd(lat)
        xhat = lat * r
        dxh = dn * gain
        dlat = r * (dxh - xhat * jnp.mean(dxh * xhat, axis=-1, keepdims=True))
        return dlat, jnp.sum(dn * xhat, axis=0, keepdims=True)

    def body(dq_ref, dkv_ref, dkr_ref, ql_ref, kvl_ref, dxp_ref, dgl_ref, x_ref, dxo_ref, win_ref, wgate_ref,
             wuq_ref, wukv_ref, gmix_ref, gq_ref, gkv_ref, c_ref, sa_ref, sb_ref,
             dproj_ref, dqp_ref, dgq_ref, dgkv_ref, dx_ref, dgmix_ref):
        c, sa, sb = c_ref[...], sa_ref[...], sb_ref[...]
        for hh in range(H):
            base = hh * HEAD_W
            dqp_ref[:, base:base + QK_NOPE] = dq_ref[:, base:base + QK_NOPE]
            dqp_ref[:, base + QK_NOPE:base + HEAD_W] = _rotate(
                dq_ref[:, base + QK_NOPE:base + HEAD_W].astype(F32), c, sa, sb, -1).astype(BF16)
        dqn = jnp.dot(dqp_ref[...], wuq_ref[...], preferred_element_type=F32)
        dkvn = lax.dot_general(dkv_ref[...], wukv_ref[...], _NT, preferred_element_type=F32)
        dql, dgq = norm_bwd(ql_ref[...], gq_ref[...], dqn)
        dkvl, dgkv = norm_bwd(kvl_ref[...], gkv_ref[...], dkvn)
        dproj_ref[:, :POOL_DIM] = dxp_ref[...].astype(BF16)
        dproj_ref[:, o_q:o_kv] = dql.astype(BF16)
        dproj_ref[:, o_kv:o_kr] = dkvl.astype(BF16)
        dproj_ref[:, o_kr:PA] = _rotate(dkr_ref[...], c, sa, sb, -1).astype(BF16)

        dh = (jnp.dot(dproj_ref[...], win_ref[...], preferred_element_type=F32)
              + jnp.dot(dgl_ref[...], wgate_ref[...], preferred_element_type=F32))
        x = x_ref[...]
        r = _rstd(x)
        xhat = x * r
        dxh = dh * gmix_ref[...]
        dx_ref[...] = dxo_ref[...] + r * (dxh - xhat * jnp.mean(dxh * xhat, axis=-1, keepdims=True))
        dgmix = jnp.sum(dh * xhat, axis=0, keepdims=True)

        @pl.when(pl.program_id(0) == 0)
        def _():
            dgq_ref[...] = dgq
            dgkv_ref[...] = dgkv
            dgmix_ref[...] = dgmix

        @pl.when(pl.program_id(0) > 0)
        def _():
            dgq_ref[...] += dgq
            dgkv_ref[...] += dgkv
            dgmix_ref[...] += dgmix

    def row(w):
        return pl.BlockSpec((tm, w), lambda i: (i, 0))

    def resident(arr, rows=None):
        shape = arr.shape if rows is None else (rows, arr.shape[1])
        return pl.BlockSpec(shape, lambda i: (0, 0), pipeline_mode=pl.Buffered(1))

    gmix2, gq2, gkv2 = g_mix.reshape(1, D), gq.reshape(1, QL), gkv.reshape(1, KVL)
    vec = pl.BlockSpec((1, D), lambda i: (0, 0))
    vq, vkv = pl.BlockSpec((1, QL), lambda i: (0, 0)), pl.BlockSpec((1, KVL), lambda i: (0, 0))
    return pl.pallas_call(
        body, name=name,
        out_shape=(jax.ShapeDtypeStruct((T, PA), BF16), jax.ShapeDtypeStruct((T, HQ), BF16),
                   jax.ShapeDtypeStruct((1, QL), F32), jax.ShapeDtypeStruct((1, KVL), F32),
                   jax.ShapeDtypeStruct((T, D), F32), jax.ShapeDtypeStruct((1, D), F32)),
        grid=(T // tm,),
        in_specs=[row(HQ), row(HQ), row(LANE), row(QL), row(KVL), row(POOL_DIM), row(2 * D), row(D), row(D),
                  resident(win_t, PA), resident(wgate_t), resident(wuq), resident(wukv),
                  vec, vq, vkv, row(LANE), row(LANE), row(LANE)],
        out_specs=(row(PA), row(HQ), vq, vkv, row(D), vec),
        compiler_params=_params("arbitrary"),
    )(dq, dkv, dkr, ql, kvl, dxp, dgl, x, dxo, win_t, wgate_t, wuq, wukv, gmix2, gq2, gkv2, *tabs)


def _pool_groups(x_of, S):
    row = lax.broadcasted_iota(jnp.int32, (S, POOL_G), 0)
    for g, w in enumerate(POOL_WINDOWS):
        x = x_of(g)
        s = x
        d = 1
        while d < w:
            s = s + jnp.where(row >= d, pltpu.roll(s, d, 0), 0.0)
            d *= 2
        cnt = jnp.minimum(row + 1, w).astype(F32)
        yield g, w, x, s / cnt - x, cnt, row


def _pool_fwd(xp, maps, scale, *, S, name):
    T = xp.shape[0]

    def body(xp_ref, maps_ref, scale_ref, ms_ref):
        for g, _, _, pooled, _, _ in _pool_groups(lambda g: xp_ref[:, g * POOL_G:(g + 1) * POOL_G], S):
            mixed = jnp.dot(pooled.astype(BF16), maps_ref[g].astype(BF16), preferred_element_type=F32)
            ms_ref[:, g * POOL_G:(g + 1) * POOL_G] = (mixed * scale_ref[:, g * POOL_G:(g + 1) * POOL_G]).astype(BF16)

    return pl.pallas_call(
        body, name=name,
        out_shape=jax.ShapeDtypeStruct((T, POOL_DIM), BF16),
        grid=(T // S,),
        in_specs=[pl.BlockSpec((S, POOL_DIM), lambda b: (b, 0)),
                  pl.BlockSpec(maps.shape, lambda b: (0, 0, 0)),
                  pl.BlockSpec((1, POOL_DIM), lambda b: (0, 0))],
        out_specs=pl.BlockSpec((S, POOL_DIM), lambda b: (b, 0)),
        compiler_params=_params("parallel"),
    )(xp, maps, scale.reshape(1, POOL_DIM))


def _pool_bwd(xp, dms, maps, scale, *, S, name):
    T = xp.shape[0]

    def body(xp_ref, dms_ref, maps_ref, scale_ref, dxp_ref, dmaps_ref, dscale_ref):
        first = pl.program_id(0) == 0
        for g, w, _, pooled, cnt, row in _pool_groups(lambda g: xp_ref[:, g * POOL_G:(g + 1) * POOL_G], S):
            cols = slice(g * POOL_G, (g + 1) * POOL_G)
            pooled_b = pooled.astype(BF16)
            maps_b = maps_ref[g].astype(BF16)
            mixed = jnp.dot(pooled_b, maps_b, preferred_element_type=F32)
            dms = dms_ref[:, cols]
            dscale = jnp.sum(dms * mixed, axis=0, keepdims=True)
            dmixed = (dms * scale_ref[:, cols]).astype(BF16)
            dmaps = lax.dot_general(pooled_b, dmixed, (((0,), (0,)), ((), ())), preferred_element_type=F32)
            dpooled = lax.dot_general(dmixed, maps_b, (((1,), (1,)), ((), ())), preferred_element_type=F32)
            z = dpooled / cnt
            d = 1
            while d < w:
                z = z + jnp.where(row < S - d, pltpu.roll(z, S - d, 0), 0.0)
                d *= 2
            dxp_ref[:, cols] = z - dpooled

            @pl.when(first)
            def _():
                dmaps_ref[g] = dmaps
                dscale_ref[:, cols] = dscale

            @pl.when(jnp.logical_not(first))
            def _():
                dmaps_ref[g] += dmaps
                dscale_ref[:, cols] += dscale

    seq = pl.BlockSpec((S, POOL_DIM), lambda b: (b, 0))
    maps_spec = pl.BlockSpec(maps.shape, lambda b: (0, 0, 0))
    vec = pl.BlockSpec((1, POOL_DIM), lambda b: (0, 0))
    return pl.pallas_call(
        body, name=name,
        out_shape=(jax.ShapeDtypeStruct((T, POOL_DIM), F32), jax.ShapeDtypeStruct(maps.shape, F32),
                   jax.ShapeDtypeStruct((1, POOL_DIM), F32)),
        grid=(T // S,),
        in_specs=[seq, seq, maps_spec, vec],
        out_specs=(seq, maps_spec, vec),
        compiler_params=_params("arbitrary"),
    )(xp, dms, maps, scale.reshape(1, POOL_DIM))


def _causal_mask(s, t):
    r = lax.broadcasted_iota(jnp.int32, (t, t), 0)
    c = lax.broadcasted_iota(jnp.int32, (t, t), 1)
    return jnp.where(r >= c, s, MASK_VALUE)


_NT = (((1,), (1,)), ((), ()))
_TN = (((0,), (0,)), ((), ()))


def _attn_fwd(q, kv, kr, *, S, name):
    T, HQ = q.shape
    H = HQ // HEAD_W
    B = T // S
    t = _tile(S, ATTN_TILE)
    n = S // t

    def body(q_ref, k_ref, v_ref, kr_ref, o_ref, lse_ref, kcat, vcat):
        kcat[:, :QK_NOPE] = k_ref[...]
        kcat[:, QK_NOPE:] = kr_ref[...]
        vcat[:, :V_DIM] = v_ref[...]
        vcat[:, V_DIM:] = jnp.ones((S, HEAD_W - V_DIM), BF16)
        for i in range(n):
            rows = slice(i * t, (i + 1) * t)
            qt = q_ref[rows, :]
            m = jnp.full((t, 1), MASK_VALUE, F32)
            acc = jnp.zeros((t, HEAD_W), F32)
            for j in range(i + 1):
                cols = slice(j * t, (j + 1) * t)
                s = lax.dot_general(qt, kcat[cols, :], _NT, preferred_element_type=F32) * ATTN_SCALE_LOG2
                if j == i:
                    s = _causal_mask(s, t)
                m_new = jnp.maximum(m, jnp.max(s, axis=1, keepdims=True))
                p = jnp.exp2(s - m_new)
                acc = jnp.exp2(m - m_new) * acc + jnp.dot(p.astype(BF16), vcat[cols, :], preferred_element_type=F32)
                m = m_new
            l = acc[:, V_DIM:V_DIM + 1]
            o_ref[rows, :] = (acc[:, :V_DIM] / l).astype(BF16)
            lse_ref[rows, :] = jnp.broadcast_to(m + jnp.log2(l), (t, LANE))

    seq_h = pl.BlockSpec((S, LANE), lambda b, h: (b, h))
    return pl.pallas_call(
        body, name=name,
        out_shape=(jax.ShapeDtypeStruct((T, H * V_DIM), BF16), jax.ShapeDtypeStruct((T, H * LANE), F32)),
        grid=(B, H),
        in_specs=[pl.BlockSpec((S, HEAD_W), lambda b, h: (b, h)),
                  pl.BlockSpec((S, QK_NOPE), lambda b, h: (b, 2 * h)),
                  pl.BlockSpec((S, V_DIM), lambda b, h: (b, 2 * h + 1)),
                  pl.BlockSpec((S, LANE), lambda b, h: (b, 0))],
        out_specs=(seq_h, seq_h),
        scratch_shapes=[pltpu.VMEM((S, HEAD_W), BF16), pltpu.VMEM((S, HEAD_W), BF16)],
        compiler_params=_params("parallel", "parallel"),
    )(q, kv, kv, kr)


def _attn_bwd(q, kv, kr, o, do, lse, *, S, name):
    T, HQ = q.shape
    H = HQ // HEAD_W
    B = T // S
    t = _tile(S, ATTN_TILE)
    n = S // t

    def body(q_ref, k_ref, v_ref, kr_ref, o_ref, do_ref, lse_ref, dq_ref, dkv_ref, dkr_ref, kcat, dq_acc):
        @pl.when(pl.program_id(1) == 0)
        def _():
            dkr_ref[...] = jnp.zeros_like(dkr_ref)

        kcat[:, :QK_NOPE] = k_ref[...]
        kcat[:, QK_NOPE:] = kr_ref[...]
        delta = [jnp.sum(do_ref[i * t:(i + 1) * t, :].astype(F32) * o_ref[i * t:(i + 1) * t, :].astype(F32),
                         axis=1, keepdims=True) for i in range(n)]
        for j in range(n):
            cols = slice(j * t, (j + 1) * t)
            kc = kcat[cols, :]
            vt = v_ref[cols, :]
            dk = jnp.zeros((t, HEAD_W), F32)
            dv = jnp.zeros((t, V_DIM), F32)
            for i in range(j, n):
                rows = slice(i * t, (i + 1) * t)
                qt = q_ref[rows, :]
                dot_ = do_ref[rows, :]
                s = lax.dot_general(qt, kc, _NT, preferred_element_type=F32) * ATTN_SCALE_LOG2
                if i == j:
                    s = _causal_mask(s, t)
                p = jnp.exp2(s - lse_ref[rows, :][:, :1])
                dv = dv + lax.dot_general(p.astype(BF16), dot_, _TN, preferred_element_type=F32)
                dp = lax.dot_general(dot_, vt, _NT, preferred_element_type=F32)
                ds = (p * (dp - delta[i]) * ATTN_SCALE).astype(BF16)
                dk = dk + lax.dot_general(ds, qt, _TN, preferred_element_type=F32)
                dq_part = jnp.dot(ds, kc, preferred_element_type=F32)
                if j == 0:
                    dq_acc[rows, :] = dq_part
                else:
                    dq_acc[rows, :] += dq_part
            dkv_ref[cols, :QK_NOPE] = dk[:, :QK_NOPE].astype(BF16)
            dkv_ref[cols, QK_NOPE:] = dv.astype(BF16)
            dkr_ref[cols, :] += dk[:, QK_NOPE:]
        dq_ref[...] = dq_acc[...].astype(BF16)

    seq_q = pl.BlockSpec((S, HEAD_W), lambda b, h: (b, h))
    seq_h = pl.BlockSpec((S, LANE), lambda b, h: (b, h))
    seq_shared = pl.BlockSpec((S, LANE), lambda b, h: (b, 0))
    return pl.pallas_call(
        body, name=name,
        out_shape=(jax.ShapeDtypeStruct((T, HQ), BF16), jax.ShapeDtypeStruct((T, HQ), BF16),
                   jax.ShapeDtypeStruct((T, LANE), F32)),
        grid=(B, H),
        in_specs=[seq_q,
                  pl.BlockSpec((S, QK_NOPE), lambda b, h: (b, 2 * h)),
                  pl.BlockSpec((S, V_DIM), lambda b, h: (b, 2 * h + 1)),
                  seq_shared, seq_h, seq_h, seq_h],
        out_specs=(seq_q, seq_q, seq_shared),
        scratch_shapes=[pltpu.VMEM((S, HEAD_W), BF16), pltpu.VMEM((S, HEAD_W), F32)],
        compiler_params=_params("parallel", "arbitrary"),
    )(q, kv, kv, kr, o, do, lse)


def _merge_out(h, ms, o, x, wgate, bgate, wpp, wap, wout, next_gain, *, name):
    T, D = x.shape
    tm = _tile(T, 256, 16)

    def body(h_ref, ms_ref, o_ref, x_ref, wgate_ref, bgate_ref, wpp_ref, wap_ref, wout_ref, ng_ref,
             gates_ref, ba_ref, bb_ref, merged_ref, xn_ref, hn_ref):
        logits = lax.dot_general(h_ref[...], wgate_ref[...], _NT, preferred_element_type=F32) + bgate_ref[...]
        gates = jax.nn.sigmoid(logits)
        ba = jnp.dot(ms_ref[...], wpp_ref[...], preferred_element_type=F32)
        bb = jnp.dot(o_ref[...], wap_ref[...], preferred_element_type=F32)
        merged = (gates[:, :D] * ba + gates[:, D:] * bb).astype(BF16)
        gates_ref[...] = gates.astype(BF16)
        ba_ref[...] = ba.astype(BF16)
        bb_ref[...] = bb.astype(BF16)
        merged_ref[...] = merged
        xn = x_ref[...] + jnp.dot(merged, wout_ref[...], preferred_element_type=F32)
        xn_ref[...] = xn
        hn_ref[...] = (xn * _rstd(xn) * ng_ref[...]).astype(BF16)

    def row(w):
        return pl.BlockSpec((tm, w), lambda i: (i, 0))

    def whole(arr):
        return pl.BlockSpec(arr.shape, lambda i: (0,) * arr.ndim)

    bg2, ng2 = bgate.reshape(1, 2 * D), next_gain.reshape(1, D)
    act = jax.ShapeDtypeStruct((T, D), BF16)
    return pl.pallas_call(
        body, name=name,
        out_shape=(jax.ShapeDtypeStruct((T, 2 * D), BF16), act, act, act, jax.ShapeDtypeStruct((T, D), F32), act),
        grid=(T // tm,),
        in_specs=[row(D), row(ms.shape[1]), row(o.shape[1]), row(D), whole(wgate), whole(bg2), whole(wpp),
                  whole(wap), whole(wout), whole(ng2)],
        out_specs=(row(2 * D), row(D), row(D), row(D), row(D), row(D)),
        compiler_params=_params("parallel"),
    )(h, ms, o, x, wgate, bg2, wpp, wap, wout, ng2)


def _merge_bwd(dxo, wout, wpp, wap, gates, ba, bb, *, name, dep=None):
    T, D = dxo.shape
    tm = _tile(T, 512, 16)

    def body(dxo_ref, wout_ref, wpp_ref, wap_ref, gates_ref, ba_ref, bb_ref, *rest):
        dba_ref, dbb_ref, dgl_ref, dbg_ref, dms_ref, do_ref = rest[-6:]
        dm = lax.dot_general(dxo_ref[...].astype(BF16), wout_ref[...], _NT, preferred_element_type=F32)
        ga = gates_ref[:, :D].astype(F32)
        gb = gates_ref[:, D:].astype(F32)
        dba = (dm * ga).astype(BF16)
        dbb = (dm * gb).astype(BF16)
        dba_ref[...] = dba
        dbb_ref[...] = dbb
        dms_ref[...] = lax.dot_general(dba, wpp_ref[...], _NT, preferred_element_type=F32)
        do_ref[...] = lax.dot_general(dbb, wap_ref[...], _NT, preferred_element_type=F32).astype(BF16)
        dgl_a = dm * ba_ref[...].astype(F32) * (ga * (1.0 - ga))
        dgl_b = dm * bb_ref[...].astype(F32) * (gb * (1.0 - gb))
        dgl_ref[:, :D] = dgl_a.astype(BF16)
        dgl_ref[:, D:] = dgl_b.astype(BF16)
        sa = jnp.sum(dgl_a, axis=0, keepdims=True)
        sb = jnp.sum(dgl_b, axis=0, keepdims=True)

        @pl.when(pl.program_id(0) == 0)
        def _():
            dbg_ref[:, :D] = sa
            dbg_ref[:, D:] = sb

        @pl.when(pl.program_id(0) > 0)
        def _():
            dbg_ref[:, :D] += sa
            dbg_ref[:, D:] += sb

    def row(w):
        return pl.BlockSpec((tm, w), lambda i: (i, 0))

    def whole(arr):
        return pl.BlockSpec(arr.shape, lambda i: (0, 0))

    P, HV = wpp.shape[0], wap.shape[0]
    act = jax.ShapeDtypeStruct((T, D), BF16)
    return pl.pallas_call(
        body, name=name,
        out_shape=(act, act, jax.ShapeDtypeStruct((T, 2 * D), BF16), jax.ShapeDtypeStruct((1, 2 * D), F32),
                   jax.ShapeDtypeStruct((T, P), F32), jax.ShapeDtypeStruct((T, HV), BF16)),
        grid=(T // tm,),
        in_specs=[row(D), whole(wout), whole(wpp), whole(wap), row(2 * D), row(D), row(D)] + _dep_spec(dep),
        out_specs=(row(D), row(D), row(2 * D), pl.BlockSpec((1, 2 * D), lambda i: (0, 0)), row(P), row(HV)),
        compiler_params=_params("arbitrary"),
    )(dxo, wout, wpp, wap, gates, ba, bb, *([] if dep is None else [dep]))


def _ffn_fwd(x, h, w, tag, next_gain, dep=None):
    gate, up, a, xn, hn = _ffn_fwd_core(x, h, w["up_t"], w["wd"], next_gain, alpha=0.5,
                                        name=f"{tag}_fwd" if next_gain is not None else f"{tag}_fwd_last", dep=dep)
    return xn, hn, (x, h, gate, up, a)


def _ffn_bwd(dxo, gain, w, saved, tag, dep=None):
    x, h, gate, up, a = saved
    F = gate.shape[1]
    dgate, dup, dx, dgain = _ffn_bwd_core(dxo, w["wd"], w["up_t"], gate, up, x, gain, alpha=0.5,
                                          name=f"{tag}_bwd_core", dep=dep)
    dwd = _mm(a, dxo, ta=True, alpha=0.5, out_dtype=BF16, name=f"{tag}_dwd", tm=1408, tn=1024, tk=1024)
    dup_t = _ffn_dw_up(dgate, dup, h, name=f"{tag}_dw_up")
    return dx, dgain, dup_t, dwd


def _mixer_fwd(x, h, p, w, tabs, S, next_gain, dep=None):
    xp, ql, kvl, qn, kvn, q, kv, kr = _mixer_in(h, w["win_t"], w["wuq_t"], w["wukv"], p["q_latent_norm"],
                                                 p["kv_latent_norm"], tabs, name="mix_in", dep=dep)
    ms = _pool_fwd(xp, p["pool_maps"], p["pool_scale"], S=S, name="pool_fwd")
    o, lse = _attn_fwd(q, kv, kr, S=S, name="attn_fwd")
    gates, ba, bb, merged, xn, hn = _merge_out(h, ms, o, x, w["wgate_t"], p["b_gate"], w["wpp"], w["wap"], w["wout"],
                                               next_gain, name="merge_out")
    return xn, hn, (x, h, xp, ql, kvl, qn, kvn, q, kv, kr, ms, o, lse, gates, ba, bb, merged)


def _mixer_bwd(dxo, p, w, tabs, saved, S, dep=None):
    x, h, xp, ql, kvl, qn, kvn, q, kv, kr, ms, o, lse, gates, ba, bb, merged = saved
    dba, dbb, dgl, dbg, dms, do = _merge_bwd(dxo, w["wout"], w["wpp"], w["wap"], gates, ba, bb, name="merge_bwd",
                                             dep=dep)
    g = {}
    g["wout"], g["wpp"], g["wap"] = _dw_multi([(merged, dxo), (ms, dba), (o, dbb)], name="d_w_merge")
    dxp, g["pool_maps"], g["pool_scale"] = _pool_bwd(xp, dms, p["pool_maps"], p["pool_scale"], S=S, name="pool_bwd")
    dq, dkv, dkr = _attn_bwd(q, kv, kr, o, do, lse, S=S, name="attn_bwd")
    dproj, dqp, g["q_latent_norm"], g["kv_latent_norm"], dx, g["norm_mix"] = _mixer_in_bwd(
        dq, dkv, dkr, ql, kvl, dxp, dgl, x, dxo, w["win_t"], w["wgate_t"], w["wuq_t"], w["wukv"],
        p["norm_mix"], p["q_latent_norm"], p["kv_latent_norm"], tabs, name="mix_in_bwd")
    g["wuq_t"], g["wukv"] = _dw_multi([(dqp, qn), (kvn, dkv)], name="d_w_qkv", tk=1024)
    g["wa_t"], g["wgate_t"] = _dw_multi([(dproj, h), (dgl, h)], name="d_w_in")
    g["b_gate"] = dbg
    return dx, g


BIG = ("ffn1_up", "ffn1_down", "w_in", "w_pool_proj", "w_uq", "w_ukv", "w_attn_proj", "w_out", "ffn2_up", "ffn2_down")
SMALL = ("norm_ffn1", "norm_mix", "b_gate", "pool_maps", "pool_scale", "q_latent_norm", "kv_latent_norm", "norm_ffn2")
PACKED = ("w_pool_proj", "w_uq", "w_ukv")
TRANSPOSED = ("ffn1_up", "ffn2_up", "w_in", "w_uq")
COL_SHARDED = ("w_pool_proj", "w_ukv")
QK_HEAD = QK_NOPE + QK_ROPE


def _rows(stacked):
    n, r, c = stacked.shape
    return stacked.reshape(n * r, c)


def _cols(stacked):
    n, k, c = stacked.shape
    return stacked.transpose(1, 0, 2).reshape(k, n * c)


FFN1_PART = ("ffn1_up", "ffn1_down")
MIXER_PART = ("w_in", "w_attn_proj", "w_out") + PACKED
FFN2_PART = ("ffn2_up", "ffn2_down")


def _kernel_weights(stacked):
    full = {}
    for tag in ("ffn1", "ffn2"):
        if tag + "_up" in stacked:
            full[tag] = {"up_t": _rows(stacked[tag + "_up"]), "wd": _rows(stacked[tag + "_down"])}
    if "w_in" in stacked:
        win_t = _rows(stacked["w_in"])
        D = win_t.shape[1]
        wuq_t = _rows(stacked["w_uq"])
        QL = wuq_t.shape[1]
        H = wuq_t.shape[0] // QK_HEAD
        wuq_t = jnp.pad(wuq_t.reshape(H, QK_HEAD, QL), ((0, 0), (0, HEAD_W - QK_HEAD), (0, 0)))
        full.update({"win_t": win_t, "wgate_t": win_t[win_t.shape[0] - 2 * D:], "wuq_t": wuq_t.reshape(H * HEAD_W, QL),
                     "wukv": _cols(stacked["w_ukv"]), "wpp": _cols(stacked["w_pool_proj"]),
                     "wap": _rows(stacked["w_attn_proj"]), "wout": _rows(stacked["w_out"])})
    return full


def _split_rows(full):
    return full.reshape(N_DEV, full.shape[0] // N_DEV, full.shape[1])


def _split_cols(full):
    k, cols = full.shape
    return full.reshape(k, N_DEV, cols // N_DEV).transpose(1, 0, 2)


def _mixer_grads_stacked(g):
    n_a = g["wa_t"].shape[0] - (LANE - QK_ROPE)
    HQ, QL = g["wuq_t"].shape
    H = HQ // HEAD_W
    wuq_t = g["wuq_t"].reshape(H, HEAD_W, QL)[:, :QK_HEAD].reshape(H * QK_HEAD, QL)
    return {"w_in": _split_rows(jnp.concatenate([g["wa_t"][:n_a], g["wgate_t"]], axis=0)),
            "w_uq": _split_rows(wuq_t),
            "w_pool_proj": _split_cols(g["wpp"]), "w_ukv": _split_cols(g["wukv"]),
            "w_attn_proj": _split_rows(g["wap"]), "w_out": _split_rows(g["wout"])}


def _mesh_place():
    x, y, c = lax.axis_index("x"), lax.axis_index("y"), lax.axis_index("c")
    chips = [(1 - x, y), (x, 1 - y), (1 - x, 1 - y)]
    return x, y, c, chips


HBM = pl.BlockSpec(memory_space=pltpu.HBM)
SEMAPHORES = pl.BlockSpec(memory_space=pltpu.SEMAPHORE)
DATAFLOW = pltpu.SideEffectType.DATAFLOW_SIDE_EFFECTING
GATHER_PEERS = 4
SCATTER_PEERS = 7


def _in_hbm(a):
    return pltpu.with_memory_space_constraint(a, pltpu.HBM)


def _gather_plan(src_refs, land_refs):
    x, y, c, chips = _mesh_place()
    me = 4 * x + 2 * y + c
    targets = [(x, y, 1 - c)] + [(cx, cy, c) for cx, cy in chips]
    return [(s, land.at[me], to) for s, land in zip(src_refs, land_refs) for to in targets]


def _scatter_plan(src_refs, land_refs):
    x, y, c, _ = _mesh_place()
    peers = [(x, y, 1 - c), (1 - x, y, c), (x, 1 - y, c), (1 - x, 1 - y, c),
             (1 - x, y, 1 - c), (x, 1 - y, 1 - c), (1 - x, 1 - y, 1 - c)]
    return [(s.at[4 * px + 2 * py + pc], land.at[k], (px, py, pc))
            for s, land in zip(src_refs, land_refs) for k, (px, py, pc) in enumerate(peers)]


def _descriptors(plan, src_refs, land_refs, send_sems, recv_sems):
    return [pltpu.make_async_remote_copy(src_ref=s, dst_ref=d, send_sem=send_sems.at[k], recv_sem=recv_sems.at[k],
                                         device_id=to, device_id_type=MESH)
            for k, (s, d, to) in enumerate(plan(src_refs, land_refs))]


def _exchange(srcs, land_shapes, plan, per_src, *, name):
    n = len(srcs)

    def body(*refs):
        copies = _descriptors(plan, refs[:n], refs[n:2 * n], refs[2 * n], refs[2 * n + 1])
        for cp in copies:
            cp.start()
        for cp in copies:
            cp.wait()

    return pl.pallas_call(
        body, name=name,
        out_shape=tuple(jax.ShapeDtypeStruct(shape, s.dtype) for shape, s in zip(land_shapes, srcs)),
        in_specs=[ANY] * n, out_specs=(ANY,) * n,
        scratch_shapes=[pltpu.SemaphoreType.DMA((per_src * n,)), pltpu.SemaphoreType.DMA((per_src * n,))],
    )(*srcs)


FORWARD_COPIES = 4


def _forward_slots():
    x, y, c, chips = _mesh_place()
    return [4 * cx + 2 * cy + c for cx, cy in chips] + [4 * x + 2 * y + (1 - c)], (x, y, 1 - c)


def _forward_plan(src_refs, land_refs):
    slots, sibling = _forward_slots()
    return [(land.at[s], land.at[s], sibling) for land in land_refs for s in slots]


def _gather_all_plan(src_refs, land_refs):
    x, y, c, _ = _mesh_place()
    me = 4 * x + 2 * y + c
    peers = [(x, y, 1 - c), (1 - x, y, c), (x, 1 - y, c), (1 - x, 1 - y, c),
             (1 - x, y, 1 - c), (x, 1 - y, 1 - c), (1 - x, 1 - y, 1 - c)]
    return [(s, land.at[me], to) for s, land in zip(src_refs, land_refs) for to in peers]


def _exchange_start(srcs, lands, plan, n_copies, *, name):
    ns, n = len(srcs), len(srcs) + len(lands)

    def body(*refs):
        for cp in _descriptors(plan, refs[:ns], refs[ns:n], refs[n], refs[n + 1]):
            cp.start()
        refs[-1][...] = jnp.zeros_like(refs[-1])

    sems = pltpu.SemaphoreType.DMA((n_copies,))
    out = pl.pallas_call(
        body, name=name,
        out_shape=(sems, sems, *[pltpu.HBM(a.shape, a.dtype) for a in srcs + lands],
                   jax.ShapeDtypeStruct((8, LANE), F32)),
        in_specs=(HBM,) * n,
        out_specs=(SEMAPHORES, SEMAPHORES, *[HBM] * n, pl.BlockSpec(memory_space=pltpu.VMEM)),
        input_output_aliases={i: 2 + i for i in range(n)},
        compiler_params=pltpu.CompilerParams(has_side_effects=DATAFLOW),
    )(*[_in_hbm(a) for a in srcs + lands])
    return out[0], out[1], list(out[2:2 + ns]), list(out[2 + ns:2 + n]), out[-1]


def _exchange_wait(send_sems, recv_sems, srcs, lands, plan, after, *, name):
    ns, n = len(srcs), len(srcs) + len(lands)

    def body(*refs):
        for cp in _descriptors(plan, refs[:ns], refs[ns:n], refs[n], refs[n + 1]):
            cp.wait_send()
            cp.wait_recv()

    out = pl.pallas_call(
        body, name=name,
        out_shape=tuple(pltpu.HBM(a.shape, a.dtype) for a in srcs + lands),
        in_specs=(*[HBM] * n, SEMAPHORES, SEMAPHORES, ANY),
        out_specs=(HBM,) * n,
        input_output_aliases={i: i for i in range(n)},
        compiler_params=pltpu.CompilerParams(has_side_effects=DATAFLOW),
    )(*srcs, *lands, send_sems, recv_sems, after)
    return list(out[:ns]), list(out[ns:])


def _gather_forward(lands, *, name):
    n = len(lands)

    def body(*refs):
        in_refs, out_refs = refs[:n], refs[n:2 * n]
        token, send_sems, recv_sems = refs[2 * n:2 * n + 3]
        slots, sibling = _forward_slots()
        passed = [pltpu.make_async_remote_copy(
            src_ref=i.at[s], dst_ref=o.at[s],
            send_sem=send_sems.at[FORWARD_COPIES * b + j], recv_sem=recv_sems.at[FORWARD_COPIES * b + j],
            device_id=sibling, device_id_type=MESH)
            for b, (i, o) in enumerate(zip(in_refs, out_refs)) for j, s in enumerate(slots)]
        for cp in passed:
            cp.start()
        for cp in passed:
            cp.wait()
        token[...] = jnp.zeros_like(token)

    out = pl.pallas_call(
        body, name=name,
        out_shape=(*[jax.ShapeDtypeStruct(a.shape, a.dtype) for a in lands], jax.ShapeDtypeStruct((8, LANE), F32)),
        in_specs=[ANY] * n,
        out_specs=(*[ANY] * n, pl.BlockSpec(memory_space=pltpu.VMEM)),
        input_output_aliases={i: i for i in range(n)},
        scratch_shapes=[pltpu.SemaphoreType.DMA((FORWARD_COPIES * n,)), pltpu.SemaphoreType.DMA((FORWARD_COPIES * n,))],
    )(*lands)
    return list(out[:n]), out[n]


def _scatter_sum(parts, got, me, *, name):
    shard = parts.shape[1:]
    cols = shard[-1]
    rows = int(np.prod(shard[:-1]))
    tr = _tile(rows, 256, 16)

    def body(me_ref, p_ref, g_ref, o_ref):
        acc = p_ref[...].astype(F32)
        for k in range(SCATTER_PEERS):
            acc = acc + g_ref[k].astype(F32)
        o_ref[...] = acc

    out = pl.pallas_call(
        body, name=name,
        out_shape=jax.ShapeDtypeStruct((rows, cols), F32),
        grid_spec=pltpu.PrefetchScalarGridSpec(
            num_scalar_prefetch=1, grid=(rows // tr,),
            in_specs=[pl.BlockSpec((None, tr, cols), lambda r, me_ref: (me_ref[0], r, 0)),
                      pl.BlockSpec((SCATTER_PEERS, tr, cols), lambda r, me_ref: (0, r, 0))],
            out_specs=pl.BlockSpec((tr, cols), lambda r, me_ref: (r, 0))),
        compiler_params=_params("parallel"),
    )(me, parts.reshape(N_DEV, rows, cols), got.reshape(SCATTER_PEERS, rows, cols))
    return out.reshape(shard)


def _sum_devices(parts, *, name):
    _, R, C = parts.shape
    tr = _tile(R, 512, 8)

    def body(p_ref, o_ref):
        acc = p_ref[0]
        for d in range(1, N_DEV):
            acc = acc + p_ref[d]
        o_ref[...] = acc

    return pl.pallas_call(
        body, name=name,
        out_shape=jax.ShapeDtypeStruct((R, C), F32),
        grid=(R // tr,),
        in_specs=[pl.BlockSpec((N_DEV, tr, C), lambda r: (0, r, 0))],
        out_specs=pl.BlockSpec((tr, C), lambda r: (r, 0)),
        compiler_params=_params("parallel"),
    )(parts)


def _adamw(w, g, m, v, *, name, dep=None):
    shape = w.shape
    cols = shape[-1]
    rows = w.size // cols
    tr = _tile(rows, 256, 8)

    def body(w_ref, g_ref, m_ref, v_ref, *rest):
        d_ref, nm_ref, nv_ref = rest[-3:]
        g = g_ref[...]
        m = ADAM_B1 * m_ref[...] + (1.0 - ADAM_B1) * g
        v = ADAM_B2 * v_ref[...] + (1.0 - ADAM_B2) * jnp.square(g)
        m_hat = m / (1.0 - ADAM_B1 ** ADAM_STEP)
        v_hat = v / (1.0 - ADAM_B2 ** ADAM_STEP)
        d_ref[...] = -ADAM_LR * (m_hat / (jnp.sqrt(v_hat) + ADAM_EPS) + ADAM_WD * w_ref[...])
        nm_ref[...] = m
        nv_ref[...] = v

    spec = pl.BlockSpec((tr, cols), lambda i: (i, 0))
    out = jax.ShapeDtypeStruct((rows, cols), F32)
    d, nm, nv = pl.pallas_call(
        body, name=name,
        out_shape=(out, out, out),
        grid=(rows // tr,),
        in_specs=[spec] * 4 + _dep_spec(dep), out_specs=(spec,) * 3,
        compiler_params=_params("parallel"),
    )(*(a.reshape(rows, cols) for a in (w, g, m, v)), *([] if dep is None else [dep]))
    return d.reshape(shape), nm.reshape(shape), nv.reshape(shape)


PACK_ALIGN = 16 * LANE


def _pack(pieces, lead):
    out = []
    for p in pieces:
        keep = p.shape[:lead]
        flat = p.reshape(*keep, -1)
        pad = (-flat.shape[-1]) % PACK_ALIGN
        if pad:
            flat = jnp.pad(flat, [(0, 0)] * lead + [(0, pad)])
        out.append(flat.reshape(*keep, -1, LANE))
    return jnp.concatenate(out, axis=lead)


def _unpack(buf, shapes, lead):
    keep = buf.shape[:lead]
    out, row = [], 0
    for shape in shapes:
        size = int(np.prod(shape))
        rows = -(-size // PACK_ALIGN) * (PACK_ALIGN // LANE)
        piece = lax.slice_in_dim(buf, row, row + rows, axis=lead).reshape(*keep, rows * LANE)
        out.append(lax.slice_in_dim(piece, 0, size, axis=lead).reshape(*keep, *shape))
        row += rows
    return out


def kernel(x, positions, norm_ffn1, ffn1_up, ffn1_down, norm_mix, w_in, b_gate, pool_maps, pool_scale, w_pool_proj, q_latent_norm, w_uq, kv_latent_norm, w_ukv, w_attn_proj, w_out, norm_ffn2, ffn2_up, ffn2_down, final_norm, loss_target, m_norm_ffn1, m_ffn1_up, m_ffn1_down, m_norm_mix, m_w_in, m_b_gate, m_pool_maps, m_pool_scale, m_w_pool_proj, m_q_latent_norm, m_w_uq, m_kv_latent_norm, m_w_ukv, m_w_attn_proj, m_w_out, m_norm_ffn2, m_ffn2_up, m_ffn2_down, m_final_norm, v_norm_ffn1, v_ffn1_up, v_ffn1_down, v_norm_mix, v_w_in, v_b_gate, v_pool_maps, v_pool_scale, v_w_pool_proj, v_q_latent_norm, v_w_uq, v_kv_latent_norm, v_w_ukv, v_w_attn_proj, v_w_out, v_norm_ffn2, v_ffn2_up, v_ffn2_down, v_final_norm):
    order = ("norm_ffn1", "ffn1_up", "ffn1_down", "norm_mix", "w_in", "b_gate", "pool_maps", "pool_scale",
             "w_pool_proj", "q_latent_norm", "w_uq", "kv_latent_norm", "w_ukv", "w_attn_proj", "w_out",
             "norm_ffn2", "ffn2_up", "ffn2_down", "final_norm")
    w = dict(zip(order, (norm_ffn1, ffn1_up, ffn1_down, norm_mix, w_in, b_gate, pool_maps, pool_scale, w_pool_proj,
                         q_latent_norm, w_uq, kv_latent_norm, w_ukv, w_attn_proj, w_out, norm_ffn2, ffn2_up,
                         ffn2_down, final_norm)))
    m = dict(zip(order, (m_norm_ffn1, m_ffn1_up, m_ffn1_down, m_norm_mix, m_w_in, m_b_gate, m_pool_maps, m_pool_scale,
                         m_w_pool_proj, m_q_latent_norm, m_w_uq, m_kv_latent_norm, m_w_ukv, m_w_attn_proj, m_w_out,
                         m_norm_ffn2, m_ffn2_up, m_ffn2_down, m_final_norm)))
    v = dict(zip(order, (v_norm_ffn1, v_ffn1_up, v_ffn1_down, v_norm_mix, v_w_in, v_b_gate, v_pool_maps, v_pool_scale,
                         v_w_pool_proj, v_q_latent_norm, v_w_uq, v_kv_latent_norm, v_w_ukv, v_w_attn_proj, v_w_out,
                         v_norm_ffn2, v_ffn2_up, v_ffn2_down, v_final_norm)))
    L = norm_ffn1.shape[0]
    B, S, D = x.shape
    T = B * S

    def turned(a, n):
        return a.transpose(0, 2, 1) if n in TRANSPOSED else a

    wk, mk, vk = ({n: turned(d[n], n) for n in order} for d in (w, m, v))
    packed_shapes = [wk[n].shape[1:] for n in PACKED]
    my_slot = 4 * lax.axis_index("x") + 2 * lax.axis_index("y") + lax.axis_index("c")
    me = jnp.stack([my_slot]).astype(jnp.int32)

    def weight_blocks(l, names, token):
        zero = token[0, 0].astype(BF16)
        blocks = [wk[n][l].astype(BF16) + zero for n in names if n not in PACKED]
        if any(n in PACKED for n in names):
            blocks.append(_pack([wk[n][l].astype(BF16) + zero for n in PACKED], 0))
        return blocks

    def kernel_weights(names, lands):
        direct = [n for n in names if n not in PACKED]
        stacked = dict(zip(direct, lands))
        if len(lands) > len(direct):
            stacked.update(zip(PACKED, _unpack(lands[-1], packed_shapes, 1)))
        return _kernel_weights(stacked)

    def gather_start(l, names, token, tag):
        blocks = weight_blocks(l, names, token)
        lands = [lax.empty((N_DEV, *b.shape), b.dtype) for b in blocks]
        send_sems, recv_sems, blocks, lands, token = _exchange_start(
            blocks, lands, _gather_plan, GATHER_PEERS * len(blocks), name=f"gather_start_{tag}")
        return (send_sems, recv_sems, blocks, lands, tag), token

    def gather_wait(state, after):
        send_sems, recv_sems, blocks, lands, tag = state
        return _exchange_wait(send_sems, recv_sems, blocks, lands, _gather_plan, after, name=f"gather_wait_{tag}")[1]

    layer_part = FFN1_PART + MIXER_PART + FFN2_PART
    tabs = _rope_tables(positions.reshape(T))
    xs = x.reshape(T, D)
    h = _rms_fwd(xs, w["norm_ffn1"][0], name="first_norm")
    full, saved = [], []

    p = {n: w[n][0] for n in SMALL}
    blocks = weight_blocks(0, FFN1_PART, jnp.zeros((8, LANE), F32))
    lands = _exchange(blocks, [(N_DEV, *b.shape) for b in blocks], _gather_plan, GATHER_PEERS, name="gather_first")
    lands, token = _gather_forward(lands, name="gather_forward")
    w0 = kernel_weights(FFN1_PART, lands)
    state, token = gather_start(0, MIXER_PART, token, "0_mix")
    xs, h, s1 = _ffn_fwd(xs, h, w0["ffn1"], "ffn1", p["norm_mix"], dep=token)
    lands, token = _gather_forward(gather_wait(state, xs), name="gather_forward")
    w0.update(kernel_weights(MIXER_PART, lands))
    state, token = gather_start(0, FFN2_PART, token, "0_ffn2")
    if L > 1:
        next_state, token = gather_start(1, layer_part, token, "1")
    xs, h, s2 = _mixer_fwd(xs, h, p, w0, tabs, S, p["norm_ffn2"], dep=token)
    lands, token = _gather_forward(gather_wait(state, xs), name="gather_forward")
    w0.update(kernel_weights(FFN2_PART, lands))
    xs, h, s3 = _ffn_fwd(xs, h, w0["ffn2"], "ffn2", w["norm_ffn1"][1] if L > 1 else None, dep=token)
    if L > 1:
        lands, token = _gather_forward(gather_wait(next_state, xs), name="gather_forward")
    full.append(w0)
    saved.append((s1, s2, s3))

    for l in range(1, L):
        full.append(kernel_weights(layer_part, lands))
        more = l + 1 < L
        p = {n: w[n][l] for n in SMALL}
        if more:
            state, token = gather_start(l + 1, layer_part, token, f"{l + 1}")
        xs, h, s1 = _ffn_fwd(xs, h, full[l]["ffn1"], "ffn1", p["norm_mix"], dep=token if more else None)
        xs, h, s2 = _mixer_fwd(xs, h, p, full[l], tabs, S, p["norm_ffn2"])
        if more:
            lands = gather_wait(state, xs)
            send_sems, recv_sems, _, lands, token = _exchange_start(
                [], lands, _forward_plan, FORWARD_COPIES * len(lands), name=f"forward_start_{l + 1}")
        xs, h, s3 = _ffn_fwd(xs, h, full[l]["ffn2"], "ffn2", w["norm_ffn1"][l + 1] if more else None,
                             dep=token if more else None)
        if more:
            _, lands = _exchange_wait(send_sems, recv_sems, [], lands, _forward_plan, xs, name=f"forward_wait_{l + 1}")
        saved.append((s1, s2, s3))
    dx, dfinal, loss = _loss_head(xs, final_norm, loss_target.reshape(T, D), name="loss_head")

    big_grads = {n: [None] * L for n in BIG}
    small_grads_of = [None] * L
    pending = None

    def scatter_start(names, stacked, tag):
        srcs = [stacked[n] for n in names if n not in PACKED]
        if any(n in PACKED for n in names):
            srcs.append(_pack([stacked[n] for n in PACKED], 1))
        lands = [lax.empty((SCATTER_PEERS, *s.shape[1:]), s.dtype) for s in srcs]
        send_sems, recv_sems, srcs, lands, token = _exchange_start(
            srcs, lands, _scatter_plan, SCATTER_PEERS * len(srcs), name=f"scatter_start_{tag}")
        return (names, send_sems, recv_sems, srcs, lands, tag), token

    def scatter_finish(state, after, l):
        names, send_sems, recv_sems, srcs, lands, tag = state
        srcs, got = _exchange_wait(send_sems, recv_sems, srcs, lands, _scatter_plan, after, name=f"scatter_wait_{tag}")
        sums = [_scatter_sum(s, g, me, name="scatter_sum") for s, g in zip(srcs, got)]
        direct = [n for n in names if n not in PACKED]
        for n, g in zip(direct, sums):
            big_grads[n][l] = g
        if len(sums) > len(direct):
            for n, g in zip(PACKED, _unpack(sums[-1], packed_shapes, 0)):
                big_grads[n][l] = g

    dep = None
    for l in reversed(range(L)):
        p = {n: w[n][l] for n in SMALL}
        s1, s2, s3 = saved[l]
        small_g = {}
        dx, small_g["norm_ffn2"], dup_t, dwd = _ffn_bwd(dx, p["norm_ffn2"], full[l]["ffn2"], s3, "ffn2", dep=dep)
        if pending is not None:
            scatter_finish(pending[0], dx, pending[1])
        stacked = {"ffn2_up": _split_rows(dup_t), "ffn2_down": _split_rows(dwd)}
        state, dep = scatter_start(("ffn2_up", "ffn2_down"), stacked, f"ffn2_{l}")
        pending = (state, l)

        dx, gm = _mixer_bwd(dx, p, full[l], tabs, s2, S, dep=dep)
        scatter_finish(pending[0], dx, pending[1])
        names = ("w_in", "w_attn_proj", "w_out") + PACKED
        state, dep = scatter_start(names, _mixer_grads_stacked(gm), f"mix_{l}")
        pending = (state, l)
        small_g.update({n: gm[n] for n in SMALL if n in gm})

        dx, small_g["norm_ffn1"], dup_t, dwd = _ffn_bwd(dx, p["norm_ffn1"], full[l]["ffn1"], s1, "ffn1", dep=dep)
        scatter_finish(pending[0], dx, pending[1])
        stacked = {"ffn1_up": _split_rows(dup_t), "ffn1_down": _split_rows(dwd)}
        state, dep = scatter_start(("ffn1_up", "ffn1_down"), stacked, f"ffn1_{l}")
        pending = (state, l)
        small_grads_of[l] = small_g
    grad_x = dx.reshape(B, S, D)

    small_parts = [small_grads_of[l][n] for l in range(L) for n in SMALL] + [dfinal, loss[0, :1]]
    small_shapes = [p.shape for p in small_parts]
    vec = _pack([jnp.concatenate([p.reshape(-1) for p in small_parts])], 0)
    small_send, small_recv, vec_thru, small_land, small_token = _exchange_start(
        [vec], [lax.empty((N_DEV, *vec.shape), F32)], _gather_all_plan, SCATTER_PEERS, name="small_start")

    gk, grad, delta, new_m, new_v = {}, {}, {}, {}, {}

    def update(n, dep=None):
        wn, gn, mn, vn = (a.reshape(1, -1) if a.ndim == 1 else a for a in (wk[n], gk[n], mk[n], vk[n]))
        d, nm, nv = _adamw(wn, gn, mn, vn, name="adamw_" + n, dep=dep)
        grad[n] = turned(gk[n], n)
        delta[n], new_m[n], new_v[n] = (turned(a.reshape(wk[n].shape), n) for a in (d, nm, nv))

    last_block = ("ffn1_up", "ffn1_down")
    deps = [dep, small_token]
    for n in BIG:
        if n not in last_block:
            gk[n] = jnp.stack(big_grads[n])
            update(n, deps.pop(0) if deps else None)
    scatter_finish(pending[0], new_v["ffn2_down"], pending[1])
    for n in last_block:
        gk[n] = jnp.stack(big_grads[n])
        update(n)

    vec_thru, small_land = _exchange_wait(small_send, small_recv, vec_thru, small_land, _gather_all_plan,
                                          new_v["ffn1_down"], name="small_wait")
    parts = lax.dynamic_update_index_in_dim(small_land[0], vec_thru[0], my_slot, 0)
    flat = _sum_devices(parts, name="sum_small").reshape(-1)
    small_grads, at = [], 0
    for shape in small_shapes:
        size = int(np.prod(shape))
        small_grads.append(lax.slice_in_dim(flat, at, at + size).reshape(shape))
        at += size
    loss_total = small_grads[-1].reshape(())
    for i, n in enumerate(SMALL):
        gk[n] = jnp.stack([small_grads[l * len(SMALL) + i] for l in range(L)]).reshape(w[n].shape)
        update(n)
    gk["final_norm"] = small_grads[-2].reshape(final_norm.shape)
    update("final_norm")
    return (loss_total, grad_x, *[grad[n] for n in order], *[delta[n] for n in order],
            *[new_m[n] for n in order], *[new_v[n] for n in order])
```

```python
import functools

import numpy as np
import jax
import jax.numpy as jnp
from jax import lax
from jax.experimental import pallas as pl
from jax.experimental.pallas import tpu as pltpu

F32 = jnp.float32
BF16 = jnp.bfloat16

NORM_EPS = 1e-6
ROPE_THETA = 10000.0
QK_NOPE = 128
QK_ROPE = 64
V_DIM = 128
HEAD_W = 256
POOL_WINDOWS = (2, 4, 8, 16)
POOL_G = 128
POOL_DIM = 512
LANE = 128
ATTN_SCALE = float((QK_NOPE + QK_ROPE) ** -0.5)
ATTN_SCALE_LOG2 = ATTN_SCALE * float(np.log2(np.e))
MASK_VALUE = -1e30
ATTN_TILE = 512

ADAM_LR = 0.001
ADAM_B1 = 0.9
ADAM_B2 = 0.999
ADAM_EPS = 1e-08
ADAM_WD = 0.01
ADAM_STEP = 10

N_DEV = 8
VMEM_LIMIT = 52 * 1024 * 1024

MESH = pl.DeviceIdType.MESH
ANY = pl.BlockSpec(memory_space=pl.ANY)


def _tile(dim, target, align=LANE):
    if dim <= target:
        return dim
    t = (target // align) * align
    while t >= align:
        if dim % t == 0:
            return t
        t -= align
    return dim


def _params(*sem):
    return pltpu.CompilerParams(dimension_semantics=sem, vmem_limit_bytes=VMEM_LIMIT)


def _rstd(x):
    return lax.rsqrt(jnp.mean(x * x, axis=-1, keepdims=True) + NORM_EPS)


def _mm(a, b, *, name, ta=False, tb=False, out_dtype=F32, alpha=1.0, tm=512, tn=1024, tk=1024):
    if ta:
        K, M = a.shape
    else:
        M, K = a.shape
    if tb:
        N, K2 = b.shape
    else:
        K2, N = b.shape
    assert K == K2, (a.shape, b.shape, ta, tb)
    tm, tn, tk = _tile(M, tm), _tile(N, tn), _tile(K, tk)
    nk = K // tk
    dims = (((0 if ta else 1,), (1 if tb else 0,)), ((), ()))

    def body(a_ref, b_ref, o_ref, *scratch):
        acc_ref = scratch[0] if nk > 1 else None
        part = lax.dot_general(a_ref[...].astype(BF16), b_ref[...].astype(BF16), dims,
                               preferred_element_type=F32)

        def finish(acc):
            o_ref[...] = (acc * alpha if alpha != 1.0 else acc).astype(out_dtype)

        if nk == 1:
            finish(part)
        else:
            k = pl.program_id(2)

            @pl.when(k == 0)
            def _():
                acc_ref[...] = part

            @pl.when(k > 0)
            def _():
                acc_ref[...] += part

            @pl.when(k == nk - 1)
            def _():
                finish(acc_ref[...])

    a_spec = pl.BlockSpec((tk, tm), lambda i, j, k: (k, i)) if ta else pl.BlockSpec((tm, tk), lambda i, j, k: (i, k))
    b_spec = pl.BlockSpec((tn, tk), lambda i, j, k: (j, k)) if tb else pl.BlockSpec((tk, tn), lambda i, j, k: (k, j))
    return pl.pallas_call(
        body, name=name,
        out_shape=jax.ShapeDtypeStruct((M, N), out_dtype),
        grid=(M // tm, N // tn, nk),
        in_specs=[a_spec, b_spec],
        out_specs=pl.BlockSpec((tm, tn), lambda i, j, k: (i, j)),
        scratch_shapes=[pltpu.VMEM((tm, tn), F32)] if nk > 1 else [],
        compiler_params=_params("parallel", "parallel", "arbitrary"),
    )(a, b)


def _dw_multi(pairs, *, name, tk=512):
    n = len(pairs)
    T = pairs[0][0].shape[0]
    tk = _tile(T, tk, 16)
    nk = T // tk
    shapes = [(a.shape[1], b.shape[1]) for a, b in pairs]

    def body(*refs):
        ins, outs, accs = refs[:2 * n], refs[2 * n:3 * n], refs[3 * n:]
        k = pl.program_id(0)
        parts = [lax.dot_general(ins[2 * i][...].astype(BF16), ins[2 * i + 1][...].astype(BF16), _TN,
                                 preferred_element_type=F32) for i in range(n)]

        @pl.when(k == 0)
        def _():
            for acc, part in zip(accs, parts):
                acc[...] = part

        @pl.when(k > 0)
        def _():
            for acc, part in zip(accs, parts):
                acc[...] += part

        @pl.when(k == nk - 1)
        def _():
            for out, acc in zip(outs, accs):
                out[...] = acc[...].astype(BF16)

    return pl.pallas_call(
        body, name=name,
        out_shape=tuple(jax.ShapeDtypeStruct(s, BF16) for s in shapes),
        grid=(nk,),
        in_specs=[pl.BlockSpec((tk, x.shape[1]), lambda k: (k, 0)) for pair in pairs for x in pair],
        out_specs=tuple(pl.BlockSpec(s, lambda k: (0, 0)) for s in shapes),
        scratch_shapes=[pltpu.VMEM(s, F32) for s in shapes],
        compiler_params=_params("arbitrary"),
    )(*[x for pair in pairs for x in pair])


def _rms_fwd(x, g, *, name):
    T, D = x.shape
    tm = _tile(T, 512, 16)

    def body(x_ref, g_ref, h_ref):
        x = x_ref[...]
        h_ref[...] = (x * _rstd(x) * g_ref[...]).astype(BF16)

    return pl.pallas_call(
        body, name=name,
        out_shape=jax.ShapeDtypeStruct((T, D), BF16),
        grid=(T // tm,),
        in_specs=[pl.BlockSpec((tm, D), lambda i: (i, 0)), pl.BlockSpec((1, D), lambda i: (0, 0))],
        out_specs=pl.BlockSpec((tm, D), lambda i: (i, 0)),
        compiler_params=_params("parallel"),
    )(x, g.reshape(1, D))


def _loss_head(x, g, target, *, name):
    T, D = x.shape
    tm = _tile(T, 512, 16)

    def body(x_ref, g_ref, t_ref, dx_ref, dg_ref, loss_ref):
        x = x_ref[...]
        gain = g_ref[...]
        r = _rstd(x)
        xhat = x * r
        err = xhat * gain - t_ref[...]
        dy = err * (1.0 / D)
        dxh = dy * gain
        dx_ref[...] = r * (dxh - xhat * jnp.mean(dxh * xhat, axis=-1, keepdims=True))
        dg_part = jnp.sum(dy * xhat, axis=0, keepdims=True)
        loss_part = jnp.full((1, LANE), 0.5 / D, F32) * jnp.sum(err * err)

        @pl.when(pl.program_id(0) == 0)
        def _():
            dg_ref[...] = dg_part
            loss_ref[...] = loss_part

        @pl.when(pl.program_id(0) > 0)
        def _():
            dg_ref[...] += dg_part
            loss_ref[...] += loss_part

    row = pl.BlockSpec((tm, D), lambda i: (i, 0))
    vec = pl.BlockSpec((1, D), lambda i: (0, 0))
    return pl.pallas_call(
        body, name=name,
        out_shape=(jax.ShapeDtypeStruct((T, D), F32), jax.ShapeDtypeStruct((1, D), F32),
                   jax.ShapeDtypeStruct((1, LANE), F32)),
        grid=(T // tm,),
        in_specs=[row, vec, row],
        out_specs=(row, vec, pl.BlockSpec((1, LANE), lambda i: (0, 0))),
        compiler_params=_params("arbitrary"),
    )(x, g.reshape(1, D), target)


def _ffn_fwd_core(x, h, w_up_t, wd, next_gain, *, alpha, name, dep=None):
    T, D = x.shape
    F = wd.shape[0]
    tm = _tile(T, 256, 16)
    has_norm = next_gain is not None

    def body(x_ref, h_ref, wg_ref, wu_ref, wd_ref, *rest):
        outs = rest[len(rest) - (5 if has_norm else 4):]
        gate_ref, up_ref, a_ref, xn_ref = outs[:4]
        h = h_ref[...]
        gate = lax.dot_general(h, wg_ref[...], _NT, preferred_element_type=F32)
        up = lax.dot_general(h, wu_ref[...], _NT, preferred_element_type=F32)
        a = (gate * jax.nn.sigmoid(gate) * up).astype(BF16)
        gate_ref[...] = gate.astype(BF16)
        up_ref[...] = up.astype(BF16)
        a_ref[...] = a
        xn = x_ref[...] + alpha * jnp.dot(a, wd_ref[...], preferred_element_type=F32)
        xn_ref[...] = xn
        if has_norm:
            outs[4][...] = (xn * _rstd(xn) * rest[0][...]).astype(BF16)

    once = pl.Buffered(1)
    row_d = pl.BlockSpec((tm, D), lambda i: (i, 0))
    row_f = pl.BlockSpec((tm, F), lambda i: (i, 0))
    vec = pl.BlockSpec((1, D), lambda i: (0, 0))
    act = jax.ShapeDtypeStruct((T, F), BF16)
    operands = [x, h, w_up_t, w_up_t, wd] + ([next_gain.reshape(1, D)] if has_norm else [])
    out = pl.pallas_call(
        body, name=name,
        out_shape=(act, act, act, jax.ShapeDtypeStruct((T, D), F32)) + ((jax.ShapeDtypeStruct((T, D), BF16),) if has_norm else ()),
        grid=(T // tm,),
        in_specs=[row_d, row_d,
                  pl.BlockSpec((F, D), lambda i: (0, 0), pipeline_mode=once),
                  pl.BlockSpec((F, D), lambda i: (1, 0), pipeline_mode=once),
                  pl.BlockSpec((F, D), lambda i: (0, 0), pipeline_mode=once)] + ([vec] if has_norm else []) + _dep_spec(dep),
        out_specs=(row_f, row_f, row_f, row_d) + ((row_d,) if has_norm else ()),
        compiler_params=_params("parallel"),
    )(*operands, *([] if dep is None else [dep]))
    return out if has_norm else (*out, None)


def _ffn_bwd_core(dxo, wd, w_up_t, gate, up, x, gain, *, alpha, name, dep=None):
    T, D = dxo.shape
    F = wd.shape[0]
    tm = _tile(T, 256, 16)

    def body(dxo_ref, wd_ref, wg_ref, wu_ref, gate_ref, up_ref, x_ref, g_ref, *rest):
        dgate_ref, dup_ref, dx_ref, dg_ref = rest[-4:]
        dxo = dxo_ref[...]
        da = lax.dot_general(dxo.astype(BF16), wd_ref[...], _NT, preferred_element_type=F32) * alpha
        gate = gate_ref[...].astype(F32)
        up = up_ref[...].astype(F32)
        sig = jax.nn.sigmoid(gate)
        dgate = (da * up * (sig * (1.0 + gate * (1.0 - sig)))).astype(BF16)
        dup = (da * (gate * sig)).astype(BF16)
        dgate_ref[...] = dgate
        dup_ref[...] = dup
        dh = (jnp.dot(dgate, wg_ref[...], preferred_element_type=F32)
              + jnp.dot(dup, wu_ref[...], preferred_element_type=F32))
        x = x_ref[...]
        r = _rstd(x)
        xhat = x * r
        dxh = dh * g_ref[...]
        dx_ref[...] = dxo + r * (dxh - xhat * jnp.mean(dxh * xhat, axis=-1, keepdims=True))
        part = jnp.sum(dh * xhat, axis=0, keepdims=True)

        @pl.when(pl.program_id(0) == 0)
        def _():
            dg_ref[...] = part

        @pl.when(pl.program_id(0) > 0)
        def _():
            dg_ref[...] += part

    once = pl.Buffered(1)
    row_d = pl.BlockSpec((tm, D), lambda i: (i, 0))
    row_f = pl.BlockSpec((tm, F), lambda i: (i, 0))
    vec = pl.BlockSpec((1, D), lambda i: (0, 0))
    act = jax.ShapeDtypeStruct((T, F), BF16)
    return pl.pallas_call(
        body, name=name,
        out_shape=(act, act, jax.ShapeDtypeStruct((T, D), F32), jax.ShapeDtypeStruct((1, D), F32)),
        grid=(T // tm,),
        in_specs=[row_d,
                  pl.BlockSpec((F, D), lambda i: (0, 0), pipeline_mode=once),
                  pl.BlockSpec((F, D), lambda i: (0, 0), pipeline_mode=once),
                  pl.BlockSpec((F, D), lambda i: (1, 0), pipeline_mode=once),
                  row_f, row_f, row_d, vec] + _dep_spec(dep),
        out_specs=(row_f, row_f, row_d, vec),
        compiler_params=_params("arbitrary"),
    )(dxo, wd, w_up_t, w_up_t, gate, up, x, gain.reshape(1, D), *([] if dep is None else [dep]))


def _ffn_dw_up(dgate, dup, h, *, name):
    T, F = dgate.shape
    D = h.shape[1]
    tm, tk = _tile(F, 1408), _tile(T, 1024, 16)
    nf, nk = F // tm, T // tk

    def body(dgate_ref, dup_ref, h_ref, o_ref, acc_ref):
        i, k = pl.program_id(0), pl.program_id(1)

        def accumulate(part):
            @pl.when(k == 0)
            def _():
                acc_ref[...] = part

            @pl.when(k > 0)
            def _():
                acc_ref[...] += part

        @pl.when(i < nf)
        def _():
            accumulate(lax.dot_general(dgate_ref[...], h_ref[...], _TN, preferred_element_type=F32))

        @pl.when(i >= nf)
        def _():
            accumulate(lax.dot_general(dup_ref[...], h_ref[...], _TN, preferred_element_type=F32))

        @pl.when(k == nk - 1)
        def _():
            o_ref[...] = acc_ref[...].astype(BF16)

    return pl.pallas_call(
        body, name=name,
        out_shape=jax.ShapeDtypeStruct((2 * F, D), BF16),
        grid=(2 * nf, nk),
        in_specs=[pl.BlockSpec((tk, tm), lambda i, k: (jnp.where(i < nf, k, nk - 1), jnp.minimum(i, nf - 1))),
                  pl.BlockSpec((tk, tm), lambda i, k: (jnp.where(i < nf, 0, k), jnp.maximum(i - nf, 0))),
                  pl.BlockSpec((tk, D), lambda i, k: (k, 0))],
        out_specs=pl.BlockSpec((tm, D), lambda i, k: (i, 0)),
        scratch_shapes=[pltpu.VMEM((tm, D), F32)],
        compiler_params=_params("parallel", "arbitrary"),
    )(dgate, dup, h)


def _dep_spec(dep):
    return [] if dep is None else [pl.BlockSpec(dep.shape, lambda *_: (0,) * dep.ndim)]


def _rope_tables(positions):
    half = QK_ROPE // 2
    inv_freq = ROPE_THETA ** (-jnp.arange(0, QK_ROPE, 2, dtype=F32) / QK_ROPE)
    ang = positions.astype(F32)[:, None] * inv_freq
    cos, sin = jnp.cos(ang), jnp.sin(ang)
    z = jnp.zeros_like(cos)
    zz = jnp.zeros((positions.shape[0], LANE - QK_ROPE), F32)
    c = jnp.concatenate([cos, cos, zz], axis=1)
    sa = jnp.concatenate([z, sin, zz], axis=1)
    sb = jnp.concatenate([-sin, z, zz], axis=1)
    return c, sa, sb


def _rotate(seg, c, sa, sb, sign):
    half = QK_ROPE // 2
    mix = pltpu.roll(seg, half, 1) * sa + pltpu.roll(seg, LANE - half, 1) * sb
    return seg * c + mix if sign > 0 else seg * c - mix


def _mixer_in(h, wa, wuq, wukv, gq, gkv, tabs, *, name, dep=None):
    T, D = h.shape
    HQ, QL = wuq.shape
    KVL = wukv.shape[0]
    H = HQ // HEAD_W
    o_q, o_kv, o_kr = POOL_DIM, POOL_DIM + QL, POOL_DIM + QL + KVL
    PA = o_kr + LANE
    assert wa.shape[0] >= PA
    tm = _tile(T, 512, 16)

    def body(h_ref, wa_ref, wuq_ref, wukv_ref, gq_ref, gkv_ref, c_ref, sa_ref, sb_ref, *rest):
        xp_ref, ql_ref, kvl_ref, qn_ref, kvn_ref, q_ref, kv_ref, kr_ref = rest[-8:]
        proj = lax.dot_general(h_ref[...], wa_ref[...], _NT, preferred_element_type=F32)
        xp_ref[...] = proj[:, :POOL_DIM]
        ql = proj[:, o_q:o_kv]
        kvl = proj[:, o_kv:o_kr]
        ql_ref[...] = ql
        kvl_ref[...] = kvl
        qn = (ql * _rstd(ql) * gq_ref[...]).astype(BF16)
        kvn = (kvl * _rstd(kvl) * gkv_ref[...]).astype(BF16)
        qn_ref[...] = qn
        kvn_ref[...] = kvn
        c, sa, sb = c_ref[...], sa_ref[...], sb_ref[...]
        q = lax.dot_general(qn, wuq_ref[...], _NT, preferred_element_type=F32)
        for hh in range(H):
            base = hh * HEAD_W
            q_ref[:, base:base + QK_NOPE] = q[:, base:base + QK_NOPE].astype(BF16)
            q_ref[:, base + QK_NOPE:base + HEAD_W] = _rotate(
                q[:, base + QK_NOPE:base + HEAD_W], c, sa, sb, 1).astype(BF16)
        kv_ref[...] = jnp.dot(kvn, wukv_ref[...], preferred_element_type=F32).astype(BF16)
        kr_ref[...] = _rotate(proj[:, o_kr:o_kr + LANE], c, sa, sb, 1).astype(BF16)

    def row(w):
        return pl.BlockSpec((tm, w), lambda i: (i, 0))

    def whole(arr):
        return pl.BlockSpec(arr.shape, lambda i: (0,) * arr.ndim)

    gq2, gkv2 = gq.reshape(1, QL), gkv.reshape(1, KVL)
    outs = [(POOL_DIM, F32), (QL, F32), (KVL, F32), (QL, BF16), (KVL, BF16), (HQ, BF16), (HQ, BF16), (LANE, BF16)]
    return pl.pallas_call(
        body, name=name,
        out_shape=tuple(jax.ShapeDtypeStruct((T, w), dt) for w, dt in outs),
        grid=(T // tm,),
        in_specs=[row(D), pl.BlockSpec((PA, D), lambda i: (0, 0)), whole(wuq), whole(wukv), whole(gq2), whole(gkv2),
                  row(LANE), row(LANE), row(LANE)] + _dep_spec(dep),
        out_specs=tuple(row(w) for w, _ in outs),
        compiler_params=_params("parallel"),
    )(h, wa, wuq, wukv, gq2, gkv2, *tabs, *([] if dep is None else [dep]))


def _mixer_in_bwd(dq, dkv, dkr, ql, kvl, dxp, dgl, x, dxo, win_t, wgate_t, wuq, wukv, g_mix, gq, gkv, tabs, *, name):
    T, HQ = dq.shape
    D = x.shape[1]
    QL, KVL = wuq.shape[1], wukv.shape[0]
    H = HQ // HEAD_W
    PA = POOL_DIM + QL + KVL + LANE
    o_q, o_kv, o_kr = POOL_DIM, POOL_DIM + QL, POOL_DIM + QL + KVL
    tm = _tile(T, 256, 16)

    def norm_bwd(lat, gain, dn):
        r = _rstd(lat)
        xhat = lat * r
        dxh = dn * gain
        dlat = r * (dxh - xhat * jnp.mean(dxh * xhat, axis=-1, keepdims=True))
        return dlat, jnp.sum(dn * xhat, axis=0, keepdims=True)

    def body(dq_ref, dkv_ref, dkr_ref, ql_ref, kvl_ref, dxp_ref, dgl_ref, x_ref, dxo_ref, win_ref, wgate_ref,
             wuq_ref, wukv_ref, gmix_ref, gq_ref, gkv_ref, c_ref, sa_ref, sb_ref,
             dproj_ref, dqp_ref, dgq_ref, dgkv_ref, dx_ref, dgmix_ref):
        c, sa, sb = c_ref[...], sa_ref[...], sb_ref[...]
        for hh in range(H):
            base = hh * HEAD_W
            dqp_ref[:, base:base + QK_NOPE] = dq_ref[:, base:base + QK_NOPE]
            dqp_ref[:, base + QK_NOPE:base + HEAD_W] = _rotate(
                dq_ref[:, base + QK_NOPE:base + HEAD_W].astype(F32), c, sa, sb, -1).astype(BF16)
        dqn = jnp.dot(dqp_ref[...], wuq_ref[...], preferred_element_type=F32)
        dkvn = lax.dot_general(dkv_ref[...], wukv_ref[...], _NT, preferred_element_type=F32)
        dql, dgq = norm_bwd(ql_ref[...], gq_ref[...], dqn)
        dkvl, dgkv = norm_bwd(kvl_ref[...], gkv_ref[...], dkvn)
        dproj_ref[:, :POOL_DIM] = dxp_ref[...].astype(BF16)
        dproj_ref[:, o_q:o_kv] = dql.astype(BF16)
        dproj_ref[:, o_kv:o_kr] = dkvl.astype(BF16)
        dproj_ref[:, o_kr:PA] = _rotate(dkr_ref[...], c, sa, sb, -1).astype(BF16)

        dh = (jnp.dot(dproj_ref[...], win_ref[...], preferred_element_type=F32)
              + jnp.dot(dgl_ref[...], wgate_ref[...], preferred_element_type=F32))
        x = x_ref[...]
        r = _rstd(x)
        xhat = x * r
        dxh = dh * gmix_ref[...]
        dx_ref[...] = dxo_ref[...] + r * (dxh - xhat * jnp.mean(dxh * xhat, axis=-1, keepdims=True))
        dgmix = jnp.sum(dh * xhat, axis=0, keepdims=True)

        @pl.when(pl.program_id(0) == 0)
        def _():
            dgq_ref[...] = dgq
            dgkv_ref[...] = dgkv
            dgmix_ref[...] = dgmix

        @pl.when(pl.program_id(0) > 0)
        def _():
            dgq_ref[...] += dgq
            dgkv_ref[...] += dgkv
            dgmix_ref[...] += dgmix

    def row(w):
        return pl.BlockSpec((tm, w), lambda i: (i, 0))

    def resident(arr, rows=None):
        shape = arr.shape if rows is None else (rows, arr.shape[1])
        return pl.BlockSpec(shape, lambda i: (0, 0), pipeline_mode=pl.Buffered(1))

    gmix2, gq2, gkv2 = g_mix.reshape(1, D), gq.reshape(1, QL), gkv.reshape(1, KVL)
    vec = pl.BlockSpec((1, D), lambda i: (0, 0))
    vq, vkv = pl.BlockSpec((1, QL), lambda i: (0, 0)), pl.BlockSpec((1, KVL), lambda i: (0, 0))
    return pl.pallas_call(
        body, name=name,
        out_shape=(jax.ShapeDtypeStruct((T, PA), BF16), jax.ShapeDtypeStruct((T, HQ), BF16),
                   jax.ShapeDtypeStruct((1, QL), F32), jax.ShapeDtypeStruct((1, KVL), F32),
                   jax.ShapeDtypeStruct((T, D), F32), jax.ShapeDtypeStruct((1, D), F32)),
        grid=(T // tm,),
        in_specs=[row(HQ), row(HQ), row(LANE), row(QL), row(KVL), row(POOL_DIM), row(2 * D), row(D), row(D),
                  resident(win_t, PA), resident(wgate_t), resident(wuq), resident(wukv),
                  vec, vq, vkv, row(LANE), row(LANE), row(LANE)],
        out_specs=(row(PA), row(HQ), vq, vkv, row(D), vec),
        compiler_params=_params("arbitrary"),
    )(dq, dkv, dkr, ql, kvl, dxp, dgl, x, dxo, win_t, wgate_t, wuq, wukv, gmix2, gq2, gkv2, *tabs)


def _pool_groups(x_of, S):
    row = lax.broadcasted_iota(jnp.int32, (S, POOL_G), 0)
    for g, w in enumerate(POOL_WINDOWS):
        x = x_of(g)
        s = x
        d = 1
        while d < w:
            s = s + jnp.where(row >= d, pltpu.roll(s, d, 0), 0.0)
            d *= 2
        cnt = jnp.minimum(row + 1, w).astype(F32)
        yield g, w, x, s / cnt - x, cnt, row


def _pool_fwd(xp, maps, scale, *, S, name):
    T = xp.shape[0]

    def body(xp_ref, maps_ref, scale_ref, ms_ref):
        for g, _, _, pooled, _, _ in _pool_groups(lambda g: xp_ref[:, g * POOL_G:(g + 1) * POOL_G], S):
            mixed = jnp.dot(pooled.astype(BF16), maps_ref[g].astype(BF16), preferred_element_type=F32)
            ms_ref[:, g * POOL_G:(g + 1) * POOL_G] = (mixed * scale_ref[:, g * POOL_G:(g + 1) * POOL_G]).astype(BF16)

    return pl.pallas_call(
        body, name=name,
        out_shape=jax.ShapeDtypeStruct((T, POOL_DIM), BF16),
        grid=(T // S,),
        in_specs=[pl.BlockSpec((S, POOL_DIM), lambda b: (b, 0)),
                  pl.BlockSpec(maps.shape, lambda b: (0, 0, 0)),
                  pl.BlockSpec((1, POOL_DIM), lambda b: (0, 0))],
        out_specs=pl.BlockSpec((S, POOL_DIM), lambda b: (b, 0)),
        compiler_params=_params("parallel"),
    )(xp, maps, scale.reshape(1, POOL_DIM))


def _pool_bwd(xp, dms, maps, scale, *, S, name):
    T = xp.shape[0]

    def body(xp_ref, dms_ref, maps_ref, scale_ref, dxp_ref, dmaps_ref, dscale_ref):
        first = pl.program_id(0) == 0
        for g, w, _, pooled, cnt, row in _pool_groups(lambda g: xp_ref[:, g * POOL_G:(g + 1) * POOL_G], S):
            cols = slice(g * POOL_G, (g + 1) * POOL_G)
            pooled_b = pooled.astype(BF16)
            maps_b = maps_ref[g].astype(BF16)
            mixed = jnp.dot(pooled_b, maps_b, preferred_element_type=F32)
            dms = dms_ref[:, cols]
            dscale = jnp.sum(dms * mixed, axis=0, keepdims=True)
            dmixed = (dms * scale_ref[:, cols]).astype(BF16)
            dmaps = lax.dot_general(pooled_b, dmixed, (((0,), (0,)), ((), ())), preferred_element_type=F32)
            dpooled = lax.dot_general(dmixed, maps_b, (((1,), (1,)), ((), ())), preferred_element_type=F32)
            z = dpooled / cnt
            d = 1
            while d < w:
                z = z + jnp.where(row < S - d, pltpu.roll(z, S - d, 0), 0.0)
                d *= 2
            dxp_ref[:, cols] = z - dpooled

            @pl.when(first)
            def _():
                dmaps_ref[g] = dmaps
                dscale_ref[:, cols] = dscale

            @pl.when(jnp.logical_not(first))
            def _():
                dmaps_ref[g] += dmaps
                dscale_ref[:, cols] += dscale

    seq = pl.BlockSpec((S, POOL_DIM), lambda b: (b, 0))
    maps_spec = pl.BlockSpec(maps.shape, lambda b: (0, 0, 0))
    vec = pl.BlockSpec((1, POOL_DIM), lambda b: (0, 0))
    return pl.pallas_call(
        body, name=name,
        out_shape=(jax.ShapeDtypeStruct((T, POOL_DIM), F32), jax.ShapeDtypeStruct(maps.shape, F32),
                   jax.ShapeDtypeStruct((1, POOL_DIM), F32)),
        grid=(T // S,),
        in_specs=[seq, seq, maps_spec, vec],
        out_specs=(seq, maps_spec, vec),
        compiler_params=_params("arbitrary"),
    )(xp, dms, maps, scale.reshape(1, POOL_DIM))


def _causal_mask(s, t):
    r = lax.broadcasted_iota(jnp.int32, (t, t), 0)
    c = lax.broadcasted_iota(jnp.int32, (t, t), 1)
    return jnp.where(r >= c, s, MASK_VALUE)


_NT = (((1,), (1,)), ((), ()))
_TN = (((0,), (0,)), ((), ()))


def _attn_fwd(q, kv, kr, *, S, name):
    T, HQ = q.shape
    H = HQ // HEAD_W
    B = T // S
    t = _tile(S, ATTN_TILE)
    n = S // t

    def body(q_ref, k_ref, v_ref, kr_ref, o_ref, lse_ref, kcat, vcat):
        kcat[:, :QK_NOPE] = k_ref[...]
        kcat[:, QK_NOPE:] = kr_ref[...]
        vcat[:, :V_DIM] = v_ref[...]
        vcat[:, V_DIM:] = jnp.ones((S, HEAD_W - V_DIM), BF16)
        for i in range(n):
            rows = slice(i * t, (i + 1) * t)
            qt = q_ref[rows, :]
            m = jnp.full((t, 1), MASK_VALUE, F32)
            acc = jnp.zeros((t, HEAD_W), F32)
            for j in range(i + 1):
                cols = slice(j * t, (j + 1) * t)
                s = lax.dot_general(qt, kcat[cols, :], _NT, preferred_element_type=F32) * ATTN_SCALE_LOG2
                if j == i:
                    s = _causal_mask(s, t)
                m_new = jnp.maximum(m, jnp.max(s, axis=1, keepdims=True))
                p = jnp.exp2(s - m_new)
                acc = jnp.exp2(m - m_new) * acc + jnp.dot(p.astype(BF16), vcat[cols, :], preferred_element_type=F32)
                m = m_new
            l = acc[:, V_DIM:V_DIM + 1]
            o_ref[rows, :] = (acc[:, :V_DIM] / l).astype(BF16)
            lse_ref[rows, :] = jnp.broadcast_to(m + jnp.log2(l), (t, LANE))

    seq_h = pl.BlockSpec((S, LANE), lambda b, h: (b, h))
    return pl.pallas_call(
        body, name=name,
        out_shape=(jax.ShapeDtypeStruct((T, H * V_DIM), BF16), jax.ShapeDtypeStruct((T, H * LANE), F32)),
        grid=(B, H),
        in_specs=[pl.BlockSpec((S, HEAD_W), lambda b, h: (b, h)),
                  pl.BlockSpec((S, QK_NOPE), lambda b, h: (b, 2 * h)),
                  pl.BlockSpec((S, V_DIM), lambda b, h: (b, 2 * h + 1)),
                  pl.BlockSpec((S, LANE), lambda b, h: (b, 0))],
        out_specs=(seq_h, seq_h),
        scratch_shapes=[pltpu.VMEM((S, HEAD_W), BF16), pltpu.VMEM((S, HEAD_W), BF16)],
        compiler_params=_params("parallel", "parallel"),
    )(q, kv, kv, kr)


def _attn_bwd(q, kv, kr, o, do, lse, *, S, name):
    T, HQ = q.shape
    H = HQ // HEAD_W
    B = T // S
    t = _tile(S, ATTN_TILE)
    n = S // t

    def body(q_ref, k_ref, v_ref, kr_ref, o_ref, do_ref, lse_ref, dq_ref, dkv_ref, dkr_ref, kcat, dq_acc):
        @pl.when(pl.program_id(1) == 0)
        def _():
            dkr_ref[...] = jnp.zeros_like(dkr_ref)

        kcat[:, :QK_NOPE] = k_ref[...]
        kcat[:, QK_NOPE:] = kr_ref[...]
        delta = [jnp.sum(do_ref[i * t:(i + 1) * t, :].astype(F32) * o_ref[i * t:(i + 1) * t, :].astype(F32),
                         axis=1, keepdims=True) for i in range(n)]
        for j in range(n):
            cols = slice(j * t, (j + 1) * t)
            kc = kcat[cols, :]
            vt = v_ref[cols, :]
            dk = jnp.zeros((t, HEAD_W), F32)
            dv = jnp.zeros((t, V_DIM), F32)
            for i in range(j, n):
                rows = slice(i * t, (i + 1) * t)
                qt = q_ref[rows, :]
                dot_ = do_ref[rows, :]
                s = lax.dot_general(qt, kc, _NT, preferred_element_type=F32) * ATTN_SCALE_LOG2
                if i == j:
                    s = _causal_mask(s, t)
                p = jnp.exp2(s - lse_ref[rows, :][:, :1])
                dv = dv + lax.dot_general(p.astype(BF16), dot_, _TN, preferred_element_type=F32)
                dp = lax.dot_general(dot_, vt, _NT, preferred_element_type=F32)
                ds = (p * (dp - delta[i]) * ATTN_SCALE).astype(BF16)
                dk = dk + lax.dot_general(ds, qt, _TN, preferred_element_type=F32)
                dq_part = jnp.dot(ds, kc, preferred_element_type=F32)
                if j == 0:
                    dq_acc[rows, :] = dq_part
                else:
                    dq_acc[rows, :] += dq_part
            dkv_ref[cols, :QK_NOPE] = dk[:, :QK_NOPE].astype(BF16)
            dkv_ref[cols, QK_NOPE:] = dv.astype(BF16)
            dkr_ref[cols, :] += dk[:, QK_NOPE:]
        dq_ref[...] = dq_acc[...].astype(BF16)

    seq_q = pl.BlockSpec((S, HEAD_W), lambda b, h: (b, h))
    seq_h = pl.BlockSpec((S, LANE), lambda b, h: (b, h))
    seq_shared = pl.BlockSpec((S, LANE), lambda b, h: (b, 0))
    return pl.pallas_call(
        body, name=name,
        out_shape=(jax.ShapeDtypeStruct((T, HQ), BF16), jax.ShapeDtypeStruct((T, HQ), BF16),
                   jax.ShapeDtypeStruct((T, LANE), F32)),
        grid=(B, H),
        in_specs=[seq_q,
                  pl.BlockSpec((S, QK_NOPE), lambda b, h: (b, 2 * h)),
                  pl.BlockSpec((S, V_DIM), lambda b, h: (b, 2 * h + 1)),
                  seq_shared, seq_h, seq_h, seq_h],
        out_specs=(seq_q, seq_q, seq_shared),
        scratch_shapes=[pltpu.VMEM((S, HEAD_W), BF16), pltpu.VMEM((S, HEAD_W), F32)],
        compiler_params=_params("parallel", "arbitrary"),
    )(q, kv, kv, kr, o, do, lse)


def _merge_out(h, ms, o, x, wgate, bgate, wpp, wap, wout, next_gain, *, name):
    T, D = x.shape
    tm = _tile(T, 256, 16)

    def body(h_ref, ms_ref, o_ref, x_ref, wgate_ref, bgate_ref, wpp_ref, wap_ref, wout_ref, ng_ref,
             gates_ref, ba_ref, bb_ref, merged_ref, xn_ref, hn_ref):
        logits = lax.dot_general(h_ref[...], wgate_ref[...], _NT, preferred_element_type=F32) + bgate_ref[...]
        gates = jax.nn.sigmoid(logits)
        ba = jnp.dot(ms_ref[...], wpp_ref[...], preferred_element_type=F32)
        bb = jnp.dot(o_ref[...], wap_ref[...], preferred_element_type=F32)
        merged = (gates[:, :D] * ba + gates[:, D:] * bb).astype(BF16)
        gates_ref[...] = gates.astype(BF16)
        ba_ref[...] = ba.astype(BF16)
        bb_ref[...] = bb.astype(BF16)
        merged_ref[...] = merged
        xn = x_ref[...] + jnp.dot(merged, wout_ref[...], preferred_element_type=F32)
        xn_ref[...] = xn
        hn_ref[...] = (xn * _rstd(xn) * ng_ref[...]).astype(BF16)

    def row(w):
        return pl.BlockSpec((tm, w), lambda i: (i, 0))

    def whole(arr):
        return pl.BlockSpec(arr.shape, lambda i: (0,) * arr.ndim)

    bg2, ng2 = bgate.reshape(1, 2 * D), next_gain.reshape(1, D)
    act = jax.ShapeDtypeStruct((T, D), BF16)
    return pl.pallas_call(
        body, name=name,
        out_shape=(jax.ShapeDtypeStruct((T, 2 * D), BF16), act, act, act, jax.ShapeDtypeStruct((T, D), F32), act),
        grid=(T // tm,),
        in_specs=[row(D), row(ms.shape[1]), row(o.shape[1]), row(D), whole(wgate), whole(bg2), whole(wpp),
                  whole(wap), whole(wout), whole(ng2)],
        out_specs=(row(2 * D), row(D), row(D), row(D), row(D), row(D)),
        compiler_params=_params("parallel"),
    )(h, ms, o, x, wgate, bg2, wpp, wap, wout, ng2)


def _merge_bwd(dxo, wout, wpp, wap, gates, ba, bb, *, name, dep=None):
    T, D = dxo.shape
    tm = _tile(T, 512, 16)

    def body(dxo_ref, wout_ref, wpp_ref, wap_ref, gates_ref, ba_ref, bb_ref, *rest):
        dba_ref, dbb_ref, dgl_ref, dbg_ref, dms_ref, do_ref = rest[-6:]
        dm = lax.dot_general(dxo_ref[...].astype(BF16), wout_ref[...], _NT, preferred_element_type=F32)
        ga = gates_ref[:, :D].astype(F32)
        gb = gates_ref[:, D:].astype(F32)
        dba = (dm * ga).astype(BF16)
        dbb = (dm * gb).astype(BF16)
        dba_ref[...] = dba
        dbb_ref[...] = dbb
        dms_ref[...] = lax.dot_general(dba, wpp_ref[...], _NT, preferred_element_type=F32)
        do_ref[...] = lax.dot_general(dbb, wap_ref[...], _NT, preferred_element_type=F32).astype(BF16)
        dgl_a = dm * ba_ref[...].astype(F32) * (ga * (1.0 - ga))
        dgl_b = dm * bb_ref[...].astype(F32) * (gb * (1.0 - gb))
        dgl_ref[:, :D] = dgl_a.astype(BF16)
        dgl_ref[:, D:] = dgl_b.astype(BF16)
        sa = jnp.sum(dgl_a, axis=0, keepdims=True)
        sb = jnp.sum(dgl_b, axis=0, keepdims=True)

        @pl.when(pl.program_id(0) == 0)
        def _():
            dbg_ref[:, :D] = sa
            dbg_ref[:, D:] = sb

        @pl.when(pl.program_id(0) > 0)
        def _():
            dbg_ref[:, :D] += sa
            dbg_ref[:, D:] += sb

    def row(w):
        return pl.BlockSpec((tm, w), lambda i: (i, 0))

    def whole(arr):
        return pl.BlockSpec(arr.shape, lambda i: (0, 0))

    P, HV = wpp.shape[0], wap.shape[0]
    act = jax.ShapeDtypeStruct((T, D), BF16)
    return pl.pallas_call(
        body, name=name,
        out_shape=(act, act, jax.ShapeDtypeStruct((T, 2 * D), BF16), jax.ShapeDtypeStruct((1, 2 * D), F32),
                   jax.ShapeDtypeStruct((T, P), F32), jax.ShapeDtypeStruct((T, HV), BF16)),
        grid=(T // tm,),
        in_specs=[row(D), whole(wout), whole(wpp), whole(wap), row(2 * D), row(D), row(D)] + _dep_spec(dep),
        out_specs=(row(D), row(D), row(2 * D), pl.BlockSpec((1, 2 * D), lambda i: (0, 0)), row(P), row(HV)),
        compiler_params=_params("arbitrary"),
    )(dxo, wout, wpp, wap, gates, ba, bb, *([] if dep is None else [dep]))


def _ffn_fwd(x, h, w, tag, next_gain, dep=None):
    gate, up, a, xn, hn = _ffn_fwd_core(x, h, w["up_t"], w["wd"], next_gain, alpha=0.5,
                                        name=f"{tag}_fwd" if next_gain is not None else f"{tag}_fwd_last", dep=dep)
    return xn, hn, (x, h, gate, up, a)


def _ffn_bwd(dxo, gain, w, saved, tag, dep=None):
    x, h, gate, up, a = saved
    F = gate.shape[1]
    dgate, dup, dx, dgain = _ffn_bwd_core(dxo, w["wd"], w["up_t"], gate, up, x, gain, alpha=0.5,
                                          name=f"{tag}_bwd_core", dep=dep)
    dwd = _mm(a, dxo, ta=True, alpha=0.5, out_dtype=BF16, name=f"{tag}_dwd", tm=1408, tn=1024, tk=1024)
    dup_t = _ffn_dw_up(dgate, dup, h, name=f"{tag}_dw_up")
    return dx, dgain, dup_t, dwd


def _mixer_fwd(x, h, p, w, tabs, S, next_gain, dep=None):
    xp, ql, kvl, qn, kvn, q, kv, kr = _mixer_in(h, w["win_t"], w["wuq_t"], w["wukv"], p["q_latent_norm"],
                                                 p["kv_latent_norm"], tabs, name="mix_in", dep=dep)
    ms = _pool_fwd(xp, p["pool_maps"], p["pool_scale"], S=S, name="pool_fwd")
    o, lse = _attn_fwd(q, kv, kr, S=S, name="attn_fwd")
    gates, ba, bb, merged, xn, hn = _merge_out(h, ms, o, x, w["wgate_t"], p["b_gate"], w["wpp"], w["wap"], w["wout"],
                                               next_gain, name="merge_out")
    return xn, hn, (x, h, xp, ql, kvl, qn, kvn, q, kv, kr, ms, o, lse, gates, ba, bb, merged)


def _mixer_bwd(dxo, p, w, tabs, saved, S, dep=None):
    x, h, xp, ql, kvl, qn, kvn, q, kv, kr, ms, o, lse, gates, ba, bb, merged = saved
    dba, dbb, dgl, dbg, dms, do = _merge_bwd(dxo, w["wout"], w["wpp"], w["wap"], gates, ba, bb, name="merge_bwd",
                                             dep=dep)
    g = {}
    g["wout"], g["wpp"], g["wap"] = _dw_multi([(merged, dxo), (ms, dba), (o, dbb)], name="d_w_merge")
    dxp, g["pool_maps"], g["pool_scale"] = _pool_bwd(xp, dms, p["pool_maps"], p["pool_scale"], S=S, name="pool_bwd")
    dq, dkv, dkr = _attn_bwd(q, kv, kr, o, do, lse, S=S, name="attn_bwd")
    dproj, dqp, g["q_latent_norm"], g["kv_latent_norm"], dx, g["norm_mix"] = _mixer_in_bwd(
        dq, dkv, dkr, ql, kvl, dxp, dgl, x, dxo, w["win_t"], w["wgate_t"], w["wuq_t"], w["wukv"],
        p["norm_mix"], p["q_latent_norm"], p["kv_latent_norm"], tabs, name="mix_in_bwd")
    g["wuq_t"], g["wukv"] = _dw_multi([(dqp, qn), (kvn, dkv)], name="d_w_qkv", tk=1024)
    g["wa_t"], g["wgate_t"] = _dw_multi([(dproj, h), (dgl, h)], name="d_w_in")
    g["b_gate"] = dbg
    return dx, g


BIG = ("ffn1_up", "ffn1_down", "w_in", "w_pool_proj", "w_uq", "w_ukv", "w_attn_proj", "w_out", "ffn2_up", "ffn2_down")
SMALL = ("norm_ffn1", "norm_mix", "b_gate", "pool_maps", "pool_scale", "q_latent_norm", "kv_latent_norm", "norm_ffn2")
PACKED = ("w_pool_proj", "w_uq", "w_ukv")
TRANSPOSED = ("ffn1_up", "ffn2_up", "w_in", "w_uq")
COL_SHARDED = ("w_pool_proj", "w_ukv")
QK_HEAD = QK_NOPE + QK_ROPE


def _rows(stacked):
    n, r, c = stacked.shape
    return stacked.reshape(n * r, c)


def _cols(stacked):
    n, k, c = stacked.shape
    return stacked.transpose(1, 0, 2).reshape(k, n * c)


FFN1_PART = ("ffn1_up", "ffn1_down")
MIXER_PART = ("w_in", "w_attn_proj", "w_out") + PACKED
FFN2_PART = ("ffn2_up", "ffn2_down")


def _kernel_weights(stacked):
    full = {}
    for tag in ("ffn1", "ffn2"):
        if tag + "_up" in stacked:
            full[tag] = {"up_t": _rows(stacked[tag + "_up"]), "wd": _rows(stacked[tag + "_down"])}
    if "w_in" in stacked:
        win_t = _rows(stacked["w_in"])
        D = win_t.shape[1]
        wuq_t = _rows(stacked["w_uq"])
        QL = wuq_t.shape[1]
        H = wuq_t.shape[0] // QK_HEAD
        wuq_t = jnp.pad(wuq_t.reshape(H, QK_HEAD, QL), ((0, 0), (0, HEAD_W - QK_HEAD), (0, 0)))
        full.update({"win_t": win_t, "wgate_t": win_t[win_t.shape[0] - 2 * D:], "wuq_t": wuq_t.reshape(H * HEAD_W, QL),
                     "wukv": _cols(stacked["w_ukv"]), "wpp": _cols(stacked["w_pool_proj"]),
                     "wap": _rows(stacked["w_attn_proj"]), "wout": _rows(stacked["w_out"])})
    return full


def _split_rows(full):
    return full.reshape(N_DEV, full.shape[0] // N_DEV, full.shape[1])


def _split_cols(full):
    k, cols = full.shape
    return full.reshape(k, N_DEV, cols // N_DEV).transpose(1, 0, 2)


def _mixer_grads_stacked(g):
    n_a = g["wa_t"].shape[0] - (LANE - QK_ROPE)
    HQ, QL = g["wuq_t"].shape
    H = HQ // HEAD_W
    wuq_t = g["wuq_t"].reshape(H, HEAD_W, QL)[:, :QK_HEAD].reshape(H * QK_HEAD, QL)
    return {"w_in": _split_rows(jnp.concatenate([g["wa_t"][:n_a], g["wgate_t"]], axis=0)),
            "w_uq": _split_rows(wuq_t),
            "w_pool_proj": _split_cols(g["wpp"]), "w_ukv": _split_cols(g["wukv"]),
            "w_attn_proj": _split_rows(g["wap"]), "w_out": _split_rows(g["wout"])}


def _mesh_place():
    x, y, c = lax.axis_index("x"), lax.axis_index("y"), lax.axis_index("c")
    chips = [(1 - x, y), (x, 1 - y), (1 - x, 1 - y)]
    return x, y, c, chips


HBM = pl.BlockSpec(memory_space=pltpu.HBM)
SEMAPHORES = pl.BlockSpec(memory_space=pltpu.SEMAPHORE)
DATAFLOW = pltpu.SideEffectType.DATAFLOW_SIDE_EFFECTING
GATHER_PEERS = 4
SCATTER_PEERS = 7


def _in_hbm(a):
    return pltpu.with_memory_space_constraint(a, pltpu.HBM)


def _gather_plan(src_refs, land_refs):
    x, y, c, chips = _mesh_place()
    me = 4 * x + 2 * y + c
    targets = [(x, y, 1 - c)] + [(cx, cy, c) for cx, cy in chips]
    return [(s, land.at[me], to) for s, land in zip(src_refs, land_refs) for to in targets]


def _scatter_plan(src_refs, land_refs):
    x, y, c, _ = _mesh_place()
    peers = [(x, y, 1 - c), (1 - x, y, c), (x, 1 - y, c), (1 - x, 1 - y, c),
             (1 - x, y, 1 - c), (x, 1 - y, 1 - c), (1 - x, 1 - y, 1 - c)]
    return [(s.at[4 * px + 2 * py + pc], land.at[k], (px, py, pc))
            for s, land in zip(src_refs, land_refs) for k, (px, py, pc) in enumerate(peers)]


def _descriptors(plan, src_refs, land_refs, send_sems, recv_sems):
    return [pltpu.make_async_remote_copy(src_ref=s, dst_ref=d, send_sem=send_sems.at[k], recv_sem=recv_sems.at[k],
                                         device_id=to, device_id_type=MESH)
            for k, (s, d, to) in enumerate(plan(src_refs, land_refs))]


def _exchange(srcs, land_shapes, plan, per_src, *, name):
    n = len(srcs)

    def body(*refs):
        copies = _descriptors(plan, refs[:n], refs[n:2 * n], refs[2 * n], refs[2 * n + 1])
        for cp in copies:
            cp.start()
        for cp in copies:
            cp.wait()

    return pl.pallas_call(
        body, name=name,
        out_shape=tuple(jax.ShapeDtypeStruct(shape, s.dtype) for shape, s in zip(land_shapes, srcs)),
        in_specs=[ANY] * n, out_specs=(ANY,) * n,
        scratch_shapes=[pltpu.SemaphoreType.DMA((per_src * n,)), pltpu.SemaphoreType.DMA((per_src * n,))],
    )(*srcs)


FORWARD_COPIES = 4


def _forward_slots():
    x, y, c, chips = _mesh_place()
    return [4 * cx + 2 * cy + c for cx, cy in chips] + [4 * x + 2 * y + (1 - c)], (x, y, 1 - c)


def _forward_plan(src_refs, land_refs):
    slots, sibling = _forward_slots()
    return [(land.at[s], land.at[s], sibling) for land in land_refs for s in slots]


def _gather_all_plan(src_refs, land_refs):
    x, y, c, _ = _mesh_place()
    me = 4 * x + 2 * y + c
    peers = [(x, y, 1 - c), (1 - x, y, c), (x, 1 - y, c), (1 - x, 1 - y, c),
             (1 - x, y, 1 - c), (x, 1 - y, 1 - c), (1 - x, 1 - y, 1 - c)]
    return [(s, land.at[me], to) for s, land in zip(src_refs, land_refs) for to in peers]


def _exchange_start(srcs, lands, plan, n_copies, *, name):
    ns, n = len(srcs), len(srcs) + len(lands)

    def body(*refs):
        for cp in _descriptors(plan, refs[:ns], refs[ns:n], refs[n], refs[n + 1]):
            cp.start()
        refs[-1][...] = jnp.zeros_like(refs[-1])

    sems = pltpu.SemaphoreType.DMA((n_copies,))
    out = pl.pallas_call(
        body, name=name,
        out_shape=(sems, sems, *[pltpu.HBM(a.shape, a.dtype) for a in srcs + lands],
                   jax.ShapeDtypeStruct((8, LANE), F32)),
        in_specs=(HBM,) * n,
        out_specs=(SEMAPHORES, SEMAPHORES, *[HBM] * n, pl.BlockSpec(memory_space=pltpu.VMEM)),
        input_output_aliases={i: 2 + i for i in range(n)},
        compiler_params=pltpu.CompilerParams(has_side_effects=DATAFLOW),
    )(*[_in_hbm(a) for a in srcs + lands])
    return out[0], out[1], list(out[2:2 + ns]), list(out[2 + ns:2 + n]), out[-1]


def _exchange_wait(send_sems, recv_sems, srcs, lands, plan, after, *, name):
    ns, n = len(srcs), len(srcs) + len(lands)

    def body(*refs):
        for cp in _descriptors(plan, refs[:ns], refs[ns:n], refs[n], refs[n + 1]):
            cp.wait_send()
            cp.wait_recv()

    out = pl.pallas_call(
        body, name=name,
        out_shape=tuple(pltpu.HBM(a.shape, a.dtype) for a in srcs + lands),
        in_specs=(*[HBM] * n, SEMAPHORES, SEMAPHORES, ANY),
        out_specs=(HBM,) * n,
        input_output_aliases={i: i for i in range(n)},
        compiler_params=pltpu.CompilerParams(has_side_effects=DATAFLOW),
    )(*srcs, *lands, send_sems, recv_sems, after)
    return list(out[:ns]), list(out[ns:])


def _gather_forward(lands, *, name):
    n = len(lands)

    def body(*refs):
        in_refs, out_refs = refs[:n], refs[n:2 * n]
        token, send_sems, recv_sems = refs[2 * n:2 * n + 3]
        slots, sibling = _forward_slots()
        passed = [pltpu.make_async_remote_copy(
            src_ref=i.at[s], dst_ref=o.at[s],
            send_sem=send_sems.at[FORWARD_COPIES * b + j], recv_sem=recv_sems.at[FORWARD_COPIES * b + j],
            device_id=sibling, device_id_type=MESH)
            for b, (i, o) in enumerate(zip(in_refs, out_refs)) for j, s in enumerate(slots)]
        for cp in passed:
            cp.start()
        for cp in passed:
            cp.wait()
        token[...] = jnp.zeros_like(token)

    out = pl.pallas_call(
        body, name=name,
        out_shape=(*[jax.ShapeDtypeStruct(a.shape, a.dtype) for a in lands], jax.ShapeDtypeStruct((8, LANE), F32)),
        in_specs=[ANY] * n,
        out_specs=(*[ANY] * n, pl.BlockSpec(memory_space=pltpu.VMEM)),
        input_output_aliases={i: i for i in range(n)},
        scratch_shapes=[pltpu.SemaphoreType.DMA((FORWARD_COPIES * n,)), pltpu.SemaphoreType.DMA((FORWARD_COPIES * n,))],
    )(*lands)
    return list(out[:n]), out[n]


def _scatter_sum(parts, got, me, *, name):
    shard = parts.shape[1:]
    cols = shard[-1]
    rows = int(np.prod(shard[:-1]))
    tr = _tile(rows, 256, 16)

    def body(me_ref, p_ref, g_ref, o_ref):
        acc = p_ref[...].astype(F32)
        for k in range(SCATTER_PEERS):
            acc = acc + g_ref[k].astype(F32)
        o_ref[...] = acc

    out = pl.pallas_call(
        body, name=name,
        out_shape=jax.ShapeDtypeStruct((rows, cols), F32),
        grid_spec=pltpu.PrefetchScalarGridSpec(
            num_scalar_prefetch=1, grid=(rows // tr,),
            in_specs=[pl.BlockSpec((None, tr, cols), lambda r, me_ref: (me_ref[0], r, 0)),
                      pl.BlockSpec((SCATTER_PEERS, tr, cols), lambda r, me_ref: (0, r, 0))],
            out_specs=pl.BlockSpec((tr, cols), lambda r, me_ref: (r, 0))),
        compiler_params=_params("parallel"),
    )(me, parts.reshape(N_DEV, rows, cols), got.reshape(SCATTER_PEERS, rows, cols))
    return out.reshape(shard)


def _sum_devices(parts, *, name):
    _, R, C = parts.shape
    tr = _tile(R, 512, 8)

    def body(p_ref, o_ref):
        acc = p_ref[0]
        for d in range(1, N_DEV):
            acc = acc + p_ref[d]
        o_ref[...] = acc

    return pl.pallas_call(
        body, name=name,
        out_shape=jax.ShapeDtypeStruct((R, C), F32),
        grid=(R // tr,),
        in_specs=[pl.BlockSpec((N_DEV, tr, C), lambda r: (0, r, 0))],
        out_specs=pl.BlockSpec((tr, C), lambda r: (r, 0)),
        compiler_params=_params("parallel"),
    )(parts)


def _adamw(w, g, m, v, *, name, dep=None):
    shape = w.shape
    cols = shape[-1]
    rows = w.size // cols
    tr = _tile(rows, 256, 8)

    def body(w_ref, g_ref, m_ref, v_ref, *rest):
        d_ref, nm_ref, nv_ref = rest[-3:]
        g = g_ref[...]
        m = ADAM_B1 * m_ref[...] + (1.0 - ADAM_B1) * g
        v = ADAM_B2 * v_ref[...] + (1.0 - ADAM_B2) * jnp.square(g)
        m_hat = m / (1.0 - ADAM_B1 ** ADAM_STEP)
        v_hat = v / (1.0 - ADAM_B2 ** ADAM_STEP)
        d_ref[...] = -ADAM_LR * (m_hat / (jnp.sqrt(v_hat) + ADAM_EPS) + ADAM_WD * w_ref[...])
        nm_ref[...] = m
        nv_ref[...] = v

    spec = pl.BlockSpec((tr, cols), lambda i: (i, 0))
    out = jax.ShapeDtypeStruct((rows, cols), F32)
    d, nm, nv = pl.pallas_call(
        body, name=name,
        out_shape=(out, out, out),
        grid=(rows // tr,),
        in_specs=[spec] * 4 + _dep_spec(dep), out_specs=(spec,) * 3,
        compiler_params=_params("parallel"),
    )(*(a.reshape(rows, cols) for a in (w, g, m, v)), *([] if dep is None else [dep]))
    return d.reshape(shape), nm.reshape(shape), nv.reshape(shape)


PACK_ALIGN = 16 * LANE


def _pack(pieces, lead):
    out = []
    for p in pieces:
        keep = p.shape[:lead]
        flat = p.reshape(*keep, -1)
        pad = (-flat.shape[-1]) % PACK_ALIGN
        if pad:
            flat = jnp.pad(flat, [(0, 0)] * lead + [(0, pad)])
        out.append(flat.reshape(*keep, -1, LANE))
    return jnp.concatenate(out, axis=lead)


def _unpack(buf, shapes, lead):
    keep = buf.shape[:lead]
    out, row = [], 0
    for shape in shapes:
        size = int(np.prod(shape))
        rows = -(-size // PACK_ALIGN) * (PACK_ALIGN // LANE)
        piece = lax.slice_in_dim(buf, row, row + rows, axis=lead).reshape(*keep, rows * LANE)
        out.append(lax.slice_in_dim(piece, 0, size, axis=lead).reshape(*keep, *shape))
        row += rows
    return out


def kernel(x, positions, norm_ffn1, ffn1_up, ffn1_down, norm_mix, w_in, b_gate, pool_maps, pool_scale, w_pool_proj, q_latent_norm, w_uq, kv_latent_norm, w_ukv, w_attn_proj, w_out, norm_ffn2, ffn2_up, ffn2_down, final_norm, loss_target, m_norm_ffn1, m_ffn1_up, m_ffn1_down, m_norm_mix, m_w_in, m_b_gate, m_pool_maps, m_pool_scale, m_w_pool_proj, m_q_latent_norm, m_w_uq, m_kv_latent_norm, m_w_ukv, m_w_attn_proj, m_w_out, m_norm_ffn2, m_ffn2_up, m_ffn2_down, m_final_norm, v_norm_ffn1, v_ffn1_up, v_ffn1_down, v_norm_mix, v_w_in, v_b_gate, v_pool_maps, v_pool_scale, v_w_pool_proj, v_q_latent_norm, v_w_uq, v_kv_latent_norm, v_w_ukv, v_w_attn_proj, v_w_out, v_norm_ffn2, v_ffn2_up, v_ffn2_down, v_final_norm):
    order = ("norm_ffn1", "ffn1_up", "ffn1_down", "norm_mix", "w_in", "b_gate", "pool_maps", "pool_scale",
             "w_pool_proj", "q_latent_norm", "w_uq", "kv_latent_norm", "w_ukv", "w_attn_proj", "w_out",
             "norm_ffn2", "ffn2_up", "ffn2_down", "final_norm")
    w = dict(zip(order, (norm_ffn1, ffn1_up, ffn1_down, norm_mix, w_in, b_gate, pool_maps, pool_scale, w_pool_proj,
                         q_latent_norm, w_uq, kv_latent_norm, w_ukv, w_attn_proj, w_out, norm_ffn2, ffn2_up,
                         ffn2_down, final_norm)))
    m = dict(zip(order, (m_norm_ffn1, m_ffn1_up, m_ffn1_down, m_norm_mix, m_w_in, m_b_gate, m_pool_maps, m_pool_scale,
                         m_w_pool_proj, m_q_latent_norm, m_w_uq, m_kv_latent_norm, m_w_ukv, m_w_attn_proj, m_w_out,
                         m_norm_ffn2, m_ffn2_up, m_ffn2_down, m_final_norm)))
    v = dict(zip(order, (v_norm_ffn1, v_ffn1_up, v_ffn1_down, v_norm_mix, v_w_in, v_b_gate, v_pool_maps, v_pool_scale,
                         v_w_pool_proj, v_q_latent_norm, v_w_uq, v_kv_latent_norm, v_w_ukv, v_w_attn_proj, v_w_out,
                         v_norm_ffn2, v_ffn2_up, v_ffn2_down, v_final_norm)))
    L = norm_ffn1.shape[0]
    B, S, D = x.shape
    T = B * S

    def turned(a, n):
        return a.transpose(0, 2, 1) if n in TRANSPOSED else a

    wk, mk, vk = ({n: turned(d[n], n) for n in order} for d in (w, m, v))
    packed_shapes = [wk[n].shape[1:] for n in PACKED]
    my_slot = 4 * lax.axis_index("x") + 2 * lax.axis_index("y") + lax.axis_index("c")
    me = jnp.stack([my_slot]).astype(jnp.int32)

    def weight_blocks(l, names, token):
        zero = token[0, 0].astype(BF16)
        blocks = [wk[n][l].astype(BF16) + zero for n in names if n not in PACKED]
        if any(n in PACKED for n in names):
            blocks.append(_pack([wk[n][l].astype(BF16) + zero for n in PACKED], 0))
        return blocks

    def kernel_weights(names, lands):
        direct = [n for n in names if n not in PACKED]
        stacked = dict(zip(direct, lands))
        if len(lands) > len(direct):
            stacked.update(zip(PACKED, _unpack(lands[-1], packed_shapes, 1)))
        return _kernel_weights(stacked)

    def gather_start(l, names, token, tag):
        blocks = weight_blocks(l, names, token)
        lands = [lax.empty((N_DEV, *b.shape), b.dtype) for b in blocks]
        send_sems, recv_sems, blocks, lands, token = _exchange_start(
            blocks, lands, _gather_plan, GATHER_PEERS * len(blocks), name=f"gather_start_{tag}")
        return (send_sems, recv_sems, blocks, lands, tag), token

    def gather_wait(state, after):
        send_sems, recv_sems, blocks, lands, tag = state
        return _exchange_wait(send_sems, recv_sems, blocks, lands, _gather_plan, after, name=f"gather_wait_{tag}")[1]

    layer_part = FFN1_PART + MIXER_PART + FFN2_PART
    tabs = _rope_tables(positions.reshape(T))
    xs = x.reshape(T, D)
    h = _rms_fwd(xs, w["norm_ffn1"][0], name="first_norm")
    full, saved = [], []

    p = {n: w[n][0] for n in SMALL}
    blocks = weight_blocks(0, FFN1_PART, jnp.zeros((8, LANE), F32))
    lands = _exchange(blocks, [(N_DEV, *b.shape) for b in blocks], _gather_plan, GATHER_PEERS, name="gather_first")
    lands, token = _gather_forward(lands, name="gather_forward")
    w0 = kernel_weights(FFN1_PART, lands)
    state, token = gather_start(0, MIXER_PART, token, "0_mix")
    xs, h, s1 = _ffn_fwd(xs, h, w0["ffn1"], "ffn1", p["norm_mix"], dep=token)
    lands, token = _gather_forward(gather_wait(state, xs), name="gather_forward")
    w0.update(kernel_weights(MIXER_PART, lands))
    state, token = gather_start(0, FFN2_PART, token, "0_ffn2")
    if L > 1:
        next_state, token = gather_start(1, layer_part, token, "1")
    xs, h, s2 = _mixer_fwd(xs, h, p, w0, tabs, S, p["norm_ffn2"], dep=token)
    lands, token = _gather_forward(gather_wait(state, xs), name="gather_forward")
    w0.update(kernel_weights(FFN2_PART, lands))
    xs, h, s3 = _ffn_fwd(xs, h, w0["ffn2"], "ffn2", w["norm_ffn1"][1] if L > 1 else None, dep=token)
    if L > 1:
        lands, token = _gather_forward(gather_wait(next_state, xs), name="gather_forward")
    full.append(w0)
    saved.append((s1, s2, s3))

    for l in range(1, L):
        full.append(kernel_weights(layer_part, lands))
        more = l + 1 < L
        p = {n: w[n][l] for n in SMALL}
        if more:
            state, token = gather_start(l + 1, layer_part, token, f"{l + 1}")
        xs, h, s1 = _ffn_fwd(xs, h, full[l]["ffn1"], "ffn1", p["norm_mix"], dep=token if more else None)
        xs, h, s2 = _mixer_fwd(xs, h, p, full[l], tabs, S, p["norm_ffn2"])
        if more:
            lands = gather_wait(state, xs)
            send_sems, recv_sems, _, lands, token = _exchange_start(
                [], lands, _forward_plan, FORWARD_COPIES * len(lands), name=f"forward_start_{l + 1}")
        xs, h, s3 = _ffn_fwd(xs, h, full[l]["ffn2"], "ffn2", w["norm_ffn1"][l + 1] if more else None,
                             dep=token if more else None)
        if more:
            _, lands = _exchange_wait(send_sems, recv_sems, [], lands, _forward_plan, xs, name=f"forward_wait_{l + 1}")
        saved.append((s1, s2, s3))
    dx, dfinal, loss = _loss_head(xs, final_norm, loss_target.reshape(T, D), name="loss_head")

    big_grads = {n: [None] * L for n in BIG}
    small_grads_of = [None] * L
    pending = None

    def scatter_start(names, stacked, tag):
        srcs = [stacked[n] for n in names if n not in PACKED]
        if any(n in PACKED for n in names):
            srcs.append(_pack([stacked[n] for n in PACKED], 1))
        lands = [lax.empty((SCATTER_PEERS, *s.shape[1:]), s.dtype) for s in srcs]
        send_sems, recv_sems, srcs, lands, token = _exchange_start(
            srcs, lands, _scatter_plan, SCATTER_PEERS * len(srcs), name=f"scatter_start_{tag}")
        return (names, send_sems, recv_sems, srcs, lands, tag), token

    def scatter_finish(state, after, l):
        names, send_sems, recv_sems, srcs, lands, tag = state
        srcs, got = _exchange_wait(send_sems, recv_sems, srcs, lands, _scatter_plan, after, name=f"scatter_wait_{tag}")
        sums = [_scatter_sum(s, g, me, name="scatter_sum") for s, g in zip(srcs, got)]
        direct = [n for n in names if n not in PACKED]
        for n, g in zip(direct, sums):
            big_grads[n][l] = g
        if len(sums) > len(direct):
            for n, g in zip(PACKED, _unpack(sums[-1], packed_shapes, 0)):
                big_grads[n][l] = g

    dep = None
    for l in reversed(range(L)):
        p = {n: w[n][l] for n in SMALL}
        s1, s2, s3 = saved[l]
        small_g = {}
        dx, small_g["norm_ffn2"], dup_t, dwd = _ffn_bwd(dx, p["norm_ffn2"], full[l]["ffn2"], s3, "ffn2", dep=dep)
        if pending is not None:
            scatter_finish(pending[0], dx, pending[1])
        stacked = {"ffn2_up": _split_rows(dup_t), "ffn2_down": _split_rows(dwd)}
        state, dep = scatter_start(("ffn2_up", "ffn2_down"), stacked, f"ffn2_{l}")
        pending = (state, l)

        dx, gm = _mixer_bwd(dx, p, full[l], tabs, s2, S, dep=dep)
        scatter_finish(pending[0], dx, pending[1])
        names = ("w_in", "w_attn_proj", "w_out") + PACKED
        state, dep = scatter_start(names, _mixer_grads_stacked(gm), f"mix_{l}")
        pending = (state, l)
        small_g.update({n: gm[n] for n in SMALL if n in gm})

        dx, small_g["norm_ffn1"], dup_t, dwd = _ffn_bwd(dx, p["norm_ffn1"], full[l]["ffn1"], s1, "ffn1", dep=dep)
        if l > 0:
            scatter_finish(pending[0], dx, pending[1])
        else:
            last_mixer = pending
        stacked = {"ffn1_up": _split_rows(dup_t), "ffn1_down": _split_rows(dwd)}
        state, dep = scatter_start(("ffn1_up", "ffn1_down"), stacked, f"ffn1_{l}")
        pending = (state, l)
        small_grads_of[l] = small_g
    grad_x = dx.reshape(B, S, D)

    small_parts = [small_grads_of[l][n] for l in range(L) for n in SMALL] + [dfinal, loss[0, :1]]
    small_shapes = [p.shape for p in small_parts]
    vec = _pack([jnp.concatenate([p.reshape(-1) for p in small_parts])], 0)
    small_send, small_recv, vec_thru, small_land, small_token = _exchange_start(
        [vec], [lax.empty((N_DEV, *vec.shape), F32)], _gather_all_plan, SCATTER_PEERS, name="small_start")

    gk, grad, delta, new_m, new_v = {}, {}, {}, {}, {}

    def update(n, dep=None):
        wn, gn, mn, vn = (a.reshape(1, -1) if a.ndim == 1 else a for a in (wk[n], gk[n], mk[n], vk[n]))
        d, nm, nv = _adamw(wn, gn, mn, vn, name="adamw_" + n, dep=dep)
        grad[n] = turned(gk[n], n)
        delta[n], new_m[n], new_v[n] = (turned(a.reshape(wk[n].shape), n) for a in (d, nm, nv))

    deps = [dep, small_token]
    for n in FFN2_PART:
        gk[n] = jnp.stack(big_grads[n])
        update(n, deps.pop(0))
    scatter_finish(last_mixer[0], new_v["ffn2_down"], last_mixer[1])
    for n in MIXER_PART:
        gk[n] = jnp.stack(big_grads[n])
        update(n)
    scatter_finish(pending[0], new_v[MIXER_PART[-1]], pending[1])
    for n in FFN1_PART:
        gk[n] = jnp.stack(big_grads[n])
        update(n)

    vec_thru, small_land = _exchange_wait(small_send, small_recv, vec_thru, small_land, _gather_all_plan,
                                          new_v["ffn1_down"], name="small_wait")
    parts = lax.dynamic_update_index_in_dim(small_land[0], vec_thru[0], my_slot, 0)
    flat = _sum_devices(parts, name="sum_small").reshape(-1)
    small_grads, at = [], 0
    for shape in small_shapes:
        size = int(np.prod(shape))
        small_grads.append(lax.slice_in_dim(flat, at, at + size).reshape(shape))
        at += size
    loss_total = small_grads[-1].reshape(())
    for i, n in enumerate(SMALL):
        gk[n] = jnp.stack([small_grads[l * len(SMALL) + i] for l in range(L)]).reshape(w[n].shape)
        update(n)
    gk["final_norm"] = small_grads[-2].reshape(final_norm.shape)
    update("final_norm")
    return (loss_total, grad_x, *[grad[n] for n in order], *[delta[n] for n in order],
            *[new_m[n] for n in order], *[new_v[n] for n in order])
```

```python
import functools

import numpy as np
import jax
import jax.numpy as jnp
from jax import lax
from jax.experimental import pallas as pl
from jax.experimental.pallas import tpu as pltpu

F32 = jnp.float32
BF16 = jnp.bfloat16

NORM_EPS = 1e-6
ROPE_THETA = 10000.0
QK_NOPE = 128
QK_ROPE = 64
V_DIM = 128
HEAD_W = 256
POOL_WINDOWS = (2, 4, 8, 16)
POOL_G = 128
POOL_DIM = 512
LANE = 128
ATTN_SCALE = float((QK_NOPE + QK_ROPE) ** -0.5)
ATTN_SCALE_LOG2 = ATTN_SCALE * float(np.log2(np.e))
MASK_VALUE = -1e30
ATTN_TILE = 512

ADAM_LR = 0.001
ADAM_B1 = 0.9
ADAM_B2 = 0.999
ADAM_EPS = 1e-08
ADAM_WD = 0.01
ADAM_STEP = 10

N_DEV = 8
VMEM_LIMIT = 52 * 1024 * 1024

MESH = pl.DeviceIdType.MESH
ANY = pl.BlockSpec(memory_space=pl.ANY)


def _tile(dim, target, align=LANE):
    if dim <= target:
        return dim
    t = (target // align) * align
    while t >= align:
        if dim % t == 0:
            return t
        t -= align
    return dim


def _params(*sem):
    return pltpu.CompilerParams(dimension_semantics=sem, vmem_limit_bytes=VMEM_LIMIT)


def _rstd(x):
    return lax.rsqrt(jnp.mean(x * x, axis=-1, keepdims=True) + NORM_EPS)


def _mm(a, b, *, name, ta=False, tb=False, out_dtype=F32, alpha=1.0, tm=512, tn=1024, tk=1024, dep=None):
    if ta:
        K, M = a.shape
    else:
        M, K = a.shape
    if tb:
        N, K2 = b.shape
    else:
        K2, N = b.shape
    assert K == K2, (a.shape, b.shape, ta, tb)
    tm, tn, tk = _tile(M, tm), _tile(N, tn), _tile(K, tk)
    nk = K // tk
    dims = (((0 if ta else 1,), (1 if tb else 0,)), ((), ()))

    def body(a_ref, b_ref, *rest):
        o_ref = rest[0 if dep is None else 1]
        acc_ref = rest[-1] if nk > 1 else None
        part = lax.dot_general(a_ref[...].astype(BF16), b_ref[...].astype(BF16), dims,
                               preferred_element_type=F32)

        def finish(acc):
            o_ref[...] = (acc * alpha if alpha != 1.0 else acc).astype(out_dtype)

        if nk == 1:
            finish(part)
        else:
            k = pl.program_id(2)

            @pl.when(k == 0)
            def _():
                acc_ref[...] = part

            @pl.when(k > 0)
            def _():
                acc_ref[...] += part

            @pl.when(k == nk - 1)
            def _():
                finish(acc_ref[...])

    a_spec = pl.BlockSpec((tk, tm), lambda i, j, k: (k, i)) if ta else pl.BlockSpec((tm, tk), lambda i, j, k: (i, k))
    b_spec = pl.BlockSpec((tn, tk), lambda i, j, k: (j, k)) if tb else pl.BlockSpec((tk, tn), lambda i, j, k: (k, j))
    return pl.pallas_call(
        body, name=name,
        out_shape=jax.ShapeDtypeStruct((M, N), out_dtype),
        grid=(M // tm, N // tn, nk),
        in_specs=[a_spec, b_spec] + _dep_spec(dep),
        out_specs=pl.BlockSpec((tm, tn), lambda i, j, k: (i, j)),
        scratch_shapes=[pltpu.VMEM((tm, tn), F32)] if nk > 1 else [],
        compiler_params=_params("parallel", "parallel", "arbitrary"),
    )(a, b, *([] if dep is None else [dep]))


def _dw_multi(pairs, *, name, tk=512):
    n = len(pairs)
    T = pairs[0][0].shape[0]
    tk = _tile(T, tk, 16)
    nk = T // tk
    shapes = [(a.shape[1], b.shape[1]) for a, b in pairs]

    def body(*refs):
        ins, outs, accs = refs[:2 * n], refs[2 * n:3 * n], refs[3 * n:]
        k = pl.program_id(0)
        parts = [lax.dot_general(ins[2 * i][...].astype(BF16), ins[2 * i + 1][...].astype(BF16), _TN,
                                 preferred_element_type=F32) for i in range(n)]

        @pl.when(k == 0)
        def _():
            for acc, part in zip(accs, parts):
                acc[...] = part

        @pl.when(k > 0)
        def _():
            for acc, part in zip(accs, parts):
                acc[...] += part

        @pl.when(k == nk - 1)
        def _():
            for out, acc in zip(outs, accs):
                out[...] = acc[...].astype(BF16)

    return pl.pallas_call(
        body, name=name,
        out_shape=tuple(jax.ShapeDtypeStruct(s, BF16) for s in shapes),
        grid=(nk,),
        in_specs=[pl.BlockSpec((tk, x.shape[1]), lambda k: (k, 0)) for pair in pairs for x in pair],
        out_specs=tuple(pl.BlockSpec(s, lambda k: (0, 0)) for s in shapes),
        scratch_shapes=[pltpu.VMEM(s, F32) for s in shapes],
        compiler_params=_params("arbitrary"),
    )(*[x for pair in pairs for x in pair])


def _rms_fwd(x, g, *, name):
    T, D = x.shape
    tm = _tile(T, 512, 16)

    def body(x_ref, g_ref, h_ref):
        x = x_ref[...]
        h_ref[...] = (x * _rstd(x) * g_ref[...]).astype(BF16)

    return pl.pallas_call(
        body, name=name,
        out_shape=jax.ShapeDtypeStruct((T, D), BF16),
        grid=(T // tm,),
        in_specs=[pl.BlockSpec((tm, D), lambda i: (i, 0)), pl.BlockSpec((1, D), lambda i: (0, 0))],
        out_specs=pl.BlockSpec((tm, D), lambda i: (i, 0)),
        compiler_params=_params("parallel"),
    )(x, g.reshape(1, D))


def _loss_head(x, g, target, *, name):
    T, D = x.shape
    tm = _tile(T, 512, 16)

    def body(x_ref, g_ref, t_ref, dx_ref, dg_ref, loss_ref):
        x = x_ref[...]
        gain = g_ref[...]
        r = _rstd(x)
        xhat = x * r
        err = xhat * gain - t_ref[...]
        dy = err * (1.0 / D)
        dxh = dy * gain
        dx_ref[...] = r * (dxh - xhat * jnp.mean(dxh * xhat, axis=-1, keepdims=True))
        dg_part = jnp.sum(dy * xhat, axis=0, keepdims=True)
        loss_part = jnp.full((1, LANE), 0.5 / D, F32) * jnp.sum(err * err)

        @pl.when(pl.program_id(0) == 0)
        def _():
            dg_ref[...] = dg_part
            loss_ref[...] = loss_part

        @pl.when(pl.program_id(0) > 0)
        def _():
            dg_ref[...] += dg_part
            loss_ref[...] += loss_part

    row = pl.BlockSpec((tm, D), lambda i: (i, 0))
    vec = pl.BlockSpec((1, D), lambda i: (0, 0))
    return pl.pallas_call(
        body, name=name,
        out_shape=(jax.ShapeDtypeStruct((T, D), F32), jax.ShapeDtypeStruct((1, D), F32),
                   jax.ShapeDtypeStruct((1, LANE), F32)),
        grid=(T // tm,),
        in_specs=[row, vec, row],
        out_specs=(row, vec, pl.BlockSpec((1, LANE), lambda i: (0, 0))),
        compiler_params=_params("arbitrary"),
    )(x, g.reshape(1, D), target)


def _ffn_fwd_core(x, h, w_up_t, wd, next_gain, *, alpha, name, dep=None):
    T, D = x.shape
    F = wd.shape[0]
    tm = _tile(T, 256, 16)
    has_norm = next_gain is not None

    def body(x_ref, h_ref, wg_ref, wu_ref, wd_ref, *rest):
        outs = rest[len(rest) - (5 if has_norm else 4):]
        gate_ref, up_ref, a_ref, xn_ref = outs[:4]
        h = h_ref[...]
        gate = lax.dot_general(h, wg_ref[...], _NT, preferred_element_type=F32)
        up = lax.dot_general(h, wu_ref[...], _NT, preferred_element_type=F32)
        a = (gate * jax.nn.sigmoid(gate) * up).astype(BF16)
        gate_ref[...] = gate.astype(BF16)
        up_ref[...] = up.astype(BF16)
        a_ref[...] = a
        xn = x_ref[...] + alpha * jnp.dot(a, wd_ref[...], preferred_element_type=F32)
        xn_ref[...] = xn
        if has_norm:
            outs[4][...] = (xn * _rstd(xn) * rest[0][...]).astype(BF16)

    once = pl.Buffered(1)
    row_d = pl.BlockSpec((tm, D), lambda i: (i, 0))
    row_f = pl.BlockSpec((tm, F), lambda i: (i, 0))
    vec = pl.BlockSpec((1, D), lambda i: (0, 0))
    act = jax.ShapeDtypeStruct((T, F), BF16)
    operands = [x, h, w_up_t, w_up_t, wd] + ([next_gain.reshape(1, D)] if has_norm else [])
    out = pl.pallas_call(
        body, name=name,
        out_shape=(act, act, act, jax.ShapeDtypeStruct((T, D), F32)) + ((jax.ShapeDtypeStruct((T, D), BF16),) if has_norm else ()),
        grid=(T // tm,),
        in_specs=[row_d, row_d,
                  pl.BlockSpec((F, D), lambda i: (0, 0), pipeline_mode=once),
                  pl.BlockSpec((F, D), lambda i: (1, 0), pipeline_mode=once),
                  pl.BlockSpec((F, D), lambda i: (0, 0), pipeline_mode=once)] + ([vec] if has_norm else []) + _dep_spec(dep),
        out_specs=(row_f, row_f, row_f, row_d) + ((row_d,) if has_norm else ()),
        compiler_params=_params("parallel"),
    )(*operands, *([] if dep is None else [dep]))
    return out if has_norm else (*out, None)


def _ffn_bwd_core(dxo, wd, w_up_t, gate, up, x, gain, *, alpha, name, dep=None):
    T, D = dxo.shape
    F = wd.shape[0]
    tm = _tile(T, 256, 16)

    def body(dxo_ref, wd_ref, wg_ref, wu_ref, gate_ref, up_ref, x_ref, g_ref, *rest):
        dgate_ref, dup_ref, dx_ref, dg_ref = rest[-4:]
        dxo = dxo_ref[...]
        da = lax.dot_general(dxo.astype(BF16), wd_ref[...], _NT, preferred_element_type=F32) * alpha
        gate = gate_ref[...].astype(F32)
        up = up_ref[...].astype(F32)
        sig = jax.nn.sigmoid(gate)
        dgate = (da * up * (sig * (1.0 + gate * (1.0 - sig)))).astype(BF16)
        dup = (da * (gate * sig)).astype(BF16)
        dgate_ref[...] = dgate
        dup_ref[...] = dup
        dh = (jnp.dot(dgate, wg_ref[...], preferred_element_type=F32)
              + jnp.dot(dup, wu_ref[...], preferred_element_type=F32))
        x = x_ref[...]
        r = _rstd(x)
        xhat = x * r
        dxh = dh * g_ref[...]
        dx_ref[...] = dxo + r * (dxh - xhat * jnp.mean(dxh * xhat, axis=-1, keepdims=True))
        part = jnp.sum(dh * xhat, axis=0, keepdims=True)

        @pl.when(pl.program_id(0) == 0)
        def _():
            dg_ref[...] = part

        @pl.when(pl.program_id(0) > 0)
        def _():
            dg_ref[...] += part

    once = pl.Buffered(1)
    row_d = pl.BlockSpec((tm, D), lambda i: (i, 0))
    row_f = pl.BlockSpec((tm, F), lambda i: (i, 0))
    vec = pl.BlockSpec((1, D), lambda i: (0, 0))
    act = jax.ShapeDtypeStruct((T, F), BF16)
    return pl.pallas_call(
        body, name=name,
        out_shape=(act, act, jax.ShapeDtypeStruct((T, D), F32), jax.ShapeDtypeStruct((1, D), F32)),
        grid=(T // tm,),
        in_specs=[row_d,
                  pl.BlockSpec((F, D), lambda i: (0, 0), pipeline_mode=once),
                  pl.BlockSpec((F, D), lambda i: (0, 0), pipeline_mode=once),
                  pl.BlockSpec((F, D), lambda i: (1, 0), pipeline_mode=once),
                  row_f, row_f, row_d, vec] + _dep_spec(dep),
        out_specs=(row_f, row_f, row_d, vec),
        compiler_params=_params("arbitrary"),
    )(dxo, wd, w_up_t, w_up_t, gate, up, x, gain.reshape(1, D), *([] if dep is None else [dep]))


def _ffn_dw_up(dgate, dup, h, *, name):
    T, F = dgate.shape
    D = h.shape[1]
    tm, tk = _tile(F, 1408), _tile(T, 1024, 16)
    nf, nk = F // tm, T // tk

    def body(dgate_ref, dup_ref, h_ref, o_ref, acc_ref):
        i, k = pl.program_id(0), pl.program_id(1)

        def accumulate(part):
            @pl.when(k == 0)
            def _():
                acc_ref[...] = part

            @pl.when(k > 0)
            def _():
                acc_ref[...] += part

        @pl.when(i < nf)
        def _():
            accumulate(lax.dot_general(dgate_ref[...], h_ref[...], _TN, preferred_element_type=F32))

        @pl.when(i >= nf)
        def _():
            accumulate(lax.dot_general(dup_ref[...], h_ref[...], _TN, preferred_element_type=F32))

        @pl.when(k == nk - 1)
        def _():
            o_ref[...] = acc_ref[...].astype(BF16)

    return pl.pallas_call(
        body, name=name,
        out_shape=jax.ShapeDtypeStruct((2 * F, D), BF16),
        grid=(2 * nf, nk),
        in_specs=[pl.BlockSpec((tk, tm), lambda i, k: (jnp.where(i < nf, k, nk - 1), jnp.minimum(i, nf - 1))),
                  pl.BlockSpec((tk, tm), lambda i, k: (jnp.where(i < nf, 0, k), jnp.maximum(i - nf, 0))),
                  pl.BlockSpec((tk, D), lambda i, k: (k, 0))],
        out_specs=pl.BlockSpec((tm, D), lambda i, k: (i, 0)),
        scratch_shapes=[pltpu.VMEM((tm, D), F32)],
        compiler_params=_params("parallel", "arbitrary"),
    )(dgate, dup, h)


def _dep_spec(dep):
    return [] if dep is None else [pl.BlockSpec(dep.shape, lambda *_: (0,) * dep.ndim)]


def _rope_tables(positions):
    half = QK_ROPE // 2
    inv_freq = ROPE_THETA ** (-jnp.arange(0, QK_ROPE, 2, dtype=F32) / QK_ROPE)
    ang = positions.astype(F32)[:, None] * inv_freq
    cos, sin = jnp.cos(ang), jnp.sin(ang)
    z = jnp.zeros_like(cos)
    zz = jnp.zeros((positions.shape[0], LANE - QK_ROPE), F32)
    c = jnp.concatenate([cos, cos, zz], axis=1)
    sa = jnp.concatenate([z, sin, zz], axis=1)
    sb = jnp.concatenate([-sin, z, zz], axis=1)
    return c, sa, sb


def _rotate(seg, c, sa, sb, sign):
    half = QK_ROPE // 2
    mix = pltpu.roll(seg, half, 1) * sa + pltpu.roll(seg, LANE - half, 1) * sb
    return seg * c + mix if sign > 0 else seg * c - mix


def _mixer_in(h, wa, wuq, wukv, gq, gkv, tabs, *, name, dep=None):
    T, D = h.shape
    HQ, QL = wuq.shape
    KVL = wukv.shape[0]
    H = HQ // HEAD_W
    o_q, o_kv, o_kr = POOL_DIM, POOL_DIM + QL, POOL_DIM + QL + KVL
    PA = o_kr + LANE
    assert wa.shape[0] >= PA
    tm = _tile(T, 512, 16)

    def body(h_ref, wa_ref, wuq_ref, wukv_ref, gq_ref, gkv_ref, c_ref, sa_ref, sb_ref, *rest):
        xp_ref, ql_ref, kvl_ref, qn_ref, kvn_ref, q_ref, kv_ref, kr_ref = rest[-8:]
        proj = lax.dot_general(h_ref[...], wa_ref[...], _NT, preferred_element_type=F32)
        xp_ref[...] = proj[:, :POOL_DIM]
        ql = proj[:, o_q:o_kv]
        kvl = proj[:, o_kv:o_kr]
        ql_ref[...] = ql
        kvl_ref[...] = kvl
        qn = (ql * _rstd(ql) * gq_ref[...]).astype(BF16)
        kvn = (kvl * _rstd(kvl) * gkv_ref[...]).astype(BF16)
        qn_ref[...] = qn
        kvn_ref[...] = kvn
        c, sa, sb = c_ref[...], sa_ref[...], sb_ref[...]
        q = lax.dot_general(qn, wuq_ref[...], _NT, preferred_element_type=F32)
        for hh in range(H):
            base = hh * HEAD_W
            q_ref[:, base:base + QK_NOPE] = q[:, base:base + QK_NOPE].astype(BF16)
            q_ref[:, base + QK_NOPE:base + HEAD_W] = _rotate(
                q[:, base + QK_NOPE:base + HEAD_W], c, sa, sb, 1).astype(BF16)
        kv_ref[...] = jnp.dot(kvn, wukv_ref[...], preferred_element_type=F32).astype(BF16)
        kr_ref[...] = _rotate(proj[:, o_kr:o_kr + LANE], c, sa, sb, 1).astype(BF16)

    def row(w):
        return pl.BlockSpec((tm, w), lambda i: (i, 0))

    def whole(arr):
        return pl.BlockSpec(arr.shape, lambda i: (0,) * arr.ndim)

    gq2, gkv2 = gq.reshape(1, QL), gkv.reshape(1, KVL)
    outs = [(POOL_DIM, F32), (QL, F32), (KVL, F32), (QL, BF16), (KVL, BF16), (HQ, BF16), (HQ, BF16), (LANE, BF16)]
    return pl.pallas_call(
        body, name=name,
        out_shape=tuple(jax.ShapeDtypeStruct((T, w), dt) for w, dt in outs),
        grid=(T // tm,),
        in_specs=[row(D), pl.BlockSpec((PA, D), lambda i: (0, 0)), whole(wuq), whole(wukv), whole(gq2), whole(gkv2),
                  row(LANE), row(LANE), row(LANE)] + _dep_spec(dep),
        out_specs=tuple(row(w) for w, _ in outs),
        compiler_params=_params("parallel"),
    )(h, wa, wuq, wukv, gq2, gkv2, *tabs, *([] if dep is None else [dep]))


def _mixer_in_bwd(dq, dkv, dkr, ql, kvl, dxp, dgl, x, dxo, win_t, wgate_t, wuq, wukv, g_mix, gq, gkv, tabs, *, name):
    T, HQ = dq.shape
    D = x.shape[1]
    QL, KVL = wuq.shape[1], wukv.shape[0]
    H = HQ // HEAD_W
    PA = POOL_DIM + QL + KVL + LANE
    o_q, o_kv, o_kr = POOL_DIM, POOL_DIM + QL, POOL_DIM + QL + KVL
    tm = _tile(T, 256, 16)

    def norm_bwd(lat, gain, dn):
        r = _rstd(lat)
        xhat = lat * r
        dxh = dn * gain
        dlat = r * (dxh - xhat * jnp.mean(dxh * xhat, axis=-1, keepdims=True))
        return dlat, jnp.sum(dn * xhat, axis=0, keepdims=True)

    def body(dq_ref, dkv_ref, dkr_ref, ql_ref, kvl_ref, dxp_ref, dgl_ref, x_ref, dxo_ref, win_ref, wgate_ref,
             wuq_ref, wukv_ref, gmix_ref, gq_ref, gkv_ref, c_ref, sa_ref, sb_ref,
             dproj_ref, dqp_ref, dgq_ref, dgkv_ref, dx_ref, dgmix_ref):
        c, sa, sb = c_ref[...], sa_ref[...], sb_ref[...]
        for hh in range(H):
            base = hh * HEAD_W
            dqp_ref[:, base:base + QK_NOPE] = dq_ref[:, base:base + QK_NOPE]
            dqp_ref[:, base + QK_NOPE:base + HEAD_W] = _rotate(
                dq_ref[:, base + QK_NOPE:base + HEAD_W].astype(F32), c, sa, sb, -1).astype(BF16)
        dqn = jnp.dot(dqp_ref[...], wuq_ref[...], preferred_element_type=F32)
        dkvn = lax.dot_general(dkv_ref[...], wukv_ref[...], _NT, preferred_element_type=F32)
        dql, dgq = norm_bwd(ql_ref[...], gq_ref[...], dqn)
        dkvl, dgkv = norm_bwd(kvl_ref[...], gkv_ref[...], dkvn)
        dproj_ref[:, :POOL_DIM] = dxp_ref[...].astype(BF16)
        dproj_ref[:, o_q:o_kv] = dql.astype(BF16)
        dproj_ref[:, o_kv:o_kr] = dkvl.astype(BF16)
        dproj_ref[:, o_kr:PA] = _rotate(dkr_ref[...], c, sa, sb, -1).astype(BF16)

        dh = (jnp.dot(dproj_ref[...], win_ref[...], preferred_element_type=F32)
              + jnp.dot(dgl_ref[...], wgate_ref[...], preferred_element_type=F32))
        x = x_ref[...]
        r = _rstd(x)
        xhat = x * r
        dxh = dh * gmix_ref[...]
        dx_ref[...] = dxo_ref[...] + r * (dxh - xhat * jnp.mean(dxh * xhat, axis=-1, keepdims=True))
        dgmix = jnp.sum(dh * xhat, axis=0, keepdims=True)

        @pl.when(pl.program_id(0) == 0)
        def _():
            dgq_ref[...] = dgq
            dgkv_ref[...] = dgkv
            dgmix_ref[...] = dgmix

        @pl.when(pl.program_id(0) > 0)
        def _():
            dgq_ref[...] += dgq
            dgkv_ref[...] += dgkv
            dgmix_ref[...] += dgmix

    def row(w):
        return pl.BlockSpec((tm, w), lambda i: (i, 0))

    def resident(arr, rows=None):
        shape = arr.shape if rows is None else (rows, arr.shape[1])
        return pl.BlockSpec(shape, lambda i: (0, 0), pipeline_mode=pl.Buffered(1))

    gmix2, gq2, gkv2 = g_mix.reshape(1, D), gq.reshape(1, QL), gkv.reshape(1, KVL)
    vec = pl.BlockSpec((1, D), lambda i: (0, 0))
    vq, vkv = pl.BlockSpec((1, QL), lambda i: (0, 0)), pl.BlockSpec((1, KVL), lambda i: (0, 0))
    return pl.pallas_call(
        body, name=name,
        out_shape=(jax.ShapeDtypeStruct((T, PA), BF16), jax.ShapeDtypeStruct((T, HQ), BF16),
                   jax.ShapeDtypeStruct((1, QL), F32), jax.ShapeDtypeStruct((1, KVL), F32),
                   jax.ShapeDtypeStruct((T, D), F32), jax.ShapeDtypeStruct((1, D), F32)),
        grid=(T // tm,),
        in_specs=[row(HQ), row(HQ), row(LANE), row(QL), row(KVL), row(POOL_DIM), row(2 * D), row(D), row(D),
                  resident(win_t, PA), resident(wgate_t), resident(wuq), resident(wukv),
                  vec, vq, vkv, row(LANE), row(LANE), row(LANE)],
        out_specs=(row(PA), row(HQ), vq, vkv, row(D), vec),
        compiler_params=_params("arbitrary"),
    )(dq, dkv, dkr, ql, kvl, dxp, dgl, x, dxo, win_t, wgate_t, wuq, wukv, gmix2, gq2, gkv2, *tabs)


def _pool_groups(x_of, S):
    row = lax.broadcasted_iota(jnp.int32, (S, POOL_G), 0)
    for g, w in enumerate(POOL_WINDOWS):
        x = x_of(g)
        s = x
        d = 1
        while d < w:
            s = s + jnp.where(row >= d, pltpu.roll(s, d, 0), 0.0)
            d *= 2
        cnt = jnp.minimum(row + 1, w).astype(F32)
        yield g, w, x, s / cnt - x, cnt, row


def _pool_fwd(xp, maps, scale, *, S, name):
    T = xp.shape[0]

    def body(xp_ref, maps_ref, scale_ref, ms_ref):
        for g, _, _, pooled, _, _ in _pool_groups(lambda g: xp_ref[:, g * POOL_G:(g + 1) * POOL_G], S):
            mixed = jnp.dot(pooled.astype(BF16), maps_ref[g].astype(BF16), preferred_element_type=F32)
            ms_ref[:, g * POOL_G:(g + 1) * POOL_G] = (mixed * scale_ref[:, g * POOL_G:(g + 1) * POOL_G]).astype(BF16)

    return pl.pallas_call(
        body, name=name,
        out_shape=jax.ShapeDtypeStruct((T, POOL_DIM), BF16),
        grid=(T // S,),
        in_specs=[pl.BlockSpec((S, POOL_DIM), lambda b: (b, 0)),
                  pl.BlockSpec(maps.shape, lambda b: (0, 0, 0)),
                  pl.BlockSpec((1, POOL_DIM), lambda b: (0, 0))],
        out_specs=pl.BlockSpec((S, POOL_DIM), lambda b: (b, 0)),
        compiler_params=_params("parallel"),
    )(xp, maps, scale.reshape(1, POOL_DIM))


def _pool_bwd(xp, dms, maps, scale, *, S, name):
    T = xp.shape[0]

    def body(xp_ref, dms_ref, maps_ref, scale_ref, dxp_ref, dmaps_ref, dscale_ref):
        first = pl.program_id(0) == 0
        for g, w, _, pooled, cnt, row in _pool_groups(lambda g: xp_ref[:, g * POOL_G:(g + 1) * POOL_G], S):
            cols = slice(g * POOL_G, (g + 1) * POOL_G)
            pooled_b = pooled.astype(BF16)
            maps_b = maps_ref[g].astype(BF16)
            mixed = jnp.dot(pooled_b, maps_b, preferred_element_type=F32)
            dms = dms_ref[:, cols]
            dscale = jnp.sum(dms * mixed, axis=0, keepdims=True)
            dmixed = (dms * scale_ref[:, cols]).astype(BF16)
            dmaps = lax.dot_general(pooled_b, dmixed, (((0,), (0,)), ((), ())), preferred_element_type=F32)
            dpooled = lax.dot_general(dmixed, maps_b, (((1,), (1,)), ((), ())), preferred_element_type=F32)
            z = dpooled / cnt
            d = 1
            while d < w:
                z = z + jnp.where(row < S - d, pltpu.roll(z, S - d, 0), 0.0)
                d *= 2
            dxp_ref[:, cols] = z - dpooled

            @pl.when(first)
            def _():
                dmaps_ref[g] = dmaps
                dscale_ref[:, cols] = dscale

            @pl.when(jnp.logical_not(first))
            def _():
                dmaps_ref[g] += dmaps
                dscale_ref[:, cols] += dscale

    seq = pl.BlockSpec((S, POOL_DIM), lambda b: (b, 0))
    maps_spec = pl.BlockSpec(maps.shape, lambda b: (0, 0, 0))
    vec = pl.BlockSpec((1, POOL_DIM), lambda b: (0, 0))
    return pl.pallas_call(
        body, name=name,
        out_shape=(jax.ShapeDtypeStruct((T, POOL_DIM), F32), jax.ShapeDtypeStruct(maps.shape, F32),
                   jax.ShapeDtypeStruct((1, POOL_DIM), F32)),
        grid=(T // S,),
        in_specs=[seq, seq, maps_spec, vec],
        out_specs=(seq, maps_spec, vec),
        compiler_params=_params("arbitrary"),
    )(xp, dms, maps, scale.reshape(1, POOL_DIM))


def _causal_mask(s, t):
    r = lax.broadcasted_iota(jnp.int32, (t, t), 0)
    c = lax.broadcasted_iota(jnp.int32, (t, t), 1)
    return jnp.where(r >= c, s, MASK_VALUE)


_NT = (((1,), (1,)), ((), ()))
_TN = (((0,), (0,)), ((), ()))


def _attn_fwd(q, kv, kr, *, S, name):
    T, HQ = q.shape
    H = HQ // HEAD_W
    B = T // S
    t = _tile(S, ATTN_TILE)
    n = S // t

    def body(q_ref, k_ref, v_ref, kr_ref, o_ref, lse_ref, kcat, vcat):
        kcat[:, :QK_NOPE] = k_ref[...]
        kcat[:, QK_NOPE:] = kr_ref[...]
        vcat[:, :V_DIM] = v_ref[...]
        vcat[:, V_DIM:] = jnp.ones((S, HEAD_W - V_DIM), BF16)
        for i in range(n):
            rows = slice(i * t, (i + 1) * t)
            qt = q_ref[rows, :]
            m = jnp.full((t, 1), MASK_VALUE, F32)
            acc = jnp.zeros((t, HEAD_W), F32)
            for j in range(i + 1):
                cols = slice(j * t, (j + 1) * t)
                s = lax.dot_general(qt, kcat[cols, :], _NT, preferred_element_type=F32) * ATTN_SCALE_LOG2
                if j == i:
                    s = _causal_mask(s, t)
                m_new = jnp.maximum(m, jnp.max(s, axis=1, keepdims=True))
                p = jnp.exp2(s - m_new)
                acc = jnp.exp2(m - m_new) * acc + jnp.dot(p.astype(BF16), vcat[cols, :], preferred_element_type=F32)
                m = m_new
            l = acc[:, V_DIM:V_DIM + 1]
            o_ref[rows, :] = (acc[:, :V_DIM] / l).astype(BF16)
            lse_ref[rows, :] = jnp.broadcast_to(m + jnp.log2(l), (t, LANE))

    seq_h = pl.BlockSpec((S, LANE), lambda b, h: (b, h))
    return pl.pallas_call(
        body, name=name,
        out_shape=(jax.ShapeDtypeStruct((T, H * V_DIM), BF16), jax.ShapeDtypeStruct((T, H * LANE), F32)),
        grid=(B, H),
        in_specs=[pl.BlockSpec((S, HEAD_W), lambda b, h: (b, h)),
                  pl.BlockSpec((S, QK_NOPE), lambda b, h: (b, 2 * h)),
                  pl.BlockSpec((S, V_DIM), lambda b, h: (b, 2 * h + 1)),
                  pl.BlockSpec((S, LANE), lambda b, h: (b, 0))],
        out_specs=(seq_h, seq_h),
        scratch_shapes=[pltpu.VMEM((S, HEAD_W), BF16), pltpu.VMEM((S, HEAD_W), BF16)],
        compiler_params=_params("parallel", "parallel"),
    )(q, kv, kv, kr)


def _attn_bwd(q, kv, kr, o, do, lse, *, S, name):
    T, HQ = q.shape
    H = HQ // HEAD_W
    B = T // S
    t = _tile(S, ATTN_TILE)
    n = S // t

    def body(q_ref, k_ref, v_ref, kr_ref, o_ref, do_ref, lse_ref, dq_ref, dkv_ref, dkr_ref, kcat, dq_acc):
        @pl.when(pl.program_id(1) == 0)
        def _():
            dkr_ref[...] = jnp.zeros_like(dkr_ref)

        kcat[:, :QK_NOPE] = k_ref[...]
        kcat[:, QK_NOPE:] = kr_ref[...]
        delta = [jnp.sum(do_ref[i * t:(i + 1) * t, :].astype(F32) * o_ref[i * t:(i + 1) * t, :].astype(F32),
                         axis=1, keepdims=True) for i in range(n)]
        for j in range(n):
            cols = slice(j * t, (j + 1) * t)
            kc = kcat[cols, :]
            vt = v_ref[cols, :]
            dk = jnp.zeros((t, HEAD_W), F32)
            dv = jnp.zeros((t, V_DIM), F32)
            for i in range(j, n):
                rows = slice(i * t, (i + 1) * t)
                qt = q_ref[rows, :]
                dot_ = do_ref[rows, :]
                s = lax.dot_general(qt, kc, _NT, preferred_element_type=F32) * ATTN_SCALE_LOG2
                if i == j:
                    s = _causal_mask(s, t)
                p = jnp.exp2(s - lse_ref[rows, :][:, :1])
                dv = dv + lax.dot_general(p.astype(BF16), dot_, _TN, preferred_element_type=F32)
                dp = lax.dot_general(dot_, vt, _NT, preferred_element_type=F32)
                ds = (p * (dp - delta[i]) * ATTN_SCALE).astype(BF16)
                dk = dk + lax.dot_general(ds, qt, _TN, preferred_element_type=F32)
                dq_part = jnp.dot(ds, kc, preferred_element_type=F32)
                if j == 0:
                    dq_acc[rows, :] = dq_part
                else:
                    dq_acc[rows, :] += dq_part
            dkv_ref[cols, :QK_NOPE] = dk[:, :QK_NOPE].astype(BF16)
            dkv_ref[cols, QK_NOPE:] = dv.astype(BF16)
            dkr_ref[cols, :] += dk[:, QK_NOPE:]
        dq_ref[...] = dq_acc[...].astype(BF16)

    seq_q = pl.BlockSpec((S, HEAD_W), lambda b, h: (b, h))
    seq_h = pl.BlockSpec((S, LANE), lambda b, h: (b, h))
    seq_shared = pl.BlockSpec((S, LANE), lambda b, h: (b, 0))
    return pl.pallas_call(
        body, name=name,
        out_shape=(jax.ShapeDtypeStruct((T, HQ), BF16), jax.ShapeDtypeStruct((T, HQ), BF16),
                   jax.ShapeDtypeStruct((T, LANE), F32)),
        grid=(B, H),
        in_specs=[seq_q,
                  pl.BlockSpec((S, QK_NOPE), lambda b, h: (b, 2 * h)),
                  pl.BlockSpec((S, V_DIM), lambda b, h: (b, 2 * h + 1)),
                  seq_shared, seq_h, seq_h, seq_h],
        out_specs=(seq_q, seq_q, seq_shared),
        scratch_shapes=[pltpu.VMEM((S, HEAD_W), BF16), pltpu.VMEM((S, HEAD_W), F32)],
        compiler_params=_params("parallel", "arbitrary"),
    )(q, kv, kv, kr, o, do, lse)


def _merge_out(h, ms, o, x, wgate, bgate, wpp, wap, wout, next_gain, *, name):
    T, D = x.shape
    tm = _tile(T, 256, 16)

    def body(h_ref, ms_ref, o_ref, x_ref, wgate_ref, bgate_ref, wpp_ref, wap_ref, wout_ref, ng_ref,
             gates_ref, ba_ref, bb_ref, merged_ref, xn_ref, hn_ref):
        logits = lax.dot_general(h_ref[...], wgate_ref[...], _NT, preferred_element_type=F32) + bgate_ref[...]
        gates = jax.nn.sigmoid(logits)
        ba = jnp.dot(ms_ref[...], wpp_ref[...], preferred_element_type=F32)
        bb = jnp.dot(o_ref[...], wap_ref[...], preferred_element_type=F32)
        merged = (gates[:, :D] * ba + gates[:, D:] * bb).astype(BF16)
        gates_ref[...] = gates.astype(BF16)
        ba_ref[...] = ba.astype(BF16)
        bb_ref[...] = bb.astype(BF16)
        merged_ref[...] = merged
        xn = x_ref[...] + jnp.dot(merged, wout_ref[...], preferred_element_type=F32)
        xn_ref[...] = xn
        hn_ref[...] = (xn * _rstd(xn) * ng_ref[...]).astype(BF16)

    def row(w):
        return pl.BlockSpec((tm, w), lambda i: (i, 0))

    def whole(arr):
        return pl.BlockSpec(arr.shape, lambda i: (0,) * arr.ndim)

    bg2, ng2 = bgate.reshape(1, 2 * D), next_gain.reshape(1, D)
    act = jax.ShapeDtypeStruct((T, D), BF16)
    return pl.pallas_call(
        body, name=name,
        out_shape=(jax.ShapeDtypeStruct((T, 2 * D), BF16), act, act, act, jax.ShapeDtypeStruct((T, D), F32), act),
        grid=(T // tm,),
        in_specs=[row(D), row(ms.shape[1]), row(o.shape[1]), row(D), whole(wgate), whole(bg2), whole(wpp),
                  whole(wap), whole(wout), whole(ng2)],
        out_specs=(row(2 * D), row(D), row(D), row(D), row(D), row(D)),
        compiler_params=_params("parallel"),
    )(h, ms, o, x, wgate, bg2, wpp, wap, wout, ng2)


def _merge_bwd(dxo, wout, wpp, wap, gates, ba, bb, *, name, dep=None):
    T, D = dxo.shape
    tm = _tile(T, 512, 16)

    def body(dxo_ref, wout_ref, wpp_ref, wap_ref, gates_ref, ba_ref, bb_ref, *rest):
        dba_ref, dbb_ref, dgl_ref, dbg_ref, dms_ref, do_ref = rest[-6:]
        dm = lax.dot_general(dxo_ref[...].astype(BF16), wout_ref[...], _NT, preferred_element_type=F32)
        ga = gates_ref[:, :D].astype(F32)
        gb = gates_ref[:, D:].astype(F32)
        dba = (dm * ga).astype(BF16)
        dbb = (dm * gb).astype(BF16)
        dba_ref[...] = dba
        dbb_ref[...] = dbb
        dms_ref[...] = lax.dot_general(dba, wpp_ref[...], _NT, preferred_element_type=F32)
        do_ref[...] = lax.dot_general(dbb, wap_ref[...], _NT, preferred_element_type=F32).astype(BF16)
        dgl_a = dm * ba_ref[...].astype(F32) * (ga * (1.0 - ga))
        dgl_b = dm * bb_ref[...].astype(F32) * (gb * (1.0 - gb))
        dgl_ref[:, :D] = dgl_a.astype(BF16)
        dgl_ref[:, D:] = dgl_b.astype(BF16)
        sa = jnp.sum(dgl_a, axis=0, keepdims=True)
        sb = jnp.sum(dgl_b, axis=0, keepdims=True)

        @pl.when(pl.program_id(0) == 0)
        def _():
            dbg_ref[:, :D] = sa
            dbg_ref[:, D:] = sb

        @pl.when(pl.program_id(0) > 0)
        def _():
            dbg_ref[:, :D] += sa
            dbg_ref[:, D:] += sb

    def row(w):
        return pl.BlockSpec((tm, w), lambda i: (i, 0))

    def whole(arr):
        return pl.BlockSpec(arr.shape, lambda i: (0, 0))

    P, HV = wpp.shape[0], wap.shape[0]
    act = jax.ShapeDtypeStruct((T, D), BF16)
    return pl.pallas_call(
        body, name=name,
        out_shape=(act, act, jax.ShapeDtypeStruct((T, 2 * D), BF16), jax.ShapeDtypeStruct((1, 2 * D), F32),
                   jax.ShapeDtypeStruct((T, P), F32), jax.ShapeDtypeStruct((T, HV), BF16)),
        grid=(T // tm,),
        in_specs=[row(D), whole(wout), whole(wpp), whole(wap), row(2 * D), row(D), row(D)] + _dep_spec(dep),
        out_specs=(row(D), row(D), row(2 * D), pl.BlockSpec((1, 2 * D), lambda i: (0, 0)), row(P), row(HV)),
        compiler_params=_params("arbitrary"),
    )(dxo, wout, wpp, wap, gates, ba, bb, *([] if dep is None else [dep]))


def _ffn_fwd(x, h, w, tag, next_gain, dep=None):
    gate, up, a, xn, hn = _ffn_fwd_core(x, h, w["up_t"], w["wd"], next_gain, alpha=0.5,
                                        name=f"{tag}_fwd" if next_gain is not None else f"{tag}_fwd_last", dep=dep)
    return xn, hn, (x, h, gate, up, a)


def _ffn_bwd(dxo, gain, w, saved, tag, dep=None, early=None):
    x, h, gate, up, a = saved
    dwd = _mm(a, dxo, ta=True, alpha=0.5, out_dtype=BF16, name=f"{tag}_dwd", tm=1408, tn=1024, tk=1024, dep=dep)
    if early is not None:
        dep = early(dwd)
    dgate, dup, dx, dgain = _ffn_bwd_core(dxo, w["wd"], w["up_t"], gate, up, x, gain, alpha=0.5,
                                          name=f"{tag}_bwd_core", dep=dep)
    dup_t = _ffn_dw_up(dgate, dup, h, name=f"{tag}_dw_up")
    return dx, dgain, dup_t, dwd


def _mixer_fwd(x, h, p, w, tabs, S, next_gain, dep=None):
    xp, ql, kvl, qn, kvn, q, kv, kr = _mixer_in(h, w["win_t"], w["wuq_t"], w["wukv"], p["q_latent_norm"],
                                                 p["kv_latent_norm"], tabs, name="mix_in", dep=dep)
    ms = _pool_fwd(xp, p["pool_maps"], p["pool_scale"], S=S, name="pool_fwd")
    o, lse = _attn_fwd(q, kv, kr, S=S, name="attn_fwd")
    gates, ba, bb, merged, xn, hn = _merge_out(h, ms, o, x, w["wgate_t"], p["b_gate"], w["wpp"], w["wap"], w["wout"],
                                               next_gain, name="merge_out")
    return xn, hn, (x, h, xp, ql, kvl, qn, kvn, q, kv, kr, ms, o, lse, gates, ba, bb, merged)


def _mixer_bwd(dxo, p, w, tabs, saved, S, dep=None):
    x, h, xp, ql, kvl, qn, kvn, q, kv, kr, ms, o, lse, gates, ba, bb, merged = saved
    dba, dbb, dgl, dbg, dms, do = _merge_bwd(dxo, w["wout"], w["wpp"], w["wap"], gates, ba, bb, name="merge_bwd",
                                             dep=dep)
    g = {}
    g["wout"], g["wpp"], g["wap"] = _dw_multi([(merged, dxo), (ms, dba), (o, dbb)], name="d_w_merge")
    dxp, g["pool_maps"], g["pool_scale"] = _pool_bwd(xp, dms, p["pool_maps"], p["pool_scale"], S=S, name="pool_bwd")
    dq, dkv, dkr = _attn_bwd(q, kv, kr, o, do, lse, S=S, name="attn_bwd")
    dproj, dqp, g["q_latent_norm"], g["kv_latent_norm"], dx, g["norm_mix"] = _mixer_in_bwd(
        dq, dkv, dkr, ql, kvl, dxp, dgl, x, dxo, w["win_t"], w["wgate_t"], w["wuq_t"], w["wukv"],
        p["norm_mix"], p["q_latent_norm"], p["kv_latent_norm"], tabs, name="mix_in_bwd")
    g["wuq_t"], g["wukv"] = _dw_multi([(dqp, qn), (kvn, dkv)], name="d_w_qkv", tk=1024)
    g["wa_t"], g["wgate_t"] = _dw_multi([(dproj, h), (dgl, h)], name="d_w_in")
    g["b_gate"] = dbg
    return dx, g


BIG = ("ffn1_up", "ffn1_down", "w_in", "w_pool_proj", "w_uq", "w_ukv", "w_attn_proj", "w_out", "ffn2_up", "ffn2_down")
SMALL = ("norm_ffn1", "norm_mix", "b_gate", "pool_maps", "pool_scale", "q_latent_norm", "kv_latent_norm", "norm_ffn2")
PACKED = ("w_pool_proj", "w_uq", "w_ukv")
TRANSPOSED = ("ffn1_up", "ffn2_up", "w_in", "w_uq")
COL_SHARDED = ("w_pool_proj", "w_ukv")
QK_HEAD = QK_NOPE + QK_ROPE


def _rows(stacked):
    n, r, c = stacked.shape
    return stacked.reshape(n * r, c)


def _cols(stacked):
    n, k, c = stacked.shape
    return stacked.transpose(1, 0, 2).reshape(k, n * c)


FFN1_PART = ("ffn1_up", "ffn1_down")
MIXER_PART = ("w_in", "w_attn_proj", "w_out") + PACKED
FFN2_PART = ("ffn2_up", "ffn2_down")


def _kernel_weights(stacked):
    full = {}
    for tag in ("ffn1", "ffn2"):
        if tag + "_up" in stacked:
            full[tag] = {"up_t": _rows(stacked[tag + "_up"]), "wd": _rows(stacked[tag + "_down"])}
    if "w_in" in stacked:
        win_t = _rows(stacked["w_in"])
        D = win_t.shape[1]
        wuq_t = _rows(stacked["w_uq"])
        QL = wuq_t.shape[1]
        H = wuq_t.shape[0] // QK_HEAD
        wuq_t = jnp.pad(wuq_t.reshape(H, QK_HEAD, QL), ((0, 0), (0, HEAD_W - QK_HEAD), (0, 0)))
        full.update({"win_t": win_t, "wgate_t": win_t[win_t.shape[0] - 2 * D:], "wuq_t": wuq_t.reshape(H * HEAD_W, QL),
                     "wukv": _cols(stacked["w_ukv"]), "wpp": _cols(stacked["w_pool_proj"]),
                     "wap": _rows(stacked["w_attn_proj"]), "wout": _rows(stacked["w_out"])})
    return full


def _split_rows(full):
    return full.reshape(N_DEV, full.shape[0] // N_DEV, full.shape[1])


def _split_cols(full):
    k, cols = full.shape
    return full.reshape(k, N_DEV, cols // N_DEV).transpose(1, 0, 2)


def _mixer_grads_stacked(g):
    n_a = g["wa_t"].shape[0] - (LANE - QK_ROPE)
    HQ, QL = g["wuq_t"].shape
    H = HQ // HEAD_W
    wuq_t = g["wuq_t"].reshape(H, HEAD_W, QL)[:, :QK_HEAD].reshape(H * QK_HEAD, QL)
    return {"w_in": _split_rows(jnp.concatenate([g["wa_t"][:n_a], g["wgate_t"]], axis=0)),
            "w_uq": _split_rows(wuq_t),
            "w_pool_proj": _split_cols(g["wpp"]), "w_ukv": _split_cols(g["wukv"]),
            "w_attn_proj": _split_rows(g["wap"]), "w_out": _split_rows(g["wout"])}


def _mesh_place():
    x, y, c = lax.axis_index("x"), lax.axis_index("y"), lax.axis_index("c")
    chips = [(1 - x, y), (x, 1 - y), (1 - x, 1 - y)]
    return x, y, c, chips


HBM = pl.BlockSpec(memory_space=pltpu.HBM)
SEMAPHORES = pl.BlockSpec(memory_space=pltpu.SEMAPHORE)
DATAFLOW = pltpu.SideEffectType.DATAFLOW_SIDE_EFFECTING
GATHER_PEERS = 4
SCATTER_PEERS = 7


def _in_hbm(a):
    return pltpu.with_memory_space_constraint(a, pltpu.HBM)


def _gather_plan(src_refs, land_refs):
    x, y, c, chips = _mesh_place()
    me = 4 * x + 2 * y + c
    targets = [(x, y, 1 - c)] + [(cx, cy, c) for cx, cy in chips]
    return [(s, land.at[me], to) for s, land in zip(src_refs, land_refs) for to in targets]


def _scatter_plan(src_refs, land_refs):
    x, y, c, _ = _mesh_place()
    peers = [(x, y, 1 - c), (1 - x, y, c), (x, 1 - y, c), (1 - x, 1 - y, c),
             (1 - x, y, 1 - c), (x, 1 - y, 1 - c), (1 - x, 1 - y, 1 - c)]
    return [(s.at[4 * px + 2 * py + pc], land.at[k], (px, py, pc))
            for s, land in zip(src_refs, land_refs) for k, (px, py, pc) in enumerate(peers)]


def _descriptors(plan, src_refs, land_refs, send_sems, recv_sems):
    return [pltpu.make_async_remote_copy(src_ref=s, dst_ref=d, send_sem=send_sems.at[k], recv_sem=recv_sems.at[k],
                                         device_id=to, device_id_type=MESH)
            for k, (s, d, to) in enumerate(plan(src_refs, land_refs))]


def _exchange(srcs, land_shapes, plan, per_src, *, name):
    n = len(srcs)

    def body(*refs):
        copies = _descriptors(plan, refs[:n], refs[n:2 * n], refs[2 * n], refs[2 * n + 1])
        for cp in copies:
            cp.start()
        for cp in copies:
            cp.wait()

    return pl.pallas_call(
        body, name=name,
        out_shape=tuple(jax.ShapeDtypeStruct(shape, s.dtype) for shape, s in zip(land_shapes, srcs)),
        in_specs=[ANY] * n, out_specs=(ANY,) * n,
        scratch_shapes=[pltpu.SemaphoreType.DMA((per_src * n,)), pltpu.SemaphoreType.DMA((per_src * n,))],
    )(*srcs)


FORWARD_COPIES = 4


def _forward_slots():
    x, y, c, chips = _mesh_place()
    return [4 * cx + 2 * cy + c for cx, cy in chips] + [4 * x + 2 * y + (1 - c)], (x, y, 1 - c)


def _forward_plan(src_refs, land_refs):
    slots, sibling = _forward_slots()
    return [(land.at[s], land.at[s], sibling) for land in land_refs for s in slots]


def _gather_all_plan(src_refs, land_refs):
    x, y, c, _ = _mesh_place()
    me = 4 * x + 2 * y + c
    peers = [(x, y, 1 - c), (1 - x, y, c), (x, 1 - y, c), (1 - x, 1 - y, c),
             (1 - x, y, 1 - c), (x, 1 - y, 1 - c), (1 - x, 1 - y, 1 - c)]
    return [(s, land.at[me], to) for s, land in zip(src_refs, land_refs) for to in peers]


def _exchange_start(srcs, lands, plan, n_copies, *, name):
    ns, n = len(srcs), len(srcs) + len(lands)

    def body(*refs):
        for cp in _descriptors(plan, refs[:ns], refs[ns:n], refs[n], refs[n + 1]):
            cp.start()
        refs[-1][...] = jnp.zeros_like(refs[-1])

    sems = pltpu.SemaphoreType.DMA((n_copies,))
    out = pl.pallas_call(
        body, name=name,
        out_shape=(sems, sems, *[pltpu.HBM(a.shape, a.dtype) for a in srcs + lands],
                   jax.ShapeDtypeStruct((8, LANE), F32)),
        in_specs=(HBM,) * n,
        out_specs=(SEMAPHORES, SEMAPHORES, *[HBM] * n, pl.BlockSpec(memory_space=pltpu.VMEM)),
        input_output_aliases={i: 2 + i for i in range(n)},
        compiler_params=pltpu.CompilerParams(has_side_effects=DATAFLOW),
    )(*[_in_hbm(a) for a in srcs + lands])
    return out[0], out[1], list(out[2:2 + ns]), list(out[2 + ns:2 + n]), out[-1]


def _exchange_wait(send_sems, recv_sems, srcs, lands, plan, after, *, name):
    ns, n = len(srcs), len(srcs) + len(lands)

    def body(*refs):
        for cp in _descriptors(plan, refs[:ns], refs[ns:n], refs[n], refs[n + 1]):
            cp.wait_send()
            cp.wait_recv()

    out = pl.pallas_call(
        body, name=name,
        out_shape=tuple(pltpu.HBM(a.shape, a.dtype) for a in srcs + lands),
        in_specs=(*[HBM] * n, SEMAPHORES, SEMAPHORES, ANY),
        out_specs=(HBM,) * n,
        input_output_aliases={i: i for i in range(n)},
        compiler_params=pltpu.CompilerParams(has_side_effects=DATAFLOW),
    )(*srcs, *lands, send_sems, recv_sems, after)
    return list(out[:ns]), list(out[ns:])


def _gather_forward(lands, *, name):
    n = len(lands)

    def body(*refs):
        in_refs, out_refs = refs[:n], refs[n:2 * n]
        token, send_sems, recv_sems = refs[2 * n:2 * n + 3]
        slots, sibling = _forward_slots()
        passed = [pltpu.make_async_remote_copy(
            src_ref=i.at[s], dst_ref=o.at[s],
            send_sem=send_sems.at[FORWARD_COPIES * b + j], recv_sem=recv_sems.at[FORWARD_COPIES * b + j],
            device_id=sibling, device_id_type=MESH)
            for b, (i, o) in enumerate(zip(in_refs, out_refs)) for j, s in enumerate(slots)]
        for cp in passed:
            cp.start()
        for cp in passed:
            cp.wait()
        token[...] = jnp.zeros_like(token)

    out = pl.pallas_call(
        body, name=name,
        out_shape=(*[jax.ShapeDtypeStruct(a.shape, a.dtype) for a in lands], jax.ShapeDtypeStruct((8, LANE), F32)),
        in_specs=[ANY] * n,
        out_specs=(*[ANY] * n, pl.BlockSpec(memory_space=pltpu.VMEM)),
        input_output_aliases={i: i for i in range(n)},
        scratch_shapes=[pltpu.SemaphoreType.DMA((FORWARD_COPIES * n,)), pltpu.SemaphoreType.DMA((FORWARD_COPIES * n,))],
    )(*lands)
    return list(out[:n]), out[n]


def _scatter_sum(parts, got, me, *, name):
    shard = parts.shape[1:]
    cols = shard[-1]
    rows = int(np.prod(shard[:-1]))
    tr = _tile(rows, 256, 16)

    def body(me_ref, p_ref, g_ref, o_ref):
        acc = p_ref[...].astype(F32)
        for k in range(SCATTER_PEERS):
            acc = acc + g_ref[k].astype(F32)
        o_ref[...] = acc

    out = pl.pallas_call(
        body, name=name,
        out_shape=jax.ShapeDtypeStruct((rows, cols), F32),
        grid_spec=pltpu.PrefetchScalarGridSpec(
            num_scalar_prefetch=1, grid=(rows // tr,),
            in_specs=[pl.BlockSpec((None, tr, cols), lambda r, me_ref: (me_ref[0], r, 0)),
                      pl.BlockSpec((SCATTER_PEERS, tr, cols), lambda r, me_ref: (0, r, 0))],
            out_specs=pl.BlockSpec((tr, cols), lambda r, me_ref: (r, 0))),
        compiler_params=_params("parallel"),
    )(me, parts.reshape(N_DEV, rows, cols), got.reshape(SCATTER_PEERS, rows, cols))
    return out.reshape(shard)


def _sum_devices(parts, *, name):
    _, R, C = parts.shape
    tr = _tile(R, 512, 8)

    def body(p_ref, o_ref):
        acc = p_ref[0]
        for d in range(1, N_DEV):
            acc = acc + p_ref[d]
        o_ref[...] = acc

    return pl.pallas_call(
        body, name=name,
        out_shape=jax.ShapeDtypeStruct((R, C), F32),
        grid=(R // tr,),
        in_specs=[pl.BlockSpec((N_DEV, tr, C), lambda r: (0, r, 0))],
        out_specs=pl.BlockSpec((tr, C), lambda r: (r, 0)),
        compiler_params=_params("parallel"),
    )(parts)


def _adamw(w, g, m, v, *, name, dep=None):
    shape = w.shape
    cols = shape[-1]
    rows = w.size // cols
    tr = _tile(rows, 256, 8)

    def body(w_ref, g_ref, m_ref, v_ref, *rest):
        d_ref, nm_ref, nv_ref = rest[-3:]
        g = g_ref[...]
        m = ADAM_B1 * m_ref[...] + (1.0 - ADAM_B1) * g
        v = ADAM_B2 * v_ref[...] + (1.0 - ADAM_B2) * jnp.square(g)
        m_hat = m / (1.0 - ADAM_B1 ** ADAM_STEP)
        v_hat = v / (1.0 - ADAM_B2 ** ADAM_STEP)
        d_ref[...] = -ADAM_LR * (m_hat / (jnp.sqrt(v_hat) + ADAM_EPS) + ADAM_WD * w_ref[...])
        nm_ref[...] = m
        nv_ref[...] = v

    spec = pl.BlockSpec((tr, cols), lambda i: (i, 0))
    out = jax.ShapeDtypeStruct((rows, cols), F32)
    d, nm, nv = pl.pallas_call(
        body, name=name,
        out_shape=(out, out, out),
        grid=(rows // tr,),
        in_specs=[spec] * 4 + _dep_spec(dep), out_specs=(spec,) * 3,
        compiler_params=_params("parallel"),
    )(*(a.reshape(rows, cols) for a in (w, g, m, v)), *([] if dep is None else [dep]))
    return d.reshape(shape), nm.reshape(shape), nv.reshape(shape)


PACK_ALIGN = 16 * LANE


def _pack(pieces, lead):
    out = []
    for p in pieces:
        keep = p.shape[:lead]
        flat = p.reshape(*keep, -1)
        pad = (-flat.shape[-1]) % PACK_ALIGN
        if pad:
            flat = jnp.pad(flat, [(0, 0)] * lead + [(0, pad)])
        out.append(flat.reshape(*keep, -1, LANE))
    return jnp.concatenate(out, axis=lead)


def _unpack(buf, shapes, lead):
    keep = buf.shape[:lead]
    out, row = [], 0
    for shape in shapes:
        size = int(np.prod(shape))
        rows = -(-size // PACK_ALIGN) * (PACK_ALIGN // LANE)
        piece = lax.slice_in_dim(buf, row, row + rows, axis=lead).reshape(*keep, rows * LANE)
        out.append(lax.slice_in_dim(piece, 0, size, axis=lead).reshape(*keep, *shape))
        row += rows
    return out


def kernel(x, positions, norm_ffn1, ffn1_up, ffn1_down, norm_mix, w_in, b_gate, pool_maps, pool_scale, w_pool_proj, q_latent_norm, w_uq, kv_latent_norm, w_ukv, w_attn_proj, w_out, norm_ffn2, ffn2_up, ffn2_down, final_norm, loss_target, m_norm_ffn1, m_ffn1_up, m_ffn1_down, m_norm_mix, m_w_in, m_b_gate, m_pool_maps, m_pool_scale, m_w_pool_proj, m_q_latent_norm, m_w_uq, m_kv_latent_norm, m_w_ukv, m_w_attn_proj, m_w_out, m_norm_ffn2, m_ffn2_up, m_ffn2_down, m_final_norm, v_norm_ffn1, v_ffn1_up, v_ffn1_down, v_norm_mix, v_w_in, v_b_gate, v_pool_maps, v_pool_scale, v_w_pool_proj, v_q_latent_norm, v_w_uq, v_kv_latent_norm, v_w_ukv, v_w_attn_proj, v_w_out, v_norm_ffn2, v_ffn2_up, v_ffn2_down, v_final_norm):
    order = ("norm_ffn1", "ffn1_up", "ffn1_down", "norm_mix", "w_in", "b_gate", "pool_maps", "pool_scale",
             "w_pool_proj", "q_latent_norm", "w_uq", "kv_latent_norm", "w_ukv", "w_attn_proj", "w_out",
             "norm_ffn2", "ffn2_up", "ffn2_down", "final_norm")
    w = dict(zip(order, (norm_ffn1, ffn1_up, ffn1_down, norm_mix, w_in, b_gate, pool_maps, pool_scale, w_pool_proj,
                         q_latent_norm, w_uq, kv_latent_norm, w_ukv, w_attn_proj, w_out, norm_ffn2, ffn2_up,
                         ffn2_down, final_norm)))
    m = dict(zip(order, (m_norm_ffn1, m_ffn1_up, m_ffn1_down, m_norm_mix, m_w_in, m_b_gate, m_pool_maps, m_pool_scale,
                         m_w_pool_proj, m_q_latent_norm, m_w_uq, m_kv_latent_norm, m_w_ukv, m_w_attn_proj, m_w_out,
                         m_norm_ffn2, m_ffn2_up, m_ffn2_down, m_final_norm)))
    v = dict(zip(order, (v_norm_ffn1, v_ffn1_up, v_ffn1_down, v_norm_mix, v_w_in, v_b_gate, v_pool_maps, v_pool_scale,
                         v_w_pool_proj, v_q_latent_norm, v_w_uq, v_kv_latent_norm, v_w_ukv, v_w_attn_proj, v_w_out,
                         v_norm_ffn2, v_ffn2_up, v_ffn2_down, v_final_norm)))
    L = norm_ffn1.shape[0]
    B, S, D = x.shape
    T = B * S

    def turned(a, n):
        return a.transpose(0, 2, 1) if n in TRANSPOSED else a

    wk, mk, vk = ({n: turned(d[n], n) for n in order} for d in (w, m, v))
    packed_shapes = [wk[n].shape[1:] for n in PACKED]
    my_slot = 4 * lax.axis_index("x") + 2 * lax.axis_index("y") + lax.axis_index("c")
    me = jnp.stack([my_slot]).astype(jnp.int32)

    def weight_blocks(l, names, token):
        zero = token[0, 0].astype(BF16)
        blocks = [wk[n][l].astype(BF16) + zero for n in names if n not in PACKED]
        if any(n in PACKED for n in names):
            blocks.append(_pack([wk[n][l].astype(BF16) + zero for n in PACKED], 0))
        return blocks

    def kernel_weights(names, lands):
        direct = [n for n in names if n not in PACKED]
        stacked = dict(zip(direct, lands))
        if len(lands) > len(direct):
            stacked.update(zip(PACKED, _unpack(lands[-1], packed_shapes, 1)))
        return _kernel_weights(stacked)

    def gather_start(l, names, token, tag):
        blocks = weight_blocks(l, names, token)
        lands = [lax.empty((N_DEV, *b.shape), b.dtype) for b in blocks]
        send_sems, recv_sems, blocks, lands, token = _exchange_start(
            blocks, lands, _gather_plan, GATHER_PEERS * len(blocks), name=f"gather_start_{tag}")
        return (send_sems, recv_sems, blocks, lands, tag), token

    def gather_wait(state, after):
        send_sems, recv_sems, blocks, lands, tag = state
        return _exchange_wait(send_sems, recv_sems, blocks, lands, _gather_plan, after, name=f"gather_wait_{tag}")[1]

    layer_part = FFN1_PART + MIXER_PART + FFN2_PART
    tabs = _rope_tables(positions.reshape(T))
    xs = x.reshape(T, D)
    h = _rms_fwd(xs, w["norm_ffn1"][0], name="first_norm")
    full, saved = [], []

    p = {n: w[n][0] for n in SMALL}
    blocks = weight_blocks(0, FFN1_PART, jnp.zeros((8, LANE), F32))
    lands = _exchange(blocks, [(N_DEV, *b.shape) for b in blocks], _gather_plan, GATHER_PEERS, name="gather_first")
    lands, token = _gather_forward(lands, name="gather_forward")
    w0 = kernel_weights(FFN1_PART, lands)
    state, token = gather_start(0, MIXER_PART, token, "0_mix")
    xs, h, s1 = _ffn_fwd(xs, h, w0["ffn1"], "ffn1", p["norm_mix"], dep=token)
    lands, token = _gather_forward(gather_wait(state, xs), name="gather_forward")
    w0.update(kernel_weights(MIXER_PART, lands))
    state, token = gather_start(0, FFN2_PART, token, "0_ffn2")
    if L > 1:
        next_state, token = gather_start(1, layer_part, token, "1")
    xs, h, s2 = _mixer_fwd(xs, h, p, w0, tabs, S, p["norm_ffn2"], dep=token)
    lands, token = _gather_forward(gather_wait(state, xs), name="gather_forward")
    w0.update(kernel_weights(FFN2_PART, lands))
    xs, h, s3 = _ffn_fwd(xs, h, w0["ffn2"], "ffn2", w["norm_ffn1"][1] if L > 1 else None, dep=token)
    if L > 1:
        lands, token = _gather_forward(gather_wait(next_state, xs), name="gather_forward")
    full.append(w0)
    saved.append((s1, s2, s3))

    for l in range(1, L):
        full.append(kernel_weights(layer_part, lands))
        more = l + 1 < L
        p = {n: w[n][l] for n in SMALL}
        if more:
            state, token = gather_start(l + 1, layer_part, token, f"{l + 1}")
        xs, h, s1 = _ffn_fwd(xs, h, full[l]["ffn1"], "ffn1", p["norm_mix"], dep=token if more else None)
        xs, h, s2 = _mixer_fwd(xs, h, p, full[l], tabs, S, p["norm_ffn2"])
        if more:
            lands = gather_wait(state, xs)
            send_sems, recv_sems, _, lands, token = _exchange_start(
                [], lands, _forward_plan, FORWARD_COPIES * len(lands), name=f"forward_start_{l + 1}")
        xs, h, s3 = _ffn_fwd(xs, h, full[l]["ffn2"], "ffn2", w["norm_ffn1"][l + 1] if more else None,
                             dep=token if more else None)
        if more:
            _, lands = _exchange_wait(send_sems, recv_sems, [], lands, _forward_plan, xs, name=f"forward_wait_{l + 1}")
        saved.append((s1, s2, s3))
    dx, dfinal, loss = _loss_head(xs, final_norm, loss_target.reshape(T, D), name="loss_head")

    big_grads = {n: [None] * L for n in BIG}
    small_grads_of = [None] * L
    pending = None

    def scatter_start(names, stacked, tag):
        srcs = [stacked[n] for n in names if n not in PACKED]
        if any(n in PACKED for n in names):
            srcs.append(_pack([stacked[n] for n in PACKED], 1))
        lands = [lax.empty((SCATTER_PEERS, *s.shape[1:]), s.dtype) for s in srcs]
        send_sems, recv_sems, srcs, lands, token = _exchange_start(
            srcs, lands, _scatter_plan, SCATTER_PEERS * len(srcs), name=f"scatter_start_{tag}")
        return (names, send_sems, recv_sems, srcs, lands, tag), token

    def scatter_finish(state, after, l):
        names, send_sems, recv_sems, srcs, lands, tag = state
        srcs, got = _exchange_wait(send_sems, recv_sems, srcs, lands, _scatter_plan, after, name=f"scatter_wait_{tag}")
        sums = [_scatter_sum(s, g, me, name="scatter_sum") for s, g in zip(srcs, got)]
        direct = [n for n in names if n not in PACKED]
        for n, g in zip(direct, sums):
            big_grads[n][l] = g
        if len(sums) > len(direct):
            for n, g in zip(PACKED, _unpack(sums[-1], packed_shapes, 0)):
                big_grads[n][l] = g

    dep = None
    for l in reversed(range(L)):
        p = {n: w[n][l] for n in SMALL}
        s1, s2, s3 = saved[l]
        small_g = {}
        dx, small_g["norm_ffn2"], dup_t, dwd = _ffn_bwd(dx, p["norm_ffn2"], full[l]["ffn2"], s3, "ffn2", dep=dep)
        if pending is not None:
            scatter_finish(pending[0], dx, pending[1])
        stacked = {"ffn2_up": _split_rows(dup_t), "ffn2_down": _split_rows(dwd)}
        state, dep = scatter_start(("ffn2_up", "ffn2_down"), stacked, f"ffn2_{l}")
        pending = (state, l)

        dx, gm = _mixer_bwd(dx, p, full[l], tabs, s2, S, dep=dep)
        scatter_finish(pending[0], dx, pending[1])
        names = ("w_in", "w_attn_proj", "w_out") + PACKED
        state, dep = scatter_start(names, _mixer_grads_stacked(gm), f"mix_{l}")
        pending = (state, l)
        small_g.update({n: gm[n] for n in SMALL if n in gm})

        if l > 0:
            dx, small_g["norm_ffn1"], dup_t, dwd = _ffn_bwd(dx, p["norm_ffn1"], full[l]["ffn1"], s1, "ffn1", dep=dep)
            scatter_finish(pending[0], dx, pending[1])
            stacked = {"ffn1_up": _split_rows(dup_t), "ffn1_down": _split_rows(dwd)}
            state, dep = scatter_start(("ffn1_up", "ffn1_down"), stacked, f"ffn1_{l}")
            pending = (state, l)
        else:
            early_states = []

            def send_down(dwd):
                state, token = scatter_start(("ffn1_down",), {"ffn1_down": _split_rows(dwd)}, "ffn1_down_0")
                early_states.append(state)
                return token

            dx, small_g["norm_ffn1"], dup_t, _ = _ffn_bwd(dx, p["norm_ffn1"], full[l]["ffn1"], s1, "ffn1", dep=dep,
                                                          early=send_down)
            last_mixer, last_down = pending, (early_states[0], 0)
            state, dep = scatter_start(("ffn1_up",), {"ffn1_up": _split_rows(dup_t)}, "ffn1_up_0")
            pending = (state, l)
        small_grads_of[l] = small_g
    grad_x = dx.reshape(B, S, D)

    small_parts = [small_grads_of[l][n] for l in range(L) for n in SMALL] + [dfinal, loss[0, :1]]
    small_shapes = [p.shape for p in small_parts]
    vec = _pack([jnp.concatenate([p.reshape(-1) for p in small_parts])], 0)
    small_send, small_recv, vec_thru, small_land, small_token = _exchange_start(
        [vec], [lax.empty((N_DEV, *vec.shape), F32)], _gather_all_plan, SCATTER_PEERS, name="small_start")

    gk, grad, delta, new_m, new_v = {}, {}, {}, {}, {}

    def update(n, dep=None):
        wn, gn, mn, vn = (a.reshape(1, -1) if a.ndim == 1 else a for a in (wk[n], gk[n], mk[n], vk[n]))
        d, nm, nv = _adamw(wn, gn, mn, vn, name="adamw_" + n, dep=dep)
        grad[n] = turned(gk[n], n)
        delta[n], new_m[n], new_v[n] = (turned(a.reshape(wk[n].shape), n) for a in (d, nm, nv))

    deps = [dep, small_token]
    for n in FFN2_PART:
        gk[n] = jnp.stack(big_grads[n])
        update(n, deps.pop(0))
    scatter_finish(last_mixer[0], new_v["ffn2_down"], last_mixer[1])
    for n in MIXER_PART:
        gk[n] = jnp.stack(big_grads[n])
        update(n)
    scatter_finish(last_down[0], new_v[MIXER_PART[-1]], last_down[1])
    scatter_finish(pending[0], new_v[MIXER_PART[-1]], pending[1])
    for n in FFN1_PART:
        gk[n] = jnp.stack(big_grads[n])
        update(n)

    vec_thru, small_land = _exchange_wait(small_send, small_recv, vec_thru, small_land, _gather_all_plan,
                                          new_v["ffn1_down"], name="small_wait")
    parts = lax.dynamic_update_index_in_dim(small_land[0], vec_thru[0], my_slot, 0)
    flat = _sum_devices(parts, name="sum_small").reshape(-1)
    small_grads, at = [], 0
    for shape in small_shapes:
        size = int(np.prod(shape))
        small_grads.append(lax.slice_in_dim(flat, at, at + size).reshape(shape))
        at += size
    loss_total = small_grads[-1].reshape(())
    for i, n in enumerate(SMALL):
        gk[n] = jnp.stack([small_grads[l * len(SMALL) + i] for l in range(L)]).reshape(w[n].shape)
        update(n)
    gk["final_norm"] = small_grads[-2].reshape(final_norm.shape)
    update("final_norm")
    return (loss_total, grad_x, *[grad[n] for n in order], *[delta[n] for n in order],
            *[new_m[n] for n in order], *[new_v[n] for n in order])
```

```python
import functools

import numpy as np
import jax
import jax.numpy as jnp
from jax import lax
from jax.experimental import pallas as pl
from jax.experimental.pallas import tpu as pltpu

F32 = jnp.float32
BF16 = jnp.bfloat16

NORM_EPS = 1e-6
ROPE_THETA = 10000.0
QK_NOPE = 128
QK_ROPE = 64
V_DIM = 128
HEAD_W = 256
POOL_WINDOWS = (2, 4, 8, 16)
POOL_G = 128
POOL_DIM = 512
LANE = 128
ATTN_SCALE = float((QK_NOPE + QK_ROPE) ** -0.5)
ATTN_SCALE_LOG2 = ATTN_SCALE * float(np.log2(np.e))
MASK_VALUE = -1e30
ATTN_TILE = 512

ADAM_LR = 0.001
ADAM_B1 = 0.9
ADAM_B2 = 0.999
ADAM_EPS = 1e-08
ADAM_WD = 0.01
ADAM_STEP = 10

N_DEV = 8
VMEM_LIMIT = 52 * 1024 * 1024

MESH = pl.DeviceIdType.MESH
ANY = pl.BlockSpec(memory_space=pl.ANY)


def _tile(dim, target, align=LANE):
    if dim <= target:
        return dim
    t = (target // align) * align
    while t >= align:
        if dim % t == 0:
            return t
        t -= align
    return dim


def _params(*sem):
    return pltpu.CompilerParams(dimension_semantics=sem, vmem_limit_bytes=VMEM_LIMIT)


def _rstd(x):
    return lax.rsqrt(jnp.mean(x * x, axis=-1, keepdims=True) + NORM_EPS)


def _mm(a, b, *, name, ta=False, tb=False, out_dtype=F32, alpha=1.0, tm=512, tn=1024, tk=1024, dep=None):
    if ta:
        K, M = a.shape
    else:
        M, K = a.shape
    if tb:
        N, K2 = b.shape
    else:
        K2, N = b.shape
    assert K == K2, (a.shape, b.shape, ta, tb)
    tm, tn, tk = _tile(M, tm), _tile(N, tn), _tile(K, tk)
    nk = K // tk
    dims = (((0 if ta else 1,), (1 if tb else 0,)), ((), ()))

    def body(a_ref, b_ref, *rest):
        o_ref = rest[0 if dep is None else 1]
        acc_ref = rest[-1] if nk > 1 else None
        part = lax.dot_general(a_ref[...].astype(BF16), b_ref[...].astype(BF16), dims,
                               preferred_element_type=F32)

        def finish(acc):
            o_ref[...] = (acc * alpha if alpha != 1.0 else acc).astype(out_dtype)

        if nk == 1:
            finish(part)
        else:
            k = pl.program_id(2)

            @pl.when(k == 0)
            def _():
                acc_ref[...] = part

            @pl.when(k > 0)
            def _():
                acc_ref[...] += part

            @pl.when(k == nk - 1)
            def _():
                finish(acc_ref[...])

    a_spec = pl.BlockSpec((tk, tm), lambda i, j, k: (k, i)) if ta else pl.BlockSpec((tm, tk), lambda i, j, k: (i, k))
    b_spec = pl.BlockSpec((tn, tk), lambda i, j, k: (j, k)) if tb else pl.BlockSpec((tk, tn), lambda i, j, k: (k, j))
    return pl.pallas_call(
        body, name=name,
        out_shape=jax.ShapeDtypeStruct((M, N), out_dtype),
        grid=(M // tm, N // tn, nk),
        in_specs=[a_spec, b_spec] + _dep_spec(dep),
        out_specs=pl.BlockSpec((tm, tn), lambda i, j, k: (i, j)),
        scratch_shapes=[pltpu.VMEM((tm, tn), F32)] if nk > 1 else [],
        compiler_params=_params("parallel", "parallel", "arbitrary"),
    )(a, b, *([] if dep is None else [dep]))


def _dw_multi(pairs, *, name, tk=512):
    n = len(pairs)
    T = pairs[0][0].shape[0]
    tk = _tile(T, tk, 16)
    nk = T // tk
    shapes = [(a.shape[1], b.shape[1]) for a, b in pairs]

    def body(*refs):
        ins, outs, accs = refs[:2 * n], refs[2 * n:3 * n], refs[3 * n:]
        k = pl.program_id(0)
        parts = [lax.dot_general(ins[2 * i][...].astype(BF16), ins[2 * i + 1][...].astype(BF16), _TN,
                                 preferred_element_type=F32) for i in range(n)]

        @pl.when(k == 0)
        def _():
            for acc, part in zip(accs, parts):
                acc[...] = part

        @pl.when(k > 0)
        def _():
            for acc, part in zip(accs, parts):
                acc[...] += part

        @pl.when(k == nk - 1)
        def _():
            for out, acc in zip(outs, accs):
                out[...] = acc[...].astype(BF16)

    return pl.pallas_call(
        body, name=name,
        out_shape=tuple(jax.ShapeDtypeStruct(s, BF16) for s in shapes),
        grid=(nk,),
        in_specs=[pl.BlockSpec((tk, x.shape[1]), lambda k: (k, 0)) for pair in pairs for x in pair],
        out_specs=tuple(pl.BlockSpec(s, lambda k: (0, 0)) for s in shapes),
        scratch_shapes=[pltpu.VMEM(s, F32) for s in shapes],
        compiler_params=_params("arbitrary"),
    )(*[x for pair in pairs for x in pair])


def _rms_fwd(x, g, *, name):
    T, D = x.shape
    tm = _tile(T, 512, 16)

    def body(x_ref, g_ref, h_ref):
        x = x_ref[...]
        h_ref[...] = (x * _rstd(x) * g_ref[...]).astype(BF16)

    return pl.pallas_call(
        body, name=name,
        out_shape=jax.ShapeDtypeStruct((T, D), BF16),
        grid=(T // tm,),
        in_specs=[pl.BlockSpec((tm, D), lambda i: (i, 0)), pl.BlockSpec((1, D), lambda i: (0, 0))],
        out_specs=pl.BlockSpec((tm, D), lambda i: (i, 0)),
        compiler_params=_params("parallel"),
    )(x, g.reshape(1, D))


def _loss_head(x, g, target, *, name):
    T, D = x.shape
    tm = _tile(T, 512, 16)

    def body(x_ref, g_ref, t_ref, dx_ref, dg_ref, loss_ref):
        x = x_ref[...]
        gain = g_ref[...]
        r = _rstd(x)
        xhat = x * r
        err = xhat * gain - t_ref[...]
        dy = err * (1.0 / D)
        dxh = dy * gain
        dx_ref[...] = r * (dxh - xhat * jnp.mean(dxh * xhat, axis=-1, keepdims=True))
        dg_part = jnp.sum(dy * xhat, axis=0, keepdims=True)
        loss_part = jnp.full((1, LANE), 0.5 / D, F32) * jnp.sum(err * err)

        @pl.when(pl.program_id(0) == 0)
        def _():
            dg_ref[...] = dg_part
            loss_ref[...] = loss_part

        @pl.when(pl.program_id(0) > 0)
        def _():
            dg_ref[...] += dg_part
            loss_ref[...] += loss_part

    row = pl.BlockSpec((tm, D), lambda i: (i, 0))
    vec = pl.BlockSpec((1, D), lambda i: (0, 0))
    return pl.pallas_call(
        body, name=name,
        out_shape=(jax.ShapeDtypeStruct((T, D), F32), jax.ShapeDtypeStruct((1, D), F32),
                   jax.ShapeDtypeStruct((1, LANE), F32)),
        grid=(T // tm,),
        in_specs=[row, vec, row],
        out_specs=(row, vec, pl.BlockSpec((1, LANE), lambda i: (0, 0))),
        compiler_params=_params("arbitrary"),
    )(x, g.reshape(1, D), target)


def _ffn_fwd_core(x, h, w_up_t, wd, next_gain, *, alpha, name, dep=None):
    T, D = x.shape
    F = wd.shape[0]
    tm = _tile(T, 256, 16)
    has_norm = next_gain is not None

    def body(x_ref, h_ref, wg_ref, wu_ref, wd_ref, *rest):
        outs = rest[len(rest) - (5 if has_norm else 4):]
        gate_ref, up_ref, a_ref, xn_ref = outs[:4]
        h = h_ref[...]
        gate = lax.dot_general(h, wg_ref[...], _NT, preferred_element_type=F32)
        up = lax.dot_general(h, wu_ref[...], _NT, preferred_element_type=F32)
        a = (gate * jax.nn.sigmoid(gate) * up).astype(BF16)
        gate_ref[...] = gate.astype(BF16)
        up_ref[...] = up.astype(BF16)
        a_ref[...] = a
        xn = x_ref[...] + alpha * jnp.dot(a, wd_ref[...], preferred_element_type=F32)
        xn_ref[...] = xn
        if has_norm:
            outs[4][...] = (xn * _rstd(xn) * rest[0][...]).astype(BF16)

    once = pl.Buffered(1)
    row_d = pl.BlockSpec((tm, D), lambda i: (i, 0))
    row_f = pl.BlockSpec((tm, F), lambda i: (i, 0))
    vec = pl.BlockSpec((1, D), lambda i: (0, 0))
    act = jax.ShapeDtypeStruct((T, F), BF16)
    operands = [x, h, w_up_t, w_up_t, wd] + ([next_gain.reshape(1, D)] if has_norm else [])
    out = pl.pallas_call(
        body, name=name,
        out_shape=(act, act, act, jax.ShapeDtypeStruct((T, D), F32)) + ((jax.ShapeDtypeStruct((T, D), BF16),) if has_norm else ()),
        grid=(T // tm,),
        in_specs=[row_d, row_d,
                  pl.BlockSpec((F, D), lambda i: (0, 0), pipeline_mode=once),
                  pl.BlockSpec((F, D), lambda i: (1, 0), pipeline_mode=once),
                  pl.BlockSpec((F, D), lambda i: (0, 0), pipeline_mode=once)] + ([vec] if has_norm else []) + _dep_spec(dep),
        out_specs=(row_f, row_f, row_f, row_d) + ((row_d,) if has_norm else ()),
        compiler_params=_params("parallel"),
    )(*operands, *([] if dep is None else [dep]))
    return out if has_norm else (*out, None)


def _ffn_bwd_core(dxo, wd, w_up_t, gate, up, x, gain, *, alpha, name, dep=None):
    T, D = dxo.shape
    F = wd.shape[0]
    tm = _tile(T, 256, 16)

    def body(dxo_ref, wd_ref, wg_ref, wu_ref, gate_ref, up_ref, x_ref, g_ref, *rest):
        dgate_ref, dup_ref, dx_ref, dg_ref = rest[-4:]
        dxo = dxo_ref[...]
        da = lax.dot_general(dxo.astype(BF16), wd_ref[...], _NT, preferred_element_type=F32) * alpha
        gate = gate_ref[...].astype(F32)
        up = up_ref[...].astype(F32)
        sig = jax.nn.sigmoid(gate)
        dgate = (da * up * (sig * (1.0 + gate * (1.0 - sig)))).astype(BF16)
        dup = (da * (gate * sig)).astype(BF16)
        dgate_ref[...] = dgate
        dup_ref[...] = dup
        dh = (jnp.dot(dgate, wg_ref[...], preferred_element_type=F32)
              + jnp.dot(dup, wu_ref[...], preferred_element_type=F32))
        x = x_ref[...]
        r = _rstd(x)
        xhat = x * r
        dxh = dh * g_ref[...]
        dx_ref[...] = dxo + r * (dxh - xhat * jnp.mean(dxh * xhat, axis=-1, keepdims=True))
        part = jnp.sum(dh * xhat, axis=0, keepdims=True)

        @pl.when(pl.program_id(0) == 0)
        def _():
            dg_ref[...] = part

        @pl.when(pl.program_id(0) > 0)
        def _():
            dg_ref[...] += part

    once = pl.Buffered(1)
    row_d = pl.BlockSpec((tm, D), lambda i: (i, 0))
    row_f = pl.BlockSpec((tm, F), lambda i: (i, 0))
    vec = pl.BlockSpec((1, D), lambda i: (0, 0))
    act = jax.ShapeDtypeStruct((T, F), BF16)
    return pl.pallas_call(
        body, name=name,
        out_shape=(act, act, jax.ShapeDtypeStruct((T, D), F32), jax.ShapeDtypeStruct((1, D), F32)),
        grid=(T // tm,),
        in_specs=[row_d,
                  pl.BlockSpec((F, D), lambda i: (0, 0), pipeline_mode=once),
                  pl.BlockSpec((F, D), lambda i: (0, 0), pipeline_mode=once),
                  pl.BlockSpec((F, D), lambda i: (1, 0), pipeline_mode=once),
                  row_f, row_f, row_d, vec] + _dep_spec(dep),
        out_specs=(row_f, row_f, row_d, vec),
        compiler_params=_params("arbitrary"),
    )(dxo, wd, w_up_t, w_up_t, gate, up, x, gain.reshape(1, D), *([] if dep is None else [dep]))


def _ffn_dw_up(dgate, dup, h, *, name):
    T, F = dgate.shape
    D = h.shape[1]
    tm, tk = _tile(F, 1408), _tile(T, 1024, 16)
    nf, nk = F // tm, T // tk

    def body(dgate_ref, dup_ref, h_ref, o_ref, acc_ref):
        i, k = pl.program_id(0), pl.program_id(1)

        def accumulate(part):
            @pl.when(k == 0)
            def _():
                acc_ref[...] = part

            @pl.when(k > 0)
            def _():
                acc_ref[...] += part

        @pl.when(i < nf)
        def _():
            accumulate(lax.dot_general(dgate_ref[...], h_ref[...], _TN, preferred_element_type=F32))

        @pl.when(i >= nf)
        def _():
            accumulate(lax.dot_general(dup_ref[...], h_ref[...], _TN, preferred_element_type=F32))

        @pl.when(k == nk - 1)
        def _():
            o_ref[...] = acc_ref[...].astype(BF16)

    return pl.pallas_call(
        body, name=name,
        out_shape=jax.ShapeDtypeStruct((2 * F, D), BF16),
        grid=(2 * nf, nk),
        in_specs=[pl.BlockSpec((tk, tm), lambda i, k: (jnp.where(i < nf, k, nk - 1), jnp.minimum(i, nf - 1))),
                  pl.BlockSpec((tk, tm), lambda i, k: (jnp.where(i < nf, 0, k), jnp.maximum(i - nf, 0))),
                  pl.BlockSpec((tk, D), lambda i, k: (k, 0))],
        out_specs=pl.BlockSpec((tm, D), lambda i, k: (i, 0)),
        scratch_shapes=[pltpu.VMEM((tm, D), F32)],
        compiler_params=_params("parallel", "arbitrary"),
    )(dgate, dup, h)


def _dep_spec(dep):
    return [] if dep is None else [pl.BlockSpec(dep.shape, lambda *_: (0,) * dep.ndim)]


def _rope_tables(positions):
    half = QK_ROPE // 2
    inv_freq = ROPE_THETA ** (-jnp.arange(0, QK_ROPE, 2, dtype=F32) / QK_ROPE)
    ang = positions.astype(F32)[:, None] * inv_freq
    cos, sin = jnp.cos(ang), jnp.sin(ang)
    z = jnp.zeros_like(cos)
    zz = jnp.zeros((positions.shape[0], LANE - QK_ROPE), F32)
    c = jnp.concatenate([cos, cos, zz], axis=1)
    sa = jnp.concatenate([z, sin, zz], axis=1)
    sb = jnp.concatenate([-sin, z, zz], axis=1)
    return c, sa, sb


def _rotate(seg, c, sa, sb, sign):
    half = QK_ROPE // 2
    mix = pltpu.roll(seg, half, 1) * sa + pltpu.roll(seg, LANE - half, 1) * sb
    return seg * c + mix if sign > 0 else seg * c - mix


def _mixer_in(h, wa, wuq, wukv, gq, gkv, tabs, *, name, dep=None):
    T, D = h.shape
    HQ, QL = wuq.shape
    KVL = wukv.shape[0]
    H = HQ // HEAD_W
    o_q, o_kv, o_kr = POOL_DIM, POOL_DIM + QL, POOL_DIM + QL + KVL
    PA = o_kr + LANE
    assert wa.shape[0] >= PA
    tm = _tile(T, 512, 16)

    def body(h_ref, wa_ref, wuq_ref, wukv_ref, gq_ref, gkv_ref, c_ref, sa_ref, sb_ref, *rest):
        xp_ref, ql_ref, kvl_ref, qn_ref, kvn_ref, q_ref, kv_ref, kr_ref = rest[-8:]
        proj = lax.dot_general(h_ref[...], wa_ref[...], _NT, preferred_element_type=F32)
        xp_ref[...] = proj[:, :POOL_DIM]
        ql = proj[:, o_q:o_kv]
        kvl = proj[:, o_kv:o_kr]
        ql_ref[...] = ql
        kvl_ref[...] = kvl
        qn = (ql * _rstd(ql) * gq_ref[...]).astype(BF16)
        kvn = (kvl * _rstd(kvl) * gkv_ref[...]).astype(BF16)
        qn_ref[...] = qn
        kvn_ref[...] = kvn
        c, sa, sb = c_ref[...], sa_ref[...], sb_ref[...]
        q = lax.dot_general(qn, wuq_ref[...], _NT, preferred_element_type=F32)
        for hh in range(H):
            base = hh * HEAD_W
            q_ref[:, base:base + QK_NOPE] = q[:, base:base + QK_NOPE].astype(BF16)
            q_ref[:, base + QK_NOPE:base + HEAD_W] = _rotate(
                q[:, base + QK_NOPE:base + HEAD_W], c, sa, sb, 1).astype(BF16)
        kv_ref[...] = jnp.dot(kvn, wukv_ref[...], preferred_element_type=F32).astype(BF16)
        kr_ref[...] = _rotate(proj[:, o_kr:o_kr + LANE], c, sa, sb, 1).astype(BF16)

    def row(w):
        return pl.BlockSpec((tm, w), lambda i: (i, 0))

    def whole(arr):
        return pl.BlockSpec(arr.shape, lambda i: (0,) * arr.ndim)

    gq2, gkv2 = gq.reshape(1, QL), gkv.reshape(1, KVL)
    outs = [(POOL_DIM, F32), (QL, F32), (KVL, F32), (QL, BF16), (KVL, BF16), (HQ, BF16), (HQ, BF16), (LANE, BF16)]
    return pl.pallas_call(
        body, name=name,
        out_shape=tuple(jax.ShapeDtypeStruct((T, w), dt) for w, dt in outs),
        grid=(T // tm,),
        in_specs=[row(D), pl.BlockSpec((PA, D), lambda i: (0, 0)), whole(wuq), whole(wukv), whole(gq2), whole(gkv2),
                  row(LANE), row(LANE), row(LANE)] + _dep_spec(dep),
        out_specs=tuple(row(w) for w, _ in outs),
        compiler_params=_params("parallel"),
    )(h, wa, wuq, wukv, gq2, gkv2, *tabs, *([] if dep is None else [dep]))


def _mixer_in_bwd(dq, dkv, dkr, ql, kvl, dxp, dgl, x, dxo, win_t, wuq, wukv, g_mix, gq, gkv, tabs, *, name):
    T, HQ = dq.shape
    D = x.shape[1]
    QL, KVL = wuq.shape[1], wukv.shape[0]
    H = HQ // HEAD_W
    PA = POOL_DIM + QL + KVL + LANE
    o_q, o_kv, o_kr = POOL_DIM, POOL_DIM + QL, POOL_DIM + QL + KVL
    tm = _tile(T, 256, 16)

    def norm_bwd(lat, gain, dn):
        r = _rstd(lat)
        xhat = lat * r
        dxh = dn * gain
        dlat = r * (dxh - xhat * jnp.mean(dxh * xhat, axis=-1, keepdims=True))
        return dlat, jnp.sum(dn * xhat, axis=0, keepdims=True)

    def body(dq_ref, dkv_ref, dkr_ref, ql_ref, kvl_ref, dxp_ref, dgl_ref, x_ref, dxo_ref, win_ref,
             wuq_ref, wukv_ref, gmix_ref, gq_ref, gkv_ref, c_ref, sa_ref, sb_ref,
             dproj_ref, dqp_ref, dgq_ref, dgkv_ref, dx_ref, dgmix_ref):
        c, sa, sb = c_ref[...], sa_ref[...], sb_ref[...]
        for hh in range(H):
            base = hh * HEAD_W
            dqp_ref[:, base:base + QK_NOPE] = dq_ref[:, base:base + QK_NOPE]
            dqp_ref[:, base + QK_NOPE:base + HEAD_W] = _rotate(
                dq_ref[:, base + QK_NOPE:base + HEAD_W].astype(F32), c, sa, sb, -1).astype(BF16)
        dqn = jnp.dot(dqp_ref[...], wuq_ref[...], preferred_element_type=F32)
        dkvn = lax.dot_general(dkv_ref[...], wukv_ref[...], _NT, preferred_element_type=F32)
        dql, dgq = norm_bwd(ql_ref[...], gq_ref[...], dqn)
        dkvl, dgkv = norm_bwd(kvl_ref[...], gkv_ref[...], dkvn)
        dproj_ref[:, :POOL_DIM] = dxp_ref[...].astype(BF16)
        dproj_ref[:, o_q:o_kv] = dql.astype(BF16)
        dproj_ref[:, o_kv:o_kr] = dkvl.astype(BF16)
        dproj_ref[:, o_kr:PA] = _rotate(dkr_ref[...], c, sa, sb, -1).astype(BF16)

        n_gate = win_ref.shape[0] - 2 * D
        dh = (jnp.dot(dproj_ref[...], win_ref[:PA, :], preferred_element_type=F32)
              + jnp.dot(dgl_ref[...], win_ref[n_gate:, :], preferred_element_type=F32))
        x = x_ref[...]
        r = _rstd(x)
        xhat = x * r
        dxh = dh * gmix_ref[...]
        dx_ref[...] = dxo_ref[...] + r * (dxh - xhat * jnp.mean(dxh * xhat, axis=-1, keepdims=True))
        dgmix = jnp.sum(dh * xhat, axis=0, keepdims=True)

        @pl.when(pl.program_id(0) == 0)
        def _():
            dgq_ref[...] = dgq
            dgkv_ref[...] = dgkv
            dgmix_ref[...] = dgmix

        @pl.when(pl.program_id(0) > 0)
        def _():
            dgq_ref[...] += dgq
            dgkv_ref[...] += dgkv
            dgmix_ref[...] += dgmix

    def row(w):
        return pl.BlockSpec((tm, w), lambda i: (i, 0))

    def resident(arr):
        return pl.BlockSpec(arr.shape, lambda i: (0, 0), pipeline_mode=pl.Buffered(1))

    gmix2, gq2, gkv2 = g_mix.reshape(1, D), gq.reshape(1, QL), gkv.reshape(1, KVL)
    vec = pl.BlockSpec((1, D), lambda i: (0, 0))
    vq, vkv = pl.BlockSpec((1, QL), lambda i: (0, 0)), pl.BlockSpec((1, KVL), lambda i: (0, 0))
    return pl.pallas_call(
        body, name=name,
        out_shape=(jax.ShapeDtypeStruct((T, PA), BF16), jax.ShapeDtypeStruct((T, HQ), BF16),
                   jax.ShapeDtypeStruct((1, QL), F32), jax.ShapeDtypeStruct((1, KVL), F32),
                   jax.ShapeDtypeStruct((T, D), F32), jax.ShapeDtypeStruct((1, D), F32)),
        grid=(T // tm,),
        in_specs=[row(HQ), row(HQ), row(LANE), row(QL), row(KVL), row(POOL_DIM), row(2 * D), row(D), row(D),
                  resident(win_t), resident(wuq), resident(wukv),
                  vec, vq, vkv, row(LANE), row(LANE), row(LANE)],
        out_specs=(row(PA), row(HQ), vq, vkv, row(D), vec),
        compiler_params=_params("arbitrary"),
    )(dq, dkv, dkr, ql, kvl, dxp, dgl, x, dxo, win_t, wuq, wukv, gmix2, gq2, gkv2, *tabs)


def _pool_groups(x_of, S):
    row = lax.broadcasted_iota(jnp.int32, (S, POOL_G), 0)
    for g, w in enumerate(POOL_WINDOWS):
        x = x_of(g)
        s = x
        d = 1
        while d < w:
            s = s + jnp.where(row >= d, pltpu.roll(s, d, 0), 0.0)
            d *= 2
        cnt = jnp.minimum(row + 1, w).astype(F32)
        yield g, w, x, s / cnt - x, cnt, row


def _pool_fwd(xp, maps, scale, *, S, name):
    T = xp.shape[0]

    def body(xp_ref, maps_ref, scale_ref, ms_ref):
        for g, _, _, pooled, _, _ in _pool_groups(lambda g: xp_ref[:, g * POOL_G:(g + 1) * POOL_G], S):
            mixed = jnp.dot(pooled.astype(BF16), maps_ref[g].astype(BF16), preferred_element_type=F32)
            ms_ref[:, g * POOL_G:(g + 1) * POOL_G] = (mixed * scale_ref[:, g * POOL_G:(g + 1) * POOL_G]).astype(BF16)

    return pl.pallas_call(
        body, name=name,
        out_shape=jax.ShapeDtypeStruct((T, POOL_DIM), BF16),
        grid=(T // S,),
        in_specs=[pl.BlockSpec((S, POOL_DIM), lambda b: (b, 0)),
                  pl.BlockSpec(maps.shape, lambda b: (0, 0, 0)),
                  pl.BlockSpec((1, POOL_DIM), lambda b: (0, 0))],
        out_specs=pl.BlockSpec((S, POOL_DIM), lambda b: (b, 0)),
        compiler_params=_params("parallel"),
    )(xp, maps, scale.reshape(1, POOL_DIM))


def _pool_bwd(xp, dms, maps, scale, *, S, name):
    T = xp.shape[0]

    def body(xp_ref, dms_ref, maps_ref, scale_ref, dxp_ref, dmaps_ref, dscale_ref):
        first = pl.program_id(0) == 0
        for g, w, _, pooled, cnt, row in _pool_groups(lambda g: xp_ref[:, g * POOL_G:(g + 1) * POOL_G], S):
            cols = slice(g * POOL_G, (g + 1) * POOL_G)
            pooled_b = pooled.astype(BF16)
            maps_b = maps_ref[g].astype(BF16)
            mixed = jnp.dot(pooled_b, maps_b, preferred_element_type=F32)
            dms = dms_ref[:, cols]
            dscale = jnp.sum(dms * mixed, axis=0, keepdims=True)
            dmixed = (dms * scale_ref[:, cols]).astype(BF16)
            dmaps = lax.dot_general(pooled_b, dmixed, (((0,), (0,)), ((), ())), preferred_element_type=F32)
            dpooled = lax.dot_general(dmixed, maps_b, (((1,), (1,)), ((), ())), preferred_element_type=F32)
            z = dpooled / cnt
            d = 1
            while d < w:
                z = z + jnp.where(row < S - d, pltpu.roll(z, S - d, 0), 0.0)
                d *= 2
            dxp_ref[:, cols] = z - dpooled

            @pl.when(first)
            def _():
                dmaps_ref[g] = dmaps
                dscale_ref[:, cols] = dscale

            @pl.when(jnp.logical_not(first))
            def _():
                dmaps_ref[g] += dmaps
                dscale_ref[:, cols] += dscale

    seq = pl.BlockSpec((S, POOL_DIM), lambda b: (b, 0))
    maps_spec = pl.BlockSpec(maps.shape, lambda b: (0, 0, 0))
    vec = pl.BlockSpec((1, POOL_DIM), lambda b: (0, 0))
    return pl.pallas_call(
        body, name=name,
        out_shape=(jax.ShapeDtypeStruct((T, POOL_DIM), F32), jax.ShapeDtypeStruct(maps.shape, F32),
                   jax.ShapeDtypeStruct((1, POOL_DIM), F32)),
        grid=(T // S,),
        in_specs=[seq, seq, maps_spec, vec],
        out_specs=(seq, maps_spec, vec),
        compiler_params=_params("arbitrary"),
    )(xp, dms, maps, scale.reshape(1, POOL_DIM))


def _causal_mask(s, t):
    r = lax.broadcasted_iota(jnp.int32, (t, t), 0)
    c = lax.broadcasted_iota(jnp.int32, (t, t), 1)
    return jnp.where(r >= c, s, MASK_VALUE)


_NT = (((1,), (1,)), ((), ()))
_TN = (((0,), (0,)), ((), ()))


def _attn_fwd(q, kv, kr, *, S, name):
    T, HQ = q.shape
    H = HQ // HEAD_W
    B = T // S
    t = _tile(S, ATTN_TILE)
    n = S // t

    def body(q_ref, k_ref, v_ref, kr_ref, o_ref, lse_ref, kcat, vcat):
        kcat[:, :QK_NOPE] = k_ref[...]
        kcat[:, QK_NOPE:] = kr_ref[...]
        vcat[:, :V_DIM] = v_ref[...]
        vcat[:, V_DIM:] = jnp.ones((S, HEAD_W - V_DIM), BF16)
        for i in range(n):
            rows = slice(i * t, (i + 1) * t)
            qt = q_ref[rows, :]
            m = jnp.full((t, 1), MASK_VALUE, F32)
            acc = jnp.zeros((t, HEAD_W), F32)
            for j in range(i + 1):
                cols = slice(j * t, (j + 1) * t)
                s = lax.dot_general(qt, kcat[cols, :], _NT, preferred_element_type=F32) * ATTN_SCALE_LOG2
                if j == i:
                    s = _causal_mask(s, t)
                m_new = jnp.maximum(m, jnp.max(s, axis=1, keepdims=True))
                p = jnp.exp2(s - m_new)
                acc = jnp.exp2(m - m_new) * acc + jnp.dot(p.astype(BF16), vcat[cols, :], preferred_element_type=F32)
                m = m_new
            l = acc[:, V_DIM:V_DIM + 1]
            o_ref[rows, :] = (acc[:, :V_DIM] / l).astype(BF16)
            lse_ref[rows, :] = jnp.broadcast_to(m + jnp.log2(l), (t, LANE))

    seq_h = pl.BlockSpec((S, LANE), lambda b, h: (b, h))
    return pl.pallas_call(
        body, name=name,
        out_shape=(jax.ShapeDtypeStruct((T, H * V_DIM), BF16), jax.ShapeDtypeStruct((T, H * LANE), F32)),
        grid=(B, H),
        in_specs=[pl.BlockSpec((S, HEAD_W), lambda b, h: (b, h)),
                  pl.BlockSpec((S, QK_NOPE), lambda b, h: (b, 2 * h)),
                  pl.BlockSpec((S, V_DIM), lambda b, h: (b, 2 * h + 1)),
                  pl.BlockSpec((S, LANE), lambda b, h: (b, 0))],
        out_specs=(seq_h, seq_h),
        scratch_shapes=[pltpu.VMEM((S, HEAD_W), BF16), pltpu.VMEM((S, HEAD_W), BF16)],
        compiler_params=_params("parallel", "parallel"),
    )(q, kv, kv, kr)


def _attn_bwd(q, kv, kr, o, do, lse, *, S, name):
    T, HQ = q.shape
    H = HQ // HEAD_W
    B = T // S
    t = _tile(S, ATTN_TILE)
    n = S // t

    def body(q_ref, k_ref, v_ref, kr_ref, o_ref, do_ref, lse_ref, dq_ref, dkv_ref, dkr_ref, kcat, dq_acc):
        @pl.when(pl.program_id(1) == 0)
        def _():
            dkr_ref[...] = jnp.zeros_like(dkr_ref)

        kcat[:, :QK_NOPE] = k_ref[...]
        kcat[:, QK_NOPE:] = kr_ref[...]
        delta = [jnp.sum(do_ref[i * t:(i + 1) * t, :].astype(F32) * o_ref[i * t:(i + 1) * t, :].astype(F32),
                         axis=1, keepdims=True) for i in range(n)]
        for j in range(n):
            cols = slice(j * t, (j + 1) * t)
            kc = kcat[cols, :]
            vt = v_ref[cols, :]
            dk = jnp.zeros((t, HEAD_W), F32)
            dv = jnp.zeros((t, V_DIM), F32)
            for i in range(j, n):
                rows = slice(i * t, (i + 1) * t)
                qt = q_ref[rows, :]
                dot_ = do_ref[rows, :]
                s = lax.dot_general(qt, kc, _NT, preferred_element_type=F32) * ATTN_SCALE_LOG2
                if i == j:
                    s = _causal_mask(s, t)
                p = jnp.exp2(s - lse_ref[rows, :][:, :1])
                dv = dv + lax.dot_general(p.astype(BF16), dot_, _TN, preferred_element_type=F32)
                dp = lax.dot_general(dot_, vt, _NT, preferred_element_type=F32)
                ds = (p * (dp - delta[i]) * ATTN_SCALE).astype(BF16)
                dk = dk + lax.dot_general(ds, qt, _TN, preferred_element_type=F32)
                dq_part = jnp.dot(ds, kc, preferred_element_type=F32)
                if j == 0:
                    dq_acc[rows, :] = dq_part
                else:
                    dq_acc[rows, :] += dq_part
            dkv_ref[cols, :QK_NOPE] = dk[:, :QK_NOPE].astype(BF16)
            dkv_ref[cols, QK_NOPE:] = dv.astype(BF16)
            dkr_ref[cols, :] += dk[:, QK_NOPE:]
        dq_ref[...] = dq_acc[...].astype(BF16)

    seq_q = pl.BlockSpec((S, HEAD_W), lambda b, h: (b, h))
    seq_h = pl.BlockSpec((S, LANE), lambda b, h: (b, h))
    seq_shared = pl.BlockSpec((S, LANE), lambda b, h: (b, 0))
    return pl.pallas_call(
        body, name=name,
        out_shape=(jax.ShapeDtypeStruct((T, HQ), BF16), jax.ShapeDtypeStruct((T, HQ), BF16),
                   jax.ShapeDtypeStruct((T, LANE), F32)),
        grid=(B, H),
        in_specs=[seq_q,
                  pl.BlockSpec((S, QK_NOPE), lambda b, h: (b, 2 * h)),
                  pl.BlockSpec((S, V_DIM), lambda b, h: (b, 2 * h + 1)),
                  seq_shared, seq_h, seq_h, seq_h],
        out_specs=(seq_q, seq_q, seq_shared),
        scratch_shapes=[pltpu.VMEM((S, HEAD_W), BF16), pltpu.VMEM((S, HEAD_W), F32)],
        compiler_params=_params("parallel", "arbitrary"),
    )(q, kv, kv, kr, o, do, lse)


def _merge_out(h, ms, o, x, win_t, bgate, wpp, wap, wout, next_gain, *, name):
    T, D = x.shape
    tm = _tile(T, 256, 16)
    n_gate = win_t.shape[0] - 2 * D

    def body(h_ref, ms_ref, o_ref, x_ref, win_ref, bgate_ref, wpp_ref, wap_ref, wout_ref, ng_ref,
             gates_ref, ba_ref, bb_ref, merged_ref, xn_ref, hn_ref):
        logits = lax.dot_general(h_ref[...], win_ref[n_gate:, :], _NT, preferred_element_type=F32) + bgate_ref[...]
        gates = jax.nn.sigmoid(logits)
        ba = jnp.dot(ms_ref[...], wpp_ref[...], preferred_element_type=F32)
        bb = jnp.dot(o_ref[...], wap_ref[...], preferred_element_type=F32)
        merged = (gates[:, :D] * ba + gates[:, D:] * bb).astype(BF16)
        gates_ref[...] = gates.astype(BF16)
        ba_ref[...] = ba.astype(BF16)
        bb_ref[...] = bb.astype(BF16)
        merged_ref[...] = merged
        xn = x_ref[...] + jnp.dot(merged, wout_ref[...], preferred_element_type=F32)
        xn_ref[...] = xn
        hn_ref[...] = (xn * _rstd(xn) * ng_ref[...]).astype(BF16)

    def row(w):
        return pl.BlockSpec((tm, w), lambda i: (i, 0))

    def whole(arr):
        return pl.BlockSpec(arr.shape, lambda i: (0,) * arr.ndim, pipeline_mode=pl.Buffered(1))

    bg2, ng2 = bgate.reshape(1, 2 * D), next_gain.reshape(1, D)
    act = jax.ShapeDtypeStruct((T, D), BF16)
    return pl.pallas_call(
        body, name=name,
        out_shape=(jax.ShapeDtypeStruct((T, 2 * D), BF16), act, act, act, jax.ShapeDtypeStruct((T, D), F32), act),
        grid=(T // tm,),
        in_specs=[row(D), row(ms.shape[1]), row(o.shape[1]), row(D), whole(win_t), whole(bg2), whole(wpp),
                  whole(wap), whole(wout), whole(ng2)],
        out_specs=(row(2 * D), row(D), row(D), row(D), row(D), row(D)),
        compiler_params=_params("parallel"),
    )(h, ms, o, x, win_t, bg2, wpp, wap, wout, ng2)


def _merge_bwd(dxo, wout, wpp, wap, gates, ba, bb, *, name, dep=None):
    T, D = dxo.shape
    tm = _tile(T, 512, 16)

    def body(dxo_ref, wout_ref, wpp_ref, wap_ref, gates_ref, ba_ref, bb_ref, *rest):
        dba_ref, dbb_ref, dgl_ref, dbg_ref, dms_ref, do_ref = rest[-6:]
        dm = lax.dot_general(dxo_ref[...].astype(BF16), wout_ref[...], _NT, preferred_element_type=F32)
        ga = gates_ref[:, :D].astype(F32)
        gb = gates_ref[:, D:].astype(F32)
        dba = (dm * ga).astype(BF16)
        dbb = (dm * gb).astype(BF16)
        dba_ref[...] = dba
        dbb_ref[...] = dbb
        dms_ref[...] = lax.dot_general(dba, wpp_ref[...], _NT, preferred_element_type=F32)
        do_ref[...] = lax.dot_general(dbb, wap_ref[...], _NT, preferred_element_type=F32).astype(BF16)
        dgl_a = dm * ba_ref[...].astype(F32) * (ga * (1.0 - ga))
        dgl_b = dm * bb_ref[...].astype(F32) * (gb * (1.0 - gb))
        dgl_ref[:, :D] = dgl_a.astype(BF16)
        dgl_ref[:, D:] = dgl_b.astype(BF16)
        sa = jnp.sum(dgl_a, axis=0, keepdims=True)
        sb = jnp.sum(dgl_b, axis=0, keepdims=True)

        @pl.when(pl.program_id(0) == 0)
        def _():
            dbg_ref[:, :D] = sa
            dbg_ref[:, D:] = sb

        @pl.when(pl.program_id(0) > 0)
        def _():
            dbg_ref[:, :D] += sa
            dbg_ref[:, D:] += sb

    def row(w):
        return pl.BlockSpec((tm, w), lambda i: (i, 0))

    def whole(arr):
        return pl.BlockSpec(arr.shape, lambda i: (0, 0))

    P, HV = wpp.shape[0], wap.shape[0]
    act = jax.ShapeDtypeStruct((T, D), BF16)
    return pl.pallas_call(
        body, name=name,
        out_shape=(act, act, jax.ShapeDtypeStruct((T, 2 * D), BF16), jax.ShapeDtypeStruct((1, 2 * D), F32),
                   jax.ShapeDtypeStruct((T, P), F32), jax.ShapeDtypeStruct((T, HV), BF16)),
        grid=(T // tm,),
        in_specs=[row(D), whole(wout), whole(wpp), whole(wap), row(2 * D), row(D), row(D)] + _dep_spec(dep),
        out_specs=(row(D), row(D), row(2 * D), pl.BlockSpec((1, 2 * D), lambda i: (0, 0)), row(P), row(HV)),
        compiler_params=_params("arbitrary"),
    )(dxo, wout, wpp, wap, gates, ba, bb, *([] if dep is None else [dep]))


def _ffn_fwd(x, h, w, tag, next_gain, dep=None):
    gate, up, a, xn, hn = _ffn_fwd_core(x, h, w["up_t"], w["wd"], next_gain, alpha=0.5,
                                        name=f"{tag}_fwd" if next_gain is not None else f"{tag}_fwd_last", dep=dep)
    return xn, hn, (x, h, gate, up, a)


def _ffn_bwd(dxo, gain, w, saved, tag, dep=None, early=None):
    x, h, gate, up, a = saved
    dwd = _mm(a, dxo, ta=True, alpha=0.5, out_dtype=BF16, name=f"{tag}_dwd", tm=1408, tn=1024, tk=1024, dep=dep)
    if early is not None:
        dep = early(dwd)
    dgate, dup, dx, dgain = _ffn_bwd_core(dxo, w["wd"], w["up_t"], gate, up, x, gain, alpha=0.5,
                                          name=f"{tag}_bwd_core", dep=dep)
    dup_t = _ffn_dw_up(dgate, dup, h, name=f"{tag}_dw_up")
    return dx, dgain, dup_t, dwd


def _mixer_fwd(x, h, p, w, tabs, S, next_gain, dep=None):
    xp, ql, kvl, qn, kvn, q, kv, kr = _mixer_in(h, w["win_t"], w["wuq_t"], w["wukv"], p["q_latent_norm"],
                                                 p["kv_latent_norm"], tabs, name="mix_in", dep=dep)
    ms = _pool_fwd(xp, p["pool_maps"], p["pool_scale"], S=S, name="pool_fwd")
    o, lse = _attn_fwd(q, kv, kr, S=S, name="attn_fwd")
    gates, ba, bb, merged, xn, hn = _merge_out(h, ms, o, x, w["win_t"], p["b_gate"], w["wpp"], w["wap"], w["wout"],
                                               next_gain, name="merge_out")
    return xn, hn, (x, h, xp, ql, kvl, qn, kvn, q, kv, kr, ms, o, lse, gates, ba, bb, merged)


def _mixer_bwd(dxo, p, w, tabs, saved, S, dep=None):
    x, h, xp, ql, kvl, qn, kvn, q, kv, kr, ms, o, lse, gates, ba, bb, merged = saved
    dba, dbb, dgl, dbg, dms, do = _merge_bwd(dxo, w["wout"], w["wpp"], w["wap"], gates, ba, bb, name="merge_bwd",
                                             dep=dep)
    g = {}
    g["wout"], g["wpp"], g["wap"] = _dw_multi([(merged, dxo), (ms, dba), (o, dbb)], name="d_w_merge")
    dxp, g["pool_maps"], g["pool_scale"] = _pool_bwd(xp, dms, p["pool_maps"], p["pool_scale"], S=S, name="pool_bwd")
    dq, dkv, dkr = _attn_bwd(q, kv, kr, o, do, lse, S=S, name="attn_bwd")
    dproj, dqp, g["q_latent_norm"], g["kv_latent_norm"], dx, g["norm_mix"] = _mixer_in_bwd(
        dq, dkv, dkr, ql, kvl, dxp, dgl, x, dxo, w["win_t"], w["wuq_t"], w["wukv"],
        p["norm_mix"], p["q_latent_norm"], p["kv_latent_norm"], tabs, name="mix_in_bwd")
    g["wuq_t"], g["wukv"] = _dw_multi([(dqp, qn), (kvn, dkv)], name="d_w_qkv", tk=1024)
    g["wa_t"], g["wgate_t"] = _dw_multi([(dproj, h), (dgl, h)], name="d_w_in")
    g["b_gate"] = dbg
    return dx, g


BIG = ("ffn1_up", "ffn1_down", "w_in", "w_pool_proj", "w_uq", "w_ukv", "w_attn_proj", "w_out", "ffn2_up", "ffn2_down")
SMALL = ("norm_ffn1", "norm_mix", "b_gate", "pool_maps", "pool_scale", "q_latent_norm", "kv_latent_norm", "norm_ffn2")
PACKED = ("w_pool_proj", "w_uq", "w_ukv")
TRANSPOSED = ("ffn1_up", "ffn2_up", "w_in", "w_uq")
COL_SHARDED = ("w_pool_proj", "w_ukv")
QK_HEAD = QK_NOPE + QK_ROPE


def _rows(stacked):
    n, r, c = stacked.shape
    return stacked.reshape(n * r, c)


def _cols(stacked):
    n, k, c = stacked.shape
    return stacked.transpose(1, 0, 2).reshape(k, n * c)


FFN1_PART = ("ffn1_up", "ffn1_down")
MIXER_PART = ("w_in", "w_attn_proj", "w_out") + PACKED
FFN2_PART = ("ffn2_up", "ffn2_down")


def _kernel_weights(stacked):
    full = {}
    for tag in ("ffn1", "ffn2"):
        if tag + "_up" in stacked:
            full[tag] = {"up_t": _rows(stacked[tag + "_up"]), "wd": _rows(stacked[tag + "_down"])}
    if "w_in" in stacked:
        win_t = _rows(stacked["w_in"])
        wuq_t = _rows(stacked["w_uq"])
        QL = wuq_t.shape[1]
        H = wuq_t.shape[0] // QK_HEAD
        wuq_t = jnp.pad(wuq_t.reshape(H, QK_HEAD, QL), ((0, 0), (0, HEAD_W - QK_HEAD), (0, 0)))
        full.update({"win_t": win_t, "wuq_t": wuq_t.reshape(H * HEAD_W, QL),
                     "wukv": _cols(stacked["w_ukv"]), "wpp": _cols(stacked["w_pool_proj"]),
                     "wap": _rows(stacked["w_attn_proj"]), "wout": _rows(stacked["w_out"])})
    return full


def _split_rows(full):
    return full.reshape(N_DEV, full.shape[0] // N_DEV, full.shape[1])


def _split_cols(full):
    k, cols = full.shape
    return full.reshape(k, N_DEV, cols // N_DEV).transpose(1, 0, 2)


def _mixer_grads_stacked(g):
    n_a = g["wa_t"].shape[0] - (LANE - QK_ROPE)
    HQ, QL = g["wuq_t"].shape
    H = HQ // HEAD_W
    wuq_t = g["wuq_t"].reshape(H, HEAD_W, QL)[:, :QK_HEAD].reshape(H * QK_HEAD, QL)
    return {"w_in": _split_rows(jnp.concatenate([g["wa_t"][:n_a], g["wgate_t"]], axis=0)),
            "w_uq": _split_rows(wuq_t),
            "w_pool_proj": _split_cols(g["wpp"]), "w_ukv": _split_cols(g["wukv"]),
            "w_attn_proj": _split_rows(g["wap"]), "w_out": _split_rows(g["wout"])}


def _mesh_place():
    x, y, c = lax.axis_index("x"), lax.axis_index("y"), lax.axis_index("c")
    chips = [(1 - x, y), (x, 1 - y), (1 - x, 1 - y)]
    return x, y, c, chips


HBM = pl.BlockSpec(memory_space=pltpu.HBM)
SEMAPHORES = pl.BlockSpec(memory_space=pltpu.SEMAPHORE)
DATAFLOW = pltpu.SideEffectType.DATAFLOW_SIDE_EFFECTING
GATHER_PEERS = 4
SCATTER_PEERS = 7


def _in_hbm(a):
    return pltpu.with_memory_space_constraint(a, pltpu.HBM)


def _gather_plan(src_refs, land_refs):
    x, y, c, chips = _mesh_place()
    me = 4 * x + 2 * y + c
    targets = [(x, y, 1 - c)] + [(cx, cy, c) for cx, cy in chips]
    return [(s, land.at[me], to) for s, land in zip(src_refs, land_refs) for to in targets]


def _scatter_plan(src_refs, land_refs):
    x, y, c, _ = _mesh_place()
    peers = [(x, y, 1 - c), (1 - x, y, c), (x, 1 - y, c), (1 - x, 1 - y, c),
             (1 - x, y, 1 - c), (x, 1 - y, 1 - c), (1 - x, 1 - y, 1 - c)]
    return [(s.at[4 * px + 2 * py + pc], land.at[k], (px, py, pc))
            for s, land in zip(src_refs, land_refs) for k, (px, py, pc) in enumerate(peers)]


def _descriptors(plan, src_refs, land_refs, send_sems, recv_sems):
    return [pltpu.make_async_remote_copy(src_ref=s, dst_ref=d, send_sem=send_sems.at[k], recv_sem=recv_sems.at[k],
                                         device_id=to, device_id_type=MESH)
            for k, (s, d, to) in enumerate(plan(src_refs, land_refs))]


def _exchange(srcs, land_shapes, plan, per_src, *, name):
    n = len(srcs)

    def body(*refs):
        copies = _descriptors(plan, refs[:n], refs[n:2 * n], refs[2 * n], refs[2 * n + 1])
        for cp in copies:
            cp.start()
        for cp in copies:
            cp.wait()

    return pl.pallas_call(
        body, name=name,
        out_shape=tuple(jax.ShapeDtypeStruct(shape, s.dtype) for shape, s in zip(land_shapes, srcs)),
        in_specs=[ANY] * n, out_specs=(ANY,) * n,
        scratch_shapes=[pltpu.SemaphoreType.DMA((per_src * n,)), pltpu.SemaphoreType.DMA((per_src * n,))],
    )(*srcs)


FORWARD_COPIES = 4


def _forward_slots():
    x, y, c, chips = _mesh_place()
    return [4 * cx + 2 * cy + c for cx, cy in chips] + [4 * x + 2 * y + (1 - c)], (x, y, 1 - c)


def _forward_plan(src_refs, land_refs):
    slots, sibling = _forward_slots()
    return [(land.at[s], land.at[s], sibling) for land in land_refs for s in slots]


def _gather_all_plan(src_refs, land_refs):
    x, y, c, _ = _mesh_place()
    me = 4 * x + 2 * y + c
    peers = [(x, y, 1 - c), (1 - x, y, c), (x, 1 - y, c), (1 - x, 1 - y, c),
             (1 - x, y, 1 - c), (x, 1 - y, 1 - c), (1 - x, 1 - y, 1 - c)]
    return [(s, land.at[me], to) for s, land in zip(src_refs, land_refs) for to in peers]


def _exchange_start(srcs, lands, plan, n_copies, *, name):
    ns, n = len(srcs), len(srcs) + len(lands)

    def body(*refs):
        for cp in _descriptors(plan, refs[:ns], refs[ns:n], refs[n], refs[n + 1]):
            cp.start()
        refs[-1][...] = jnp.zeros_like(refs[-1])

    sems = pltpu.SemaphoreType.DMA((n_copies,))
    out = pl.pallas_call(
        body, name=name,
        out_shape=(sems, sems, *[pltpu.HBM(a.shape, a.dtype) for a in srcs + lands],
                   jax.ShapeDtypeStruct((8, LANE), F32)),
        in_specs=(HBM,) * n,
        out_specs=(SEMAPHORES, SEMAPHORES, *[HBM] * n, pl.BlockSpec(memory_space=pltpu.VMEM)),
        input_output_aliases={i: 2 + i for i in range(n)},
        compiler_params=pltpu.CompilerParams(has_side_effects=DATAFLOW),
    )(*[_in_hbm(a) for a in srcs + lands])
    return out[0], out[1], list(out[2:2 + ns]), list(out[2 + ns:2 + n]), out[-1]


def _exchange_wait(send_sems, recv_sems, srcs, lands, plan, after, *, name):
    ns, n = len(srcs), len(srcs) + len(lands)

    def body(*refs):
        for cp in _descriptors(plan, refs[:ns], refs[ns:n], refs[n], refs[n + 1]):
            cp.wait_send()
            cp.wait_recv()

    out = pl.pallas_call(
        body, name=name,
        out_shape=tuple(pltpu.HBM(a.shape, a.dtype) for a in srcs + lands),
        in_specs=(*[HBM] * n, SEMAPHORES, SEMAPHORES, ANY),
        out_specs=(HBM,) * n,
        input_output_aliases={i: i for i in range(n)},
        compiler_params=pltpu.CompilerParams(has_side_effects=DATAFLOW),
    )(*srcs, *lands, send_sems, recv_sems, after)
    return list(out[:ns]), list(out[ns:])


def _gather_forward(lands, *, name):
    n = len(lands)

    def body(*refs):
        in_refs, out_refs = refs[:n], refs[n:2 * n]
        token, send_sems, recv_sems = refs[2 * n:2 * n + 3]
        slots, sibling = _forward_slots()
        passed = [pltpu.make_async_remote_copy(
            src_ref=i.at[s], dst_ref=o.at[s],
            send_sem=send_sems.at[FORWARD_COPIES * b + j], recv_sem=recv_sems.at[FORWARD_COPIES * b + j],
            device_id=sibling, device_id_type=MESH)
            for b, (i, o) in enumerate(zip(in_refs, out_refs)) for j, s in enumerate(slots)]
        for cp in passed:
            cp.start()
        for cp in passed:
            cp.wait()
        token[...] = jnp.zeros_like(token)

    out = pl.pallas_call(
        body, name=name,
        out_shape=(*[jax.ShapeDtypeStruct(a.shape, a.dtype) for a in lands], jax.ShapeDtypeStruct((8, LANE), F32)),
        in_specs=[ANY] * n,
        out_specs=(*[ANY] * n, pl.BlockSpec(memory_space=pltpu.VMEM)),
        input_output_aliases={i: i for i in range(n)},
        scratch_shapes=[pltpu.SemaphoreType.DMA((FORWARD_COPIES * n,)), pltpu.SemaphoreType.DMA((FORWARD_COPIES * n,))],
    )(*lands)
    return list(out[:n]), out[n]


def _scatter_sum(parts, got, me, into, layer, *, name):
    shard = parts.shape[1:]
    cols = shard[-1]
    rows = int(np.prod(shard[:-1]))
    tr = _tile(rows, 256, 16)
    layers = into.shape[0]

    def body(me_ref, p_ref, g_ref, into_ref, o_ref):
        acc = p_ref[...].astype(F32)
        for k in range(SCATTER_PEERS):
            acc = acc + g_ref[k].astype(F32)
        o_ref[...] = acc

    slab = pl.BlockSpec((None, tr, cols), lambda r, me_ref: (layer, r, 0))
    out = pl.pallas_call(
        body, name=name,
        out_shape=jax.ShapeDtypeStruct((layers, rows, cols), F32),
        grid_spec=pltpu.PrefetchScalarGridSpec(
            num_scalar_prefetch=1, grid=(rows // tr,),
            in_specs=[pl.BlockSpec((None, tr, cols), lambda r, me_ref: (me_ref[0], r, 0)),
                      pl.BlockSpec((SCATTER_PEERS, tr, cols), lambda r, me_ref: (0, r, 0)),
                      ANY],
            out_specs=slab),
        input_output_aliases={3: 0},
        compiler_params=_params("parallel"),
    )(me, parts.reshape(N_DEV, rows, cols), got.reshape(SCATTER_PEERS, rows, cols), into.reshape(layers, rows, cols))
    return out.reshape(layers, *shard)


def _sum_devices(parts, *, name):
    _, R, C = parts.shape
    tr = _tile(R, 512, 8)

    def body(p_ref, o_ref):
        acc = p_ref[0]
        for d in range(1, N_DEV):
            acc = acc + p_ref[d]
        o_ref[...] = acc

    return pl.pallas_call(
        body, name=name,
        out_shape=jax.ShapeDtypeStruct((R, C), F32),
        grid=(R // tr,),
        in_specs=[pl.BlockSpec((N_DEV, tr, C), lambda r: (0, r, 0))],
        out_specs=pl.BlockSpec((tr, C), lambda r: (r, 0)),
        compiler_params=_params("parallel"),
    )(parts)


def _adamw(w, g, m, v, *, name, dep=None):
    shape = w.shape
    cols = shape[-1]
    rows = w.size // cols
    tr = _tile(rows, 256, 8)

    def body(w_ref, g_ref, m_ref, v_ref, *rest):
        d_ref, nm_ref, nv_ref = rest[-3:]
        g = g_ref[...]
        m = ADAM_B1 * m_ref[...] + (1.0 - ADAM_B1) * g
        v = ADAM_B2 * v_ref[...] + (1.0 - ADAM_B2) * jnp.square(g)
        m_hat = m / (1.0 - ADAM_B1 ** ADAM_STEP)
        v_hat = v / (1.0 - ADAM_B2 ** ADAM_STEP)
        d_ref[...] = -ADAM_LR * (m_hat / (jnp.sqrt(v_hat) + ADAM_EPS) + ADAM_WD * w_ref[...])
        nm_ref[...] = m
        nv_ref[...] = v

    spec = pl.BlockSpec((tr, cols), lambda i: (i, 0))
    out = jax.ShapeDtypeStruct((rows, cols), F32)
    d, nm, nv = pl.pallas_call(
        body, name=name,
        out_shape=(out, out, out),
        grid=(rows // tr,),
        in_specs=[spec] * 4 + _dep_spec(dep), out_specs=(spec,) * 3,
        compiler_params=_params("parallel"),
    )(*(a.reshape(rows, cols) for a in (w, g, m, v)), *([] if dep is None else [dep]))
    return d.reshape(shape), nm.reshape(shape), nv.reshape(shape)


PACK_ALIGN = 16 * LANE


def _pack(pieces, lead):
    out = []
    for p in pieces:
        keep = p.shape[:lead]
        flat = p.reshape(*keep, -1)
        pad = (-flat.shape[-1]) % PACK_ALIGN
        if pad:
            flat = jnp.pad(flat, [(0, 0)] * lead + [(0, pad)])
        out.append(flat.reshape(*keep, -1, LANE))
    return jnp.concatenate(out, axis=lead)


def _unpack(buf, shapes, lead):
    keep = buf.shape[:lead]
    out, row = [], 0
    for shape in shapes:
        size = int(np.prod(shape))
        rows = -(-size // PACK_ALIGN) * (PACK_ALIGN // LANE)
        piece = lax.slice_in_dim(buf, row, row + rows, axis=lead).reshape(*keep, rows * LANE)
        out.append(lax.slice_in_dim(piece, 0, size, axis=lead).reshape(*keep, *shape))
        row += rows
    return out


def kernel(x, positions, norm_ffn1, ffn1_up, ffn1_down, norm_mix, w_in, b_gate, pool_maps, pool_scale, w_pool_proj, q_latent_norm, w_uq, kv_latent_norm, w_ukv, w_attn_proj, w_out, norm_ffn2, ffn2_up, ffn2_down, final_norm, loss_target, m_norm_ffn1, m_ffn1_up, m_ffn1_down, m_norm_mix, m_w_in, m_b_gate, m_pool_maps, m_pool_scale, m_w_pool_proj, m_q_latent_norm, m_w_uq, m_kv_latent_norm, m_w_ukv, m_w_attn_proj, m_w_out, m_norm_ffn2, m_ffn2_up, m_ffn2_down, m_final_norm, v_norm_ffn1, v_ffn1_up, v_ffn1_down, v_norm_mix, v_w_in, v_b_gate, v_pool_maps, v_pool_scale, v_w_pool_proj, v_q_latent_norm, v_w_uq, v_kv_latent_norm, v_w_ukv, v_w_attn_proj, v_w_out, v_norm_ffn2, v_ffn2_up, v_ffn2_down, v_final_norm):
    order = ("norm_ffn1", "ffn1_up", "ffn1_down", "norm_mix", "w_in", "b_gate", "pool_maps", "pool_scale",
             "w_pool_proj", "q_latent_norm", "w_uq", "kv_latent_norm", "w_ukv", "w_attn_proj", "w_out",
             "norm_ffn2", "ffn2_up", "ffn2_down", "final_norm")
    w = dict(zip(order, (norm_ffn1, ffn1_up, ffn1_down, norm_mix, w_in, b_gate, pool_maps, pool_scale, w_pool_proj,
                         q_latent_norm, w_uq, kv_latent_norm, w_ukv, w_attn_proj, w_out, norm_ffn2, ffn2_up,
                         ffn2_down, final_norm)))
    m = dict(zip(order, (m_norm_ffn1, m_ffn1_up, m_ffn1_down, m_norm_mix, m_w_in, m_b_gate, m_pool_maps, m_pool_scale,
                         m_w_pool_proj, m_q_latent_norm, m_w_uq, m_kv_latent_norm, m_w_ukv, m_w_attn_proj, m_w_out,
                         m_norm_ffn2, m_ffn2_up, m_ffn2_down, m_final_norm)))
    v = dict(zip(order, (v_norm_ffn1, v_ffn1_up, v_ffn1_down, v_norm_mix, v_w_in, v_b_gate, v_pool_maps, v_pool_scale,
                         v_w_pool_proj, v_q_latent_norm, v_w_uq, v_kv_latent_norm, v_w_ukv, v_w_attn_proj, v_w_out,
                         v_norm_ffn2, v_ffn2_up, v_ffn2_down, v_final_norm)))
    L = norm_ffn1.shape[0]
    B, S, D = x.shape
    T = B * S

    def turned(a, n):
        return a.transpose(0, 2, 1) if n in TRANSPOSED else a

    wk, mk, vk = ({n: turned(d[n], n) for n in order} for d in (w, m, v))
    packed_shapes = [wk[n].shape[1:] for n in PACKED]
    my_slot = 4 * lax.axis_index("x") + 2 * lax.axis_index("y") + lax.axis_index("c")
    me = jnp.stack([my_slot]).astype(jnp.int32)

    def weight_blocks(l, names, token):
        zero = token[0, 0].astype(BF16)
        blocks = [wk[n][l].astype(BF16) + zero for n in names if n not in PACKED]
        if any(n in PACKED for n in names):
            blocks.append(_pack([wk[n][l].astype(BF16) + zero for n in PACKED], 0))
        return blocks

    def kernel_weights(names, lands):
        direct = [n for n in names if n not in PACKED]
        stacked = dict(zip(direct, lands))
        if len(lands) > len(direct):
            stacked.update(zip(PACKED, _unpack(lands[-1], packed_shapes, 1)))
        return _kernel_weights(stacked)

    def gather_start(l, names, token, tag):
        blocks = weight_blocks(l, names, token)
        lands = [lax.empty((N_DEV, *b.shape), b.dtype) for b in blocks]
        send_sems, recv_sems, blocks, lands, token = _exchange_start(
            blocks, lands, _gather_plan, GATHER_PEERS * len(blocks), name=f"gather_start_{tag}")
        return (send_sems, recv_sems, blocks, lands, tag), token

    def gather_wait(state, after):
        send_sems, recv_sems, blocks, lands, tag = state
        return _exchange_wait(send_sems, recv_sems, blocks, lands, _gather_plan, after, name=f"gather_wait_{tag}")[1]

    layer_part = FFN1_PART + MIXER_PART + FFN2_PART
    tabs = _rope_tables(positions.reshape(T))
    xs = x.reshape(T, D)
    h = _rms_fwd(xs, w["norm_ffn1"][0], name="first_norm")
    full, saved = [], []

    p = {n: w[n][0] for n in SMALL}
    blocks = weight_blocks(0, FFN1_PART, jnp.zeros((8, LANE), F32))
    lands = _exchange(blocks, [(N_DEV, *b.shape) for b in blocks], _gather_plan, GATHER_PEERS, name="gather_first")
    lands, token = _gather_forward(lands, name="gather_forward")
    w0 = kernel_weights(FFN1_PART, lands)
    state, token = gather_start(0, MIXER_PART, token, "0_mix")
    xs, h, s1 = _ffn_fwd(xs, h, w0["ffn1"], "ffn1", p["norm_mix"], dep=token)
    lands, token = _gather_forward(gather_wait(state, xs), name="gather_forward")
    w0.update(kernel_weights(MIXER_PART, lands))
    state, token = gather_start(0, FFN2_PART, token, "0_ffn2")
    if L > 1:
        next_state, token = gather_start(1, layer_part, token, "1")
    xs, h, s2 = _mixer_fwd(xs, h, p, w0, tabs, S, p["norm_ffn2"], dep=token)
    lands, token = _gather_forward(gather_wait(state, xs), name="gather_forward")
    w0.update(kernel_weights(FFN2_PART, lands))
    xs, h, s3 = _ffn_fwd(xs, h, w0["ffn2"], "ffn2", w["norm_ffn1"][1] if L > 1 else None, dep=token)
    if L > 1:
        lands, token = _gather_forward(gather_wait(next_state, xs), name="gather_forward")
    full.append(w0)
    saved.append((s1, s2, s3))

    for l in range(1, L):
        full.append(kernel_weights(layer_part, lands))
        more = l + 1 < L
        p = {n: w[n][l] for n in SMALL}
        if more:
            state, token = gather_start(l + 1, layer_part, token, f"{l + 1}")
        xs, h, s1 = _ffn_fwd(xs, h, full[l]["ffn1"], "ffn1", p["norm_mix"], dep=token if more else None)
        xs, h, s2 = _mixer_fwd(xs, h, p, full[l], tabs, S, p["norm_ffn2"])
        if more:
            lands = gather_wait(state, xs)
            send_sems, recv_sems, _, lands, token = _exchange_start(
                [], lands, _forward_plan, FORWARD_COPIES * len(lands), name=f"forward_start_{l + 1}")
        xs, h, s3 = _ffn_fwd(xs, h, full[l]["ffn2"], "ffn2", w["norm_ffn1"][l + 1] if more else None,
                             dep=token if more else None)
        if more:
            _, lands = _exchange_wait(send_sems, recv_sems, [], lands, _forward_plan, xs, name=f"forward_wait_{l + 1}")
        saved.append((s1, s2, s3))
    dx, dfinal, loss = _loss_head(xs, final_norm, loss_target.reshape(T, D), name="loss_head")

    big_grads = {n: [None] * L if n in PACKED else lax.empty((L, *wk[n].shape[1:]), F32) for n in BIG}
    small_grads_of = [None] * L
    pending = None

    def scatter_start(names, stacked, tag):
        srcs = [stacked[n] for n in names if n not in PACKED]
        if any(n in PACKED for n in names):
            srcs.append(_pack([stacked[n] for n in PACKED], 1))
        lands = [lax.empty((SCATTER_PEERS, *s.shape[1:]), s.dtype) for s in srcs]
        send_sems, recv_sems, srcs, lands, token = _exchange_start(
            srcs, lands, _scatter_plan, SCATTER_PEERS * len(srcs), name=f"scatter_start_{tag}")
        return (names, send_sems, recv_sems, srcs, lands, tag), token

    def scatter_finish(state, after, l):
        names, send_sems, recv_sems, srcs, lands, tag = state
        srcs, got = _exchange_wait(send_sems, recv_sems, srcs, lands, _scatter_plan, after, name=f"scatter_wait_{tag}")
        direct = [n for n in names if n not in PACKED]
        for n, s, g in zip(direct, srcs, got):
            big_grads[n] = _scatter_sum(s, g, me, big_grads[n], l, name="scatter_sum")
        if len(srcs) > len(direct):
            packed = _scatter_sum(srcs[-1], got[-1], me, lax.empty((1, *srcs[-1].shape[1:]), F32), 0,
                                  name="scatter_sum")[0]
            for n, g in zip(PACKED, _unpack(packed, packed_shapes, 0)):
                big_grads[n][l] = g

    dep = None
    for l in reversed(range(L)):
        p = {n: w[n][l] for n in SMALL}
        s1, s2, s3 = saved[l]
        small_g = {}
        dx, small_g["norm_ffn2"], dup_t, dwd = _ffn_bwd(dx, p["norm_ffn2"], full[l]["ffn2"], s3, "ffn2", dep=dep)
        if pending is not None:
            scatter_finish(pending[0], dx, pending[1])
        stacked = {"ffn2_up": _split_rows(dup_t), "ffn2_down": _split_rows(dwd)}
        state, dep = scatter_start(("ffn2_up", "ffn2_down"), stacked, f"ffn2_{l}")
        pending = (state, l)

        dx, gm = _mixer_bwd(dx, p, full[l], tabs, s2, S, dep=dep)
        scatter_finish(pending[0], dx, pending[1])
        names = ("w_in", "w_attn_proj", "w_out") + PACKED
        state, dep = scatter_start(names, _mixer_grads_stacked(gm), f"mix_{l}")
        pending = (state, l)
        small_g.update({n: gm[n] for n in SMALL if n in gm})

        if l > 0:
            dx, small_g["norm_ffn1"], dup_t, dwd = _ffn_bwd(dx, p["norm_ffn1"], full[l]["ffn1"], s1, "ffn1", dep=dep)
            scatter_finish(pending[0], dx, pending[1])
            stacked = {"ffn1_up": _split_rows(dup_t), "ffn1_down": _split_rows(dwd)}
            state, dep = scatter_start(("ffn1_up", "ffn1_down"), stacked, f"ffn1_{l}")
            pending = (state, l)
        else:
            early_states = []

            def send_down(dwd):
                state, token = scatter_start(("ffn1_down",), {"ffn1_down": _split_rows(dwd)}, "ffn1_down_0")
                early_states.append(state)
                return token

            dx, small_g["norm_ffn1"], dup_t, _ = _ffn_bwd(dx, p["norm_ffn1"], full[l]["ffn1"], s1, "ffn1", dep=dep,
                                                          early=send_down)
            last_mixer, last_down = pending, (early_states[0], 0)
            state, dep = scatter_start(("ffn1_up",), {"ffn1_up": _split_rows(dup_t)}, "ffn1_up_0")
            pending = (state, l)
        small_grads_of[l] = small_g
    grad_x = dx.reshape(B, S, D)

    small_parts = [small_grads_of[l][n] for l in range(L) for n in SMALL] + [dfinal, loss[0, :1]]
    small_shapes = [p.shape for p in small_parts]
    vec = _pack([jnp.concatenate([p.reshape(-1) for p in small_parts])], 0)
    small_send, small_recv, vec_thru, small_land, small_token = _exchange_start(
        [vec], [lax.empty((N_DEV, *vec.shape), F32)], _gather_all_plan, SCATTER_PEERS, name="small_start")

    gk, grad, delta, new_m, new_v = {}, {}, {}, {}, {}

    def update(n, dep=None):
        wn, gn, mn, vn = (a.reshape(1, -1) if a.ndim == 1 else a for a in (wk[n], gk[n], mk[n], vk[n]))
        d, nm, nv = _adamw(wn, gn, mn, vn, name="adamw_" + n, dep=dep)
        grad[n] = turned(gk[n], n)
        delta[n], new_m[n], new_v[n] = (turned(a.reshape(wk[n].shape), n) for a in (d, nm, nv))

    deps = [dep, small_token]
    for n in FFN2_PART:
        gk[n] = jnp.stack(big_grads[n]) if n in PACKED else big_grads[n]
        update(n, deps.pop(0))
    scatter_finish(last_mixer[0], new_v["ffn2_down"], last_mixer[1])
    for n in MIXER_PART:
        gk[n] = jnp.stack(big_grads[n]) if n in PACKED else big_grads[n]
        update(n)
    scatter_finish(last_down[0], new_v[MIXER_PART[-1]], last_down[1])
    scatter_finish(pending[0], new_v[MIXER_PART[-1]], pending[1])
    for n in FFN1_PART:
        gk[n] = jnp.stack(big_grads[n]) if n in PACKED else big_grads[n]
        update(n)

    vec_thru, small_land = _exchange_wait(small_send, small_recv, vec_thru, small_land, _gather_all_plan,
                                          new_v["ffn1_down"], name="small_wait")
    parts = lax.dynamic_update_index_in_dim(small_land[0], vec_thru[0], my_slot, 0)
    flat = _sum_devices(parts, name="sum_small").reshape(-1)
    small_grads, at = [], 0
    for shape in small_shapes:
        size = int(np.prod(shape))
        small_grads.append(lax.slice_in_dim(flat, at, at + size).reshape(shape))
        at += size
    loss_total = small_grads[-1].reshape(())
    for i, n in enumerate(SMALL):
        gk[n] = jnp.stack([small_grads[l * len(SMALL) + i] for l in range(L)]).reshape(w[n].shape)
        update(n)
    gk["final_norm"] = small_grads[-2].reshape(final_norm.shape)
    update("final_norm")
    return (loss_total, grad_x, *[grad[n] for n in order], *[delta[n] for n in order],
            *[new_m[n] for n in order], *[new_v[n] for n in order])
```

```python
import functools

import numpy as np
import jax
import jax.numpy as jnp
from jax import lax
from jax.experimental import pallas as pl
from jax.experimental.pallas import tpu as pltpu

F32 = jnp.float32
BF16 = jnp.bfloat16

NORM_EPS = 1e-6
ROPE_THETA = 10000.0
QK_NOPE = 128
QK_ROPE = 64
V_DIM = 128
HEAD_W = 256
POOL_WINDOWS = (2, 4, 8, 16)
POOL_G = 128
POOL_DIM = 512
LANE = 128
ATTN_SCALE = float((QK_NOPE + QK_ROPE) ** -0.5)
ATTN_SCALE_LOG2 = ATTN_SCALE * float(np.log2(np.e))
MASK_VALUE = -1e30
ATTN_TILE = 512

ADAM_LR = 0.001
ADAM_B1 = 0.9
ADAM_B2 = 0.999
ADAM_EPS = 1e-08
ADAM_WD = 0.01
ADAM_STEP = 10

N_DEV = 8
VMEM_LIMIT = 52 * 1024 * 1024

MESH = pl.DeviceIdType.MESH
ANY = pl.BlockSpec(memory_space=pl.ANY)


def _tile(dim, target, align=LANE):
    if dim <= target:
        return dim
    t = (target // align) * align
    while t >= align:
        if dim % t == 0:
            return t
        t -= align
    return dim


def _params(*sem):
    return pltpu.CompilerParams(dimension_semantics=sem, vmem_limit_bytes=VMEM_LIMIT)


def _rstd(x):
    return lax.rsqrt(jnp.mean(x * x, axis=-1, keepdims=True) + NORM_EPS)


def _mm(a, b, *, name, ta=False, tb=False, out_dtype=F32, alpha=1.0, tm=512, tn=1024, tk=1024, dep=None):
    if ta:
        K, M = a.shape
    else:
        M, K = a.shape
    if tb:
        N, K2 = b.shape
    else:
        K2, N = b.shape
    assert K == K2, (a.shape, b.shape, ta, tb)
    tm, tn, tk = _tile(M, tm), _tile(N, tn), _tile(K, tk)
    nk = K // tk
    dims = (((0 if ta else 1,), (1 if tb else 0,)), ((), ()))

    def body(a_ref, b_ref, *rest):
        o_ref = rest[0 if dep is None else 1]
        acc_ref = rest[-1] if nk > 1 else None
        part = lax.dot_general(a_ref[...].astype(BF16), b_ref[...].astype(BF16), dims,
                               preferred_element_type=F32)

        def finish(acc):
            o_ref[...] = (acc * alpha if alpha != 1.0 else acc).astype(out_dtype)

        if nk == 1:
            finish(part)
        else:
            k = pl.program_id(2)

            @pl.when(k == 0)
            def _():
                acc_ref[...] = part

            @pl.when(k > 0)
            def _():
                acc_ref[...] += part

            @pl.when(k == nk - 1)
            def _():
                finish(acc_ref[...])

    a_spec = pl.BlockSpec((tk, tm), lambda i, j, k: (k, i)) if ta else pl.BlockSpec((tm, tk), lambda i, j, k: (i, k))
    b_spec = pl.BlockSpec((tn, tk), lambda i, j, k: (j, k)) if tb else pl.BlockSpec((tk, tn), lambda i, j, k: (k, j))
    return pl.pallas_call(
        body, name=name,
        out_shape=jax.ShapeDtypeStruct((M, N), out_dtype),
        grid=(M // tm, N // tn, nk),
        in_specs=[a_spec, b_spec] + _dep_spec(dep),
        out_specs=pl.BlockSpec((tm, tn), lambda i, j, k: (i, j)),
        scratch_shapes=[pltpu.VMEM((tm, tn), F32)] if nk > 1 else [],
        compiler_params=_params("parallel", "parallel", "arbitrary"),
    )(a, b, *([] if dep is None else [dep]))


def _dw_multi(pairs, *, name, tk=512):
    n = len(pairs)
    T = pairs[0][0].shape[0]
    tk = _tile(T, tk, 16)
    nk = T // tk
    shapes = [(a.shape[1], b.shape[1]) for a, b in pairs]

    def body(*refs):
        ins, outs, accs = refs[:2 * n], refs[2 * n:3 * n], refs[3 * n:]
        k = pl.program_id(0)
        parts = [lax.dot_general(ins[2 * i][...].astype(BF16), ins[2 * i + 1][...].astype(BF16), _TN,
                                 preferred_element_type=F32) for i in range(n)]

        @pl.when(k == 0)
        def _():
            for acc, part in zip(accs, parts):
                acc[...] = part

        @pl.when(k > 0)
        def _():
            for acc, part in zip(accs, parts):
                acc[...] += part

        @pl.when(k == nk - 1)
        def _():
            for out, acc in zip(outs, accs):
                out[...] = acc[...].astype(BF16)

    return pl.pallas_call(
        body, name=name,
        out_shape=tuple(jax.ShapeDtypeStruct(s, BF16) for s in shapes),
        grid=(nk,),
        in_specs=[pl.BlockSpec((tk, x.shape[1]), lambda k: (k, 0)) for pair in pairs for x in pair],
        out_specs=tuple(pl.BlockSpec(s, lambda k: (0, 0)) for s in shapes),
        scratch_shapes=[pltpu.VMEM(s, F32) for s in shapes],
        compiler_params=_params("arbitrary"),
    )(*[x for pair in pairs for x in pair])


def _rms_fwd(x, g, *, name):
    T, D = x.shape
    tm = _tile(T, 512, 16)

    def body(x_ref, g_ref, h_ref):
        x = x_ref[...]
        h_ref[...] = (x * _rstd(x) * g_ref[...]).astype(BF16)

    return pl.pallas_call(
        body, name=name,
        out_shape=jax.ShapeDtypeStruct((T, D), BF16),
        grid=(T // tm,),
        in_specs=[pl.BlockSpec((tm, D), lambda i: (i, 0)), pl.BlockSpec((1, D), lambda i: (0, 0))],
        out_specs=pl.BlockSpec((tm, D), lambda i: (i, 0)),
        compiler_params=_params("parallel"),
    )(x, g.reshape(1, D))


def _loss_head(x, g, target, *, name):
    T, D = x.shape
    tm = _tile(T, 512, 16)

    def body(x_ref, g_ref, t_ref, dx_ref, dg_ref, loss_ref):
        x = x_ref[...]
        gain = g_ref[...]
        r = _rstd(x)
        xhat = x * r
        err = xhat * gain - t_ref[...]
        dy = err * (1.0 / D)
        dxh = dy * gain
        dx_ref[...] = r * (dxh - xhat * jnp.mean(dxh * xhat, axis=-1, keepdims=True))
        dg_part = jnp.sum(dy * xhat, axis=0, keepdims=True)
        loss_part = jnp.full((1, LANE), 0.5 / D, F32) * jnp.sum(err * err)

        @pl.when(pl.program_id(0) == 0)
        def _():
            dg_ref[...] = dg_part
            loss_ref[...] = loss_part

        @pl.when(pl.program_id(0) > 0)
        def _():
            dg_ref[...] += dg_part
            loss_ref[...] += loss_part

    row = pl.BlockSpec((tm, D), lambda i: (i, 0))
    vec = pl.BlockSpec((1, D), lambda i: (0, 0))
    return pl.pallas_call(
        body, name=name,
        out_shape=(jax.ShapeDtypeStruct((T, D), F32), jax.ShapeDtypeStruct((1, D), F32),
                   jax.ShapeDtypeStruct((1, LANE), F32)),
        grid=(T // tm,),
        in_specs=[row, vec, row],
        out_specs=(row, vec, pl.BlockSpec((1, LANE), lambda i: (0, 0))),
        compiler_params=_params("arbitrary"),
    )(x, g.reshape(1, D), target)


def _ffn_fwd_core(x, h, w_up_t, wd, next_gain, *, alpha, name, dep=None):
    T, D = x.shape
    F = wd.shape[0]
    tm = _tile(T, 256, 16)
    has_norm = next_gain is not None

    def body(x_ref, h_ref, wg_ref, wu_ref, wd_ref, *rest):
        outs = rest[len(rest) - (5 if has_norm else 4):]
        gate_ref, up_ref, a_ref, xn_ref = outs[:4]
        h = h_ref[...]
        gate = lax.dot_general(h, wg_ref[...], _NT, preferred_element_type=F32)
        up = lax.dot_general(h, wu_ref[...], _NT, preferred_element_type=F32)
        a = (gate * jax.nn.sigmoid(gate) * up).astype(BF16)
        gate_ref[...] = gate.astype(BF16)
        up_ref[...] = up.astype(BF16)
        a_ref[...] = a
        xn = x_ref[...] + alpha * jnp.dot(a, wd_ref[...], preferred_element_type=F32)
        xn_ref[...] = xn
        if has_norm:
            outs[4][...] = (xn * _rstd(xn) * rest[0][...]).astype(BF16)

    once = pl.Buffered(1)
    row_d = pl.BlockSpec((tm, D), lambda i: (i, 0))
    row_f = pl.BlockSpec((tm, F), lambda i: (i, 0))
    vec = pl.BlockSpec((1, D), lambda i: (0, 0))
    act = jax.ShapeDtypeStruct((T, F), BF16)
    operands = [x, h, w_up_t, w_up_t, wd] + ([next_gain.reshape(1, D)] if has_norm else [])
    out = pl.pallas_call(
        body, name=name,
        out_shape=(act, act, act, jax.ShapeDtypeStruct((T, D), F32)) + ((jax.ShapeDtypeStruct((T, D), BF16),) if has_norm else ()),
        grid=(T // tm,),
        in_specs=[row_d, row_d,
                  pl.BlockSpec((F, D), lambda i: (0, 0), pipeline_mode=once),
                  pl.BlockSpec((F, D), lambda i: (1, 0), pipeline_mode=once),
                  pl.BlockSpec((F, D), lambda i: (0, 0), pipeline_mode=once)] + ([vec] if has_norm else []) + _dep_spec(dep),
        out_specs=(row_f, row_f, row_f, row_d) + ((row_d,) if has_norm else ()),
        compiler_params=_params("parallel"),
    )(*operands, *([] if dep is None else [dep]))
    return out if has_norm else (*out, None)


def _ffn_bwd_core(dxo, wd, w_up_t, gate, up, x, gain, *, alpha, name, dep=None):
    T, D = dxo.shape
    F = wd.shape[0]
    tm = _tile(T, 256, 16)

    def body(dxo_ref, wd_ref, wg_ref, wu_ref, gate_ref, up_ref, x_ref, g_ref, *rest):
        dgate_ref, dup_ref, dx_ref, dg_ref = rest[-4:]
        dxo = dxo_ref[...]
        da = lax.dot_general(dxo.astype(BF16), wd_ref[...], _NT, preferred_element_type=F32) * alpha
        gate = gate_ref[...].astype(F32)
        up = up_ref[...].astype(F32)
        sig = jax.nn.sigmoid(gate)
        dgate = (da * up * (sig * (1.0 + gate * (1.0 - sig)))).astype(BF16)
        dup = (da * (gate * sig)).astype(BF16)
        dgate_ref[...] = dgate
        dup_ref[...] = dup
        dh = (jnp.dot(dgate, wg_ref[...], preferred_element_type=F32)
              + jnp.dot(dup, wu_ref[...], preferred_element_type=F32))
        x = x_ref[...]
        r = _rstd(x)
        xhat = x * r
        dxh = dh * g_ref[...]
        dx_ref[...] = dxo + r * (dxh - xhat * jnp.mean(dxh * xhat, axis=-1, keepdims=True))
        part = jnp.sum(dh * xhat, axis=0, keepdims=True)

        @pl.when(pl.program_id(0) == 0)
        def _():
            dg_ref[...] = part

        @pl.when(pl.program_id(0) > 0)
        def _():
            dg_ref[...] += part

    once = pl.Buffered(1)
    row_d = pl.BlockSpec((tm, D), lambda i: (i, 0))
    row_f = pl.BlockSpec((tm, F), lambda i: (i, 0))
    vec = pl.BlockSpec((1, D), lambda i: (0, 0))
    act = jax.ShapeDtypeStruct((T, F), BF16)
    return pl.pallas_call(
        body, name=name,
        out_shape=(act, act, jax.ShapeDtypeStruct((T, D), F32), jax.ShapeDtypeStruct((1, D), F32)),
        grid=(T // tm,),
        in_specs=[row_d,
                  pl.BlockSpec((F, D), lambda i: (0, 0), pipeline_mode=once),
                  pl.BlockSpec((F, D), lambda i: (0, 0), pipeline_mode=once),
                  pl.BlockSpec((F, D), lambda i: (1, 0), pipeline_mode=once),
                  row_f, row_f, row_d, vec] + _dep_spec(dep),
        out_specs=(row_f, row_f, row_d, vec),
        compiler_params=_params("arbitrary"),
    )(dxo, wd, w_up_t, w_up_t, gate, up, x, gain.reshape(1, D), *([] if dep is None else [dep]))


def _ffn_dw_up(dgate, dup, h, *, name, dep=None):
    T, F = dgate.shape
    D = h.shape[1]
    tm, tk = _tile(F, 1408), _tile(T, 1024, 16)
    nf, nk = F // tm, T // tk

    def body(dgate_ref, dup_ref, h_ref, *rest):
        o_ref, acc_ref = rest[-2:]
        i, k = pl.program_id(0), pl.program_id(1)

        def accumulate(part):
            @pl.when(k == 0)
            def _():
                acc_ref[...] = part

            @pl.when(k > 0)
            def _():
                acc_ref[...] += part

        @pl.when(i < nf)
        def _():
            accumulate(lax.dot_general(dgate_ref[...], h_ref[...], _TN, preferred_element_type=F32))

        @pl.when(i >= nf)
        def _():
            accumulate(lax.dot_general(dup_ref[...], h_ref[...], _TN, preferred_element_type=F32))

        @pl.when(k == nk - 1)
        def _():
            o_ref[...] = acc_ref[...].astype(BF16)

    return pl.pallas_call(
        body, name=name,
        out_shape=jax.ShapeDtypeStruct((2 * F, D), BF16),
        grid=(2 * nf, nk),
        in_specs=[pl.BlockSpec((tk, tm), lambda i, k: (jnp.where(i < nf, k, nk - 1), jnp.minimum(i, nf - 1))),
                  pl.BlockSpec((tk, tm), lambda i, k: (jnp.where(i < nf, 0, k), jnp.maximum(i - nf, 0))),
                  pl.BlockSpec((tk, D), lambda i, k: (k, 0))] + _dep_spec(dep),
        out_specs=pl.BlockSpec((tm, D), lambda i, k: (i, 0)),
        scratch_shapes=[pltpu.VMEM((tm, D), F32)],
        compiler_params=_params("parallel", "arbitrary"),
    )(dgate, dup, h, *([] if dep is None else [dep]))


def _dep_spec(dep):
    return [] if dep is None else [pl.BlockSpec(dep.shape, lambda *_: (0,) * dep.ndim)]


def _rope_tables(positions):
    half = QK_ROPE // 2
    inv_freq = ROPE_THETA ** (-jnp.arange(0, QK_ROPE, 2, dtype=F32) / QK_ROPE)
    ang = positions.astype(F32)[:, None] * inv_freq
    cos, sin = jnp.cos(ang), jnp.sin(ang)
    z = jnp.zeros_like(cos)
    zz = jnp.zeros((positions.shape[0], LANE - QK_ROPE), F32)
    c = jnp.concatenate([cos, cos, zz], axis=1)
    sa = jnp.concatenate([z, sin, zz], axis=1)
    sb = jnp.concatenate([-sin, z, zz], axis=1)
    return c, sa, sb


def _rotate(seg, c, sa, sb, sign):
    half = QK_ROPE // 2
    mix = pltpu.roll(seg, half, 1) * sa + pltpu.roll(seg, LANE - half, 1) * sb
    return seg * c + mix if sign > 0 else seg * c - mix


def _mixer_in(h, wa, wuq, wukv, gq, gkv, tabs, *, name, dep=None):
    T, D = h.shape
    HQ, QL = wuq.shape
    KVL = wukv.shape[0]
    H = HQ // HEAD_W
    o_q, o_kv, o_kr = POOL_DIM, POOL_DIM + QL, POOL_DIM + QL + KVL
    PA = o_kr + LANE
    assert wa.shape[0] >= PA
    tm = _tile(T, 512, 16)

    def body(h_ref, wa_ref, wuq_ref, wukv_ref, gq_ref, gkv_ref, c_ref, sa_ref, sb_ref, *rest):
        xp_ref, ql_ref, kvl_ref, qn_ref, kvn_ref, q_ref, kv_ref, kr_ref = rest[-8:]
        proj = lax.dot_general(h_ref[...], wa_ref[...], _NT, preferred_element_type=F32)
        xp_ref[...] = proj[:, :POOL_DIM]
        ql = proj[:, o_q:o_kv]
        kvl = proj[:, o_kv:o_kr]
        ql_ref[...] = ql
        kvl_ref[...] = kvl
        qn = (ql * _rstd(ql) * gq_ref[...]).astype(BF16)
        kvn = (kvl * _rstd(kvl) * gkv_ref[...]).astype(BF16)
        qn_ref[...] = qn
        kvn_ref[...] = kvn
        c, sa, sb = c_ref[...], sa_ref[...], sb_ref[...]
        q = lax.dot_general(qn, wuq_ref[...], _NT, preferred_element_type=F32)
        for hh in range(H):
            base = hh * HEAD_W
            q_ref[:, base:base + QK_NOPE] = q[:, base:base + QK_NOPE].astype(BF16)
            q_ref[:, base + QK_NOPE:base + HEAD_W] = _rotate(
                q[:, base + QK_NOPE:base + HEAD_W], c, sa, sb, 1).astype(BF16)
        kv_ref[...] = jnp.dot(kvn, wukv_ref[...], preferred_element_type=F32).astype(BF16)
        kr_ref[...] = _rotate(proj[:, o_kr:o_kr + LANE], c, sa, sb, 1).astype(BF16)

    def row(w):
        return pl.BlockSpec((tm, w), lambda i: (i, 0))

    def whole(arr):
        return pl.BlockSpec(arr.shape, lambda i: (0,) * arr.ndim)

    gq2, gkv2 = gq.reshape(1, QL), gkv.reshape(1, KVL)
    outs = [(POOL_DIM, F32), (QL, F32), (KVL, F32), (QL, BF16), (KVL, BF16), (HQ, BF16), (HQ, BF16), (LANE, BF16)]
    return pl.pallas_call(
        body, name=name,
        out_shape=tuple(jax.ShapeDtypeStruct((T, w), dt) for w, dt in outs),
        grid=(T // tm,),
        in_specs=[row(D), pl.BlockSpec((PA, D), lambda i: (0, 0)), whole(wuq), whole(wukv), whole(gq2), whole(gkv2),
                  row(LANE), row(LANE), row(LANE)] + _dep_spec(dep),
        out_specs=tuple(row(w) for w, _ in outs),
        compiler_params=_params("parallel"),
    )(h, wa, wuq, wukv, gq2, gkv2, *tabs, *([] if dep is None else [dep]))


def _mixer_in_bwd(dq, dkv, dkr, ql, kvl, dxp, dgl, x, dxo, win_t, wuq, wukv, g_mix, gq, gkv, tabs, *, name):
    T, HQ = dq.shape
    D = x.shape[1]
    QL, KVL = wuq.shape[1], wukv.shape[0]
    H = HQ // HEAD_W
    PA = POOL_DIM + QL + KVL + LANE
    o_q, o_kv, o_kr = POOL_DIM, POOL_DIM + QL, POOL_DIM + QL + KVL
    tm = _tile(T, 256, 16)

    def norm_bwd(lat, gain, dn):
        r = _rstd(lat)
        xhat = lat * r
        dxh = dn * gain
        dlat = r * (dxh - xhat * jnp.mean(dxh * xhat, axis=-1, keepdims=True))
        return dlat, jnp.sum(dn * xhat, axis=0, keepdims=True)

    def body(dq_ref, dkv_ref, dkr_ref, ql_ref, kvl_ref, dxp_ref, dgl_ref, x_ref, dxo_ref, win_ref,
             wuq_ref, wukv_ref, gmix_ref, gq_ref, gkv_ref, c_ref, sa_ref, sb_ref,
             dproj_ref, dqp_ref, dgq_ref, dgkv_ref, dx_ref, dgmix_ref):
        c, sa, sb = c_ref[...], sa_ref[...], sb_ref[...]
        for hh in range(H):
            base = hh * HEAD_W
            dqp_ref[:, base:base + QK_NOPE] = dq_ref[:, base:base + QK_NOPE]
            dqp_ref[:, base + QK_NOPE:base + HEAD_W] = _rotate(
                dq_ref[:, base + QK_NOPE:base + HEAD_W].astype(F32), c, sa, sb, -1).astype(BF16)
        dqn = jnp.dot(dqp_ref[...], wuq_ref[...], preferred_element_type=F32)
        dkvn = lax.dot_general(dkv_ref[...], wukv_ref[...], _NT, preferred_element_type=F32)
        dql, dgq = norm_bwd(ql_ref[...], gq_ref[...], dqn)
        dkvl, dgkv = norm_bwd(kvl_ref[...], gkv_ref[...], dkvn)
        dproj_ref[:, :POOL_DIM] = dxp_ref[...].astype(BF16)
        dproj_ref[:, o_q:o_kv] = dql.astype(BF16)
        dproj_ref[:, o_kv:o_kr] = dkvl.astype(BF16)
        dproj_ref[:, o_kr:PA] = _rotate(dkr_ref[...], c, sa, sb, -1).astype(BF16)

        n_gate = win_ref.shape[0] - 2 * D
        dh = (jnp.dot(dproj_ref[...], win_ref[:PA, :], preferred_element_type=F32)
              + jnp.dot(dgl_ref[...], win_ref[n_gate:, :], preferred_element_type=F32))
        x = x_ref[...]
        r = _rstd(x)
        xhat = x * r
        dxh = dh * gmix_ref[...]
        dx_ref[...] = dxo_ref[...] + r * (dxh - xhat * jnp.mean(dxh * xhat, axis=-1, keepdims=True))
        dgmix = jnp.sum(dh * xhat, axis=0, keepdims=True)

        @pl.when(pl.program_id(0) == 0)
        def _():
            dgq_ref[...] = dgq
            dgkv_ref[...] = dgkv
            dgmix_ref[...] = dgmix

        @pl.when(pl.program_id(0) > 0)
        def _():
            dgq_ref[...] += dgq
            dgkv_ref[...] += dgkv
            dgmix_ref[...] += dgmix

    def row(w):
        return pl.BlockSpec((tm, w), lambda i: (i, 0))

    def resident(arr):
        return pl.BlockSpec(arr.shape, lambda i: (0, 0), pipeline_mode=pl.Buffered(1))

    gmix2, gq2, gkv2 = g_mix.reshape(1, D), gq.reshape(1, QL), gkv.reshape(1, KVL)
    vec = pl.BlockSpec((1, D), lambda i: (0, 0))
    vq, vkv = pl.BlockSpec((1, QL), lambda i: (0, 0)), pl.BlockSpec((1, KVL), lambda i: (0, 0))
    return pl.pallas_call(
        body, name=name,
        out_shape=(jax.ShapeDtypeStruct((T, PA), BF16), jax.ShapeDtypeStruct((T, HQ), BF16),
                   jax.ShapeDtypeStruct((1, QL), F32), jax.ShapeDtypeStruct((1, KVL), F32),
                   jax.ShapeDtypeStruct((T, D), F32), jax.ShapeDtypeStruct((1, D), F32)),
        grid=(T // tm,),
        in_specs=[row(HQ), row(HQ), row(LANE), row(QL), row(KVL), row(POOL_DIM), row(2 * D), row(D), row(D),
                  resident(win_t), resident(wuq), resident(wukv),
                  vec, vq, vkv, row(LANE), row(LANE), row(LANE)],
        out_specs=(row(PA), row(HQ), vq, vkv, row(D), vec),
        compiler_params=_params("arbitrary"),
    )(dq, dkv, dkr, ql, kvl, dxp, dgl, x, dxo, win_t, wuq, wukv, gmix2, gq2, gkv2, *tabs)


def _pool_groups(x_of, S):
    row = lax.broadcasted_iota(jnp.int32, (S, POOL_G), 0)
    for g, w in enumerate(POOL_WINDOWS):
        x = x_of(g)
        s = x
        d = 1
        while d < w:
            s = s + jnp.where(row >= d, pltpu.roll(s, d, 0), 0.0)
            d *= 2
        cnt = jnp.minimum(row + 1, w).astype(F32)
        yield g, w, x, s / cnt - x, cnt, row


def _pool_fwd(xp, maps, scale, *, S, name):
    T = xp.shape[0]

    def body(xp_ref, maps_ref, scale_ref, ms_ref):
        for g, _, _, pooled, _, _ in _pool_groups(lambda g: xp_ref[:, g * POOL_G:(g + 1) * POOL_G], S):
            mixed = jnp.dot(pooled.astype(BF16), maps_ref[g].astype(BF16), preferred_element_type=F32)
            ms_ref[:, g * POOL_G:(g + 1) * POOL_G] = (mixed * scale_ref[:, g * POOL_G:(g + 1) * POOL_G]).astype(BF16)

    return pl.pallas_call(
        body, name=name,
        out_shape=jax.ShapeDtypeStruct((T, POOL_DIM), BF16),
        grid=(T // S,),
        in_specs=[pl.BlockSpec((S, POOL_DIM), lambda b: (b, 0)),
                  pl.BlockSpec(maps.shape, lambda b: (0, 0, 0)),
                  pl.BlockSpec((1, POOL_DIM), lambda b: (0, 0))],
        out_specs=pl.BlockSpec((S, POOL_DIM), lambda b: (b, 0)),
        compiler_params=_params("parallel"),
    )(xp, maps, scale.reshape(1, POOL_DIM))


def _pool_bwd(xp, dms, maps, scale, *, S, name):
    T = xp.shape[0]

    def body(xp_ref, dms_ref, maps_ref, scale_ref, dxp_ref, dmaps_ref, dscale_ref):
        first = pl.program_id(0) == 0
        for g, w, _, pooled, cnt, row in _pool_groups(lambda g: xp_ref[:, g * POOL_G:(g + 1) * POOL_G], S):
            cols = slice(g * POOL_G, (g + 1) * POOL_G)
            pooled_b = pooled.astype(BF16)
            maps_b = maps_ref[g].astype(BF16)
            mixed = jnp.dot(pooled_b, maps_b, preferred_element_type=F32)
            dms = dms_ref[:, cols]
            dscale = jnp.sum(dms * mixed, axis=0, keepdims=True)
            dmixed = (dms * scale_ref[:, cols]).astype(BF16)
            dmaps = lax.dot_general(pooled_b, dmixed, (((0,), (0,)), ((), ())), preferred_element_type=F32)
            dpooled = lax.dot_general(dmixed, maps_b, (((1,), (1,)), ((), ())), preferred_element_type=F32)
            z = dpooled / cnt
            d = 1
            while d < w:
                z = z + jnp.where(row < S - d, pltpu.roll(z, S - d, 0), 0.0)
                d *= 2
            dxp_ref[:, cols] = z - dpooled

            @pl.when(first)
            def _():
                dmaps_ref[g] = dmaps
                dscale_ref[:, cols] = dscale

            @pl.when(jnp.logical_not(first))
            def _():
                dmaps_ref[g] += dmaps
                dscale_ref[:, cols] += dscale

    seq = pl.BlockSpec((S, POOL_DIM), lambda b: (b, 0))
    maps_spec = pl.BlockSpec(maps.shape, lambda b: (0, 0, 0))
    vec = pl.BlockSpec((1, POOL_DIM), lambda b: (0, 0))
    return pl.pallas_call(
        body, name=name,
        out_shape=(jax.ShapeDtypeStruct((T, POOL_DIM), F32), jax.ShapeDtypeStruct(maps.shape, F32),
                   jax.ShapeDtypeStruct((1, POOL_DIM), F32)),
        grid=(T // S,),
        in_specs=[seq, seq, maps_spec, vec],
        out_specs=(seq, maps_spec, vec),
        compiler_params=_params("arbitrary"),
    )(xp, dms, maps, scale.reshape(1, POOL_DIM))


def _causal_mask(s, t):
    r = lax.broadcasted_iota(jnp.int32, (t, t), 0)
    c = lax.broadcasted_iota(jnp.int32, (t, t), 1)
    return jnp.where(r >= c, s, MASK_VALUE)


_NT = (((1,), (1,)), ((), ()))
_TN = (((0,), (0,)), ((), ()))


def _attn_fwd(q, kv, kr, *, S, name):
    T, HQ = q.shape
    H = HQ // HEAD_W
    B = T // S
    t = _tile(S, ATTN_TILE)
    n = S // t

    def body(q_ref, k_ref, v_ref, kr_ref, o_ref, lse_ref, kcat, vcat):
        kcat[:, :QK_NOPE] = k_ref[...]
        kcat[:, QK_NOPE:] = kr_ref[...]
        vcat[:, :V_DIM] = v_ref[...]
        vcat[:, V_DIM:] = jnp.ones((S, HEAD_W - V_DIM), BF16)
        for i in range(n):
            rows = slice(i * t, (i + 1) * t)
            qt = q_ref[rows, :]
            m = jnp.full((t, 1), MASK_VALUE, F32)
            acc = jnp.zeros((t, HEAD_W), F32)
            for j in range(i + 1):
                cols = slice(j * t, (j + 1) * t)
                s = lax.dot_general(qt, kcat[cols, :], _NT, preferred_element_type=F32) * ATTN_SCALE_LOG2
                if j == i:
                    s = _causal_mask(s, t)
                m_new = jnp.maximum(m, jnp.max(s, axis=1, keepdims=True))
                p = jnp.exp2(s - m_new)
                acc = jnp.exp2(m - m_new) * acc + jnp.dot(p.astype(BF16), vcat[cols, :], preferred_element_type=F32)
                m = m_new
            l = acc[:, V_DIM:V_DIM + 1]
            o_ref[rows, :] = (acc[:, :V_DIM] / l).astype(BF16)
            lse_ref[rows, :] = jnp.broadcast_to(m + jnp.log2(l), (t, LANE))

    seq_h = pl.BlockSpec((S, LANE), lambda b, h: (b, h))
    return pl.pallas_call(
        body, name=name,
        out_shape=(jax.ShapeDtypeStruct((T, H * V_DIM), BF16), jax.ShapeDtypeStruct((T, H * LANE), F32)),
        grid=(B, H),
        in_specs=[pl.BlockSpec((S, HEAD_W), lambda b, h: (b, h)),
                  pl.BlockSpec((S, QK_NOPE), lambda b, h: (b, 2 * h)),
                  pl.BlockSpec((S, V_DIM), lambda b, h: (b, 2 * h + 1)),
                  pl.BlockSpec((S, LANE), lambda b, h: (b, 0))],
        out_specs=(seq_h, seq_h),
        scratch_shapes=[pltpu.VMEM((S, HEAD_W), BF16), pltpu.VMEM((S, HEAD_W), BF16)],
        compiler_params=_params("parallel", "parallel"),
    )(q, kv, kv, kr)


def _attn_bwd(q, kv, kr, o, do, lse, *, S, name):
    T, HQ = q.shape
    H = HQ // HEAD_W
    B = T // S
    t = _tile(S, ATTN_TILE)
    n = S // t

    def body(q_ref, k_ref, v_ref, kr_ref, o_ref, do_ref, lse_ref, dq_ref, dkv_ref, dkr_ref, kcat, dq_acc):
        @pl.when(pl.program_id(1) == 0)
        def _():
            dkr_ref[...] = jnp.zeros_like(dkr_ref)

        kcat[:, :QK_NOPE] = k_ref[...]
        kcat[:, QK_NOPE:] = kr_ref[...]
        delta = [jnp.sum(do_ref[i * t:(i + 1) * t, :].astype(F32) * o_ref[i * t:(i + 1) * t, :].astype(F32),
                         axis=1, keepdims=True) for i in range(n)]
        for j in range(n):
            cols = slice(j * t, (j + 1) * t)
            kc = kcat[cols, :]
            vt = v_ref[cols, :]
            dk = jnp.zeros((t, HEAD_W), F32)
            dv = jnp.zeros((t, V_DIM), F32)
            for i in range(j, n):
                rows = slice(i * t, (i + 1) * t)
                qt = q_ref[rows, :]
                dot_ = do_ref[rows, :]
                s = lax.dot_general(qt, kc, _NT, preferred_element_type=F32) * ATTN_SCALE_LOG2
                if i == j:
                    s = _causal_mask(s, t)
                p = jnp.exp2(s - lse_ref[rows, :][:, :1])
                dv = dv + lax.dot_general(p.astype(BF16), dot_, _TN, preferred_element_type=F32)
                dp = lax.dot_general(dot_, vt, _NT, preferred_element_type=F32)
                ds = (p * (dp - delta[i]) * ATTN_SCALE).astype(BF16)
                dk = dk + lax.dot_general(ds, qt, _TN, preferred_element_type=F32)
                dq_part = jnp.dot(ds, kc, preferred_element_type=F32)
                if j == 0:
                    dq_acc[rows, :] = dq_part
                else:
                    dq_acc[rows, :] += dq_part
            dkv_ref[cols, :QK_NOPE] = dk[:, :QK_NOPE].astype(BF16)
            dkv_ref[cols, QK_NOPE:] = dv.astype(BF16)
            dkr_ref[cols, :] += dk[:, QK_NOPE:]
        dq_ref[...] = dq_acc[...].astype(BF16)

    seq_q = pl.BlockSpec((S, HEAD_W), lambda b, h: (b, h))
    seq_h = pl.BlockSpec((S, LANE), lambda b, h: (b, h))
    seq_shared = pl.BlockSpec((S, LANE), lambda b, h: (b, 0))
    return pl.pallas_call(
        body, name=name,
        out_shape=(jax.ShapeDtypeStruct((T, HQ), BF16), jax.ShapeDtypeStruct((T, HQ), BF16),
                   jax.ShapeDtypeStruct((T, LANE), F32)),
        grid=(B, H),
        in_specs=[seq_q,
                  pl.BlockSpec((S, QK_NOPE), lambda b, h: (b, 2 * h)),
                  pl.BlockSpec((S, V_DIM), lambda b, h: (b, 2 * h + 1)),
                  seq_shared, seq_h, seq_h, seq_h],
        out_specs=(seq_q, seq_q, seq_shared),
        scratch_shapes=[pltpu.VMEM((S, HEAD_W), BF16), pltpu.VMEM((S, HEAD_W), F32)],
        compiler_params=_params("parallel", "arbitrary"),
    )(q, kv, kv, kr, o, do, lse)


def _merge_out(h, ms, o, x, win_t, bgate, wpp, wap, wout, next_gain, *, name):
    T, D = x.shape
    tm = _tile(T, 256, 16)
    n_gate = win_t.shape[0] - 2 * D

    def body(h_ref, ms_ref, o_ref, x_ref, win_ref, bgate_ref, wpp_ref, wap_ref, wout_ref, ng_ref,
             gates_ref, ba_ref, bb_ref, merged_ref, xn_ref, hn_ref):
        logits = lax.dot_general(h_ref[...], win_ref[n_gate:, :], _NT, preferred_element_type=F32) + bgate_ref[...]
        gates = jax.nn.sigmoid(logits)
        ba = jnp.dot(ms_ref[...], wpp_ref[...], preferred_element_type=F32)
        bb = jnp.dot(o_ref[...], wap_ref[...], preferred_element_type=F32)
        merged = (gates[:, :D] * ba + gates[:, D:] * bb).astype(BF16)
        gates_ref[...] = gates.astype(BF16)
        ba_ref[...] = ba.astype(BF16)
        bb_ref[...] = bb.astype(BF16)
        merged_ref[...] = merged
        xn = x_ref[...] + jnp.dot(merged, wout_ref[...], preferred_element_type=F32)
        xn_ref[...] = xn
        hn_ref[...] = (xn * _rstd(xn) * ng_ref[...]).astype(BF16)

    def row(w):
        return pl.BlockSpec((tm, w), lambda i: (i, 0))

    def whole(arr):
        return pl.BlockSpec(arr.shape, lambda i: (0,) * arr.ndim, pipeline_mode=pl.Buffered(1))

    bg2, ng2 = bgate.reshape(1, 2 * D), next_gain.reshape(1, D)
    act = jax.ShapeDtypeStruct((T, D), BF16)
    return pl.pallas_call(
        body, name=name,
        out_shape=(jax.ShapeDtypeStruct((T, 2 * D), BF16), act, act, act, jax.ShapeDtypeStruct((T, D), F32), act),
        grid=(T // tm,),
        in_specs=[row(D), row(ms.shape[1]), row(o.shape[1]), row(D), whole(win_t), whole(bg2), whole(wpp),
                  whole(wap), whole(wout), whole(ng2)],
        out_specs=(row(2 * D), row(D), row(D), row(D), row(D), row(D)),
        compiler_params=_params("parallel"),
    )(h, ms, o, x, win_t, bg2, wpp, wap, wout, ng2)


def _merge_bwd(dxo, wout, wpp, wap, gates, ba, bb, *, name, dep=None):
    T, D = dxo.shape
    tm = _tile(T, 512, 16)

    def body(dxo_ref, wout_ref, wpp_ref, wap_ref, gates_ref, ba_ref, bb_ref, *rest):
        dba_ref, dbb_ref, dgl_ref, dbg_ref, dms_ref, do_ref = rest[-6:]
        dm = lax.dot_general(dxo_ref[...].astype(BF16), wout_ref[...], _NT, preferred_element_type=F32)
        ga = gates_ref[:, :D].astype(F32)
        gb = gates_ref[:, D:].astype(F32)
        dba = (dm * ga).astype(BF16)
        dbb = (dm * gb).astype(BF16)
        dba_ref[...] = dba
        dbb_ref[...] = dbb
        dms_ref[...] = lax.dot_general(dba, wpp_ref[...], _NT, preferred_element_type=F32)
        do_ref[...] = lax.dot_general(dbb, wap_ref[...], _NT, preferred_element_type=F32).astype(BF16)
        dgl_a = dm * ba_ref[...].astype(F32) * (ga * (1.0 - ga))
        dgl_b = dm * bb_ref[...].astype(F32) * (gb * (1.0 - gb))
        dgl_ref[:, :D] = dgl_a.astype(BF16)
        dgl_ref[:, D:] = dgl_b.astype(BF16)
        sa = jnp.sum(dgl_a, axis=0, keepdims=True)
        sb = jnp.sum(dgl_b, axis=0, keepdims=True)

        @pl.when(pl.program_id(0) == 0)
        def _():
            dbg_ref[:, :D] = sa
            dbg_ref[:, D:] = sb

        @pl.when(pl.program_id(0) > 0)
        def _():
            dbg_ref[:, :D] += sa
            dbg_ref[:, D:] += sb

    def row(w):
        return pl.BlockSpec((tm, w), lambda i: (i, 0))

    def whole(arr):
        return pl.BlockSpec(arr.shape, lambda i: (0, 0))

    P, HV = wpp.shape[0], wap.shape[0]
    act = jax.ShapeDtypeStruct((T, D), BF16)
    return pl.pallas_call(
        body, name=name,
        out_shape=(act, act, jax.ShapeDtypeStruct((T, 2 * D), BF16), jax.ShapeDtypeStruct((1, 2 * D), F32),
                   jax.ShapeDtypeStruct((T, P), F32), jax.ShapeDtypeStruct((T, HV), BF16)),
        grid=(T // tm,),
        in_specs=[row(D), whole(wout), whole(wpp), whole(wap), row(2 * D), row(D), row(D)] + _dep_spec(dep),
        out_specs=(row(D), row(D), row(2 * D), pl.BlockSpec((1, 2 * D), lambda i: (0, 0)), row(P), row(HV)),
        compiler_params=_params("arbitrary"),
    )(dxo, wout, wpp, wap, gates, ba, bb, *([] if dep is None else [dep]))


def _ffn_fwd(x, h, w, tag, next_gain, dep=None):
    gate, up, a, xn, hn = _ffn_fwd_core(x, h, w["up_t"], w["wd"], next_gain, alpha=0.5,
                                        name=f"{tag}_fwd" if next_gain is not None else f"{tag}_fwd_last", dep=dep)
    return xn, hn, (x, h, gate, up, a)


def _ffn_bwd(dxo, gain, w, saved, tag, dep=None, early=None, mid=None):
    x, h, gate, up, a = saved
    dwd = _mm(a, dxo, ta=True, alpha=0.5, out_dtype=BF16, name=f"{tag}_dwd", tm=1408, tn=1024, tk=1024, dep=dep)
    if early is not None:
        dep = early(dwd)
    dgate, dup, dx, dgain = _ffn_bwd_core(dxo, w["wd"], w["up_t"], gate, up, x, gain, alpha=0.5,
                                          name=f"{tag}_bwd_core", dep=dep)
    dup_t = _ffn_dw_up(dgate, dup, h, name=f"{tag}_dw_up", dep=None if mid is None else mid(dgain))
    return dx, dgain, dup_t, dwd


def _mixer_fwd(x, h, p, w, tabs, S, next_gain, dep=None):
    xp, ql, kvl, qn, kvn, q, kv, kr = _mixer_in(h, w["win_t"], w["wuq_t"], w["wukv"], p["q_latent_norm"],
                                                 p["kv_latent_norm"], tabs, name="mix_in", dep=dep)
    ms = _pool_fwd(xp, p["pool_maps"], p["pool_scale"], S=S, name="pool_fwd")
    o, lse = _attn_fwd(q, kv, kr, S=S, name="attn_fwd")
    gates, ba, bb, merged, xn, hn = _merge_out(h, ms, o, x, w["win_t"], p["b_gate"], w["wpp"], w["wap"], w["wout"],
                                               next_gain, name="merge_out")
    return xn, hn, (x, h, xp, ql, kvl, qn, kvn, q, kv, kr, ms, o, lse, gates, ba, bb, merged)


def _mixer_bwd(dxo, p, w, tabs, saved, S, dep=None):
    x, h, xp, ql, kvl, qn, kvn, q, kv, kr, ms, o, lse, gates, ba, bb, merged = saved
    dba, dbb, dgl, dbg, dms, do = _merge_bwd(dxo, w["wout"], w["wpp"], w["wap"], gates, ba, bb, name="merge_bwd",
                                             dep=dep)
    g = {}
    g["wout"], g["wpp"], g["wap"] = _dw_multi([(merged, dxo), (ms, dba), (o, dbb)], name="d_w_merge")
    dxp, g["pool_maps"], g["pool_scale"] = _pool_bwd(xp, dms, p["pool_maps"], p["pool_scale"], S=S, name="pool_bwd")
    dq, dkv, dkr = _attn_bwd(q, kv, kr, o, do, lse, S=S, name="attn_bwd")
    dproj, dqp, g["q_latent_norm"], g["kv_latent_norm"], dx, g["norm_mix"] = _mixer_in_bwd(
        dq, dkv, dkr, ql, kvl, dxp, dgl, x, dxo, w["win_t"], w["wuq_t"], w["wukv"],
        p["norm_mix"], p["q_latent_norm"], p["kv_latent_norm"], tabs, name="mix_in_bwd")
    g["wuq_t"], g["wukv"] = _dw_multi([(dqp, qn), (kvn, dkv)], name="d_w_qkv", tk=1024)
    g["wa_t"], g["wgate_t"] = _dw_multi([(dproj, h), (dgl, h)], name="d_w_in")
    g["b_gate"] = dbg
    return dx, g


BIG = ("ffn1_up", "ffn1_down", "w_in", "w_pool_proj", "w_uq", "w_ukv", "w_attn_proj", "w_out", "ffn2_up", "ffn2_down")
SMALL = ("norm_ffn1", "norm_mix", "b_gate", "pool_maps", "pool_scale", "q_latent_norm", "kv_latent_norm", "norm_ffn2")
PACKED = ("w_pool_proj", "w_uq", "w_ukv")
TRANSPOSED = ("ffn1_up", "ffn2_up", "w_in", "w_uq")
COL_SHARDED = ("w_pool_proj", "w_ukv")
QK_HEAD = QK_NOPE + QK_ROPE


def _rows(stacked):
    n, r, c = stacked.shape
    return stacked.reshape(n * r, c)


def _cols(stacked):
    n, k, c = stacked.shape
    return stacked.transpose(1, 0, 2).reshape(k, n * c)


FFN1_PART = ("ffn1_up", "ffn1_down")
MIXER_PART = ("w_in", "w_attn_proj", "w_out") + PACKED
FFN2_PART = ("ffn2_up", "ffn2_down")


def _kernel_weights(stacked):
    full = {}
    for tag in ("ffn1", "ffn2"):
        if tag + "_up" in stacked:
            full[tag] = {"up_t": _rows(stacked[tag + "_up"]), "wd": _rows(stacked[tag + "_down"])}
    if "w_in" in stacked:
        win_t = _rows(stacked["w_in"])
        wuq_t = _rows(stacked["w_uq"])
        QL = wuq_t.shape[1]
        H = wuq_t.shape[0] // QK_HEAD
        wuq_t = jnp.pad(wuq_t.reshape(H, QK_HEAD, QL), ((0, 0), (0, HEAD_W - QK_HEAD), (0, 0)))
        full.update({"win_t": win_t, "wuq_t": wuq_t.reshape(H * HEAD_W, QL),
                     "wukv": _cols(stacked["w_ukv"]), "wpp": _cols(stacked["w_pool_proj"]),
                     "wap": _rows(stacked["w_attn_proj"]), "wout": _rows(stacked["w_out"])})
    return full


def _split_rows(full):
    return full.reshape(N_DEV, full.shape[0] // N_DEV, full.shape[1])


def _split_cols(full):
    k, cols = full.shape
    return full.reshape(k, N_DEV, cols // N_DEV).transpose(1, 0, 2)


def _mixer_grads_stacked(g):
    n_a = g["wa_t"].shape[0] - (LANE - QK_ROPE)
    HQ, QL = g["wuq_t"].shape
    H = HQ // HEAD_W
    wuq_t = g["wuq_t"].reshape(H, HEAD_W, QL)[:, :QK_HEAD].reshape(H * QK_HEAD, QL)
    return {"w_in": _split_rows(jnp.concatenate([g["wa_t"][:n_a], g["wgate_t"]], axis=0)),
            "w_uq": _split_rows(wuq_t),
            "w_pool_proj": _split_cols(g["wpp"]), "w_ukv": _split_cols(g["wukv"]),
            "w_attn_proj": _split_rows(g["wap"]), "w_out": _split_rows(g["wout"])}


def _mesh_place():
    x, y, c = lax.axis_index("x"), lax.axis_index("y"), lax.axis_index("c")
    chips = [(1 - x, y), (x, 1 - y), (1 - x, 1 - y)]
    return x, y, c, chips


HBM = pl.BlockSpec(memory_space=pltpu.HBM)
SEMAPHORES = pl.BlockSpec(memory_space=pltpu.SEMAPHORE)
DATAFLOW = pltpu.SideEffectType.DATAFLOW_SIDE_EFFECTING
GATHER_PEERS = 4
SCATTER_PEERS = 7


def _in_hbm(a):
    return pltpu.with_memory_space_constraint(a, pltpu.HBM)


def _gather_plan(src_refs, land_refs):
    x, y, c, chips = _mesh_place()
    me = 4 * x + 2 * y + c
    targets = [(x, y, 1 - c)] + [(cx, cy, c) for cx, cy in chips]
    return [(s, land.at[me], to) for s, land in zip(src_refs, land_refs) for to in targets]


def _scatter_plan(src_refs, land_refs):
    x, y, c, _ = _mesh_place()
    peers = [(x, y, 1 - c), (1 - x, y, c), (x, 1 - y, c), (1 - x, 1 - y, c),
             (1 - x, y, 1 - c), (x, 1 - y, 1 - c), (1 - x, 1 - y, 1 - c)]
    return [(s.at[4 * px + 2 * py + pc], land.at[k], (px, py, pc))
            for s, land in zip(src_refs, land_refs) for k, (px, py, pc) in enumerate(peers)]


def _descriptors(plan, src_refs, land_refs, send_sems, recv_sems):
    return [pltpu.make_async_remote_copy(src_ref=s, dst_ref=d, send_sem=send_sems.at[k], recv_sem=recv_sems.at[k],
                                         device_id=to, device_id_type=MESH)
            for k, (s, d, to) in enumerate(plan(src_refs, land_refs))]


def _exchange(srcs, land_shapes, plan, per_src, *, name):
    n = len(srcs)

    def body(*refs):
        copies = _descriptors(plan, refs[:n], refs[n:2 * n], refs[2 * n], refs[2 * n + 1])
        for cp in copies:
            cp.start()
        for cp in copies:
            cp.wait()

    return pl.pallas_call(
        body, name=name,
        out_shape=tuple(jax.ShapeDtypeStruct(shape, s.dtype) for shape, s in zip(land_shapes, srcs)),
        in_specs=[ANY] * n, out_specs=(ANY,) * n,
        scratch_shapes=[pltpu.SemaphoreType.DMA((per_src * n,)), pltpu.SemaphoreType.DMA((per_src * n,))],
    )(*srcs)


FORWARD_COPIES = 4


def _forward_slots():
    x, y, c, chips = _mesh_place()
    return [4 * cx + 2 * cy + c for cx, cy in chips] + [4 * x + 2 * y + (1 - c)], (x, y, 1 - c)


def _forward_plan(src_refs, land_refs):
    slots, sibling = _forward_slots()
    return [(land.at[s], land.at[s], sibling) for land in land_refs for s in slots]


def _gather_all_plan(src_refs, land_refs):
    x, y, c, _ = _mesh_place()
    me = 4 * x + 2 * y + c
    peers = [(x, y, 1 - c), (1 - x, y, c), (x, 1 - y, c), (1 - x, 1 - y, c),
             (1 - x, y, 1 - c), (x, 1 - y, 1 - c), (1 - x, 1 - y, 1 - c)]
    return [(s, land.at[me], to) for s, land in zip(src_refs, land_refs) for to in peers]


def _exchange_start(srcs, lands, plan, n_copies, *, name):
    ns, n = len(srcs), len(srcs) + len(lands)

    def body(*refs):
        for cp in _descriptors(plan, refs[:ns], refs[ns:n], refs[n], refs[n + 1]):
            cp.start()
        refs[-1][...] = jnp.zeros_like(refs[-1])

    sems = pltpu.SemaphoreType.DMA((n_copies,))
    out = pl.pallas_call(
        body, name=name,
        out_shape=(sems, sems, *[pltpu.HBM(a.shape, a.dtype) for a in srcs + lands],
                   jax.ShapeDtypeStruct((8, LANE), F32)),
        in_specs=(HBM,) * n,
        out_specs=(SEMAPHORES, SEMAPHORES, *[HBM] * n, pl.BlockSpec(memory_space=pltpu.VMEM)),
        input_output_aliases={i: 2 + i for i in range(n)},
        compiler_params=pltpu.CompilerParams(has_side_effects=DATAFLOW),
    )(*[_in_hbm(a) for a in srcs + lands])
    return out[0], out[1], list(out[2:2 + ns]), list(out[2 + ns:2 + n]), out[-1]


def _exchange_wait(send_sems, recv_sems, srcs, lands, plan, after, *, name):
    ns, n = len(srcs), len(srcs) + len(lands)

    def body(*refs):
        for cp in _descriptors(plan, refs[:ns], refs[ns:n], refs[n], refs[n + 1]):
            cp.wait_send()
            cp.wait_recv()

    out = pl.pallas_call(
        body, name=name,
        out_shape=tuple(pltpu.HBM(a.shape, a.dtype) for a in srcs + lands),
        in_specs=(*[HBM] * n, SEMAPHORES, SEMAPHORES, ANY),
        out_specs=(HBM,) * n,
        input_output_aliases={i: i for i in range(n)},
        compiler_params=pltpu.CompilerParams(has_side_effects=DATAFLOW),
    )(*srcs, *lands, send_sems, recv_sems, after)
    return list(out[:ns]), list(out[ns:])


def _gather_forward(lands, *, name):
    n = len(lands)

    def body(*refs):
        in_refs, out_refs = refs[:n], refs[n:2 * n]
        token, send_sems, recv_sems = refs[2 * n:2 * n + 3]
        slots, sibling = _forward_slots()
        passed = [pltpu.make_async_remote_copy(
            src_ref=i.at[s], dst_ref=o.at[s],
            send_sem=send_sems.at[FORWARD_COPIES * b + j], recv_sem=recv_sems.at[FORWARD_COPIES * b + j],
            device_id=sibling, device_id_type=MESH)
            for b, (i, o) in enumerate(zip(in_refs, out_refs)) for j, s in enumerate(slots)]
        for cp in passed:
            cp.start()
        for cp in passed:
            cp.wait()
        token[...] = jnp.zeros_like(token)

    out = pl.pallas_call(
        body, name=name,
        out_shape=(*[jax.ShapeDtypeStruct(a.shape, a.dtype) for a in lands], jax.ShapeDtypeStruct((8, LANE), F32)),
        in_specs=[ANY] * n,
        out_specs=(*[ANY] * n, pl.BlockSpec(memory_space=pltpu.VMEM)),
        input_output_aliases={i: i for i in range(n)},
        scratch_shapes=[pltpu.SemaphoreType.DMA((FORWARD_COPIES * n,)), pltpu.SemaphoreType.DMA((FORWARD_COPIES * n,))],
    )(*lands)
    return list(out[:n]), out[n]


def _scatter_sum(parts, got, me, into, layer, *, name):
    shard = parts.shape[1:]
    cols = shard[-1]
    rows = int(np.prod(shard[:-1]))
    tr = _tile(rows, 256, 16)
    layers = into.shape[0]

    def body(me_ref, p_ref, g_ref, into_ref, o_ref):
        acc = p_ref[...].astype(F32)
        for k in range(SCATTER_PEERS):
            acc = acc + g_ref[k].astype(F32)
        o_ref[...] = acc

    slab = pl.BlockSpec((None, tr, cols), lambda r, me_ref: (layer, r, 0))
    out = pl.pallas_call(
        body, name=name,
        out_shape=jax.ShapeDtypeStruct((layers, rows, cols), F32),
        grid_spec=pltpu.PrefetchScalarGridSpec(
            num_scalar_prefetch=1, grid=(rows // tr,),
            in_specs=[pl.BlockSpec((None, tr, cols), lambda r, me_ref: (me_ref[0], r, 0)),
                      pl.BlockSpec((SCATTER_PEERS, tr, cols), lambda r, me_ref: (0, r, 0)),
                      ANY],
            out_specs=slab),
        input_output_aliases={3: 0},
        compiler_params=_params("parallel"),
    )(me, parts.reshape(N_DEV, rows, cols), got.reshape(SCATTER_PEERS, rows, cols), into.reshape(layers, rows, cols))
    return out.reshape(layers, *shard)


def _sum_devices(parts, *, name):
    _, R, C = parts.shape
    tr = _tile(R, 512, 8)

    def body(p_ref, o_ref):
        acc = p_ref[0]
        for d in range(1, N_DEV):
            acc = acc + p_ref[d]
        o_ref[...] = acc

    return pl.pallas_call(
        body, name=name,
        out_shape=jax.ShapeDtypeStruct((R, C), F32),
        grid=(R // tr,),
        in_specs=[pl.BlockSpec((N_DEV, tr, C), lambda r: (0, r, 0))],
        out_specs=pl.BlockSpec((tr, C), lambda r: (r, 0)),
        compiler_params=_params("parallel"),
    )(parts)


def _adamw(w, g, m, v, *, name, dep=None):
    shape = w.shape
    cols = shape[-1]
    rows = w.size // cols
    tr = _tile(rows, 256, 8)

    def body(w_ref, g_ref, m_ref, v_ref, *rest):
        d_ref, nm_ref, nv_ref = rest[-3:]
        g = g_ref[...]
        m = ADAM_B1 * m_ref[...] + (1.0 - ADAM_B1) * g
        v = ADAM_B2 * v_ref[...] + (1.0 - ADAM_B2) * jnp.square(g)
        m_hat = m / (1.0 - ADAM_B1 ** ADAM_STEP)
        v_hat = v / (1.0 - ADAM_B2 ** ADAM_STEP)
        d_ref[...] = -ADAM_LR * (m_hat / (jnp.sqrt(v_hat) + ADAM_EPS) + ADAM_WD * w_ref[...])
        nm_ref[...] = m
        nv_ref[...] = v

    spec = pl.BlockSpec((tr, cols), lambda i: (i, 0))
    out = jax.ShapeDtypeStruct((rows, cols), F32)
    d, nm, nv = pl.pallas_call(
        body, name=name,
        out_shape=(out, out, out),
        grid=(rows // tr,),
        in_specs=[spec] * 4 + _dep_spec(dep), out_specs=(spec,) * 3,
        compiler_params=_params("parallel"),
    )(*(a.reshape(rows, cols) for a in (w, g, m, v)), *([] if dep is None else [dep]))
    return d.reshape(shape), nm.reshape(shape), nv.reshape(shape)


PACK_ALIGN = 16 * LANE


def _pack(pieces, lead):
    out = []
    for p in pieces:
        keep = p.shape[:lead]
        flat = p.reshape(*keep, -1)
        pad = (-flat.shape[-1]) % PACK_ALIGN
        if pad:
            flat = jnp.pad(flat, [(0, 0)] * lead + [(0, pad)])
        out.append(flat.reshape(*keep, -1, LANE))
    return jnp.concatenate(out, axis=lead)


def _unpack(buf, shapes, lead):
    keep = buf.shape[:lead]
    out, row = [], 0
    for shape in shapes:
        size = int(np.prod(shape))
        rows = -(-size // PACK_ALIGN) * (PACK_ALIGN // LANE)
        piece = lax.slice_in_dim(buf, row, row + rows, axis=lead).reshape(*keep, rows * LANE)
        out.append(lax.slice_in_dim(piece, 0, size, axis=lead).reshape(*keep, *shape))
        row += rows
    return out


def kernel(x, positions, norm_ffn1, ffn1_up, ffn1_down, norm_mix, w_in, b_gate, pool_maps, pool_scale, w_pool_proj, q_latent_norm, w_uq, kv_latent_norm, w_ukv, w_attn_proj, w_out, norm_ffn2, ffn2_up, ffn2_down, final_norm, loss_target, m_norm_ffn1, m_ffn1_up, m_ffn1_down, m_norm_mix, m_w_in, m_b_gate, m_pool_maps, m_pool_scale, m_w_pool_proj, m_q_latent_norm, m_w_uq, m_kv_latent_norm, m_w_ukv, m_w_attn_proj, m_w_out, m_norm_ffn2, m_ffn2_up, m_ffn2_down, m_final_norm, v_norm_ffn1, v_ffn1_up, v_ffn1_down, v_norm_mix, v_w_in, v_b_gate, v_pool_maps, v_pool_scale, v_w_pool_proj, v_q_latent_norm, v_w_uq, v_kv_latent_norm, v_w_ukv, v_w_attn_proj, v_w_out, v_norm_ffn2, v_ffn2_up, v_ffn2_down, v_final_norm):
    order = ("norm_ffn1", "ffn1_up", "ffn1_down", "norm_mix", "w_in", "b_gate", "pool_maps", "pool_scale",
             "w_pool_proj", "q_latent_norm", "w_uq", "kv_latent_norm", "w_ukv", "w_attn_proj", "w_out",
             "norm_ffn2", "ffn2_up", "ffn2_down", "final_norm")
    w = dict(zip(order, (norm_ffn1, ffn1_up, ffn1_down, norm_mix, w_in, b_gate, pool_maps, pool_scale, w_pool_proj,
                         q_latent_norm, w_uq, kv_latent_norm, w_ukv, w_attn_proj, w_out, norm_ffn2, ffn2_up,
                         ffn2_down, final_norm)))
    m = dict(zip(order, (m_norm_ffn1, m_ffn1_up, m_ffn1_down, m_norm_mix, m_w_in, m_b_gate, m_pool_maps, m_pool_scale,
                         m_w_pool_proj, m_q_latent_norm, m_w_uq, m_kv_latent_norm, m_w_ukv, m_w_attn_proj, m_w_out,
                         m_norm_ffn2, m_ffn2_up, m_ffn2_down, m_final_norm)))
    v = dict(zip(order, (v_norm_ffn1, v_ffn1_up, v_ffn1_down, v_norm_mix, v_w_in, v_b_gate, v_pool_maps, v_pool_scale,
                         v_w_pool_proj, v_q_latent_norm, v_w_uq, v_kv_latent_norm, v_w_ukv, v_w_attn_proj, v_w_out,
                         v_norm_ffn2, v_ffn2_up, v_ffn2_down, v_final_norm)))
    L = norm_ffn1.shape[0]
    B, S, D = x.shape
    T = B * S

    def turned(a, n):
        return a.transpose(0, 2, 1) if n in TRANSPOSED else a

    wk, mk, vk = ({n: turned(d[n], n) for n in order} for d in (w, m, v))
    packed_shapes = [wk[n].shape[1:] for n in PACKED]
    my_slot = 4 * lax.axis_index("x") + 2 * lax.axis_index("y") + lax.axis_index("c")
    me = jnp.stack([my_slot]).astype(jnp.int32)

    def weight_blocks(l, names, token):
        zero = token[0, 0].astype(BF16)
        blocks = [wk[n][l].astype(BF16) + zero for n in names if n not in PACKED]
        if any(n in PACKED for n in names):
            blocks.append(_pack([wk[n][l].astype(BF16) + zero for n in PACKED], 0))
        return blocks

    def kernel_weights(names, lands):
        direct = [n for n in names if n not in PACKED]
        stacked = dict(zip(direct, lands))
        if len(lands) > len(direct):
            stacked.update(zip(PACKED, _unpack(lands[-1], packed_shapes, 1)))
        return _kernel_weights(stacked)

    def gather_start(l, names, token, tag):
        blocks = weight_blocks(l, names, token)
        lands = [lax.empty((N_DEV, *b.shape), b.dtype) for b in blocks]
        send_sems, recv_sems, blocks, lands, token = _exchange_start(
            blocks, lands, _gather_plan, GATHER_PEERS * len(blocks), name=f"gather_start_{tag}")
        return (send_sems, recv_sems, blocks, lands, tag), token

    def gather_wait(state, after):
        send_sems, recv_sems, blocks, lands, tag = state
        return _exchange_wait(send_sems, recv_sems, blocks, lands, _gather_plan, after, name=f"gather_wait_{tag}")[1]

    layer_part = FFN1_PART + MIXER_PART + FFN2_PART
    tabs = _rope_tables(positions.reshape(T))
    xs = x.reshape(T, D)
    h = _rms_fwd(xs, w["norm_ffn1"][0], name="first_norm")
    full, saved = [], []

    p = {n: w[n][0] for n in SMALL}
    blocks = weight_blocks(0, FFN1_PART, jnp.zeros((8, LANE), F32))
    lands = _exchange(blocks, [(N_DEV, *b.shape) for b in blocks], _gather_plan, GATHER_PEERS, name="gather_first")
    lands, token = _gather_forward(lands, name="gather_forward")
    w0 = kernel_weights(FFN1_PART, lands)
    state, token = gather_start(0, MIXER_PART, token, "0_mix")
    xs, h, s1 = _ffn_fwd(xs, h, w0["ffn1"], "ffn1", p["norm_mix"], dep=token)
    lands, token = _gather_forward(gather_wait(state, xs), name="gather_forward")
    w0.update(kernel_weights(MIXER_PART, lands))
    state, token = gather_start(0, FFN2_PART, token, "0_ffn2")
    if L > 1:
        next_state, token = gather_start(1, layer_part, token, "1")
    xs, h, s2 = _mixer_fwd(xs, h, p, w0, tabs, S, p["norm_ffn2"], dep=token)
    lands, token = _gather_forward(gather_wait(state, xs), name="gather_forward")
    w0.update(kernel_weights(FFN2_PART, lands))
    xs, h, s3 = _ffn_fwd(xs, h, w0["ffn2"], "ffn2", w["norm_ffn1"][1] if L > 1 else None, dep=token)
    if L > 1:
        lands, token = _gather_forward(gather_wait(next_state, xs), name="gather_forward")
    full.append(w0)
    saved.append((s1, s2, s3))

    for l in range(1, L):
        full.append(kernel_weights(layer_part, lands))
        more = l + 1 < L
        p = {n: w[n][l] for n in SMALL}
        if more:
            state, token = gather_start(l + 1, layer_part, token, f"{l + 1}")
        xs, h, s1 = _ffn_fwd(xs, h, full[l]["ffn1"], "ffn1", p["norm_mix"], dep=token if more else None)
        xs, h, s2 = _mixer_fwd(xs, h, p, full[l], tabs, S, p["norm_ffn2"])
        if more:
            lands = gather_wait(state, xs)
            send_sems, recv_sems, _, lands, token = _exchange_start(
                [], lands, _forward_plan, FORWARD_COPIES * len(lands), name=f"forward_start_{l + 1}")
        xs, h, s3 = _ffn_fwd(xs, h, full[l]["ffn2"], "ffn2", w["norm_ffn1"][l + 1] if more else None,
                             dep=token if more else None)
        if more:
            _, lands = _exchange_wait(send_sems, recv_sems, [], lands, _forward_plan, xs, name=f"forward_wait_{l + 1}")
        saved.append((s1, s2, s3))
    dx, dfinal, loss = _loss_head(xs, final_norm, loss_target.reshape(T, D), name="loss_head")

    big_grads = {n: [None] * L if n in PACKED else lax.empty((L, *wk[n].shape[1:]), F32) for n in BIG}
    small_grads_of = [None] * L
    pending = None

    def scatter_start(names, stacked, tag):
        srcs = [stacked[n] for n in names if n not in PACKED]
        if any(n in PACKED for n in names):
            srcs.append(_pack([stacked[n] for n in PACKED], 1))
        lands = [lax.empty((SCATTER_PEERS, *s.shape[1:]), s.dtype) for s in srcs]
        send_sems, recv_sems, srcs, lands, token = _exchange_start(
            srcs, lands, _scatter_plan, SCATTER_PEERS * len(srcs), name=f"scatter_start_{tag}")
        return (names, send_sems, recv_sems, srcs, lands, tag), token

    def scatter_finish(state, after, l):
        names, send_sems, recv_sems, srcs, lands, tag = state
        srcs, got = _exchange_wait(send_sems, recv_sems, srcs, lands, _scatter_plan, after, name=f"scatter_wait_{tag}")
        direct = [n for n in names if n not in PACKED]
        for n, s, g in zip(direct, srcs, got):
            big_grads[n] = _scatter_sum(s, g, me, big_grads[n], l, name="scatter_sum")
        if len(srcs) > len(direct):
            packed = _scatter_sum(srcs[-1], got[-1], me, lax.empty((1, *srcs[-1].shape[1:]), F32), 0,
                                  name="scatter_sum")[0]
            for n, g in zip(PACKED, _unpack(packed, packed_shapes, 0)):
                big_grads[n][l] = g

    dep = None
    for l in reversed(range(L)):
        p = {n: w[n][l] for n in SMALL}
        s1, s2, s3 = saved[l]
        small_g = {}
        dx, small_g["norm_ffn2"], dup_t, dwd = _ffn_bwd(dx, p["norm_ffn2"], full[l]["ffn2"], s3, "ffn2", dep=dep)
        if pending is not None:
            scatter_finish(pending[0], dx, pending[1])
        stacked = {"ffn2_up": _split_rows(dup_t), "ffn2_down": _split_rows(dwd)}
        state, dep = scatter_start(("ffn2_up", "ffn2_down"), stacked, f"ffn2_{l}")
        pending = (state, l)

        dx, gm = _mixer_bwd(dx, p, full[l], tabs, s2, S, dep=dep)
        scatter_finish(pending[0], dx, pending[1])
        names = ("w_in", "w_attn_proj", "w_out") + PACKED
        state, dep = scatter_start(names, _mixer_grads_stacked(gm), f"mix_{l}")
        pending = (state, l)
        small_g.update({n: gm[n] for n in SMALL if n in gm})

        if l > 0:
            dx, small_g["norm_ffn1"], dup_t, dwd = _ffn_bwd(dx, p["norm_ffn1"], full[l]["ffn1"], s1, "ffn1", dep=dep)
            scatter_finish(pending[0], dx, pending[1])
            stacked = {"ffn1_up": _split_rows(dup_t), "ffn1_down": _split_rows(dwd)}
            state, dep = scatter_start(("ffn1_up", "ffn1_down"), stacked, f"ffn1_{l}")
            pending = (state, l)
        else:
            early_states = []

            def send_down(dwd):
                state, token = scatter_start(("ffn1_down",), {"ffn1_down": _split_rows(dwd)}, "ffn1_down_0")
                early_states.append(state)
                return token

            def send_small(dgain):
                small_g["norm_ffn1"] = dgain
                small_grads_of[0] = small_g
                parts = [small_grads_of[k][n] for k in range(L) for n in SMALL] + [dfinal, loss[0, :1]]
                vec = _pack([jnp.concatenate([a.reshape(-1) for a in parts])], 0)
                out = _exchange_start([vec], [lax.empty((N_DEV, *vec.shape), F32)], _gather_all_plan, SCATTER_PEERS,
                                      name="small_start")
                early_states.append((out, [a.shape for a in parts]))
                return out[4]

            dx, _, dup_t, _ = _ffn_bwd(dx, p["norm_ffn1"], full[l]["ffn1"], s1, "ffn1", dep=dep, early=send_down,
                                       mid=send_small)
            last_mixer, last_down = pending, (early_states[0], 0)
            (small_send, small_recv, vec_thru, small_land, _), small_shapes = early_states[1]
            state, dep = scatter_start(("ffn1_up",), {"ffn1_up": _split_rows(dup_t)}, "ffn1_up_0")
            pending = (state, l)
        small_grads_of[l] = small_g
    grad_x = dx.reshape(B, S, D)

    gk, grad, delta, new_m, new_v = {}, {}, {}, {}, {}

    def update(n, dep=None):
        wn, gn, mn, vn = (a.reshape(1, -1) if a.ndim == 1 else a for a in (wk[n], gk[n], mk[n], vk[n]))
        d, nm, nv = _adamw(wn, gn, mn, vn, name="adamw_" + n, dep=dep)
        grad[n] = turned(gk[n], n)
        delta[n], new_m[n], new_v[n] = (turned(a.reshape(wk[n].shape), n) for a in (d, nm, nv))

    for i, n in enumerate(FFN2_PART):
        gk[n] = jnp.stack(big_grads[n]) if n in PACKED else big_grads[n]
        update(n, dep if i == 0 else None)
    scatter_finish(last_mixer[0], new_v["ffn2_down"], last_mixer[1])
    for n in MIXER_PART:
        gk[n] = jnp.stack(big_grads[n]) if n in PACKED else big_grads[n]
        update(n)
    scatter_finish(last_down[0], new_v[MIXER_PART[-1]], last_down[1])
    scatter_finish(pending[0], new_v[MIXER_PART[-1]], pending[1])
    for n in FFN1_PART:
        gk[n] = jnp.stack(big_grads[n]) if n in PACKED else big_grads[n]
        update(n)

    vec_thru, small_land = _exchange_wait(small_send, small_recv, vec_thru, small_land, _gather_all_plan,
                                          new_v["ffn1_down"], name="small_wait")
    parts = lax.dynamic_update_index_in_dim(small_land[0], vec_thru[0], my_slot, 0)
    flat = _sum_devices(parts, name="sum_small").reshape(-1)
    small_grads, at = [], 0
    for shape in small_shapes:
        size = int(np.prod(shape))
        small_grads.append(lax.slice_in_dim(flat, at, at + size).reshape(shape))
        at += size
    loss_total = small_grads[-1].reshape(())
    for i, n in enumerate(SMALL):
        gk[n] = jnp.stack([small_grads[l * len(SMALL) + i] for l in range(L)]).reshape(w[n].shape)
        update(n)
    gk["final_norm"] = small_grads[-2].reshape(final_norm.shape)
    update("final_norm")
    return (loss_total, grad_x, *[grad[n] for n in order], *[delta[n] for n in order],
            *[new_m[n] for n in order], *[new_v[n] for n in order])
```

```python
import functools

import numpy as np
import jax
import jax.numpy as jnp
from jax import lax
from jax.experimental import pallas as pl
from jax.experimental.pallas import tpu as pltpu

F32 = jnp.float32
BF16 = jnp.bfloat16

NORM_EPS = 1e-6
ROPE_THETA = 10000.0
QK_NOPE = 128
QK_ROPE = 64
V_DIM = 128
HEAD_W = 256
POOL_WINDOWS = (2, 4, 8, 16)
POOL_G = 128
POOL_DIM = 512
LANE = 128
ATTN_SCALE = float((QK_NOPE + QK_ROPE) ** -0.5)
ATTN_SCALE_LOG2 = ATTN_SCALE * float(np.log2(np.e))
MASK_VALUE = -1e30
ATTN_TILE = 512

ADAM_LR = 0.001
ADAM_B1 = 0.9
ADAM_B2 = 0.999
ADAM_EPS = 1e-08
ADAM_WD = 0.01
ADAM_STEP = 10

N_DEV = 8
VMEM_LIMIT = 52 * 1024 * 1024

MESH = pl.DeviceIdType.MESH
ANY = pl.BlockSpec(memory_space=pl.ANY)


def _tile(dim, target, align=LANE):
    if dim <= target:
        return dim
    t = (target // align) * align
    while t >= align:
        if dim % t == 0:
            return t
        t -= align
    return dim


def _params(*sem):
    return pltpu.CompilerParams(dimension_semantics=sem, vmem_limit_bytes=VMEM_LIMIT)


def _rstd(x):
    return lax.rsqrt(jnp.mean(x * x, axis=-1, keepdims=True) + NORM_EPS)


def _mm(a, b, *, name, ta=False, tb=False, out_dtype=F32, alpha=1.0, tm=512, tn=1024, tk=1024, dep=None):
    if ta:
        K, M = a.shape
    else:
        M, K = a.shape
    if tb:
        N, K2 = b.shape
    else:
        K2, N = b.shape
    assert K == K2, (a.shape, b.shape, ta, tb)
    tm, tn, tk = _tile(M, tm), _tile(N, tn), _tile(K, tk)
    nk = K // tk
    dims = (((0 if ta else 1,), (1 if tb else 0,)), ((), ()))

    def body(a_ref, b_ref, *rest):
        o_ref = rest[0 if dep is None else 1]
        acc_ref = rest[-1] if nk > 1 else None
        part = lax.dot_general(a_ref[...].astype(BF16), b_ref[...].astype(BF16), dims,
                               preferred_element_type=F32)

        def finish(acc):
            o_ref[...] = (acc * alpha if alpha != 1.0 else acc).astype(out_dtype)

        if nk == 1:
            finish(part)
        else:
            k = pl.program_id(2)

            @pl.when(k == 0)
            def _():
                acc_ref[...] = part

            @pl.when(k > 0)
            def _():
                acc_ref[...] += part

            @pl.when(k == nk - 1)
            def _():
                finish(acc_ref[...])

    a_spec = pl.BlockSpec((tk, tm), lambda i, j, k: (k, i)) if ta else pl.BlockSpec((tm, tk), lambda i, j, k: (i, k))
    b_spec = pl.BlockSpec((tn, tk), lambda i, j, k: (j, k)) if tb else pl.BlockSpec((tk, tn), lambda i, j, k: (k, j))
    return pl.pallas_call(
        body, name=name,
        out_shape=jax.ShapeDtypeStruct((M, N), out_dtype),
        grid=(M // tm, N // tn, nk),
        in_specs=[a_spec, b_spec] + _dep_spec(dep),
        out_specs=pl.BlockSpec((tm, tn), lambda i, j, k: (i, j)),
        scratch_shapes=[pltpu.VMEM((tm, tn), F32)] if nk > 1 else [],
        compiler_params=_params("parallel", "parallel", "arbitrary"),
    )(a, b, *([] if dep is None else [dep]))


def _dw_multi(pairs, *, name, tk=512):
    n = len(pairs)
    T = pairs[0][0].shape[0]
    tk = _tile(T, tk, 16)
    nk = T // tk
    shapes = [(a.shape[1], b.shape[1]) for a, b in pairs]

    def body(*refs):
        ins, outs, accs = refs[:2 * n], refs[2 * n:3 * n], refs[3 * n:]
        k = pl.program_id(0)
        parts = [lax.dot_general(ins[2 * i][...].astype(BF16), ins[2 * i + 1][...].astype(BF16), _TN,
                                 preferred_element_type=F32) for i in range(n)]

        @pl.when(k == 0)
        def _():
            for acc, part in zip(accs, parts):
                acc[...] = part

        @pl.when(k > 0)
        def _():
            for acc, part in zip(accs, parts):
                acc[...] += part

        @pl.when(k == nk - 1)
        def _():
            for out, acc in zip(outs, accs):
                out[...] = acc[...].astype(BF16)

    return pl.pallas_call(
        body, name=name,
        out_shape=tuple(jax.ShapeDtypeStruct(s, BF16) for s in shapes),
        grid=(nk,),
        in_specs=[pl.BlockSpec((tk, x.shape[1]), lambda k: (k, 0)) for pair in pairs for x in pair],
        out_specs=tuple(pl.BlockSpec(s, lambda k: (0, 0)) for s in shapes),
        scratch_shapes=[pltpu.VMEM(s, F32) for s in shapes],
        compiler_params=_params("arbitrary"),
    )(*[x for pair in pairs for x in pair])


def _rms_fwd(x, g, *, name):
    T, D = x.shape
    tm = _tile(T, 512, 16)

    def body(x_ref, g_ref, h_ref):
        x = x_ref[...]
        h_ref[...] = (x * _rstd(x) * g_ref[...]).astype(BF16)

    return pl.pallas_call(
        body, name=name,
        out_shape=jax.ShapeDtypeStruct((T, D), BF16),
        grid=(T // tm,),
        in_specs=[pl.BlockSpec((tm, D), lambda i: (i, 0)), pl.BlockSpec((1, D), lambda i: (0, 0))],
        out_specs=pl.BlockSpec((tm, D), lambda i: (i, 0)),
        compiler_params=_params("parallel"),
    )(x, g.reshape(1, D))


def _loss_head(x, g, target, *, name):
    T, D = x.shape
    tm = _tile(T, 512, 16)

    def body(x_ref, g_ref, t_ref, dx_ref, dg_ref, loss_ref):
        x = x_ref[...]
        gain = g_ref[...]
        r = _rstd(x)
        xhat = x * r
        err = xhat * gain - t_ref[...]
        dy = err * (1.0 / D)
        dxh = dy * gain
        dx_ref[...] = r * (dxh - xhat * jnp.mean(dxh * xhat, axis=-1, keepdims=True))
        dg_part = jnp.sum(dy * xhat, axis=0, keepdims=True)
        loss_part = jnp.full((1, LANE), 0.5 / D, F32) * jnp.sum(err * err)

        @pl.when(pl.program_id(0) == 0)
        def _():
            dg_ref[...] = dg_part
            loss_ref[...] = loss_part

        @pl.when(pl.program_id(0) > 0)
        def _():
            dg_ref[...] += dg_part
            loss_ref[...] += loss_part

    row = pl.BlockSpec((tm, D), lambda i: (i, 0))
    vec = pl.BlockSpec((1, D), lambda i: (0, 0))
    return pl.pallas_call(
        body, name=name,
        out_shape=(jax.ShapeDtypeStruct((T, D), F32), jax.ShapeDtypeStruct((1, D), F32),
                   jax.ShapeDtypeStruct((1, LANE), F32)),
        grid=(T // tm,),
        in_specs=[row, vec, row],
        out_specs=(row, vec, pl.BlockSpec((1, LANE), lambda i: (0, 0))),
        compiler_params=_params("arbitrary"),
    )(x, g.reshape(1, D), target)


def _ffn_fwd_core(x, h, w_up_t, wd, next_gain, *, alpha, name, dep=None):
    T, D = x.shape
    F = wd.shape[0]
    tm = _tile(T, 256, 16)
    has_norm = next_gain is not None

    def body(x_ref, h_ref, wg_ref, wu_ref, wd_ref, *rest):
        outs = rest[len(rest) - (5 if has_norm else 4):]
        gate_ref, up_ref, a_ref, xn_ref = outs[:4]
        h = h_ref[...]
        gate = lax.dot_general(h, wg_ref[...], _NT, preferred_element_type=F32)
        up = lax.dot_general(h, wu_ref[...], _NT, preferred_element_type=F32)
        a = (gate * jax.nn.sigmoid(gate) * up).astype(BF16)
        gate_ref[...] = gate.astype(BF16)
        up_ref[...] = up.astype(BF16)
        a_ref[...] = a
        xn = x_ref[...] + alpha * jnp.dot(a, wd_ref[...], preferred_element_type=F32)
        xn_ref[...] = xn
        if has_norm:
            outs[4][...] = (xn * _rstd(xn) * rest[0][...]).astype(BF16)

    once = pl.Buffered(1)
    row_d = pl.BlockSpec((tm, D), lambda i: (i, 0))
    row_f = pl.BlockSpec((tm, F), lambda i: (i, 0))
    vec = pl.BlockSpec((1, D), lambda i: (0, 0))
    act = jax.ShapeDtypeStruct((T, F), BF16)
    operands = [x, h, w_up_t, w_up_t, wd] + ([next_gain.reshape(1, D)] if has_norm else [])
    out = pl.pallas_call(
        body, name=name,
        out_shape=(act, act, act, jax.ShapeDtypeStruct((T, D), F32)) + ((jax.ShapeDtypeStruct((T, D), BF16),) if has_norm else ()),
        grid=(T // tm,),
        in_specs=[row_d, row_d,
                  pl.BlockSpec((F, D), lambda i: (0, 0), pipeline_mode=once),
                  pl.BlockSpec((F, D), lambda i: (1, 0), pipeline_mode=once),
                  pl.BlockSpec((F, D), lambda i: (0, 0), pipeline_mode=once)] + ([vec] if has_norm else []) + _dep_spec(dep),
        out_specs=(row_f, row_f, row_f, row_d) + ((row_d,) if has_norm else ()),
        compiler_params=_params("parallel"),
    )(*operands, *([] if dep is None else [dep]))
    return out if has_norm else (*out, None)


def _ffn_bwd_core(dxo, wd, w_up_t, gate, up, x, gain, *, alpha, name, dep=None):
    T, D = dxo.shape
    F = wd.shape[0]
    tm = _tile(T, 256, 16)

    def body(dxo_ref, wd_ref, wg_ref, wu_ref, gate_ref, up_ref, x_ref, g_ref, *rest):
        dgate_ref, dup_ref, dx_ref, dg_ref = rest[-4:]
        dxo = dxo_ref[...]
        da = lax.dot_general(dxo.astype(BF16), wd_ref[...], _NT, preferred_element_type=F32) * alpha
        gate = gate_ref[...].astype(F32)
        up = up_ref[...].astype(F32)
        sig = jax.nn.sigmoid(gate)
        dgate = (da * up * (sig * (1.0 + gate * (1.0 - sig)))).astype(BF16)
        dup = (da * (gate * sig)).astype(BF16)
        dgate_ref[...] = dgate
        dup_ref[...] = dup
        dh = (jnp.dot(dgate, wg_ref[...], preferred_element_type=F32)
              + jnp.dot(dup, wu_ref[...], preferred_element_type=F32))
        x = x_ref[...]
        r = _rstd(x)
        xhat = x * r
        dxh = dh * g_ref[...]
        dx_ref[...] = dxo + r * (dxh - xhat * jnp.mean(dxh * xhat, axis=-1, keepdims=True))
        part = jnp.sum(dh * xhat, axis=0, keepdims=True)

        @pl.when(pl.program_id(0) == 0)
        def _():
            dg_ref[...] = part

        @pl.when(pl.program_id(0) > 0)
        def _():
            dg_ref[...] += part

    once = pl.Buffered(1)
    row_d = pl.BlockSpec((tm, D), lambda i: (i, 0))
    row_f = pl.BlockSpec((tm, F), lambda i: (i, 0))
    vec = pl.BlockSpec((1, D), lambda i: (0, 0))
    act = jax.ShapeDtypeStruct((T, F), BF16)
    return pl.pallas_call(
        body, name=name,
        out_shape=(act, act, jax.ShapeDtypeStruct((T, D), F32), jax.ShapeDtypeStruct((1, D), F32)),
        grid=(T // tm,),
        in_specs=[row_d,
                  pl.BlockSpec((F, D), lambda i: (0, 0), pipeline_mode=once),
                  pl.BlockSpec((F, D), lambda i: (0, 0), pipeline_mode=once),
                  pl.BlockSpec((F, D), lambda i: (1, 0), pipeline_mode=once),
                  row_f, row_f, row_d, vec] + _dep_spec(dep),
        out_specs=(row_f, row_f, row_d, vec),
        compiler_params=_params("arbitrary"),
    )(dxo, wd, w_up_t, w_up_t, gate, up, x, gain.reshape(1, D), *([] if dep is None else [dep]))


def _ffn_dw_up(dgate, dup, h, *, name, dep=None):
    T, F = dgate.shape
    D = h.shape[1]
    tm, tk = _tile(F, 1408), _tile(T, 1024, 16)
    nf, nk = F // tm, T // tk

    def body(dgate_ref, dup_ref, h_ref, *rest):
        o_ref, acc_ref = rest[-2:]
        i, k = pl.program_id(0), pl.program_id(1)

        def accumulate(part):
            @pl.when(k == 0)
            def _():
                acc_ref[...] = part

            @pl.when(k > 0)
            def _():
                acc_ref[...] += part

        @pl.when(i < nf)
        def _():
            accumulate(lax.dot_general(dgate_ref[...], h_ref[...], _TN, preferred_element_type=F32))

        @pl.when(i >= nf)
        def _():
            accumulate(lax.dot_general(dup_ref[...], h_ref[...], _TN, preferred_element_type=F32))

        @pl.when(k == nk - 1)
        def _():
            o_ref[...] = acc_ref[...].astype(BF16)

    return pl.pallas_call(
        body, name=name,
        out_shape=jax.ShapeDtypeStruct((2 * F, D), BF16),
        grid=(2 * nf, nk),
        in_specs=[pl.BlockSpec((tk, tm), lambda i, k: (jnp.where(i < nf, k, nk - 1), jnp.minimum(i, nf - 1))),
                  pl.BlockSpec((tk, tm), lambda i, k: (jnp.where(i < nf, 0, k), jnp.maximum(i - nf, 0))),
                  pl.BlockSpec((tk, D), lambda i, k: (k, 0))] + _dep_spec(dep),
        out_specs=pl.BlockSpec((tm, D), lambda i, k: (i, 0)),
        scratch_shapes=[pltpu.VMEM((tm, D), F32)],
        compiler_params=_params("parallel", "arbitrary"),
    )(dgate, dup, h, *([] if dep is None else [dep]))


def _dep_spec(dep):
    return [] if dep is None else [pl.BlockSpec(dep.shape, lambda *_: (0,) * dep.ndim)]


def _rope_tables(positions):
    half = QK_ROPE // 2
    inv_freq = ROPE_THETA ** (-jnp.arange(0, QK_ROPE, 2, dtype=F32) / QK_ROPE)
    ang = positions.astype(F32)[:, None] * inv_freq
    cos, sin = jnp.cos(ang), jnp.sin(ang)
    z = jnp.zeros_like(cos)
    zz = jnp.zeros((positions.shape[0], LANE - QK_ROPE), F32)
    c = jnp.concatenate([cos, cos, zz], axis=1)
    sa = jnp.concatenate([z, sin, zz], axis=1)
    sb = jnp.concatenate([-sin, z, zz], axis=1)
    return c, sa, sb


def _rotate(seg, c, sa, sb, sign):
    half = QK_ROPE // 2
    mix = pltpu.roll(seg, half, 1) * sa + pltpu.roll(seg, LANE - half, 1) * sb
    return seg * c + mix if sign > 0 else seg * c - mix


def _mixer_in(h, wa, wuq, wukv, gq, gkv, tabs, *, name, dep=None):
    T, D = h.shape
    HQ, QL = wuq.shape
    KVL = wukv.shape[0]
    H = HQ // HEAD_W
    o_q, o_kv, o_kr = POOL_DIM, POOL_DIM + QL, POOL_DIM + QL + KVL
    PA = o_kr + LANE
    assert wa.shape[0] >= PA
    tm = _tile(T, 512, 16)

    def body(h_ref, wa_ref, wuq_ref, wukv_ref, gq_ref, gkv_ref, c_ref, sa_ref, sb_ref, *rest):
        xp_ref, ql_ref, kvl_ref, qn_ref, kvn_ref, q_ref, kv_ref, kr_ref = rest[-8:]
        proj = lax.dot_general(h_ref[...], wa_ref[...], _NT, preferred_element_type=F32)
        xp_ref[...] = proj[:, :POOL_DIM]
        ql = proj[:, o_q:o_kv]
        kvl = proj[:, o_kv:o_kr]
        ql_ref[...] = ql
        kvl_ref[...] = kvl
        qn = (ql * _rstd(ql) * gq_ref[...]).astype(BF16)
        kvn = (kvl * _rstd(kvl) * gkv_ref[...]).astype(BF16)
        qn_ref[...] = qn
        kvn_ref[...] = kvn
        c, sa, sb = c_ref[...], sa_ref[...], sb_ref[...]
        q = lax.dot_general(qn, wuq_ref[...], _NT, preferred_element_type=F32)
        for hh in range(H):
            base = hh * HEAD_W
            q_ref[:, base:base + QK_NOPE] = q[:, base:base + QK_NOPE].astype(BF16)
            q_ref[:, base + QK_NOPE:base + HEAD_W] = _rotate(
                q[:, base + QK_NOPE:base + HEAD_W], c, sa, sb, 1).astype(BF16)
        kv_ref[...] = jnp.dot(kvn, wukv_ref[...], preferred_element_type=F32).astype(BF16)
        kr_ref[...] = _rotate(proj[:, o_kr:o_kr + LANE], c, sa, sb, 1).astype(BF16)

    def row(w):
        return pl.BlockSpec((tm, w), lambda i: (i, 0))

    def whole(arr):
        return pl.BlockSpec(arr.shape, lambda i: (0,) * arr.ndim)

    gq2, gkv2 = gq.reshape(1, QL), gkv.reshape(1, KVL)
    outs = [(POOL_DIM, F32), (QL, F32), (KVL, F32), (QL, BF16), (KVL, BF16), (HQ, BF16), (HQ, BF16), (LANE, BF16)]
    return pl.pallas_call(
        body, name=name,
        out_shape=tuple(jax.ShapeDtypeStruct((T, w), dt) for w, dt in outs),
        grid=(T // tm,),
        in_specs=[row(D), pl.BlockSpec((PA, D), lambda i: (0, 0)), whole(wuq), whole(wukv), whole(gq2), whole(gkv2),
                  row(LANE), row(LANE), row(LANE)] + _dep_spec(dep),
        out_specs=tuple(row(w) for w, _ in outs),
        compiler_params=_params("parallel"),
    )(h, wa, wuq, wukv, gq2, gkv2, *tabs, *([] if dep is None else [dep]))


def _mixer_in_bwd(dq, dkv, dkr, ql, kvl, dxp, dgl, x, dxo, win_t, wuq, wukv, g_mix, gq, gkv, tabs, *, name):
    T, HQ = dq.shape
    D = x.shape[1]
    QL, KVL = wuq.shape[1], wukv.shape[0]
    H = HQ // HEAD_W
    PA = POOL_DIM + QL + KVL + LANE
    o_q, o_kv, o_kr = POOL_DIM, POOL_DIM + QL, POOL_DIM + QL + KVL
    tm = _tile(T, 256, 16)

    def norm_bwd(lat, gain, dn):
        r = _rstd(lat)
        xhat = lat * r
        dxh = dn * gain
        dlat = r * (dxh - xhat * jnp.mean(dxh * xhat, axis=-1, keepdims=True))
        return dlat, jnp.sum(dn * xhat, axis=0, keepdims=True)

    def body(dq_ref, dkv_ref, dkr_ref, ql_ref, kvl_ref, dxp_ref, dgl_ref, x_ref, dxo_ref, win_ref,
             wuq_ref, wukv_ref, gmix_ref, gq_ref, gkv_ref, c_ref, sa_ref, sb_ref,
             dproj_ref, dqp_ref, dgq_ref, dgkv_ref, dx_ref, dgmix_ref):
        c, sa, sb = c_ref[...], sa_ref[...], sb_ref[...]
        for hh in range(H):
            base = hh * HEAD_W
            dqp_ref[:, base:base + QK_NOPE] = dq_ref[:, base:base + QK_NOPE]
            dqp_ref[:, base + QK_NOPE:base + HEAD_W] = _rotate(
                dq_ref[:, base + QK_NOPE:base + HEAD_W].astype(F32), c, sa, sb, -1).astype(BF16)
        dqn = jnp.dot(dqp_ref[...], wuq_ref[...], preferred_element_type=F32)
        dkvn = lax.dot_general(dkv_ref[...], wukv_ref[...], _NT, preferred_element_type=F32)
        dql, dgq = norm_bwd(ql_ref[...], gq_ref[...], dqn)
        dkvl, dgkv = norm_bwd(kvl_ref[...], gkv_ref[...], dkvn)
        dproj_ref[:, :POOL_DIM] = dxp_ref[...].astype(BF16)
        dproj_ref[:, o_q:o_kv] = dql.astype(BF16)
        dproj_ref[:, o_kv:o_kr] = dkvl.astype(BF16)
        dproj_ref[:, o_kr:PA] = _rotate(dkr_ref[...], c, sa, sb, -1).astype(BF16)

        n_gate = win_ref.shape[0] - 2 * D
        dh = (jnp.dot(dproj_ref[...], win_ref[:PA, :], preferred_element_type=F32)
              + jnp.dot(dgl_ref[...], win_ref[n_gate:, :], preferred_element_type=F32))
        x = x_ref[...]
        r = _rstd(x)
        xhat = x * r
        dxh = dh * gmix_ref[...]
        dx_ref[...] = dxo_ref[...] + r * (dxh - xhat * jnp.mean(dxh * xhat, axis=-1, keepdims=True))
        dgmix = jnp.sum(dh * xhat, axis=0, keepdims=True)

        @pl.when(pl.program_id(0) == 0)
        def _():
            dgq_ref[...] = dgq
            dgkv_ref[...] = dgkv
            dgmix_ref[...] = dgmix

        @pl.when(pl.program_id(0) > 0)
        def _():
            dgq_ref[...] += dgq
            dgkv_ref[...] += dgkv
            dgmix_ref[...] += dgmix

    def row(w):
        return pl.BlockSpec((tm, w), lambda i: (i, 0))

    def resident(arr):
        return pl.BlockSpec(arr.shape, lambda i: (0, 0), pipeline_mode=pl.Buffered(1))

    gmix2, gq2, gkv2 = g_mix.reshape(1, D), gq.reshape(1, QL), gkv.reshape(1, KVL)
    vec = pl.BlockSpec((1, D), lambda i: (0, 0))
    vq, vkv = pl.BlockSpec((1, QL), lambda i: (0, 0)), pl.BlockSpec((1, KVL), lambda i: (0, 0))
    return pl.pallas_call(
        body, name=name,
        out_shape=(jax.ShapeDtypeStruct((T, PA), BF16), jax.ShapeDtypeStruct((T, HQ), BF16),
                   jax.ShapeDtypeStruct((1, QL), F32), jax.ShapeDtypeStruct((1, KVL), F32),
                   jax.ShapeDtypeStruct((T, D), F32), jax.ShapeDtypeStruct((1, D), F32)),
        grid=(T // tm,),
        in_specs=[row(HQ), row(HQ), row(LANE), row(QL), row(KVL), row(POOL_DIM), row(2 * D), row(D), row(D),
                  resident(win_t), resident(wuq), resident(wukv),
                  vec, vq, vkv, row(LANE), row(LANE), row(LANE)],
        out_specs=(row(PA), row(HQ), vq, vkv, row(D), vec),
        compiler_params=_params("arbitrary"),
    )(dq, dkv, dkr, ql, kvl, dxp, dgl, x, dxo, win_t, wuq, wukv, gmix2, gq2, gkv2, *tabs)


def _pool_groups(x_of, S):
    row = lax.broadcasted_iota(jnp.int32, (S, POOL_G), 0)
    for g, w in enumerate(POOL_WINDOWS):
        x = x_of(g)
        s = x
        d = 1
        while d < w:
            s = s + jnp.where(row >= d, pltpu.roll(s, d, 0), 0.0)
            d *= 2
        cnt = jnp.minimum(row + 1, w).astype(F32)
        yield g, w, x, s / cnt - x, cnt, row


def _pool_fwd(xp, maps, scale, *, S, name):
    T = xp.shape[0]

    def body(xp_ref, maps_ref, scale_ref, ms_ref):
        for g, _, _, pooled, _, _ in _pool_groups(lambda g: xp_ref[:, g * POOL_G:(g + 1) * POOL_G], S):
            mixed = jnp.dot(pooled.astype(BF16), maps_ref[g].astype(BF16), preferred_element_type=F32)
            ms_ref[:, g * POOL_G:(g + 1) * POOL_G] = (mixed * scale_ref[:, g * POOL_G:(g + 1) * POOL_G]).astype(BF16)

    return pl.pallas_call(
        body, name=name,
        out_shape=jax.ShapeDtypeStruct((T, POOL_DIM), BF16),
        grid=(T // S,),
        in_specs=[pl.BlockSpec((S, POOL_DIM), lambda b: (b, 0)),
                  pl.BlockSpec(maps.shape, lambda b: (0, 0, 0)),
                  pl.BlockSpec((1, POOL_DIM), lambda b: (0, 0))],
        out_specs=pl.BlockSpec((S, POOL_DIM), lambda b: (b, 0)),
        compiler_params=_params("parallel"),
    )(xp, maps, scale.reshape(1, POOL_DIM))


def _pool_bwd(xp, dms, maps, scale, *, S, name):
    T = xp.shape[0]

    def body(xp_ref, dms_ref, maps_ref, scale_ref, dxp_ref, dmaps_ref, dscale_ref):
        first = pl.program_id(0) == 0
        for g, w, _, pooled, cnt, row in _pool_groups(lambda g: xp_ref[:, g * POOL_G:(g + 1) * POOL_G], S):
            cols = slice(g * POOL_G, (g + 1) * POOL_G)
            pooled_b = pooled.astype(BF16)
            maps_b = maps_ref[g].astype(BF16)
            mixed = jnp.dot(pooled_b, maps_b, preferred_element_type=F32)
            dms = dms_ref[:, cols]
            dscale = jnp.sum(dms * mixed, axis=0, keepdims=True)
            dmixed = (dms * scale_ref[:, cols]).astype(BF16)
            dmaps = lax.dot_general(pooled_b, dmixed, (((0,), (0,)), ((), ())), preferred_element_type=F32)
            dpooled = lax.dot_general(dmixed, maps_b, (((1,), (1,)), ((), ())), preferred_element_type=F32)
            z = dpooled / cnt
            d = 1
            while d < w:
                z = z + jnp.where(row < S - d, pltpu.roll(z, S - d, 0), 0.0)
                d *= 2
            dxp_ref[:, cols] = z - dpooled

            @pl.when(first)
            def _():
                dmaps_ref[g] = dmaps
                dscale_ref[:, cols] = dscale

            @pl.when(jnp.logical_not(first))
            def _():
                dmaps_ref[g] += dmaps
                dscale_ref[:, cols] += dscale

    seq = pl.BlockSpec((S, POOL_DIM), lambda b: (b, 0))
    maps_spec = pl.BlockSpec(maps.shape, lambda b: (0, 0, 0))
    vec = pl.BlockSpec((1, POOL_DIM), lambda b: (0, 0))
    return pl.pallas_call(
        body, name=name,
        out_shape=(jax.ShapeDtypeStruct((T, POOL_DIM), F32), jax.ShapeDtypeStruct(maps.shape, F32),
                   jax.ShapeDtypeStruct((1, POOL_DIM), F32)),
        grid=(T // S,),
        in_specs=[seq, seq, maps_spec, vec],
        out_specs=(seq, maps_spec, vec),
        compiler_params=_params("arbitrary"),
    )(xp, dms, maps, scale.reshape(1, POOL_DIM))


def _causal_mask(s, t):
    r = lax.broadcasted_iota(jnp.int32, (t, t), 0)
    c = lax.broadcasted_iota(jnp.int32, (t, t), 1)
    return jnp.where(r >= c, s, MASK_VALUE)


_NT = (((1,), (1,)), ((), ()))
_TN = (((0,), (0,)), ((), ()))


def _attn_fwd(q, kv, kr, *, S, name):
    T, HQ = q.shape
    H = HQ // HEAD_W
    B = T // S
    t = _tile(S, ATTN_TILE)
    n = S // t

    def body(q_ref, k_ref, v_ref, kr_ref, o_ref, lse_ref, kcat, vcat):
        kcat[:, :QK_NOPE] = k_ref[...]
        kcat[:, QK_NOPE:] = kr_ref[...]
        vcat[:, :V_DIM] = v_ref[...]
        vcat[:, V_DIM:] = jnp.ones((S, HEAD_W - V_DIM), BF16)
        for i in range(n):
            rows = slice(i * t, (i + 1) * t)
            qt = q_ref[rows, :]
            m = jnp.full((t, 1), MASK_VALUE, F32)
            acc = jnp.zeros((t, HEAD_W), F32)
            for j in range(i + 1):
                cols = slice(j * t, (j + 1) * t)
                s = lax.dot_general(qt, kcat[cols, :], _NT, preferred_element_type=F32) * ATTN_SCALE_LOG2
                if j == i:
                    s = _causal_mask(s, t)
                m_new = jnp.maximum(m, jnp.max(s, axis=1, keepdims=True))
                p = jnp.exp2(s - m_new)
                acc = jnp.exp2(m - m_new) * acc + jnp.dot(p.astype(BF16), vcat[cols, :], preferred_element_type=F32)
                m = m_new
            l = acc[:, V_DIM:V_DIM + 1]
            o_ref[rows, :] = (acc[:, :V_DIM] / l).astype(BF16)
            lse_ref[rows, :] = jnp.broadcast_to(m + jnp.log2(l), (t, LANE))

    seq_h = pl.BlockSpec((S, LANE), lambda b, h: (b, h))
    return pl.pallas_call(
        body, name=name,
        out_shape=(jax.ShapeDtypeStruct((T, H * V_DIM), BF16), jax.ShapeDtypeStruct((T, H * LANE), F32)),
        grid=(B, H),
        in_specs=[pl.BlockSpec((S, HEAD_W), lambda b, h: (b, h)),
                  pl.BlockSpec((S, QK_NOPE), lambda b, h: (b, 2 * h)),
                  pl.BlockSpec((S, V_DIM), lambda b, h: (b, 2 * h + 1)),
                  pl.BlockSpec((S, LANE), lambda b, h: (b, 0))],
        out_specs=(seq_h, seq_h),
        scratch_shapes=[pltpu.VMEM((S, HEAD_W), BF16), pltpu.VMEM((S, HEAD_W), BF16)],
        compiler_params=_params("parallel", "parallel"),
    )(q, kv, kv, kr)


def _attn_bwd(q, kv, kr, o, do, lse, *, S, name):
    T, HQ = q.shape
    H = HQ // HEAD_W
    B = T // S
    t = _tile(S, ATTN_TILE)
    n = S // t

    def body(q_ref, k_ref, v_ref, kr_ref, o_ref, do_ref, lse_ref, dq_ref, dkv_ref, dkr_ref, kcat, dq_acc):
        @pl.when(pl.program_id(1) == 0)
        def _():
            dkr_ref[...] = jnp.zeros_like(dkr_ref)

        kcat[:, :QK_NOPE] = k_ref[...]
        kcat[:, QK_NOPE:] = kr_ref[...]
        delta = [jnp.sum(do_ref[i * t:(i + 1) * t, :].astype(F32) * o_ref[i * t:(i + 1) * t, :].astype(F32),
                         axis=1, keepdims=True) for i in range(n)]
        for j in range(n):
            cols = slice(j * t, (j + 1) * t)
            kc = kcat[cols, :]
            vt = v_ref[cols, :]
            dk = jnp.zeros((t, HEAD_W), F32)
            dv = jnp.zeros((t, V_DIM), F32)
            for i in range(j, n):
                rows = slice(i * t, (i + 1) * t)
                qt = q_ref[rows, :]
                dot_ = do_ref[rows, :]
                s = lax.dot_general(qt, kc, _NT, preferred_element_type=F32) * ATTN_SCALE_LOG2
                if i == j:
                    s = _causal_mask(s, t)
                p = jnp.exp2(s - lse_ref[rows, :][:, :1])
                dv = dv + lax.dot_general(p.astype(BF16), dot_, _TN, preferred_element_type=F32)
                dp = lax.dot_general(dot_, vt, _NT, preferred_element_type=F32)
                ds = (p * (dp - delta[i]) * ATTN_SCALE).astype(BF16)
                dk = dk + lax.dot_general(ds, qt, _TN, preferred_element_type=F32)
                dq_part = jnp.dot(ds, kc, preferred_element_type=F32)
                if j == 0:
                    dq_acc[rows, :] = dq_part
                else:
                    dq_acc[rows, :] += dq_part
            dkv_ref[cols, :QK_NOPE] = dk[:, :QK_NOPE].astype(BF16)
            dkv_ref[cols, QK_NOPE:] = dv.astype(BF16)
            dkr_ref[cols, :] += dk[:, QK_NOPE:]
        dq_ref[...] = dq_acc[...].astype(BF16)

    seq_q = pl.BlockSpec((S, HEAD_W), lambda b, h: (b, h))
    seq_h = pl.BlockSpec((S, LANE), lambda b, h: (b, h))
    seq_shared = pl.BlockSpec((S, LANE), lambda b, h: (b, 0))
    return pl.pallas_call(
        body, name=name,
        out_shape=(jax.ShapeDtypeStruct((T, HQ), BF16), jax.ShapeDtypeStruct((T, HQ), BF16),
                   jax.ShapeDtypeStruct((T, LANE), F32)),
        grid=(B, H),
        in_specs=[seq_q,
                  pl.BlockSpec((S, QK_NOPE), lambda b, h: (b, 2 * h)),
                  pl.BlockSpec((S, V_DIM), lambda b, h: (b, 2 * h + 1)),
                  seq_shared, seq_h, seq_h, seq_h],
        out_specs=(seq_q, seq_q, seq_shared),
        scratch_shapes=[pltpu.VMEM((S, HEAD_W), BF16), pltpu.VMEM((S, HEAD_W), F32)],
        compiler_params=_params("parallel", "arbitrary"),
    )(q, kv, kv, kr, o, do, lse)


def _merge_out(h, ms, o, x, win_t, bgate, wpp, wap, wout, next_gain, *, name):
    T, D = x.shape
    tm = _tile(T, 256, 16)
    n_gate = win_t.shape[0] - 2 * D

    def body(h_ref, ms_ref, o_ref, x_ref, win_ref, bgate_ref, wpp_ref, wap_ref, wout_ref, ng_ref,
             gates_ref, ba_ref, bb_ref, merged_ref, xn_ref, hn_ref):
        logits = lax.dot_general(h_ref[...], win_ref[n_gate:, :], _NT, preferred_element_type=F32) + bgate_ref[...]
        gates = jax.nn.sigmoid(logits)
        ba = jnp.dot(ms_ref[...], wpp_ref[...], preferred_element_type=F32)
        bb = jnp.dot(o_ref[...], wap_ref[...], preferred_element_type=F32)
        merged = (gates[:, :D] * ba + gates[:, D:] * bb).astype(BF16)
        gates_ref[...] = gates.astype(BF16)
        ba_ref[...] = ba.astype(BF16)
        bb_ref[...] = bb.astype(BF16)
        merged_ref[...] = merged
        xn = x_ref[...] + jnp.dot(merged, wout_ref[...], preferred_element_type=F32)
        xn_ref[...] = xn
        hn_ref[...] = (xn * _rstd(xn) * ng_ref[...]).astype(BF16)

    def row(w):
        return pl.BlockSpec((tm, w), lambda i: (i, 0))

    def whole(arr):
        return pl.BlockSpec(arr.shape, lambda i: (0,) * arr.ndim, pipeline_mode=pl.Buffered(1))

    bg2, ng2 = bgate.reshape(1, 2 * D), next_gain.reshape(1, D)
    act = jax.ShapeDtypeStruct((T, D), BF16)
    return pl.pallas_call(
        body, name=name,
        out_shape=(jax.ShapeDtypeStruct((T, 2 * D), BF16), act, act, act, jax.ShapeDtypeStruct((T, D), F32), act),
        grid=(T // tm,),
        in_specs=[row(D), row(ms.shape[1]), row(o.shape[1]), row(D), whole(win_t), whole(bg2), whole(wpp),
                  whole(wap), whole(wout), whole(ng2)],
        out_specs=(row(2 * D), row(D), row(D), row(D), row(D), row(D)),
        compiler_params=_params("parallel"),
    )(h, ms, o, x, win_t, bg2, wpp, wap, wout, ng2)


def _merge_bwd(dxo, wout, wpp, wap, gates, ba, bb, *, name, dep=None):
    T, D = dxo.shape
    tm = _tile(T, 512, 16)

    def body(dxo_ref, wout_ref, wpp_ref, wap_ref, gates_ref, ba_ref, bb_ref, *rest):
        dba_ref, dbb_ref, dgl_ref, dbg_ref, dms_ref, do_ref = rest[-6:]
        dm = lax.dot_general(dxo_ref[...].astype(BF16), wout_ref[...], _NT, preferred_element_type=F32)
        ga = gates_ref[:, :D].astype(F32)
        gb = gates_ref[:, D:].astype(F32)
        dba = (dm * ga).astype(BF16)
        dbb = (dm * gb).astype(BF16)
        dba_ref[...] = dba
        dbb_ref[...] = dbb
        dms_ref[...] = lax.dot_general(dba, wpp_ref[...], _NT, preferred_element_type=F32)
        do_ref[...] = lax.dot_general(dbb, wap_ref[...], _NT, preferred_element_type=F32).astype(BF16)
        dgl_a = dm * ba_ref[...].astype(F32) * (ga * (1.0 - ga))
        dgl_b = dm * bb_ref[...].astype(F32) * (gb * (1.0 - gb))
        dgl_ref[:, :D] = dgl_a.astype(BF16)
        dgl_ref[:, D:] = dgl_b.astype(BF16)
        sa = jnp.sum(dgl_a, axis=0, keepdims=True)
        sb = jnp.sum(dgl_b, axis=0, keepdims=True)

        @pl.when(pl.program_id(0) == 0)
        def _():
            dbg_ref[:, :D] = sa
            dbg_ref[:, D:] = sb

        @pl.when(pl.program_id(0) > 0)
        def _():
            dbg_ref[:, :D] += sa
            dbg_ref[:, D:] += sb

    def row(w):
        return pl.BlockSpec((tm, w), lambda i: (i, 0))

    def whole(arr):
        return pl.BlockSpec(arr.shape, lambda i: (0, 0))

    P, HV = wpp.shape[0], wap.shape[0]
    act = jax.ShapeDtypeStruct((T, D), BF16)
    return pl.pallas_call(
        body, name=name,
        out_shape=(act, act, jax.ShapeDtypeStruct((T, 2 * D), BF16), jax.ShapeDtypeStruct((1, 2 * D), F32),
                   jax.ShapeDtypeStruct((T, P), F32), jax.ShapeDtypeStruct((T, HV), BF16)),
        grid=(T // tm,),
        in_specs=[row(D), whole(wout), whole(wpp), whole(wap), row(2 * D), row(D), row(D)] + _dep_spec(dep),
        out_specs=(row(D), row(D), row(2 * D), pl.BlockSpec((1, 2 * D), lambda i: (0, 0)), row(P), row(HV)),
        compiler_params=_params("arbitrary"),
    )(dxo, wout, wpp, wap, gates, ba, bb, *([] if dep is None else [dep]))


def _ffn_fwd(x, h, w, tag, next_gain, dep=None):
    gate, up, a, xn, hn = _ffn_fwd_core(x, h, w["up_t"], w["wd"], next_gain, alpha=0.5,
                                        name=f"{tag}_fwd" if next_gain is not None else f"{tag}_fwd_last", dep=dep)
    return xn, hn, (x, h, gate, up, a)


def _ffn_bwd(dxo, gain, w, saved, tag, dep=None, early=None, mid=None):
    x, h, gate, up, a = saved
    dwd = _mm(a, dxo, ta=True, alpha=0.5, out_dtype=BF16, name=f"{tag}_dwd", tm=1408, tn=1024, tk=1024, dep=dep)
    if early is not None:
        dep = early(dwd)
    dgate, dup, dx, dgain = _ffn_bwd_core(dxo, w["wd"], w["up_t"], gate, up, x, gain, alpha=0.5,
                                          name=f"{tag}_bwd_core", dep=dep)
    dup_t = _ffn_dw_up(dgate, dup, h, name=f"{tag}_dw_up", dep=None if mid is None else mid(dgain))
    return dx, dgain, dup_t, dwd


def _mixer_fwd(x, h, p, w, tabs, S, next_gain, dep=None):
    xp, ql, kvl, qn, kvn, q, kv, kr = _mixer_in(h, w["win_t"], w["wuq_t"], w["wukv"], p["q_latent_norm"],
                                                 p["kv_latent_norm"], tabs, name="mix_in", dep=dep)
    ms = _pool_fwd(xp, p["pool_maps"], p["pool_scale"], S=S, name="pool_fwd")
    o, lse = _attn_fwd(q, kv, kr, S=S, name="attn_fwd")
    gates, ba, bb, merged, xn, hn = _merge_out(h, ms, o, x, w["win_t"], p["b_gate"], w["wpp"], w["wap"], w["wout"],
                                               next_gain, name="merge_out")
    return xn, hn, (x, h, xp, ql, kvl, qn, kvn, q, kv, kr, ms, o, lse, gates, ba, bb, merged)


def _mixer_bwd(dxo, p, w, tabs, saved, S, dep=None):
    x, h, xp, ql, kvl, qn, kvn, q, kv, kr, ms, o, lse, gates, ba, bb, merged = saved
    dba, dbb, dgl, dbg, dms, do = _merge_bwd(dxo, w["wout"], w["wpp"], w["wap"], gates, ba, bb, name="merge_bwd",
                                             dep=dep)
    g = {}
    g["wout"], g["wpp"], g["wap"] = _dw_multi([(merged, dxo), (ms, dba), (o, dbb)], name="d_w_merge")
    dxp, g["pool_maps"], g["pool_scale"] = _pool_bwd(xp, dms, p["pool_maps"], p["pool_scale"], S=S, name="pool_bwd")
    dq, dkv, dkr = _attn_bwd(q, kv, kr, o, do, lse, S=S, name="attn_bwd")
    dproj, dqp, g["q_latent_norm"], g["kv_latent_norm"], dx, g["norm_mix"] = _mixer_in_bwd(
        dq, dkv, dkr, ql, kvl, dxp, dgl, x, dxo, w["win_t"], w["wuq_t"], w["wukv"],
        p["norm_mix"], p["q_latent_norm"], p["kv_latent_norm"], tabs, name="mix_in_bwd")
    g["wuq_t"], g["wukv"] = _dw_multi([(dqp, qn), (kvn, dkv)], name="d_w_qkv", tk=1024)
    g["wa_t"], g["wgate_t"] = _dw_multi([(dproj, h), (dgl, h)], name="d_w_in")
    g["b_gate"] = dbg
    return dx, g


BIG = ("ffn1_up", "ffn1_down", "w_in", "w_pool_proj", "w_uq", "w_ukv", "w_attn_proj", "w_out", "ffn2_up", "ffn2_down")
SMALL = ("norm_ffn1", "norm_mix", "b_gate", "pool_maps", "pool_scale", "q_latent_norm", "kv_latent_norm", "norm_ffn2")
PACKED = ("w_pool_proj", "w_uq", "w_ukv")
TRANSPOSED = ("ffn1_up", "ffn2_up", "w_in", "w_uq")
COL_SHARDED = ("w_pool_proj", "w_ukv")
QK_HEAD = QK_NOPE + QK_ROPE


def _rows(stacked):
    n, r, c = stacked.shape
    return stacked.reshape(n * r, c)


def _cols(stacked):
    n, k, c = stacked.shape
    return stacked.transpose(1, 0, 2).reshape(k, n * c)


FFN1_PART = ("ffn1_up", "ffn1_down")
MIXER_PART = ("w_in", "w_attn_proj", "w_out") + PACKED
FFN2_PART = ("ffn2_up", "ffn2_down")


def _kernel_weights(stacked):
    full = {}
    for tag in ("ffn1", "ffn2"):
        if tag + "_up" in stacked:
            full[tag] = {"up_t": _rows(stacked[tag + "_up"]), "wd": _rows(stacked[tag + "_down"])}
    if "w_in" in stacked:
        win_t = _rows(stacked["w_in"])
        wuq_t = _rows(stacked["w_uq"])
        QL = wuq_t.shape[1]
        H = wuq_t.shape[0] // QK_HEAD
        wuq_t = jnp.pad(wuq_t.reshape(H, QK_HEAD, QL), ((0, 0), (0, HEAD_W - QK_HEAD), (0, 0)))
        full.update({"win_t": win_t, "wuq_t": wuq_t.reshape(H * HEAD_W, QL),
                     "wukv": _cols(stacked["w_ukv"]), "wpp": _cols(stacked["w_pool_proj"]),
                     "wap": _rows(stacked["w_attn_proj"]), "wout": _rows(stacked["w_out"])})
    return full


def _split_rows(full):
    return full.reshape(N_DEV, full.shape[0] // N_DEV, full.shape[1])


def _split_cols(full):
    k, cols = full.shape
    return full.reshape(k, N_DEV, cols // N_DEV).transpose(1, 0, 2)


def _mixer_grads_stacked(g):
    n_a = g["wa_t"].shape[0] - (LANE - QK_ROPE)
    HQ, QL = g["wuq_t"].shape
    H = HQ // HEAD_W
    wuq_t = g["wuq_t"].reshape(H, HEAD_W, QL)[:, :QK_HEAD].reshape(H * QK_HEAD, QL)
    return {"w_in": _split_rows(jnp.concatenate([g["wa_t"][:n_a], g["wgate_t"]], axis=0)),
            "w_uq": _split_rows(wuq_t),
            "w_pool_proj": _split_cols(g["wpp"]), "w_ukv": _split_cols(g["wukv"]),
            "w_attn_proj": _split_rows(g["wap"]), "w_out": _split_rows(g["wout"])}


def _mesh_place():
    x, y, c = lax.axis_index("x"), lax.axis_index("y"), lax.axis_index("c")
    chips = [(1 - x, y), (x, 1 - y), (1 - x, 1 - y)]
    return x, y, c, chips


HBM = pl.BlockSpec(memory_space=pltpu.HBM)
SEMAPHORES = pl.BlockSpec(memory_space=pltpu.SEMAPHORE)
DATAFLOW = pltpu.SideEffectType.DATAFLOW_SIDE_EFFECTING
GATHER_PEERS = 4
SCATTER_PEERS = 7


def _in_hbm(a):
    return pltpu.with_memory_space_constraint(a, pltpu.HBM)


def _gather_plan(src_refs, land_refs):
    x, y, c, chips = _mesh_place()
    me = 4 * x + 2 * y + c
    targets = [(x, y, 1 - c)] + [(cx, cy, c) for cx, cy in chips]
    return [(s, land.at[me], to) for s, land in zip(src_refs, land_refs) for to in targets]


def _scatter_plan(src_refs, land_refs):
    x, y, c, _ = _mesh_place()
    peers = [(x, y, 1 - c), (1 - x, y, c), (x, 1 - y, c), (1 - x, 1 - y, c),
             (1 - x, y, 1 - c), (x, 1 - y, 1 - c), (1 - x, 1 - y, 1 - c)]
    return [(s.at[4 * px + 2 * py + pc], land.at[k], (px, py, pc))
            for s, land in zip(src_refs, land_refs) for k, (px, py, pc) in enumerate(peers)]


def _descriptors(plan, src_refs, land_refs, send_sems, recv_sems):
    return [pltpu.make_async_remote_copy(src_ref=s, dst_ref=d, send_sem=send_sems.at[k], recv_sem=recv_sems.at[k],
                                         device_id=to, device_id_type=MESH)
            for k, (s, d, to) in enumerate(plan(src_refs, land_refs))]


def _exchange(srcs, land_shapes, plan, per_src, *, name):
    n = len(srcs)

    def body(*refs):
        copies = _descriptors(plan, refs[:n], refs[n:2 * n], refs[2 * n], refs[2 * n + 1])
        for cp in copies:
            cp.start()
        for cp in copies:
            cp.wait()

    return pl.pallas_call(
        body, name=name,
        out_shape=tuple(jax.ShapeDtypeStruct(shape, s.dtype) for shape, s in zip(land_shapes, srcs)),
        in_specs=[ANY] * n, out_specs=(ANY,) * n,
        scratch_shapes=[pltpu.SemaphoreType.DMA((per_src * n,)), pltpu.SemaphoreType.DMA((per_src * n,))],
    )(*srcs)


FORWARD_COPIES = 4


def _forward_slots():
    x, y, c, chips = _mesh_place()
    return [4 * cx + 2 * cy + c for cx, cy in chips] + [4 * x + 2 * y + (1 - c)], (x, y, 1 - c)


def _forward_plan(src_refs, land_refs):
    slots, sibling = _forward_slots()
    return [(land.at[s], land.at[s], sibling) for land in land_refs for s in slots]


def _gather_all_plan(src_refs, land_refs):
    x, y, c, _ = _mesh_place()
    me = 4 * x + 2 * y + c
    peers = [(x, y, 1 - c), (1 - x, y, c), (x, 1 - y, c), (1 - x, 1 - y, c),
             (1 - x, y, 1 - c), (x, 1 - y, 1 - c), (1 - x, 1 - y, 1 - c)]
    return [(s, land.at[me], to) for s, land in zip(src_refs, land_refs) for to in peers]


def _exchange_start(srcs, lands, plan, n_copies, *, name):
    ns, n = len(srcs), len(srcs) + len(lands)

    def body(*refs):
        for cp in _descriptors(plan, refs[:ns], refs[ns:n], refs[n], refs[n + 1]):
            cp.start()
        refs[-1][...] = jnp.zeros_like(refs[-1])

    sems = pltpu.SemaphoreType.DMA((n_copies,))
    out = pl.pallas_call(
        body, name=name,
        out_shape=(sems, sems, *[pltpu.HBM(a.shape, a.dtype) for a in srcs + lands],
                   jax.ShapeDtypeStruct((8, LANE), F32)),
        in_specs=(HBM,) * n,
        out_specs=(SEMAPHORES, SEMAPHORES, *[HBM] * n, pl.BlockSpec(memory_space=pltpu.VMEM)),
        input_output_aliases={i: 2 + i for i in range(n)},
        compiler_params=pltpu.CompilerParams(has_side_effects=DATAFLOW),
    )(*[_in_hbm(a) for a in srcs + lands])
    return out[0], out[1], list(out[2:2 + ns]), list(out[2 + ns:2 + n]), out[-1]


def _exchange_wait(send_sems, recv_sems, srcs, lands, plan, after, *, name):
    ns, n = len(srcs), len(srcs) + len(lands)

    def body(*refs):
        for cp in _descriptors(plan, refs[:ns], refs[ns:n], refs[n], refs[n + 1]):
            cp.wait_send()
            cp.wait_recv()

    out = pl.pallas_call(
        body, name=name,
        out_shape=tuple(pltpu.HBM(a.shape, a.dtype) for a in srcs + lands),
        in_specs=(*[HBM] * n, SEMAPHORES, SEMAPHORES, ANY),
        out_specs=(HBM,) * n,
        input_output_aliases={i: i for i in range(n)},
        compiler_params=pltpu.CompilerParams(has_side_effects=DATAFLOW),
    )(*srcs, *lands, send_sems, recv_sems, after)
    return list(out[:ns]), list(out[ns:])


def _gather_forward(lands, *, name):
    n = len(lands)

    def body(*refs):
        in_refs, out_refs = refs[:n], refs[n:2 * n]
        token, send_sems, recv_sems = refs[2 * n:2 * n + 3]
        slots, sibling = _forward_slots()
        passed = [pltpu.make_async_remote_copy(
            src_ref=i.at[s], dst_ref=o.at[s],
            send_sem=send_sems.at[FORWARD_COPIES * b + j], recv_sem=recv_sems.at[FORWARD_COPIES * b + j],
            device_id=sibling, device_id_type=MESH)
            for b, (i, o) in enumerate(zip(in_refs, out_refs)) for j, s in enumerate(slots)]
        for cp in passed:
            cp.start()
        for cp in passed:
            cp.wait()
        token[...] = jnp.zeros_like(token)

    out = pl.pallas_call(
        body, name=name,
        out_shape=(*[jax.ShapeDtypeStruct(a.shape, a.dtype) for a in lands], jax.ShapeDtypeStruct((8, LANE), F32)),
        in_specs=[ANY] * n,
        out_specs=(*[ANY] * n, pl.BlockSpec(memory_space=pltpu.VMEM)),
        input_output_aliases={i: i for i in range(n)},
        scratch_shapes=[pltpu.SemaphoreType.DMA((FORWARD_COPIES * n,)), pltpu.SemaphoreType.DMA((FORWARD_COPIES * n,))],
    )(*lands)
    return list(out[:n]), out[n]


def _scatter_sum(parts, got, me, into, layer, *, name):
    shard = parts.shape[1:]
    cols = shard[-1]
    rows = int(np.prod(shard[:-1]))
    tr = _tile(rows, 1024, 16)
    layers = into.shape[0]

    def body(me_ref, p_ref, g_ref, into_ref, o_ref):
        acc = p_ref[...].astype(F32)
        for k in range(SCATTER_PEERS):
            acc = acc + g_ref[k].astype(F32)
        o_ref[...] = acc

    slab = pl.BlockSpec((None, tr, cols), lambda r, me_ref: (layer, r, 0))
    out = pl.pallas_call(
        body, name=name,
        out_shape=jax.ShapeDtypeStruct((layers, rows, cols), F32),
        grid_spec=pltpu.PrefetchScalarGridSpec(
            num_scalar_prefetch=1, grid=(rows // tr,),
            in_specs=[pl.BlockSpec((None, tr, cols), lambda r, me_ref: (me_ref[0], r, 0)),
                      pl.BlockSpec((SCATTER_PEERS, tr, cols), lambda r, me_ref: (0, r, 0)),
                      ANY],
            out_specs=slab),
        input_output_aliases={3: 0},
        compiler_params=_params("parallel"),
    )(me, parts.reshape(N_DEV, rows, cols), got.reshape(SCATTER_PEERS, rows, cols), into.reshape(layers, rows, cols))
    return out.reshape(layers, *shard)


def _sum_devices(parts, *, name):
    _, R, C = parts.shape
    tr = _tile(R, 512, 8)

    def body(p_ref, o_ref):
        acc = p_ref[0]
        for d in range(1, N_DEV):
            acc = acc + p_ref[d]
        o_ref[...] = acc

    return pl.pallas_call(
        body, name=name,
        out_shape=jax.ShapeDtypeStruct((R, C), F32),
        grid=(R // tr,),
        in_specs=[pl.BlockSpec((N_DEV, tr, C), lambda r: (0, r, 0))],
        out_specs=pl.BlockSpec((tr, C), lambda r: (r, 0)),
        compiler_params=_params("parallel"),
    )(parts)


def _adamw(w, g, m, v, *, name, dep=None):
    shape = w.shape
    cols = shape[-1]
    rows = w.size // cols
    tr = _tile(rows, 512, 8)

    def body(w_ref, g_ref, m_ref, v_ref, *rest):
        d_ref, nm_ref, nv_ref = rest[-3:]
        g = g_ref[...]
        m = ADAM_B1 * m_ref[...] + (1.0 - ADAM_B1) * g
        v = ADAM_B2 * v_ref[...] + (1.0 - ADAM_B2) * jnp.square(g)
        m_hat = m / (1.0 - ADAM_B1 ** ADAM_STEP)
        v_hat = v / (1.0 - ADAM_B2 ** ADAM_STEP)
        d_ref[...] = -ADAM_LR * (m_hat / (jnp.sqrt(v_hat) + ADAM_EPS) + ADAM_WD * w_ref[...])
        nm_ref[...] = m
        nv_ref[...] = v

    spec = pl.BlockSpec((tr, cols), lambda i: (i, 0))
    out = jax.ShapeDtypeStruct((rows, cols), F32)
    d, nm, nv = pl.pallas_call(
        body, name=name,
        out_shape=(out, out, out),
        grid=(rows // tr,),
        in_specs=[spec] * 4 + _dep_spec(dep), out_specs=(spec,) * 3,
        compiler_params=_params("parallel"),
    )(*(a.reshape(rows, cols) for a in (w, g, m, v)), *([] if dep is None else [dep]))
    return d.reshape(shape), nm.reshape(shape), nv.reshape(shape)


PACK_ALIGN = 16 * LANE


def _pack(pieces, lead):
    out = []
    for p in pieces:
        keep = p.shape[:lead]
        flat = p.reshape(*keep, -1)
        pad = (-flat.shape[-1]) % PACK_ALIGN
        if pad:
            flat = jnp.pad(flat, [(0, 0)] * lead + [(0, pad)])
        out.append(flat.reshape(*keep, -1, LANE))
    return jnp.concatenate(out, axis=lead)


def _unpack(buf, shapes, lead):
    keep = buf.shape[:lead]
    out, row = [], 0
    for shape in shapes:
        size = int(np.prod(shape))
        rows = -(-size // PACK_ALIGN) * (PACK_ALIGN // LANE)
        piece = lax.slice_in_dim(buf, row, row + rows, axis=lead).reshape(*keep, rows * LANE)
        out.append(lax.slice_in_dim(piece, 0, size, axis=lead).reshape(*keep, *shape))
        row += rows
    return out


def kernel(x, positions, norm_ffn1, ffn1_up, ffn1_down, norm_mix, w_in, b_gate, pool_maps, pool_scale, w_pool_proj, q_latent_norm, w_uq, kv_latent_norm, w_ukv, w_attn_proj, w_out, norm_ffn2, ffn2_up, ffn2_down, final_norm, loss_target, m_norm_ffn1, m_ffn1_up, m_ffn1_down, m_norm_mix, m_w_in, m_b_gate, m_pool_maps, m_pool_scale, m_w_pool_proj, m_q_latent_norm, m_w_uq, m_kv_latent_norm, m_w_ukv, m_w_attn_proj, m_w_out, m_norm_ffn2, m_ffn2_up, m_ffn2_down, m_final_norm, v_norm_ffn1, v_ffn1_up, v_ffn1_down, v_norm_mix, v_w_in, v_b_gate, v_pool_maps, v_pool_scale, v_w_pool_proj, v_q_latent_norm, v_w_uq, v_kv_latent_norm, v_w_ukv, v_w_attn_proj, v_w_out, v_norm_ffn2, v_ffn2_up, v_ffn2_down, v_final_norm):
    order = ("norm_ffn1", "ffn1_up", "ffn1_down", "norm_mix", "w_in", "b_gate", "pool_maps", "pool_scale",
             "w_pool_proj", "q_latent_norm", "w_uq", "kv_latent_norm", "w_ukv", "w_attn_proj", "w_out",
             "norm_ffn2", "ffn2_up", "ffn2_down", "final_norm")
    w = dict(zip(order, (norm_ffn1, ffn1_up, ffn1_down, norm_mix, w_in, b_gate, pool_maps, pool_scale, w_pool_proj,
                         q_latent_norm, w_uq, kv_latent_norm, w_ukv, w_attn_proj, w_out, norm_ffn2, ffn2_up,
                         ffn2_down, final_norm)))
    m = dict(zip(order, (m_norm_ffn1, m_ffn1_up, m_ffn1_down, m_norm_mix, m_w_in, m_b_gate, m_pool_maps, m_pool_scale,
                         m_w_pool_proj, m_q_latent_norm, m_w_uq, m_kv_latent_norm, m_w_ukv, m_w_attn_proj, m_w_out,
                         m_norm_ffn2, m_ffn2_up, m_ffn2_down, m_final_norm)))
    v = dict(zip(order, (v_norm_ffn1, v_ffn1_up, v_ffn1_down, v_norm_mix, v_w_in, v_b_gate, v_pool_maps, v_pool_scale,
                         v_w_pool_proj, v_q_latent_norm, v_w_uq, v_kv_latent_norm, v_w_ukv, v_w_attn_proj, v_w_out,
                         v_norm_ffn2, v_ffn2_up, v_ffn2_down, v_final_norm)))
    L = norm_ffn1.shape[0]
    B, S, D = x.shape
    T = B * S

    def turned(a, n):
        return a.transpose(0, 2, 1) if n in TRANSPOSED else a

    wk, mk, vk = ({n: turned(d[n], n) for n in order} for d in (w, m, v))
    packed_shapes = [wk[n].shape[1:] for n in PACKED]
    my_slot = 4 * lax.axis_index("x") + 2 * lax.axis_index("y") + lax.axis_index("c")
    me = jnp.stack([my_slot]).astype(jnp.int32)

    def weight_blocks(l, names, token):
        zero = token[0, 0].astype(BF16)
        blocks = [wk[n][l].astype(BF16) + zero for n in names if n not in PACKED]
        if any(n in PACKED for n in names):
            blocks.append(_pack([wk[n][l].astype(BF16) + zero for n in PACKED], 0))
        return blocks

    def kernel_weights(names, lands):
        direct = [n for n in names if n not in PACKED]
        stacked = dict(zip(direct, lands))
        if len(lands) > len(direct):
            stacked.update(zip(PACKED, _unpack(lands[-1], packed_shapes, 1)))
        return _kernel_weights(stacked)

    def gather_start(l, names, token, tag):
        blocks = weight_blocks(l, names, token)
        lands = [lax.empty((N_DEV, *b.shape), b.dtype) for b in blocks]
        send_sems, recv_sems, blocks, lands, token = _exchange_start(
            blocks, lands, _gather_plan, GATHER_PEERS * len(blocks), name=f"gather_start_{tag}")
        return (send_sems, recv_sems, blocks, lands, tag), token

    def gather_wait(state, after):
        send_sems, recv_sems, blocks, lands, tag = state
        return _exchange_wait(send_sems, recv_sems, blocks, lands, _gather_plan, after, name=f"gather_wait_{tag}")[1]

    layer_part = FFN1_PART + MIXER_PART + FFN2_PART
    tabs = _rope_tables(positions.reshape(T))
    xs = x.reshape(T, D)
    h = _rms_fwd(xs, w["norm_ffn1"][0], name="first_norm")
    full, saved = [], []

    p = {n: w[n][0] for n in SMALL}
    blocks = weight_blocks(0, FFN1_PART, jnp.zeros((8, LANE), F32))
    lands = _exchange(blocks, [(N_DEV, *b.shape) for b in blocks], _gather_plan, GATHER_PEERS, name="gather_first")
    lands, token = _gather_forward(lands, name="gather_forward")
    w0 = kernel_weights(FFN1_PART, lands)
    state, token = gather_start(0, MIXER_PART, token, "0_mix")
    xs, h, s1 = _ffn_fwd(xs, h, w0["ffn1"], "ffn1", p["norm_mix"], dep=token)
    lands, token = _gather_forward(gather_wait(state, xs), name="gather_forward")
    w0.update(kernel_weights(MIXER_PART, lands))
    state, token = gather_start(0, FFN2_PART, token, "0_ffn2")
    if L > 1:
        next_state, token = gather_start(1, layer_part, token, "1")
    xs, h, s2 = _mixer_fwd(xs, h, p, w0, tabs, S, p["norm_ffn2"], dep=token)
    lands, token = _gather_forward(gather_wait(state, xs), name="gather_forward")
    w0.update(kernel_weights(FFN2_PART, lands))
    xs, h, s3 = _ffn_fwd(xs, h, w0["ffn2"], "ffn2", w["norm_ffn1"][1] if L > 1 else None, dep=token)
    if L > 1:
        lands, token = _gather_forward(gather_wait(next_state, xs), name="gather_forward")
    full.append(w0)
    saved.append((s1, s2, s3))

    for l in range(1, L):
        full.append(kernel_weights(layer_part, lands))
        more = l + 1 < L
        p = {n: w[n][l] for n in SMALL}
        if more:
            state, token = gather_start(l + 1, layer_part, token, f"{l + 1}")
        xs, h, s1 = _ffn_fwd(xs, h, full[l]["ffn1"], "ffn1", p["norm_mix"], dep=token if more else None)
        xs, h, s2 = _mixer_fwd(xs, h, p, full[l], tabs, S, p["norm_ffn2"])
        if more:
            lands = gather_wait(state, xs)
            send_sems, recv_sems, _, lands, token = _exchange_start(
                [], lands, _forward_plan, FORWARD_COPIES * len(lands), name=f"forward_start_{l + 1}")
        xs, h, s3 = _ffn_fwd(xs, h, full[l]["ffn2"], "ffn2", w["norm_ffn1"][l + 1] if more else None,
                             dep=token if more else None)
        if more:
            _, lands = _exchange_wait(send_sems, recv_sems, [], lands, _forward_plan, xs, name=f"forward_wait_{l + 1}")
        saved.append((s1, s2, s3))
    dx, dfinal, loss = _loss_head(xs, final_norm, loss_target.reshape(T, D), name="loss_head")

    big_grads = {n: [None] * L if n in PACKED else lax.empty((L, *wk[n].shape[1:]), F32) for n in BIG}
    small_grads_of = [None] * L
    pending = None

    def scatter_start(names, stacked, tag):
        srcs = [stacked[n] for n in names if n not in PACKED]
        if any(n in PACKED for n in names):
            srcs.append(_pack([stacked[n] for n in PACKED], 1))
        lands = [lax.empty((SCATTER_PEERS, *s.shape[1:]), s.dtype) for s in srcs]
        send_sems, recv_sems, srcs, lands, token = _exchange_start(
            srcs, lands, _scatter_plan, SCATTER_PEERS * len(srcs), name=f"scatter_start_{tag}")
        return (names, send_sems, recv_sems, srcs, lands, tag), token

    def scatter_finish(state, after, l):
        names, send_sems, recv_sems, srcs, lands, tag = state
        srcs, got = _exchange_wait(send_sems, recv_sems, srcs, lands, _scatter_plan, after, name=f"scatter_wait_{tag}")
        direct = [n for n in names if n not in PACKED]
        for n, s, g in zip(direct, srcs, got):
            big_grads[n] = _scatter_sum(s, g, me, big_grads[n], l, name="scatter_sum")
        if len(srcs) > len(direct):
            packed = _scatter_sum(srcs[-1], got[-1], me, lax.empty((1, *srcs[-1].shape[1:]), F32), 0,
                                  name="scatter_sum")[0]
            for n, g in zip(PACKED, _unpack(packed, packed_shapes, 0)):
                big_grads[n][l] = g

    dep = None
    for l in reversed(range(L)):
        p = {n: w[n][l] for n in SMALL}
        s1, s2, s3 = saved[l]
        small_g = {}
        dx, small_g["norm_ffn2"], dup_t, dwd = _ffn_bwd(dx, p["norm_ffn2"], full[l]["ffn2"], s3, "ffn2", dep=dep)
        if pending is not None:
            scatter_finish(pending[0], dx, pending[1])
        stacked = {"ffn2_up": _split_rows(dup_t), "ffn2_down": _split_rows(dwd)}
        state, dep = scatter_start(("ffn2_up", "ffn2_down"), stacked, f"ffn2_{l}")
        pending = (state, l)

        dx, gm = _mixer_bwd(dx, p, full[l], tabs, s2, S, dep=dep)
        scatter_finish(pending[0], dx, pending[1])
        names = ("w_in", "w_attn_proj", "w_out") + PACKED
        state, dep = scatter_start(names, _mixer_grads_stacked(gm), f"mix_{l}")
        pending = (state, l)
        small_g.update({n: gm[n] for n in SMALL if n in gm})

        if l > 0:
            dx, small_g["norm_ffn1"], dup_t, dwd = _ffn_bwd(dx, p["norm_ffn1"], full[l]["ffn1"], s1, "ffn1", dep=dep)
            scatter_finish(pending[0], dx, pending[1])
            stacked = {"ffn1_up": _split_rows(dup_t), "ffn1_down": _split_rows(dwd)}
            state, dep = scatter_start(("ffn1_up", "ffn1_down"), stacked, f"ffn1_{l}")
            pending = (state, l)
        else:
            early_states = []

            def send_down(dwd):
                state, token = scatter_start(("ffn1_down",), {"ffn1_down": _split_rows(dwd)}, "ffn1_down_0")
                early_states.append(state)
                return token

            def send_small(dgain):
                small_g["norm_ffn1"] = dgain
                small_grads_of[0] = small_g
                parts = [small_grads_of[k][n] for k in range(L) for n in SMALL] + [dfinal, loss[0, :1]]
                vec = _pack([jnp.concatenate([a.reshape(-1) for a in parts])], 0)
                out = _exchange_start([vec], [lax.empty((N_DEV, *vec.shape), F32)], _gather_all_plan, SCATTER_PEERS,
                                      name="small_start")
                early_states.append((out, [a.shape for a in parts]))
                return out[4]

            dx, _, dup_t, _ = _ffn_bwd(dx, p["norm_ffn1"], full[l]["ffn1"], s1, "ffn1", dep=dep, early=send_down,
                                       mid=send_small)
            last_mixer, last_down = pending, (early_states[0], 0)
            (small_send, small_recv, vec_thru, small_land, _), small_shapes = early_states[1]
            state, dep = scatter_start(("ffn1_up",), {"ffn1_up": _split_rows(dup_t)}, "ffn1_up_0")
            pending = (state, l)
        small_grads_of[l] = small_g
    grad_x = dx.reshape(B, S, D)

    gk, grad, delta, new_m, new_v = {}, {}, {}, {}, {}

    def update(n, dep=None):
        wn, gn, mn, vn = (a.reshape(1, -1) if a.ndim == 1 else a for a in (wk[n], gk[n], mk[n], vk[n]))
        d, nm, nv = _adamw(wn, gn, mn, vn, name="adamw_" + n, dep=dep)
        grad[n] = turned(gk[n], n)
        delta[n], new_m[n], new_v[n] = (turned(a.reshape(wk[n].shape), n) for a in (d, nm, nv))

    for i, n in enumerate(FFN2_PART):
        gk[n] = jnp.stack(big_grads[n]) if n in PACKED else big_grads[n]
        update(n, dep if i == 0 else None)
    scatter_finish(last_mixer[0], new_v["ffn2_down"], last_mixer[1])
    for n in MIXER_PART:
        gk[n] = jnp.stack(big_grads[n]) if n in PACKED else big_grads[n]
        update(n)
    scatter_finish(last_down[0], new_v[MIXER_PART[-1]], last_down[1])
    scatter_finish(pending[0], new_v[MIXER_PART[-1]], pending[1])
    for n in FFN1_PART:
        gk[n] = jnp.stack(big_grads[n]) if n in PACKED else big_grads[n]
        update(n)

    vec_thru, small_land = _exchange_wait(small_send, small_recv, vec_thru, small_land, _gather_all_plan,
                                          new_v["ffn1_down"], name="small_wait")
    parts = lax.dynamic_update_index_in_dim(small_land[0], vec_thru[0], my_slot, 0)
    flat = _sum_devices(parts, name="sum_small").reshape(-1)
    small_grads, at = [], 0
    for shape in small_shapes:
        size = int(np.prod(shape))
        small_grads.append(lax.slice_in_dim(flat, at, at + size).reshape(shape))
        at += size
    loss_total = small_grads[-1].reshape(())
    for i, n in enumerate(SMALL):
        gk[n] = jnp.stack([small_grads[l * len(SMALL) + i] for l in range(L)]).reshape(w[n].shape)
        update(n)
    gk["final_norm"] = small_grads[-2].reshape(final_norm.shape)
    update("final_norm")
    return (loss_total, grad_x, *[grad[n] for n in order], *[delta[n] for n in order],
            *[new_m[n] for n in order], *[new_v[n] for n in order])
```

```python
import numpy as np
import jax
import jax.numpy as jnp
from jax import lax
from jax.experimental import pallas as pl
from jax.experimental.pallas import tpu as pltpu

F32 = jnp.float32
BF16 = jnp.bfloat16

NORM_EPS = 1e-6
ROPE_THETA = 10000.0
QK_NOPE = 128
QK_ROPE = 64
V_DIM = 128
HEAD_W = 256
POOL_WINDOWS = (2, 4, 8, 16)
POOL_G = 128
POOL_DIM = 512
LANE = 128
ATTN_SCALE = float((QK_NOPE + QK_ROPE) ** -0.5)
ATTN_SCALE_LOG2 = ATTN_SCALE * float(np.log2(np.e))
MASK_VALUE = -1e30
ATTN_TILE = 512

ADAM_LR = 0.001
ADAM_B1 = 0.9
ADAM_B2 = 0.999
ADAM_EPS = 1e-08
ADAM_WD = 0.01
ADAM_STEP = 10

N_DEV = 8
VMEM_LIMIT = 52 * 1024 * 1024

MESH = pl.DeviceIdType.MESH
ANY = pl.BlockSpec(memory_space=pl.ANY)


def _tile(dim, target, align=LANE):
    if dim <= target:
        return dim
    t = (target // align) * align
    while t >= align:
        if dim % t == 0:
            return t
        t -= align
    return dim


def _params(*sem):
    return pltpu.CompilerParams(dimension_semantics=sem, vmem_limit_bytes=VMEM_LIMIT)


def _rstd(x):
    return lax.rsqrt(jnp.mean(x * x, axis=-1, keepdims=True) + NORM_EPS)


def _mm(a, b, *, name, ta=False, tb=False, out_dtype=F32, alpha=1.0, tm=512, tn=1024, tk=1024, dep=None):
    if ta:
        K, M = a.shape
    else:
        M, K = a.shape
    if tb:
        N, K2 = b.shape
    else:
        K2, N = b.shape
    assert K == K2, (a.shape, b.shape, ta, tb)
    tm, tn, tk = _tile(M, tm), _tile(N, tn), _tile(K, tk)
    nk = K // tk
    dims = (((0 if ta else 1,), (1 if tb else 0,)), ((), ()))

    def body(a_ref, b_ref, *rest):
        o_ref = rest[0 if dep is None else 1]
        acc_ref = rest[-1] if nk > 1 else None
        part = lax.dot_general(a_ref[...].astype(BF16), b_ref[...].astype(BF16), dims,
                               preferred_element_type=F32)

        def finish(acc):
            o_ref[...] = (acc * alpha if alpha != 1.0 else acc).astype(out_dtype)

        if nk == 1:
            finish(part)
        else:
            k = pl.program_id(2)

            @pl.when(k == 0)
            def _():
                acc_ref[...] = part

            @pl.when(k > 0)
            def _():
                acc_ref[...] += part

            @pl.when(k == nk - 1)
            def _():
                finish(acc_ref[...])

    a_spec = pl.BlockSpec((tk, tm), lambda i, j, k: (k, i)) if ta else pl.BlockSpec((tm, tk), lambda i, j, k: (i, k))
    b_spec = pl.BlockSpec((tn, tk), lambda i, j, k: (j, k)) if tb else pl.BlockSpec((tk, tn), lambda i, j, k: (k, j))
    return pl.pallas_call(
        body, name=name,
        out_shape=jax.ShapeDtypeStruct((M, N), out_dtype),
        grid=(M // tm, N // tn, nk),
        in_specs=[a_spec, b_spec] + _dep_spec(dep),
        out_specs=pl.BlockSpec((tm, tn), lambda i, j, k: (i, j)),
        scratch_shapes=[pltpu.VMEM((tm, tn), F32)] if nk > 1 else [],
        compiler_params=_params("parallel", "parallel", "arbitrary"),
    )(a, b, *([] if dep is None else [dep]))


def _dw_multi(pairs, *, name, tk=512):
    n = len(pairs)
    T = pairs[0][0].shape[0]
    tk = _tile(T, tk, 16)
    nk = T // tk
    shapes = [(a.shape[1], b.shape[1]) for a, b in pairs]

    def body(*refs):
        ins, outs, accs = refs[:2 * n], refs[2 * n:3 * n], refs[3 * n:]
        k = pl.program_id(0)
        parts = [lax.dot_general(ins[2 * i][...].astype(BF16), ins[2 * i + 1][...].astype(BF16), _TN,
                                 preferred_element_type=F32) for i in range(n)]

        @pl.when(k == 0)
        def _():
            for acc, part in zip(accs, parts):
                acc[...] = part

        @pl.when(k > 0)
        def _():
            for acc, part in zip(accs, parts):
                acc[...] += part

        @pl.when(k == nk - 1)
        def _():
            for out, acc in zip(outs, accs):
                out[...] = acc[...].astype(BF16)

    return pl.pallas_call(
        body, name=name,
        out_shape=tuple(jax.ShapeDtypeStruct(s, BF16) for s in shapes),
        grid=(nk,),
        in_specs=[pl.BlockSpec((tk, x.shape[1]), lambda k: (k, 0)) for pair in pairs for x in pair],
        out_specs=tuple(pl.BlockSpec(s, lambda k: (0, 0)) for s in shapes),
        scratch_shapes=[pltpu.VMEM(s, F32) for s in shapes],
        compiler_params=_params("arbitrary"),
    )(*[x for pair in pairs for x in pair])


def _rms_fwd(x, g, *, name):
    T, D = x.shape
    tm = _tile(T, 512, 16)

    def body(x_ref, g_ref, h_ref):
        x = x_ref[...]
        h_ref[...] = (x * _rstd(x) * g_ref[...]).astype(BF16)

    return pl.pallas_call(
        body, name=name,
        out_shape=jax.ShapeDtypeStruct((T, D), BF16),
        grid=(T // tm,),
        in_specs=[pl.BlockSpec((tm, D), lambda i: (i, 0)), pl.BlockSpec((1, D), lambda i: (0, 0))],
        out_specs=pl.BlockSpec((tm, D), lambda i: (i, 0)),
        compiler_params=_params("parallel"),
    )(x, g.reshape(1, D))


def _loss_head(x, g, target, *, name):
    T, D = x.shape
    tm = _tile(T, 512, 16)

    def body(x_ref, g_ref, t_ref, dx_ref, dg_ref, loss_ref):
        x = x_ref[...]
        gain = g_ref[...]
        r = _rstd(x)
        xhat = x * r
        err = xhat * gain - t_ref[...]
        dy = err * (1.0 / D)
        dxh = dy * gain
        dx_ref[...] = r * (dxh - xhat * jnp.mean(dxh * xhat, axis=-1, keepdims=True))
        dg_part = jnp.sum(dy * xhat, axis=0, keepdims=True)
        loss_part = jnp.full((1, LANE), 0.5 / D, F32) * jnp.sum(err * err)

        @pl.when(pl.program_id(0) == 0)
        def _():
            dg_ref[...] = dg_part
            loss_ref[...] = loss_part

        @pl.when(pl.program_id(0) > 0)
        def _():
            dg_ref[...] += dg_part
            loss_ref[...] += loss_part

    row = pl.BlockSpec((tm, D), lambda i: (i, 0))
    vec = pl.BlockSpec((1, D), lambda i: (0, 0))
    return pl.pallas_call(
        body, name=name,
        out_shape=(jax.ShapeDtypeStruct((T, D), F32), jax.ShapeDtypeStruct((1, D), F32),
                   jax.ShapeDtypeStruct((1, LANE), F32)),
        grid=(T // tm,),
        in_specs=[row, vec, row],
        out_specs=(row, vec, pl.BlockSpec((1, LANE), lambda i: (0, 0))),
        compiler_params=_params("arbitrary"),
    )(x, g.reshape(1, D), target)


def _ffn_fwd_core(x, h, w_up_t, wd, next_gain, *, alpha, name, dep=None):
    T, D = x.shape
    F = wd.shape[0]
    tm = _tile(T, 256, 16)
    has_norm = next_gain is not None

    def body(x_ref, h_ref, wg_ref, wu_ref, wd_ref, *rest):
        outs = rest[len(rest) - (5 if has_norm else 4):]
        gate_ref, up_ref, a_ref, xn_ref = outs[:4]
        h = h_ref[...]
        gate = lax.dot_general(h, wg_ref[...], _NT, preferred_element_type=F32)
        up = lax.dot_general(h, wu_ref[...], _NT, preferred_element_type=F32)
        a = (gate * jax.nn.sigmoid(gate) * up).astype(BF16)
        gate_ref[...] = gate.astype(BF16)
        up_ref[...] = up.astype(BF16)
        a_ref[...] = a
        xn = x_ref[...] + alpha * jnp.dot(a, wd_ref[...], preferred_element_type=F32)
        xn_ref[...] = xn
        if has_norm:
            outs[4][...] = (xn * _rstd(xn) * rest[0][...]).astype(BF16)

    once = pl.Buffered(1)
    row_d = pl.BlockSpec((tm, D), lambda i: (i, 0))
    row_f = pl.BlockSpec((tm, F), lambda i: (i, 0))
    vec = pl.BlockSpec((1, D), lambda i: (0, 0))
    act = jax.ShapeDtypeStruct((T, F), BF16)
    operands = [x, h, w_up_t, w_up_t, wd] + ([next_gain.reshape(1, D)] if has_norm else [])
    out = pl.pallas_call(
        body, name=name,
        out_shape=(act, act, act, jax.ShapeDtypeStruct((T, D), F32)) + ((jax.ShapeDtypeStruct((T, D), BF16),) if has_norm else ()),
        grid=(T // tm,),
        in_specs=[row_d, row_d,
                  pl.BlockSpec((F, D), lambda i: (0, 0), pipeline_mode=once),
                  pl.BlockSpec((F, D), lambda i: (1, 0), pipeline_mode=once),
                  pl.BlockSpec((F, D), lambda i: (0, 0), pipeline_mode=once)] + ([vec] if has_norm else []) + _dep_spec(dep),
        out_specs=(row_f, row_f, row_f, row_d) + ((row_d,) if has_norm else ()),
        compiler_params=_params("parallel"),
    )(*operands, *([] if dep is None else [dep]))
    return out if has_norm else (*out, None)


def _ffn_bwd_core(dxo, wd, w_up_t, gate, up, x, gain, *, alpha, name, dep=None):
    T, D = dxo.shape
    F = wd.shape[0]
    tm = _tile(T, 256, 16)

    def body(dxo_ref, wd_ref, wg_ref, wu_ref, gate_ref, up_ref, x_ref, g_ref, *rest):
        dgate_ref, dup_ref, dx_ref, dg_ref = rest[-4:]
        dxo = dxo_ref[...]
        da = lax.dot_general(dxo.astype(BF16), wd_ref[...], _NT, preferred_element_type=F32) * alpha
        gate = gate_ref[...].astype(F32)
        up = up_ref[...].astype(F32)
        sig = jax.nn.sigmoid(gate)
        dgate = (da * up * (sig * (1.0 + gate * (1.0 - sig)))).astype(BF16)
        dup = (da * (gate * sig)).astype(BF16)
        dgate_ref[...] = dgate
        dup_ref[...] = dup
        dh = (jnp.dot(dgate, wg_ref[...], preferred_element_type=F32)
              + jnp.dot(dup, wu_ref[...], preferred_element_type=F32))
        x = x_ref[...]
        r = _rstd(x)
        xhat = x * r
        dxh = dh * g_ref[...]
        dx_ref[...] = dxo + r * (dxh - xhat * jnp.mean(dxh * xhat, axis=-1, keepdims=True))
        part = jnp.sum(dh * xhat, axis=0, keepdims=True)

        @pl.when(pl.program_id(0) == 0)
        def _():
            dg_ref[...] = part

        @pl.when(pl.program_id(0) > 0)
        def _():
            dg_ref[...] += part

    once = pl.Buffered(1)
    row_d = pl.BlockSpec((tm, D), lambda i: (i, 0))
    row_f = pl.BlockSpec((tm, F), lambda i: (i, 0))
    vec = pl.BlockSpec((1, D), lambda i: (0, 0))
    act = jax.ShapeDtypeStruct((T, F), BF16)
    return pl.pallas_call(
        body, name=name,
        out_shape=(act, act, jax.ShapeDtypeStruct((T, D), F32), jax.ShapeDtypeStruct((1, D), F32)),
        grid=(T // tm,),
        in_specs=[row_d,
                  pl.BlockSpec((F, D), lambda i: (0, 0), pipeline_mode=once),
                  pl.BlockSpec((F, D), lambda i: (0, 0), pipeline_mode=once),
                  pl.BlockSpec((F, D), lambda i: (1, 0), pipeline_mode=once),
                  row_f, row_f, row_d, vec] + _dep_spec(dep),
        out_specs=(row_f, row_f, row_d, vec),
        compiler_params=_params("arbitrary"),
    )(dxo, wd, w_up_t, w_up_t, gate, up, x, gain.reshape(1, D), *([] if dep is None else [dep]))


def _ffn_dw_up(dgate, dup, h, *, name, dep=None):
    T, F = dgate.shape
    D = h.shape[1]
    tm, tk = _tile(F, 1408), _tile(T, 1024, 16)
    nf, nk = F // tm, T // tk

    def body(dgate_ref, dup_ref, h_ref, *rest):
        o_ref, acc_ref = rest[-2:]
        i, k = pl.program_id(0), pl.program_id(1)

        def accumulate(part):
            @pl.when(k == 0)
            def _():
                acc_ref[...] = part

            @pl.when(k > 0)
            def _():
                acc_ref[...] += part

        @pl.when(i < nf)
        def _():
            accumulate(lax.dot_general(dgate_ref[...], h_ref[...], _TN, preferred_element_type=F32))

        @pl.when(i >= nf)
        def _():
            accumulate(lax.dot_general(dup_ref[...], h_ref[...], _TN, preferred_element_type=F32))

        @pl.when(k == nk - 1)
        def _():
            o_ref[...] = acc_ref[...].astype(BF16)

    return pl.pallas_call(
        body, name=name,
        out_shape=jax.ShapeDtypeStruct((2 * F, D), BF16),
        grid=(2 * nf, nk),
        in_specs=[pl.BlockSpec((tk, tm), lambda i, k: (jnp.where(i < nf, k, nk - 1), jnp.minimum(i, nf - 1))),
                  pl.BlockSpec((tk, tm), lambda i, k: (jnp.where(i < nf, 0, k), jnp.maximum(i - nf, 0))),
                  pl.BlockSpec((tk, D), lambda i, k: (k, 0))] + _dep_spec(dep),
        out_specs=pl.BlockSpec((tm, D), lambda i, k: (i, 0)),
        scratch_shapes=[pltpu.VMEM((tm, D), F32)],
        compiler_params=_params("parallel", "arbitrary"),
    )(dgate, dup, h, *([] if dep is None else [dep]))


def _dep_spec(dep):
    return [] if dep is None else [pl.BlockSpec(dep.shape, lambda *_: (0,) * dep.ndim)]


def _rope_tables(positions):
    half = QK_ROPE // 2
    inv_freq = ROPE_THETA ** (-jnp.arange(0, QK_ROPE, 2, dtype=F32) / QK_ROPE)
    ang = positions.astype(F32)[:, None] * inv_freq
    cos, sin = jnp.cos(ang), jnp.sin(ang)
    z = jnp.zeros_like(cos)
    zz = jnp.zeros((positions.shape[0], LANE - QK_ROPE), F32)
    c = jnp.concatenate([cos, cos, zz], axis=1)
    sa = jnp.concatenate([z, sin, zz], axis=1)
    sb = jnp.concatenate([-sin, z, zz], axis=1)
    return c, sa, sb


def _rotate(seg, c, sa, sb, sign):
    half = QK_ROPE // 2
    mix = pltpu.roll(seg, half, 1) * sa + pltpu.roll(seg, LANE - half, 1) * sb
    return seg * c + mix if sign > 0 else seg * c - mix


def _mixer_in(h, wa, wuq, wukv, gq, gkv, tabs, *, name, dep=None):
    T, D = h.shape
    HQ, QL = wuq.shape
    KVL = wukv.shape[0]
    H = HQ // HEAD_W
    o_q, o_kv, o_kr = POOL_DIM, POOL_DIM + QL, POOL_DIM + QL + KVL
    PA = o_kr + LANE
    assert wa.shape[0] >= PA
    tm = _tile(T, 512, 16)

    def body(h_ref, wa_ref, wuq_ref, wukv_ref, gq_ref, gkv_ref, c_ref, sa_ref, sb_ref, *rest):
        xp_ref, ql_ref, kvl_ref, qn_ref, kvn_ref, q_ref, kv_ref, kr_ref = rest[-8:]
        proj = lax.dot_general(h_ref[...], wa_ref[...], _NT, preferred_element_type=F32)
        xp_ref[...] = proj[:, :POOL_DIM]
        ql = proj[:, o_q:o_kv]
        kvl = proj[:, o_kv:o_kr]
        ql_ref[...] = ql
        kvl_ref[...] = kvl
        qn = (ql * _rstd(ql) * gq_ref[...]).astype(BF16)
        kvn = (kvl * _rstd(kvl) * gkv_ref[...]).astype(BF16)
        qn_ref[...] = qn
        kvn_ref[...] = kvn
        c, sa, sb = c_ref[...], sa_ref[...], sb_ref[...]
        q = lax.dot_general(qn, wuq_ref[...], _NT, preferred_element_type=F32)
        for hh in range(H):
            base = hh * HEAD_W
            q_ref[:, base:base + QK_NOPE] = q[:, base:base + QK_NOPE].astype(BF16)
            q_ref[:, base + QK_NOPE:base + HEAD_W] = _rotate(
                q[:, base + QK_NOPE:base + HEAD_W], c, sa, sb, 1).astype(BF16)
        kv_ref[...] = jnp.dot(kvn, wukv_ref[...], preferred_element_type=F32).astype(BF16)
        kr_ref[...] = _rotate(proj[:, o_kr:o_kr + LANE], c, sa, sb, 1).astype(BF16)

    def row(w):
        return pl.BlockSpec((tm, w), lambda i: (i, 0))

    def whole(arr):
        return pl.BlockSpec(arr.shape, lambda i: (0,) * arr.ndim)

    gq2, gkv2 = gq.reshape(1, QL), gkv.reshape(1, KVL)
    outs = [(POOL_DIM, F32), (QL, F32), (KVL, F32), (QL, BF16), (KVL, BF16), (HQ, BF16), (HQ, BF16), (LANE, BF16)]
    return pl.pallas_call(
        body, name=name,
        out_shape=tuple(jax.ShapeDtypeStruct((T, w), dt) for w, dt in outs),
        grid=(T // tm,),
        in_specs=[row(D), pl.BlockSpec((PA, D), lambda i: (0, 0)), whole(wuq), whole(wukv), whole(gq2), whole(gkv2),
                  row(LANE), row(LANE), row(LANE)] + _dep_spec(dep),
        out_specs=tuple(row(w) for w, _ in outs),
        compiler_params=_params("parallel"),
    )(h, wa, wuq, wukv, gq2, gkv2, *tabs, *([] if dep is None else [dep]))


def _mixer_in_bwd(dq, dkv, dkr, ql, kvl, dxp, dgl, x, dxo, win_t, wuq, wukv, g_mix, gq, gkv, tabs, *, name):
    T, HQ = dq.shape
    D = x.shape[1]
    QL, KVL = wuq.shape[1], wukv.shape[0]
    H = HQ // HEAD_W
    PA = POOL_DIM + QL + KVL + LANE
    o_q, o_kv, o_kr = POOL_DIM, POOL_DIM + QL, POOL_DIM + QL + KVL
    tm = _tile(T, 256, 16)

    def norm_bwd(lat, gain, dn):
        r = _rstd(lat)
        xhat = lat * r
        dxh = dn * gain
        dlat = r * (dxh - xhat * jnp.mean(dxh * xhat, axis=-1, keepdims=True))
        return dlat, jnp.sum(dn * xhat, axis=0, keepdims=True)

    def body(dq_ref, dkv_ref, dkr_ref, ql_ref, kvl_ref, dxp_ref, dgl_ref, x_ref, dxo_ref, win_ref,
             wuq_ref, wukv_ref, gmix_ref, gq_ref, gkv_ref, c_ref, sa_ref, sb_ref,
             dproj_ref, dqp_ref, dgq_ref, dgkv_ref, dx_ref, dgmix_ref):
        c, sa, sb = c_ref[...], sa_ref[...], sb_ref[...]
        for hh in range(H):
            base = hh * HEAD_W
            dqp_ref[:, base:base + QK_NOPE] = dq_ref[:, base:base + QK_NOPE]
            dqp_ref[:, base + QK_NOPE:base + HEAD_W] = _rotate(
                dq_ref[:, base + QK_NOPE:base + HEAD_W].astype(F32), c, sa, sb, -1).astype(BF16)
        dqn = jnp.dot(dqp_ref[...], wuq_ref[...], preferred_element_type=F32)
        dkvn = lax.dot_general(dkv_ref[...], wukv_ref[...], _NT, preferred_element_type=F32)
        dql, dgq = norm_bwd(ql_ref[...], gq_ref[...], dqn)
        dkvl, dgkv = norm_bwd(kvl_ref[...], gkv_ref[...], dkvn)
        dproj_ref[:, :POOL_DIM] = dxp_ref[...].astype(BF16)
        dproj_ref[:, o_q:o_kv] = dql.astype(BF16)
        dproj_ref[:, o_kv:o_kr] = dkvl.astype(BF16)
        dproj_ref[:, o_kr:PA] = _rotate(dkr_ref[...], c, sa, sb, -1).astype(BF16)

        n_gate = win_ref.shape[0] - 2 * D
        dh = (jnp.dot(dproj_ref[...], win_ref[:PA, :], preferred_element_type=F32)
              + jnp.dot(dgl_ref[...], win_ref[n_gate:, :], preferred_element_type=F32))
        x = x_ref[...]
        r = _rstd(x)
        xhat = x * r
        dxh = dh * gmix_ref[...]
        dx_ref[...] = dxo_ref[...] + r * (dxh - xhat * jnp.mean(dxh * xhat, axis=-1, keepdims=True))
        dgmix = jnp.sum(dh * xhat, axis=0, keepdims=True)

        @pl.when(pl.program_id(0) == 0)
        def _():
            dgq_ref[...] = dgq
            dgkv_ref[...] = dgkv
            dgmix_ref[...] = dgmix

        @pl.when(pl.program_id(0) > 0)
        def _():
            dgq_ref[...] += dgq
            dgkv_ref[...] += dgkv
            dgmix_ref[...] += dgmix

    def row(w):
        return pl.BlockSpec((tm, w), lambda i: (i, 0))

    def resident(arr):
        return pl.BlockSpec(arr.shape, lambda i: (0, 0), pipeline_mode=pl.Buffered(1))

    gmix2, gq2, gkv2 = g_mix.reshape(1, D), gq.reshape(1, QL), gkv.reshape(1, KVL)
    vec = pl.BlockSpec((1, D), lambda i: (0, 0))
    vq, vkv = pl.BlockSpec((1, QL), lambda i: (0, 0)), pl.BlockSpec((1, KVL), lambda i: (0, 0))
    return pl.pallas_call(
        body, name=name,
        out_shape=(jax.ShapeDtypeStruct((T, PA), BF16), jax.ShapeDtypeStruct((T, HQ), BF16),
                   jax.ShapeDtypeStruct((1, QL), F32), jax.ShapeDtypeStruct((1, KVL), F32),
                   jax.ShapeDtypeStruct((T, D), F32), jax.ShapeDtypeStruct((1, D), F32)),
        grid=(T // tm,),
        in_specs=[row(HQ), row(HQ), row(LANE), row(QL), row(KVL), row(POOL_DIM), row(2 * D), row(D), row(D),
                  resident(win_t), resident(wuq), resident(wukv),
                  vec, vq, vkv, row(LANE), row(LANE), row(LANE)],
        out_specs=(row(PA), row(HQ), vq, vkv, row(D), vec),
        compiler_params=_params("arbitrary"),
    )(dq, dkv, dkr, ql, kvl, dxp, dgl, x, dxo, win_t, wuq, wukv, gmix2, gq2, gkv2, *tabs)


def _pool_groups(x_of, S):
    row = lax.broadcasted_iota(jnp.int32, (S, POOL_G), 0)
    for g, w in enumerate(POOL_WINDOWS):
        x = x_of(g)
        s = x
        d = 1
        while d < w:
            s = s + jnp.where(row >= d, pltpu.roll(s, d, 0), 0.0)
            d *= 2
        cnt = jnp.minimum(row + 1, w).astype(F32)
        yield g, w, x, s / cnt - x, cnt, row


def _pool_fwd(xp, maps, scale, *, S, name):
    T = xp.shape[0]

    def body(xp_ref, maps_ref, scale_ref, ms_ref):
        for g, _, _, pooled, _, _ in _pool_groups(lambda g: xp_ref[:, g * POOL_G:(g + 1) * POOL_G], S):
            mixed = jnp.dot(pooled.astype(BF16), maps_ref[g].astype(BF16), preferred_element_type=F32)
            ms_ref[:, g * POOL_G:(g + 1) * POOL_G] = (mixed * scale_ref[:, g * POOL_G:(g + 1) * POOL_G]).astype(BF16)

    return pl.pallas_call(
        body, name=name,
        out_shape=jax.ShapeDtypeStruct((T, POOL_DIM), BF16),
        grid=(T // S,),
        in_specs=[pl.BlockSpec((S, POOL_DIM), lambda b: (b, 0)),
                  pl.BlockSpec(maps.shape, lambda b: (0, 0, 0)),
                  pl.BlockSpec((1, POOL_DIM), lambda b: (0, 0))],
        out_specs=pl.BlockSpec((S, POOL_DIM), lambda b: (b, 0)),
        compiler_params=_params("parallel"),
    )(xp, maps, scale.reshape(1, POOL_DIM))


def _pool_bwd(xp, dms, maps, scale, *, S, name):
    T = xp.shape[0]

    def body(xp_ref, dms_ref, maps_ref, scale_ref, dxp_ref, dmaps_ref, dscale_ref):
        first = pl.program_id(0) == 0
        for g, w, _, pooled, cnt, row in _pool_groups(lambda g: xp_ref[:, g * POOL_G:(g + 1) * POOL_G], S):
            cols = slice(g * POOL_G, (g + 1) * POOL_G)
            pooled_b = pooled.astype(BF16)
            maps_b = maps_ref[g].astype(BF16)
            mixed = jnp.dot(pooled_b, maps_b, preferred_element_type=F32)
            dms = dms_ref[:, cols]
            dscale = jnp.sum(dms * mixed, axis=0, keepdims=True)
            dmixed = (dms * scale_ref[:, cols]).astype(BF16)
            dmaps = lax.dot_general(pooled_b, dmixed, (((0,), (0,)), ((), ())), preferred_element_type=F32)
            dpooled = lax.dot_general(dmixed, maps_b, (((1,), (1,)), ((), ())), preferred_element_type=F32)
            z = dpooled / cnt
            d = 1
            while d < w:
                z = z + jnp.where(row < S - d, pltpu.roll(z, S - d, 0), 0.0)
                d *= 2
            dxp_ref[:, cols] = z - dpooled

            @pl.when(first)
            def _():
                dmaps_ref[g] = dmaps
                dscale_ref[:, cols] = dscale

            @pl.when(jnp.logical_not(first))
            def _():
                dmaps_ref[g] += dmaps
                dscale_ref[:, cols] += dscale

    seq = pl.BlockSpec((S, POOL_DIM), lambda b: (b, 0))
    maps_spec = pl.BlockSpec(maps.shape, lambda b: (0, 0, 0))
    vec = pl.BlockSpec((1, POOL_DIM), lambda b: (0, 0))
    return pl.pallas_call(
        body, name=name,
        out_shape=(jax.ShapeDtypeStruct((T, POOL_DIM), F32), jax.ShapeDtypeStruct(maps.shape, F32),
                   jax.ShapeDtypeStruct((1, POOL_DIM), F32)),
        grid=(T // S,),
        in_specs=[seq, seq, maps_spec, vec],
        out_specs=(seq, maps_spec, vec),
        compiler_params=_params("arbitrary"),
    )(xp, dms, maps, scale.reshape(1, POOL_DIM))


def _causal_mask(s, t):
    r = lax.broadcasted_iota(jnp.int32, (t, t), 0)
    c = lax.broadcasted_iota(jnp.int32, (t, t), 1)
    return jnp.where(r >= c, s, MASK_VALUE)


_NT = (((1,), (1,)), ((), ()))
_TN = (((0,), (0,)), ((), ()))


def _attn_fwd(q, kv, kr, *, S, name):
    T, HQ = q.shape
    H = HQ // HEAD_W
    B = T // S
    t = _tile(S, ATTN_TILE)
    n = S // t

    def body(q_ref, k_ref, v_ref, kr_ref, o_ref, lse_ref, kcat, vcat):
        kcat[:, :QK_NOPE] = k_ref[...]
        kcat[:, QK_NOPE:] = kr_ref[...]
        vcat[:, :V_DIM] = v_ref[...]
        vcat[:, V_DIM:] = jnp.ones((S, HEAD_W - V_DIM), BF16)
        for i in range(n):
            rows = slice(i * t, (i + 1) * t)
            qt = q_ref[rows, :]
            m = jnp.full((t, 1), MASK_VALUE, F32)
            acc = jnp.zeros((t, HEAD_W), F32)
            for j in range(i + 1):
                cols = slice(j * t, (j + 1) * t)
                s = lax.dot_general(qt, kcat[cols, :], _NT, preferred_element_type=F32) * ATTN_SCALE_LOG2
                if j == i:
                    s = _causal_mask(s, t)
                m_new = jnp.maximum(m, jnp.max(s, axis=1, keepdims=True))
                p = jnp.exp2(s - m_new)
                acc = jnp.exp2(m - m_new) * acc + jnp.dot(p.astype(BF16), vcat[cols, :], preferred_element_type=F32)
                m = m_new
            l = acc[:, V_DIM:V_DIM + 1]
            o_ref[rows, :] = (acc[:, :V_DIM] / l).astype(BF16)
            lse_ref[rows, :] = jnp.broadcast_to(m + jnp.log2(l), (t, LANE))

    seq_h = pl.BlockSpec((S, LANE), lambda b, h: (b, h))
    return pl.pallas_call(
        body, name=name,
        out_shape=(jax.ShapeDtypeStruct((T, H * V_DIM), BF16), jax.ShapeDtypeStruct((T, H * LANE), F32)),
        grid=(B, H),
        in_specs=[pl.BlockSpec((S, HEAD_W), lambda b, h: (b, h)),
                  pl.BlockSpec((S, QK_NOPE), lambda b, h: (b, 2 * h)),
                  pl.BlockSpec((S, V_DIM), lambda b, h: (b, 2 * h + 1)),
                  pl.BlockSpec((S, LANE), lambda b, h: (b, 0))],
        out_specs=(seq_h, seq_h),
        scratch_shapes=[pltpu.VMEM((S, HEAD_W), BF16), pltpu.VMEM((S, HEAD_W), BF16)],
        compiler_params=_params("parallel", "parallel"),
    )(q, kv, kv, kr)


def _attn_bwd(q, kv, kr, o, do, lse, *, S, name):
    T, HQ = q.shape
    H = HQ // HEAD_W
    B = T // S
    t = _tile(S, ATTN_TILE)
    n = S // t

    def body(q_ref, k_ref, v_ref, kr_ref, o_ref, do_ref, lse_ref, dq_ref, dkv_ref, dkr_ref, kcat, dq_acc):
        @pl.when(pl.program_id(1) == 0)
        def _():
            dkr_ref[...] = jnp.zeros_like(dkr_ref)

        kcat[:, :QK_NOPE] = k_ref[...]
        kcat[:, QK_NOPE:] = kr_ref[...]
        delta = [jnp.sum(do_ref[i * t:(i + 1) * t, :].astype(F32) * o_ref[i * t:(i + 1) * t, :].astype(F32),
                         axis=1, keepdims=True) for i in range(n)]
        for j in range(n):
            cols = slice(j * t, (j + 1) * t)
            kc = kcat[cols, :]
            vt = v_ref[cols, :]
            dk = jnp.zeros((t, HEAD_W), F32)
            dv = jnp.zeros((t, V_DIM), F32)
            for i in range(j, n):
                rows = slice(i * t, (i + 1) * t)
                qt = q_ref[rows, :]
                dot_ = do_ref[rows, :]
                s = lax.dot_general(qt, kc, _NT, preferred_element_type=F32) * ATTN_SCALE_LOG2
                if i == j:
                    s = _causal_mask(s, t)
                p = jnp.exp2(s - lse_ref[rows, :][:, :1])
                dv = dv + lax.dot_general(p.astype(BF16), dot_, _TN, preferred_element_type=F32)
                dp = lax.dot_general(dot_, vt, _NT, preferred_element_type=F32)
                ds = (p * (dp - delta[i]) * ATTN_SCALE).astype(BF16)
                dk = dk + lax.dot_general(ds, qt, _TN, preferred_element_type=F32)
                dq_part = jnp.dot(ds, kc, preferred_element_type=F32)
                if j == 0:
                    dq_acc[rows, :] = dq_part
                else:
                    dq_acc[rows, :] += dq_part
            dkv_ref[cols, :QK_NOPE] = dk[:, :QK_NOPE].astype(BF16)
            dkv_ref[cols, QK_NOPE:] = dv.astype(BF16)
            dkr_ref[cols, :] += dk[:, QK_NOPE:]
        dq_ref[...] = dq_acc[...].astype(BF16)

    seq_q = pl.BlockSpec((S, HEAD_W), lambda b, h: (b, h))
    seq_h = pl.BlockSpec((S, LANE), lambda b, h: (b, h))
    seq_shared = pl.BlockSpec((S, LANE), lambda b, h: (b, 0))
    return pl.pallas_call(
        body, name=name,
        out_shape=(jax.ShapeDtypeStruct((T, HQ), BF16), jax.ShapeDtypeStruct((T, HQ), BF16),
                   jax.ShapeDtypeStruct((T, LANE), F32)),
        grid=(B, H),
        in_specs=[seq_q,
                  pl.BlockSpec((S, QK_NOPE), lambda b, h: (b, 2 * h)),
                  pl.BlockSpec((S, V_DIM), lambda b, h: (b, 2 * h + 1)),
                  seq_shared, seq_h, seq_h, seq_h],
        out_specs=(seq_q, seq_q, seq_shared),
        scratch_shapes=[pltpu.VMEM((S, HEAD_W), BF16), pltpu.VMEM((S, HEAD_W), F32)],
        compiler_params=_params("parallel", "arbitrary"),
    )(q, kv, kv, kr, o, do, lse)


def _merge_out(h, ms, o, x, win_t, bgate, wpp, wap, wout, next_gain, *, name):
    T, D = x.shape
    tm = _tile(T, 256, 16)
    n_gate = win_t.shape[0] - 2 * D

    def body(h_ref, ms_ref, o_ref, x_ref, win_ref, bgate_ref, wpp_ref, wap_ref, wout_ref, ng_ref,
             gates_ref, ba_ref, bb_ref, merged_ref, xn_ref, hn_ref):
        logits = lax.dot_general(h_ref[...], win_ref[n_gate:, :], _NT, preferred_element_type=F32) + bgate_ref[...]
        gates = jax.nn.sigmoid(logits)
        ba = jnp.dot(ms_ref[...], wpp_ref[...], preferred_element_type=F32)
        bb = jnp.dot(o_ref[...], wap_ref[...], preferred_element_type=F32)
        merged = (gates[:, :D] * ba + gates[:, D:] * bb).astype(BF16)
        gates_ref[...] = gates.astype(BF16)
        ba_ref[...] = ba.astype(BF16)
        bb_ref[...] = bb.astype(BF16)
        merged_ref[...] = merged
        xn = x_ref[...] + jnp.dot(merged, wout_ref[...], preferred_element_type=F32)
        xn_ref[...] = xn
        hn_ref[...] = (xn * _rstd(xn) * ng_ref[...]).astype(BF16)

    def row(w):
        return pl.BlockSpec((tm, w), lambda i: (i, 0))

    def whole(arr):
        return pl.BlockSpec(arr.shape, lambda i: (0,) * arr.ndim, pipeline_mode=pl.Buffered(1))

    bg2, ng2 = bgate.reshape(1, 2 * D), next_gain.reshape(1, D)
    act = jax.ShapeDtypeStruct((T, D), BF16)
    return pl.pallas_call(
        body, name=name,
        out_shape=(jax.ShapeDtypeStruct((T, 2 * D), BF16), act, act, act, jax.ShapeDtypeStruct((T, D), F32), act),
        grid=(T // tm,),
        in_specs=[row(D), row(ms.shape[1]), row(o.shape[1]), row(D), whole(win_t), whole(bg2), whole(wpp),
                  whole(wap), whole(wout), whole(ng2)],
        out_specs=(row(2 * D), row(D), row(D), row(D), row(D), row(D)),
        compiler_params=_params("parallel"),
    )(h, ms, o, x, win_t, bg2, wpp, wap, wout, ng2)


def _merge_bwd(dxo, wout, wpp, wap, gates, ba, bb, *, name, dep=None):
    T, D = dxo.shape
    tm = _tile(T, 512, 16)

    def body(dxo_ref, wout_ref, wpp_ref, wap_ref, gates_ref, ba_ref, bb_ref, *rest):
        dba_ref, dbb_ref, dgl_ref, dbg_ref, dms_ref, do_ref = rest[-6:]
        dm = lax.dot_general(dxo_ref[...].astype(BF16), wout_ref[...], _NT, preferred_element_type=F32)
        ga = gates_ref[:, :D].astype(F32)
        gb = gates_ref[:, D:].astype(F32)
        dba = (dm * ga).astype(BF16)
        dbb = (dm * gb).astype(BF16)
        dba_ref[...] = dba
        dbb_ref[...] = dbb
        dms_ref[...] = lax.dot_general(dba, wpp_ref[...], _NT, preferred_element_type=F32)
        do_ref[...] = lax.dot_general(dbb, wap_ref[...], _NT, preferred_element_type=F32).astype(BF16)
        dgl_a = dm * ba_ref[...].astype(F32) * (ga * (1.0 - ga))
        dgl_b = dm * bb_ref[...].astype(F32) * (gb * (1.0 - gb))
        dgl_ref[:, :D] = dgl_a.astype(BF16)
        dgl_ref[:, D:] = dgl_b.astype(BF16)
        sa = jnp.sum(dgl_a, axis=0, keepdims=True)
        sb = jnp.sum(dgl_b, axis=0, keepdims=True)

        @pl.when(pl.program_id(0) == 0)
        def _():
            dbg_ref[:, :D] = sa
            dbg_ref[:, D:] = sb

        @pl.when(pl.program_id(0) > 0)
        def _():
            dbg_ref[:, :D] += sa
            dbg_ref[:, D:] += sb

    def row(w):
        return pl.BlockSpec((tm, w), lambda i: (i, 0))

    def whole(arr):
        return pl.BlockSpec(arr.shape, lambda i: (0, 0))

    P, HV = wpp.shape[0], wap.shape[0]
    act = jax.ShapeDtypeStruct((T, D), BF16)
    return pl.pallas_call(
        body, name=name,
        out_shape=(act, act, jax.ShapeDtypeStruct((T, 2 * D), BF16), jax.ShapeDtypeStruct((1, 2 * D), F32),
                   jax.ShapeDtypeStruct((T, P), F32), jax.ShapeDtypeStruct((T, HV), BF16)),
        grid=(T // tm,),
        in_specs=[row(D), whole(wout), whole(wpp), whole(wap), row(2 * D), row(D), row(D)] + _dep_spec(dep),
        out_specs=(row(D), row(D), row(2 * D), pl.BlockSpec((1, 2 * D), lambda i: (0, 0)), row(P), row(HV)),
        compiler_params=_params("arbitrary"),
    )(dxo, wout, wpp, wap, gates, ba, bb, *([] if dep is None else [dep]))


def _ffn_fwd(x, h, w, tag, next_gain, dep=None):
    gate, up, a, xn, hn = _ffn_fwd_core(x, h, w["up_t"], w["wd"], next_gain, alpha=0.5,
                                        name=f"{tag}_fwd" if next_gain is not None else f"{tag}_fwd_last", dep=dep)
    return xn, hn, (x, h, gate, up, a)


def _ffn_bwd(dxo, gain, w, saved, tag, dep=None, early=None, mid=None):
    x, h, gate, up, a = saved
    dwd = _mm(a, dxo, ta=True, alpha=0.5, out_dtype=BF16, name=f"{tag}_dwd", tm=1408, tn=1024, tk=1024, dep=dep)
    if early is not None:
        dep = early(dwd)
    dgate, dup, dx, dgain = _ffn_bwd_core(dxo, w["wd"], w["up_t"], gate, up, x, gain, alpha=0.5,
                                          name=f"{tag}_bwd_core", dep=dep)
    dup_t = _ffn_dw_up(dgate, dup, h, name=f"{tag}_dw_up", dep=None if mid is None else mid(dgain))
    return dx, dgain, dup_t, dwd


def _mixer_fwd(x, h, p, w, tabs, S, next_gain, dep=None):
    xp, ql, kvl, qn, kvn, q, kv, kr = _mixer_in(h, w["win_t"], w["wuq_t"], w["wukv"], p["q_latent_norm"],
                                                 p["kv_latent_norm"], tabs, name="mix_in", dep=dep)
    ms = _pool_fwd(xp, p["pool_maps"], p["pool_scale"], S=S, name="pool_fwd")
    o, lse = _attn_fwd(q, kv, kr, S=S, name="attn_fwd")
    gates, ba, bb, merged, xn, hn = _merge_out(h, ms, o, x, w["win_t"], p["b_gate"], w["wpp"], w["wap"], w["wout"],
                                               next_gain, name="merge_out")
    return xn, hn, (x, h, xp, ql, kvl, qn, kvn, q, kv, kr, ms, o, lse, gates, ba, bb, merged)


def _mixer_bwd(dxo, p, w, tabs, saved, S, dep=None):
    x, h, xp, ql, kvl, qn, kvn, q, kv, kr, ms, o, lse, gates, ba, bb, merged = saved
    dba, dbb, dgl, dbg, dms, do = _merge_bwd(dxo, w["wout"], w["wpp"], w["wap"], gates, ba, bb, name="merge_bwd",
                                             dep=dep)
    g = {}
    g["wout"], g["wpp"], g["wap"] = _dw_multi([(merged, dxo), (ms, dba), (o, dbb)], name="d_w_merge")
    dxp, g["pool_maps"], g["pool_scale"] = _pool_bwd(xp, dms, p["pool_maps"], p["pool_scale"], S=S, name="pool_bwd")
    dq, dkv, dkr = _attn_bwd(q, kv, kr, o, do, lse, S=S, name="attn_bwd")
    dproj, dqp, g["q_latent_norm"], g["kv_latent_norm"], dx, g["norm_mix"] = _mixer_in_bwd(
        dq, dkv, dkr, ql, kvl, dxp, dgl, x, dxo, w["win_t"], w["wuq_t"], w["wukv"],
        p["norm_mix"], p["q_latent_norm"], p["kv_latent_norm"], tabs, name="mix_in_bwd")
    g["wuq_t"], g["wukv"] = _dw_multi([(dqp, qn), (kvn, dkv)], name="d_w_qkv", tk=1024)
    g["wa_t"], g["wgate_t"] = _dw_multi([(dproj, h), (dgl, h)], name="d_w_in")
    g["b_gate"] = dbg
    return dx, g


BIG = ("ffn1_up", "ffn1_down", "w_in", "w_pool_proj", "w_uq", "w_ukv", "w_attn_proj", "w_out", "ffn2_up", "ffn2_down")
SMALL = ("norm_ffn1", "norm_mix", "b_gate", "pool_maps", "pool_scale", "q_latent_norm", "kv_latent_norm", "norm_ffn2")
PACKED = ("w_pool_proj", "w_uq", "w_ukv")
TRANSPOSED = ("ffn1_up", "ffn2_up", "w_in", "w_uq")
COL_SHARDED = ("w_pool_proj", "w_ukv")
QK_HEAD = QK_NOPE + QK_ROPE


def _rows(stacked):
    n, r, c = stacked.shape
    return stacked.reshape(n * r, c)


def _cols(stacked):
    n, k, c = stacked.shape
    return stacked.transpose(1, 0, 2).reshape(k, n * c)


FFN1_PART = ("ffn1_up", "ffn1_down")
MIXER_PART = ("w_in", "w_attn_proj", "w_out") + PACKED
FFN2_PART = ("ffn2_up", "ffn2_down")


def _kernel_weights(stacked):
    full = {}
    for tag in ("ffn1", "ffn2"):
        if tag + "_up" in stacked:
            full[tag] = {"up_t": _rows(stacked[tag + "_up"]), "wd": _rows(stacked[tag + "_down"])}
    if "w_in" in stacked:
        win_t = _rows(stacked["w_in"])
        wuq_t = _rows(stacked["w_uq"])
        QL = wuq_t.shape[1]
        H = wuq_t.shape[0] // QK_HEAD
        wuq_t = jnp.pad(wuq_t.reshape(H, QK_HEAD, QL), ((0, 0), (0, HEAD_W - QK_HEAD), (0, 0)))
        full.update({"win_t": win_t, "wuq_t": wuq_t.reshape(H * HEAD_W, QL),
                     "wukv": _cols(stacked["w_ukv"]), "wpp": _cols(stacked["w_pool_proj"]),
                     "wap": _rows(stacked["w_attn_proj"]), "wout": _rows(stacked["w_out"])})
    return full


def _split_rows(full):
    return full.reshape(N_DEV, full.shape[0] // N_DEV, full.shape[1])


def _split_cols(full):
    k, cols = full.shape
    return full.reshape(k, N_DEV, cols // N_DEV).transpose(1, 0, 2)


def _mixer_grads_stacked(g):
    n_a = g["wa_t"].shape[0] - (LANE - QK_ROPE)
    HQ, QL = g["wuq_t"].shape
    H = HQ // HEAD_W
    wuq_t = g["wuq_t"].reshape(H, HEAD_W, QL)[:, :QK_HEAD].reshape(H * QK_HEAD, QL)
    return {"w_in": _split_rows(jnp.concatenate([g["wa_t"][:n_a], g["wgate_t"]], axis=0)),
            "w_uq": _split_rows(wuq_t),
            "w_pool_proj": _split_cols(g["wpp"]), "w_ukv": _split_cols(g["wukv"]),
            "w_attn_proj": _split_rows(g["wap"]), "w_out": _split_rows(g["wout"])}


def _mesh_place():
    x, y, c = lax.axis_index("x"), lax.axis_index("y"), lax.axis_index("c")
    chips = [(1 - x, y), (x, 1 - y), (1 - x, 1 - y)]
    return x, y, c, chips


HBM = pl.BlockSpec(memory_space=pltpu.HBM)
SEMAPHORES = pl.BlockSpec(memory_space=pltpu.SEMAPHORE)
DATAFLOW = pltpu.SideEffectType.DATAFLOW_SIDE_EFFECTING
GATHER_PEERS = 4
SCATTER_PEERS = 7


def _in_hbm(a):
    return pltpu.with_memory_space_constraint(a, pltpu.HBM)


def _gather_plan(src_refs, land_refs):
    x, y, c, chips = _mesh_place()
    me = 4 * x + 2 * y + c
    targets = [(x, y, 1 - c)] + [(cx, cy, c) for cx, cy in chips]
    return [(s, land.at[me], to) for s, land in zip(src_refs, land_refs) for to in targets]


def _scatter_plan(src_refs, land_refs):
    x, y, c, _ = _mesh_place()
    peers = [(x, y, 1 - c), (1 - x, y, c), (x, 1 - y, c), (1 - x, 1 - y, c),
             (1 - x, y, 1 - c), (x, 1 - y, 1 - c), (1 - x, 1 - y, 1 - c)]
    return [(s.at[4 * px + 2 * py + pc], land.at[k], (px, py, pc))
            for s, land in zip(src_refs, land_refs) for k, (px, py, pc) in enumerate(peers)]


def _descriptors(plan, src_refs, land_refs, send_sems, recv_sems):
    return [pltpu.make_async_remote_copy(src_ref=s, dst_ref=d, send_sem=send_sems.at[k], recv_sem=recv_sems.at[k],
                                         device_id=to, device_id_type=MESH)
            for k, (s, d, to) in enumerate(plan(src_refs, land_refs))]


FORWARD_COPIES = 4


def _forward_slots():
    x, y, c, chips = _mesh_place()
    return [4 * cx + 2 * cy + c for cx, cy in chips] + [4 * x + 2 * y + (1 - c)], (x, y, 1 - c)


def _forward_plan(src_refs, land_refs):
    slots, sibling = _forward_slots()
    return [(land.at[s], land.at[s], sibling) for land in land_refs for s in slots]


def _gather_all_plan(src_refs, land_refs):
    x, y, c, _ = _mesh_place()
    me = 4 * x + 2 * y + c
    peers = [(x, y, 1 - c), (1 - x, y, c), (x, 1 - y, c), (1 - x, 1 - y, c),
             (1 - x, y, 1 - c), (x, 1 - y, 1 - c), (1 - x, 1 - y, 1 - c)]
    return [(s, land.at[me], to) for s, land in zip(src_refs, land_refs) for to in peers]


def _exchange_start(srcs, lands, plan, n_copies, *, name):
    ns, n = len(srcs), len(srcs) + len(lands)

    def body(*refs):
        for cp in _descriptors(plan, refs[:ns], refs[ns:n], refs[n], refs[n + 1]):
            cp.start()
        refs[-1][...] = jnp.zeros_like(refs[-1])

    sems = pltpu.SemaphoreType.DMA((n_copies,))
    out = pl.pallas_call(
        body, name=name,
        out_shape=(sems, sems, *[pltpu.HBM(a.shape, a.dtype) for a in srcs + lands],
                   jax.ShapeDtypeStruct((8, LANE), F32)),
        in_specs=(HBM,) * n,
        out_specs=(SEMAPHORES, SEMAPHORES, *[HBM] * n, pl.BlockSpec(memory_space=pltpu.VMEM)),
        input_output_aliases={i: 2 + i for i in range(n)},
        compiler_params=pltpu.CompilerParams(has_side_effects=DATAFLOW),
    )(*[_in_hbm(a) for a in srcs + lands])
    return out[0], out[1], list(out[2:2 + ns]), list(out[2 + ns:2 + n]), out[-1]


def _exchange_wait(send_sems, recv_sems, srcs, lands, plan, after, *, name):
    ns, n = len(srcs), len(srcs) + len(lands)

    def body(*refs):
        for cp in _descriptors(plan, refs[:ns], refs[ns:n], refs[n], refs[n + 1]):
            cp.wait_send()
            cp.wait_recv()

    out = pl.pallas_call(
        body, name=name,
        out_shape=tuple(pltpu.HBM(a.shape, a.dtype) for a in srcs + lands),
        in_specs=(*[HBM] * n, SEMAPHORES, SEMAPHORES, ANY),
        out_specs=(HBM,) * n,
        input_output_aliases={i: i for i in range(n)},
        compiler_params=pltpu.CompilerParams(has_side_effects=DATAFLOW),
    )(*srcs, *lands, send_sems, recv_sems, after)
    return list(out[:ns]), list(out[ns:])


def _gather_forward(lands, *, name):
    n = len(lands)

    def body(*refs):
        in_refs, out_refs = refs[:n], refs[n:2 * n]
        token, send_sems, recv_sems = refs[2 * n:2 * n + 3]
        slots, sibling = _forward_slots()
        passed = [pltpu.make_async_remote_copy(
            src_ref=i.at[s], dst_ref=o.at[s],
            send_sem=send_sems.at[FORWARD_COPIES * b + j], recv_sem=recv_sems.at[FORWARD_COPIES * b + j],
            device_id=sibling, device_id_type=MESH)
            for b, (i, o) in enumerate(zip(in_refs, out_refs)) for j, s in enumerate(slots)]
        for cp in passed:
            cp.start()
        for cp in passed:
            cp.wait()
        token[...] = jnp.zeros_like(token)

    out = pl.pallas_call(
        body, name=name,
        out_shape=(*[jax.ShapeDtypeStruct(a.shape, a.dtype) for a in lands], jax.ShapeDtypeStruct((8, LANE), F32)),
        in_specs=[ANY] * n,
        out_specs=(*[ANY] * n, pl.BlockSpec(memory_space=pltpu.VMEM)),
        input_output_aliases={i: i for i in range(n)},
        scratch_shapes=[pltpu.SemaphoreType.DMA((FORWARD_COPIES * n,)), pltpu.SemaphoreType.DMA((FORWARD_COPIES * n,))],
    )(*lands)
    return list(out[:n]), out[n]


def _scatter_sum(parts, got, me, into, layer, *, name):
    shard = parts.shape[1:]
    cols = shard[-1]
    rows = int(np.prod(shard[:-1]))
    tr = _tile(rows, 1024, 16)
    layers = into.shape[0]

    def body(me_ref, p_ref, g_ref, into_ref, o_ref):
        acc = p_ref[...].astype(F32)
        for k in range(SCATTER_PEERS):
            acc = acc + g_ref[k].astype(F32)
        o_ref[...] = acc

    slab = pl.BlockSpec((None, tr, cols), lambda r, me_ref: (layer, r, 0))
    out = pl.pallas_call(
        body, name=name,
        out_shape=jax.ShapeDtypeStruct((layers, rows, cols), F32),
        grid_spec=pltpu.PrefetchScalarGridSpec(
            num_scalar_prefetch=1, grid=(rows // tr,),
            in_specs=[pl.BlockSpec((None, tr, cols), lambda r, me_ref: (me_ref[0], r, 0)),
                      pl.BlockSpec((SCATTER_PEERS, tr, cols), lambda r, me_ref: (0, r, 0)),
                      ANY],
            out_specs=slab),
        input_output_aliases={3: 0},
        compiler_params=_params("parallel"),
    )(me, parts.reshape(N_DEV, rows, cols), got.reshape(SCATTER_PEERS, rows, cols), into.reshape(layers, rows, cols))
    return out.reshape(layers, *shard)


def _sum_devices(parts, *, name):
    _, R, C = parts.shape
    tr = _tile(R, 512, 8)

    def body(p_ref, o_ref):
        acc = p_ref[0]
        for d in range(1, N_DEV):
            acc = acc + p_ref[d]
        o_ref[...] = acc

    return pl.pallas_call(
        body, name=name,
        out_shape=jax.ShapeDtypeStruct((R, C), F32),
        grid=(R // tr,),
        in_specs=[pl.BlockSpec((N_DEV, tr, C), lambda r: (0, r, 0))],
        out_specs=pl.BlockSpec((tr, C), lambda r: (r, 0)),
        compiler_params=_params("parallel"),
    )(parts)


def _adamw(w, g, m, v, *, name, dep=None):
    shape = w.shape
    cols = shape[-1]
    rows = w.size // cols
    tr = _tile(rows, 512, 8)

    def body(w_ref, g_ref, m_ref, v_ref, *rest):
        d_ref, nm_ref, nv_ref = rest[-3:]
        g = g_ref[...]
        m = ADAM_B1 * m_ref[...] + (1.0 - ADAM_B1) * g
        v = ADAM_B2 * v_ref[...] + (1.0 - ADAM_B2) * jnp.square(g)
        m_hat = m / (1.0 - ADAM_B1 ** ADAM_STEP)
        v_hat = v / (1.0 - ADAM_B2 ** ADAM_STEP)
        d_ref[...] = -ADAM_LR * (m_hat / (jnp.sqrt(v_hat) + ADAM_EPS) + ADAM_WD * w_ref[...])
        nm_ref[...] = m
        nv_ref[...] = v

    spec = pl.BlockSpec((tr, cols), lambda i: (i, 0))
    out = jax.ShapeDtypeStruct((rows, cols), F32)
    d, nm, nv = pl.pallas_call(
        body, name=name,
        out_shape=(out, out, out),
        grid=(rows // tr,),
        in_specs=[spec] * 4 + _dep_spec(dep), out_specs=(spec,) * 3,
        compiler_params=_params("parallel"),
    )(*(a.reshape(rows, cols) for a in (w, g, m, v)), *([] if dep is None else [dep]))
    return d.reshape(shape), nm.reshape(shape), nv.reshape(shape)


PACK_ALIGN = 16 * LANE


def _pack(pieces, lead):
    out = []
    for p in pieces:
        keep = p.shape[:lead]
        flat = p.reshape(*keep, -1)
        pad = (-flat.shape[-1]) % PACK_ALIGN
        if pad:
            flat = jnp.pad(flat, [(0, 0)] * lead + [(0, pad)])
        out.append(flat.reshape(*keep, -1, LANE))
    return jnp.concatenate(out, axis=lead)


def _unpack(buf, shapes, lead):
    keep = buf.shape[:lead]
    out, row = [], 0
    for shape in shapes:
        size = int(np.prod(shape))
        rows = -(-size // PACK_ALIGN) * (PACK_ALIGN // LANE)
        piece = lax.slice_in_dim(buf, row, row + rows, axis=lead).reshape(*keep, rows * LANE)
        out.append(lax.slice_in_dim(piece, 0, size, axis=lead).reshape(*keep, *shape))
        row += rows
    return out


def kernel(x, positions, norm_ffn1, ffn1_up, ffn1_down, norm_mix, w_in, b_gate, pool_maps, pool_scale, w_pool_proj, q_latent_norm, w_uq, kv_latent_norm, w_ukv, w_attn_proj, w_out, norm_ffn2, ffn2_up, ffn2_down, final_norm, loss_target, m_norm_ffn1, m_ffn1_up, m_ffn1_down, m_norm_mix, m_w_in, m_b_gate, m_pool_maps, m_pool_scale, m_w_pool_proj, m_q_latent_norm, m_w_uq, m_kv_latent_norm, m_w_ukv, m_w_attn_proj, m_w_out, m_norm_ffn2, m_ffn2_up, m_ffn2_down, m_final_norm, v_norm_ffn1, v_ffn1_up, v_ffn1_down, v_norm_mix, v_w_in, v_b_gate, v_pool_maps, v_pool_scale, v_w_pool_proj, v_q_latent_norm, v_w_uq, v_kv_latent_norm, v_w_ukv, v_w_attn_proj, v_w_out, v_norm_ffn2, v_ffn2_up, v_ffn2_down, v_final_norm):
    order = ("norm_ffn1", "ffn1_up", "ffn1_down", "norm_mix", "w_in", "b_gate", "pool_maps", "pool_scale",
             "w_pool_proj", "q_latent_norm", "w_uq", "kv_latent_norm", "w_ukv", "w_attn_proj", "w_out",
             "norm_ffn2", "ffn2_up", "ffn2_down", "final_norm")
    w = dict(zip(order, (norm_ffn1, ffn1_up, ffn1_down, norm_mix, w_in, b_gate, pool_maps, pool_scale, w_pool_proj,
                         q_latent_norm, w_uq, kv_latent_norm, w_ukv, w_attn_proj, w_out, norm_ffn2, ffn2_up,
                         ffn2_down, final_norm)))
    m = dict(zip(order, (m_norm_ffn1, m_ffn1_up, m_ffn1_down, m_norm_mix, m_w_in, m_b_gate, m_pool_maps, m_pool_scale,
                         m_w_pool_proj, m_q_latent_norm, m_w_uq, m_kv_latent_norm, m_w_ukv, m_w_attn_proj, m_w_out,
                         m_norm_ffn2, m_ffn2_up, m_ffn2_down, m_final_norm)))
    v = dict(zip(order, (v_norm_ffn1, v_ffn1_up, v_ffn1_down, v_norm_mix, v_w_in, v_b_gate, v_pool_maps, v_pool_scale,
                         v_w_pool_proj, v_q_latent_norm, v_w_uq, v_kv_latent_norm, v_w_ukv, v_w_attn_proj, v_w_out,
                         v_norm_ffn2, v_ffn2_up, v_ffn2_down, v_final_norm)))
    L = norm_ffn1.shape[0]
    B, S, D = x.shape
    T = B * S

    def turned(a, n):
        return a.transpose(0, 2, 1) if n in TRANSPOSED else a

    wk, mk, vk = ({n: turned(d[n], n) for n in order} for d in (w, m, v))
    packed_shapes = [wk[n].shape[1:] for n in PACKED]
    my_slot = 4 * lax.axis_index("x") + 2 * lax.axis_index("y") + lax.axis_index("c")
    me = jnp.stack([my_slot]).astype(jnp.int32)

    def weight_blocks(l, names, token):
        zero = token[0, 0].astype(BF16)
        blocks = [wk[n][l].astype(BF16) + zero for n in names if n not in PACKED]
        if any(n in PACKED for n in names):
            blocks.append(_pack([wk[n][l].astype(BF16) + zero for n in PACKED], 0))
        return blocks

    def kernel_weights(names, lands):
        direct = [n for n in names if n not in PACKED]
        stacked = dict(zip(direct, lands))
        if len(lands) > len(direct):
            stacked.update(zip(PACKED, _unpack(lands[-1], packed_shapes, 1)))
        return _kernel_weights(stacked)

    def gather_start(l, names, token, tag):
        blocks = weight_blocks(l, names, token)
        lands = [lax.empty((N_DEV, *b.shape), b.dtype) for b in blocks]
        send_sems, recv_sems, blocks, lands, token = _exchange_start(
            blocks, lands, _gather_plan, GATHER_PEERS * len(blocks), name=f"gather_start_{tag}")
        return (send_sems, recv_sems, blocks, lands, tag), token

    def gather_wait(state, after):
        send_sems, recv_sems, blocks, lands, tag = state
        return _exchange_wait(send_sems, recv_sems, blocks, lands, _gather_plan, after, name=f"gather_wait_{tag}")[1]

    layer_part = FFN1_PART + MIXER_PART + FFN2_PART
    tabs = _rope_tables(positions.reshape(T))
    xs = x.reshape(T, D)
    full, saved = [], []

    p = {n: w[n][0] for n in SMALL}
    state, token = gather_start(0, FFN1_PART, jnp.zeros((8, LANE), F32), "0_ffn1")
    h = _rms_fwd(xs, w["norm_ffn1"][0] + token[0, 0], name="first_norm")
    lands, token = _gather_forward(gather_wait(state, h), name="gather_forward")
    w0 = kernel_weights(FFN1_PART, lands)
    state, token = gather_start(0, MIXER_PART, token, "0_mix")
    xs, h, s1 = _ffn_fwd(xs, h, w0["ffn1"], "ffn1", p["norm_mix"], dep=token)
    lands, token = _gather_forward(gather_wait(state, xs), name="gather_forward")
    w0.update(kernel_weights(MIXER_PART, lands))
    state, token = gather_start(0, FFN2_PART, token, "0_ffn2")
    if L > 1:
        next_state, token = gather_start(1, layer_part, token, "1")
    xs, h, s2 = _mixer_fwd(xs, h, p, w0, tabs, S, p["norm_ffn2"], dep=token)
    lands, token = _gather_forward(gather_wait(state, xs), name="gather_forward")
    w0.update(kernel_weights(FFN2_PART, lands))
    xs, h, s3 = _ffn_fwd(xs, h, w0["ffn2"], "ffn2", w["norm_ffn1"][1] if L > 1 else None, dep=token)
    if L > 1:
        lands, token = _gather_forward(gather_wait(next_state, xs), name="gather_forward")
    full.append(w0)
    saved.append((s1, s2, s3))

    for l in range(1, L):
        full.append(kernel_weights(layer_part, lands))
        more = l + 1 < L
        p = {n: w[n][l] for n in SMALL}
        if more:
            state, token = gather_start(l + 1, layer_part, token, f"{l + 1}")
        xs, h, s1 = _ffn_fwd(xs, h, full[l]["ffn1"], "ffn1", p["norm_mix"], dep=token if more else None)
        xs, h, s2 = _mixer_fwd(xs, h, p, full[l], tabs, S, p["norm_ffn2"])
        if more:
            lands = gather_wait(state, xs)
            send_sems, recv_sems, _, lands, token = _exchange_start(
                [], lands, _forward_plan, FORWARD_COPIES * len(lands), name=f"forward_start_{l + 1}")
        xs, h, s3 = _ffn_fwd(xs, h, full[l]["ffn2"], "ffn2", w["norm_ffn1"][l + 1] if more else None,
                             dep=token if more else None)
        if more:
            _, lands = _exchange_wait(send_sems, recv_sems, [], lands, _forward_plan, xs, name=f"forward_wait_{l + 1}")
        saved.append((s1, s2, s3))
    dx, dfinal, loss = _loss_head(xs, final_norm, loss_target.reshape(T, D), name="loss_head")

    big_grads = {n: [None] * L if n in PACKED else lax.empty((L, *wk[n].shape[1:]), F32) for n in BIG}
    small_grads_of = [None] * L
    pending = None

    def scatter_start(names, stacked, tag):
        srcs = [stacked[n] for n in names if n not in PACKED]
        if any(n in PACKED for n in names):
            srcs.append(_pack([stacked[n] for n in PACKED], 1))
        lands = [lax.empty((SCATTER_PEERS, *s.shape[1:]), s.dtype) for s in srcs]
        send_sems, recv_sems, srcs, lands, token = _exchange_start(
            srcs, lands, _scatter_plan, SCATTER_PEERS * len(srcs), name=f"scatter_start_{tag}")
        return (names, send_sems, recv_sems, srcs, lands, tag), token

    def scatter_finish(state, after, l):
        names, send_sems, recv_sems, srcs, lands, tag = state
        srcs, got = _exchange_wait(send_sems, recv_sems, srcs, lands, _scatter_plan, after, name=f"scatter_wait_{tag}")
        direct = [n for n in names if n not in PACKED]
        for n, s, g in zip(direct, srcs, got):
            big_grads[n] = _scatter_sum(s, g, me, big_grads[n], l, name="scatter_sum")
        if len(srcs) > len(direct):
            packed = _scatter_sum(srcs[-1], got[-1], me, lax.empty((1, *srcs[-1].shape[1:]), F32), 0,
                                  name="scatter_sum")[0]
            for n, g in zip(PACKED, _unpack(packed, packed_shapes, 0)):
                big_grads[n][l] = g

    dep = None
    for l in reversed(range(L)):
        p = {n: w[n][l] for n in SMALL}
        s1, s2, s3 = saved[l]
        small_g = {}
        dx, small_g["norm_ffn2"], dup_t, dwd = _ffn_bwd(dx, p["norm_ffn2"], full[l]["ffn2"], s3, "ffn2", dep=dep)
        if pending is not None:
            scatter_finish(pending[0], dx, pending[1])
        stacked = {"ffn2_up": _split_rows(dup_t), "ffn2_down": _split_rows(dwd)}
        state, dep = scatter_start(("ffn2_up", "ffn2_down"), stacked, f"ffn2_{l}")
        pending = (state, l)

        dx, gm = _mixer_bwd(dx, p, full[l], tabs, s2, S, dep=dep)
        scatter_finish(pending[0], dx, pending[1])
        names = ("w_in", "w_attn_proj", "w_out") + PACKED
        state, dep = scatter_start(names, _mixer_grads_stacked(gm), f"mix_{l}")
        pending = (state, l)
        small_g.update({n: gm[n] for n in SMALL if n in gm})

        if l > 0:
            dx, small_g["norm_ffn1"], dup_t, dwd = _ffn_bwd(dx, p["norm_ffn1"], full[l]["ffn1"], s1, "ffn1", dep=dep)
            scatter_finish(pending[0], dx, pending[1])
            stacked = {"ffn1_up": _split_rows(dup_t), "ffn1_down": _split_rows(dwd)}
            state, dep = scatter_start(("ffn1_up", "ffn1_down"), stacked, f"ffn1_{l}")
            pending = (state, l)
        else:
            early_states = []

            def send_down(dwd):
                state, token = scatter_start(("ffn1_down",), {"ffn1_down": _split_rows(dwd)}, "ffn1_down_0")
                early_states.append(state)
                return token

            def send_small(dgain):
                small_g["norm_ffn1"] = dgain
                small_grads_of[0] = small_g
                parts = [small_grads_of[k][n] for k in range(L) for n in SMALL] + [dfinal, loss[0, :1]]
                vec = _pack([jnp.concatenate([a.reshape(-1) for a in parts])], 0)
                out = _exchange_start([vec], [lax.empty((N_DEV, *vec.shape), F32)], _gather_all_plan, SCATTER_PEERS,
                                      name="small_start")
                early_states.append((out, [a.shape for a in parts]))
                return out[4]

            dx, _, dup_t, _ = _ffn_bwd(dx, p["norm_ffn1"], full[l]["ffn1"], s1, "ffn1", dep=dep, early=send_down,
                                       mid=send_small)
            last_mixer, last_down = pending, (early_states[0], 0)
            (small_send, small_recv, vec_thru, small_land, _), small_shapes = early_states[1]
            state, dep = scatter_start(("ffn1_up",), {"ffn1_up": _split_rows(dup_t)}, "ffn1_up_0")
            pending = (state, l)
        small_grads_of[l] = small_g
    grad_x = dx.reshape(B, S, D)

    gk, grad, delta, new_m, new_v = {}, {}, {}, {}, {}

    def update(n, dep=None):
        wn, gn, mn, vn = (a.reshape(1, -1) if a.ndim == 1 else a for a in (wk[n], gk[n], mk[n], vk[n]))
        d, nm, nv = _adamw(wn, gn, mn, vn, name="adamw_" + n, dep=dep)
        grad[n] = turned(gk[n], n)
        delta[n], new_m[n], new_v[n] = (turned(a.reshape(wk[n].shape), n) for a in (d, nm, nv))

    for i, n in enumerate(FFN2_PART):
        gk[n] = jnp.stack(big_grads[n]) if n in PACKED else big_grads[n]
        update(n, dep if i == 0 else None)
    scatter_finish(last_mixer[0], new_v["ffn2_down"], last_mixer[1])
    for n in MIXER_PART:
        gk[n] = jnp.stack(big_grads[n]) if n in PACKED else big_grads[n]
        update(n)
    scatter_finish(last_down[0], new_v[MIXER_PART[-1]], last_down[1])
    scatter_finish(pending[0], new_v[MIXER_PART[-1]], pending[1])
    for n in FFN1_PART:
        gk[n] = jnp.stack(big_grads[n]) if n in PACKED else big_grads[n]
        update(n)

    vec_thru, small_land = _exchange_wait(small_send, small_recv, vec_thru, small_land, _gather_all_plan,
                                          new_v["ffn1_down"], name="small_wait")
    parts = lax.dynamic_update_index_in_dim(small_land[0], vec_thru[0], my_slot, 0)
    flat = _sum_devices(parts, name="sum_small").reshape(-1)
    small_grads, at = [], 0
    for shape in small_shapes:
        size = int(np.prod(shape))
        small_grads.append(lax.slice_in_dim(flat, at, at + size).reshape(shape))
        at += size
    loss_total = small_grads[-1].reshape(())
    for i, n in enumerate(SMALL):
        gk[n] = jnp.stack([small_grads[l * len(SMALL) + i] for l in range(L)]).reshape(w[n].shape)
        update(n)
    gk["final_norm"] = small_grads[-2].reshape(final_norm.shape)
    update("final_norm")
    return (loss_total, grad_x, *[grad[n] for n in order], *[delta[n] for n in order],
            *[new_m[n] for n in order], *[new_v[n] for n in order])
```

```python
import numpy as np
import jax
import jax.numpy as jnp
from jax import lax
from jax.experimental import pallas as pl
from jax.experimental.pallas import tpu as pltpu

F32 = jnp.float32
BF16 = jnp.bfloat16

NORM_EPS = 1e-6
ROPE_THETA = 10000.0
QK_NOPE = 128
QK_ROPE = 64
V_DIM = 128
HEAD_W = 256
POOL_WINDOWS = (2, 4, 8, 16)
POOL_G = 128
POOL_DIM = 512
LANE = 128
ATTN_SCALE = float((QK_NOPE + QK_ROPE) ** -0.5)
ATTN_SCALE_LOG2 = ATTN_SCALE * float(np.log2(np.e))
MASK_VALUE = -1e30
ATTN_TILE = 512

ADAM_LR = 0.001
ADAM_B1 = 0.9
ADAM_B2 = 0.999
ADAM_EPS = 1e-08
ADAM_WD = 0.01
ADAM_STEP = 10

N_DEV = 8
VMEM_LIMIT = 52 * 1024 * 1024

MESH = pl.DeviceIdType.MESH
ANY = pl.BlockSpec(memory_space=pl.ANY)


def _tile(dim, target, align=LANE):
    if dim <= target:
        return dim
    t = (target // align) * align
    while t >= align:
        if dim % t == 0:
            return t
        t -= align
    return dim


def _params(*sem):
    return pltpu.CompilerParams(dimension_semantics=sem, vmem_limit_bytes=VMEM_LIMIT)


def _rstd(x):
    return lax.rsqrt(jnp.mean(x * x, axis=-1, keepdims=True) + NORM_EPS)


def _mm(a, b, *, name, ta=False, tb=False, out_dtype=F32, alpha=1.0, tm=512, tn=1024, tk=1024, dep=None):
    if ta:
        K, M = a.shape
    else:
        M, K = a.shape
    if tb:
        N, K2 = b.shape
    else:
        K2, N = b.shape
    assert K == K2, (a.shape, b.shape, ta, tb)
    tm, tn, tk = _tile(M, tm), _tile(N, tn), _tile(K, tk)
    nk = K // tk
    dims = (((0 if ta else 1,), (1 if tb else 0,)), ((), ()))

    def body(a_ref, b_ref, *rest):
        o_ref = rest[0 if dep is None else 1]
        acc_ref = rest[-1] if nk > 1 else None
        part = lax.dot_general(a_ref[...].astype(BF16), b_ref[...].astype(BF16), dims,
                               preferred_element_type=F32)

        def finish(acc):
            o_ref[...] = (acc * alpha if alpha != 1.0 else acc).astype(out_dtype)

        if nk == 1:
            finish(part)
        else:
            k = pl.program_id(2)

            @pl.when(k == 0)
            def _():
                acc_ref[...] = part

            @pl.when(k > 0)
            def _():
                acc_ref[...] += part

            @pl.when(k == nk - 1)
            def _():
                finish(acc_ref[...])

    a_spec = pl.BlockSpec((tk, tm), lambda i, j, k: (k, i)) if ta else pl.BlockSpec((tm, tk), lambda i, j, k: (i, k))
    b_spec = pl.BlockSpec((tn, tk), lambda i, j, k: (j, k)) if tb else pl.BlockSpec((tk, tn), lambda i, j, k: (k, j))
    return pl.pallas_call(
        body, name=name,
        out_shape=jax.ShapeDtypeStruct((M, N), out_dtype),
        grid=(M // tm, N // tn, nk),
        in_specs=[a_spec, b_spec] + _dep_spec(dep),
        out_specs=pl.BlockSpec((tm, tn), lambda i, j, k: (i, j)),
        scratch_shapes=[pltpu.VMEM((tm, tn), F32)] if nk > 1 else [],
        compiler_params=_params("parallel", "parallel", "arbitrary"),
    )(a, b, *([] if dep is None else [dep]))


def _dw_multi(pairs, *, name, tk=512):
    n = len(pairs)
    T = pairs[0][0].shape[0]
    tk = _tile(T, tk, 16)
    nk = T // tk
    shapes = [(a.shape[1], b.shape[1]) for a, b in pairs]

    def body(*refs):
        ins, outs, accs = refs[:2 * n], refs[2 * n:3 * n], refs[3 * n:]
        k = pl.program_id(0)
        parts = [lax.dot_general(ins[2 * i][...].astype(BF16), ins[2 * i + 1][...].astype(BF16), _TN,
                                 preferred_element_type=F32) for i in range(n)]

        @pl.when(k == 0)
        def _():
            for acc, part in zip(accs, parts):
                acc[...] = part

        @pl.when(k > 0)
        def _():
            for acc, part in zip(accs, parts):
                acc[...] += part

        @pl.when(k == nk - 1)
        def _():
            for out, acc in zip(outs, accs):
                out[...] = acc[...].astype(BF16)

    return pl.pallas_call(
        body, name=name,
        out_shape=tuple(jax.ShapeDtypeStruct(s, BF16) for s in shapes),
        grid=(nk,),
        in_specs=[pl.BlockSpec((tk, x.shape[1]), lambda k: (k, 0)) for pair in pairs for x in pair],
        out_specs=tuple(pl.BlockSpec(s, lambda k: (0, 0)) for s in shapes),
        scratch_shapes=[pltpu.VMEM(s, F32) for s in shapes],
        compiler_params=_params("arbitrary"),
    )(*[x for pair in pairs for x in pair])


def _rms_fwd(x, g, *, name):
    T, D = x.shape
    tm = _tile(T, 512, 16)

    def body(x_ref, g_ref, h_ref):
        x = x_ref[...]
        h_ref[...] = (x * _rstd(x) * g_ref[...]).astype(BF16)

    return pl.pallas_call(
        body, name=name,
        out_shape=jax.ShapeDtypeStruct((T, D), BF16),
        grid=(T // tm,),
        in_specs=[pl.BlockSpec((tm, D), lambda i: (i, 0)), pl.BlockSpec((1, D), lambda i: (0, 0))],
        out_specs=pl.BlockSpec((tm, D), lambda i: (i, 0)),
        compiler_params=_params("parallel"),
    )(x, g.reshape(1, D))


def _loss_head(x, g, target, *, name):
    T, D = x.shape
    tm = _tile(T, 512, 16)

    def body(x_ref, g_ref, t_ref, dx_ref, dg_ref, loss_ref):
        x = x_ref[...]
        gain = g_ref[...]
        r = _rstd(x)
        xhat = x * r
        err = xhat * gain - t_ref[...]
        dy = err * (1.0 / D)
        dxh = dy * gain
        dx_ref[...] = r * (dxh - xhat * jnp.mean(dxh * xhat, axis=-1, keepdims=True))
        dg_part = jnp.sum(dy * xhat, axis=0, keepdims=True)
        loss_part = jnp.full((1, LANE), 0.5 / D, F32) * jnp.sum(err * err)

        @pl.when(pl.program_id(0) == 0)
        def _():
            dg_ref[...] = dg_part
            loss_ref[...] = loss_part

        @pl.when(pl.program_id(0) > 0)
        def _():
            dg_ref[...] += dg_part
            loss_ref[...] += loss_part

    row = pl.BlockSpec((tm, D), lambda i: (i, 0))
    vec = pl.BlockSpec((1, D), lambda i: (0, 0))
    return pl.pallas_call(
        body, name=name,
        out_shape=(jax.ShapeDtypeStruct((T, D), F32), jax.ShapeDtypeStruct((1, D), F32),
                   jax.ShapeDtypeStruct((1, LANE), F32)),
        grid=(T // tm,),
        in_specs=[row, vec, row],
        out_specs=(row, vec, pl.BlockSpec((1, LANE), lambda i: (0, 0))),
        compiler_params=_params("arbitrary"),
    )(x, g.reshape(1, D), target)


def _ffn_fwd_core(x, h, w_up_t, wd, next_gain, *, alpha, name, dep=None):
    T, D = x.shape
    F = wd.shape[0]
    tm = _tile(T, 256, 16)
    has_norm = next_gain is not None

    def body(x_ref, h_ref, wg_ref, wu_ref, wd_ref, *rest):
        outs = rest[len(rest) - (5 if has_norm else 4):]
        gate_ref, up_ref, a_ref, xn_ref = outs[:4]
        h = h_ref[...]
        gate = lax.dot_general(h, wg_ref[...], _NT, preferred_element_type=F32)
        up = lax.dot_general(h, wu_ref[...], _NT, preferred_element_type=F32)
        a = (gate * jax.nn.sigmoid(gate) * up).astype(BF16)
        gate_ref[...] = gate.astype(BF16)
        up_ref[...] = up.astype(BF16)
        a_ref[...] = a
        xn = x_ref[...] + alpha * jnp.dot(a, wd_ref[...], preferred_element_type=F32)
        xn_ref[...] = xn
        if has_norm:
            outs[4][...] = (xn * _rstd(xn) * rest[0][...]).astype(BF16)

    once = pl.Buffered(1)
    row_d = pl.BlockSpec((tm, D), lambda i: (i, 0))
    row_f = pl.BlockSpec((tm, F), lambda i: (i, 0))
    vec = pl.BlockSpec((1, D), lambda i: (0, 0))
    act = jax.ShapeDtypeStruct((T, F), BF16)
    operands = [x, h, w_up_t, w_up_t, wd] + ([next_gain.reshape(1, D)] if has_norm else [])
    out = pl.pallas_call(
        body, name=name,
        out_shape=(act, act, act, jax.ShapeDtypeStruct((T, D), F32)) + ((jax.ShapeDtypeStruct((T, D), BF16),) if has_norm else ()),
        grid=(T // tm,),
        in_specs=[row_d, row_d,
                  pl.BlockSpec((F, D), lambda i: (0, 0), pipeline_mode=once),
                  pl.BlockSpec((F, D), lambda i: (1, 0), pipeline_mode=once),
                  pl.BlockSpec((F, D), lambda i: (0, 0), pipeline_mode=once)] + ([vec] if has_norm else []) + _dep_spec(dep),
        out_specs=(row_f, row_f, row_f, row_d) + ((row_d,) if has_norm else ()),
        compiler_params=_params("parallel"),
    )(*operands, *([] if dep is None else [dep]))
    return out if has_norm else (*out, None)


def _ffn_bwd_core(dxo, wd, w_up_t, gate, up, x, gain, *, alpha, name, dep=None):
    T, D = dxo.shape
    F = wd.shape[0]
    tm = _tile(T, 256, 16)

    def body(dxo_ref, wd_ref, wg_ref, wu_ref, gate_ref, up_ref, x_ref, g_ref, *rest):
        dgate_ref, dup_ref, dx_ref, dg_ref = rest[-4:]
        dxo = dxo_ref[...]
        da = lax.dot_general(dxo.astype(BF16), wd_ref[...], _NT, preferred_element_type=F32) * alpha
        gate = gate_ref[...].astype(F32)
        up = up_ref[...].astype(F32)
        sig = jax.nn.sigmoid(gate)
        dgate = (da * up * (sig * (1.0 + gate * (1.0 - sig)))).astype(BF16)
        dup = (da * (gate * sig)).astype(BF16)
        dgate_ref[...] = dgate
        dup_ref[...] = dup
        dh = (jnp.dot(dgate, wg_ref[...], preferred_element_type=F32)
              + jnp.dot(dup, wu_ref[...], preferred_element_type=F32))
        x = x_ref[...]
        r = _rstd(x)
        xhat = x * r
        dxh = dh * g_ref[...]
        dx_ref[...] = dxo + r * (dxh - xhat * jnp.mean(dxh * xhat, axis=-1, keepdims=True))
        part = jnp.sum(dh * xhat, axis=0, keepdims=True)

        @pl.when(pl.program_id(0) == 0)
        def _():
            dg_ref[...] = part

        @pl.when(pl.program_id(0) > 0)
        def _():
            dg_ref[...] += part

    once = pl.Buffered(1)
    row_d = pl.BlockSpec((tm, D), lambda i: (i, 0))
    row_f = pl.BlockSpec((tm, F), lambda i: (i, 0))
    vec = pl.BlockSpec((1, D), lambda i: (0, 0))
    act = jax.ShapeDtypeStruct((T, F), BF16)
    return pl.pallas_call(
        body, name=name,
        out_shape=(act, act, jax.ShapeDtypeStruct((T, D), F32), jax.ShapeDtypeStruct((1, D), F32)),
        grid=(T // tm,),
        in_specs=[row_d,
                  pl.BlockSpec((F, D), lambda i: (0, 0), pipeline_mode=once),
                  pl.BlockSpec((F, D), lambda i: (0, 0), pipeline_mode=once),
                  pl.BlockSpec((F, D), lambda i: (1, 0), pipeline_mode=once),
                  row_f, row_f, row_d, vec] + _dep_spec(dep),
        out_specs=(row_f, row_f, row_d, vec),
        compiler_params=_params("arbitrary"),
    )(dxo, wd, w_up_t, w_up_t, gate, up, x, gain.reshape(1, D), *([] if dep is None else [dep]))


def _ffn_dw_up(dgate, dup, h, *, name, dep=None):
    T, F = dgate.shape
    D = h.shape[1]
    tm, tk = _tile(F, 1408), _tile(T, 1024, 16)
    nf, nk = F // tm, T // tk

    def body(dgate_ref, dup_ref, h_ref, *rest):
        o_ref, acc_ref = rest[-2:]
        i, k = pl.program_id(0), pl.program_id(1)

        def accumulate(part):
            @pl.when(k == 0)
            def _():
                acc_ref[...] = part

            @pl.when(k > 0)
            def _():
                acc_ref[...] += part

        @pl.when(i < nf)
        def _():
            accumulate(lax.dot_general(dgate_ref[...], h_ref[...], _TN, preferred_element_type=F32))

        @pl.when(i >= nf)
        def _():
            accumulate(lax.dot_general(dup_ref[...], h_ref[...], _TN, preferred_element_type=F32))

        @pl.when(k == nk - 1)
        def _():
            o_ref[...] = acc_ref[...].astype(BF16)

    return pl.pallas_call(
        body, name=name,
        out_shape=jax.ShapeDtypeStruct((2 * F, D), BF16),
        grid=(2 * nf, nk),
        in_specs=[pl.BlockSpec((tk, tm), lambda i, k: (jnp.where(i < nf, k, nk - 1), jnp.minimum(i, nf - 1))),
                  pl.BlockSpec((tk, tm), lambda i, k: (jnp.where(i < nf, 0, k), jnp.maximum(i - nf, 0))),
                  pl.BlockSpec((tk, D), lambda i, k: (k, 0))] + _dep_spec(dep),
        out_specs=pl.BlockSpec((tm, D), lambda i, k: (i, 0)),
        scratch_shapes=[pltpu.VMEM((tm, D), F32)],
        compiler_params=_params("parallel", "arbitrary"),
    )(dgate, dup, h, *([] if dep is None else [dep]))


def _dep_spec(dep):
    return [] if dep is None else [pl.BlockSpec(dep.shape, lambda *_: (0,) * dep.ndim)]


def _rope_tables(positions):
    half = QK_ROPE // 2
    inv_freq = ROPE_THETA ** (-jnp.arange(0, QK_ROPE, 2, dtype=F32) / QK_ROPE)
    ang = positions.astype(F32)[:, None] * inv_freq
    cos, sin = jnp.cos(ang), jnp.sin(ang)
    z = jnp.zeros_like(cos)
    zz = jnp.zeros((positions.shape[0], LANE - QK_ROPE), F32)
    c = jnp.concatenate([cos, cos, zz], axis=1)
    sa = jnp.concatenate([z, sin, zz], axis=1)
    sb = jnp.concatenate([-sin, z, zz], axis=1)
    return c, sa, sb


def _rotate(seg, c, sa, sb, sign):
    half = QK_ROPE // 2
    mix = pltpu.roll(seg, half, 1) * sa + pltpu.roll(seg, LANE - half, 1) * sb
    return seg * c + mix if sign > 0 else seg * c - mix


def _mixer_in(h, wa, wuq, wukv, gq, gkv, tabs, *, name, dep=None):
    T, D = h.shape
    HQ, QL = wuq.shape
    KVL = wukv.shape[0]
    H = HQ // HEAD_W
    o_q, o_kv, o_kr = POOL_DIM, POOL_DIM + QL, POOL_DIM + QL + KVL
    PA = o_kr + LANE
    assert wa.shape[0] >= PA
    tm = _tile(T, 512, 16)

    def body(h_ref, wa_ref, wuq_ref, wukv_ref, gq_ref, gkv_ref, c_ref, sa_ref, sb_ref, *rest):
        xp_ref, ql_ref, kvl_ref, qn_ref, kvn_ref, q_ref, kv_ref, kr_ref = rest[-8:]
        proj = lax.dot_general(h_ref[...], wa_ref[...], _NT, preferred_element_type=F32)
        xp_ref[...] = proj[:, :POOL_DIM]
        ql = proj[:, o_q:o_kv]
        kvl = proj[:, o_kv:o_kr]
        ql_ref[...] = ql
        kvl_ref[...] = kvl
        qn = (ql * _rstd(ql) * gq_ref[...]).astype(BF16)
        kvn = (kvl * _rstd(kvl) * gkv_ref[...]).astype(BF16)
        qn_ref[...] = qn
        kvn_ref[...] = kvn
        c, sa, sb = c_ref[...], sa_ref[...], sb_ref[...]
        q = lax.dot_general(qn, wuq_ref[...], _NT, preferred_element_type=F32)
        for hh in range(H):
            base = hh * HEAD_W
            q_ref[:, base:base + QK_NOPE] = q[:, base:base + QK_NOPE].astype(BF16)
            q_ref[:, base + QK_NOPE:base + HEAD_W] = _rotate(
                q[:, base + QK_NOPE:base + HEAD_W], c, sa, sb, 1).astype(BF16)
        kv_ref[...] = jnp.dot(kvn, wukv_ref[...], preferred_element_type=F32).astype(BF16)
        kr_ref[...] = _rotate(proj[:, o_kr:o_kr + LANE], c, sa, sb, 1).astype(BF16)

    def row(w):
        return pl.BlockSpec((tm, w), lambda i: (i, 0))

    def whole(arr):
        return pl.BlockSpec(arr.shape, lambda i: (0,) * arr.ndim)

    gq2, gkv2 = gq.reshape(1, QL), gkv.reshape(1, KVL)
    outs = [(POOL_DIM, F32), (QL, F32), (KVL, F32), (QL, BF16), (KVL, BF16), (HQ, BF16), (HQ, BF16), (LANE, BF16)]
    return pl.pallas_call(
        body, name=name,
        out_shape=tuple(jax.ShapeDtypeStruct((T, w), dt) for w, dt in outs),
        grid=(T // tm,),
        in_specs=[row(D), pl.BlockSpec((PA, D), lambda i: (0, 0)), whole(wuq), whole(wukv), whole(gq2), whole(gkv2),
                  row(LANE), row(LANE), row(LANE)] + _dep_spec(dep),
        out_specs=tuple(row(w) for w, _ in outs),
        compiler_params=_params("parallel"),
    )(h, wa, wuq, wukv, gq2, gkv2, *tabs, *([] if dep is None else [dep]))


def _mixer_in_bwd(dq, dkv, dkr, ql, kvl, dxp, dgl, x, dxo, win_t, wuq, wukv, g_mix, gq, gkv, tabs, *, name):
    T, HQ = dq.shape
    D = x.shape[1]
    QL, KVL = wuq.shape[1], wukv.shape[0]
    H = HQ // HEAD_W
    PA = POOL_DIM + QL + KVL + LANE
    o_q, o_kv, o_kr = POOL_DIM, POOL_DIM + QL, POOL_DIM + QL + KVL
    tm = _tile(T, 256, 16)

    def norm_bwd(lat, gain, dn):
        r = _rstd(lat)
        xhat = lat * r
        dxh = dn * gain
        dlat = r * (dxh - xhat * jnp.mean(dxh * xhat, axis=-1, keepdims=True))
        return dlat, jnp.sum(dn * xhat, axis=0, keepdims=True)

    def body(dq_ref, dkv_ref, dkr_ref, ql_ref, kvl_ref, dxp_ref, dgl_ref, x_ref, dxo_ref, win_ref,
             wuq_ref, wukv_ref, gmix_ref, gq_ref, gkv_ref, c_ref, sa_ref, sb_ref,
             dproj_ref, dqp_ref, dgq_ref, dgkv_ref, dx_ref, dgmix_ref):
        c, sa, sb = c_ref[...], sa_ref[...], sb_ref[...]
        for hh in range(H):
            base = hh * HEAD_W
            dqp_ref[:, base:base + QK_NOPE] = dq_ref[:, base:base + QK_NOPE]
            dqp_ref[:, base + QK_NOPE:base + HEAD_W] = _rotate(
                dq_ref[:, base + QK_NOPE:base + HEAD_W].astype(F32), c, sa, sb, -1).astype(BF16)
        dqn = jnp.dot(dqp_ref[...], wuq_ref[...], preferred_element_type=F32)
        dkvn = lax.dot_general(dkv_ref[...], wukv_ref[...], _NT, preferred_element_type=F32)
        dql, dgq = norm_bwd(ql_ref[...], gq_ref[...], dqn)
        dkvl, dgkv = norm_bwd(kvl_ref[...], gkv_ref[...], dkvn)
        dproj_ref[:, :POOL_DIM] = dxp_ref[...].astype(BF16)
        dproj_ref[:, o_q:o_kv] = dql.astype(BF16)
        dproj_ref[:, o_kv:o_kr] = dkvl.astype(BF16)
        dproj_ref[:, o_kr:PA] = _rotate(dkr_ref[...], c, sa, sb, -1).astype(BF16)

        n_gate = win_ref.shape[0] - 2 * D
        dh = (jnp.dot(dproj_ref[...], win_ref[:PA, :], preferred_element_type=F32)
              + jnp.dot(dgl_ref[...], win_ref[n_gate:, :], preferred_element_type=F32))
        x = x_ref[...]
        r = _rstd(x)
        xhat = x * r
        dxh = dh * gmix_ref[...]
        dx_ref[...] = dxo_ref[...] + r * (dxh - xhat * jnp.mean(dxh * xhat, axis=-1, keepdims=True))
        dgmix = jnp.sum(dh * xhat, axis=0, keepdims=True)

        @pl.when(pl.program_id(0) == 0)
        def _():
            dgq_ref[...] = dgq
            dgkv_ref[...] = dgkv
            dgmix_ref[...] = dgmix

        @pl.when(pl.program_id(0) > 0)
        def _():
            dgq_ref[...] += dgq
            dgkv_ref[...] += dgkv
            dgmix_ref[...] += dgmix

    def row(w):
        return pl.BlockSpec((tm, w), lambda i: (i, 0))

    def resident(arr):
        return pl.BlockSpec(arr.shape, lambda i: (0, 0), pipeline_mode=pl.Buffered(1))

    gmix2, gq2, gkv2 = g_mix.reshape(1, D), gq.reshape(1, QL), gkv.reshape(1, KVL)
    vec = pl.BlockSpec((1, D), lambda i: (0, 0))
    vq, vkv = pl.BlockSpec((1, QL), lambda i: (0, 0)), pl.BlockSpec((1, KVL), lambda i: (0, 0))
    return pl.pallas_call(
        body, name=name,
        out_shape=(jax.ShapeDtypeStruct((T, PA), BF16), jax.ShapeDtypeStruct((T, HQ), BF16),
                   jax.ShapeDtypeStruct((1, QL), F32), jax.ShapeDtypeStruct((1, KVL), F32),
                   jax.ShapeDtypeStruct((T, D), F32), jax.ShapeDtypeStruct((1, D), F32)),
        grid=(T // tm,),
        in_specs=[row(HQ), row(HQ), row(LANE), row(QL), row(KVL), row(POOL_DIM), row(2 * D), row(D), row(D),
                  resident(win_t), resident(wuq), resident(wukv),
                  vec, vq, vkv, row(LANE), row(LANE), row(LANE)],
        out_specs=(row(PA), row(HQ), vq, vkv, row(D), vec),
        compiler_params=_params("arbitrary"),
    )(dq, dkv, dkr, ql, kvl, dxp, dgl, x, dxo, win_t, wuq, wukv, gmix2, gq2, gkv2, *tabs)


def _pool_groups(x_of, S):
    row = lax.broadcasted_iota(jnp.int32, (S, POOL_G), 0)
    for g, w in enumerate(POOL_WINDOWS):
        x = x_of(g)
        s = x
        d = 1
        while d < w:
            s = s + jnp.where(row >= d, pltpu.roll(s, d, 0), 0.0)
            d *= 2
        cnt = jnp.minimum(row + 1, w).astype(F32)
        yield g, w, x, s / cnt - x, cnt, row


def _pool_fwd(xp, maps, scale, *, S, name):
    T = xp.shape[0]

    def body(xp_ref, maps_ref, scale_ref, ms_ref):
        for g, _, _, pooled, _, _ in _pool_groups(lambda g: xp_ref[:, g * POOL_G:(g + 1) * POOL_G], S):
            mixed = jnp.dot(pooled.astype(BF16), maps_ref[g].astype(BF16), preferred_element_type=F32)
            ms_ref[:, g * POOL_G:(g + 1) * POOL_G] = (mixed * scale_ref[:, g * POOL_G:(g + 1) * POOL_G]).astype(BF16)

    return pl.pallas_call(
        body, name=name,
        out_shape=jax.ShapeDtypeStruct((T, POOL_DIM), BF16),
        grid=(T // S,),
        in_specs=[pl.BlockSpec((S, POOL_DIM), lambda b: (b, 0)),
                  pl.BlockSpec(maps.shape, lambda b: (0, 0, 0)),
                  pl.BlockSpec((1, POOL_DIM), lambda b: (0, 0))],
        out_specs=pl.BlockSpec((S, POOL_DIM), lambda b: (b, 0)),
        compiler_params=_params("parallel"),
    )(xp, maps, scale.reshape(1, POOL_DIM))


def _pool_bwd(xp, dms, maps, scale, *, S, name):
    T = xp.shape[0]

    def body(xp_ref, dms_ref, maps_ref, scale_ref, dxp_ref, dmaps_ref, dscale_ref):
        first = pl.program_id(0) == 0
        for g, w, _, pooled, cnt, row in _pool_groups(lambda g: xp_ref[:, g * POOL_G:(g + 1) * POOL_G], S):
            cols = slice(g * POOL_G, (g + 1) * POOL_G)
            pooled_b = pooled.astype(BF16)
            maps_b = maps_ref[g].astype(BF16)
            mixed = jnp.dot(pooled_b, maps_b, preferred_element_type=F32)
            dms = dms_ref[:, cols]
            dscale = jnp.sum(dms * mixed, axis=0, keepdims=True)
            dmixed = (dms * scale_ref[:, cols]).astype(BF16)
            dmaps = lax.dot_general(pooled_b, dmixed, (((0,), (0,)), ((), ())), preferred_element_type=F32)
            dpooled = lax.dot_general(dmixed, maps_b, (((1,), (1,)), ((), ())), preferred_element_type=F32)
            z = dpooled / cnt
            d = 1
            while d < w:
                z = z + jnp.where(row < S - d, pltpu.roll(z, S - d, 0), 0.0)
                d *= 2
            dxp_ref[:, cols] = z - dpooled

            @pl.when(first)
            def _():
                dmaps_ref[g] = dmaps
                dscale_ref[:, cols] = dscale

            @pl.when(jnp.logical_not(first))
            def _():
                dmaps_ref[g] += dmaps
                dscale_ref[:, cols] += dscale

    seq = pl.BlockSpec((S, POOL_DIM), lambda b: (b, 0))
    maps_spec = pl.BlockSpec(maps.shape, lambda b: (0, 0, 0))
    vec = pl.BlockSpec((1, POOL_DIM), lambda b: (0, 0))
    return pl.pallas_call(
        body, name=name,
        out_shape=(jax.ShapeDtypeStruct((T, POOL_DIM), F32), jax.ShapeDtypeStruct(maps.shape, F32),
                   jax.ShapeDtypeStruct((1, POOL_DIM), F32)),
        grid=(T // S,),
        in_specs=[seq, seq, maps_spec, vec],
        out_specs=(seq, maps_spec, vec),
        compiler_params=_params("arbitrary"),
    )(xp, dms, maps, scale.reshape(1, POOL_DIM))


def _causal_mask(s, t):
    r = lax.broadcasted_iota(jnp.int32, (t, t), 0)
    c = lax.broadcasted_iota(jnp.int32, (t, t), 1)
    return jnp.where(r >= c, s, MASK_VALUE)


_NT = (((1,), (1,)), ((), ()))
_TN = (((0,), (0,)), ((), ()))


def _attn_fwd(q, kv, kr, *, S, name):
    T, HQ = q.shape
    H = HQ // HEAD_W
    B = T // S
    t = _tile(S, ATTN_TILE)
    n = S // t

    def body(q_ref, k_ref, v_ref, kr_ref, o_ref, lse_ref, kcat, vcat):
        kcat[:, :QK_NOPE] = k_ref[...]
        kcat[:, QK_NOPE:] = kr_ref[...]
        vcat[:, :V_DIM] = v_ref[...]
        vcat[:, V_DIM:] = jnp.ones((S, HEAD_W - V_DIM), BF16)
        for i in range(n):
            rows = slice(i * t, (i + 1) * t)
            qt = q_ref[rows, :]
            m = jnp.full((t, 1), MASK_VALUE, F32)
            acc = jnp.zeros((t, HEAD_W), F32)
            for j in range(i + 1):
                cols = slice(j * t, (j + 1) * t)
                s = lax.dot_general(qt, kcat[cols, :], _NT, preferred_element_type=F32) * ATTN_SCALE_LOG2
                if j == i:
                    s = _causal_mask(s, t)
                m_new = jnp.maximum(m, jnp.max(s, axis=1, keepdims=True))
                p = jnp.exp2(s - m_new)
                acc = jnp.exp2(m - m_new) * acc + jnp.dot(p.astype(BF16), vcat[cols, :], preferred_element_type=F32)
                m = m_new
            l = acc[:, V_DIM:V_DIM + 1]
            o_ref[rows, :] = (acc[:, :V_DIM] / l).astype(BF16)
            lse_ref[rows, :] = jnp.broadcast_to(m + jnp.log2(l), (t, LANE))

    seq_h = pl.BlockSpec((S, LANE), lambda b, h: (b, h))
    return pl.pallas_call(
        body, name=name,
        out_shape=(jax.ShapeDtypeStruct((T, H * V_DIM), BF16), jax.ShapeDtypeStruct((T, H * LANE), F32)),
        grid=(B, H),
        in_specs=[pl.BlockSpec((S, HEAD_W), lambda b, h: (b, h)),
                  pl.BlockSpec((S, QK_NOPE), lambda b, h: (b, 2 * h)),
                  pl.BlockSpec((S, V_DIM), lambda b, h: (b, 2 * h + 1)),
                  pl.BlockSpec((S, LANE), lambda b, h: (b, 0))],
        out_specs=(seq_h, seq_h),
        scratch_shapes=[pltpu.VMEM((S, HEAD_W), BF16), pltpu.VMEM((S, HEAD_W), BF16)],
        compiler_params=_params("parallel", "parallel"),
    )(q, kv, kv, kr)


def _attn_bwd(q, kv, kr, o, do, lse, *, S, name):
    T, HQ = q.shape
    H = HQ // HEAD_W
    B = T // S
    t = _tile(S, ATTN_TILE)
    n = S // t

    def body(q_ref, k_ref, v_ref, kr_ref, o_ref, do_ref, lse_ref, dq_ref, dkv_ref, dkr_ref, kcat, dq_acc):
        @pl.when(pl.program_id(1) == 0)
        def _():
            dkr_ref[...] = jnp.zeros_like(dkr_ref)

        kcat[:, :QK_NOPE] = k_ref[...]
        kcat[:, QK_NOPE:] = kr_ref[...]
        delta = [jnp.sum(do_ref[i * t:(i + 1) * t, :].astype(F32) * o_ref[i * t:(i + 1) * t, :].astype(F32),
                         axis=1, keepdims=True) for i in range(n)]
        for j in range(n):
            cols = slice(j * t, (j + 1) * t)
            kc = kcat[cols, :]
            vt = v_ref[cols, :]
            dk = jnp.zeros((t, HEAD_W), F32)
            dv = jnp.zeros((t, V_DIM), F32)
            for i in range(j, n):
                rows = slice(i * t, (i + 1) * t)
                qt = q_ref[rows, :]
                dot_ = do_ref[rows, :]
                s = lax.dot_general(qt, kc, _NT, preferred_element_type=F32) * ATTN_SCALE_LOG2
                if i == j:
                    s = _causal_mask(s, t)
                p = jnp.exp2(s - lse_ref[rows, :][:, :1])
                dv = dv + lax.dot_general(p.astype(BF16), dot_, _TN, preferred_element_type=F32)
                dp = lax.dot_general(dot_, vt, _NT, preferred_element_type=F32)
                ds = (p * (dp - delta[i]) * ATTN_SCALE).astype(BF16)
                dk = dk + lax.dot_general(ds, qt, _TN, preferred_element_type=F32)
                dq_part = jnp.dot(ds, kc, preferred_element_type=F32)
                if j == 0:
                    dq_acc[rows, :] = dq_part
                else:
                    dq_acc[rows, :] += dq_part
            dkv_ref[cols, :QK_NOPE] = dk[:, :QK_NOPE].astype(BF16)
            dkv_ref[cols, QK_NOPE:] = dv.astype(BF16)
            dkr_ref[cols, :] += dk[:, QK_NOPE:]
        dq_ref[...] = dq_acc[...].astype(BF16)

    seq_q = pl.BlockSpec((S, HEAD_W), lambda b, h: (b, h))
    seq_h = pl.BlockSpec((S, LANE), lambda b, h: (b, h))
    seq_shared = pl.BlockSpec((S, LANE), lambda b, h: (b, 0))
    return pl.pallas_call(
        body, name=name,
        out_shape=(jax.ShapeDtypeStruct((T, HQ), BF16), jax.ShapeDtypeStruct((T, HQ), BF16),
                   jax.ShapeDtypeStruct((T, LANE), F32)),
        grid=(B, H),
        in_specs=[seq_q,
                  pl.BlockSpec((S, QK_NOPE), lambda b, h: (b, 2 * h)),
                  pl.BlockSpec((S, V_DIM), lambda b, h: (b, 2 * h + 1)),
                  seq_shared, seq_h, seq_h, seq_h],
        out_specs=(seq_q, seq_q, seq_shared),
        scratch_shapes=[pltpu.VMEM((S, HEAD_W), BF16), pltpu.VMEM((S, HEAD_W), F32)],
        compiler_params=_params("parallel", "arbitrary"),
    )(q, kv, kv, kr, o, do, lse)


def _merge_out(h, ms, o, x, win_t, bgate, wpp, wap, wout, next_gain, *, name):
    T, D = x.shape
    tm = _tile(T, 256, 16)
    n_gate = win_t.shape[0] - 2 * D

    def body(h_ref, ms_ref, o_ref, x_ref, win_ref, bgate_ref, wpp_ref, wap_ref, wout_ref, ng_ref,
             gates_ref, ba_ref, bb_ref, merged_ref, xn_ref, hn_ref):
        logits = lax.dot_general(h_ref[...], win_ref[n_gate:, :], _NT, preferred_element_type=F32) + bgate_ref[...]
        gates = jax.nn.sigmoid(logits)
        ba = jnp.dot(ms_ref[...], wpp_ref[...], preferred_element_type=F32)
        bb = jnp.dot(o_ref[...], wap_ref[...], preferred_element_type=F32)
        merged = (gates[:, :D] * ba + gates[:, D:] * bb).astype(BF16)
        gates_ref[...] = gates.astype(BF16)
        ba_ref[...] = ba.astype(BF16)
        bb_ref[...] = bb.astype(BF16)
        merged_ref[...] = merged
        xn = x_ref[...] + jnp.dot(merged, wout_ref[...], preferred_element_type=F32)
        xn_ref[...] = xn
        hn_ref[...] = (xn * _rstd(xn) * ng_ref[...]).astype(BF16)

    def row(w):
        return pl.BlockSpec((tm, w), lambda i: (i, 0))

    def whole(arr):
        return pl.BlockSpec(arr.shape, lambda i: (0,) * arr.ndim, pipeline_mode=pl.Buffered(1))

    bg2, ng2 = bgate.reshape(1, 2 * D), next_gain.reshape(1, D)
    act = jax.ShapeDtypeStruct((T, D), BF16)
    return pl.pallas_call(
        body, name=name,
        out_shape=(jax.ShapeDtypeStruct((T, 2 * D), BF16), act, act, act, jax.ShapeDtypeStruct((T, D), F32), act),
        grid=(T // tm,),
        in_specs=[row(D), row(ms.shape[1]), row(o.shape[1]), row(D), whole(win_t), whole(bg2), whole(wpp),
                  whole(wap), whole(wout), whole(ng2)],
        out_specs=(row(2 * D), row(D), row(D), row(D), row(D), row(D)),
        compiler_params=_params("parallel"),
    )(h, ms, o, x, win_t, bg2, wpp, wap, wout, ng2)


def _merge_bwd(dxo, wout, wpp, wap, gates, ba, bb, *, name, dep=None):
    T, D = dxo.shape
    tm = _tile(T, 512, 16)

    def body(dxo_ref, wout_ref, wpp_ref, wap_ref, gates_ref, ba_ref, bb_ref, *rest):
        dba_ref, dbb_ref, dgl_ref, dbg_ref, dms_ref, do_ref = rest[-6:]
        dm = lax.dot_general(dxo_ref[...].astype(BF16), wout_ref[...], _NT, preferred_element_type=F32)
        ga = gates_ref[:, :D].astype(F32)
        gb = gates_ref[:, D:].astype(F32)
        dba = (dm * ga).astype(BF16)
        dbb = (dm * gb).astype(BF16)
        dba_ref[...] = dba
        dbb_ref[...] = dbb
        dms_ref[...] = lax.dot_general(dba, wpp_ref[...], _NT, preferred_element_type=F32)
        do_ref[...] = lax.dot_general(dbb, wap_ref[...], _NT, preferred_element_type=F32).astype(BF16)
        dgl_a = dm * ba_ref[...].astype(F32) * (ga * (1.0 - ga))
        dgl_b = dm * bb_ref[...].astype(F32) * (gb * (1.0 - gb))
        dgl_ref[:, :D] = dgl_a.astype(BF16)
        dgl_ref[:, D:] = dgl_b.astype(BF16)
        sa = jnp.sum(dgl_a, axis=0, keepdims=True)
        sb = jnp.sum(dgl_b, axis=0, keepdims=True)

        @pl.when(pl.program_id(0) == 0)
        def _():
            dbg_ref[:, :D] = sa
            dbg_ref[:, D:] = sb

        @pl.when(pl.program_id(0) > 0)
        def _():
            dbg_ref[:, :D] += sa
            dbg_ref[:, D:] += sb

    def row(w):
        return pl.BlockSpec((tm, w), lambda i: (i, 0))

    def whole(arr):
        return pl.BlockSpec(arr.shape, lambda i: (0, 0))

    P, HV = wpp.shape[0], wap.shape[0]
    act = jax.ShapeDtypeStruct((T, D), BF16)
    return pl.pallas_call(
        body, name=name,
        out_shape=(act, act, jax.ShapeDtypeStruct((T, 2 * D), BF16), jax.ShapeDtypeStruct((1, 2 * D), F32),
                   jax.ShapeDtypeStruct((T, P), F32), jax.ShapeDtypeStruct((T, HV), BF16)),
        grid=(T // tm,),
        in_specs=[row(D), whole(wout), whole(wpp), whole(wap), row(2 * D), row(D), row(D)] + _dep_spec(dep),
        out_specs=(row(D), row(D), row(2 * D), pl.BlockSpec((1, 2 * D), lambda i: (0, 0)), row(P), row(HV)),
        compiler_params=_params("arbitrary"),
    )(dxo, wout, wpp, wap, gates, ba, bb, *([] if dep is None else [dep]))


def _ffn_fwd(x, h, w, tag, next_gain, dep=None):
    gate, up, a, xn, hn = _ffn_fwd_core(x, h, w["up_t"], w["wd"], next_gain, alpha=0.5,
                                        name=f"{tag}_fwd" if next_gain is not None else f"{tag}_fwd_last", dep=dep)
    return xn, hn, (x, h, gate, up, a)


def _ffn_bwd(dxo, gain, w, saved, tag, dep=None, early=None, mid=None):
    x, h, gate, up, a = saved
    dwd = _mm(a, dxo, ta=True, alpha=0.5, out_dtype=BF16, name=f"{tag}_dwd", tm=1408, tn=1024, tk=1024, dep=dep)
    if early is not None:
        dep = early(dwd)
    dgate, dup, dx, dgain = _ffn_bwd_core(dxo, w["wd"], w["up_t"], gate, up, x, gain, alpha=0.5,
                                          name=f"{tag}_bwd_core", dep=dep)
    dup_t = _ffn_dw_up(dgate, dup, h, name=f"{tag}_dw_up", dep=None if mid is None else mid(dgain))
    return dx, dgain, dup_t, dwd


def _mixer_fwd(x, h, p, w, tabs, S, next_gain, dep=None):
    xp, ql, kvl, qn, kvn, q, kv, kr = _mixer_in(h, w["win_t"], w["wuq_t"], w["wukv"], p["q_latent_norm"],
                                                 p["kv_latent_norm"], tabs, name="mix_in", dep=dep)
    ms = _pool_fwd(xp, p["pool_maps"], p["pool_scale"], S=S, name="pool_fwd")
    o, lse = _attn_fwd(q, kv, kr, S=S, name="attn_fwd")
    gates, ba, bb, merged, xn, hn = _merge_out(h, ms, o, x, w["win_t"], p["b_gate"], w["wpp"], w["wap"], w["wout"],
                                               next_gain, name="merge_out")
    return xn, hn, (x, h, xp, ql, kvl, qn, kvn, q, kv, kr, ms, o, lse, gates, ba, bb, merged)


def _mixer_bwd(dxo, p, w, tabs, saved, S, dep=None):
    x, h, xp, ql, kvl, qn, kvn, q, kv, kr, ms, o, lse, gates, ba, bb, merged = saved
    dba, dbb, dgl, dbg, dms, do = _merge_bwd(dxo, w["wout"], w["wpp"], w["wap"], gates, ba, bb, name="merge_bwd",
                                             dep=dep)
    g = {}
    g["wout"], g["wpp"], g["wap"] = _dw_multi([(merged, dxo), (ms, dba), (o, dbb)], name="d_w_merge")
    dxp, g["pool_maps"], g["pool_scale"] = _pool_bwd(xp, dms, p["pool_maps"], p["pool_scale"], S=S, name="pool_bwd")
    dq, dkv, dkr = _attn_bwd(q, kv, kr, o, do, lse, S=S, name="attn_bwd")
    dproj, dqp, g["q_latent_norm"], g["kv_latent_norm"], dx, g["norm_mix"] = _mixer_in_bwd(
        dq, dkv, dkr, ql, kvl, dxp, dgl, x, dxo, w["win_t"], w["wuq_t"], w["wukv"],
        p["norm_mix"], p["q_latent_norm"], p["kv_latent_norm"], tabs, name="mix_in_bwd")
    g["wuq_t"], g["wukv"] = _dw_multi([(dqp, qn), (kvn, dkv)], name="d_w_qkv", tk=1024)
    g["wa_t"], g["wgate_t"] = _dw_multi([(dproj, h), (dgl, h)], name="d_w_in")
    g["b_gate"] = dbg
    return dx, g


BIG = ("ffn1_up", "ffn1_down", "w_in", "w_pool_proj", "w_uq", "w_ukv", "w_attn_proj", "w_out", "ffn2_up", "ffn2_down")
SMALL = ("norm_ffn1", "norm_mix", "b_gate", "pool_maps", "pool_scale", "q_latent_norm", "kv_latent_norm", "norm_ffn2")
PACKED = ("w_pool_proj", "w_uq", "w_ukv")
TRANSPOSED = ("ffn1_up", "ffn2_up", "w_in", "w_uq")
COL_SHARDED = ("w_pool_proj", "w_ukv")
QK_HEAD = QK_NOPE + QK_ROPE


def _rows(stacked):
    n, r, c = stacked.shape
    return stacked.reshape(n * r, c)


def _cols(stacked):
    n, k, c = stacked.shape
    return stacked.transpose(1, 0, 2).reshape(k, n * c)


FFN1_PART = ("ffn1_up", "ffn1_down")
MIXER_PART = ("w_in", "w_attn_proj", "w_out") + PACKED
FFN2_PART = ("ffn2_up", "ffn2_down")


def _kernel_weights(stacked):
    full = {}
    for tag in ("ffn1", "ffn2"):
        if tag + "_up" in stacked:
            full[tag] = {"up_t": _rows(stacked[tag + "_up"]), "wd": _rows(stacked[tag + "_down"])}
    if "w_in" in stacked:
        win_t = _rows(stacked["w_in"])
        wuq_t = _rows(stacked["w_uq"])
        QL = wuq_t.shape[1]
        H = wuq_t.shape[0] // QK_HEAD
        wuq_t = jnp.pad(wuq_t.reshape(H, QK_HEAD, QL), ((0, 0), (0, HEAD_W - QK_HEAD), (0, 0)))
        full.update({"win_t": win_t, "wuq_t": wuq_t.reshape(H * HEAD_W, QL),
                     "wukv": _cols(stacked["w_ukv"]), "wpp": _cols(stacked["w_pool_proj"]),
                     "wap": _rows(stacked["w_attn_proj"]), "wout": _rows(stacked["w_out"])})
    return full


def _split_rows(full):
    return full.reshape(N_DEV, full.shape[0] // N_DEV, full.shape[1])


def _split_cols(full):
    k, cols = full.shape
    return full.reshape(k, N_DEV, cols // N_DEV).transpose(1, 0, 2)


def _mixer_grads_stacked(g):
    n_a = g["wa_t"].shape[0] - (LANE - QK_ROPE)
    HQ, QL = g["wuq_t"].shape
    H = HQ // HEAD_W
    wuq_t = g["wuq_t"].reshape(H, HEAD_W, QL)[:, :QK_HEAD].reshape(H * QK_HEAD, QL)
    return {"w_in": _split_rows(jnp.concatenate([g["wa_t"][:n_a], g["wgate_t"]], axis=0)),
            "w_uq": _split_rows(wuq_t),
            "w_pool_proj": _split_cols(g["wpp"]), "w_ukv": _split_cols(g["wukv"]),
            "w_attn_proj": _split_rows(g["wap"]), "w_out": _split_rows(g["wout"])}


def _mesh_place():
    x, y, c = lax.axis_index("x"), lax.axis_index("y"), lax.axis_index("c")
    chips = [(1 - x, y), (x, 1 - y), (1 - x, 1 - y)]
    return x, y, c, chips


HBM = pl.BlockSpec(memory_space=pltpu.HBM)
SEMAPHORES = pl.BlockSpec(memory_space=pltpu.SEMAPHORE)
DATAFLOW = pltpu.SideEffectType.DATAFLOW_SIDE_EFFECTING
GATHER_PEERS = 4
SCATTER_PEERS = 7


def _in_hbm(a):
    return pltpu.with_memory_space_constraint(a, pltpu.HBM)


def _gather_plan(src_refs, land_refs):
    x, y, c, chips = _mesh_place()
    me = 4 * x + 2 * y + c
    targets = [(x, y, 1 - c)] + [(cx, cy, c) for cx, cy in chips]
    return [(s, land.at[me], to) for s, land in zip(src_refs, land_refs) for to in targets]


def _scatter_plan(src_refs, land_refs):
    x, y, c, _ = _mesh_place()
    peers = [(x, y, 1 - c), (1 - x, y, c), (x, 1 - y, c), (1 - x, 1 - y, c),
             (1 - x, y, 1 - c), (x, 1 - y, 1 - c), (1 - x, 1 - y, 1 - c)]
    return [(s.at[4 * px + 2 * py + pc], land.at[k], (px, py, pc))
            for s, land in zip(src_refs, land_refs) for k, (px, py, pc) in enumerate(peers)]


def _descriptors(plan, src_refs, land_refs, send_sems, recv_sems):
    return [pltpu.make_async_remote_copy(src_ref=s, dst_ref=d, send_sem=send_sems.at[k], recv_sem=recv_sems.at[k],
                                         device_id=to, device_id_type=MESH)
            for k, (s, d, to) in enumerate(plan(src_refs, land_refs))]


FORWARD_COPIES = 4


def _forward_slots():
    x, y, c, chips = _mesh_place()
    return [4 * cx + 2 * cy + c for cx, cy in chips] + [4 * x + 2 * y + (1 - c)], (x, y, 1 - c)


def _forward_plan(src_refs, land_refs):
    slots, sibling = _forward_slots()
    return [(land.at[s], land.at[s], sibling) for land in land_refs for s in slots]


def _gather_all_plan(src_refs, land_refs):
    x, y, c, _ = _mesh_place()
    me = 4 * x + 2 * y + c
    peers = [(x, y, 1 - c), (1 - x, y, c), (x, 1 - y, c), (1 - x, 1 - y, c),
             (1 - x, y, 1 - c), (x, 1 - y, 1 - c), (1 - x, 1 - y, 1 - c)]
    return [(s, land.at[me], to) for s, land in zip(src_refs, land_refs) for to in peers]


def _exchange_start(srcs, lands, plan, n_copies, *, name):
    ns, n = len(srcs), len(srcs) + len(lands)

    def body(*refs):
        for cp in _descriptors(plan, refs[:ns], refs[ns:n], refs[n], refs[n + 1]):
            cp.start()
        refs[-1][...] = jnp.zeros_like(refs[-1])

    sems = pltpu.SemaphoreType.DMA((n_copies,))
    out = pl.pallas_call(
        body, name=name,
        out_shape=(sems, sems, *[pltpu.HBM(a.shape, a.dtype) for a in srcs + lands],
                   jax.ShapeDtypeStruct((8, LANE), F32)),
        in_specs=(HBM,) * n,
        out_specs=(SEMAPHORES, SEMAPHORES, *[HBM] * n, pl.BlockSpec(memory_space=pltpu.VMEM)),
        input_output_aliases={i: 2 + i for i in range(n)},
        compiler_params=pltpu.CompilerParams(has_side_effects=DATAFLOW),
    )(*[_in_hbm(a) for a in srcs + lands])
    return out[0], out[1], list(out[2:2 + ns]), list(out[2 + ns:2 + n]), out[-1]


def _exchange_wait(send_sems, recv_sems, srcs, lands, plan, after, *, name):
    ns, n = len(srcs), len(srcs) + len(lands)

    def body(*refs):
        for cp in _descriptors(plan, refs[:ns], refs[ns:n], refs[n], refs[n + 1]):
            cp.wait_send()
            cp.wait_recv()

    out = pl.pallas_call(
        body, name=name,
        out_shape=tuple(pltpu.HBM(a.shape, a.dtype) for a in srcs + lands),
        in_specs=(*[HBM] * n, SEMAPHORES, SEMAPHORES, ANY),
        out_specs=(HBM,) * n,
        input_output_aliases={i: i for i in range(n)},
        compiler_params=pltpu.CompilerParams(has_side_effects=DATAFLOW),
    )(*srcs, *lands, send_sems, recv_sems, after)
    return list(out[:ns]), list(out[ns:])


def _gather_forward(lands, *, name):
    n = len(lands)

    def body(*refs):
        in_refs, out_refs = refs[:n], refs[n:2 * n]
        token, send_sems, recv_sems = refs[2 * n:2 * n + 3]
        slots, sibling = _forward_slots()
        passed = [pltpu.make_async_remote_copy(
            src_ref=i.at[s], dst_ref=o.at[s],
            send_sem=send_sems.at[FORWARD_COPIES * b + j], recv_sem=recv_sems.at[FORWARD_COPIES * b + j],
            device_id=sibling, device_id_type=MESH)
            for b, (i, o) in enumerate(zip(in_refs, out_refs)) for j, s in enumerate(slots)]
        for cp in passed:
            cp.start()
        for cp in passed:
            cp.wait()
        token[...] = jnp.zeros_like(token)

    out = pl.pallas_call(
        body, name=name,
        out_shape=(*[jax.ShapeDtypeStruct(a.shape, a.dtype) for a in lands], jax.ShapeDtypeStruct((8, LANE), F32)),
        in_specs=[ANY] * n,
        out_specs=(*[ANY] * n, pl.BlockSpec(memory_space=pltpu.VMEM)),
        input_output_aliases={i: i for i in range(n)},
        scratch_shapes=[pltpu.SemaphoreType.DMA((FORWARD_COPIES * n,)), pltpu.SemaphoreType.DMA((FORWARD_COPIES * n,))],
    )(*lands)
    return list(out[:n]), out[n]


def _scatter_sum(parts, got, me, into, layer, *, name):
    shard = parts.shape[1:]
    cols = shard[-1]
    rows = int(np.prod(shard[:-1]))
    tr = _tile(rows, 1024, 16)
    layers = into.shape[0]

    def body(me_ref, p_ref, g_ref, into_ref, o_ref):
        acc = p_ref[...].astype(F32)
        for k in range(SCATTER_PEERS):
            acc = acc + g_ref[k].astype(F32)
        o_ref[...] = acc

    slab = pl.BlockSpec((None, tr, cols), lambda r, me_ref: (layer, r, 0))
    out = pl.pallas_call(
        body, name=name,
        out_shape=jax.ShapeDtypeStruct((layers, rows, cols), F32),
        grid_spec=pltpu.PrefetchScalarGridSpec(
            num_scalar_prefetch=1, grid=(rows // tr,),
            in_specs=[pl.BlockSpec((None, tr, cols), lambda r, me_ref: (me_ref[0], r, 0)),
                      pl.BlockSpec((SCATTER_PEERS, tr, cols), lambda r, me_ref: (0, r, 0)),
                      ANY],
            out_specs=slab),
        input_output_aliases={3: 0},
        compiler_params=_params("parallel"),
    )(me, parts.reshape(N_DEV, rows, cols), got.reshape(SCATTER_PEERS, rows, cols), into.reshape(layers, rows, cols))
    return out.reshape(layers, *shard)


def _sum_devices(parts, *, name):
    _, R, C = parts.shape
    tr = _tile(R, 512, 8)

    def body(p_ref, o_ref):
        acc = p_ref[0]
        for d in range(1, N_DEV):
            acc = acc + p_ref[d]
        o_ref[...] = acc

    return pl.pallas_call(
        body, name=name,
        out_shape=jax.ShapeDtypeStruct((R, C), F32),
        grid=(R // tr,),
        in_specs=[pl.BlockSpec((N_DEV, tr, C), lambda r: (0, r, 0))],
        out_specs=pl.BlockSpec((tr, C), lambda r: (r, 0)),
        compiler_params=_params("parallel"),
    )(parts)


def _adamw(w, g, m, v, *, name, dep=None):
    shape = w.shape
    cols = shape[-1]
    rows = w.size // cols
    tr = _tile(rows, 512, 8)

    def body(w_ref, g_ref, m_ref, v_ref, *rest):
        d_ref, nm_ref, nv_ref = rest[-3:]
        g = g_ref[...]
        m = ADAM_B1 * m_ref[...] + (1.0 - ADAM_B1) * g
        v = ADAM_B2 * v_ref[...] + (1.0 - ADAM_B2) * jnp.square(g)
        m_hat = m / (1.0 - ADAM_B1 ** ADAM_STEP)
        v_hat = v / (1.0 - ADAM_B2 ** ADAM_STEP)
        d_ref[...] = -ADAM_LR * (m_hat / (jnp.sqrt(v_hat) + ADAM_EPS) + ADAM_WD * w_ref[...])
        nm_ref[...] = m
        nv_ref[...] = v

    spec = pl.BlockSpec((tr, cols), lambda i: (i, 0))
    out = jax.ShapeDtypeStruct((rows, cols), F32)
    d, nm, nv = pl.pallas_call(
        body, name=name,
        out_shape=(out, out, out),
        grid=(rows // tr,),
        in_specs=[spec] * 4 + _dep_spec(dep), out_specs=(spec,) * 3,
        compiler_params=_params("parallel"),
    )(*(a.reshape(rows, cols) for a in (w, g, m, v)), *([] if dep is None else [dep]))
    return d.reshape(shape), nm.reshape(shape), nv.reshape(shape)


PACK_ALIGN = 16 * LANE


def _pack(pieces, lead):
    out = []
    for p in pieces:
        keep = p.shape[:lead]
        flat = p.reshape(*keep, -1)
        pad = (-flat.shape[-1]) % PACK_ALIGN
        if pad:
            flat = jnp.pad(flat, [(0, 0)] * lead + [(0, pad)])
        out.append(flat.reshape(*keep, -1, LANE))
    return jnp.concatenate(out, axis=lead)


def _unpack(buf, shapes, lead):
    keep = buf.shape[:lead]
    out, row = [], 0
    for shape in shapes:
        size = int(np.prod(shape))
        rows = -(-size // PACK_ALIGN) * (PACK_ALIGN // LANE)
        piece = lax.slice_in_dim(buf, row, row + rows, axis=lead).reshape(*keep, rows * LANE)
        out.append(lax.slice_in_dim(piece, 0, size, axis=lead).reshape(*keep, *shape))
        row += rows
    return out


def kernel(x, positions, norm_ffn1, ffn1_up, ffn1_down, norm_mix, w_in, b_gate, pool_maps, pool_scale, w_pool_proj, q_latent_norm, w_uq, kv_latent_norm, w_ukv, w_attn_proj, w_out, norm_ffn2, ffn2_up, ffn2_down, final_norm, loss_target, m_norm_ffn1, m_ffn1_up, m_ffn1_down, m_norm_mix, m_w_in, m_b_gate, m_pool_maps, m_pool_scale, m_w_pool_proj, m_q_latent_norm, m_w_uq, m_kv_latent_norm, m_w_ukv, m_w_attn_proj, m_w_out, m_norm_ffn2, m_ffn2_up, m_ffn2_down, m_final_norm, v_norm_ffn1, v_ffn1_up, v_ffn1_down, v_norm_mix, v_w_in, v_b_gate, v_pool_maps, v_pool_scale, v_w_pool_proj, v_q_latent_norm, v_w_uq, v_kv_latent_norm, v_w_ukv, v_w_attn_proj, v_w_out, v_norm_ffn2, v_ffn2_up, v_ffn2_down, v_final_norm):
    order = ("norm_ffn1", "ffn1_up", "ffn1_down", "norm_mix", "w_in", "b_gate", "pool_maps", "pool_scale",
             "w_pool_proj", "q_latent_norm", "w_uq", "kv_latent_norm", "w_ukv", "w_attn_proj", "w_out",
             "norm_ffn2", "ffn2_up", "ffn2_down", "final_norm")
    w = dict(zip(order, (norm_ffn1, ffn1_up, ffn1_down, norm_mix, w_in, b_gate, pool_maps, pool_scale, w_pool_proj,
                         q_latent_norm, w_uq, kv_latent_norm, w_ukv, w_attn_proj, w_out, norm_ffn2, ffn2_up,
                         ffn2_down, final_norm)))
    m = dict(zip(order, (m_norm_ffn1, m_ffn1_up, m_ffn1_down, m_norm_mix, m_w_in, m_b_gate, m_pool_maps, m_pool_scale,
                         m_w_pool_proj, m_q_latent_norm, m_w_uq, m_kv_latent_norm, m_w_ukv, m_w_attn_proj, m_w_out,
                         m_norm_ffn2, m_ffn2_up, m_ffn2_down, m_final_norm)))
    v = dict(zip(order, (v_norm_ffn1, v_ffn1_up, v_ffn1_down, v_norm_mix, v_w_in, v_b_gate, v_pool_maps, v_pool_scale,
                         v_w_pool_proj, v_q_latent_norm, v_w_uq, v_kv_latent_norm, v_w_ukv, v_w_attn_proj, v_w_out,
                         v_norm_ffn2, v_ffn2_up, v_ffn2_down, v_final_norm)))
    L = norm_ffn1.shape[0]
    B, S, D = x.shape
    T = B * S

    def turned(a, n):
        return a.transpose(0, 2, 1) if n in TRANSPOSED else a

    wk, mk, vk = ({n: turned(d[n], n) for n in order} for d in (w, m, v))
    packed_shapes = [wk[n].shape[1:] for n in PACKED]
    my_slot = 4 * lax.axis_index("x") + 2 * lax.axis_index("y") + lax.axis_index("c")
    me = jnp.stack([my_slot]).astype(jnp.int32)

    def weight_blocks(l, names, token):
        zero = token[0, 0].astype(BF16)
        blocks = [wk[n][l].astype(BF16) + zero for n in names if n not in PACKED]
        if any(n in PACKED for n in names):
            blocks.append(_pack([wk[n][l].astype(BF16) + zero for n in PACKED], 0))
        return blocks

    def kernel_weights(names, lands):
        direct = [n for n in names if n not in PACKED]
        stacked = dict(zip(direct, lands))
        if len(lands) > len(direct):
            stacked.update(zip(PACKED, _unpack(lands[-1], packed_shapes, 1)))
        return _kernel_weights(stacked)

    def gather_start(l, names, token, tag):
        blocks = weight_blocks(l, names, token)
        lands = [lax.empty((N_DEV, *b.shape), b.dtype) for b in blocks]
        send_sems, recv_sems, blocks, lands, token = _exchange_start(
            blocks, lands, _gather_plan, GATHER_PEERS * len(blocks), name=f"gather_start_{tag}")
        return (send_sems, recv_sems, blocks, lands, tag), token

    def gather_wait(state, after):
        send_sems, recv_sems, blocks, lands, tag = state
        return _exchange_wait(send_sems, recv_sems, blocks, lands, _gather_plan, after, name=f"gather_wait_{tag}")[1]

    layer_part = FFN1_PART + MIXER_PART + FFN2_PART
    tabs = _rope_tables(positions.reshape(T))
    xs = x.reshape(T, D)
    full, saved = [], []

    p = {n: w[n][0] for n in SMALL}
    state, token = gather_start(0, FFN1_PART, jnp.zeros((8, LANE), F32), "0_ffn1")
    h = _rms_fwd(xs, w["norm_ffn1"][0] + token[0, 0], name="first_norm")
    lands, token = _gather_forward(gather_wait(state, h), name="gather_forward")
    w0 = kernel_weights(FFN1_PART, lands)
    state, token = gather_start(0, MIXER_PART, token, "0_mix")
    xs, h, s1 = _ffn_fwd(xs, h, w0["ffn1"], "ffn1", p["norm_mix"], dep=token)
    lands, token = _gather_forward(gather_wait(state, xs), name="gather_forward")
    w0.update(kernel_weights(MIXER_PART, lands))
    state, token = gather_start(0, FFN2_PART, token, "0_ffn2")
    if L > 1:
        next_state, token = gather_start(1, layer_part, token, "1")
    xs, h, s2 = _mixer_fwd(xs, h, p, w0, tabs, S, p["norm_ffn2"], dep=token)
    lands, token = _gather_forward(gather_wait(state, xs), name="gather_forward")
    w0.update(kernel_weights(FFN2_PART, lands))
    xs, h, s3 = _ffn_fwd(xs, h, w0["ffn2"], "ffn2", w["norm_ffn1"][1] if L > 1 else None, dep=token)
    if L > 1:
        lands = gather_wait(next_state, xs)
        first, token = _gather_forward(lands[:len(FFN1_PART)], name="gather_forward")
        rest_send, rest_recv, _, rest, token = _exchange_start(
            [], lands[len(FFN1_PART):], _forward_plan, FORWARD_COPIES * (len(lands) - len(FFN1_PART)),
            name="forward_start_1")
    full.append(w0)
    saved.append((s1, s2, s3))

    for l in range(1, L):
        full.append(kernel_weights(FFN1_PART, first) if l == 1 else kernel_weights(layer_part, lands))
        more = l + 1 < L
        p = {n: w[n][l] for n in SMALL}
        if more:
            state, token = gather_start(l + 1, layer_part, token, f"{l + 1}")
        xs, h, s1 = _ffn_fwd(xs, h, full[l]["ffn1"], "ffn1", p["norm_mix"], dep=token if more or l == 1 else None)
        if l == 1:
            _, rest = _exchange_wait(rest_send, rest_recv, [], rest, _forward_plan, xs, name="forward_wait_1")
            full[l].update(kernel_weights(MIXER_PART + FFN2_PART, rest))
        xs, h, s2 = _mixer_fwd(xs, h, p, full[l], tabs, S, p["norm_ffn2"])
        if more:
            lands = gather_wait(state, xs)
            send_sems, recv_sems, _, lands, token = _exchange_start(
                [], lands, _forward_plan, FORWARD_COPIES * len(lands), name=f"forward_start_{l + 1}")
        xs, h, s3 = _ffn_fwd(xs, h, full[l]["ffn2"], "ffn2", w["norm_ffn1"][l + 1] if more else None,
                             dep=token if more else None)
        if more:
            _, lands = _exchange_wait(send_sems, recv_sems, [], lands, _forward_plan, xs, name=f"forward_wait_{l + 1}")
        saved.append((s1, s2, s3))
    dx, dfinal, loss = _loss_head(xs, final_norm, loss_target.reshape(T, D), name="loss_head")

    big_grads = {n: [None] * L if n in PACKED else lax.empty((L, *wk[n].shape[1:]), F32) for n in BIG}
    small_grads_of = [None] * L
    pending = None

    def scatter_start(names, stacked, tag):
        srcs = [stacked[n] for n in names if n not in PACKED]
        if any(n in PACKED for n in names):
            srcs.append(_pack([stacked[n] for n in PACKED], 1))
        lands = [lax.empty((SCATTER_PEERS, *s.shape[1:]), s.dtype) for s in srcs]
        send_sems, recv_sems, srcs, lands, token = _exchange_start(
            srcs, lands, _scatter_plan, SCATTER_PEERS * len(srcs), name=f"scatter_start_{tag}")
        return (names, send_sems, recv_sems, srcs, lands, tag), token

    def scatter_finish(state, after, l):
        names, send_sems, recv_sems, srcs, lands, tag = state
        srcs, got = _exchange_wait(send_sems, recv_sems, srcs, lands, _scatter_plan, after, name=f"scatter_wait_{tag}")
        direct = [n for n in names if n not in PACKED]
        for n, s, g in zip(direct, srcs, got):
            big_grads[n] = _scatter_sum(s, g, me, big_grads[n], l, name="scatter_sum")
        if len(srcs) > len(direct):
            packed = _scatter_sum(srcs[-1], got[-1], me, lax.empty((1, *srcs[-1].shape[1:]), F32), 0,
                                  name="scatter_sum")[0]
            for n, g in zip(PACKED, _unpack(packed, packed_shapes, 0)):
                big_grads[n][l] = g

    dep = None
    for l in reversed(range(L)):
        p = {n: w[n][l] for n in SMALL}
        s1, s2, s3 = saved[l]
        small_g = {}
        dx, small_g["norm_ffn2"], dup_t, dwd = _ffn_bwd(dx, p["norm_ffn2"], full[l]["ffn2"], s3, "ffn2", dep=dep)
        if pending is not None:
            scatter_finish(pending[0], dx, pending[1])
        stacked = {"ffn2_up": _split_rows(dup_t), "ffn2_down": _split_rows(dwd)}
        state, dep = scatter_start(("ffn2_up", "ffn2_down"), stacked, f"ffn2_{l}")
        pending = (state, l)

        dx, gm = _mixer_bwd(dx, p, full[l], tabs, s2, S, dep=dep)
        scatter_finish(pending[0], dx, pending[1])
        names = ("w_in", "w_attn_proj", "w_out") + PACKED
        state, dep = scatter_start(names, _mixer_grads_stacked(gm), f"mix_{l}")
        pending = (state, l)
        small_g.update({n: gm[n] for n in SMALL if n in gm})

        if l > 0:
            dx, small_g["norm_ffn1"], dup_t, dwd = _ffn_bwd(dx, p["norm_ffn1"], full[l]["ffn1"], s1, "ffn1", dep=dep)
            scatter_finish(pending[0], dx, pending[1])
            stacked = {"ffn1_up": _split_rows(dup_t), "ffn1_down": _split_rows(dwd)}
            state, dep = scatter_start(("ffn1_up", "ffn1_down"), stacked, f"ffn1_{l}")
            pending = (state, l)
        else:
            early_states = []

            def send_down(dwd):
                state, token = scatter_start(("ffn1_down",), {"ffn1_down": _split_rows(dwd)}, "ffn1_down_0")
                early_states.append(state)
                return token

            def send_small(dgain):
                small_g["norm_ffn1"] = dgain
                small_grads_of[0] = small_g
                parts = [small_grads_of[k][n] for k in range(L) for n in SMALL] + [dfinal, loss[0, :1]]
                vec = _pack([jnp.concatenate([a.reshape(-1) for a in parts])], 0)
                out = _exchange_start([vec], [lax.empty((N_DEV, *vec.shape), F32)], _gather_all_plan, SCATTER_PEERS,
                                      name="small_start")
                early_states.append((out, [a.shape for a in parts]))
                return out[4]

            dx, _, dup_t, _ = _ffn_bwd(dx, p["norm_ffn1"], full[l]["ffn1"], s1, "ffn1", dep=dep, early=send_down,
                                       mid=send_small)
            last_mixer, last_down = pending, (early_states[0], 0)
            (small_send, small_recv, vec_thru, small_land, _), small_shapes = early_states[1]
            state, dep = scatter_start(("ffn1_up",), {"ffn1_up": _split_rows(dup_t)}, "ffn1_up_0")
            pending = (state, l)
        small_grads_of[l] = small_g
    grad_x = dx.reshape(B, S, D)

    gk, grad, delta, new_m, new_v = {}, {}, {}, {}, {}

    def update(n, dep=None):
        wn, gn, mn, vn = (a.reshape(1, -1) if a.ndim == 1 else a for a in (wk[n], gk[n], mk[n], vk[n]))
        d, nm, nv = _adamw(wn, gn, mn, vn, name="adamw_" + n, dep=dep)
        grad[n] = turned(gk[n], n)
        delta[n], new_m[n], new_v[n] = (turned(a.reshape(wk[n].shape), n) for a in (d, nm, nv))

    for i, n in enumerate(FFN2_PART):
        gk[n] = jnp.stack(big_grads[n]) if n in PACKED else big_grads[n]
        update(n, dep if i == 0 else None)
    scatter_finish(last_mixer[0], new_v["ffn2_down"], last_mixer[1])
    for n in MIXER_PART:
        gk[n] = jnp.stack(big_grads[n]) if n in PACKED else big_grads[n]
        update(n)
    scatter_finish(last_down[0], new_v[MIXER_PART[-1]], last_down[1])
    scatter_finish(pending[0], new_v[MIXER_PART[-1]], pending[1])
    for n in FFN1_PART:
        gk[n] = jnp.stack(big_grads[n]) if n in PACKED else big_grads[n]
        update(n)

    vec_thru, small_land = _exchange_wait(small_send, small_recv, vec_thru, small_land, _gather_all_plan,
                                          new_v["ffn1_down"], name="small_wait")
    parts = lax.dynamic_update_index_in_dim(small_land[0], vec_thru[0], my_slot, 0)
    flat = _sum_devices(parts, name="sum_small").reshape(-1)
    small_grads, at = [], 0
    for shape in small_shapes:
        size = int(np.prod(shape))
        small_grads.append(lax.slice_in_dim(flat, at, at + size).reshape(shape))
        at += size
    loss_total = small_grads[-1].reshape(())
    for i, n in enumerate(SMALL):
        gk[n] = jnp.stack([small_grads[l * len(SMALL) + i] for l in range(L)]).reshape(w[n].shape)
        update(n)
    gk["final_norm"] = small_grads[-2].reshape(final_norm.shape)
    update("final_norm")
    return (loss_total, grad_x, *[grad[n] for n in order], *[delta[n] for n in order],
            *[new_m[n] for n in order], *[new_v[n] for n in order])
```
